```python
import jax
import jax.numpy as jnp
from jax import lax
import numpy as np

D_MODEL = 1024
BATCH = 4
SEQ = 4096
DEPTH = 2

GRID_W = 64
CTX_LEN = 256
N_EVEN = (DEPTH + 1) // 2
N_ODD = DEPTH // 2
EPS = 1e-6
ROPE_BASE = 10000.0
Q_BLOCK = 128
F32 = jnp.float32

MLA_HEADS = 8
MLA_Q_RANK = 256
MLA_KV_RANK = 128
MLA_NOPE = 64
MLA_ROPE = 32
MLA_V = 64
WIN_HEADS = 8
WIN_KV_HEADS = 2
WIN_HEAD_DIM = 64
WINDOW = 128
WIN_BLOCK = 128
GLA_HEADS = 4
GLA_DK = 64
GLA_DV = 128
GLA_GATE_RANK = 16
GLA_TAU = 16.0
GLA_CHUNK = 64
SG_GROUPS = 4
SG_CHUNK = 128
SG_WIDTH = 512
MOE_GROUPS = 4
MOE_PER_GROUP = 8
MOE_EXPERTS = MOE_GROUPS * MOE_PER_GROUP
MOE_TOPK = 2
MOE_HIDDEN = 512

WIN_Q = WIN_HEADS * WIN_HEAD_DIM
WIN_KV = WIN_KV_HEADS * WIN_HEAD_DIM
EVEN_IN = MLA_Q_RANK + MLA_KV_RANK + MLA_ROPE + WIN_Q + 2 * WIN_KV
EVEN_MIX = MLA_HEADS * MLA_V + WIN_Q
GLA_K = GLA_HEADS * GLA_DK
GLA_V = GLA_HEADS * GLA_DV
ODD_IN = 2 * GLA_K + GLA_V + 2 * GLA_GATE_RANK + GLA_V + 2 * SG_WIDTH
ODD_MIX = GLA_V + SG_WIDTH

kernel_name = "hybrid_mla_swa_gla_gmlp_hmoe_dit"


def rmsnorm(x, g):
    xf = x.astype(F32)
    y = xf * lax.rsqrt(jnp.mean(xf * xf, axis=-1, keepdims=True) + EPS)
    return (y * g.astype(F32)).astype(x.dtype)


def layernorm(x, g, b):
    xf = x.astype(F32)
    mu = jnp.mean(xf, axis=-1, keepdims=True)
    xc = xf - mu
    var = jnp.mean(xc * xc, axis=-1, keepdims=True)
    return (xc * lax.rsqrt(var + EPS) * g.astype(F32) + b.astype(F32)).astype(x.dtype)


def split_cols(z, sizes):
    cuts, acc = [], 0
    for s in sizes[:-1]:
        acc += s
        cuts.append(acc)
    return jnp.split(z, cuts, axis=-1)


def axial_rope(rows, dim):
    row = jnp.repeat(jnp.arange(rows, dtype=F32), GRID_W)
    col = jnp.tile(jnp.arange(GRID_W, dtype=F32), rows)
    half = dim // 2
    inv = jnp.power(ROPE_BASE, -jnp.arange(0, half, 2, dtype=F32) / half)
    ar = row[:, None] * inv[None, :]
    ac = col[:, None] * inv[None, :]
    ang = jnp.concatenate([ar, ar, ac, ac], axis=-1)
    return jnp.cos(ang), jnp.sin(ang)


def rotate_half(t):
    t1, t2 = jnp.split(t, 2, axis=-1)
    return jnp.concatenate([-t2, t1], axis=-1)


def apply_axial_rope(t, rope):
    cos, sin = rope
    half = t.shape[-1] // 2
    rot = jnp.concatenate([rotate_half(t[..., :half]), rotate_half(t[..., half:])], axis=-1)
    return (t * cos[None, :, None, :] + rot * sin[None, :, None, :]).astype(t.dtype)


def adaln(cond, w, b):
    return jnp.split(jax.nn.silu(cond) @ w + b, 6, axis=-1)


def modulate(h, shift, scale):
    return h * (1 + scale) + shift


def mla_heads(cq, ckv, kr, q_norm_g, w_uq, kv_norm_g, w_ukv, rope):
    B, n, _ = cq.shape
    q = (rmsnorm(cq, q_norm_g) @ w_uq).reshape(B, n, MLA_HEADS, MLA_NOPE + MLA_ROPE)
    kv = (rmsnorm(ckv, kv_norm_g) @ w_ukv).reshape(B, n, MLA_HEADS, MLA_NOPE + MLA_V)
    q_nope, q_rope = q[..., :MLA_NOPE], q[..., MLA_NOPE:]
    k_nope, v = kv[..., :MLA_NOPE], kv[..., MLA_NOPE:]
    k_rope = kr[:, :, None, :]
    if rope is not None:
        q_rope = apply_axial_rope(q_rope, rope)
        k_rope = apply_axial_rope(k_rope, rope)
    k_rope = jnp.broadcast_to(k_rope, (B, n, MLA_HEADS, MLA_ROPE))
    q = jnp.concatenate([q_nope, q_rope], axis=-1)
    k = jnp.concatenate([k_nope, k_rope], axis=-1)
    return q, k, v


def dense_attention(q, k, v):
    B, n, H, d = q.shape
    nb = n // Q_BLOCK
    scale = d ** -0.5
    qb = jnp.moveaxis(q.reshape(B, nb, Q_BLOCK, H, d), 1, 0)

    def one_block(qblk):
        s = jnp.einsum("bqhd,bkhd->bhqk", qblk, k).astype(F32) * scale
        p = jax.nn.softmax(s, axis=-1).astype(v.dtype)
        return jnp.einsum("bhqk,bkhd->bqhd", p, v)

    o = lax.map(one_block, qb)
    return jnp.moveaxis(o, 0, 1).reshape(B, n, H, v.shape[-1])


def window_gqa_latent(q, k, v, kc, vc, sink):
    B, S, H, d = q.shape
    W = WIN_BLOCK
    nb = S // W
    rep = H // WIN_KV_HEADS
    Lc = kc.shape[1]
    qb = q.reshape(B, nb, W, WIN_KV_HEADS, rep, d)

    def band(t):
        tb = t.reshape(B, nb, W, WIN_KV_HEADS, d)
        tp = jnp.pad(tb, ((0, 0), (1, 1), (0, 0), (0, 0), (0, 0)))
        return jnp.concatenate([tp[:, :-2], tp[:, 1:-1], tp[:, 2:]], axis=2)

    kw, vw = band(k), band(v)
    qpos = jnp.arange(nb)[:, None] * W + jnp.arange(W)[None, :]
    kpos = jnp.arange(nb)[:, None] * W + jnp.arange(-W, 2 * W)[None, :]
    mask = ((jnp.abs(qpos[:, :, None] - kpos[:, None, :]) <= WINDOW)
            & (kpos[:, None, :] >= 0) & (kpos[:, None, :] < S))
    scale = d ** -0.5
    s_win = jnp.einsum("bnqgrd,bnkgd->bngrqk", qb, kw).astype(F32) * scale
    s_win = jnp.where(mask[None, :, None, None], s_win, -jnp.inf)
    s_ctx = jnp.einsum("bnqgrd,bkgd->bngrqk", qb, kc).astype(F32) * scale
    s_sink = jnp.broadcast_to(sink.astype(F32).reshape(1, 1, WIN_KV_HEADS, rep, 1, 1),
                              s_win.shape[:-1] + (1,))
    p = jax.nn.softmax(jnp.concatenate([s_win, s_ctx, s_sink], axis=-1), axis=-1).astype(v.dtype)
    o = (jnp.einsum("bngrqk,bnkgd->bnqgrd", p[..., :3 * W], vw)
         + jnp.einsum("bngrqk,bkgd->bnqgrd", p[..., 3 * W:3 * W + Lc], vc))
    return o.reshape(B, S, H, d)


def gqa_context(qc, kc, vc, sink):
    B, L, H, d = qc.shape
    rep = H // WIN_KV_HEADS
    qg = qc.reshape(B, L, WIN_KV_HEADS, rep, d)
    s = jnp.einsum("bqgrd,bkgd->bgrqk", qg, kc).astype(F32) * (d ** -0.5)
    s_sink = jnp.broadcast_to(sink.astype(F32).reshape(1, WIN_KV_HEADS, rep, 1, 1), s.shape[:-1] + (1,))
    p = jax.nn.softmax(jnp.concatenate([s, s_sink], axis=-1), axis=-1)[..., :L].astype(vc.dtype)
    return jnp.einsum("bgrqk,bkgd->bqgrd", p, vc).reshape(B, L, H, d)


def even_mixer(hl, hc, w_in, q_norm_g, w_uq, kv_norm_g, w_ukv, sink, w_out, rope_mla, rope_win, with_ctx):
    def heads(h, r_mla, r_win):
        B, n, _ = h.shape
        cq, ckv, kr, qw, kw, vw = split_cols(
            h @ w_in, (MLA_Q_RANK, MLA_KV_RANK, MLA_ROPE, WIN_Q, WIN_KV, WIN_KV))
        qa, ka, va = mla_heads(cq, ckv, kr, q_norm_g, w_uq, kv_norm_g, w_ukv, r_mla)
        qb = qw.reshape(B, n, WIN_HEADS, WIN_HEAD_DIM)
        kb = kw.reshape(B, n, WIN_KV_HEADS, WIN_HEAD_DIM)
        vb = vw.reshape(B, n, WIN_KV_HEADS, WIN_HEAD_DIM)
        if r_win is not None:
            qb = apply_axial_rope(qb, r_win)
            kb = apply_axial_rope(kb, r_win)
        return qa, ka, va, qb, kb, vb

    qa_l, ka_l, va_l, qb_l, kb_l, vb_l = heads(hl, rope_mla, rope_win)
    qa_c, ka_c, va_c, qb_c, kb_c, vb_c = heads(hc, None, None)
    B, n, _ = hl.shape
    oa_l = dense_attention(qa_l, jnp.concatenate([ka_l, ka_c], axis=1), jnp.concatenate([va_l, va_c], axis=1))
    ob_l = window_gqa_latent(qb_l, kb_l, vb_l, kb_c, vb_c, sink)
    out_l = jnp.concatenate([oa_l.reshape(B, n, -1), ob_l.reshape(B, n, -1)], axis=-1) @ w_out
    if not with_ctx:
        return out_l, None
    L = hc.shape[1]
    oa_c = dense_attention(qa_c, ka_c, va_c)
    ob_c = gqa_context(qb_c, kb_c, vb_c, sink)
    out_c = jnp.concatenate([oa_c.reshape(B, L, -1), ob_c.reshape(B, L, -1)], axis=-1) @ w_out
    return out_l, out_c


def gla_chunked(q, k, v, log_a, s0):
    B, n, H, dk = q.shape
    C = GLA_CHUNK
    nc = n // C

    def to_chunks(t):
        return jnp.moveaxis(t.astype(F32).reshape(B, nc, C, *t.shape[2:]), 1, 0)

    tri = jnp.tril(jnp.ones((C, C), dtype=bool))

    def step(S, inp):
        qc, kc, vc, gc = inp
        b = jnp.cumsum(gc, axis=1)
        inter = jnp.einsum("bthk,bhkv->bthv", qc * jnp.exp(b), S)
        decay = jnp.exp(jnp.where(tri[None, :, :, None, None], b[:, :, None] - b[:, None, :], -jnp.inf))
        A = jnp.einsum("bthk,btshk,bshk->bhts", qc, decay, kc)
        intra = jnp.einsum("bhts,bshv->bthv", A, vc)
        b_last = b[:, -1]
        k_dec = kc * jnp.exp(b_last[:, None] - b)
        S_new = S * jnp.exp(b_last)[..., None] + jnp.einsum("bshk,bshv->bhkv", k_dec, vc)
        return S_new, inter + intra

    S_fin, o = lax.scan(step, s0.astype(F32), (to_chunks(q), to_chunks(k), to_chunks(v), to_chunks(log_a)))
    return jnp.moveaxis(o, 0, 1).reshape(B, n, H, v.shape[-1]), S_fin


def gla_bidirectional(lat, ctx_, with_ctx):
    ql, kl, vl, fl, bl = lat
    qc, kc, vc, fc, bc = ctx_
    B = ql.shape[0]
    s0 = jnp.zeros((B, GLA_HEADS, GLA_DK, GLA_DV), F32)

    def flip(t):
        return jnp.flip(t, axis=1)

    oc_f, sc_f = gla_chunked(qc, kc, vc, fc, s0)
    oc_b, sc_b = gla_chunked(flip(qc), flip(kc), flip(vc), flip(bc), s0)
    ol_f, _ = gla_chunked(ql, kl, vl, fl, sc_f)
    ol_b, _ = gla_chunked(flip(ql), flip(kl), flip(vl), flip(bl), sc_b)
    ol = ol_f + flip(ol_b)
    oc = oc_f + flip(oc_b) if with_ctx else None
    return ol, oc


def head_rmsnorm(o, g):
    B, n = o.shape[:2]
    o = o * lax.rsqrt(jnp.mean(o * o, axis=-1, keepdims=True) + EPS)
    return o.reshape(B, n, -1) * g.astype(F32)


def spatial_gating(u, vg, ln_g, ln_b, w_s, b_s):
    u = jax.nn.gelu(u)
    vg = layernorm(jax.nn.gelu(vg), ln_g, ln_b)
    B, n, _ = vg.shape
    nck = n // SG_CHUNK
    vb = vg.reshape(B, nck, SG_CHUNK, SG_GROUPS, SG_WIDTH // SG_GROUPS)
    s = jnp.einsum("gts,bnsgc->bntgc", w_s, vb) + b_s.T[None, None, :, :, None]
    return u * s.reshape(B, n, SG_WIDTH)


def odd_mixer(hl, hc, w_in, w_g2, b_g, gla_g, ln_g, ln_b, w_s, b_s, w_out, with_ctx):
    def project(h):
        B, n, _ = h.shape
        q, k, v, g, r, u, vg = split_cols(
            h @ w_in, (GLA_K, GLA_K, GLA_V, 2 * GLA_GATE_RANK, GLA_V, SG_WIDTH, SG_WIDTH))
        q = q.reshape(B, n, GLA_HEADS, GLA_DK) * (GLA_DK ** -0.5)
        k = k.reshape(B, n, GLA_HEADS, GLA_DK)
        v = v.reshape(B, n, GLA_HEADS, GLA_DV)
        z = jnp.einsum("bnzr,zrk->bnzk", g.reshape(B, n, 2, GLA_GATE_RANK).astype(F32),
                       w_g2.astype(F32)) + b_g.astype(F32)
        log_a = jax.nn.log_sigmoid(z) / GLA_TAU
        la_f = log_a[:, :, 0].reshape(B, n, GLA_HEADS, GLA_DK)
        la_b = log_a[:, :, 1].reshape(B, n, GLA_HEADS, GLA_DK)
        return (q, k, v, la_f, la_b), r, u, vg

    gl, rl, ul, vl = project(hl)
    gc, rc, uc, vc = project(hc)
    ol, oc = gla_bidirectional(gl, gc, with_ctx)
    cl = (head_rmsnorm(ol, gla_g) * jax.nn.silu(rl.astype(F32))).astype(hl.dtype)
    dl = spatial_gating(ul, vl, ln_g, ln_b, w_s, b_s)
    out_l = jnp.concatenate([cl, dl], axis=-1) @ w_out
    if not with_ctx:
        return out_l, None
    cc = (head_rmsnorm(oc, gla_g) * jax.nn.silu(rc.astype(F32))).astype(hc.dtype)
    dc = spatial_gating(uc, vc, ln_g, ln_b, w_s, b_s)
    out_c = jnp.concatenate([cc, dc], axis=-1) @ w_out
    return out_l, out_c


def hier_moe(h, w_rg, w_re, w_gate, w_up, w_down):
    shp = h.shape
    x = h.reshape(-1, shp[-1])
    N = x.shape[0]
    g_logits = (x @ w_rg).astype(F32)
    g_prob = jax.nn.softmax(g_logits, axis=-1)
    g_sel = jnp.argmax(g_logits, axis=-1)
    g_oh = jax.nn.one_hot(g_sel, MOE_GROUPS, dtype=F32)
    e_logits = (x @ w_re).astype(F32).reshape(N, MOE_GROUPS, MOE_PER_GROUP)
    e_in = jnp.einsum("ngp,ng->np", e_logits, g_oh)
    top_v, top_i = lax.top_k(e_in, MOE_TOPK)
    w_top = jax.nn.softmax(top_v, axis=-1) * jnp.max(g_prob, axis=-1, keepdims=True)
    local = jnp.sum(jax.nn.one_hot(top_i, MOE_PER_GROUP, dtype=F32) * w_top[..., None], axis=1)
    combine = (g_oh[:, :, None] * local[:, None, :]).astype(x.dtype)
    y = jnp.zeros_like(x)
    for g in range(MOE_GROUPS):
        sl = slice(g * MOE_PER_GROUP, (g + 1) * MOE_PER_GROUP)
        hg = (jax.nn.silu(jnp.einsum("nd,edf->nef", x, w_gate[sl]))
              * jnp.einsum("nd,edf->nef", x, w_up[sl]))
        y = y + jnp.einsum("nef,efd->nd", hg * combine[:, g, :, None], w_down[sl])
    return y.reshape(shp)


def setup_inputs(seed: int = 0) -> dict:
    key = jax.random.key(seed)
    ks = iter(jax.random.split(key, 32))

    def nrm(shape, scale):
        return jax.random.normal(next(ks), shape, F32) * scale

    def gain(shape):
        return 1.0 + 0.02 * jax.random.normal(next(ks), shape, F32)

    D = D_MODEL
    return {
        "x": nrm((BATCH, SEQ, D), 1.0),
        "c": nrm((BATCH, D), 1.0),
        "ctx": nrm((BATCH, CTX_LEN, D), 1.0),
        "c_ctx": nrm((D,), 1.0),
        "ada_w": nrm((DEPTH, D, 6 * D), 0.5 * D ** -0.5),
        "ada_b": nrm((DEPTH, 6 * D), 0.02),
        "norm_mix_g": gain((DEPTH, D)),
        "norm_ffn_g": gain((DEPTH, D)),
        "even_w_in": nrm((N_EVEN, D, EVEN_IN), D ** -0.5),
        "mla_q_norm_g": gain((N_EVEN, MLA_Q_RANK)),
        "mla_w_uq": nrm((N_EVEN, MLA_Q_RANK, MLA_HEADS * (MLA_NOPE + MLA_ROPE)), MLA_Q_RANK ** -0.5),
        "mla_kv_norm_g": gain((N_EVEN, MLA_KV_RANK)),
        "mla_w_ukv": nrm((N_EVEN, MLA_KV_RANK, MLA_HEADS * (MLA_NOPE + MLA_V)), MLA_KV_RANK ** -0.5),
        "win_sink": nrm((N_EVEN, WIN_HEADS), 0.5),
        "even_w_out": nrm((N_EVEN, EVEN_MIX, D), EVEN_MIX ** -0.5),
        "odd_w_in": nrm((N_ODD, D, ODD_IN), D ** -0.5),
        "gla_w_g2": nrm((N_ODD, 2, GLA_GATE_RANK, GLA_K), GLA_GATE_RANK ** -0.5),
        "gla_b_g": nrm((N_ODD, 2, GLA_K), 0.1),
        "gla_norm_g": gain((N_ODD, GLA_V)),
        "sg_ln_g": gain((N_ODD, SG_WIDTH)),
        "sg_ln_b": nrm((N_ODD, SG_WIDTH), 0.02),
        "sg_w_s": nrm((N_ODD, SG_GROUPS, SG_CHUNK, SG_CHUNK), SG_CHUNK ** -0.5),
        "sg_b_s": gain((N_ODD, SG_GROUPS, SG_CHUNK)),
        "odd_w_out": nrm((N_ODD, ODD_MIX, D), ODD_MIX ** -0.5),
        "moe_w_rg": nrm((DEPTH, D, MOE_GROUPS), D ** -0.5),
        "moe_w_re": nrm((DEPTH, D, MOE_EXPERTS), D ** -0.5),
        "moe_w_gate": nrm((DEPTH, MOE_EXPERTS, D, MOE_HIDDEN), D ** -0.5),
        "moe_w_up": nrm((DEPTH, MOE_EXPERTS, D, MOE_HIDDEN), D ** -0.5),
        "moe_w_down": nrm((DEPTH, MOE_EXPERTS, MOE_HIDDEN, D), MOE_HIDDEN ** -0.5),
        "final_norm_g": gain((D,)),
    }


def reference(x, c, ctx, c_ctx, ada_w, ada_b, norm_mix_g, norm_ffn_g,
              even_w_in, mla_q_norm_g, mla_w_uq, mla_kv_norm_g, mla_w_ukv, win_sink, even_w_out,
              odd_w_in, gla_w_g2, gla_b_g, gla_norm_g, sg_ln_g, sg_ln_b, sg_w_s, sg_b_s, odd_w_out,
              moe_w_rg, moe_w_re, moe_w_gate, moe_w_up, moe_w_down, final_norm_g):
    n = x.shape[1]
    ROWS = n // GRID_W
    rope_mla = axial_rope(ROWS, MLA_ROPE)
    rope_win = axial_rope(ROWS, WIN_HEAD_DIM)
    xl, xc = x, ctx
    for layer in range(DEPTH):
        with_ctx = layer < DEPTH - 1
        sh1, sc1, gt1, sh2, sc2, gt2 = [m[:, None, :] for m in adaln(c, ada_w[layer], ada_b[layer])]
        csh1, csc1, cgt1, csh2, csc2, cgt2 = adaln(c_ctx, ada_w[layer], ada_b[layer])
        hl = modulate(rmsnorm(xl, norm_mix_g[layer]), sh1, sc1)
        hc = modulate(rmsnorm(xc, norm_mix_g[layer]), csh1, csc1)
        i = layer // 2
        if layer % 2 == 0:
            ml, mc = even_mixer(hl, hc, even_w_in[i], mla_q_norm_g[i], mla_w_uq[i], mla_kv_norm_g[i],
                                mla_w_ukv[i], win_sink[i], even_w_out[i], rope_mla, rope_win, with_ctx)
        else:
            ml, mc = odd_mixer(hl, hc, odd_w_in[i], gla_w_g2[i], gla_b_g[i], gla_norm_g[i], sg_ln_g[i],
                               sg_ln_b[i], sg_w_s[i], sg_b_s[i], odd_w_out[i], with_ctx)
        xl = xl + gt1 * ml
        xl = xl + gt2 * hier_moe(modulate(rmsnorm(xl, norm_ffn_g[layer]), sh2, sc2),
                                 moe_w_rg[layer], moe_w_re[layer], moe_w_gate[layer],
                                 moe_w_up[layer], moe_w_down[layer])
        if with_ctx:
            xc = xc + cgt1 * mc
            xc = xc + cgt2 * hier_moe(modulate(rmsnorm(xc, norm_ffn_g[layer]), csh2, csc2),
                                      moe_w_rg[layer], moe_w_re[layer], moe_w_gate[layer],
                                      moe_w_up[layer], moe_w_down[layer])
    return rmsnorm(xl, final_norm_g)
```

```python
import functools

import numpy as np
import jax
import jax.numpy as jnp
from jax import lax
from jax.experimental import pallas as pl
from jax.experimental.pallas import tpu as pltpu

F32 = jnp.float32
BF16 = jnp.bfloat16
I32 = jnp.int32

D_MODEL = 1024
GRID_W = 64
EPS = 1e-6
ROPE_BASE = 10000.0
MLA_HEADS = 8
MLA_Q_RANK = 256
MLA_KV_RANK = 128
MLA_NOPE = 64
MLA_ROPE = 32
MLA_V = 64
WIN_HEADS = 8
WIN_KV_HEADS = 2
WIN_HEAD_DIM = 64
WIN_BLOCK = 128
GLA_HEADS = 4
GLA_DK = 64
GLA_DV = 128
GLA_GATE_RANK = 16
GLA_TAU = 16.0
GLA_K = GLA_HEADS * GLA_DK
GLA_V = GLA_HEADS * GLA_DV
SG_GROUPS = 4
SG_CHUNK = 128
SG_WIDTH = 512
MOE_GROUPS = 4
MOE_PER_GROUP = 8
MOE_EXPERTS = 32
MOE_HIDDEN = 512

LANES = 128
GLA_BLOCK = 128
MOE_TILE = 256
NEG = -1e30
VMEM_LIMIT = 56 * 1024 * 1024


def _cparams(sem):
    return pltpu.CompilerParams(dimension_semantics=sem, vmem_limit_bytes=VMEM_LIMIT)


def _dot(a, b):
    return jnp.dot(a, b, preferred_element_type=F32)


def _dot_nt(a, b):
    return lax.dot_general(a, b, (((1,), (1,)), ((), ())), preferred_element_type=F32)


def _split2(a):
    hi = a.astype(BF16)
    lo = (a - hi.astype(F32)).astype(BF16)
    return hi, lo


def _split3(a):
    hi = a.astype(BF16)
    r = a - hi.astype(F32)
    mid = r.astype(BF16)
    lo = (r - mid.astype(F32)).astype(BF16)
    return hi, mid, lo


def _rms(x, g):
    ms = jnp.mean(x * x, axis=-1, keepdims=True)
    return x * lax.rsqrt(ms + EPS) * g


def _lane_tile(t, reps):
    return t if reps == 1 else jnp.concatenate([t] * reps, axis=1)


def _rope(t, cos, sin, quarter):
    n = t.shape[1]
    lane = lax.broadcasted_iota(I32, t.shape, 1)
    first = (lane & (2 * quarter - 1)) < quarter
    rot = jnp.where(first, -pltpu.roll(t, n - quarter, 1), pltpu.roll(t, quarter, 1))
    return t * cos + rot * sin


def _adaln_kernel(c_ref, w_ref, b_ref, o_ref):
    c = c_ref[...]
    s_hi, s_lo = _split2(c * jax.nn.sigmoid(c))
    w_hi, w_lo = _split2(w_ref[...])
    o_ref[...] = _dot(s_hi, w_hi) + _dot(s_lo, w_hi) + _dot(s_hi, w_lo) + b_ref[...]


def _adaln(cond8, ada_w, ada_b):
    depth, d, n6 = ada_w.shape
    tn = 1536
    return pl.pallas_call(
        _adaln_kernel,
        out_shape=jax.ShapeDtypeStruct((depth, 8, n6), F32),
        grid=(depth, n6 // tn),
        in_specs=[
            pl.BlockSpec((8, d), lambda l, j: (0, 0)),
            pl.BlockSpec((None, d, tn), lambda l, j: (l, 0, j)),
            pl.BlockSpec((None, 1, tn), lambda l, j: (l, 0, j)),
        ],
        out_specs=pl.BlockSpec((None, 8, tn), lambda l, j: (l, 0, j)),
        compiler_params=_cparams(("parallel", "parallel")),
    )(cond8, ada_w, ada_b.reshape(depth, 1, n6))


def _even_in_kernel(x_ref, mod_ref, gn_ref, win_ref, qg_ref, wuq_ref, kvg_ref, wukk_ref, wukv_ref,
                    cq_ref, sq_ref, cw_ref, sw_ref,
                    qm_ref, km_ref, vm_ref, qw_ref, kw_ref, vw_ref):
    d = D_MODEL
    mod = mod_ref[...]
    h = _rms(x_ref[...], gn_ref[...]) * (1.0 + mod[:, d:2 * d]) + mod[:, 0:d]
    z = _dot(h.astype(BF16), win_ref[...])
    cq, sq, cw, sw = cq_ref[...], sq_ref[...], cw_ref[...], sw_ref[...]
    cqn = _rms(z[:, 0:256], qg_ref[...]).astype(BF16)
    q = _dot(cqn, wuq_ref[...])
    q = _rope(q, _lane_tile(cq, 8), _lane_tile(sq, 8), MLA_ROPE // 4)
    qm_ref[...] = (q * ((MLA_NOPE + MLA_ROPE) ** -0.5)).astype(BF16)
    ckvn = _rms(z[:, 256:384], kvg_ref[...]).astype(BF16)
    kn = _dot(ckvn, wukk_ref[...])
    kr = _rope(z[:, 384:512], cq, sq, MLA_ROPE // 4)
    km_ref[...] = (kn + _lane_tile(kr, 8)).astype(BF16)
    vm_ref[...] = _dot(ckvn, wukv_ref[...]).astype(BF16)
    qw = _rope(z[:, 512:1024], _lane_tile(cw, 4), _lane_tile(sw, 4), WIN_HEAD_DIM // 4)
    qw_ref[...] = (qw * (WIN_HEAD_DIM ** -0.5)).astype(BF16)
    kw = _rope(z[:, 1024:1280], _lane_tile(cw, 2), _lane_tile(sw, 2), WIN_HEAD_DIM // 4)
    kw_ref[...] = kw.astype(BF16)
    vw_ref[...] = z[:, 1280:1536].astype(BF16)


def _even_in(x, mod, mod_row, gn, wts, tabs, tm):
    b, n, d = x.shape
    win, qg, wuq, kvg, wukk, wukv = wts
    nt = n // tm
    row = (lambda bi, i: (bi, 0, 0)) if mod_row is None else (lambda bi, i: (mod_row, 0, 0))
    full = lambda a: pl.BlockSpec(a.shape, lambda bi, i: (0,) * a.ndim)
    tab = pl.BlockSpec((tm, LANES), lambda bi, i: (i, 0))
    outw = (1024, 1024, 512, 512, 256, 256)
    return pl.pallas_call(
        _even_in_kernel,
        out_shape=[jax.ShapeDtypeStruct((b, n, w), BF16) for w in outw],
        grid=(b, nt),
        in_specs=[pl.BlockSpec((None, tm, d), lambda bi, i: (bi, i, 0)),
                  pl.BlockSpec((None, 1, 6 * d), row),
                  full(gn), full(win), full(qg), full(wuq), full(kvg), full(wukk), full(wukv),
                  tab, tab, tab, tab],
        out_specs=[pl.BlockSpec((None, tm, w), lambda bi, i: (bi, i, 0)) for w in outw],
        compiler_params=_cparams(("parallel", "parallel")),
    )(x, mod, gn, win, qg, wuq, kvg, wukk, wukv, *tabs)


def _mla_attn_kernel(nseg, q_ref, *refs):
    ks, vs, o_ref = refs[0:2 * nseg:2], refs[1:2 * nseg:2], refs[2 * nseg]
    tq = q_ref.shape[0]
    lane = lax.broadcasted_iota(I32, (tq, LANES), 1)
    for hp in range(MLA_HEADS // 2):
        pair = slice(hp * LANES, (hp + 1) * LANES)
        outs = []
        for e in range(2):
            hs = slice((2 * hp + e) * LANES, (2 * hp + e + 1) * LANES)
            qh = q_ref[:, hs]
            ss = [_dot_nt(qh, k[:, hs]) for k in ks]
            m = functools.reduce(jnp.maximum, [jnp.max(s, axis=-1, keepdims=True) for s in ss])
            ps = [jnp.exp(s - m) for s in ss]
            l = functools.reduce(jnp.add, [jnp.sum(p, axis=-1, keepdims=True) for p in ps])
            o = functools.reduce(jnp.add, [_dot(p.astype(BF16), v[:, pair]) for p, v in zip(ps, vs)])
            outs.append(o * (1.0 / l))
        o_ref[:, pair] = jnp.where(lane < MLA_V, outs[0], outs[1]).astype(BF16)


def _mla_attn(q, segs, tq):
    b, n, _ = q.shape
    in_specs = [pl.BlockSpec((None, tq, 1024), lambda bi, i: (bi, i, 0))]
    args = [q]
    for k, v in segs:
        lk = k.shape[1]
        in_specs += [pl.BlockSpec((None, lk, 1024), lambda bi, i: (bi, 0, 0)),
                     pl.BlockSpec((None, lk, 512), lambda bi, i: (bi, 0, 0))]
        args += [k, v]
    return pl.pallas_call(
        functools.partial(_mla_attn_kernel, len(segs)),
        out_shape=jax.ShapeDtypeStruct((b, n, 512), BF16),
        grid=(b, n // tq),
        in_specs=in_specs,
        out_specs=pl.BlockSpec((None, tq, 512), lambda bi, i: (bi, i, 0)),
        compiler_params=_cparams(("parallel", "parallel")),
    )(*args)


def _gqa_kernel(has_win, nb, sink_ref, q_ref, *refs):
    if has_win:
        kp, kc, kn, vp, vc, vn, kx, vx, o_ref = refs
    else:
        kx, vx, o_ref = refs
    tq = q_ref.shape[0]
    i = pl.program_id(1)
    lane = lax.broadcasted_iota(I32, (tq, LANES), 1)
    row2 = lax.broadcasted_iota(I32, (2 * tq, 1), 0)
    half = WIN_HEAD_DIM
    for j in range(WIN_HEADS // 2):
        g = j // 2
        gs = slice(g * LANES, (g + 1) * LANES)
        qp = q_ref[:, j * LANES:(j + 1) * LANES]
        zero = jnp.zeros_like(qp)
        q2 = jnp.concatenate([jnp.where(lane < half, qp, zero), jnp.where(lane >= half, qp, zero)], axis=0)
        if has_win:
            kcat = jnp.concatenate([kp[:, gs], kc[:, gs], kn[:, gs], kx[:, gs]], axis=0)
            vcat = jnp.concatenate([vp[:, gs], vc[:, gs], vn[:, gs], vx[:, gs]], axis=0)
        else:
            kcat, vcat = kx[:, gs], vx[:, gs]
        s = _dot_nt(q2, kcat)
        if has_win:
            w = WIN_BLOCK
            r = lax.broadcasted_iota(I32, s.shape, 0) & (tq - 1)
            c = lax.broadcasted_iota(I32, s.shape, 1)
            big = jnp.int32(1 << 20)
            no_prev = jnp.where(i > 0, 0, big)
            no_next = jnp.where(i < nb - 1, 0, big)
            ok_prev = c >= r + no_prev
            ok_next = (c - 2 * w) <= r - no_next
            valid = ((c >= w) | ok_prev) & ((c < 2 * w) | (c >= 3 * w) | ok_next)
            s = jnp.where(valid, s, NEG)
        sk = jnp.where(row2 < tq, sink_ref[2 * j], sink_ref[2 * j + 1])
        m = jnp.maximum(jnp.max(s, axis=-1, keepdims=True), sk)
        p = jnp.exp(s - m)
        l = jnp.sum(p, axis=-1, keepdims=True) + jnp.exp(sk - m)
        o2 = _dot(p.astype(BF16), vcat) * (1.0 / l)
        o_ref[:, j * LANES:(j + 1) * LANES] = jnp.where(lane < half, o2[:tq], o2[tq:]).astype(BF16)


def _gqa(q, k, v, kx, vx, sink, has_win):
    b, n, _ = q.shape
    lc = kx.shape[1]
    smem = pl.BlockSpec(memory_space=pltpu.SMEM)
    ctxs = pl.BlockSpec((None, lc, 256), lambda bi, i: (bi, 0, 0))
    if has_win:
        tq = WIN_BLOCK
        nb = n // tq
        blk = lambda f: pl.BlockSpec((None, tq, 256), f)
        prev = lambda bi, i: (bi, jnp.maximum(i - 1, 0), 0)
        cur = lambda bi, i: (bi, i, 0)
        nxt = lambda bi, i: (bi, jnp.minimum(i + 1, nb - 1), 0)
        in_specs = [smem, pl.BlockSpec((None, tq, 512), cur),
                    blk(prev), blk(cur), blk(nxt), blk(prev), blk(cur), blk(nxt), ctxs, ctxs]
        args = (sink, q, k, k, k, v, v, v, kx, vx)
    else:
        tq, nb = n, 1
        in_specs = [smem, pl.BlockSpec((None, tq, 512), lambda bi, i: (bi, i, 0)), ctxs, ctxs]
        args = (sink, q, kx, vx)
    return pl.pallas_call(
        functools.partial(_gqa_kernel, has_win, nb),
        out_shape=jax.ShapeDtypeStruct((b, n, 512), BF16),
        grid=(b, nb),
        in_specs=in_specs,
        out_specs=pl.BlockSpec((None, tq, 512), lambda bi, i: (bi, i, 0)),
        compiler_params=_cparams(("parallel", "parallel")),
    )(*args)


def _even_out_kernel(x_ref, a_ref, b_ref, w_ref, mod_ref, o_ref):
    d = D_MODEL
    y = _dot(a_ref[...], w_ref[0:512, :]) + _dot(b_ref[...], w_ref[512:1024, :])
    o_ref[...] = x_ref[...] + mod_ref[:, 2 * d:3 * d] * y


def _even_out(x, oa, ob, w, mod, mod_row, tm):
    b, n, d = x.shape
    row = (lambda bi, i: (bi, 0, 0)) if mod_row is None else (lambda bi, i: (mod_row, 0, 0))
    act = lambda wd: pl.BlockSpec((None, tm, wd), lambda bi, i: (bi, i, 0))
    return pl.pallas_call(
        _even_out_kernel,
        out_shape=jax.ShapeDtypeStruct((b, n, d), F32),
        grid=(b, n // tm),
        in_specs=[act(d), act(512), act(512), pl.BlockSpec(w.shape, lambda bi, i: (0, 0)),
                  pl.BlockSpec((None, 1, 6 * d), row)],
        out_specs=act(d),
        compiler_params=_cparams(("parallel", "parallel")),
    )(x, oa, ob, w, mod)


def _log_sigmoid(z):
    return jnp.minimum(z, 0.0) - jnp.log(1.0 + jnp.exp(-jnp.abs(z)))


def _odd_in_kernel(x_ref, mod_ref, gn_ref, win_ref, wvt_ref, wg_ref, bg_ref, lng_ref, lnb_ref, ws_ref, bst_ref,
                   q_ref, k_ref, v_ref, vt_ref, la_ref, r_ref, dl_ref):
    d = D_MODEL
    tm = x_ref.shape[0]
    mod = mod_ref[...]
    h = (_rms(x_ref[...], gn_ref[...]) * (1.0 + mod[:, d:2 * d]) + mod[:, 0:d]).astype(BF16)
    z = _dot(h, win_ref[...])
    q_ref[...] = z[:, 0:256] * (GLA_DK ** -0.5)
    k_ref[...] = z[:, 256:512]
    v_ref[...] = z[:, 512:1024].astype(BF16)
    vt_ref[...] = _dot_nt(wvt_ref[...], h).astype(BF16)
    g_hi, g_lo = _split2(z[:, 1024:1152])
    w_hi, w_lo = _split2(wg_ref[...])
    zg = _dot(g_hi, w_hi) + _dot(g_lo, w_hi) + _dot(g_hi, w_lo) + bg_ref[...]
    la_ref[...] = _log_sigmoid(zg) / GLA_TAU
    r_ref[...] = z[:, 1152:1664]
    u = jax.nn.gelu(z[:, 1664:2176])
    vg = jax.nn.gelu(z[:, 2176:2688])
    mu = jnp.mean(vg, axis=-1, keepdims=True)
    vc = vg - mu
    var = jnp.mean(vc * vc, axis=-1, keepdims=True)
    vn = (vc * lax.rsqrt(var + EPS) * lng_ref[...] + lnb_ref[...]).astype(BF16)
    bst = bst_ref[...]
    for c in range(tm // SG_CHUNK):
        rows = slice(c * SG_CHUNK, (c + 1) * SG_CHUNK)
        parts = []
        for g in range(SG_GROUPS):
            cols = slice(g * LANES, (g + 1) * LANES)
            parts.append(_dot(ws_ref[g], vn[rows, cols]) + bst[:, g:g + 1])
        dl_ref[rows, :] = (u[rows, :] * jnp.concatenate(parts, axis=1)).astype(BF16)


def _odd_in(x, mod, mod_row, gn, wts, tm):
    b, n, d = x.shape
    win, wvt, wg, bg, lng, lnb, ws, bst = wts
    row = (lambda bi, i: (bi, 0, 0)) if mod_row is None else (lambda bi, i: (mod_row, 0, 0))
    full = lambda a: pl.BlockSpec(a.shape, lambda bi, i: (0,) * a.ndim)
    act = lambda wd: pl.BlockSpec((None, tm, wd), lambda bi, i: (bi, i, 0))
    outs = [((b, n, 256), F32, act(256)), ((b, n, 256), F32, act(256)), ((b, n, 512), BF16, act(512)),
            ((b, 512, n), BF16, pl.BlockSpec((None, 512, tm), lambda bi, i: (bi, 0, i))),
            ((b, n, 512), F32, act(512)), ((b, n, 512), F32, act(512)), ((b, n, 512), BF16, act(512))]
    return pl.pallas_call(
        _odd_in_kernel,
        out_shape=[jax.ShapeDtypeStruct(s, t) for s, t, _ in outs],
        grid=(b, n // tm),
        in_specs=[act(d), pl.BlockSpec((None, 1, 6 * d), row), full(gn), full(win), full(wvt), full(wg),
                  full(bg), full(lng), full(lnb), full(ws), full(bst)],
        out_specs=[sp for _, _, sp in outs],
        compiler_params=_cparams(("parallel", "parallel")),
    )(x, mod, gn, win, wvt, wg, bg, lng, lnb, ws, bst)


def _gla_tables():
    c = GLA_BLOCK
    t = np.arange(c)[:, None]
    u = np.arange(c)[None, :]
    levels = [c >> i for i in range(int(np.log2(c)) + 1)]
    cum = np.zeros((2, 2 * len(levels), c, c), np.float32)
    pair = np.zeros((2, len(levels), c, c), np.float32)
    for li, m in enumerate(levels):
        same = (t // m) == (u // m)
        cum[0, 2 * li] = same & (u <= t)
        cum[0, 2 * li + 1] = same & (u > t)
        cum[1, 2 * li] = same & (u >= t)
        cum[1, 2 * li + 1] = same & (u < t)
        if li > 0:
            pair[0, li] = ((t // m) % 2 == 1) & ((u // m) == (t // m) - 1)
            pair[1, li] = ((t // m) % 2 == 0) & ((u // m) == (t // m) + 1)
    pair[:, 0] = np.eye(c, dtype=np.float32)
    return cum.reshape(2, 2 * len(levels) * c, c), pair, len(levels)


def _gla_kernel(nlev, q_ref, k_ref, v_ref, vt_ref, la_ref, cum_ref, pm_ref, s0_ref, o_ref, sf_ref, st_ref):
    c = GLA_BLOCK
    step = pl.program_id(2)

    @pl.when(step == 0)
    def _():
        st_ref[...] = s0_ref[...]

    la = la_ref[...]
    l_hi, l_mid, l_lo = _split3(la)
    cum = cum_ref[...]
    ex = jnp.exp(_dot(cum, l_hi) + _dot(cum, l_mid) + _dot(cum, l_lo))
    q = q_ref[...]
    k = k_ref[...]
    gtot = jnp.exp(jnp.sum(la, axis=0, keepdims=True))
    qe = [(q * ex[(2 * li) * c:(2 * li + 1) * c]).astype(BF16) for li in range(nlev)]
    ke = [(k * ex[(2 * li + 1) * c:(2 * li + 2) * c]).astype(BF16) for li in range(nlev)]
    qb, kb = q.astype(BF16), k.astype(BF16)
    for hd in range(GLA_HEADS):
        ks = slice(hd * GLA_DK, (hd + 1) * GLA_DK)
        vs = slice(hd * GLA_DV, (hd + 1) * GLA_DV)
        a = pm_ref[0] * _dot_nt(qb[:, ks], kb[:, ks])
        for li in range(1, nlev):
            a = a + pm_ref[li] * _dot_nt(qe[li][:, ks], ke[li][:, ks])
        st = st_ref[hd]
        o = _dot_nt(qe[0][:, ks], st.astype(BF16)) + _dot(a.astype(BF16), v_ref[:, vs])
        o_ref[:, vs] = o
        st_ref[hd] = st * gtot[:, ks] + _dot(vt_ref[vs, :], ke[0][:, ks])
    sf_ref[...] = st_ref[...]


def _gla(q, k, v, vt, la, s0, cum, pm, nlev):
    b, n, _ = q.shape
    nc = n // GLA_BLOCK
    c = GLA_BLOCK
    pos = lambda d_, s_: s_ + d_ * (nc - 1 - 2 * s_)
    act = lambda wd: pl.BlockSpec((None, c, wd), lambda bi, d_, s_: (bi, pos(d_, s_), 0))
    st_spec = pl.BlockSpec((None, None, GLA_HEADS, GLA_DV, GLA_DK), lambda bi, d_, s_: (bi, d_, 0, 0, 0))
    o, sf = pl.pallas_call(
        functools.partial(_gla_kernel, nlev),
        out_shape=[jax.ShapeDtypeStruct((2, b, n, GLA_V), F32),
                   jax.ShapeDtypeStruct((b, 2, GLA_HEADS, GLA_DV, GLA_DK), F32)],
        grid=(b, 2, nc),
        in_specs=[act(256), act(256), act(512),
                  pl.BlockSpec((None, 512, c), lambda bi, d_, s_: (bi, 0, pos(d_, s_))),
                  pl.BlockSpec((None, c, 256), lambda bi, d_, s_: (bi, pos(d_, s_), d_)),
                  pl.BlockSpec((None,) + cum.shape[1:], lambda bi, d_, s_: (d_, 0, 0)),
                  pl.BlockSpec((None,) + pm.shape[1:], lambda bi, d_, s_: (d_, 0, 0, 0)),
                  st_spec],
        out_specs=[pl.BlockSpec((None, None, c, GLA_V), lambda bi, d_, s_: (d_, bi, pos(d_, s_), 0)), st_spec],
        scratch_shapes=[pltpu.VMEM((GLA_HEADS, GLA_DV, GLA_DK), F32)],
        compiler_params=_cparams(("parallel", "parallel", "arbitrary")),
    )(q, k, v, vt, la, cum, pm, s0)
    return o, sf


def _odd_out_kernel(x_ref, of_ref, ob_ref, r_ref, dl_ref, gg_ref, w_ref, mod_ref, o_ref):
    d = D_MODEL
    o = of_ref[...] + ob_ref[...]
    gg = gg_ref[...]
    r = r_ref[...]
    parts = []
    for hd in range(GLA_HEADS):
        vs = slice(hd * GLA_DV, (hd + 1) * GLA_DV)
        oh = o[:, vs]
        parts.append(oh * lax.rsqrt(jnp.mean(oh * oh, axis=-1, keepdims=True) + EPS) * gg[:, vs])
    cl = (jnp.concatenate(parts, axis=1) * (r * jax.nn.sigmoid(r))).astype(BF16)
    y = _dot(cl, w_ref[0:512, :]) + _dot(dl_ref[...], w_ref[512:1024, :])
    o_ref[...] = x_ref[...] + mod_ref[:, 2 * d:3 * d] * y


def _odd_out(x, ol, r, dl, gg, w, mod, tm):
    b, n, d = x.shape
    act = lambda wd: pl.BlockSpec((None, tm, wd), lambda bi, i: (bi, i, 0))
    dirspec = lambda dr: pl.BlockSpec((None, None, tm, GLA_V), lambda bi, i: (dr, bi, i, 0))
    return pl.pallas_call(
        _odd_out_kernel,
        out_shape=jax.ShapeDtypeStruct((b, n, d), F32),
        grid=(b, n // tm),
        in_specs=[act(d), dirspec(0), dirspec(1), act(512), act(512),
                  pl.BlockSpec(gg.shape, lambda bi, i: (0, 0)), pl.BlockSpec(w.shape, lambda bi, i: (0, 0)),
                  pl.BlockSpec((None, 1, 6 * d), lambda bi, i: (bi, 0, 0))],
        out_specs=act(d),
        compiler_params=_cparams(("parallel", "parallel")),
    )(x, ol, ol, r, dl, gg, w, mod)


def _router_kernel(x_ref, mod_ref, gn_ref, wr_ref, u_ref, h_ref, e_ref, wt_ref, r_ref, cnt_ref, carry_ref):
    d = D_MODEL
    tm = x_ref.shape[0]
    i = pl.program_id(0)

    @pl.when(i == 0)
    def _():
        carry_ref[...] = jnp.zeros_like(carry_ref)

    mod = mod_ref[...]
    h = _rms(x_ref[...], gn_ref[...]) * (1.0 + mod[:, 4 * d:5 * d]) + mod[:, 3 * d:4 * d]
    h_ref[...] = h
    h_hi, h_lo = _split2(h)
    w_hi, w_lo = _split2(wr_ref[...])
    lg = _dot_nt(w_hi, h_hi) + _dot_nt(w_lo, h_hi) + _dot_nt(w_hi, h_lo)
    rid = lax.broadcasted_iota(I32, (8, tm), 0)
    gl = jnp.where(rid < MOE_GROUPS, lg[0:8], NEG)
    gmax = jnp.max(gl, axis=0, keepdims=True)
    gsel = jnp.min(jnp.where(gl == gmax, rid, 8), axis=0, keepdims=True)
    pmax = 1.0 / jnp.sum(jnp.where(rid < MOE_GROUPS, jnp.exp(gl - gmax), 0.0), axis=0, keepdims=True)
    e_in = jnp.zeros((MOE_PER_GROUP, tm), F32)
    for g in range(MOE_GROUPS):
        e_in = e_in + jnp.where(gsel == g, lg[8 + 8 * g:16 + 8 * g], 0.0)
    v1 = jnp.max(e_in, axis=0, keepdims=True)
    i1 = jnp.min(jnp.where(e_in == v1, rid, 8), axis=0, keepdims=True)
    e_rest = jnp.where(rid == i1, -jnp.inf, e_in)
    v2 = jnp.max(e_rest, axis=0, keepdims=True)
    i2 = jnp.min(jnp.where(e_rest == v2, rid, 8), axis=0, keepdims=True)
    t = jnp.exp(v2 - v1)
    w1 = pmax / (1.0 + t)
    w2 = pmax * t / (1.0 + t)
    e1 = gsel * MOE_PER_GROUP + i1
    e2 = gsel * MOE_PER_GROUP + i2
    eid = lax.broadcasted_iota(I32, (MOE_EXPERTS, tm), 0)
    oh1 = jnp.where(eid == e1, 1.0, 0.0)
    oh2 = jnp.where(eid == e2, 1.0, 0.0)
    ohs = oh1 + oh2
    base = carry_ref[:, 0:1] + _dot(ohs.astype(BF16), u_ref[...])
    r1 = jnp.sum(oh1 * base, axis=0, keepdims=True)
    r2 = jnp.sum(oh2 * base, axis=0, keepdims=True)
    carry_ref[...] = carry_ref[...] + jnp.sum(ohs, axis=1, keepdims=True)
    cnt_ref[...] = carry_ref[...]
    e_ref[...] = jnp.concatenate([e1, e2], axis=0)
    r_ref[...] = jnp.concatenate([r1, r2], axis=0).astype(I32)
    w8 = jnp.concatenate([w1, w2, jnp.zeros((6, tm), F32)], axis=0)
    wt_ref[...] = w8.T


def _router(x2, mod, rows_per_mod, mod_row, gn, wr, u):
    n, d = x2.shape
    tm = MOE_TILE
    if mod_row is None:
        row = lambda i: (i // (rows_per_mod // tm), 0, 0)
    else:
        row = lambda i: (mod_row, 0, 0)
    return pl.pallas_call(
        _router_kernel,
        out_shape=[jax.ShapeDtypeStruct((n, d), F32), jax.ShapeDtypeStruct((2, n), I32),
                   jax.ShapeDtypeStruct((n, 8), F32), jax.ShapeDtypeStruct((2, n), I32),
                   jax.ShapeDtypeStruct((MOE_EXPERTS, LANES), F32)],
        grid=(n // tm,),
        in_specs=[pl.BlockSpec((tm, d), lambda i: (i, 0)), pl.BlockSpec((None, 1, 6 * d), row),
                  pl.BlockSpec(gn.shape, lambda i: (0, 0)), pl.BlockSpec(wr.shape, lambda i: (0, 0)),
                  pl.BlockSpec(u.shape, lambda i: (0, 0))],
        out_specs=[pl.BlockSpec((tm, d), lambda i: (i, 0)), pl.BlockSpec((2, tm), lambda i: (0, i)),
                   pl.BlockSpec((tm, 8), lambda i: (i, 0)), pl.BlockSpec((2, tm), lambda i: (0, i)),
                   pl.BlockSpec((MOE_EXPERTS, LANES), lambda i: (0, 0))],
        scratch_shapes=[pltpu.VMEM((MOE_EXPERTS, LANES), F32)],
        compiler_params=_cparams(("arbitrary",)),
    )(x2, mod, gn, wr, u)


def _dispatch_kernel(n, dest_ref, h_ref, xs_in_ref, xs_ref, sem):
    del xs_in_ref
    tm = h_ref.shape[0]
    base = pl.program_id(0) * tm

    def row_copy(t, j):
        dst = dest_ref[j * n + base + t]
        return pltpu.make_async_copy(h_ref.at[pl.ds(t, 1), :], xs_ref.at[pl.ds(dst, 1), :], sem)

    def issue(t, carry):
        row_copy(t, 0).start()
        row_copy(t, 1).start()
        return carry

    lax.fori_loop(0, tm, issue, 0)

    def drain(t, carry):
        row_copy(t, 0).wait()
        row_copy(t, 1).wait()
        return carry

    lax.fori_loop(0, tm, drain, 0)


def _dispatch(h, dest, rows):
    n, d = h.shape
    tm = MOE_TILE
    return pl.pallas_call(
        functools.partial(_dispatch_kernel, n),
        out_shape=jax.ShapeDtypeStruct((rows, d), F32),
        grid_spec=pltpu.PrefetchScalarGridSpec(
            num_scalar_prefetch=1, grid=(n // tm,),
            in_specs=[pl.BlockSpec((tm, d), lambda i, dst: (i, 0)), pl.BlockSpec(memory_space=pl.ANY)],
            out_specs=pl.BlockSpec(memory_space=pl.ANY),
            scratch_shapes=[pltpu.SemaphoreType.DMA]),
        input_output_aliases={2: 0},
        compiler_params=_cparams(("arbitrary",)),
    )(dest, h, jnp.zeros((rows, d), F32))


def _gmm_kernel(te_ref, nu_ref, xs_ref, wg_ref, wu_ref, wd_ref, ys_ref):
    del te_ref
    used = pl.program_id(0) < nu_ref[0]

    @pl.when(used)
    def _():
        x = xs_ref[...].astype(BF16)
        g = _dot(x, wg_ref[...].astype(BF16))
        u = _dot(x, wu_ref[...].astype(BF16))
        hm = (g * jax.nn.sigmoid(g) * u).astype(BF16)
        ys_ref[...] = _dot(hm, wd_ref[...].astype(BF16))

    @pl.when(jnp.logical_not(used))
    def _():
        ys_ref[...] = jnp.zeros_like(ys_ref)


def _gmm(xs, te, nu, wg, wu, wd):
    rows, d = xs.shape
    tm = MOE_TILE
    hid = wg.shape[-1]
    tile = lambda i, te_, nu_: (jnp.minimum(i, nu_[0] - 1), 0)
    return pl.pallas_call(
        _gmm_kernel,
        out_shape=jax.ShapeDtypeStruct((rows, d), F32),
        grid_spec=pltpu.PrefetchScalarGridSpec(
            num_scalar_prefetch=2, grid=(rows // tm,),
            in_specs=[pl.BlockSpec((tm, d), tile),
                      pl.BlockSpec((None, d, hid), lambda i, te_, nu_: (te_[i], 0, 0)),
                      pl.BlockSpec((None, d, hid), lambda i, te_, nu_: (te_[i], 0, 0)),
                      pl.BlockSpec((None, hid, d), lambda i, te_, nu_: (te_[i], 0, 0))],
            out_specs=pl.BlockSpec((tm, d), lambda i, te_, nu_: (i, 0))),
        compiler_params=_cparams(("arbitrary",)),
    )(te, nu, xs, wg, wu, wd)


def _combine_kernel(n, final, dest_ref, x_ref, wt_ref, mod_ref, fg_ref, ys_ref, o_ref, buf_ref, sem):
    d = D_MODEL
    tm = x_ref.shape[0]
    base = pl.program_id(0) * tm

    def row_copy(t, j):
        src = dest_ref[j * n + base + t]
        return pltpu.make_async_copy(ys_ref.at[pl.ds(src, 1), :], buf_ref.at[j, pl.ds(t, 1), :], sem)

    def issue(t, carry):
        row_copy(t, 0).start()
        row_copy(t, 1).start()
        return carry

    lax.fori_loop(0, tm, issue, 0)

    def drain(t, carry):
        row_copy(t, 0).wait()
        row_copy(t, 1).wait()
        return carry

    lax.fori_loop(0, tm, drain, 0)
    wt = wt_ref[...]
    y = wt[:, 0:1] * buf_ref[0] + wt[:, 1:2] * buf_ref[1]
    out = x_ref[...] + mod_ref[:, 5 * d:6 * d] * y
    if final:
        out = _rms(out, fg_ref[...])
    o_ref[...] = out


def _combine(x2, wt, mod, rows_per_mod, mod_row, fg, ys, dest, final):
    n, d = x2.shape
    tm = MOE_TILE
    if mod_row is None:
        row = lambda i, dst: (i // (rows_per_mod // tm), 0, 0)
    else:
        row = lambda i, dst: (mod_row, 0, 0)
    return pl.pallas_call(
        functools.partial(_combine_kernel, n, final),
        out_shape=jax.ShapeDtypeStruct((n, d), F32),
        grid_spec=pltpu.PrefetchScalarGridSpec(
            num_scalar_prefetch=1, grid=(n // tm,),
            in_specs=[pl.BlockSpec((tm, d), lambda i, dst: (i, 0)), pl.BlockSpec((tm, 8), lambda i, dst: (i, 0)),
                      pl.BlockSpec((None, 1, 6 * d), row), pl.BlockSpec(fg.shape, lambda i, dst: (0, 0)),
                      pl.BlockSpec(memory_space=pl.ANY)],
            out_specs=pl.BlockSpec((tm, d), lambda i, dst: (i, 0)),
            scratch_shapes=[pltpu.VMEM((2, tm, d), F32), pltpu.SemaphoreType.DMA]),
        compiler_params=_cparams(("arbitrary",)),
    )(dest, x2, wt, mod, fg, ys)


def _moe(x, mod, mod_row, gn, wr, u, wg, wu, wd, fg, final):
    b, n, d = x.shape
    nt = b * n
    tm = MOE_TILE
    x2 = x.reshape(nt, d)
    h, e, wt, r, cnt = _router(x2, mod, n, mod_row, gn, wr, u)
    rows = 2 * nt + MOE_EXPERTS * tm
    counts = cnt[:, 0].astype(I32)
    padded = ((counts + tm - 1) // tm) * tm
    ends = jnp.cumsum(padded)
    starts = ends - padded
    dest = (starts[e] + r).reshape(2 * nt)
    nu = (ends[-1:] // tm).astype(I32)
    tile_ids = jnp.minimum(jnp.arange(rows // tm, dtype=I32), nu[0] - 1)
    te = jnp.minimum(jnp.searchsorted(ends, tile_ids * tm, side="right"), MOE_EXPERTS - 1).astype(I32)
    xs = _dispatch(h, dest, rows)
    ys = _gmm(xs, te, nu, wg, wu, wd)
    out = _combine(x2, wt, mod, n, mod_row, fg, ys, dest, final)
    return out.reshape(b, n, d)


def _rope_tables(rows, dim):
    row = jnp.repeat(jnp.arange(rows, dtype=F32), GRID_W)
    col = jnp.tile(jnp.arange(GRID_W, dtype=F32), rows)
    half = dim // 2
    inv = jnp.power(ROPE_BASE, -jnp.arange(0, half, 2, dtype=F32) / half)
    ar = row[:, None] * inv[None, :]
    ac = col[:, None] * inv[None, :]
    ang = jnp.concatenate([ar, ar, ac, ac], axis=-1)
    return jnp.cos(ang), jnp.sin(ang)


def _even_tables(n, with_rope):
    if with_rope:
        cm, sm = _rope_tables(n // GRID_W, MLA_ROPE)
        cwin, swin = _rope_tables(n // GRID_W, WIN_HEAD_DIM)
    else:
        cm, sm = jnp.ones((n, MLA_ROPE), F32), jnp.zeros((n, MLA_ROPE), F32)
        cwin, swin = jnp.ones((n, WIN_HEAD_DIM), F32), jnp.zeros((n, WIN_HEAD_DIM), F32)
    one, zero = jnp.ones((n, MLA_NOPE), F32), jnp.zeros((n, MLA_NOPE), F32)
    pad = jnp.zeros((n, LANES - MLA_NOPE - MLA_ROPE), F32)
    return (jnp.concatenate([one, cm, pad], axis=1), jnp.concatenate([zero, sm, pad], axis=1),
            jnp.concatenate([cwin, cwin], axis=1), jnp.concatenate([swin, swin], axis=1))


def _even_weights(w_in, qg, w_uq, kvg, w_ukv):
    d = w_in.shape[0]
    o = np.cumsum([0, MLA_Q_RANK, MLA_KV_RANK, MLA_ROPE, 512, 128, 128])
    cq, ckv, kr, qw, kw, vw = [w_in[:, o[i]:o[i + 1]] for i in range(6)]
    z = lambda c: jnp.zeros((d, c), F32)
    kr128 = jnp.concatenate([z(MLA_NOPE), kr, z(LANES - MLA_NOPE - MLA_ROPE)], axis=1)
    dup = lambda t: jnp.concatenate([t[:, 0:64], t[:, 0:64], t[:, 64:128], t[:, 64:128]], axis=1)
    win = jnp.concatenate([cq, ckv, kr128, qw, dup(kw), dup(vw)], axis=1).astype(BF16)
    uq = w_uq.reshape(MLA_Q_RANK, MLA_HEADS, MLA_NOPE + MLA_ROPE)
    uq = jnp.pad(uq, ((0, 0), (0, 0), (0, LANES - MLA_NOPE - MLA_ROPE))).reshape(MLA_Q_RANK, MLA_HEADS * LANES)
    ukv = w_ukv.reshape(MLA_KV_RANK, MLA_HEADS, MLA_NOPE + MLA_V)
    ukk = jnp.pad(ukv[:, :, :MLA_NOPE], ((0, 0), (0, 0), (0, LANES - MLA_NOPE))).reshape(MLA_KV_RANK, MLA_HEADS * LANES)
    ukvv = ukv[:, :, MLA_NOPE:].reshape(MLA_KV_RANK, MLA_HEADS * MLA_V)
    return (win, qg.reshape(1, -1), uq.astype(BF16), kvg.reshape(1, -1), ukk.astype(BF16), ukvv.astype(BF16))


def _odd_weights(w_in, w_g2, b_g, ln_g, ln_b, w_s, b_s):
    d = w_in.shape[0]
    o = np.cumsum([0, GLA_K, GLA_K, GLA_V, 2 * GLA_GATE_RANK, GLA_V, SG_WIDTH, SG_WIDTH])
    q, k, v, g, r, u, vg = [w_in[:, o[i]:o[i + 1]] for i in range(7)]
    g128 = jnp.concatenate([g, jnp.zeros((d, LANES - 2 * GLA_GATE_RANK), F32)], axis=1)
    win = jnp.concatenate([q, k, v, g128, r, u, vg], axis=1).astype(BF16)
    wvt = v.T.astype(BF16)
    zr = jnp.zeros((GLA_GATE_RANK, GLA_K), F32)
    pad = jnp.zeros((LANES - 2 * GLA_GATE_RANK, GLA_K), F32)
    wg = jnp.concatenate([jnp.concatenate([w_g2[0], zr, pad], axis=0),
                          jnp.concatenate([zr, w_g2[1], pad], axis=0)], axis=1)
    bg = b_g.reshape(1, 2 * GLA_K)
    return (win, wvt, wg, bg, ln_g.reshape(1, -1), ln_b.reshape(1, -1), w_s.astype(BF16), b_s.T)


def kernel(x, c, ctx, c_ctx, ada_w, ada_b, norm_mix_g, norm_ffn_g, even_w_in, mla_q_norm_g, mla_w_uq, mla_kv_norm_g, mla_w_ukv, win_sink, even_w_out, odd_w_in, gla_w_g2, gla_b_g, gla_norm_g, sg_ln_g, sg_ln_b, sg_w_s, sg_b_s, odd_w_out, moe_w_rg, moe_w_re, moe_w_gate, moe_w_up, moe_w_down, final_norm_g):
    b, n, d = x.shape
    lc = ctx.shape[1]
    depth = ada_w.shape[0]
    assert depth == 2 and d == D_MODEL and b < 8
    assert n % 512 == 0 and lc % MOE_TILE == 0 and n % GRID_W == 0
    tm = 512 if n % 512 == 0 else 256
    tq = 256

    cond8 = jnp.concatenate([c, c_ctx[None, :], jnp.zeros((8 - b - 1, d), F32)], axis=0)
    mod_all = _adaln(cond8, ada_w, ada_b).reshape(depth, 8, 1, 6 * d)
    ctx_row = b
    u_tri = jnp.asarray(np.triu(np.ones((MOE_TILE, MOE_TILE), np.float32), 1), BF16)
    fg = final_norm_g.reshape(1, d)

    def router_w(layer):
        return jnp.concatenate([moe_w_rg[layer].T, jnp.zeros((8 - MOE_GROUPS, d), F32), moe_w_re[layer].T], axis=0)

    def moe(xx, layer, mod_row, final):
        return _moe(xx, mod_all[layer], mod_row, norm_ffn_g[layer].reshape(1, d), router_w(layer), u_tri,
                    moe_w_gate[layer], moe_w_up[layer], moe_w_down[layer], fg, final)

    mod = mod_all[0]
    gn = norm_mix_g[0].reshape(1, d)
    ew = _even_weights(even_w_in[0], mla_q_norm_g[0], mla_w_uq[0], mla_kv_norm_g[0], mla_w_ukv[0])
    qm_l, km_l, vm_l, qw_l, kw_l, vw_l = _even_in(x, mod, None, gn, ew, _even_tables(n, True), tm)
    qm_c, km_c, vm_c, qw_c, kw_c, vw_c = _even_in(ctx, mod, ctx_row, gn, ew, _even_tables(lc, False), lc)
    w_out = even_w_out[0].astype(BF16)
    sink = win_sink[0]
    oa_l = _mla_attn(qm_l, [(km_l, vm_l), (km_c, vm_c)], tq)
    ob_l = _gqa(qw_l, kw_l, vw_l, kw_c, vw_c, sink, True)
    xl = _even_out(x, oa_l, ob_l, w_out, mod, None, tm)
    oa_c = _mla_attn(qm_c, [(km_c, vm_c)], lc)
    ob_c = _gqa(qw_c, None, None, kw_c, vw_c, sink, False)
    xc = _even_out(ctx, oa_c, ob_c, w_out, mod, ctx_row, lc)
    xl = moe(xl, 0, None, False)
    xc = moe(xc, 0, ctx_row, False)

    mod = mod_all[1]
    gn = norm_mix_g[1].reshape(1, d)
    ow = _odd_weights(odd_w_in[0], gla_w_g2[0], gla_b_g[0], sg_ln_g[0], sg_ln_b[0], sg_w_s[0], sg_b_s[0])
    q_l, k_l, v_l, vt_l, la_l, r_l, dl_l = _odd_in(xl, mod, None, gn, ow, 256)
    q_c, k_c, v_c, vt_c, la_c, _, _ = _odd_in(xc, mod, ctx_row, gn, ow, lc)
    cum_np, pm_np, nlev = _gla_tables()
    cum = jnp.asarray(cum_np, BF16)
    pm = jnp.asarray(pm_np, F32)
    s0 = jnp.zeros((b, 2, GLA_HEADS, GLA_DV, GLA_DK), F32)
    _, s_ctx = _gla(q_c, k_c, v_c, vt_c, la_c, s0, cum, pm, nlev)
    ol, _ = _gla(q_l, k_l, v_l, vt_l, la_l, s_ctx, cum, pm, nlev)
    xl = _odd_out(xl, ol, r_l, dl_l, gla_norm_g[0].reshape(1, -1), odd_w_out[0].astype(BF16), mod, tm)
    return moe(xl, 1, None, True)
```

```python
import functools

import numpy as np
import jax
import jax.numpy as jnp
from jax import lax
from jax.experimental import pallas as pl
from jax.experimental.pallas import tpu as pltpu

F32 = jnp.float32
BF16 = jnp.bfloat16
I32 = jnp.int32

D_MODEL = 1024
GRID_W = 64
EPS = 1e-6
ROPE_BASE = 10000.0
MLA_HEADS = 8
MLA_Q_RANK = 256
MLA_KV_RANK = 128
MLA_NOPE = 64
MLA_ROPE = 32
MLA_V = 64
WIN_HEADS = 8
WIN_KV_HEADS = 2
WIN_HEAD_DIM = 64
WIN_BLOCK = 128
GLA_HEADS = 4
GLA_DK = 64
GLA_DV = 128
GLA_GATE_RANK = 16
GLA_TAU = 16.0
GLA_K = GLA_HEADS * GLA_DK
GLA_V = GLA_HEADS * GLA_DV
SG_GROUPS = 4
SG_CHUNK = 128
SG_WIDTH = 512
MOE_GROUPS = 4
MOE_PER_GROUP = 8
MOE_EXPERTS = 32
MOE_HIDDEN = 512

LANES = 128
GLA_BLOCK = 128
MOE_TILE = 256
DMA_UNROLL = 8
NEG = -1e30
VMEM_LIMIT = 56 * 1024 * 1024


def _cparams(sem):
    return pltpu.CompilerParams(dimension_semantics=sem, vmem_limit_bytes=VMEM_LIMIT)


def _dot(a, b):
    return jnp.dot(a, b, preferred_element_type=F32)


def _dot_nt(a, b):
    return lax.dot_general(a, b, (((1,), (1,)), ((), ())), preferred_element_type=F32)


def _split2(a):
    hi = a.astype(BF16)
    lo = (a - hi.astype(F32)).astype(BF16)
    return hi, lo


def _split3(a):
    hi = a.astype(BF16)
    r = a - hi.astype(F32)
    mid = r.astype(BF16)
    lo = (r - mid.astype(F32)).astype(BF16)
    return hi, mid, lo


def _rms(x, g):
    ms = jnp.mean(x * x, axis=-1, keepdims=True)
    return x * lax.rsqrt(ms + EPS) * g


def _lane_tile(t, reps):
    return t if reps == 1 else jnp.concatenate([t] * reps, axis=1)


def _rope(t, cos, sin, quarter):
    n = t.shape[1]
    lane = lax.broadcasted_iota(I32, t.shape, 1)
    first = (lane & (2 * quarter - 1)) < quarter
    rot = jnp.where(first, -pltpu.roll(t, n - quarter, 1), pltpu.roll(t, quarter, 1))
    return t * cos + rot * sin


def _adaln_kernel(c_ref, w_ref, b_ref, o_ref):
    c = c_ref[...]
    s_hi, s_lo = _split2(c * jax.nn.sigmoid(c))
    w_hi, w_lo = _split2(w_ref[...])
    o_ref[...] = _dot(s_hi, w_hi) + _dot(s_lo, w_hi) + _dot(s_hi, w_lo) + b_ref[...]


def _adaln(cond8, ada_w, ada_b):
    depth, d, n6 = ada_w.shape
    tn = 1536
    return pl.pallas_call(
        _adaln_kernel,
        out_shape=jax.ShapeDtypeStruct((depth, 8, n6), F32),
        grid=(depth, n6 // tn),
        in_specs=[
            pl.BlockSpec((8, d), lambda l, j: (0, 0)),
            pl.BlockSpec((None, d, tn), lambda l, j: (l, 0, j)),
            pl.BlockSpec((None, 1, tn), lambda l, j: (l, 0, j)),
        ],
        out_specs=pl.BlockSpec((None, 8, tn), lambda l, j: (l, 0, j)),
        compiler_params=_cparams(("parallel", "parallel")),
    )(cond8, ada_w, ada_b.reshape(depth, 1, n6))


def _even_in_kernel(x_ref, mod_ref, gn_ref, win_ref, qg_ref, wuq_ref, kvg_ref, wukk_ref, wukv_ref,
                    cq_ref, sq_ref, cw_ref, sw_ref,
                    qm_ref, km_ref, vm_ref, qw_ref, kw_ref, vw_ref):
    d = D_MODEL
    mod = mod_ref[...]
    h = _rms(x_ref[...], gn_ref[...]) * (1.0 + mod[:, d:2 * d]) + mod[:, 0:d]
    z = _dot(h.astype(BF16), win_ref[...])
    cq, sq, cw, sw = cq_ref[...], sq_ref[...], cw_ref[...], sw_ref[...]
    cqn = _rms(z[:, 0:256], qg_ref[...]).astype(BF16)
    q = _dot(cqn, wuq_ref[...])
    q = _rope(q, _lane_tile(cq, 8), _lane_tile(sq, 8), MLA_ROPE // 4)
    qm_ref[...] = (q * ((MLA_NOPE + MLA_ROPE) ** -0.5)).astype(BF16)
    ckvn = _rms(z[:, 256:384], kvg_ref[...]).astype(BF16)
    kn = _dot(ckvn, wukk_ref[...])
    kr = _rope(z[:, 384:512], cq, sq, MLA_ROPE // 4)
    km_ref[...] = (kn + _lane_tile(kr, 8)).astype(BF16)
    vm_ref[...] = _dot(ckvn, wukv_ref[...]).astype(BF16)
    qw = _rope(z[:, 512:1024], _lane_tile(cw, 4), _lane_tile(sw, 4), WIN_HEAD_DIM // 4)
    qw_ref[...] = (qw * (WIN_HEAD_DIM ** -0.5)).astype(BF16)
    kw = _rope(z[:, 1024:1280], _lane_tile(cw, 2), _lane_tile(sw, 2), WIN_HEAD_DIM // 4)
    kw_ref[...] = kw.astype(BF16)
    vw_ref[...] = z[:, 1280:1536].astype(BF16)


def _even_in(x, mod, mod_row, gn, wts, tabs, tm):
    b, n, d = x.shape
    win, qg, wuq, kvg, wukk, wukv = wts
    nt = n // tm
    row = (lambda bi, i: (bi, 0, 0)) if mod_row is None else (lambda bi, i: (mod_row, 0, 0))
    full = lambda a: pl.BlockSpec(a.shape, lambda bi, i: (0,) * a.ndim)
    tab = pl.BlockSpec((tm, LANES), lambda bi, i: (i, 0))
    outw = (1024, 1024, 512, 512, 256, 256)
    return pl.pallas_call(
        _even_in_kernel,
        out_shape=[jax.ShapeDtypeStruct((b, n, w), BF16) for w in outw],
        grid=(b, nt),
        in_specs=[pl.BlockSpec((None, tm, d), lambda bi, i: (bi, i, 0)),
                  pl.BlockSpec((None, 1, 6 * d), row),
                  full(gn), full(win), full(qg), full(wuq), full(kvg), full(wukk), full(wukv),
                  tab, tab, tab, tab],
        out_specs=[pl.BlockSpec((None, tm, w), lambda bi, i: (bi, i, 0)) for w in outw],
        compiler_params=_cparams(("parallel", "parallel")),
    )(x, mod, gn, win, qg, wuq, kvg, wukk, wukv, *tabs)


def _mla_attn_kernel(nseg, q_ref, *refs):
    ks, vs, o_ref = refs[0:2 * nseg:2], refs[1:2 * nseg:2], refs[2 * nseg]
    tq = q_ref.shape[0]
    lane = lax.broadcasted_iota(I32, (tq, LANES), 1)
    for hp in range(MLA_HEADS // 2):
        pair = slice(hp * LANES, (hp + 1) * LANES)
        outs = []
        for e in range(2):
            hs = slice((2 * hp + e) * LANES, (2 * hp + e + 1) * LANES)
            qh = q_ref[:, hs]
            ss = [_dot_nt(qh, k[:, hs]) for k in ks]
            m = functools.reduce(jnp.maximum, [jnp.max(s, axis=-1, keepdims=True) for s in ss])
            ps = [jnp.exp(s - m) for s in ss]
            l = functools.reduce(jnp.add, [jnp.sum(p, axis=-1, keepdims=True) for p in ps])
            o = functools.reduce(jnp.add, [_dot(p.astype(BF16), v[:, pair]) for p, v in zip(ps, vs)])
            outs.append(o * (1.0 / l))
        o_ref[:, pair] = jnp.where(lane < MLA_V, outs[0], outs[1]).astype(BF16)


def _mla_attn(q, segs, tq):
    b, n, _ = q.shape
    in_specs = [pl.BlockSpec((None, tq, 1024), lambda bi, i: (bi, i, 0))]
    args = [q]
    for k, v in segs:
        lk = k.shape[1]
        in_specs += [pl.BlockSpec((None, lk, 1024), lambda bi, i: (bi, 0, 0)),
                     pl.BlockSpec((None, lk, 512), lambda bi, i: (bi, 0, 0))]
        args += [k, v]
    return pl.pallas_call(
        functools.partial(_mla_attn_kernel, len(segs)),
        out_shape=jax.ShapeDtypeStruct((b, n, 512), BF16),
        grid=(b, n // tq),
        in_specs=in_specs,
        out_specs=pl.BlockSpec((None, tq, 512), lambda bi, i: (bi, i, 0)),
        compiler_params=_cparams(("parallel", "parallel")),
    )(*args)


def _gqa_kernel(has_win, nb, sink_ref, q_ref, *refs):
    if has_win:
        kp, kc, kn, vp, vc, vn, kx, vx, o_ref = refs
    else:
        kx, vx, o_ref = refs
    tq = q_ref.shape[0]
    i = pl.program_id(1)
    lane = lax.broadcasted_iota(I32, (tq, LANES), 1)
    row2 = lax.broadcasted_iota(I32, (2 * tq, 1), 0)
    half = WIN_HEAD_DIM
    for j in range(WIN_HEADS // 2):
        g = j // 2
        gs = slice(g * LANES, (g + 1) * LANES)
        qp = q_ref[:, j * LANES:(j + 1) * LANES]
        zero = jnp.zeros_like(qp)
        q2 = jnp.concatenate([jnp.where(lane < half, qp, zero), jnp.where(lane >= half, qp, zero)], axis=0)
        if has_win:
            kcat = jnp.concatenate([kp[:, gs], kc[:, gs], kn[:, gs], kx[:, gs]], axis=0)
            vcat = jnp.concatenate([vp[:, gs], vc[:, gs], vn[:, gs], vx[:, gs]], axis=0)
        else:
            kcat, vcat = kx[:, gs], vx[:, gs]
        s = _dot_nt(q2, kcat)
        if has_win:
            w = WIN_BLOCK
            r = lax.broadcasted_iota(I32, s.shape, 0) & (tq - 1)
            c = lax.broadcasted_iota(I32, s.shape, 1)
            big = jnp.int32(1 << 20)
            no_prev = jnp.where(i > 0, 0, big)
            no_next = jnp.where(i < nb - 1, 0, big)
            ok_prev = c >= r + no_prev
            ok_next = (c - 2 * w) <= r - no_next
            valid = ((c >= w) | ok_prev) & ((c < 2 * w) | (c >= 3 * w) | ok_next)
            s = jnp.where(valid, s, NEG)
        sk = jnp.where(row2 < tq, sink_ref[2 * j], sink_ref[2 * j + 1])
        m = jnp.maximum(jnp.max(s, axis=-1, keepdims=True), sk)
        p = jnp.exp(s - m)
        l = jnp.sum(p, axis=-1, keepdims=True) + jnp.exp(sk - m)
        o2 = _dot(p.astype(BF16), vcat) * (1.0 / l)
        o_ref[:, j * LANES:(j + 1) * LANES] = jnp.where(lane < half, o2[:tq], o2[tq:]).astype(BF16)


def _gqa(q, k, v, kx, vx, sink, has_win):
    b, n, _ = q.shape
    lc = kx.shape[1]
    smem = pl.BlockSpec(memory_space=pltpu.SMEM)
    ctxs = pl.BlockSpec((None, lc, 256), lambda bi, i: (bi, 0, 0))
    if has_win:
        tq = WIN_BLOCK
        nb = n // tq
        blk = lambda f: pl.BlockSpec((None, tq, 256), f)
        prev = lambda bi, i: (bi, jnp.maximum(i - 1, 0), 0)
        cur = lambda bi, i: (bi, i, 0)
        nxt = lambda bi, i: (bi, jnp.minimum(i + 1, nb - 1), 0)
        in_specs = [smem, pl.BlockSpec((None, tq, 512), cur),
                    blk(prev), blk(cur), blk(nxt), blk(prev), blk(cur), blk(nxt), ctxs, ctxs]
        args = (sink, q, k, k, k, v, v, v, kx, vx)
    else:
        tq, nb = n, 1
        in_specs = [smem, pl.BlockSpec((None, tq, 512), lambda bi, i: (bi, i, 0)), ctxs, ctxs]
        args = (sink, q, kx, vx)
    return pl.pallas_call(
        functools.partial(_gqa_kernel, has_win, nb),
        out_shape=jax.ShapeDtypeStruct((b, n, 512), BF16),
        grid=(b, nb),
        in_specs=in_specs,
        out_specs=pl.BlockSpec((None, tq, 512), lambda bi, i: (bi, i, 0)),
        compiler_params=_cparams(("parallel", "parallel")),
    )(*args)


def _even_out_kernel(x_ref, a_ref, b_ref, w_ref, mod_ref, o_ref):
    d = D_MODEL
    y = _dot(a_ref[...], w_ref[0:512, :]) + _dot(b_ref[...], w_ref[512:1024, :])
    o_ref[...] = x_ref[...] + mod_ref[:, 2 * d:3 * d] * y


def _even_out(x, oa, ob, w, mod, mod_row, tm):
    b, n, d = x.shape
    row = (lambda bi, i: (bi, 0, 0)) if mod_row is None else (lambda bi, i: (mod_row, 0, 0))
    act = lambda wd: pl.BlockSpec((None, tm, wd), lambda bi, i: (bi, i, 0))
    return pl.pallas_call(
        _even_out_kernel,
        out_shape=jax.ShapeDtypeStruct((b, n, d), F32),
        grid=(b, n // tm),
        in_specs=[act(d), act(512), act(512), pl.BlockSpec(w.shape, lambda bi, i: (0, 0)),
                  pl.BlockSpec((None, 1, 6 * d), row)],
        out_specs=act(d),
        compiler_params=_cparams(("parallel", "parallel")),
    )(x, oa, ob, w, mod)


def _log_sigmoid(z):
    return jnp.minimum(z, 0.0) - jnp.log(1.0 + jnp.exp(-jnp.abs(z)))


def _odd_in_kernel(x_ref, mod_ref, gn_ref, win_ref, wvt_ref, wg_ref, bg_ref, lng_ref, lnb_ref, ws_ref, bst_ref,
                   q_ref, k_ref, v_ref, vt_ref, la_ref, r_ref, dl_ref):
    d = D_MODEL
    tm = x_ref.shape[0]
    mod = mod_ref[...]
    h = (_rms(x_ref[...], gn_ref[...]) * (1.0 + mod[:, d:2 * d]) + mod[:, 0:d]).astype(BF16)
    z = _dot(h, win_ref[...])
    q_ref[...] = z[:, 0:256] * (GLA_DK ** -0.5)
    k_ref[...] = z[:, 256:512]
    v_ref[...] = z[:, 512:1024].astype(BF16)
    vt_ref[...] = _dot_nt(wvt_ref[...], h).astype(BF16)
    g_hi, g_lo = _split2(z[:, 1024:1152])
    w_hi, w_lo = _split2(wg_ref[...])
    zg = _dot(g_hi, w_hi) + _dot(g_lo, w_hi) + _dot(g_hi, w_lo) + bg_ref[...]
    la_ref[...] = _log_sigmoid(zg) / GLA_TAU
    r_ref[...] = z[:, 1152:1664]
    u = jax.nn.gelu(z[:, 1664:2176])
    vg = jax.nn.gelu(z[:, 2176:2688])
    mu = jnp.mean(vg, axis=-1, keepdims=True)
    vc = vg - mu
    var = jnp.mean(vc * vc, axis=-1, keepdims=True)
    vn = (vc * lax.rsqrt(var + EPS) * lng_ref[...] + lnb_ref[...]).astype(BF16)
    bst = bst_ref[...]
    for c in range(tm // SG_CHUNK):
        rows = slice(c * SG_CHUNK, (c + 1) * SG_CHUNK)
        parts = []
        for g in range(SG_GROUPS):
            cols = slice(g * LANES, (g + 1) * LANES)
            parts.append(_dot(ws_ref[g], vn[rows, cols]) + bst[:, g:g + 1])
        dl_ref[rows, :] = (u[rows, :] * jnp.concatenate(parts, axis=1)).astype(BF16)


def _odd_in(x, mod, mod_row, gn, wts, tm):
    b, n, d = x.shape
    win, wvt, wg, bg, lng, lnb, ws, bst = wts
    row = (lambda bi, i: (bi, 0, 0)) if mod_row is None else (lambda bi, i: (mod_row, 0, 0))
    full = lambda a: pl.BlockSpec(a.shape, lambda bi, i: (0,) * a.ndim)
    act = lambda wd: pl.BlockSpec((None, tm, wd), lambda bi, i: (bi, i, 0))
    outs = [((b, n, 256), F32, act(256)), ((b, n, 256), F32, act(256)), ((b, n, 512), BF16, act(512)),
            ((b, 512, n), BF16, pl.BlockSpec((None, 512, tm), lambda bi, i: (bi, 0, i))),
            ((b, n, 512), F32, act(512)), ((b, n, 512), F32, act(512)), ((b, n, 512), BF16, act(512))]
    return pl.pallas_call(
        _odd_in_kernel,
        out_shape=[jax.ShapeDtypeStruct(s, t) for s, t, _ in outs],
        grid=(b, n // tm),
        in_specs=[act(d), pl.BlockSpec((None, 1, 6 * d), row), full(gn), full(win), full(wvt), full(wg),
                  full(bg), full(lng), full(lnb), full(ws), full(bst)],
        out_specs=[sp for _, _, sp in outs],
        compiler_params=_cparams(("parallel", "parallel")),
    )(x, mod, gn, win, wvt, wg, bg, lng, lnb, ws, bst)


def _gla_tables():
    c = GLA_BLOCK
    t = np.arange(c)[:, None]
    u = np.arange(c)[None, :]
    levels = [c >> i for i in range(int(np.log2(c)) + 1)]
    cum = np.zeros((2, 2 * len(levels), c, c), np.float32)
    pair = np.zeros((2, len(levels), c, c), np.float32)
    for li, m in enumerate(levels):
        same = (t // m) == (u // m)
        cum[0, 2 * li] = same & (u <= t)
        cum[0, 2 * li + 1] = same & (u > t)
        cum[1, 2 * li] = same & (u >= t)
        cum[1, 2 * li + 1] = same & (u < t)
        if li > 0:
            pair[0, li] = ((t // m) % 2 == 1) & ((u // m) == (t // m) - 1)
            pair[1, li] = ((t // m) % 2 == 0) & ((u // m) == (t // m) + 1)
    pair[:, 0] = np.eye(c, dtype=np.float32)
    return cum.reshape(2, 2 * len(levels) * c, c), pair, len(levels)


def _gla_kernel(nlev, q_ref, k_ref, v_ref, vt_ref, la_ref, cum_ref, pm_ref, s0_ref, o_ref, sf_ref, st_ref):
    c = GLA_BLOCK
    step = pl.program_id(2)

    @pl.when(step == 0)
    def _():
        st_ref[...] = s0_ref[...]

    la = la_ref[...]
    l_hi, l_mid, l_lo = _split3(la)
    cum = cum_ref[...]
    ex = jnp.exp(_dot(cum, l_hi) + _dot(cum, l_mid) + _dot(cum, l_lo))
    q = q_ref[...]
    k = k_ref[...]
    gtot = jnp.exp(jnp.sum(la, axis=0, keepdims=True))
    qe = [(q * ex[(2 * li) * c:(2 * li + 1) * c]).astype(BF16) for li in range(nlev)]
    ke = [(k * ex[(2 * li + 1) * c:(2 * li + 2) * c]).astype(BF16) for li in range(nlev)]
    qb, kb = q.astype(BF16), k.astype(BF16)
    for hd in range(GLA_HEADS):
        ks = slice(hd * GLA_DK, (hd + 1) * GLA_DK)
        vs = slice(hd * GLA_DV, (hd + 1) * GLA_DV)
        a = pm_ref[0] * _dot_nt(qb[:, ks], kb[:, ks])
        for li in range(1, nlev):
            a = a + pm_ref[li] * _dot_nt(qe[li][:, ks], ke[li][:, ks])
        st = st_ref[hd]
        o = _dot_nt(qe[0][:, ks], st.astype(BF16)) + _dot(a.astype(BF16), v_ref[:, vs])
        o_ref[:, vs] = o
        st_ref[hd] = st * gtot[:, ks] + _dot(vt_ref[vs, :], ke[0][:, ks])
    sf_ref[...] = st_ref[...]


def _gla(q, k, v, vt, la, s0, cum, pm, nlev):
    b, n, _ = q.shape
    nc = n // GLA_BLOCK
    c = GLA_BLOCK
    pos = lambda d_, s_: s_ + d_ * (nc - 1 - 2 * s_)
    act = lambda wd: pl.BlockSpec((None, c, wd), lambda bi, d_, s_: (bi, pos(d_, s_), 0))
    st_spec = pl.BlockSpec((None, None, GLA_HEADS, GLA_DV, GLA_DK), lambda bi, d_, s_: (bi, d_, 0, 0, 0))
    o, sf = pl.pallas_call(
        functools.partial(_gla_kernel, nlev),
        out_shape=[jax.ShapeDtypeStruct((2, b, n, GLA_V), F32),
                   jax.ShapeDtypeStruct((b, 2, GLA_HEADS, GLA_DV, GLA_DK), F32)],
        grid=(b, 2, nc),
        in_specs=[act(256), act(256), act(512),
                  pl.BlockSpec((None, 512, c), lambda bi, d_, s_: (bi, 0, pos(d_, s_))),
                  pl.BlockSpec((None, c, 256), lambda bi, d_, s_: (bi, pos(d_, s_), d_)),
                  pl.BlockSpec((None,) + cum.shape[1:], lambda bi, d_, s_: (d_, 0, 0)),
                  pl.BlockSpec((None,) + pm.shape[1:], lambda bi, d_, s_: (d_, 0, 0, 0)),
                  st_spec],
        out_specs=[pl.BlockSpec((None, None, c, GLA_V), lambda bi, d_, s_: (d_, bi, pos(d_, s_), 0)), st_spec],
        scratch_shapes=[pltpu.VMEM((GLA_HEADS, GLA_DV, GLA_DK), F32)],
        compiler_params=_cparams(("parallel", "parallel", "arbitrary")),
    )(q, k, v, vt, la, cum, pm, s0)
    return o, sf


def _odd_out_kernel(x_ref, of_ref, ob_ref, r_ref, dl_ref, gg_ref, w_ref, mod_ref, o_ref):
    d = D_MODEL
    o = of_ref[...] + ob_ref[...]
    gg = gg_ref[...]
    r = r_ref[...]
    parts = []
    for hd in range(GLA_HEADS):
        vs = slice(hd * GLA_DV, (hd + 1) * GLA_DV)
        oh = o[:, vs]
        parts.append(oh * lax.rsqrt(jnp.mean(oh * oh, axis=-1, keepdims=True) + EPS) * gg[:, vs])
    cl = (jnp.concatenate(parts, axis=1) * (r * jax.nn.sigmoid(r))).astype(BF16)
    y = _dot(cl, w_ref[0:512, :]) + _dot(dl_ref[...], w_ref[512:1024, :])
    o_ref[...] = x_ref[...] + mod_ref[:, 2 * d:3 * d] * y


def _odd_out(x, ol, r, dl, gg, w, mod, tm):
    b, n, d = x.shape
    act = lambda wd: pl.BlockSpec((None, tm, wd), lambda bi, i: (bi, i, 0))
    dirspec = lambda dr: pl.BlockSpec((None, None, tm, GLA_V), lambda bi, i: (dr, bi, i, 0))
    return pl.pallas_call(
        _odd_out_kernel,
        out_shape=jax.ShapeDtypeStruct((b, n, d), F32),
        grid=(b, n // tm),
        in_specs=[act(d), dirspec(0), dirspec(1), act(512), act(512),
                  pl.BlockSpec(gg.shape, lambda bi, i: (0, 0)), pl.BlockSpec(w.shape, lambda bi, i: (0, 0)),
                  pl.BlockSpec((None, 1, 6 * d), lambda bi, i: (bi, 0, 0))],
        out_specs=act(d),
        compiler_params=_cparams(("parallel", "parallel")),
    )(x, ol, ol, r, dl, gg, w, mod)


def _router_kernel(x_ref, mod_ref, gn_ref, wr_ref, u_ref, h_ref, e_ref, wt_ref, r_ref, cnt_ref, carry_ref):
    d = D_MODEL
    tm = x_ref.shape[0]
    i = pl.program_id(0)

    @pl.when(i == 0)
    def _():
        carry_ref[...] = jnp.zeros_like(carry_ref)

    mod = mod_ref[...]
    h = _rms(x_ref[...], gn_ref[...]) * (1.0 + mod[:, 4 * d:5 * d]) + mod[:, 3 * d:4 * d]
    h_ref[...] = h
    h_hi, h_lo = _split2(h)
    w_hi, w_lo = _split2(wr_ref[...])
    lg = _dot_nt(w_hi, h_hi) + _dot_nt(w_lo, h_hi) + _dot_nt(w_hi, h_lo)
    rid = lax.broadcasted_iota(I32, (8, tm), 0)
    gl = jnp.where(rid < MOE_GROUPS, lg[0:8], NEG)
    gmax = jnp.max(gl, axis=0, keepdims=True)
    gsel = jnp.min(jnp.where(gl == gmax, rid, 8), axis=0, keepdims=True)
    pmax = 1.0 / jnp.sum(jnp.where(rid < MOE_GROUPS, jnp.exp(gl - gmax), 0.0), axis=0, keepdims=True)
    e_in = jnp.zeros((MOE_PER_GROUP, tm), F32)
    for g in range(MOE_GROUPS):
        e_in = e_in + jnp.where(gsel == g, lg[8 + 8 * g:16 + 8 * g], 0.0)
    v1 = jnp.max(e_in, axis=0, keepdims=True)
    i1 = jnp.min(jnp.where(e_in == v1, rid, 8), axis=0, keepdims=True)
    e_rest = jnp.where(rid == i1, -jnp.inf, e_in)
    v2 = jnp.max(e_rest, axis=0, keepdims=True)
    i2 = jnp.min(jnp.where(e_rest == v2, rid, 8), axis=0, keepdims=True)
    t = jnp.exp(v2 - v1)
    w1 = pmax / (1.0 + t)
    w2 = pmax * t / (1.0 + t)
    e1 = gsel * MOE_PER_GROUP + i1
    e2 = gsel * MOE_PER_GROUP + i2
    eid = lax.broadcasted_iota(I32, (MOE_EXPERTS, tm), 0)
    oh1 = jnp.where(eid == e1, 1.0, 0.0)
    oh2 = jnp.where(eid == e2, 1.0, 0.0)
    ohs = oh1 + oh2
    base = carry_ref[:, 0:1] + _dot(ohs.astype(BF16), u_ref[...])
    r1 = jnp.sum(oh1 * base, axis=0, keepdims=True)
    r2 = jnp.sum(oh2 * base, axis=0, keepdims=True)
    carry_ref[...] = carry_ref[...] + jnp.sum(ohs, axis=1, keepdims=True)
    cnt_ref[...] = carry_ref[...]
    e_ref[...] = jnp.concatenate([e1, e2], axis=0)
    r_ref[...] = jnp.concatenate([r1, r2], axis=0).astype(I32)
    w8 = jnp.concatenate([w1, w2, jnp.zeros((6, tm), F32)], axis=0)
    wt_ref[...] = w8.T


def _router(x2, mod, rows_per_mod, mod_row, gn, wr, u):
    n, d = x2.shape
    tm = MOE_TILE
    if mod_row is None:
        row = lambda i: (i // (rows_per_mod // tm), 0, 0)
    else:
        row = lambda i: (mod_row, 0, 0)
    return pl.pallas_call(
        _router_kernel,
        out_shape=[jax.ShapeDtypeStruct((n, d), F32), jax.ShapeDtypeStruct((2, n), I32),
                   jax.ShapeDtypeStruct((n, 8), F32), jax.ShapeDtypeStruct((2, n), I32),
                   jax.ShapeDtypeStruct((MOE_EXPERTS, LANES), F32)],
        grid=(n // tm,),
        in_specs=[pl.BlockSpec((tm, d), lambda i: (i, 0)), pl.BlockSpec((None, 1, 6 * d), row),
                  pl.BlockSpec(gn.shape, lambda i: (0, 0)), pl.BlockSpec(wr.shape, lambda i: (0, 0)),
                  pl.BlockSpec(u.shape, lambda i: (0, 0))],
        out_specs=[pl.BlockSpec((tm, d), lambda i: (i, 0)), pl.BlockSpec((2, tm), lambda i: (0, i)),
                   pl.BlockSpec((tm, 8), lambda i: (i, 0)), pl.BlockSpec((2, tm), lambda i: (0, i)),
                   pl.BlockSpec((MOE_EXPERTS, LANES), lambda i: (0, 0))],
        scratch_shapes=[pltpu.VMEM((MOE_EXPERTS, LANES), F32)],
        compiler_params=_cparams(("arbitrary",)),
    )(x2, mod, gn, wr, u)


def _row_dma_burst(tm, make_copy):
    def issue(g, carry):
        for u in range(DMA_UNROLL):
            t = g * DMA_UNROLL + u
            make_copy(t, 0).start(priority=0)
            make_copy(t, 1).start(priority=1)
        return carry

    lax.fori_loop(0, tm // DMA_UNROLL, issue, 0)


def _dispatch_kernel(n, dest_ref, h_ref, xs_ref, sem):
    tm = h_ref.shape[0]
    base = pl.program_id(0) * tm

    def row_copy(t, j):
        dst = dest_ref[j * n + base + t]
        return pltpu.make_async_copy(h_ref.at[pl.ds(t, 1), :], xs_ref.at[pl.ds(dst, 1), :], sem)

    _row_dma_burst(tm, row_copy)
    for _ in range(2):
        pltpu.make_async_copy(h_ref, xs_ref.at[pl.ds(0, tm), :], sem).wait()


def _dispatch(h, dest):
    n, d = h.shape
    tm = MOE_TILE
    return pl.pallas_call(
        functools.partial(_dispatch_kernel, n),
        out_shape=jax.ShapeDtypeStruct((2 * n, d), F32),
        grid_spec=pltpu.PrefetchScalarGridSpec(
            num_scalar_prefetch=1, grid=(n // tm,),
            in_specs=[pl.BlockSpec((tm, d), lambda i, dst: (i, 0))],
            out_specs=pl.BlockSpec(memory_space=pl.ANY),
            scratch_shapes=[pltpu.SemaphoreType.DMA]),
        compiler_params=_cparams(("arbitrary",)),
    )(dest, h)


def _gmm_kernel(vt_ref, ve_ref, vlo_ref, vhi_ref, vfirst_ref, nv_ref, xs_ref, wg_ref, wu_ref, wd_ref, ys_ref,
                wgb_ref, wub_ref, wdb_ref):
    del vt_ref
    v = pl.program_id(0)

    @pl.when(v < nv_ref[0])
    def _():
        @pl.when((v == 0) | (ve_ref[v] != ve_ref[jnp.maximum(v - 1, 0)]))
        def _():
            wgb_ref[...] = wg_ref[...].astype(BF16)
            wub_ref[...] = wu_ref[...].astype(BF16)
            wdb_ref[...] = wd_ref[...].astype(BF16)

        x = xs_ref[...].astype(BF16)
        g = _dot(x, wgb_ref[...])
        u = _dot(x, wub_ref[...])
        y = _dot((g * jax.nn.sigmoid(g) * u).astype(BF16), wdb_ref[...])
        row = lax.broadcasted_iota(I32, (y.shape[0], 1), 0)
        mine = (row >= vlo_ref[v]) & (row < vhi_ref[v])

        @pl.when(vfirst_ref[v] == 1)
        def _():
            ys_ref[...] = jnp.where(mine, y, 0.0)

        @pl.when(vfirst_ref[v] == 0)
        def _():
            ys_ref[...] = jnp.where(mine, y, ys_ref[...])


def _gmm(xs, visits, layer, wg, wu, wd):
    rows, d = xs.shape
    tm = MOE_TILE
    hid = wg.shape[-1]
    nvis = rows // tm + MOE_EXPERTS - 1
    tile = lambda v, vt, *_: (vt[v], 0)
    wspec = lambda r, c: pl.BlockSpec((None, None, r, c), lambda v, vt, ve, *_: (layer, ve[v], 0, 0))
    return pl.pallas_call(
        _gmm_kernel,
        out_shape=jax.ShapeDtypeStruct((rows, d), F32),
        grid_spec=pltpu.PrefetchScalarGridSpec(
            num_scalar_prefetch=6, grid=(nvis,),
            in_specs=[pl.BlockSpec((tm, d), tile), wspec(d, hid), wspec(d, hid), wspec(hid, d)],
            out_specs=pl.BlockSpec((tm, d), tile),
            scratch_shapes=[pltpu.VMEM((d, hid), BF16), pltpu.VMEM((d, hid), BF16), pltpu.VMEM((hid, d), BF16)]),
        compiler_params=_cparams(("arbitrary",)),
    )(*visits, xs, wg, wu, wd)


def _combine_kernel(n, final, dest_ref, x_ref, wt_ref, mod_ref, fg_ref, ys_ref, o_ref, buf_ref, sem):
    d = D_MODEL
    tm = x_ref.shape[0]
    base = pl.program_id(0) * tm

    def row_copy(t, j):
        src = dest_ref[j * n + base + t]
        return pltpu.make_async_copy(ys_ref.at[pl.ds(src, 1), :], buf_ref.at[j, pl.ds(t, 1), :], sem)

    _row_dma_burst(tm, row_copy)
    for j in range(2):
        pltpu.make_async_copy(ys_ref.at[pl.ds(0, tm), :], buf_ref.at[j], sem).wait()
    wt = wt_ref[...]
    y = wt[:, 0:1] * buf_ref[0] + wt[:, 1:2] * buf_ref[1]
    out = x_ref[...] + mod_ref[:, 5 * d:6 * d] * y
    if final:
        out = _rms(out, fg_ref[...])
    o_ref[...] = out


def _combine(x2, wt, mod, rows_per_mod, mod_row, fg, ys, dest, final):
    n, d = x2.shape
    tm = MOE_TILE
    if mod_row is None:
        row = lambda i, dst: (i // (rows_per_mod // tm), 0, 0)
    else:
        row = lambda i, dst: (mod_row, 0, 0)
    return pl.pallas_call(
        functools.partial(_combine_kernel, n, final),
        out_shape=jax.ShapeDtypeStruct((n, d), F32),
        grid_spec=pltpu.PrefetchScalarGridSpec(
            num_scalar_prefetch=1, grid=(n // tm,),
            in_specs=[pl.BlockSpec((tm, d), lambda i, dst: (i, 0)), pl.BlockSpec((tm, 8), lambda i, dst: (i, 0)),
                      pl.BlockSpec((None, 1, 6 * d), row), pl.BlockSpec(fg.shape, lambda i, dst: (0, 0)),
                      pl.BlockSpec(memory_space=pl.ANY)],
            out_specs=pl.BlockSpec((tm, d), lambda i, dst: (i, 0)),
            scratch_shapes=[pltpu.VMEM((2, tm, d), F32), pltpu.SemaphoreType.DMA]),
        compiler_params=_cparams(("arbitrary",)),
    )(dest, x2, wt, mod, fg, ys)


def _pick(table, idx):
    hot = idx[..., None] == jnp.arange(table.shape[0], dtype=I32)
    return jnp.sum(jnp.where(hot, table, 0), axis=-1)


def _moe_plan(counts, e, r, rows):
    tm = MOE_TILE
    ends = jnp.cumsum(counts)
    starts = ends - counts
    dest = (_pick(starts, e) + r).reshape(-1)
    first_tile = starts // tm
    nvis = jnp.where(counts > 0, (ends - 1) // tm - first_tile + 1, 0)
    vend = jnp.cumsum(nvis)
    nv = vend[-1:]
    v = jnp.minimum(jnp.arange(rows // tm + MOE_EXPERTS - 1, dtype=I32), nv[0] - 1)
    ve = jnp.sum((vend[None, :] <= v[:, None]).astype(I32), axis=1)
    vt = _pick(first_tile, ve) + v - _pick(vend - nvis, ve)
    vlo = jnp.maximum(_pick(starts, ve) - vt * tm, 0)
    vhi = jnp.minimum(_pick(ends, ve) - vt * tm, tm)
    vfirst = jnp.concatenate([jnp.ones((1,), I32), (vt[1:] != vt[:-1]).astype(I32)])
    return dest, (vt, ve, vlo, vhi, vfirst, nv)


def _moe(x, mod, mod_row, gn, wr, u, layer, wg, wu, wd, fg, final):
    b, n, d = x.shape
    nt = b * n
    x2 = x.reshape(nt, d)
    h, e, wt, r, cnt = _router(x2, mod, n, mod_row, gn, wr, u)
    dest, visits = _moe_plan(cnt[:, 0].astype(I32), e, r, 2 * nt)
    xs = _dispatch(h, dest)
    ys = _gmm(xs, visits, layer, wg, wu, wd)
    out = _combine(x2, wt, mod, n, mod_row, fg, ys, dest, final)
    return out.reshape(b, n, d)


def _rope_tables(rows, dim):
    row = jnp.repeat(jnp.arange(rows, dtype=F32), GRID_W)
    col = jnp.tile(jnp.arange(GRID_W, dtype=F32), rows)
    half = dim // 2
    inv = jnp.power(ROPE_BASE, -jnp.arange(0, half, 2, dtype=F32) / half)
    ar = row[:, None] * inv[None, :]
    ac = col[:, None] * inv[None, :]
    ang = jnp.concatenate([ar, ar, ac, ac], axis=-1)
    return jnp.cos(ang), jnp.sin(ang)


def _even_tables(n, with_rope):
    if with_rope:
        cm, sm = _rope_tables(n // GRID_W, MLA_ROPE)
        cwin, swin = _rope_tables(n // GRID_W, WIN_HEAD_DIM)
    else:
        cm, sm = jnp.ones((n, MLA_ROPE), F32), jnp.zeros((n, MLA_ROPE), F32)
        cwin, swin = jnp.ones((n, WIN_HEAD_DIM), F32), jnp.zeros((n, WIN_HEAD_DIM), F32)
    one, zero = jnp.ones((n, MLA_NOPE), F32), jnp.zeros((n, MLA_NOPE), F32)
    pad = jnp.zeros((n, LANES - MLA_NOPE - MLA_ROPE), F32)
    return (jnp.concatenate([one, cm, pad], axis=1), jnp.concatenate([zero, sm, pad], axis=1),
            jnp.concatenate([cwin, cwin], axis=1), jnp.concatenate([swin, swin], axis=1))


def _even_weights(w_in, qg, w_uq, kvg, w_ukv):
    d = w_in.shape[0]
    o = np.cumsum([0, MLA_Q_RANK, MLA_KV_RANK, MLA_ROPE, 512, 128, 128])
    cq, ckv, kr, qw, kw, vw = [w_in[:, o[i]:o[i + 1]] for i in range(6)]
    z = lambda c: jnp.zeros((d, c), F32)
    kr128 = jnp.concatenate([z(MLA_NOPE), kr, z(LANES - MLA_NOPE - MLA_ROPE)], axis=1)
    dup = lambda t: jnp.concatenate([t[:, 0:64], t[:, 0:64], t[:, 64:128], t[:, 64:128]], axis=1)
    win = jnp.concatenate([cq, ckv, kr128, qw, dup(kw), dup(vw)], axis=1).astype(BF16)
    uq = w_uq.reshape(MLA_Q_RANK, MLA_HEADS, MLA_NOPE + MLA_ROPE)
    uq = jnp.pad(uq, ((0, 0), (0, 0), (0, LANES - MLA_NOPE - MLA_ROPE))).reshape(MLA_Q_RANK, MLA_HEADS * LANES)
    ukv = w_ukv.reshape(MLA_KV_RANK, MLA_HEADS, MLA_NOPE + MLA_V)
    ukk = jnp.pad(ukv[:, :, :MLA_NOPE], ((0, 0), (0, 0), (0, LANES - MLA_NOPE))).reshape(MLA_KV_RANK, MLA_HEADS * LANES)
    ukvv = ukv[:, :, MLA_NOPE:].reshape(MLA_KV_RANK, MLA_HEADS * MLA_V)
    return (win, qg.reshape(1, -1), uq.astype(BF16), kvg.reshape(1, -1), ukk.astype(BF16), ukvv.astype(BF16))


def _odd_weights(w_in, w_g2, b_g, ln_g, ln_b, w_s, b_s):
    d = w_in.shape[0]
    o = np.cumsum([0, GLA_K, GLA_K, GLA_V, 2 * GLA_GATE_RANK, GLA_V, SG_WIDTH, SG_WIDTH])
    q, k, v, g, r, u, vg = [w_in[:, o[i]:o[i + 1]] for i in range(7)]
    g128 = jnp.concatenate([g, jnp.zeros((d, LANES - 2 * GLA_GATE_RANK), F32)], axis=1)
    win = jnp.concatenate([q, k, v, g128, r, u, vg], axis=1).astype(BF16)
    wvt = v.T.astype(BF16)
    zr = jnp.zeros((GLA_GATE_RANK, GLA_K), F32)
    pad = jnp.zeros((LANES - 2 * GLA_GATE_RANK, GLA_K), F32)
    wg = jnp.concatenate([jnp.concatenate([w_g2[0], zr, pad], axis=0),
                          jnp.concatenate([zr, w_g2[1], pad], axis=0)], axis=1)
    bg = b_g.reshape(1, 2 * GLA_K)
    return (win, wvt, wg, bg, ln_g.reshape(1, -1), ln_b.reshape(1, -1), w_s.astype(BF16), b_s.T)


def kernel(x, c, ctx, c_ctx, ada_w, ada_b, norm_mix_g, norm_ffn_g, even_w_in, mla_q_norm_g, mla_w_uq, mla_kv_norm_g, mla_w_ukv, win_sink, even_w_out, odd_w_in, gla_w_g2, gla_b_g, gla_norm_g, sg_ln_g, sg_ln_b, sg_w_s, sg_b_s, odd_w_out, moe_w_rg, moe_w_re, moe_w_gate, moe_w_up, moe_w_down, final_norm_g):
    b, n, d = x.shape
    lc = ctx.shape[1]
    depth = ada_w.shape[0]
    assert depth == 2 and d == D_MODEL and b < 8
    assert n % 512 == 0 and lc % MOE_TILE == 0 and n % GRID_W == 0
    tm = 512 if n % 512 == 0 else 256
    tq = 256

    cond8 = jnp.concatenate([c, c_ctx[None, :], jnp.zeros((8 - b - 1, d), F32)], axis=0)
    mod_all = _adaln(cond8, ada_w, ada_b).reshape(depth, 8, 1, 6 * d)
    ctx_row = b
    u_tri = jnp.asarray(np.triu(np.ones((MOE_TILE, MOE_TILE), np.float32), 1), BF16)
    fg = final_norm_g.reshape(1, d)

    def router_w(layer):
        return jnp.concatenate([moe_w_rg[layer].T, jnp.zeros((8 - MOE_GROUPS, d), F32), moe_w_re[layer].T], axis=0)

    def moe(xx, layer, mod_row, final):
        return _moe(xx, mod_all[layer], mod_row, norm_ffn_g[layer].reshape(1, d), router_w(layer), u_tri,
                    layer, moe_w_gate, moe_w_up, moe_w_down, fg, final)

    mod = mod_all[0]
    gn = norm_mix_g[0].reshape(1, d)
    ew = _even_weights(even_w_in[0], mla_q_norm_g[0], mla_w_uq[0], mla_kv_norm_g[0], mla_w_ukv[0])
    qm_l, km_l, vm_l, qw_l, kw_l, vw_l = _even_in(x, mod, None, gn, ew, _even_tables(n, True), tm)
    qm_c, km_c, vm_c, qw_c, kw_c, vw_c = _even_in(ctx, mod, ctx_row, gn, ew, _even_tables(lc, False), lc)
    w_out = even_w_out[0].astype(BF16)
    sink = win_sink[0]
    oa_l = _mla_attn(qm_l, [(km_l, vm_l), (km_c, vm_c)], tq)
    ob_l = _gqa(qw_l, kw_l, vw_l, kw_c, vw_c, sink, True)
    xl = _even_out(x, oa_l, ob_l, w_out, mod, None, tm)
    oa_c = _mla_attn(qm_c, [(km_c, vm_c)], lc)
    ob_c = _gqa(qw_c, None, None, kw_c, vw_c, sink, False)
    xc = _even_out(ctx, oa_c, ob_c, w_out, mod, ctx_row, lc)
    xl = moe(xl, 0, None, False)
    xc = moe(xc, 0, ctx_row, False)

    mod = mod_all[1]
    gn = norm_mix_g[1].reshape(1, d)
    ow = _odd_weights(odd_w_in[0], gla_w_g2[0], gla_b_g[0], sg_ln_g[0], sg_ln_b[0], sg_w_s[0], sg_b_s[0])
    q_l, k_l, v_l, vt_l, la_l, r_l, dl_l = _odd_in(xl, mod, None, gn, ow, 256)
    q_c, k_c, v_c, vt_c, la_c, _, _ = _odd_in(xc, mod, ctx_row, gn, ow, lc)
    cum_np, pm_np, nlev = _gla_tables()
    cum = jnp.asarray(cum_np, BF16)
    pm = jnp.asarray(pm_np, F32)
    s0 = jnp.zeros((b, 2, GLA_HEADS, GLA_DV, GLA_DK), F32)
    _, s_ctx = _gla(q_c, k_c, v_c, vt_c, la_c, s0, cum, pm, nlev)
    ol, _ = _gla(q_l, k_l, v_l, vt_l, la_l, s_ctx, cum, pm, nlev)
    xl = _odd_out(xl, ol, r_l, dl_l, gla_norm_g[0].reshape(1, -1), odd_w_out[0].astype(BF16), mod, tm)
    return moe(xl, 1, None, True)
```

```python
import functools

import numpy as np
import jax
import jax.numpy as jnp
from jax import lax
from jax.experimental import pallas as pl
from jax.experimental.pallas import tpu as pltpu

F32 = jnp.float32
BF16 = jnp.bfloat16
I32 = jnp.int32

D_MODEL = 1024
GRID_W = 64
EPS = 1e-6
ROPE_BASE = 10000.0
MLA_HEADS = 8
MLA_Q_RANK = 256
MLA_KV_RANK = 128
MLA_NOPE = 64
MLA_ROPE = 32
MLA_V = 64
WIN_HEADS = 8
WIN_KV_HEADS = 2
WIN_HEAD_DIM = 64
WIN_BLOCK = 128
GLA_HEADS = 4
GLA_DK = 64
GLA_DV = 128
GLA_GATE_RANK = 16
GLA_TAU = 16.0
GLA_K = GLA_HEADS * GLA_DK
GLA_V = GLA_HEADS * GLA_DV
SG_GROUPS = 4
SG_CHUNK = 128
SG_WIDTH = 512
MOE_GROUPS = 4
MOE_PER_GROUP = 8
MOE_EXPERTS = 32
MOE_HIDDEN = 512

LANES = 128
GLA_BLOCK = 128
MOE_TILE = 256
DMA_UNROLL = 8
NEG = -1e30
LOG2E = 1.4426950408889634
VMEM_LIMIT = 56 * 1024 * 1024


def _cparams(sem):
    return pltpu.CompilerParams(dimension_semantics=sem, vmem_limit_bytes=VMEM_LIMIT)


def _dot(a, b):
    return jnp.dot(a, b, preferred_element_type=F32)


def _dot_nt(a, b):
    return lax.dot_general(a, b, (((1,), (1,)), ((), ())), preferred_element_type=F32)


def _split2(a):
    hi = a.astype(BF16)
    lo = (a - hi.astype(F32)).astype(BF16)
    return hi, lo


def _split3(a):
    hi = a.astype(BF16)
    r = a - hi.astype(F32)
    mid = r.astype(BF16)
    lo = (r - mid.astype(F32)).astype(BF16)
    return hi, mid, lo


def _rms(x, g):
    ms = jnp.mean(x * x, axis=-1, keepdims=True)
    return x * lax.rsqrt(ms + EPS) * g


def _lane_tile(t, reps):
    return t if reps == 1 else jnp.concatenate([t] * reps, axis=1)


def _rope(t, cos, sin, quarter):
    n = t.shape[1]
    lane = lax.broadcasted_iota(I32, t.shape, 1)
    first = (lane & (2 * quarter - 1)) < quarter
    rot = jnp.where(first, -pltpu.roll(t, n - quarter, 1), pltpu.roll(t, quarter, 1))
    return t * cos + rot * sin


def _adaln_kernel(c_ref, w_ref, b_ref, o_ref):
    c = c_ref[...]
    s_hi, s_lo = _split2(c * jax.nn.sigmoid(c))
    w_hi, w_lo = _split2(w_ref[...])
    o_ref[...] = _dot(s_hi, w_hi) + _dot(s_lo, w_hi) + _dot(s_hi, w_lo) + b_ref[...]


def _adaln(cond8, ada_w, ada_b):
    depth, d, n6 = ada_w.shape
    tn = 1536
    return pl.pallas_call(
        _adaln_kernel,
        out_shape=jax.ShapeDtypeStruct((depth, 8, n6), F32),
        grid=(depth, n6 // tn),
        in_specs=[
            pl.BlockSpec((8, d), lambda l, j: (0, 0)),
            pl.BlockSpec((None, d, tn), lambda l, j: (l, 0, j)),
            pl.BlockSpec((None, 1, tn), lambda l, j: (l, 0, j)),
        ],
        out_specs=pl.BlockSpec((None, 8, tn), lambda l, j: (l, 0, j)),
        compiler_params=_cparams(("parallel", "parallel")),
    )(cond8, ada_w, ada_b.reshape(depth, 1, n6))


def _even_in_kernel(x_ref, mod_ref, gn_ref, win_ref, qg_ref, wuq_ref, kvg_ref, wukk_ref, wukv_ref,
                    cq_ref, sq_ref, cw_ref, sw_ref,
                    qm_ref, km_ref, vm_ref, qw_ref, kw_ref, vw_ref):
    d = D_MODEL
    mod = mod_ref[...]
    h = _rms(x_ref[...], gn_ref[...]) * (1.0 + mod[:, d:2 * d]) + mod[:, 0:d]
    z = _dot(h.astype(BF16), win_ref[...])
    cq, sq, cw, sw = cq_ref[...], sq_ref[...], cw_ref[...], sw_ref[...]
    cqn = _rms(z[:, 0:256], qg_ref[...]).astype(BF16)
    q = _dot(cqn, wuq_ref[...])
    q = _rope(q, _lane_tile(cq, 8), _lane_tile(sq, 8), MLA_ROPE // 4)
    qm_ref[...] = (q * (LOG2E * (MLA_NOPE + MLA_ROPE) ** -0.5)).astype(BF16)
    ckvn = _rms(z[:, 256:384], kvg_ref[...]).astype(BF16)
    kn = _dot(ckvn, wukk_ref[...])
    kr = _rope(z[:, 384:512], cq, sq, MLA_ROPE // 4)
    km_ref[...] = (kn + _lane_tile(kr, 8)).astype(BF16)
    vm_ref[...] = _dot(ckvn, wukv_ref[...]).astype(BF16)
    qw = _rope(z[:, 512:1024], _lane_tile(cw, 4), _lane_tile(sw, 4), WIN_HEAD_DIM // 4)
    qw_ref[...] = (qw * (WIN_HEAD_DIM ** -0.5)).astype(BF16)
    kw = _rope(z[:, 1024:1280], _lane_tile(cw, 2), _lane_tile(sw, 2), WIN_HEAD_DIM // 4)
    kw_ref[...] = kw.astype(BF16)
    vw_ref[...] = z[:, 1280:1536].astype(BF16)


def _even_in(x, mod, mod_row, gn, wts, tabs, tm):
    b, n, d = x.shape
    win, qg, wuq, kvg, wukk, wukv = wts
    nt = n // tm
    row = (lambda bi, i: (bi, 0, 0)) if mod_row is None else (lambda bi, i: (mod_row, 0, 0))
    full = lambda a: pl.BlockSpec(a.shape, lambda bi, i: (0,) * a.ndim)
    tab = pl.BlockSpec((tm, LANES), lambda bi, i: (i, 0))
    outw = (1024, 1024, 512, 512, 256, 256)
    return pl.pallas_call(
        _even_in_kernel,
        out_shape=[jax.ShapeDtypeStruct((b, n, w), BF16) for w in outw],
        grid=(b, nt),
        in_specs=[pl.BlockSpec((None, tm, d), lambda bi, i: (bi, i, 0)),
                  pl.BlockSpec((None, 1, 6 * d), row),
                  full(gn), full(win), full(qg), full(wuq), full(kvg), full(wukk), full(wukv),
                  tab, tab, tab, tab],
        out_specs=[pl.BlockSpec((None, tm, w), lambda bi, i: (bi, i, 0)) for w in outw],
        compiler_params=_cparams(("parallel", "parallel")),
    )(x, mod, gn, win, qg, wuq, kvg, wukk, wukv, *tabs)


def _mla_attn_kernel(nseg, q_ref, *refs):
    ks, vs, o_ref = refs[0:2 * nseg:2], refs[1:2 * nseg:2], refs[2 * nseg]
    tq = q_ref.shape[0]
    lane = lax.broadcasted_iota(I32, (tq, LANES), 1)
    for hp in range(MLA_HEADS // 2):
        pair = slice(hp * LANES, (hp + 1) * LANES)
        outs = []
        for e in range(2):
            hs = slice((2 * hp + e) * LANES, (2 * hp + e + 1) * LANES)
            qh = q_ref[:, hs]
            ss = [_dot_nt(qh, k[:, hs]) for k in ks]
            m = functools.reduce(jnp.maximum, [jnp.max(s, axis=-1, keepdims=True) for s in ss])
            ps = [jnp.exp2(s - m) for s in ss]
            l = functools.reduce(jnp.add, [jnp.sum(p, axis=-1, keepdims=True) for p in ps])
            o = functools.reduce(jnp.add, [_dot(p.astype(BF16), v[:, pair]) for p, v in zip(ps, vs)])
            outs.append(o * (1.0 / l))
        o_ref[:, pair] = jnp.where(lane < MLA_V, outs[0], outs[1]).astype(BF16)


def _mla_attn(q, segs, tq):
    b, n, _ = q.shape
    in_specs = [pl.BlockSpec((None, tq, 1024), lambda bi, i: (bi, i, 0))]
    args = [q]
    for k, v in segs:
        lk = k.shape[1]
        in_specs += [pl.BlockSpec((None, lk, 1024), lambda bi, i: (bi, 0, 0)),
                     pl.BlockSpec((None, lk, 512), lambda bi, i: (bi, 0, 0))]
        args += [k, v]
    return pl.pallas_call(
        functools.partial(_mla_attn_kernel, len(segs)),
        out_shape=jax.ShapeDtypeStruct((b, n, 512), BF16),
        grid=(b, n // tq),
        in_specs=in_specs,
        out_specs=pl.BlockSpec((None, tq, 512), lambda bi, i: (bi, i, 0)),
        compiler_params=_cparams(("parallel", "parallel")),
    )(*args)


def _gqa_kernel(has_win, nb, sink_ref, q_ref, *refs):
    if has_win:
        kp, kc, kn, vp, vc, vn, kx, vx, o_ref = refs
    else:
        kx, vx, o_ref = refs
    tq = q_ref.shape[0]
    i = pl.program_id(1)
    lane = lax.broadcasted_iota(I32, (tq, LANES), 1)
    row2 = lax.broadcasted_iota(I32, (2 * tq, 1), 0)
    half = WIN_HEAD_DIM
    for j in range(WIN_HEADS // 2):
        g = j // 2
        gs = slice(g * LANES, (g + 1) * LANES)
        qp = q_ref[:, j * LANES:(j + 1) * LANES]
        zero = jnp.zeros_like(qp)
        q2 = jnp.concatenate([jnp.where(lane < half, qp, zero), jnp.where(lane >= half, qp, zero)], axis=0)
        if has_win:
            kcat = jnp.concatenate([kp[:, gs], kc[:, gs], kn[:, gs], kx[:, gs]], axis=0)
            vcat = jnp.concatenate([vp[:, gs], vc[:, gs], vn[:, gs], vx[:, gs]], axis=0)
        else:
            kcat, vcat = kx[:, gs], vx[:, gs]
        s = _dot_nt(q2, kcat)
        if has_win:
            w = WIN_BLOCK
            r = lax.broadcasted_iota(I32, s.shape, 0) & (tq - 1)
            c = lax.broadcasted_iota(I32, s.shape, 1)
            big = jnp.int32(1 << 20)
            no_prev = jnp.where(i > 0, 0, big)
            no_next = jnp.where(i < nb - 1, 0, big)
            ok_prev = c >= r + no_prev
            ok_next = (c - 2 * w) <= r - no_next
            valid = ((c >= w) | ok_prev) & ((c < 2 * w) | (c >= 3 * w) | ok_next)
            s = jnp.where(valid, s, NEG)
        sk = jnp.where(row2 < tq, sink_ref[2 * j], sink_ref[2 * j + 1])
        m = jnp.maximum(jnp.max(s, axis=-1, keepdims=True), sk)
        p = jnp.exp(s - m)
        l = jnp.sum(p, axis=-1, keepdims=True) + jnp.exp(sk - m)
        o2 = _dot(p.astype(BF16), vcat) * (1.0 / l)
        o_ref[:, j * LANES:(j + 1) * LANES] = jnp.where(lane < half, o2[:tq], o2[tq:]).astype(BF16)


def _gqa(q, k, v, kx, vx, sink, has_win):
    b, n, _ = q.shape
    lc = kx.shape[1]
    smem = pl.BlockSpec(memory_space=pltpu.SMEM)
    ctxs = pl.BlockSpec((None, lc, 256), lambda bi, i: (bi, 0, 0))
    if has_win:
        tq = WIN_BLOCK
        nb = n // tq
        blk = lambda f: pl.BlockSpec((None, tq, 256), f)
        prev = lambda bi, i: (bi, jnp.maximum(i - 1, 0), 0)
        cur = lambda bi, i: (bi, i, 0)
        nxt = lambda bi, i: (bi, jnp.minimum(i + 1, nb - 1), 0)
        in_specs = [smem, pl.BlockSpec((None, tq, 512), cur),
                    blk(prev), blk(cur), blk(nxt), blk(prev), blk(cur), blk(nxt), ctxs, ctxs]
        args = (sink, q, k, k, k, v, v, v, kx, vx)
    else:
        tq, nb = n, 1
        in_specs = [smem, pl.BlockSpec((None, tq, 512), lambda bi, i: (bi, i, 0)), ctxs, ctxs]
        args = (sink, q, kx, vx)
    return pl.pallas_call(
        functools.partial(_gqa_kernel, has_win, nb),
        out_shape=jax.ShapeDtypeStruct((b, n, 512), BF16),
        grid=(b, nb),
        in_specs=in_specs,
        out_specs=pl.BlockSpec((None, tq, 512), lambda bi, i: (bi, i, 0)),
        compiler_params=_cparams(("parallel", "parallel")),
    )(*args)


def _even_out_kernel(x_ref, a_ref, b_ref, w_ref, mod_ref, o_ref):
    d = D_MODEL
    y = _dot(a_ref[...], w_ref[0:512, :]) + _dot(b_ref[...], w_ref[512:1024, :])
    o_ref[...] = x_ref[...] + mod_ref[:, 2 * d:3 * d] * y


def _even_out(x, oa, ob, w, mod, mod_row, tm):
    b, n, d = x.shape
    row = (lambda bi, i: (bi, 0, 0)) if mod_row is None else (lambda bi, i: (mod_row, 0, 0))
    act = lambda wd: pl.BlockSpec((None, tm, wd), lambda bi, i: (bi, i, 0))
    return pl.pallas_call(
        _even_out_kernel,
        out_shape=jax.ShapeDtypeStruct((b, n, d), F32),
        grid=(b, n // tm),
        in_specs=[act(d), act(512), act(512), pl.BlockSpec(w.shape, lambda bi, i: (0, 0)),
                  pl.BlockSpec((None, 1, 6 * d), row)],
        out_specs=act(d),
        compiler_params=_cparams(("parallel", "parallel")),
    )(x, oa, ob, w, mod)


def _log_sigmoid(z):
    return jnp.minimum(z, 0.0) - jnp.log(1.0 + jnp.exp(-jnp.abs(z)))


def _odd_in_kernel(x_ref, mod_ref, gn_ref, win_ref, wvt_ref, wg_ref, bg_ref, lng_ref, lnb_ref, ws_ref, bst_ref,
                   q_ref, k_ref, v_ref, vt_ref, la_ref, r_ref, dl_ref):
    d = D_MODEL
    tm = x_ref.shape[0]
    mod = mod_ref[...]
    h = (_rms(x_ref[...], gn_ref[...]) * (1.0 + mod[:, d:2 * d]) + mod[:, 0:d]).astype(BF16)
    z = _dot(h, win_ref[...])
    q_ref[...] = z[:, 0:256] * (GLA_DK ** -0.5)
    k_ref[...] = z[:, 256:512]
    v_ref[...] = z[:, 512:1024].astype(BF16)
    vt_ref[...] = _dot_nt(wvt_ref[...], h).astype(BF16)
    g_hi, g_lo = _split2(z[:, 1024:1152])
    w_hi, w_lo = _split2(wg_ref[...])
    zg = _dot(g_hi, w_hi) + _dot(g_lo, w_hi) + _dot(g_hi, w_lo) + bg_ref[...]
    la_ref[...] = _log_sigmoid(zg) / GLA_TAU
    r_ref[...] = z[:, 1152:1664]
    u = jax.nn.gelu(z[:, 1664:2176])
    vg = jax.nn.gelu(z[:, 2176:2688])
    mu = jnp.mean(vg, axis=-1, keepdims=True)
    vc = vg - mu
    var = jnp.mean(vc * vc, axis=-1, keepdims=True)
    vn = (vc * lax.rsqrt(var + EPS) * lng_ref[...] + lnb_ref[...]).astype(BF16)
    bst = bst_ref[...]
    for c in range(tm // SG_CHUNK):
        rows = slice(c * SG_CHUNK, (c + 1) * SG_CHUNK)
        parts = []
        for g in range(SG_GROUPS):
            cols = slice(g * LANES, (g + 1) * LANES)
            parts.append(_dot(ws_ref[g], vn[rows, cols]) + bst[:, g:g + 1])
        dl_ref[rows, :] = (u[rows, :] * jnp.concatenate(parts, axis=1)).astype(BF16)


def _odd_in(x, mod, mod_row, gn, wts, tm):
    b, n, d = x.shape
    win, wvt, wg, bg, lng, lnb, ws, bst = wts
    row = (lambda bi, i: (bi, 0, 0)) if mod_row is None else (lambda bi, i: (mod_row, 0, 0))
    full = lambda a: pl.BlockSpec(a.shape, lambda bi, i: (0,) * a.ndim)
    act = lambda wd: pl.BlockSpec((None, tm, wd), lambda bi, i: (bi, i, 0))
    outs = [((b, n, 256), F32, act(256)), ((b, n, 256), F32, act(256)), ((b, n, 512), BF16, act(512)),
            ((b, 512, n), BF16, pl.BlockSpec((None, 512, tm), lambda bi, i: (bi, 0, i))),
            ((b, n, 512), F32, act(512)), ((b, n, 512), F32, act(512)), ((b, n, 512), BF16, act(512))]
    return pl.pallas_call(
        _odd_in_kernel,
        out_shape=[jax.ShapeDtypeStruct(s, t) for s, t, _ in outs],
        grid=(b, n // tm),
        in_specs=[act(d), pl.BlockSpec((None, 1, 6 * d), row), full(gn), full(win), full(wvt), full(wg),
                  full(bg), full(lng), full(lnb), full(ws), full(bst)],
        out_specs=[sp for _, _, sp in outs],
        compiler_params=_cparams(("parallel", "parallel")),
    )(x, mod, gn, win, wvt, wg, bg, lng, lnb, ws, bst)


def _gla_tables():
    c = GLA_BLOCK
    t = np.arange(c)[:, None]
    u = np.arange(c)[None, :]
    levels = [c >> i for i in range(int(np.log2(c)) + 1)]
    cum = np.zeros((2, 2 * len(levels), c, c), np.float32)
    pair = np.zeros((2, len(levels), c, c), np.float32)
    for li, m in enumerate(levels):
        same = (t // m) == (u // m)
        cum[0, 2 * li] = same & (u <= t)
        cum[0, 2 * li + 1] = same & (u > t)
        cum[1, 2 * li] = same & (u >= t)
        cum[1, 2 * li + 1] = same & (u < t)
        if li > 0:
            pair[0, li] = ((t // m) % 2 == 1) & ((u // m) == (t // m) - 1)
            pair[1, li] = ((t // m) % 2 == 0) & ((u // m) == (t // m) + 1)
    pair[:, 0] = np.eye(c, dtype=np.float32)
    return cum.reshape(2, 2 * len(levels) * c, c), pair, len(levels)


def _gla_chain(nlev, q, k, la, v_ref, vt_ref, cum2, pm_ref, st_ref, o_ref):
    c = GLA_BLOCK
    l_hi, l_mid = _split2(la)
    ex = jnp.exp(_dot(cum2, jnp.concatenate([l_hi, l_mid], axis=0)))
    gtot = jnp.exp(jnp.sum(la, axis=0, keepdims=True))
    yield
    qe = [(q * ex[(2 * li) * c:(2 * li + 1) * c]).astype(BF16) for li in range(nlev)]
    ke = [(k * ex[(2 * li + 1) * c:(2 * li + 2) * c]).astype(BF16) for li in range(nlev)]
    qb, kb = q.astype(BF16), k.astype(BF16)
    states = [st_ref[hd] for hd in range(GLA_HEADS)]
    yield
    outs, new_states = [], []
    lane = lax.broadcasted_iota(I32, (c, LANES), 1)
    zero = jnp.zeros((c, LANES), BF16)
    for hd in range(GLA_HEADS):
        ps = slice((hd // 2) * LANES, (hd // 2 + 1) * LANES)
        vs = slice(hd * GLA_DV, (hd + 1) * GLA_DV)
        mine = (lane < GLA_DK) if hd % 2 == 0 else (lane >= GLA_DK)
        pick = lambda t: jnp.where(mine, t[:, ps], zero)
        a = pm_ref[0] * _dot_nt(pick(qb), kb[:, ps])
        for li in range(1, nlev):
            a = a + pm_ref[li] * _dot_nt(pick(qe[li]), ke[li][:, ps])
        outs.append(_dot_nt(qe[0][:, ps], states[hd].astype(BF16)) + _dot(a.astype(BF16), v_ref[:, vs]))
        new_states.append(states[hd] * gtot[:, ps] + _dot(vt_ref[vs, :], pick(ke[0])))
        yield
    o_ref[...] = jnp.concatenate(outs, axis=1)
    for hd in range(GLA_HEADS):
        st_ref[hd] = new_states[hd]
    yield


def _gla_kernel(nlev, nb, *refs):
    ins_f, ins_b = refs[0:5], refs[5:10]
    cum_ref, pm_ref, s0_ref, of_ref, ob_ref, sf_ref = refs[10:16]
    st_refs = refs[16:]
    step = pl.program_id(1)

    @pl.when(step == 0)
    def _():
        for bb in range(nb):
            for d_ in range(2):
                st_refs[2 * bb + d_][...] = s0_ref[bb, d_]

    chains = []
    for bb in range(nb):
        for d_, (ins, o_ref) in enumerate(((ins_f, of_ref), (ins_b, ob_ref))):
            q_ref, k_ref, v_ref, vt_ref, la_ref = ins
            chains.append(_gla_chain(nlev, q_ref[bb], k_ref[bb], la_ref[bb], v_ref.at[bb], vt_ref.at[bb],
                                     cum_ref[d_], pm_ref.at[d_], st_refs[2 * bb + d_], o_ref.at[bb]))
    for _ in range(GLA_HEADS + 3):
        for ch in chains:
            next(ch)
    for bb in range(nb):
        for d_ in range(2):
            sf_ref[bb, d_] = st_refs[2 * bb + d_][...]


def _gla(q, k, v, vt, la, s0, cum2, pm, nlev):
    b, n, _ = q.shape
    c = GLA_BLOCK
    nc = n // c
    nb = 2 if b % 2 == 0 else 1
    specs = []
    for d_ in range(2):
        pos = (lambda s_: s_) if d_ == 0 else (lambda s_: nc - 1 - s_)
        specs += [pl.BlockSpec((nb, c, 256), lambda bi, s_, pos=pos: (bi, pos(s_), 0)),
                  pl.BlockSpec((nb, c, 256), lambda bi, s_, pos=pos: (bi, pos(s_), 0)),
                  pl.BlockSpec((nb, c, 512), lambda bi, s_, pos=pos: (bi, pos(s_), 0)),
                  pl.BlockSpec((nb, 512, c), lambda bi, s_, pos=pos: (bi, 0, pos(s_))),
                  pl.BlockSpec((nb, c, 256), lambda bi, s_, pos=pos, d_=d_: (bi, pos(s_), d_))]
    st_spec = pl.BlockSpec((nb, 2, GLA_HEADS, GLA_DV, LANES), lambda bi, s_: (bi, 0, 0, 0, 0))
    full = lambda a: pl.BlockSpec(a.shape, lambda bi, s_: (0,) * a.ndim)
    return pl.pallas_call(
        functools.partial(_gla_kernel, nlev, nb),
        out_shape=[jax.ShapeDtypeStruct((b, n, GLA_V), F32), jax.ShapeDtypeStruct((b, n, GLA_V), F32),
                   jax.ShapeDtypeStruct((b, 2, GLA_HEADS, GLA_DV, LANES), F32)],
        grid=(b // nb, nc),
        in_specs=specs + [full(cum2), full(pm), st_spec],
        out_specs=[pl.BlockSpec((nb, c, GLA_V), lambda bi, s_: (bi, s_, 0)),
                   pl.BlockSpec((nb, c, GLA_V), lambda bi, s_: (bi, nc - 1 - s_, 0)), st_spec],
        scratch_shapes=[pltpu.VMEM((GLA_HEADS, GLA_DV, LANES), F32) for _ in range(2 * nb)],
        compiler_params=_cparams(("parallel", "arbitrary")),
    )(q, k, v, vt, la, q, k, v, vt, la, cum2, pm, s0)


def _odd_out_kernel(x_ref, of_ref, ob_ref, r_ref, dl_ref, gg_ref, w_ref, mod_ref, o_ref):
    d = D_MODEL
    o = of_ref[...] + ob_ref[...]
    gg = gg_ref[...]
    r = r_ref[...]
    parts = []
    for hd in range(GLA_HEADS):
        vs = slice(hd * GLA_DV, (hd + 1) * GLA_DV)
        oh = o[:, vs]
        parts.append(oh * lax.rsqrt(jnp.mean(oh * oh, axis=-1, keepdims=True) + EPS) * gg[:, vs])
    cl = (jnp.concatenate(parts, axis=1) * (r * jax.nn.sigmoid(r))).astype(BF16)
    y = _dot(cl, w_ref[0:512, :]) + _dot(dl_ref[...], w_ref[512:1024, :])
    o_ref[...] = x_ref[...] + mod_ref[:, 2 * d:3 * d] * y


def _odd_out(x, o_fwd, o_bwd, r, dl, gg, w, mod, tm):
    b, n, d = x.shape
    act = lambda wd: pl.BlockSpec((None, tm, wd), lambda bi, i: (bi, i, 0))
    return pl.pallas_call(
        _odd_out_kernel,
        out_shape=jax.ShapeDtypeStruct((b, n, d), F32),
        grid=(b, n // tm),
        in_specs=[act(d), act(GLA_V), act(GLA_V), act(512), act(512),
                  pl.BlockSpec(gg.shape, lambda bi, i: (0, 0)), pl.BlockSpec(w.shape, lambda bi, i: (0, 0)),
                  pl.BlockSpec((None, 1, 6 * d), lambda bi, i: (bi, 0, 0))],
        out_specs=act(d),
        compiler_params=_cparams(("parallel", "parallel")),
    )(x, o_fwd, o_bwd, r, dl, gg, w, mod)


def _router_kernel(x_ref, mod_ref, gn_ref, wr_ref, u_ref, h_ref, e_ref, wt_ref, r_ref, cnt_ref, carry_ref):
    d = D_MODEL
    tm = x_ref.shape[0]
    i = pl.program_id(0)

    @pl.when(i == 0)
    def _():
        carry_ref[...] = jnp.zeros_like(carry_ref)

    mod = mod_ref[...]
    h = _rms(x_ref[...], gn_ref[...]) * (1.0 + mod[:, 4 * d:5 * d]) + mod[:, 3 * d:4 * d]
    h_ref[...] = h
    h_hi, h_lo = _split2(h)
    w_hi, w_lo = _split2(wr_ref[...])
    lg = _dot_nt(w_hi, h_hi) + _dot_nt(w_lo, h_hi) + _dot_nt(w_hi, h_lo)
    rid = lax.broadcasted_iota(I32, (8, tm), 0)
    gl = jnp.where(rid < MOE_GROUPS, lg[0:8], NEG)
    gmax = jnp.max(gl, axis=0, keepdims=True)
    gsel = jnp.min(jnp.where(gl == gmax, rid, 8), axis=0, keepdims=True)
    pmax = 1.0 / jnp.sum(jnp.where(rid < MOE_GROUPS, jnp.exp(gl - gmax), 0.0), axis=0, keepdims=True)
    e_in = jnp.zeros((MOE_PER_GROUP, tm), F32)
    for g in range(MOE_GROUPS):
        e_in = e_in + jnp.where(gsel == g, lg[8 + 8 * g:16 + 8 * g], 0.0)
    v1 = jnp.max(e_in, axis=0, keepdims=True)
    i1 = jnp.min(jnp.where(e_in == v1, rid, 8), axis=0, keepdims=True)
    e_rest = jnp.where(rid == i1, -jnp.inf, e_in)
    v2 = jnp.max(e_rest, axis=0, keepdims=True)
    i2 = jnp.min(jnp.where(e_rest == v2, rid, 8), axis=0, keepdims=True)
    t = jnp.exp(v2 - v1)
    w1 = pmax / (1.0 + t)
    w2 = pmax * t / (1.0 + t)
    e1 = gsel * MOE_PER_GROUP + i1
    e2 = gsel * MOE_PER_GROUP + i2
    eid = lax.broadcasted_iota(I32, (MOE_EXPERTS, tm), 0)
    oh1 = jnp.where(eid == e1, 1.0, 0.0)
    oh2 = jnp.where(eid == e2, 1.0, 0.0)
    ohs = oh1 + oh2
    base = carry_ref[:, 0:1] + _dot(ohs.astype(BF16), u_ref[...])
    r1 = jnp.sum(oh1 * base, axis=0, keepdims=True)
    r2 = jnp.sum(oh2 * base, axis=0, keepdims=True)
    carry_ref[...] = carry_ref[...] + jnp.sum(ohs, axis=1, keepdims=True)
    cnt_ref[...] = carry_ref[...]
    e_ref[...] = jnp.concatenate([e1, e2], axis=0)
    r_ref[...] = jnp.concatenate([r1, r2], axis=0).astype(I32)
    w8 = jnp.concatenate([w1, w2, jnp.zeros((6, tm), F32)], axis=0)
    wt_ref[...] = w8.T


def _router(x2, mod, rows_per_mod, mod_row, gn, wr, u):
    n, d = x2.shape
    tm = MOE_TILE
    if mod_row is None:
        row = lambda i: (i // (rows_per_mod // tm), 0, 0)
    else:
        row = lambda i: (mod_row, 0, 0)
    return pl.pallas_call(
        _router_kernel,
        out_shape=[jax.ShapeDtypeStruct((n, d), F32), jax.ShapeDtypeStruct((2, n), I32),
                   jax.ShapeDtypeStruct((n, 8), F32), jax.ShapeDtypeStruct((2, n), I32),
                   jax.ShapeDtypeStruct((MOE_EXPERTS, LANES), F32)],
        grid=(n // tm,),
        in_specs=[pl.BlockSpec((tm, d), lambda i: (i, 0)), pl.BlockSpec((None, 1, 6 * d), row),
                  pl.BlockSpec(gn.shape, lambda i: (0, 0)), pl.BlockSpec(wr.shape, lambda i: (0, 0)),
                  pl.BlockSpec(u.shape, lambda i: (0, 0))],
        out_specs=[pl.BlockSpec((tm, d), lambda i: (i, 0)), pl.BlockSpec((2, tm), lambda i: (0, i)),
                   pl.BlockSpec((tm, 8), lambda i: (i, 0)), pl.BlockSpec((2, tm), lambda i: (0, i)),
                   pl.BlockSpec((MOE_EXPERTS, LANES), lambda i: (0, 0))],
        scratch_shapes=[pltpu.VMEM((MOE_EXPERTS, LANES), F32)],
        compiler_params=_cparams(("arbitrary",)),
    )(x2, mod, gn, wr, u)


def _row_dma_burst(tm, make_copy):
    def issue(g, carry):
        for u in range(DMA_UNROLL):
            t = g * DMA_UNROLL + u
            make_copy(t, 0).start(priority=0)
            make_copy(t, 1).start(priority=1)
        return carry

    lax.fori_loop(0, tm // DMA_UNROLL, issue, 0)


def _dispatch_kernel(n, dest_ref, h_ref, xs_ref, sem):
    tm = h_ref.shape[0]
    base = pl.program_id(0) * tm

    def row_copy(t, j):
        dst = dest_ref[j * n + base + t]
        return pltpu.make_async_copy(h_ref.at[pl.ds(t, 1), :], xs_ref.at[pl.ds(dst, 1), :], sem)

    _row_dma_burst(tm, row_copy)
    for _ in range(2):
        pltpu.make_async_copy(h_ref, xs_ref.at[pl.ds(0, tm), :], sem).wait()


def _dispatch(h, dest):
    n, d = h.shape
    tm = MOE_TILE
    return pl.pallas_call(
        functools.partial(_dispatch_kernel, n),
        out_shape=jax.ShapeDtypeStruct((2 * n, d), F32),
        grid_spec=pltpu.PrefetchScalarGridSpec(
            num_scalar_prefetch=1, grid=(n // tm,),
            in_specs=[pl.BlockSpec((tm, d), lambda i, dst: (i, 0))],
            out_specs=pl.BlockSpec(memory_space=pl.ANY),
            scratch_shapes=[pltpu.SemaphoreType.DMA]),
        compiler_params=_cparams(("arbitrary",)),
    )(dest, h)


def _gmm_kernel(vt_ref, ve_ref, vlo_ref, vhi_ref, vfirst_ref, nv_ref, xs_ref, wg_ref, wu_ref, wd_ref, ys_ref,
                wgb_ref, wub_ref, wdb_ref):
    del vt_ref
    v = pl.program_id(0)

    @pl.when(v < nv_ref[0])
    def _():
        @pl.when((v == 0) | (ve_ref[v] != ve_ref[jnp.maximum(v - 1, 0)]))
        def _():
            wgb_ref[...] = wg_ref[...].astype(BF16)
            wub_ref[...] = wu_ref[...].astype(BF16)
            wdb_ref[...] = wd_ref[...].astype(BF16)

        x = xs_ref[...].astype(BF16)
        g = _dot(x, wgb_ref[...])
        u = _dot(x, wub_ref[...])
        y = _dot((g * jax.nn.sigmoid(g) * u).astype(BF16), wdb_ref[...])
        row = lax.broadcasted_iota(I32, (y.shape[0], 1), 0)
        mine = (row >= vlo_ref[v]) & (row < vhi_ref[v])

        @pl.when(vfirst_ref[v] == 1)
        def _():
            ys_ref[...] = jnp.where(mine, y, 0.0)

        @pl.when(vfirst_ref[v] == 0)
        def _():
            ys_ref[...] = jnp.where(mine, y, ys_ref[...])


def _gmm(xs, visits, layer, wg, wu, wd):
    rows, d = xs.shape
    tm = MOE_TILE
    hid = wg.shape[-1]
    nvis = rows // tm + MOE_EXPERTS - 1
    tile = lambda v, vt, *_: (vt[v], 0)
    wspec = lambda r, c: pl.BlockSpec((None, None, r, c), lambda v, vt, ve, *_: (layer, ve[v], 0, 0))
    return pl.pallas_call(
        _gmm_kernel,
        out_shape=jax.ShapeDtypeStruct((rows, d), F32),
        grid_spec=pltpu.PrefetchScalarGridSpec(
            num_scalar_prefetch=6, grid=(nvis,),
            in_specs=[pl.BlockSpec((tm, d), tile), wspec(d, hid), wspec(d, hid), wspec(hid, d)],
            out_specs=pl.BlockSpec((tm, d), tile),
            scratch_shapes=[pltpu.VMEM((d, hid), BF16), pltpu.VMEM((d, hid), BF16), pltpu.VMEM((hid, d), BF16)]),
        compiler_params=_cparams(("arbitrary",)),
    )(*visits, xs, wg, wu, wd)


def _combine_kernel(n, final, dest_ref, x_ref, wt_ref, mod_ref, fg_ref, ys_ref, o_ref, buf_ref, sem):
    d = D_MODEL
    tm = x_ref.shape[0]
    base = pl.program_id(0) * tm

    def row_copy(t, j):
        src = dest_ref[j * n + base + t]
        return pltpu.make_async_copy(ys_ref.at[pl.ds(src, 1), :], buf_ref.at[j, pl.ds(t, 1), :], sem)

    _row_dma_burst(tm, row_copy)
    for j in range(2):
        pltpu.make_async_copy(ys_ref.at[pl.ds(0, tm), :], buf_ref.at[j], sem).wait()
    wt = wt_ref[...]
    y = wt[:, 0:1] * buf_ref[0] + wt[:, 1:2] * buf_ref[1]
    out = x_ref[...] + mod_ref[:, 5 * d:6 * d] * y
    if final:
        out = _rms(out, fg_ref[...])
    o_ref[...] = out


def _combine(x2, wt, mod, rows_per_mod, mod_row, fg, ys, dest, final):
    n, d = x2.shape
    tm = MOE_TILE
    if mod_row is None:
        row = lambda i, dst: (i // (rows_per_mod // tm), 0, 0)
    else:
        row = lambda i, dst: (mod_row, 0, 0)
    return pl.pallas_call(
        functools.partial(_combine_kernel, n, final),
        out_shape=jax.ShapeDtypeStruct((n, d), F32),
        grid_spec=pltpu.PrefetchScalarGridSpec(
            num_scalar_prefetch=1, grid=(n // tm,),
            in_specs=[pl.BlockSpec((tm, d), lambda i, dst: (i, 0)), pl.BlockSpec((tm, 8), lambda i, dst: (i, 0)),
                      pl.BlockSpec((None, 1, 6 * d), row), pl.BlockSpec(fg.shape, lambda i, dst: (0, 0)),
                      pl.BlockSpec(memory_space=pl.ANY)],
            out_specs=pl.BlockSpec((tm, d), lambda i, dst: (i, 0)),
            scratch_shapes=[pltpu.VMEM((2, tm, d), F32), pltpu.SemaphoreType.DMA]),
        compiler_params=_cparams(("arbitrary",)),
    )(dest, x2, wt, mod, fg, ys)


def _pick(table, idx):
    hot = idx[..., None] == jnp.arange(table.shape[0], dtype=I32)
    return jnp.sum(jnp.where(hot, table, 0), axis=-1)


def _moe_plan(counts, e, r, rows):
    tm = MOE_TILE
    ends = jnp.cumsum(counts)
    starts = ends - counts
    dest = (_pick(starts, e) + r).reshape(-1)
    first_tile = starts // tm
    nvis = jnp.where(counts > 0, (ends - 1) // tm - first_tile + 1, 0)
    vend = jnp.cumsum(nvis)
    nv = vend[-1:]
    v = jnp.minimum(jnp.arange(rows // tm + MOE_EXPERTS - 1, dtype=I32), nv[0] - 1)
    ve = jnp.sum((vend[None, :] <= v[:, None]).astype(I32), axis=1)
    vt = _pick(first_tile, ve) + v - _pick(vend - nvis, ve)
    vlo = jnp.maximum(_pick(starts, ve) - vt * tm, 0)
    vhi = jnp.minimum(_pick(ends, ve) - vt * tm, tm)
    vfirst = jnp.concatenate([jnp.ones((1,), I32), (vt[1:] != vt[:-1]).astype(I32)])
    return dest, (vt, ve, vlo, vhi, vfirst, nv)


def _moe(x, mod, mod_row, gn, wr, u, layer, wg, wu, wd, fg, final):
    b, n, d = x.shape
    nt = b * n
    x2 = x.reshape(nt, d)
    h, e, wt, r, cnt = _router(x2, mod, n, mod_row, gn, wr, u)
    dest, visits = _moe_plan(cnt[:, 0].astype(I32), e, r, 2 * nt)
    xs = _dispatch(h, dest)
    ys = _gmm(xs, visits, layer, wg, wu, wd)
    out = _combine(x2, wt, mod, n, mod_row, fg, ys, dest, final)
    return out.reshape(b, n, d)


def _rope_tables(rows, dim):
    row = jnp.repeat(jnp.arange(rows, dtype=F32), GRID_W)
    col = jnp.tile(jnp.arange(GRID_W, dtype=F32), rows)
    half = dim // 2
    inv = jnp.power(ROPE_BASE, -jnp.arange(0, half, 2, dtype=F32) / half)
    ar = row[:, None] * inv[None, :]
    ac = col[:, None] * inv[None, :]
    ang = jnp.concatenate([ar, ar, ac, ac], axis=-1)
    return jnp.cos(ang), jnp.sin(ang)


def _even_tables(n, with_rope):
    if with_rope:
        cm, sm = _rope_tables(n // GRID_W, MLA_ROPE)
        cwin, swin = _rope_tables(n // GRID_W, WIN_HEAD_DIM)
    else:
        cm, sm = jnp.ones((n, MLA_ROPE), F32), jnp.zeros((n, MLA_ROPE), F32)
        cwin, swin = jnp.ones((n, WIN_HEAD_DIM), F32), jnp.zeros((n, WIN_HEAD_DIM), F32)
    one, zero = jnp.ones((n, MLA_NOPE), F32), jnp.zeros((n, MLA_NOPE), F32)
    pad = jnp.zeros((n, LANES - MLA_NOPE - MLA_ROPE), F32)
    return (jnp.concatenate([one, cm, pad], axis=1), jnp.concatenate([zero, sm, pad], axis=1),
            jnp.concatenate([cwin, cwin], axis=1), jnp.concatenate([swin, swin], axis=1))


def _even_weights(w_in, qg, w_uq, kvg, w_ukv):
    d = w_in.shape[0]
    o = np.cumsum([0, MLA_Q_RANK, MLA_KV_RANK, MLA_ROPE, 512, 128, 128])
    cq, ckv, kr, qw, kw, vw = [w_in[:, o[i]:o[i + 1]] for i in range(6)]
    z = lambda c: jnp.zeros((d, c), F32)
    kr128 = jnp.concatenate([z(MLA_NOPE), kr, z(LANES - MLA_NOPE - MLA_ROPE)], axis=1)
    dup = lambda t: jnp.concatenate([t[:, 0:64], t[:, 0:64], t[:, 64:128], t[:, 64:128]], axis=1)
    win = jnp.concatenate([cq, ckv, kr128, qw, dup(kw), dup(vw)], axis=1).astype(BF16)
    uq = w_uq.reshape(MLA_Q_RANK, MLA_HEADS, MLA_NOPE + MLA_ROPE)
    uq = jnp.pad(uq, ((0, 0), (0, 0), (0, LANES - MLA_NOPE - MLA_ROPE))).reshape(MLA_Q_RANK, MLA_HEADS * LANES)
    ukv = w_ukv.reshape(MLA_KV_RANK, MLA_HEADS, MLA_NOPE + MLA_V)
    ukk = jnp.pad(ukv[:, :, :MLA_NOPE], ((0, 0), (0, 0), (0, LANES - MLA_NOPE))).reshape(MLA_KV_RANK, MLA_HEADS * LANES)
    ukvv = ukv[:, :, MLA_NOPE:].reshape(MLA_KV_RANK, MLA_HEADS * MLA_V)
    return (win, qg.reshape(1, -1), uq.astype(BF16), kvg.reshape(1, -1), ukk.astype(BF16), ukvv.astype(BF16))


def _odd_weights(w_in, w_g2, b_g, ln_g, ln_b, w_s, b_s):
    d = w_in.shape[0]
    o = np.cumsum([0, GLA_K, GLA_K, GLA_V, 2 * GLA_GATE_RANK, GLA_V, SG_WIDTH, SG_WIDTH])
    q, k, v, g, r, u, vg = [w_in[:, o[i]:o[i + 1]] for i in range(7)]
    g128 = jnp.concatenate([g, jnp.zeros((d, LANES - 2 * GLA_GATE_RANK), F32)], axis=1)
    win = jnp.concatenate([q, k, v, g128, r, u, vg], axis=1).astype(BF16)
    wvt = v.T.astype(BF16)
    zr = jnp.zeros((GLA_GATE_RANK, GLA_K), F32)
    pad = jnp.zeros((LANES - 2 * GLA_GATE_RANK, GLA_K), F32)
    wg = jnp.concatenate([jnp.concatenate([w_g2[0], zr, pad], axis=0),
                          jnp.concatenate([zr, w_g2[1], pad], axis=0)], axis=1)
    bg = b_g.reshape(1, 2 * GLA_K)
    return (win, wvt, wg, bg, ln_g.reshape(1, -1), ln_b.reshape(1, -1), w_s.astype(BF16), b_s.T)


def kernel(x, c, ctx, c_ctx, ada_w, ada_b, norm_mix_g, norm_ffn_g, even_w_in, mla_q_norm_g, mla_w_uq, mla_kv_norm_g, mla_w_ukv, win_sink, even_w_out, odd_w_in, gla_w_g2, gla_b_g, gla_norm_g, sg_ln_g, sg_ln_b, sg_w_s, sg_b_s, odd_w_out, moe_w_rg, moe_w_re, moe_w_gate, moe_w_up, moe_w_down, final_norm_g):
    b, n, d = x.shape
    lc = ctx.shape[1]
    depth = ada_w.shape[0]
    assert depth == 2 and d == D_MODEL and b < 8
    assert n % 512 == 0 and lc % MOE_TILE == 0 and n % GRID_W == 0
    tm = 512 if n % 512 == 0 else 256
    tq = 256

    cond8 = jnp.concatenate([c, c_ctx[None, :], jnp.zeros((8 - b - 1, d), F32)], axis=0)
    mod_all = _adaln(cond8, ada_w, ada_b).reshape(depth, 8, 1, 6 * d)
    ctx_row = b
    u_tri = jnp.asarray(np.triu(np.ones((MOE_TILE, MOE_TILE), np.float32), 1), BF16)
    fg = final_norm_g.reshape(1, d)

    def router_w(layer):
        return jnp.concatenate([moe_w_rg[layer].T, jnp.zeros((8 - MOE_GROUPS, d), F32), moe_w_re[layer].T], axis=0)

    def moe(xx, layer, mod_row, final):
        return _moe(xx, mod_all[layer], mod_row, norm_ffn_g[layer].reshape(1, d), router_w(layer), u_tri,
                    layer, moe_w_gate, moe_w_up, moe_w_down, fg, final)

    mod = mod_all[0]
    gn = norm_mix_g[0].reshape(1, d)
    ew = _even_weights(even_w_in[0], mla_q_norm_g[0], mla_w_uq[0], mla_kv_norm_g[0], mla_w_ukv[0])
    qm_l, km_l, vm_l, qw_l, kw_l, vw_l = _even_in(x, mod, None, gn, ew, _even_tables(n, True), tm)
    qm_c, km_c, vm_c, qw_c, kw_c, vw_c = _even_in(ctx, mod, ctx_row, gn, ew, _even_tables(lc, False), lc)
    w_out = even_w_out[0].astype(BF16)
    sink = win_sink[0]
    oa_l = _mla_attn(qm_l, [(km_l, vm_l), (km_c, vm_c)], tq)
    ob_l = _gqa(qw_l, kw_l, vw_l, kw_c, vw_c, sink, True)
    xl = _even_out(x, oa_l, ob_l, w_out, mod, None, tm)
    oa_c = _mla_attn(qm_c, [(km_c, vm_c)], lc)
    ob_c = _gqa(qw_c, None, None, kw_c, vw_c, sink, False)
    xc = _even_out(ctx, oa_c, ob_c, w_out, mod, ctx_row, lc)
    xl = moe(xl, 0, None, False)
    xc = moe(xc, 0, ctx_row, False)

    mod = mod_all[1]
    gn = norm_mix_g[1].reshape(1, d)
    ow = _odd_weights(odd_w_in[0], gla_w_g2[0], gla_b_g[0], sg_ln_g[0], sg_ln_b[0], sg_w_s[0], sg_b_s[0])
    q_l, k_l, v_l, vt_l, la_l, r_l, dl_l = _odd_in(xl, mod, None, gn, ow, 256)
    q_c, k_c, v_c, vt_c, la_c, _, _ = _odd_in(xc, mod, ctx_row, gn, ow, lc)
    cum_np, pm_np, nlev = _gla_tables()
    cum2 = jnp.asarray(np.concatenate([cum_np, cum_np], axis=-1), BF16)
    pm = jnp.asarray(pm_np, F32)
    s0 = jnp.zeros((b, 2, GLA_HEADS, GLA_DV, LANES), F32)
    _, _, s_ctx = _gla(q_c, k_c, v_c, vt_c, la_c, s0, cum2, pm, nlev)
    o_fwd, o_bwd, _ = _gla(q_l, k_l, v_l, vt_l, la_l, s_ctx, cum2, pm, nlev)
    xl = _odd_out(xl, o_fwd, o_bwd, r_l, dl_l, gla_norm_g[0].reshape(1, -1), odd_w_out[0].astype(BF16), mod, tm)
    return moe(xl, 1, None, True)
```

```python
import functools

import numpy as np
import jax
import jax.numpy as jnp
from jax import lax
from jax.experimental import pallas as pl
from jax.experimental.pallas import tpu as pltpu

F32 = jnp.float32
BF16 = jnp.bfloat16
I32 = jnp.int32

D_MODEL = 1024
GRID_W = 64
EPS = 1e-6
ROPE_BASE = 10000.0
MLA_HEADS = 8
MLA_Q_RANK = 256
MLA_KV_RANK = 128
MLA_NOPE = 64
MLA_ROPE = 32
MLA_V = 64
WIN_HEADS = 8
WIN_KV_HEADS = 2
WIN_HEAD_DIM = 64
WIN_BLOCK = 128
GLA_HEADS = 4
GLA_DK = 64
GLA_DV = 128
GLA_GATE_RANK = 16
GLA_TAU = 16.0
GLA_K = GLA_HEADS * GLA_DK
GLA_V = GLA_HEADS * GLA_DV
SG_GROUPS = 4
SG_CHUNK = 128
SG_WIDTH = 512
MOE_GROUPS = 4
MOE_PER_GROUP = 8
MOE_EXPERTS = 32
MOE_HIDDEN = 512

LANES = 128
GLA_BLOCK = 128
MOE_TILE = 256
MLA_KEY_CHUNK = 512
DMA_UNROLL = 8
NEG = -1e30
LOG2E = 1.4426950408889634
VMEM_LIMIT = 56 * 1024 * 1024


def _cparams(sem):
    return pltpu.CompilerParams(dimension_semantics=sem, vmem_limit_bytes=VMEM_LIMIT)


def _dot(a, b):
    return jnp.dot(a, b, preferred_element_type=F32)


def _dot_nt(a, b):
    return lax.dot_general(a, b, (((1,), (1,)), ((), ())), preferred_element_type=F32)


def _split2(a):
    hi = a.astype(BF16)
    lo = (a - hi.astype(F32)).astype(BF16)
    return hi, lo


def _split3(a):
    hi = a.astype(BF16)
    r = a - hi.astype(F32)
    mid = r.astype(BF16)
    lo = (r - mid.astype(F32)).astype(BF16)
    return hi, mid, lo


def _rms(x, g):
    ms = jnp.mean(x * x, axis=-1, keepdims=True)
    return x * lax.rsqrt(ms + EPS) * g


def _lane_tile(t, reps):
    return t if reps == 1 else jnp.concatenate([t] * reps, axis=1)


def _rope(t, cos, sin, quarter):
    n = t.shape[1]
    lane = lax.broadcasted_iota(I32, t.shape, 1)
    first = (lane & (2 * quarter - 1)) < quarter
    rot = jnp.where(first, -pltpu.roll(t, n - quarter, 1), pltpu.roll(t, quarter, 1))
    return t * cos + rot * sin


def _adaln_kernel(c_ref, w_ref, b_ref, o_ref):
    c = c_ref[...]
    s_hi, s_lo = _split2(c * jax.nn.sigmoid(c))
    w_hi, w_lo = _split2(w_ref[...])
    o_ref[...] = _dot(s_hi, w_hi) + _dot(s_lo, w_hi) + _dot(s_hi, w_lo) + b_ref[...]


def _adaln(cond8, ada_w, ada_b):
    depth, d, n6 = ada_w.shape
    tn = 1536
    return pl.pallas_call(
        _adaln_kernel,
        out_shape=jax.ShapeDtypeStruct((depth, 8, n6), F32),
        grid=(depth, n6 // tn),
        in_specs=[
            pl.BlockSpec((8, d), lambda l, j: (0, 0)),
            pl.BlockSpec((None, d, tn), lambda l, j: (l, 0, j)),
            pl.BlockSpec((None, 1, tn), lambda l, j: (l, 0, j)),
        ],
        out_specs=pl.BlockSpec((None, 8, tn), lambda l, j: (l, 0, j)),
        compiler_params=_cparams(("parallel", "parallel")),
    )(cond8, ada_w, ada_b.reshape(depth, 1, n6))


def _even_in_kernel(x_ref, mod_ref, gn_ref, win_ref, qg_ref, wuq_ref, kvg_ref, wukk_ref, wukv_ref,
                    cq_ref, sq_ref, cw_ref, sw_ref,
                    qm_ref, km_ref, vm_ref, qw_ref, kw_ref, vw_ref):
    d = D_MODEL
    mod = mod_ref[...]
    h = _rms(x_ref[...], gn_ref[...]) * (1.0 + mod[:, d:2 * d]) + mod[:, 0:d]
    z = _dot(h.astype(BF16), win_ref[...])
    cq, sq, cw, sw = cq_ref[...], sq_ref[...], cw_ref[...], sw_ref[...]
    cqn = _rms(z[:, 0:256], qg_ref[...]).astype(BF16)
    q = _dot(cqn, wuq_ref[...])
    q = _rope(q, _lane_tile(cq, 8), _lane_tile(sq, 8), MLA_ROPE // 4)
    qm_ref[...] = (q * (LOG2E * (MLA_NOPE + MLA_ROPE) ** -0.5)).astype(BF16)
    ckvn = _rms(z[:, 256:384], kvg_ref[...]).astype(BF16)
    kn = _dot(ckvn, wukk_ref[...])
    kr = _rope(z[:, 384:512], cq, sq, MLA_ROPE // 4)
    km_ref[...] = (kn + _lane_tile(kr, 8)).astype(BF16)
    vm_ref[...] = _dot_nt(wukv_ref[...], ckvn).astype(BF16)
    qw = _rope(z[:, 512:1024], _lane_tile(cw, 4), _lane_tile(sw, 4), WIN_HEAD_DIM // 4)
    qw_ref[...] = (qw * (WIN_HEAD_DIM ** -0.5)).astype(BF16)
    kw = _rope(z[:, 1024:1280], _lane_tile(cw, 2), _lane_tile(sw, 2), WIN_HEAD_DIM // 4)
    kw_ref[...] = kw.astype(BF16)
    vw_ref[...] = z[:, 1280:1536].astype(BF16)


def _even_in(x, mod, mod_row, gn, wts, tabs, tm):
    b, n, d = x.shape
    win, qg, wuq, kvg, wukk, wukv = wts
    nt = n // tm
    row = (lambda bi, i: (bi, 0, 0)) if mod_row is None else (lambda bi, i: (mod_row, 0, 0))
    full = lambda a: pl.BlockSpec(a.shape, lambda bi, i: (0,) * a.ndim)
    tab = pl.BlockSpec((tm, LANES), lambda bi, i: (i, 0))
    outw = (1024, 1024, None, 512, 256, 256)
    rowspec = lambda w: pl.BlockSpec((None, tm, w), lambda bi, i: (bi, i, 0))
    colspec = pl.BlockSpec((None, 512, tm), lambda bi, i: (bi, 0, i))
    return pl.pallas_call(
        _even_in_kernel,
        out_shape=[jax.ShapeDtypeStruct((b, 512, n) if w is None else (b, n, w), BF16) for w in outw],
        grid=(b, nt),
        in_specs=[pl.BlockSpec((None, tm, d), lambda bi, i: (bi, i, 0)),
                  pl.BlockSpec((None, 1, 6 * d), row),
                  full(gn), full(win), full(qg), full(wuq), full(kvg), full(wukk), full(wukv),
                  tab, tab, tab, tab],
        out_specs=[colspec if w is None else rowspec(w) for w in outw],
        compiler_params=_cparams(("parallel", "parallel")),
    )(x, mod, gn, win, qg, wuq, kvg, wukk, wukv, *tabs)


def _mla_attn_kernel(nseg, q_ref, *refs):
    ks, vts = refs[0:2 * nseg:2], refs[1:2 * nseg:2]
    o_ref = refs[2 * nseg]
    s_bufs = refs[2 * nseg + 1:2 * nseg + 3]
    p_bufs = refs[2 * nseg + 3:2 * nseg + 5]
    pieces, base = [], 0
    for k in ks:
        n = k.shape[0]
        pieces += [(k, c0, min(n, c0 + MLA_KEY_CHUNK), base + c0) for c0 in range(0, n, MLA_KEY_CHUNK)]
        base += n

    def score_chunk(h, piece, buf):
        k, c0, c1, g0 = piece
        hs = slice(h * LANES, (h + 1) * LANES)
        half = (c1 - c0) // 2
        maxes = []
        for a in (c0, c0 + half):
            s = _dot_nt(k[a:a + half, hs], q_ref[:, hs])
            buf[g0 + a - c0:g0 + a - c0 + half, :] = s
            maxes.append(jnp.max(s, axis=0, keepdims=True))
        return jnp.maximum(*maxes)

    def prob_chunk(piece, sbuf, pbuf, m):
        _, c0, c1, g0 = piece
        p = jnp.exp2(sbuf[g0:g0 + c1 - c0, :] - m)
        pbuf[g0:g0 + c1 - c0, :] = p.astype(BF16)
        return jnp.sum(p, axis=0, keepdims=True)

    m_next = functools.reduce(jnp.maximum, [score_chunk(0, pc, s_bufs[0]) for pc in pieces])
    outs = []
    for h in range(MLA_HEADS):
        m_cur, maxes, sums = m_next, [], []
        for pc in pieces:
            if h + 1 < MLA_HEADS:
                maxes.append(score_chunk(h + 1, pc, s_bufs[(h + 1) % 2]))
            sums.append(prob_chunk(pc, s_bufs[h % 2], p_bufs[h % 2], m_cur))
        if h + 1 < MLA_HEADS:
            m_next = functools.reduce(jnp.maximum, maxes)
        l = functools.reduce(jnp.add, sums)
        vrows = slice(h * MLA_V, (h + 1) * MLA_V)
        ot, base = None, 0
        for k, vt in zip(ks, vts):
            n = k.shape[0]
            part = _dot(vt[vrows, :], p_bufs[h % 2][base:base + n, :])
            ot = part if ot is None else ot + part
            base += n
        outs.append(ot * (1.0 / l))
    o_ref[...] = jnp.concatenate(outs, axis=0).T.astype(BF16)


def _mla_attn(q, segs, tq):
    b, n, _ = q.shape
    in_specs = [pl.BlockSpec((None, tq, 1024), lambda bi, i: (bi, i, 0))]
    args = [q]
    keys = 0
    for k, vt in segs:
        lk = k.shape[1]
        keys += lk
        in_specs += [pl.BlockSpec((None, lk, 1024), lambda bi, i: (bi, 0, 0)),
                     pl.BlockSpec((None, 512, lk), lambda bi, i: (bi, 0, 0))]
        args += [k, vt]
    return pl.pallas_call(
        functools.partial(_mla_attn_kernel, len(segs)),
        out_shape=jax.ShapeDtypeStruct((b, n, 512), BF16),
        grid=(b, n // tq),
        in_specs=in_specs,
        out_specs=pl.BlockSpec((None, tq, 512), lambda bi, i: (bi, i, 0)),
        scratch_shapes=[pltpu.VMEM((keys, tq), F32), pltpu.VMEM((keys, tq), F32),
                        pltpu.VMEM((keys, tq), BF16), pltpu.VMEM((keys, tq), BF16)],
        compiler_params=_cparams(("parallel", "parallel")),
    )(*args)


def _gqa_kernel(has_win, nb, sink_ref, q_ref, *refs):
    if has_win:
        kp, kc, kn, vp, vc, vn, kx, vx, o_ref = refs
    else:
        kx, vx, o_ref = refs
    tq = q_ref.shape[0]
    i = pl.program_id(1)
    lane = lax.broadcasted_iota(I32, (tq, LANES), 1)
    row2 = lax.broadcasted_iota(I32, (2 * tq, 1), 0)
    half = WIN_HEAD_DIM
    npair = WIN_HEADS // 2
    kcats, vcats = [], []
    for g in range(WIN_KV_HEADS):
        gs = slice(g * LANES, (g + 1) * LANES)
        if has_win:
            kcats.append(jnp.concatenate([kp[:, gs], kc[:, gs], kn[:, gs], kx[:, gs]], axis=0))
            vcats.append(jnp.concatenate([vp[:, gs], vc[:, gs], vn[:, gs], vx[:, gs]], axis=0))
        else:
            kcats.append(kx[:, gs])
            vcats.append(vx[:, gs])
    scores = []
    for j in range(npair):
        qp = q_ref[:, j * LANES:(j + 1) * LANES]
        zero = jnp.zeros_like(qp)
        q2 = jnp.concatenate([jnp.where(lane < half, qp, zero), jnp.where(lane >= half, qp, zero)], axis=0)
        scores.append(_dot_nt(q2, kcats[j // 2]))
    if has_win:
        w = WIN_BLOCK
        r = lax.broadcasted_iota(I32, scores[0].shape, 0) & (tq - 1)
        c = lax.broadcasted_iota(I32, scores[0].shape, 1)
        big = jnp.int32(1 << 20)
        no_prev = jnp.where(i > 0, 0, big)
        no_next = jnp.where(i < nb - 1, 0, big)
        ok_prev = c >= r + no_prev
        ok_next = (c - 2 * w) <= r - no_next
        valid = ((c >= w) | ok_prev) & ((c < 2 * w) | (c >= 3 * w) | ok_next)
    probs, inv = [], []
    for j in range(npair):
        s = jnp.where(valid, scores[j], NEG) if has_win else scores[j]
        sk = jnp.where(row2 < tq, sink_ref[2 * j], sink_ref[2 * j + 1])
        m = jnp.maximum(jnp.max(s, axis=-1, keepdims=True), sk)
        p = jnp.exp(s - m)
        inv.append(1.0 / (jnp.sum(p, axis=-1, keepdims=True) + jnp.exp(sk - m)))
        probs.append(p.astype(BF16))
    for j in range(npair):
        o2 = _dot(probs[j], vcats[j // 2]) * inv[j]
        o_ref[:, j * LANES:(j + 1) * LANES] = jnp.where(lane < half, o2[:tq], o2[tq:]).astype(BF16)


def _gqa(q, k, v, kx, vx, sink, has_win):
    b, n, _ = q.shape
    lc = kx.shape[1]
    smem = pl.BlockSpec(memory_space=pltpu.SMEM)
    ctxs = pl.BlockSpec((None, lc, 256), lambda bi, i: (bi, 0, 0))
    if has_win:
        tq = WIN_BLOCK
        nb = n // tq
        blk = lambda f: pl.BlockSpec((None, tq, 256), f)
        prev = lambda bi, i: (bi, jnp.maximum(i - 1, 0), 0)
        cur = lambda bi, i: (bi, i, 0)
        nxt = lambda bi, i: (bi, jnp.minimum(i + 1, nb - 1), 0)
        in_specs = [smem, pl.BlockSpec((None, tq, 512), cur),
                    blk(prev), blk(cur), blk(nxt), blk(prev), blk(cur), blk(nxt), ctxs, ctxs]
        args = (sink, q, k, k, k, v, v, v, kx, vx)
    else:
        tq, nb = n, 1
        in_specs = [smem, pl.BlockSpec((None, tq, 512), lambda bi, i: (bi, i, 0)), ctxs, ctxs]
        args = (sink, q, kx, vx)
    return pl.pallas_call(
        functools.partial(_gqa_kernel, has_win, nb),
        out_shape=jax.ShapeDtypeStruct((b, n, 512), BF16),
        grid=(b, nb),
        in_specs=in_specs,
        out_specs=pl.BlockSpec((None, tq, 512), lambda bi, i: (bi, i, 0)),
        compiler_params=_cparams(("parallel", "parallel")),
    )(*args)


def _even_out_kernel(x_ref, a_ref, b_ref, w_ref, mod_ref, o_ref):
    d = D_MODEL
    y = _dot(a_ref[...], w_ref[0:512, :]) + _dot(b_ref[...], w_ref[512:1024, :])
    o_ref[...] = x_ref[...] + mod_ref[:, 2 * d:3 * d] * y


def _even_out(x, oa, ob, w, mod, mod_row, tm):
    b, n, d = x.shape
    row = (lambda bi, i: (bi, 0, 0)) if mod_row is None else (lambda bi, i: (mod_row, 0, 0))
    act = lambda wd: pl.BlockSpec((None, tm, wd), lambda bi, i: (bi, i, 0))
    return pl.pallas_call(
        _even_out_kernel,
        out_shape=jax.ShapeDtypeStruct((b, n, d), F32),
        grid=(b, n // tm),
        in_specs=[act(d), act(512), act(512), pl.BlockSpec(w.shape, lambda bi, i: (0, 0)),
                  pl.BlockSpec((None, 1, 6 * d), row)],
        out_specs=act(d),
        compiler_params=_cparams(("parallel", "parallel")),
    )(x, oa, ob, w, mod)


def _log_sigmoid(z):
    return jnp.minimum(z, 0.0) - jnp.log(1.0 + jnp.exp(-jnp.abs(z)))


def _odd_in_kernel(x_ref, mod_ref, gn_ref, win_ref, wvt_ref, wg_ref, bg_ref, lng_ref, lnb_ref, ws_ref, bst_ref,
                   q_ref, k_ref, v_ref, vt_ref, la_ref, r_ref, dl_ref):
    d = D_MODEL
    tm = x_ref.shape[0]
    mod = mod_ref[...]
    h = (_rms(x_ref[...], gn_ref[...]) * (1.0 + mod[:, d:2 * d]) + mod[:, 0:d]).astype(BF16)
    z = _dot(h, win_ref[...])
    q_ref[...] = z[:, 0:256] * (GLA_DK ** -0.5)
    k_ref[...] = z[:, 256:512]
    v_ref[...] = z[:, 512:1024].astype(BF16)
    vt_ref[...] = _dot_nt(wvt_ref[...], h).astype(BF16)
    g_hi, g_lo = _split2(z[:, 1024:1152])
    w_hi, w_lo = _split2(wg_ref[...])
    zg = _dot(g_hi, w_hi) + _dot(g_lo, w_hi) + _dot(g_hi, w_lo) + bg_ref[...]
    la_ref[...] = _log_sigmoid(zg) / GLA_TAU
    r_ref[...] = z[:, 1152:1664]
    u = jax.nn.gelu(z[:, 1664:2176])
    vg = jax.nn.gelu(z[:, 2176:2688])
    mu = jnp.mean(vg, axis=-1, keepdims=True)
    vc = vg - mu
    var = jnp.mean(vc * vc, axis=-1, keepdims=True)
    vn = (vc * lax.rsqrt(var + EPS) * lng_ref[...] + lnb_ref[...]).astype(BF16)
    bst = bst_ref[...]
    for c in range(tm // SG_CHUNK):
        rows = slice(c * SG_CHUNK, (c + 1) * SG_CHUNK)
        parts = []
        for g in range(SG_GROUPS):
            cols = slice(g * LANES, (g + 1) * LANES)
            parts.append(_dot(ws_ref[g], vn[rows, cols]) + bst[:, g:g + 1])
        dl_ref[rows, :] = (u[rows, :] * jnp.concatenate(parts, axis=1)).astype(BF16)


def _odd_in(x, mod, mod_row, gn, wts, tm):
    b, n, d = x.shape
    win, wvt, wg, bg, lng, lnb, ws, bst = wts
    row = (lambda bi, i: (bi, 0, 0)) if mod_row is None else (lambda bi, i: (mod_row, 0, 0))
    full = lambda a: pl.BlockSpec(a.shape, lambda bi, i: (0,) * a.ndim)
    act = lambda wd: pl.BlockSpec((None, tm, wd), lambda bi, i: (bi, i, 0))
    outs = [((b, n, 256), F32, act(256)), ((b, n, 256), F32, act(256)), ((b, n, 512), BF16, act(512)),
            ((b, 512, n), BF16, pl.BlockSpec((None, 512, tm), lambda bi, i: (bi, 0, i))),
            ((b, n, 512), F32, act(512)), ((b, n, 512), F32, act(512)), ((b, n, 512), BF16, act(512))]
    return pl.pallas_call(
        _odd_in_kernel,
        out_shape=[jax.ShapeDtypeStruct(s, t) for s, t, _ in outs],
        grid=(b, n // tm),
        in_specs=[act(d), pl.BlockSpec((None, 1, 6 * d), row), full(gn), full(win), full(wvt), full(wg),
                  full(bg), full(lng), full(lnb), full(ws), full(bst)],
        out_specs=[sp for _, _, sp in outs],
        compiler_params=_cparams(("parallel", "parallel")),
    )(x, mod, gn, win, wvt, wg, bg, lng, lnb, ws, bst)


def _gla_tables():
    c = GLA_BLOCK
    t = np.arange(c)[:, None]
    u = np.arange(c)[None, :]
    levels = [c >> i for i in range(int(np.log2(c)) + 1)]
    cum = np.zeros((2, 2 * len(levels), c, c), np.float32)
    pair = np.zeros((2, len(levels), c, c), np.float32)
    for li, m in enumerate(levels):
        same = (t // m) == (u // m)
        cum[0, 2 * li] = same & (u <= t)
        cum[0, 2 * li + 1] = same & (u > t)
        cum[1, 2 * li] = same & (u >= t)
        cum[1, 2 * li + 1] = same & (u < t)
        if li > 0:
            pair[0, li] = ((t // m) % 2 == 1) & ((u // m) == (t // m) - 1)
            pair[1, li] = ((t // m) % 2 == 0) & ((u // m) == (t // m) + 1)
    pair[:, 0] = np.eye(c, dtype=np.float32)
    return cum.reshape(2, 2 * len(levels) * c, c), pair, len(levels)


def _gla_chain(nlev, q, k, la, v_ref, vt_ref, cum2, pm_ref, st_ref, o_ref):
    c = GLA_BLOCK
    l_hi, l_mid = _split2(la)
    ex = jnp.exp(_dot(cum2, jnp.concatenate([l_hi, l_mid], axis=0)))
    gtot = jnp.exp(jnp.sum(la, axis=0, keepdims=True))
    yield
    qe = [(q * ex[(2 * li) * c:(2 * li + 1) * c]).astype(BF16) for li in range(nlev)]
    ke = [(k * ex[(2 * li + 1) * c:(2 * li + 2) * c]).astype(BF16) for li in range(nlev)]
    qb, kb = q.astype(BF16), k.astype(BF16)
    states = [st_ref[hd] for hd in range(GLA_HEADS)]
    yield
    outs, new_states = [], []
    lane = lax.broadcasted_iota(I32, (c, LANES), 1)
    zero = jnp.zeros((c, LANES), BF16)
    for hd in range(GLA_HEADS):
        ps = slice((hd // 2) * LANES, (hd // 2 + 1) * LANES)
        vs = slice(hd * GLA_DV, (hd + 1) * GLA_DV)
        mine = (lane < GLA_DK) if hd % 2 == 0 else (lane >= GLA_DK)
        pick = lambda t: jnp.where(mine, t[:, ps], zero)
        a = pm_ref[0] * _dot_nt(pick(qb), kb[:, ps])
        for li in range(1, nlev):
            a = a + pm_ref[li] * _dot_nt(pick(qe[li]), ke[li][:, ps])
        outs.append(_dot_nt(qe[0][:, ps], states[hd].astype(BF16)) + _dot(a.astype(BF16), v_ref[:, vs]))
        new_states.append(states[hd] * gtot[:, ps] + _dot(vt_ref[vs, :], pick(ke[0])))
        yield
    o_ref[...] = jnp.concatenate(outs, axis=1)
    for hd in range(GLA_HEADS):
        st_ref[hd] = new_states[hd]
    yield


def _gla_kernel(nlev, nb, *refs):
    ins_f, ins_b = refs[0:5], refs[5:10]
    cum_ref, pm_ref, s0_ref, of_ref, ob_ref, sf_ref = refs[10:16]
    st_refs = refs[16:]
    step = pl.program_id(1)

    @pl.when(step == 0)
    def _():
        for bb in range(nb):
            for d_ in range(2):
                st_refs[2 * bb + d_][...] = s0_ref[bb, d_]

    chains = []
    for bb in range(nb):
        for d_, (ins, o_ref) in enumerate(((ins_f, of_ref), (ins_b, ob_ref))):
            q_ref, k_ref, v_ref, vt_ref, la_ref = ins
            chains.append(_gla_chain(nlev, q_ref[bb], k_ref[bb], la_ref[bb], v_ref.at[bb], vt_ref.at[bb],
                                     cum_ref[d_], pm_ref.at[d_], st_refs[2 * bb + d_], o_ref.at[bb]))
    for _ in range(GLA_HEADS + 3):
        for ch in chains:
            next(ch)
    for bb in range(nb):
        for d_ in range(2):
            sf_ref[bb, d_] = st_refs[2 * bb + d_][...]


def _gla(q, k, v, vt, la, s0, cum2, pm, nlev):
    b, n, _ = q.shape
    c = GLA_BLOCK
    nc = n // c
    nb = 2 if b % 2 == 0 else 1
    specs = []
    for d_ in range(2):
        pos = (lambda s_: s_) if d_ == 0 else (lambda s_: nc - 1 - s_)
        specs += [pl.BlockSpec((nb, c, 256), lambda bi, s_, pos=pos: (bi, pos(s_), 0)),
                  pl.BlockSpec((nb, c, 256), lambda bi, s_, pos=pos: (bi, pos(s_), 0)),
                  pl.BlockSpec((nb, c, 512), lambda bi, s_, pos=pos: (bi, pos(s_), 0)),
                  pl.BlockSpec((nb, 512, c), lambda bi, s_, pos=pos: (bi, 0, pos(s_))),
                  pl.BlockSpec((nb, c, 256), lambda bi, s_, pos=pos, d_=d_: (bi, pos(s_), d_))]
    st_spec = pl.BlockSpec((nb, 2, GLA_HEADS, GLA_DV, LANES), lambda bi, s_: (bi, 0, 0, 0, 0))
    full = lambda a: pl.BlockSpec(a.shape, lambda bi, s_: (0,) * a.ndim)
    return pl.pallas_call(
        functools.partial(_gla_kernel, nlev, nb),
        out_shape=[jax.ShapeDtypeStruct((b, n, GLA_V), F32), jax.ShapeDtypeStruct((b, n, GLA_V), F32),
                   jax.ShapeDtypeStruct((b, 2, GLA_HEADS, GLA_DV, LANES), F32)],
        grid=(b // nb, nc),
        in_specs=specs + [full(cum2), full(pm), st_spec],
        out_specs=[pl.BlockSpec((nb, c, GLA_V), lambda bi, s_: (bi, s_, 0)),
                   pl.BlockSpec((nb, c, GLA_V), lambda bi, s_: (bi, nc - 1 - s_, 0)), st_spec],
        scratch_shapes=[pltpu.VMEM((GLA_HEADS, GLA_DV, LANES), F32) for _ in range(2 * nb)],
        compiler_params=_cparams(("parallel", "arbitrary")),
    )(q, k, v, vt, la, q, k, v, vt, la, cum2, pm, s0)


def _odd_out_kernel(x_ref, of_ref, ob_ref, r_ref, dl_ref, gg_ref, w_ref, mod_ref, o_ref):
    d = D_MODEL
    o = of_ref[...] + ob_ref[...]
    gg = gg_ref[...]
    r = r_ref[...]
    parts = []
    for hd in range(GLA_HEADS):
        vs = slice(hd * GLA_DV, (hd + 1) * GLA_DV)
        oh = o[:, vs]
        parts.append(oh * lax.rsqrt(jnp.mean(oh * oh, axis=-1, keepdims=True) + EPS) * gg[:, vs])
    cl = (jnp.concatenate(parts, axis=1) * (r * jax.nn.sigmoid(r))).astype(BF16)
    y = _dot(cl, w_ref[0:512, :]) + _dot(dl_ref[...], w_ref[512:1024, :])
    o_ref[...] = x_ref[...] + mod_ref[:, 2 * d:3 * d] * y


def _odd_out(x, o_fwd, o_bwd, r, dl, gg, w, mod, tm):
    b, n, d = x.shape
    act = lambda wd: pl.BlockSpec((None, tm, wd), lambda bi, i: (bi, i, 0))
    return pl.pallas_call(
        _odd_out_kernel,
        out_shape=jax.ShapeDtypeStruct((b, n, d), F32),
        grid=(b, n // tm),
        in_specs=[act(d), act(GLA_V), act(GLA_V), act(512), act(512),
                  pl.BlockSpec(gg.shape, lambda bi, i: (0, 0)), pl.BlockSpec(w.shape, lambda bi, i: (0, 0)),
                  pl.BlockSpec((None, 1, 6 * d), lambda bi, i: (bi, 0, 0))],
        out_specs=act(d),
        compiler_params=_cparams(("parallel", "parallel")),
    )(x, o_fwd, o_bwd, r, dl, gg, w, mod)


def _router_kernel(x_ref, mod_ref, gn_ref, wr_ref, u_ref, h_ref, e_ref, wt_ref, r_ref, cnt_ref, carry_ref):
    d = D_MODEL
    tm = x_ref.shape[0]
    i = pl.program_id(0)

    @pl.when(i == 0)
    def _():
        carry_ref[...] = jnp.zeros_like(carry_ref)

    mod = mod_ref[...]
    h = _rms(x_ref[...], gn_ref[...]) * (1.0 + mod[:, 4 * d:5 * d]) + mod[:, 3 * d:4 * d]
    h_ref[...] = h
    h_hi, h_lo = _split2(h)
    w_hi, w_lo = _split2(wr_ref[...])
    lg = _dot_nt(w_hi, h_hi) + _dot_nt(w_lo, h_hi) + _dot_nt(w_hi, h_lo)
    rid = lax.broadcasted_iota(I32, (8, tm), 0)
    gl = jnp.where(rid < MOE_GROUPS, lg[0:8], NEG)
    gmax = jnp.max(gl, axis=0, keepdims=True)
    gsel = jnp.min(jnp.where(gl == gmax, rid, 8), axis=0, keepdims=True)
    pmax = 1.0 / jnp.sum(jnp.where(rid < MOE_GROUPS, jnp.exp(gl - gmax), 0.0), axis=0, keepdims=True)
    e_in = jnp.zeros((MOE_PER_GROUP, tm), F32)
    for g in range(MOE_GROUPS):
        e_in = e_in + jnp.where(gsel == g, lg[8 + 8 * g:16 + 8 * g], 0.0)
    v1 = jnp.max(e_in, axis=0, keepdims=True)
    i1 = jnp.min(jnp.where(e_in == v1, rid, 8), axis=0, keepdims=True)
    e_rest = jnp.where(rid == i1, -jnp.inf, e_in)
    v2 = jnp.max(e_rest, axis=0, keepdims=True)
    i2 = jnp.min(jnp.where(e_rest == v2, rid, 8), axis=0, keepdims=True)
    t = jnp.exp(v2 - v1)
    w1 = pmax / (1.0 + t)
    w2 = pmax * t / (1.0 + t)
    e1 = gsel * MOE_PER_GROUP + i1
    e2 = gsel * MOE_PER_GROUP + i2
    eid = lax.broadcasted_iota(I32, (MOE_EXPERTS, tm), 0)
    oh1 = jnp.where(eid == e1, 1.0, 0.0)
    oh2 = jnp.where(eid == e2, 1.0, 0.0)
    ohs = oh1 + oh2
    base = carry_ref[:, 0:1] + _dot(ohs.astype(BF16), u_ref[...])
    r1 = jnp.sum(oh1 * base, axis=0, keepdims=True)
    r2 = jnp.sum(oh2 * base, axis=0, keepdims=True)
    carry_ref[...] = carry_ref[...] + jnp.sum(ohs, axis=1, keepdims=True)
    cnt_ref[...] = carry_ref[...]
    e_ref[...] = jnp.concatenate([e1, e2], axis=0)
    r_ref[...] = jnp.concatenate([r1, r2], axis=0).astype(I32)
    w8 = jnp.concatenate([w1, w2, jnp.zeros((6, tm), F32)], axis=0)
    wt_ref[...] = w8.T


def _router(x2, mod, rows_per_mod, mod_row, gn, wr, u):
    n, d = x2.shape
    tm = MOE_TILE
    if mod_row is None:
        row = lambda i: (i // (rows_per_mod // tm), 0, 0)
    else:
        row = lambda i: (mod_row, 0, 0)
    return pl.pallas_call(
        _router_kernel,
        out_shape=[jax.ShapeDtypeStruct((n, d), F32), jax.ShapeDtypeStruct((2, n), I32),
                   jax.ShapeDtypeStruct((n, 8), F32), jax.ShapeDtypeStruct((2, n), I32),
                   jax.ShapeDtypeStruct((MOE_EXPERTS, LANES), F32)],
        grid=(n // tm,),
        in_specs=[pl.BlockSpec((tm, d), lambda i: (i, 0)), pl.BlockSpec((None, 1, 6 * d), row),
                  pl.BlockSpec(gn.shape, lambda i: (0, 0)), pl.BlockSpec(wr.shape, lambda i: (0, 0)),
                  pl.BlockSpec(u.shape, lambda i: (0, 0))],
        out_specs=[pl.BlockSpec((tm, d), lambda i: (i, 0)), pl.BlockSpec((2, tm), lambda i: (0, i)),
                   pl.BlockSpec((tm, 8), lambda i: (i, 0)), pl.BlockSpec((2, tm), lambda i: (0, i)),
                   pl.BlockSpec((MOE_EXPERTS, LANES), lambda i: (0, 0))],
        scratch_shapes=[pltpu.VMEM((MOE_EXPERTS, LANES), F32)],
        compiler_params=_cparams(("arbitrary",)),
    )(x2, mod, gn, wr, u)


def _row_dma_burst(tm, make_copy):
    def issue(g, carry):
        for u in range(DMA_UNROLL):
            t = g * DMA_UNROLL + u
            make_copy(t, 0).start(priority=0)
            make_copy(t, 1).start(priority=1)
        return carry

    lax.fori_loop(0, tm // DMA_UNROLL, issue, 0)


def _dispatch_kernel(n, dest_ref, h_ref, xs_ref, sem):
    tm = h_ref.shape[0]
    base = pl.program_id(0) * tm

    def row_copy(t, j):
        dst = dest_ref[j * n + base + t]
        return pltpu.make_async_copy(h_ref.at[pl.ds(t, 1), :], xs_ref.at[pl.ds(dst, 1), :], sem)

    _row_dma_burst(tm, row_copy)
    for _ in range(2):
        pltpu.make_async_copy(h_ref, xs_ref.at[pl.ds(0, tm), :], sem).wait()


def _dispatch(h, dest):
    n, d = h.shape
    tm = MOE_TILE
    return pl.pallas_call(
        functools.partial(_dispatch_kernel, n),
        out_shape=jax.ShapeDtypeStruct((2 * n, d), F32),
        grid_spec=pltpu.PrefetchScalarGridSpec(
            num_scalar_prefetch=1, grid=(n // tm,),
            in_specs=[pl.BlockSpec((tm, d), lambda i, dst: (i, 0))],
            out_specs=pl.BlockSpec(memory_space=pl.ANY),
            scratch_shapes=[pltpu.SemaphoreType.DMA]),
        compiler_params=_cparams(("arbitrary",)),
    )(dest, h)


def _gmm_kernel(vt_ref, ve_ref, vlo_ref, vhi_ref, vfirst_ref, nv_ref, xs_ref, wg_ref, wu_ref, wd_ref, ys_ref,
                wgb_ref, wub_ref, wdb_ref):
    del vt_ref
    v = pl.program_id(0)

    @pl.when(v < nv_ref[0])
    def _():
        @pl.when((v == 0) | (ve_ref[v] != ve_ref[jnp.maximum(v - 1, 0)]))
        def _():
            wgb_ref[...] = wg_ref[...].astype(BF16)
            wub_ref[...] = wu_ref[...].astype(BF16)
            wdb_ref[...] = wd_ref[...].astype(BF16)

        x = xs_ref[...].astype(BF16)
        g = _dot(x, wgb_ref[...])
        u = _dot(x, wub_ref[...])
        y = _dot((g * jax.nn.sigmoid(g) * u).astype(BF16), wdb_ref[...])
        row = lax.broadcasted_iota(I32, (y.shape[0], 1), 0)
        mine = (row >= vlo_ref[v]) & (row < vhi_ref[v])

        @pl.when(vfirst_ref[v] == 1)
        def _():
            ys_ref[...] = jnp.where(mine, y, 0.0)

        @pl.when(vfirst_ref[v] == 0)
        def _():
            ys_ref[...] = jnp.where(mine, y, ys_ref[...])


def _gmm(xs, visits, layer, wg, wu, wd):
    rows, d = xs.shape
    tm = MOE_TILE
    hid = wg.shape[-1]
    nvis = rows // tm + MOE_EXPERTS - 1
    tile = lambda v, vt, *_: (vt[v], 0)
    wspec = lambda r, c: pl.BlockSpec((None, None, r, c), lambda v, vt, ve, *_: (layer, ve[v], 0, 0))
    return pl.pallas_call(
        _gmm_kernel,
        out_shape=jax.ShapeDtypeStruct((rows, d), F32),
        grid_spec=pltpu.PrefetchScalarGridSpec(
            num_scalar_prefetch=6, grid=(nvis,),
            in_specs=[pl.BlockSpec((tm, d), tile), wspec(d, hid), wspec(d, hid), wspec(hid, d)],
            out_specs=pl.BlockSpec((tm, d), tile),
            scratch_shapes=[pltpu.VMEM((d, hid), BF16), pltpu.VMEM((d, hid), BF16), pltpu.VMEM((hid, d), BF16)]),
        compiler_params=_cparams(("arbitrary",)),
    )(*visits, xs, wg, wu, wd)


def _combine_kernel(n, final, dest_ref, x_ref, wt_ref, mod_ref, fg_ref, ys_ref, o_ref, buf_ref, sem):
    d = D_MODEL
    tm = x_ref.shape[0]
    step = pl.program_id(0)

    def gather_tile(tile, half):
        def row_copy(t, j):
            src = dest_ref[j * n + tile * tm + t]
            return pltpu.make_async_copy(ys_ref.at[pl.ds(src, 1), :], buf_ref.at[half, j, pl.ds(t, 1), :],
                                         sem.at[half])
        _row_dma_burst(tm, row_copy)

    @pl.when(step == 0)
    def _():
        gather_tile(0, 0)

    @pl.when(step + 1 < pl.num_programs(0))
    def _():
        gather_tile(step + 1, (step + 1) % 2)

    half = step % 2
    for j in range(2):
        pltpu.make_async_copy(ys_ref.at[pl.ds(0, tm), :], buf_ref.at[half, j], sem.at[half]).wait()
    wt = wt_ref[...]
    y = wt[:, 0:1] * buf_ref[half, 0] + wt[:, 1:2] * buf_ref[half, 1]
    out = x_ref[...] + mod_ref[:, 5 * d:6 * d] * y
    if final:
        out = _rms(out, fg_ref[...])
    o_ref[...] = out


def _combine(x2, wt, mod, rows_per_mod, mod_row, fg, ys, dest, final):
    n, d = x2.shape
    tm = MOE_TILE
    if mod_row is None:
        row = lambda i, dst: (i // (rows_per_mod // tm), 0, 0)
    else:
        row = lambda i, dst: (mod_row, 0, 0)
    return pl.pallas_call(
        functools.partial(_combine_kernel, n, final),
        out_shape=jax.ShapeDtypeStruct((n, d), F32),
        grid_spec=pltpu.PrefetchScalarGridSpec(
            num_scalar_prefetch=1, grid=(n // tm,),
            in_specs=[pl.BlockSpec((tm, d), lambda i, dst: (i, 0)), pl.BlockSpec((tm, 8), lambda i, dst: (i, 0)),
                      pl.BlockSpec((None, 1, 6 * d), row), pl.BlockSpec(fg.shape, lambda i, dst: (0, 0)),
                      pl.BlockSpec(memory_space=pl.ANY)],
            out_specs=pl.BlockSpec((tm, d), lambda i, dst: (i, 0)),
            scratch_shapes=[pltpu.VMEM((2, 2, tm, d), F32), pltpu.SemaphoreType.DMA((2,))]),
        compiler_params=_cparams(("arbitrary",)),
    )(dest, x2, wt, mod, fg, ys)


def _pick(table, idx):
    hot = idx[..., None] == jnp.arange(table.shape[0], dtype=I32)
    return jnp.sum(jnp.where(hot, table, 0), axis=-1)


def _moe_plan(counts, e, r, rows):
    tm = MOE_TILE
    ends = jnp.cumsum(counts)
    starts = ends - counts
    dest = (_pick(starts, e) + r).reshape(-1)
    first_tile = starts // tm
    nvis = jnp.where(counts > 0, (ends - 1) // tm - first_tile + 1, 0)
    vend = jnp.cumsum(nvis)
    nv = vend[-1:]
    v = jnp.minimum(jnp.arange(rows // tm + MOE_EXPERTS - 1, dtype=I32), nv[0] - 1)
    ve = jnp.sum((vend[None, :] <= v[:, None]).astype(I32), axis=1)
    vt = _pick(first_tile, ve) + v - _pick(vend - nvis, ve)
    vlo = jnp.maximum(_pick(starts, ve) - vt * tm, 0)
    vhi = jnp.minimum(_pick(ends, ve) - vt * tm, tm)
    vfirst = jnp.concatenate([jnp.ones((1,), I32), (vt[1:] != vt[:-1]).astype(I32)])
    return dest, (vt, ve, vlo, vhi, vfirst, nv)


def _moe(x, mod, mod_row, gn, wr, u, layer, wg, wu, wd, fg, final):
    b, n, d = x.shape
    nt = b * n
    x2 = x.reshape(nt, d)
    h, e, wt, r, cnt = _router(x2, mod, n, mod_row, gn, wr, u)
    dest, visits = _moe_plan(cnt[:, 0].astype(I32), e, r, 2 * nt)
    xs = _dispatch(h, dest)
    ys = _gmm(xs, visits, layer, wg, wu, wd)
    out = _combine(x2, wt, mod, n, mod_row, fg, ys, dest, final)
    return out.reshape(b, n, d)


def _rope_tables(rows, dim):
    row = jnp.repeat(jnp.arange(rows, dtype=F32), GRID_W)
    col = jnp.tile(jnp.arange(GRID_W, dtype=F32), rows)
    half = dim // 2
    inv = jnp.power(ROPE_BASE, -jnp.arange(0, half, 2, dtype=F32) / half)
    ar = row[:, None] * inv[None, :]
    ac = col[:, None] * inv[None, :]
    ang = jnp.concatenate([ar, ar, ac, ac], axis=-1)
    return jnp.cos(ang), jnp.sin(ang)


def _even_tables(n, with_rope):
    if with_rope:
        cm, sm = _rope_tables(n // GRID_W, MLA_ROPE)
        cwin, swin = _rope_tables(n // GRID_W, WIN_HEAD_DIM)
    else:
        cm, sm = jnp.ones((n, MLA_ROPE), F32), jnp.zeros((n, MLA_ROPE), F32)
        cwin, swin = jnp.ones((n, WIN_HEAD_DIM), F32), jnp.zeros((n, WIN_HEAD_DIM), F32)
    one, zero = jnp.ones((n, MLA_NOPE), F32), jnp.zeros((n, MLA_NOPE), F32)
    pad = jnp.zeros((n, LANES - MLA_NOPE - MLA_ROPE), F32)
    return (jnp.concatenate([one, cm, pad], axis=1), jnp.concatenate([zero, sm, pad], axis=1),
            jnp.concatenate([cwin, cwin], axis=1), jnp.concatenate([swin, swin], axis=1))


def _even_weights(w_in, qg, w_uq, kvg, w_ukv):
    d = w_in.shape[0]
    o = np.cumsum([0, MLA_Q_RANK, MLA_KV_RANK, MLA_ROPE, 512, 128, 128])
    cq, ckv, kr, qw, kw, vw = [w_in[:, o[i]:o[i + 1]] for i in range(6)]
    z = lambda c: jnp.zeros((d, c), F32)
    kr128 = jnp.concatenate([z(MLA_NOPE), kr, z(LANES - MLA_NOPE - MLA_ROPE)], axis=1)
    dup = lambda t: jnp.concatenate([t[:, 0:64], t[:, 0:64], t[:, 64:128], t[:, 64:128]], axis=1)
    win = jnp.concatenate([cq, ckv, kr128, qw, dup(kw), dup(vw)], axis=1).astype(BF16)
    uq = w_uq.reshape(MLA_Q_RANK, MLA_HEADS, MLA_NOPE + MLA_ROPE)
    uq = jnp.pad(uq, ((0, 0), (0, 0), (0, LANES - MLA_NOPE - MLA_ROPE))).reshape(MLA_Q_RANK, MLA_HEADS * LANES)
    ukv = w_ukv.reshape(MLA_KV_RANK, MLA_HEADS, MLA_NOPE + MLA_V)
    ukk = jnp.pad(ukv[:, :, :MLA_NOPE], ((0, 0), (0, 0), (0, LANES - MLA_NOPE))).reshape(MLA_KV_RANK, MLA_HEADS * LANES)
    ukvv = ukv[:, :, MLA_NOPE:].reshape(MLA_KV_RANK, MLA_HEADS * MLA_V)
    return (win, qg.reshape(1, -1), uq.astype(BF16), kvg.reshape(1, -1), ukk.astype(BF16), ukvv.T.astype(BF16))


def _odd_weights(w_in, w_g2, b_g, ln_g, ln_b, w_s, b_s):
    d = w_in.shape[0]
    o = np.cumsum([0, GLA_K, GLA_K, GLA_V, 2 * GLA_GATE_RANK, GLA_V, SG_WIDTH, SG_WIDTH])
    q, k, v, g, r, u, vg = [w_in[:, o[i]:o[i + 1]] for i in range(7)]
    g128 = jnp.concatenate([g, jnp.zeros((d, LANES - 2 * GLA_GATE_RANK), F32)], axis=1)
    win = jnp.concatenate([q, k, v, g128, r, u, vg], axis=1).astype(BF16)
    wvt = v.T.astype(BF16)
    zr = jnp.zeros((GLA_GATE_RANK, GLA_K), F32)
    pad = jnp.zeros((LANES - 2 * GLA_GATE_RANK, GLA_K), F32)
    wg = jnp.concatenate([jnp.concatenate([w_g2[0], zr, pad], axis=0),
                          jnp.concatenate([zr, w_g2[1], pad], axis=0)], axis=1)
    bg = b_g.reshape(1, 2 * GLA_K)
    return (win, wvt, wg, bg, ln_g.reshape(1, -1), ln_b.reshape(1, -1), w_s.astype(BF16), b_s.T)


def kernel(x, c, ctx, c_ctx, ada_w, ada_b, norm_mix_g, norm_ffn_g, even_w_in, mla_q_norm_g, mla_w_uq, mla_kv_norm_g, mla_w_ukv, win_sink, even_w_out, odd_w_in, gla_w_g2, gla_b_g, gla_norm_g, sg_ln_g, sg_ln_b, sg_w_s, sg_b_s, odd_w_out, moe_w_rg, moe_w_re, moe_w_gate, moe_w_up, moe_w_down, final_norm_g):
    b, n, d = x.shape
    lc = ctx.shape[1]
    depth = ada_w.shape[0]
    assert depth == 2 and d == D_MODEL and b < 8
    assert n % 512 == 0 and lc % MOE_TILE == 0 and n % GRID_W == 0
    tm = 512 if n % 512 == 0 else 256
    tq = 256

    cond8 = jnp.concatenate([c, c_ctx[None, :], jnp.zeros((8 - b - 1, d), F32)], axis=0)
    mod_all = _adaln(cond8, ada_w, ada_b).reshape(depth, 8, 1, 6 * d)
    ctx_row = b
    u_tri = jnp.asarray(np.triu(np.ones((MOE_TILE, MOE_TILE), np.float32), 1), BF16)
    fg = final_norm_g.reshape(1, d)

    def router_w(layer):
        return jnp.concatenate([moe_w_rg[layer].T, jnp.zeros((8 - MOE_GROUPS, d), F32), moe_w_re[layer].T], axis=0)

    def moe(xx, layer, mod_row, final):
        return _moe(xx, mod_all[layer], mod_row, norm_ffn_g[layer].reshape(1, d), router_w(layer), u_tri,
                    layer, moe_w_gate, moe_w_up, moe_w_down, fg, final)

    mod = mod_all[0]
    gn = norm_mix_g[0].reshape(1, d)
    ew = _even_weights(even_w_in[0], mla_q_norm_g[0], mla_w_uq[0], mla_kv_norm_g[0], mla_w_ukv[0])
    qm_l, km_l, vm_l, qw_l, kw_l, vw_l = _even_in(x, mod, None, gn, ew, _even_tables(n, True), tm)
    qm_c, km_c, vm_c, qw_c, kw_c, vw_c = _even_in(ctx, mod, ctx_row, gn, ew, _even_tables(lc, False), lc)
    w_out = even_w_out[0].astype(BF16)
    sink = win_sink[0]
    oa_l = _mla_attn(qm_l, [(km_l, vm_l), (km_c, vm_c)], tq)
    ob_l = _gqa(qw_l, kw_l, vw_l, kw_c, vw_c, sink, True)
    xl = _even_out(x, oa_l, ob_l, w_out, mod, None, tm)
    oa_c = _mla_attn(qm_c, [(km_c, vm_c)], lc)
    ob_c = _gqa(qw_c, None, None, kw_c, vw_c, sink, False)
    xc = _even_out(ctx, oa_c, ob_c, w_out, mod, ctx_row, lc)
    xl = moe(xl, 0, None, False)
    xc = moe(xc, 0, ctx_row, False)

    mod = mod_all[1]
    gn = norm_mix_g[1].reshape(1, d)
    ow = _odd_weights(odd_w_in[0], gla_w_g2[0], gla_b_g[0], sg_ln_g[0], sg_ln_b[0], sg_w_s[0], sg_b_s[0])
    q_l, k_l, v_l, vt_l, la_l, r_l, dl_l = _odd_in(xl, mod, None, gn, ow, 256)
    q_c, k_c, v_c, vt_c, la_c, _, _ = _odd_in(xc, mod, ctx_row, gn, ow, lc)
    cum_np, pm_np, nlev = _gla_tables()
    cum2 = jnp.asarray(np.concatenate([cum_np, cum_np], axis=-1), BF16)
    pm = jnp.asarray(pm_np, F32)
    s0 = jnp.zeros((b, 2, GLA_HEADS, GLA_DV, LANES), F32)
    _, _, s_ctx = _gla(q_c, k_c, v_c, vt_c, la_c, s0, cum2, pm, nlev)
    o_fwd, o_bwd, _ = _gla(q_l, k_l, v_l, vt_l, la_l, s_ctx, cum2, pm, nlev)
    xl = _odd_out(xl, o_fwd, o_bwd, r_l, dl_l, gla_norm_g[0].reshape(1, -1), odd_w_out[0].astype(BF16), mod, tm)
    return moe(xl, 1, None, True)
```

```python
import functools

import numpy as np
import jax
import jax.numpy as jnp
from jax import lax
from jax.experimental import pallas as pl
from jax.experimental.pallas import tpu as pltpu

F32 = jnp.float32
BF16 = jnp.bfloat16
I32 = jnp.int32

D_MODEL = 1024
GRID_W = 64
EPS = 1e-6
ROPE_BASE = 10000.0
MLA_HEADS = 8
MLA_Q_RANK = 256
MLA_KV_RANK = 128
MLA_NOPE = 64
MLA_ROPE = 32
MLA_V = 64
WIN_HEADS = 8
WIN_KV_HEADS = 2
WIN_HEAD_DIM = 64
WIN_BLOCK = 128
GLA_HEADS = 4
GLA_DK = 64
GLA_DV = 128
GLA_GATE_RANK = 16
GLA_TAU = 16.0
GLA_K = GLA_HEADS * GLA_DK
GLA_V = GLA_HEADS * GLA_DV
SG_GROUPS = 4
SG_CHUNK = 128
SG_WIDTH = 512
MOE_GROUPS = 4
MOE_PER_GROUP = 8
MOE_EXPERTS = 32
MOE_HIDDEN = 512

LANES = 128
GLA_BLOCK = 128
MOE_TILE = 256
MLA_KEY_CHUNK = 512
DMA_UNROLL = 8
NEG = -1e30
LOG2E = 1.4426950408889634
VMEM_LIMIT = 56 * 1024 * 1024


def _cparams(sem):
    return pltpu.CompilerParams(dimension_semantics=sem, vmem_limit_bytes=VMEM_LIMIT)


def _dot(a, b):
    return jnp.dot(a, b, preferred_element_type=F32)


def _dot_nt(a, b):
    return lax.dot_general(a, b, (((1,), (1,)), ((), ())), preferred_element_type=F32)


def _split2(a):
    hi = a.astype(BF16)
    lo = (a - hi.astype(F32)).astype(BF16)
    return hi, lo


def _split3(a):
    hi = a.astype(BF16)
    r = a - hi.astype(F32)
    mid = r.astype(BF16)
    lo = (r - mid.astype(F32)).astype(BF16)
    return hi, mid, lo


def _rms(x, g):
    ms = jnp.mean(x * x, axis=-1, keepdims=True)
    return x * lax.rsqrt(ms + EPS) * g


def _lane_tile(t, reps):
    return t if reps == 1 else jnp.concatenate([t] * reps, axis=1)


def _rope(t, cos, sin, quarter):
    n = t.shape[1]
    lane = lax.broadcasted_iota(I32, t.shape, 1)
    first = (lane & (2 * quarter - 1)) < quarter
    rot = jnp.where(first, -pltpu.roll(t, n - quarter, 1), pltpu.roll(t, quarter, 1))
    return t * cos + rot * sin


def _adaln_kernel(c_ref, w_ref, b_ref, o_ref):
    c = c_ref[...]
    s_hi, s_lo = _split2(c * jax.nn.sigmoid(c))
    w_hi, w_lo = _split2(w_ref[...])
    o_ref[...] = _dot(s_hi, w_hi) + _dot(s_lo, w_hi) + _dot(s_hi, w_lo) + b_ref[...]


def _adaln(cond8, ada_w, ada_b):
    depth, d, n6 = ada_w.shape
    tn = 1536
    return pl.pallas_call(
        _adaln_kernel,
        out_shape=jax.ShapeDtypeStruct((depth, 8, n6), F32),
        grid=(depth, n6 // tn),
        in_specs=[
            pl.BlockSpec((8, d), lambda l, j: (0, 0)),
            pl.BlockSpec((None, d, tn), lambda l, j: (l, 0, j)),
            pl.BlockSpec((None, 1, tn), lambda l, j: (l, 0, j)),
        ],
        out_specs=pl.BlockSpec((None, 8, tn), lambda l, j: (l, 0, j)),
        compiler_params=_cparams(("parallel", "parallel")),
    )(cond8, ada_w, ada_b.reshape(depth, 1, n6))


def _even_in_kernel(x_ref, mod_ref, gn_ref, win_ref, qg_ref, wuq_ref, kvg_ref, wukk_ref, wukv_ref,
                    cq_ref, sq_ref, cw_ref, sw_ref,
                    qm_ref, km_ref, vm_ref, qw_ref, kw_ref, vw_ref):
    d = D_MODEL
    mod = mod_ref[...]
    h = _rms(x_ref[...], gn_ref[...]) * (1.0 + mod[:, d:2 * d]) + mod[:, 0:d]
    z = _dot(h.astype(BF16), win_ref[...])
    cq, sq, cw, sw = cq_ref[...], sq_ref[...], cw_ref[...], sw_ref[...]
    cqn = _rms(z[:, 0:256], qg_ref[...]).astype(BF16)
    q = _dot(cqn, wuq_ref[...])
    q = _rope(q, _lane_tile(cq, 8), _lane_tile(sq, 8), MLA_ROPE // 4)
    qm_ref[...] = (q * (LOG2E * (MLA_NOPE + MLA_ROPE) ** -0.5)).astype(BF16)
    ckvn = _rms(z[:, 256:384], kvg_ref[...]).astype(BF16)
    kn = _dot(ckvn, wukk_ref[...])
    kr = _rope(z[:, 384:512], cq, sq, MLA_ROPE // 4)
    km_ref[...] = (kn + _lane_tile(kr, 8)).astype(BF16)
    vm_ref[...] = _dot_nt(wukv_ref[...], ckvn).astype(BF16)
    qw = _rope(z[:, 512:1024], _lane_tile(cw, 4), _lane_tile(sw, 4), WIN_HEAD_DIM // 4)
    qw_ref[...] = (qw * (WIN_HEAD_DIM ** -0.5)).astype(BF16)
    kw = _rope(z[:, 1024:1280], _lane_tile(cw, 2), _lane_tile(sw, 2), WIN_HEAD_DIM // 4)
    kw_ref[...] = kw.astype(BF16)
    vw_ref[...] = z[:, 1280:1536].astype(BF16)


def _even_in(x, mod, mod_row, gn, wts, tabs, tm):
    b, n, d = x.shape
    win, qg, wuq, kvg, wukk, wukv = wts
    nt = n // tm
    row = (lambda bi, i: (bi, 0, 0)) if mod_row is None else (lambda bi, i: (mod_row, 0, 0))
    full = lambda a: pl.BlockSpec(a.shape, lambda bi, i: (0,) * a.ndim)
    tab = pl.BlockSpec((tm, LANES), lambda bi, i: (i, 0))
    outw = (1024, 1024, None, 512, 256, 256)
    rowspec = lambda w: pl.BlockSpec((None, tm, w), lambda bi, i: (bi, i, 0))
    colspec = pl.BlockSpec((None, 512, tm), lambda bi, i: (bi, 0, i))
    return pl.pallas_call(
        _even_in_kernel,
        out_shape=[jax.ShapeDtypeStruct((b, 512, n) if w is None else (b, n, w), BF16) for w in outw],
        grid=(b, nt),
        in_specs=[pl.BlockSpec((None, tm, d), lambda bi, i: (bi, i, 0)),
                  pl.BlockSpec((None, 1, 6 * d), row),
                  full(gn), full(win), full(qg), full(wuq), full(kvg), full(wukk), full(wukv),
                  tab, tab, tab, tab],
        out_specs=[colspec if w is None else rowspec(w) for w in outw],
        compiler_params=_cparams(("parallel", "parallel")),
    )(x, mod, gn, win, qg, wuq, kvg, wukk, wukv, *tabs)


def _mla_attn_kernel(nseg, q_ref, *refs):
    ks, vts = refs[0:2 * nseg:2], refs[1:2 * nseg:2]
    o_ref = refs[2 * nseg]
    s_bufs = refs[2 * nseg + 1:2 * nseg + 3]
    p_bufs = refs[2 * nseg + 3:2 * nseg + 5]
    pieces, base = [], 0
    for k in ks:
        n = k.shape[0]
        pieces += [(k, c0, min(n, c0 + MLA_KEY_CHUNK), base + c0) for c0 in range(0, n, MLA_KEY_CHUNK)]
        base += n

    def score_chunk(h, piece, buf):
        k, c0, c1, g0 = piece
        hs = slice(h * LANES, (h + 1) * LANES)
        half = (c1 - c0) // 2
        maxes = []
        for a in (c0, c0 + half):
            s = _dot_nt(k[a:a + half, hs], q_ref[:, hs])
            buf[g0 + a - c0:g0 + a - c0 + half, :] = s
            maxes.append(jnp.max(s, axis=0, keepdims=True))
        return jnp.maximum(*maxes)

    def prob_chunk(piece, sbuf, pbuf, m):
        _, c0, c1, g0 = piece
        p = jnp.exp2(sbuf[g0:g0 + c1 - c0, :] - m)
        pbuf[g0:g0 + c1 - c0, :] = p.astype(BF16)
        return jnp.sum(p, axis=0, keepdims=True)

    m_next = functools.reduce(jnp.maximum, [score_chunk(0, pc, s_bufs[0]) for pc in pieces])
    outs = []
    for h in range(MLA_HEADS):
        m_cur, maxes, sums = m_next, [], []
        for pc in pieces:
            if h + 1 < MLA_HEADS:
                maxes.append(score_chunk(h + 1, pc, s_bufs[(h + 1) % 2]))
            sums.append(prob_chunk(pc, s_bufs[h % 2], p_bufs[h % 2], m_cur))
        if h + 1 < MLA_HEADS:
            m_next = functools.reduce(jnp.maximum, maxes)
        l = functools.reduce(jnp.add, sums)
        vrows = slice(h * MLA_V, (h + 1) * MLA_V)
        ot, base = None, 0
        for k, vt in zip(ks, vts):
            n = k.shape[0]
            part = _dot(vt[vrows, :], p_bufs[h % 2][base:base + n, :])
            ot = part if ot is None else ot + part
            base += n
        outs.append(ot * (1.0 / l))
    o_ref[...] = jnp.concatenate(outs, axis=0).T.astype(BF16)


def _mla_attn(q, segs, tq):
    b, n, _ = q.shape
    in_specs = [pl.BlockSpec((None, tq, 1024), lambda bi, i: (bi, i, 0))]
    args = [q]
    keys = 0
    for k, vt in segs:
        lk = k.shape[1]
        keys += lk
        in_specs += [pl.BlockSpec((None, lk, 1024), lambda bi, i: (bi, 0, 0)),
                     pl.BlockSpec((None, 512, lk), lambda bi, i: (bi, 0, 0))]
        args += [k, vt]
    return pl.pallas_call(
        functools.partial(_mla_attn_kernel, len(segs)),
        out_shape=jax.ShapeDtypeStruct((b, n, 512), BF16),
        grid=(b, n // tq),
        in_specs=in_specs,
        out_specs=pl.BlockSpec((None, tq, 512), lambda bi, i: (bi, i, 0)),
        scratch_shapes=[pltpu.VMEM((keys, tq), F32), pltpu.VMEM((keys, tq), F32),
                        pltpu.VMEM((keys, tq), BF16), pltpu.VMEM((keys, tq), BF16)],
        compiler_params=_cparams(("parallel", "parallel")),
    )(*args)


def _gqa_kernel(has_win, nb, sink_ref, q_ref, *refs):
    if has_win:
        kp, kc, kn, vp, vc, vn, kx, vx, o_ref = refs
    else:
        kx, vx, o_ref = refs
    tq = q_ref.shape[0]
    i = pl.program_id(1)
    lane = lax.broadcasted_iota(I32, (tq, LANES), 1)
    row2 = lax.broadcasted_iota(I32, (2 * tq, 1), 0)
    half = WIN_HEAD_DIM
    npair = WIN_HEADS // 2
    kcats, vcats = [], []
    for g in range(WIN_KV_HEADS):
        gs = slice(g * LANES, (g + 1) * LANES)
        if has_win:
            kcats.append(jnp.concatenate([kp[:, gs], kc[:, gs], kn[:, gs], kx[:, gs]], axis=0))
            vcats.append(jnp.concatenate([vp[:, gs], vc[:, gs], vn[:, gs], vx[:, gs]], axis=0))
        else:
            kcats.append(kx[:, gs])
            vcats.append(vx[:, gs])
    scores = []
    for j in range(npair):
        qp = q_ref[:, j * LANES:(j + 1) * LANES]
        zero = jnp.zeros_like(qp)
        q2 = jnp.concatenate([jnp.where(lane < half, qp, zero), jnp.where(lane >= half, qp, zero)], axis=0)
        scores.append(_dot_nt(q2, kcats[j // 2]))
    if has_win:
        w = WIN_BLOCK
        r = lax.broadcasted_iota(I32, scores[0].shape, 0) & (tq - 1)
        c = lax.broadcasted_iota(I32, scores[0].shape, 1)
        big = jnp.int32(1 << 20)
        no_prev = jnp.where(i > 0, 0, big)
        no_next = jnp.where(i < nb - 1, 0, big)
        ok_prev = c >= r + no_prev
        ok_next = (c - 2 * w) <= r - no_next
        valid = ((c >= w) | ok_prev) & ((c < 2 * w) | (c >= 3 * w) | ok_next)
    probs, inv = [], []
    for j in range(npair):
        s = jnp.where(valid, scores[j], NEG) if has_win else scores[j]
        sk = jnp.where(row2 < tq, sink_ref[2 * j], sink_ref[2 * j + 1])
        m = jnp.maximum(jnp.max(s, axis=-1, keepdims=True), sk)
        p = jnp.exp(s - m)
        inv.append(1.0 / (jnp.sum(p, axis=-1, keepdims=True) + jnp.exp(sk - m)))
        probs.append(p.astype(BF16))
    for j in range(npair):
        o2 = _dot(probs[j], vcats[j // 2]) * inv[j]
        o_ref[:, j * LANES:(j + 1) * LANES] = jnp.where(lane < half, o2[:tq], o2[tq:]).astype(BF16)


def _gqa(q, k, v, kx, vx, sink, has_win):
    b, n, _ = q.shape
    lc = kx.shape[1]
    smem = pl.BlockSpec(memory_space=pltpu.SMEM)
    ctxs = pl.BlockSpec((None, lc, 256), lambda bi, i: (bi, 0, 0))
    if has_win:
        tq = WIN_BLOCK
        nb = n // tq
        blk = lambda f: pl.BlockSpec((None, tq, 256), f)
        prev = lambda bi, i: (bi, jnp.maximum(i - 1, 0), 0)
        cur = lambda bi, i: (bi, i, 0)
        nxt = lambda bi, i: (bi, jnp.minimum(i + 1, nb - 1), 0)
        in_specs = [smem, pl.BlockSpec((None, tq, 512), cur),
                    blk(prev), blk(cur), blk(nxt), blk(prev), blk(cur), blk(nxt), ctxs, ctxs]
        args = (sink, q, k, k, k, v, v, v, kx, vx)
    else:
        tq, nb = n, 1
        in_specs = [smem, pl.BlockSpec((None, tq, 512), lambda bi, i: (bi, i, 0)), ctxs, ctxs]
        args = (sink, q, kx, vx)
    return pl.pallas_call(
        functools.partial(_gqa_kernel, has_win, nb),
        out_shape=jax.ShapeDtypeStruct((b, n, 512), BF16),
        grid=(b, nb),
        in_specs=in_specs,
        out_specs=pl.BlockSpec((None, tq, 512), lambda bi, i: (bi, i, 0)),
        compiler_params=_cparams(("parallel", "parallel")),
    )(*args)


def _even_out_kernel(x_ref, a_ref, b_ref, w_ref, mod_ref, o_ref):
    d = D_MODEL
    y = _dot(a_ref[...], w_ref[0:512, :]) + _dot(b_ref[...], w_ref[512:1024, :])
    o_ref[...] = x_ref[...] + mod_ref[:, 2 * d:3 * d] * y


def _even_out(x, oa, ob, w, mod, mod_row, tm):
    b, n, d = x.shape
    row = (lambda bi, i: (bi, 0, 0)) if mod_row is None else (lambda bi, i: (mod_row, 0, 0))
    act = lambda wd: pl.BlockSpec((None, tm, wd), lambda bi, i: (bi, i, 0))
    return pl.pallas_call(
        _even_out_kernel,
        out_shape=jax.ShapeDtypeStruct((b, n, d), F32),
        grid=(b, n // tm),
        in_specs=[act(d), act(512), act(512), pl.BlockSpec(w.shape, lambda bi, i: (0, 0)),
                  pl.BlockSpec((None, 1, 6 * d), row)],
        out_specs=act(d),
        compiler_params=_cparams(("parallel", "parallel")),
    )(x, oa, ob, w, mod)


def _log_sigmoid(z):
    return jnp.minimum(z, 0.0) - jnp.log(1.0 + jnp.exp(-jnp.abs(z)))


def _odd_in_kernel(x_ref, mod_ref, gn_ref, win_ref, wg_ref, bg_ref, lng_ref, lnb_ref, ws_ref, bst_ref,
                   q_ref, k_ref, v_ref, la_ref, r_ref, dl_ref):
    d = D_MODEL
    tm = x_ref.shape[0]
    mod = mod_ref[...]
    h = (_rms(x_ref[...], gn_ref[...]) * (1.0 + mod[:, d:2 * d]) + mod[:, 0:d]).astype(BF16)
    z = _dot(h, win_ref[...])
    q_ref[...] = z[:, 0:256] * (GLA_DK ** -0.5)
    k_ref[...] = z[:, 256:512]
    v_ref[...] = z[:, 512:1024].astype(BF16)
    g_hi, g_lo = _split2(z[:, 1024:1152])
    w_hi, w_lo = _split2(wg_ref[...])
    zg = _dot(g_hi, w_hi) + _dot(g_lo, w_hi) + _dot(g_hi, w_lo) + bg_ref[...]
    la_ref[...] = _log_sigmoid(zg) / GLA_TAU
    r_ref[...] = z[:, 1152:1664]
    u = jax.nn.gelu(z[:, 1664:2176])
    vg = jax.nn.gelu(z[:, 2176:2688])
    mu = jnp.mean(vg, axis=-1, keepdims=True)
    vc = vg - mu
    var = jnp.mean(vc * vc, axis=-1, keepdims=True)
    vn = (vc * lax.rsqrt(var + EPS) * lng_ref[...] + lnb_ref[...]).astype(BF16)
    bst = bst_ref[...]
    for c in range(tm // SG_CHUNK):
        rows = slice(c * SG_CHUNK, (c + 1) * SG_CHUNK)
        parts = []
        for g in range(SG_GROUPS):
            cols = slice(g * LANES, (g + 1) * LANES)
            parts.append(_dot(ws_ref[g], vn[rows, cols]) + bst[:, g:g + 1])
        dl_ref[rows, :] = (u[rows, :] * jnp.concatenate(parts, axis=1)).astype(BF16)


def _odd_in(x, mod, mod_row, gn, wts, tm):
    b, n, d = x.shape
    win, wg, bg, lng, lnb, ws, bst = wts
    row = (lambda bi, i: (bi, 0, 0)) if mod_row is None else (lambda bi, i: (mod_row, 0, 0))
    full = lambda a: pl.BlockSpec(a.shape, lambda bi, i: (0,) * a.ndim)
    act = lambda wd: pl.BlockSpec((None, tm, wd), lambda bi, i: (bi, i, 0))
    outs = [((b, n, 256), F32, act(256)), ((b, n, 256), F32, act(256)), ((b, n, 512), BF16, act(512)),
            ((b, n, 512), F32, act(512)), ((b, n, 512), F32, act(512)), ((b, n, 512), BF16, act(512))]
    return pl.pallas_call(
        _odd_in_kernel,
        out_shape=[jax.ShapeDtypeStruct(s, t) for s, t, _ in outs],
        grid=(b, n // tm),
        in_specs=[act(d), pl.BlockSpec((None, 1, 6 * d), row), full(gn), full(win), full(wg),
                  full(bg), full(lng), full(lnb), full(ws), full(bst)],
        out_specs=[sp for _, _, sp in outs],
        compiler_params=_cparams(("parallel", "parallel")),
    )(x, mod, gn, win, wg, bg, lng, lnb, ws, bst)


def _gla_tables():
    c = GLA_BLOCK
    t = np.arange(c)[:, None]
    u = np.arange(c)[None, :]
    levels = [c >> i for i in range(int(np.log2(c)) + 1)]
    cum = np.zeros((2, 2 * len(levels), c, c), np.float32)
    pair = np.zeros((2, len(levels), c, c), np.float32)
    for li, m in enumerate(levels):
        same = (t // m) == (u // m)
        cum[0, 2 * li] = same & (u <= t)
        cum[0, 2 * li + 1] = same & (u > t)
        cum[1, 2 * li] = same & (u >= t)
        cum[1, 2 * li + 1] = same & (u < t)
        if li > 0:
            pair[0, li] = ((t // m) % 2 == 1) & ((u // m) == (t // m) - 1)
            pair[1, li] = ((t // m) % 2 == 0) & ((u // m) == (t // m) + 1)
    pair[:, 0] = np.eye(c, dtype=np.float32)
    nlev = len(levels)
    m1 = cum[:, 0::2].reshape(2, nlev * c, c)
    m2t = np.concatenate([cum[:, 2 * li + 1].transpose(0, 2, 1) for li in range(nlev)], axis=2)
    return np.concatenate([m1, m1], axis=2), np.concatenate([m2t, m2t], axis=1), pair, nlev


def _gla_chain(nlev, q, k, la, v_ref, cumq, cumkt, pm_ref, st_ref, o_ref):
    c = GLA_BLOCK
    lat, kt = la.T, k.T
    l_hi, l_mid = _split2(la)
    t_hi, t_mid = _split2(lat)
    exq = jnp.exp(_dot(cumq, jnp.concatenate([l_hi, l_mid], axis=0)))
    exk = jnp.exp(_dot(jnp.concatenate([t_hi, t_mid], axis=1), cumkt))
    gcol = jnp.exp(jnp.sum(lat, axis=1, keepdims=True))
    yield
    qe = [(q * exq[li * c:(li + 1) * c]).astype(BF16) for li in range(nlev)]
    ke = [(kt * exk[:, li * c:(li + 1) * c]).astype(BF16) for li in range(nlev)]
    qb, kb = q.astype(BF16), kt.astype(BF16)
    states = [st_ref[hd] for hd in range(GLA_HEADS)]
    yield
    outs, new_states = [], []
    lane = lax.broadcasted_iota(I32, (c, LANES), 1)
    srow = lax.broadcasted_iota(I32, (LANES, 1), 0)
    zero = jnp.zeros((c, LANES), BF16)
    for hd in range(GLA_HEADS):
        ps = slice((hd // 2) * LANES, (hd // 2 + 1) * LANES)
        vs = slice(hd * GLA_DV, (hd + 1) * GLA_DV)
        mine = (lane < GLA_DK) if hd % 2 == 0 else (lane >= GLA_DK)
        mine_row = (srow < GLA_DK) if hd % 2 == 0 else (srow >= GLA_DK)
        pick = lambda t: jnp.where(mine, t[:, ps], zero)
        a = pm_ref[0] * _dot(pick(qb), kb[ps, :])
        for li in range(1, nlev):
            a = a + pm_ref[li] * _dot(pick(qe[li]), ke[li][ps, :])
        v_h = v_ref[:, vs]
        outs.append(_dot(qe[0][:, ps], states[hd].astype(BF16)) + _dot(a.astype(BF16), v_h))
        new_states.append(states[hd] * gcol[ps, :] + jnp.where(mine_row, _dot(ke[0][ps, :], v_h), 0.0))
        yield
    o_ref[...] = jnp.concatenate(outs, axis=1)
    for hd in range(GLA_HEADS):
        st_ref[hd] = new_states[hd]
    yield


def _gla_kernel(nlev, nb, *refs):
    ins_f, ins_b = refs[0:4], refs[4:8]
    cumq_ref, cumkt_ref, pm_ref, s0_ref, of_ref, ob_ref, sf_ref = refs[8:15]
    st_refs = refs[15:]
    step = pl.program_id(1)

    @pl.when(step == 0)
    def _():
        for bb in range(nb):
            for d_ in range(2):
                st_refs[2 * bb + d_][...] = s0_ref[bb, d_]

    chains = []
    for bb in range(nb):
        for d_, (ins, o_ref) in enumerate(((ins_f, of_ref), (ins_b, ob_ref))):
            q_ref, k_ref, v_ref, la_ref = ins
            chains.append(_gla_chain(nlev, q_ref[bb], k_ref[bb], la_ref[bb], v_ref.at[bb], cumq_ref[d_],
                                     cumkt_ref[d_], pm_ref.at[d_], st_refs[2 * bb + d_], o_ref.at[bb]))
    for _ in range(GLA_HEADS + 3):
        for ch in chains:
            next(ch)
    for bb in range(nb):
        for d_ in range(2):
            sf_ref[bb, d_] = st_refs[2 * bb + d_][...]


def _gla(q, k, v, la, s0, cumq, cumkt, pm, nlev):
    b, n, _ = q.shape
    c = GLA_BLOCK
    nc = n // c
    nb = 2 if b % 2 == 0 else 1
    specs = []
    for d_ in range(2):
        pos = (lambda s_: s_) if d_ == 0 else (lambda s_: nc - 1 - s_)
        specs += [pl.BlockSpec((nb, c, 256), lambda bi, s_, pos=pos: (bi, pos(s_), 0)),
                  pl.BlockSpec((nb, c, 256), lambda bi, s_, pos=pos: (bi, pos(s_), 0)),
                  pl.BlockSpec((nb, c, 512), lambda bi, s_, pos=pos: (bi, pos(s_), 0)),
                  pl.BlockSpec((nb, c, 256), lambda bi, s_, pos=pos, d_=d_: (bi, pos(s_), d_))]
    st_spec = pl.BlockSpec((nb, 2, GLA_HEADS, GLA_DV, LANES), lambda bi, s_: (bi, 0, 0, 0, 0))
    full = lambda a: pl.BlockSpec(a.shape, lambda bi, s_: (0,) * a.ndim)
    return pl.pallas_call(
        functools.partial(_gla_kernel, nlev, nb),
        out_shape=[jax.ShapeDtypeStruct((b, n, GLA_V), F32), jax.ShapeDtypeStruct((b, n, GLA_V), F32),
                   jax.ShapeDtypeStruct((b, 2, GLA_HEADS, GLA_DV, LANES), F32)],
        grid=(b // nb, nc),
        in_specs=specs + [full(cumq), full(cumkt), full(pm), st_spec],
        out_specs=[pl.BlockSpec((nb, c, GLA_V), lambda bi, s_: (bi, s_, 0)),
                   pl.BlockSpec((nb, c, GLA_V), lambda bi, s_: (bi, nc - 1 - s_, 0)), st_spec],
        scratch_shapes=[pltpu.VMEM((GLA_HEADS, GLA_DV, LANES), F32) for _ in range(2 * nb)],
        compiler_params=_cparams(("parallel", "arbitrary")),
    )(q, k, v, la, q, k, v, la, cumq, cumkt, pm, s0)


def _odd_out_kernel(x_ref, of_ref, ob_ref, r_ref, dl_ref, gg_ref, w_ref, mod_ref, o_ref):
    d = D_MODEL
    o = of_ref[...] + ob_ref[...]
    gg = gg_ref[...]
    r = r_ref[...]
    parts = []
    for hd in range(GLA_HEADS):
        vs = slice(hd * GLA_DV, (hd + 1) * GLA_DV)
        oh = o[:, vs]
        parts.append(oh * lax.rsqrt(jnp.mean(oh * oh, axis=-1, keepdims=True) + EPS) * gg[:, vs])
    cl = (jnp.concatenate(parts, axis=1) * (r * jax.nn.sigmoid(r))).astype(BF16)
    y = _dot(cl, w_ref[0:512, :]) + _dot(dl_ref[...], w_ref[512:1024, :])
    o_ref[...] = x_ref[...] + mod_ref[:, 2 * d:3 * d] * y


def _odd_out(x, o_fwd, o_bwd, r, dl, gg, w, mod, tm):
    b, n, d = x.shape
    act = lambda wd: pl.BlockSpec((None, tm, wd), lambda bi, i: (bi, i, 0))
    return pl.pallas_call(
        _odd_out_kernel,
        out_shape=jax.ShapeDtypeStruct((b, n, d), F32),
        grid=(b, n // tm),
        in_specs=[act(d), act(GLA_V), act(GLA_V), act(512), act(512),
                  pl.BlockSpec(gg.shape, lambda bi, i: (0, 0)), pl.BlockSpec(w.shape, lambda bi, i: (0, 0)),
                  pl.BlockSpec((None, 1, 6 * d), lambda bi, i: (bi, 0, 0))],
        out_specs=act(d),
        compiler_params=_cparams(("parallel", "parallel")),
    )(x, o_fwd, o_bwd, r, dl, gg, w, mod)


def _router_kernel(nlat, x_ref, xc_ref, mod_ref, gn_ref, wr_ref, u_ref, h_ref, e_ref, wt_ref, r_ref, cnt_ref,
                   carry_ref):
    d = D_MODEL
    tm = x_ref.shape[0]
    i = pl.program_id(0)

    @pl.when(i == 0)
    def _():
        carry_ref[...] = jnp.zeros_like(carry_ref)

    mod = mod_ref[...]
    x = x_ref[...] if xc_ref is None else jnp.where(i < nlat, x_ref[...], xc_ref[...])
    h = _rms(x, gn_ref[...]) * (1.0 + mod[:, 4 * d:5 * d]) + mod[:, 3 * d:4 * d]
    h_ref[...] = h
    h_hi, h_lo = _split2(h)
    w_hi, w_lo = _split2(wr_ref[...])
    lg = _dot_nt(w_hi, h_hi) + _dot_nt(w_lo, h_hi) + _dot_nt(w_hi, h_lo)
    rid = lax.broadcasted_iota(I32, (8, tm), 0)
    gl = jnp.where(rid < MOE_GROUPS, lg[0:8], NEG)
    gmax = jnp.max(gl, axis=0, keepdims=True)
    gsel = jnp.min(jnp.where(gl == gmax, rid, 8), axis=0, keepdims=True)
    pmax = 1.0 / jnp.sum(jnp.where(rid < MOE_GROUPS, jnp.exp(gl - gmax), 0.0), axis=0, keepdims=True)
    e_in = jnp.zeros((MOE_PER_GROUP, tm), F32)
    for g in range(MOE_GROUPS):
        e_in = e_in + jnp.where(gsel == g, lg[8 + 8 * g:16 + 8 * g], 0.0)
    v1 = jnp.max(e_in, axis=0, keepdims=True)
    i1 = jnp.min(jnp.where(e_in == v1, rid, 8), axis=0, keepdims=True)
    e_rest = jnp.where(rid == i1, -jnp.inf, e_in)
    v2 = jnp.max(e_rest, axis=0, keepdims=True)
    i2 = jnp.min(jnp.where(e_rest == v2, rid, 8), axis=0, keepdims=True)
    t = jnp.exp(v2 - v1)
    w1 = pmax / (1.0 + t)
    w2 = pmax * t / (1.0 + t)
    e1 = gsel * MOE_PER_GROUP + i1
    e2 = gsel * MOE_PER_GROUP + i2
    eid = lax.broadcasted_iota(I32, (MOE_EXPERTS, tm), 0)
    oh1 = jnp.where(eid == e1, 1.0, 0.0)
    oh2 = jnp.where(eid == e2, 1.0, 0.0)
    ohs = oh1 + oh2
    base = carry_ref[:, 0:1] + _dot(ohs.astype(BF16), u_ref[...])
    r1 = jnp.sum(oh1 * base, axis=0, keepdims=True)
    r2 = jnp.sum(oh2 * base, axis=0, keepdims=True)
    carry_ref[...] = carry_ref[...] + jnp.sum(ohs, axis=1, keepdims=True)
    cnt_ref[...] = carry_ref[...]
    e_ref[...] = jnp.concatenate([e1, e2], axis=0)
    r_ref[...] = jnp.concatenate([r1, r2], axis=0).astype(I32)
    w8 = jnp.concatenate([w1, w2, jnp.zeros((6, tm), F32)], axis=0)
    wt_ref[...] = w8.T


def _stream_specs(x2, xc2, rows_per_mod, ctx_row):
    tm = MOE_TILE
    d = x2.shape[1]
    nlat = x2.shape[0] // tm
    lat = lambda i, *_: (jnp.minimum(i, nlat - 1), 0)
    ctx = lambda i, *_: (jnp.maximum(i - nlat, 0), 0)
    if xc2 is None:
        modrow = lambda i, *_: (i // (rows_per_mod // tm), 0, 0)
    else:
        modrow = lambda i, *_: (jnp.where(i < nlat, i // (rows_per_mod // tm), ctx_row), 0, 0)
    specs = [pl.BlockSpec((tm, d), lat)] + ([] if xc2 is None else [pl.BlockSpec((tm, d), ctx)])
    return nlat, specs, pl.BlockSpec((None, 1, 6 * d), modrow)


def _router(x2, xc2, mod, rows_per_mod, ctx_row, gn, wr, u):
    d = x2.shape[1]
    tm = MOE_TILE
    n = x2.shape[0] + (0 if xc2 is None else xc2.shape[0])
    nlat, xspecs, modspec = _stream_specs(x2, xc2, rows_per_mod, ctx_row)
    body = functools.partial(_router_kernel, nlat)
    if xc2 is None:
        body = lambda x_ref, *rest: _router_kernel(nlat, x_ref, None, *rest)
    return pl.pallas_call(
        body,
        out_shape=[jax.ShapeDtypeStruct((n, d), F32), jax.ShapeDtypeStruct((2, n), I32),
                   jax.ShapeDtypeStruct((n, 8), F32), jax.ShapeDtypeStruct((2, n), I32),
                   jax.ShapeDtypeStruct((MOE_EXPERTS, LANES), F32)],
        grid=(n // tm,),
        in_specs=xspecs + [modspec,
                  pl.BlockSpec(gn.shape, lambda i: (0, 0)), pl.BlockSpec(wr.shape, lambda i: (0, 0)),
                  pl.BlockSpec(u.shape, lambda i: (0, 0))],
        out_specs=[pl.BlockSpec((tm, d), lambda i: (i, 0)), pl.BlockSpec((2, tm), lambda i: (0, i)),
                   pl.BlockSpec((tm, 8), lambda i: (i, 0)), pl.BlockSpec((2, tm), lambda i: (0, i)),
                   pl.BlockSpec((MOE_EXPERTS, LANES), lambda i: (0, 0))],
        scratch_shapes=[pltpu.VMEM((MOE_EXPERTS, LANES), F32)],
        compiler_params=_cparams(("arbitrary",)),
    )(*([x2] if xc2 is None else [x2, xc2]), mod, gn, wr, u)


def _row_dma_burst(tm, make_copy):
    def issue(g, carry):
        for u in range(DMA_UNROLL):
            t = g * DMA_UNROLL + u
            make_copy(t, 0).start(priority=0)
            make_copy(t, 1).start(priority=1)
        return carry

    lax.fori_loop(0, tm // DMA_UNROLL, issue, 0)


def _dispatch_kernel(n, dest_ref, h_ref, xs_ref, sem):
    tm = h_ref.shape[0]
    base = pl.program_id(0) * tm

    def row_copy(t, j):
        dst = dest_ref[j * n + base + t]
        return pltpu.make_async_copy(h_ref.at[pl.ds(t, 1), :], xs_ref.at[pl.ds(dst, 1), :], sem)

    _row_dma_burst(tm, row_copy)
    for _ in range(2):
        pltpu.make_async_copy(h_ref, xs_ref.at[pl.ds(0, tm), :], sem).wait()


def _dispatch(h, dest):
    n, d = h.shape
    tm = MOE_TILE
    return pl.pallas_call(
        functools.partial(_dispatch_kernel, n),
        out_shape=jax.ShapeDtypeStruct((2 * n, d), F32),
        grid_spec=pltpu.PrefetchScalarGridSpec(
            num_scalar_prefetch=1, grid=(n // tm,),
            in_specs=[pl.BlockSpec((tm, d), lambda i, dst: (i, 0))],
            out_specs=pl.BlockSpec(memory_space=pl.ANY),
            scratch_shapes=[pltpu.SemaphoreType.DMA]),
        compiler_params=_cparams(("arbitrary",)),
    )(dest, h)


def _gmm_kernel(vt_ref, ve_ref, vlo_ref, vhi_ref, vfirst_ref, nv_ref, xs_ref, wg_ref, wu_ref, wd_ref, ys_ref,
                wgb_ref, wub_ref, wdb_ref):
    del vt_ref
    v = pl.program_id(0)

    @pl.when(v < nv_ref[0])
    def _():
        @pl.when((v == 0) | (ve_ref[v] != ve_ref[jnp.maximum(v - 1, 0)]))
        def _():
            wgb_ref[...] = wg_ref[...].astype(BF16)
            wub_ref[...] = wu_ref[...].astype(BF16)
            wdb_ref[...] = wd_ref[...].astype(BF16)

        x = xs_ref[...].astype(BF16)
        g = _dot(x, wgb_ref[...])
        u = _dot(x, wub_ref[...])
        y = _dot((g * jax.nn.sigmoid(g) * u).astype(BF16), wdb_ref[...])
        row = lax.broadcasted_iota(I32, (y.shape[0], 1), 0)
        mine = (row >= vlo_ref[v]) & (row < vhi_ref[v])

        @pl.when(vfirst_ref[v] == 1)
        def _():
            ys_ref[...] = jnp.where(mine, y, 0.0)

        @pl.when(vfirst_ref[v] == 0)
        def _():
            ys_ref[...] = jnp.where(mine, y, ys_ref[...])


def _gmm(xs, visits, layer, wg, wu, wd):
    rows, d = xs.shape
    tm = MOE_TILE
    hid = wg.shape[-1]
    nvis = rows // tm + MOE_EXPERTS - 1
    tile = lambda v, vt, *_: (vt[v], 0)
    wspec = lambda r, c: pl.BlockSpec((None, None, r, c), lambda v, vt, ve, *_: (layer, ve[v], 0, 0))
    return pl.pallas_call(
        _gmm_kernel,
        out_shape=jax.ShapeDtypeStruct((rows, d), F32),
        grid_spec=pltpu.PrefetchScalarGridSpec(
            num_scalar_prefetch=6, grid=(nvis,),
            in_specs=[pl.BlockSpec((tm, d), tile), wspec(d, hid), wspec(d, hid), wspec(hid, d)],
            out_specs=pl.BlockSpec((tm, d), tile),
            scratch_shapes=[pltpu.VMEM((d, hid), BF16), pltpu.VMEM((d, hid), BF16), pltpu.VMEM((hid, d), BF16)]),
        compiler_params=_cparams(("arbitrary",)),
    )(*visits, xs, wg, wu, wd)


def _combine_kernel(n, nlat, final, two, dest_ref, *refs):
    if two:
        x_ref, xc_ref, wt_ref, mod_ref, fg_ref, ys_ref, o_ref, oc_ref, buf_ref, sem = refs
    else:
        x_ref, wt_ref, mod_ref, fg_ref, ys_ref, o_ref, buf_ref, sem = refs
    d = D_MODEL
    tm = x_ref.shape[0]
    step = pl.program_id(0)

    def gather_tile(tile, half):
        def row_copy(t, j):
            src = dest_ref[j * n + tile * tm + t]
            return pltpu.make_async_copy(ys_ref.at[pl.ds(src, 1), :], buf_ref.at[half, j, pl.ds(t, 1), :],
                                         sem.at[half])
        _row_dma_burst(tm, row_copy)

    @pl.when(step == 0)
    def _():
        gather_tile(0, 0)

    @pl.when(step + 1 < pl.num_programs(0))
    def _():
        gather_tile(step + 1, (step + 1) % 2)

    half = step % 2
    for j in range(2):
        pltpu.make_async_copy(ys_ref.at[pl.ds(0, tm), :], buf_ref.at[half, j], sem.at[half]).wait()
    wt = wt_ref[...]
    y = wt[:, 0:1] * buf_ref[half, 0] + wt[:, 1:2] * buf_ref[half, 1]
    x = jnp.where(step < nlat, x_ref[...], xc_ref[...]) if two else x_ref[...]
    out = x + mod_ref[:, 5 * d:6 * d] * y
    if final:
        out = _rms(out, fg_ref[...])
    if two:
        @pl.when(step < nlat)
        def _():
            o_ref[...] = out

        @pl.when(step >= nlat)
        def _():
            oc_ref[...] = out
    else:
        o_ref[...] = out


def _combine(x2, xc2, wt, mod, rows_per_mod, ctx_row, fg, ys, dest, final):
    d = x2.shape[1]
    tm = MOE_TILE
    two = xc2 is not None
    n = x2.shape[0] + (xc2.shape[0] if two else 0)
    nlat, xspecs, modspec = _stream_specs(x2, xc2, rows_per_mod, ctx_row)
    out_shape = [jax.ShapeDtypeStruct(a.shape, F32) for a in ([x2, xc2] if two else [x2])]
    return pl.pallas_call(
        functools.partial(_combine_kernel, n, nlat, final, two),
        out_shape=out_shape,
        grid_spec=pltpu.PrefetchScalarGridSpec(
            num_scalar_prefetch=1, grid=(n // tm,),
            in_specs=xspecs + [pl.BlockSpec((tm, 8), lambda i, dst: (i, 0)), modspec,
                               pl.BlockSpec(fg.shape, lambda i, dst: (0, 0)), pl.BlockSpec(memory_space=pl.ANY)],
            out_specs=xspecs,
            scratch_shapes=[pltpu.VMEM((2, 2, tm, d), F32), pltpu.SemaphoreType.DMA((2,))]),
        compiler_params=_cparams(("arbitrary",)),
    )(dest, *([x2, xc2] if two else [x2]), wt, mod, fg, ys)


def _pick(table, idx):
    hot = idx[..., None] == jnp.arange(table.shape[0], dtype=I32)
    return jnp.sum(jnp.where(hot, table, 0), axis=-1)


def _moe_plan(counts, e, r, rows):
    tm = MOE_TILE
    ends = jnp.cumsum(counts)
    starts = ends - counts
    dest = (_pick(starts, e) + r).reshape(-1)
    first_tile = starts // tm
    nvis = jnp.where(counts > 0, (ends - 1) // tm - first_tile + 1, 0)
    vend = jnp.cumsum(nvis)
    nv = vend[-1:]
    v = jnp.minimum(jnp.arange(rows // tm + MOE_EXPERTS - 1, dtype=I32), nv[0] - 1)
    ve = jnp.sum((vend[None, :] <= v[:, None]).astype(I32), axis=1)
    vt = _pick(first_tile, ve) + v - _pick(vend - nvis, ve)
    vlo = jnp.maximum(_pick(starts, ve) - vt * tm, 0)
    vhi = jnp.minimum(_pick(ends, ve) - vt * tm, tm)
    vfirst = jnp.concatenate([jnp.ones((1,), I32), (vt[1:] != vt[:-1]).astype(I32)])
    return dest, (vt, ve, vlo, vhi, vfirst, nv)


def _moe(x, xc, mod, ctx_row, gn, wr, u, layer, wg, wu, wd, fg, final):
    b, n, d = x.shape
    x2 = x.reshape(b * n, d)
    xc2 = None if xc is None else xc.reshape(-1, d)
    nt = x2.shape[0] + (0 if xc is None else xc2.shape[0])
    h, e, wt, r, cnt = _router(x2, xc2, mod, n, ctx_row, gn, wr, u)
    dest, visits = _moe_plan(cnt[:, 0].astype(I32), e, r, 2 * nt)
    xs = _dispatch(h, dest)
    ys = _gmm(xs, visits, layer, wg, wu, wd)
    outs = _combine(x2, xc2, wt, mod, n, ctx_row, fg, ys, dest, final)
    if xc is None:
        return outs[0].reshape(b, n, d), None
    return outs[0].reshape(b, n, d), outs[1].reshape(xc.shape)


def _rope_tables(rows, dim):
    row = jnp.repeat(jnp.arange(rows, dtype=F32), GRID_W)
    col = jnp.tile(jnp.arange(GRID_W, dtype=F32), rows)
    half = dim // 2
    inv = jnp.power(ROPE_BASE, -jnp.arange(0, half, 2, dtype=F32) / half)
    ar = row[:, None] * inv[None, :]
    ac = col[:, None] * inv[None, :]
    ang = jnp.concatenate([ar, ar, ac, ac], axis=-1)
    return jnp.cos(ang), jnp.sin(ang)


def _even_tables(n, with_rope):
    if with_rope:
        cm, sm = _rope_tables(n // GRID_W, MLA_ROPE)
        cwin, swin = _rope_tables(n // GRID_W, WIN_HEAD_DIM)
    else:
        cm, sm = jnp.ones((n, MLA_ROPE), F32), jnp.zeros((n, MLA_ROPE), F32)
        cwin, swin = jnp.ones((n, WIN_HEAD_DIM), F32), jnp.zeros((n, WIN_HEAD_DIM), F32)
    one, zero = jnp.ones((n, MLA_NOPE), F32), jnp.zeros((n, MLA_NOPE), F32)
    pad = jnp.zeros((n, LANES - MLA_NOPE - MLA_ROPE), F32)
    return (jnp.concatenate([one, cm, pad], axis=1), jnp.concatenate([zero, sm, pad], axis=1),
            jnp.concatenate([cwin, cwin], axis=1), jnp.concatenate([swin, swin], axis=1))


def _even_weights(w_in, qg, w_uq, kvg, w_ukv):
    d = w_in.shape[0]
    o = np.cumsum([0, MLA_Q_RANK, MLA_KV_RANK, MLA_ROPE, 512, 128, 128])
    cq, ckv, kr, qw, kw, vw = [w_in[:, o[i]:o[i + 1]] for i in range(6)]
    z = lambda c: jnp.zeros((d, c), F32)
    kr128 = jnp.concatenate([z(MLA_NOPE), kr, z(LANES - MLA_NOPE - MLA_ROPE)], axis=1)
    dup = lambda t: jnp.concatenate([t[:, 0:64], t[:, 0:64], t[:, 64:128], t[:, 64:128]], axis=1)
    win = jnp.concatenate([cq, ckv, kr128, qw, dup(kw), dup(vw)], axis=1).astype(BF16)
    uq = w_uq.reshape(MLA_Q_RANK, MLA_HEADS, MLA_NOPE + MLA_ROPE)
    uq = jnp.pad(uq, ((0, 0), (0, 0), (0, LANES - MLA_NOPE - MLA_ROPE))).reshape(MLA_Q_RANK, MLA_HEADS * LANES)
    ukv = w_ukv.reshape(MLA_KV_RANK, MLA_HEADS, MLA_NOPE + MLA_V)
    ukk = jnp.pad(ukv[:, :, :MLA_NOPE], ((0, 0), (0, 0), (0, LANES - MLA_NOPE))).reshape(MLA_KV_RANK, MLA_HEADS * LANES)
    ukvv = ukv[:, :, MLA_NOPE:].reshape(MLA_KV_RANK, MLA_HEADS * MLA_V)
    return (win, qg.reshape(1, -1), uq.astype(BF16), kvg.reshape(1, -1), ukk.astype(BF16), ukvv.T.astype(BF16))


def _odd_weights(w_in, w_g2, b_g, ln_g, ln_b, w_s, b_s):
    d = w_in.shape[0]
    o = np.cumsum([0, GLA_K, GLA_K, GLA_V, 2 * GLA_GATE_RANK, GLA_V, SG_WIDTH, SG_WIDTH])
    q, k, v, g, r, u, vg = [w_in[:, o[i]:o[i + 1]] for i in range(7)]
    g128 = jnp.concatenate([g, jnp.zeros((d, LANES - 2 * GLA_GATE_RANK), F32)], axis=1)
    win = jnp.concatenate([q, k, v, g128, r, u, vg], axis=1).astype(BF16)
    zr = jnp.zeros((GLA_GATE_RANK, GLA_K), F32)
    pad = jnp.zeros((LANES - 2 * GLA_GATE_RANK, GLA_K), F32)
    wg = jnp.concatenate([jnp.concatenate([w_g2[0], zr, pad], axis=0),
                          jnp.concatenate([zr, w_g2[1], pad], axis=0)], axis=1)
    bg = b_g.reshape(1, 2 * GLA_K)
    return (win, wg, bg, ln_g.reshape(1, -1), ln_b.reshape(1, -1), w_s.astype(BF16), b_s.T)


def kernel(x, c, ctx, c_ctx, ada_w, ada_b, norm_mix_g, norm_ffn_g, even_w_in, mla_q_norm_g, mla_w_uq, mla_kv_norm_g, mla_w_ukv, win_sink, even_w_out, odd_w_in, gla_w_g2, gla_b_g, gla_norm_g, sg_ln_g, sg_ln_b, sg_w_s, sg_b_s, odd_w_out, moe_w_rg, moe_w_re, moe_w_gate, moe_w_up, moe_w_down, final_norm_g):
    b, n, d = x.shape
    lc = ctx.shape[1]
    depth = ada_w.shape[0]
    assert depth == 2 and d == D_MODEL and b < 8
    assert n % 512 == 0 and lc % MOE_TILE == 0 and n % GRID_W == 0
    tm = 512 if n % 512 == 0 else 256
    tq = 256

    cond8 = jnp.concatenate([c, c_ctx[None, :], jnp.zeros((8 - b - 1, d), F32)], axis=0)
    mod_all = _adaln(cond8, ada_w, ada_b).reshape(depth, 8, 1, 6 * d)
    ctx_row = b
    u_tri = jnp.asarray(np.triu(np.ones((MOE_TILE, MOE_TILE), np.float32), 1), BF16)
    fg = final_norm_g.reshape(1, d)

    def router_w(layer):
        return jnp.concatenate([moe_w_rg[layer].T, jnp.zeros((8 - MOE_GROUPS, d), F32), moe_w_re[layer].T], axis=0)

    def moe(xx, xx_ctx, layer, final):
        return _moe(xx, xx_ctx, mod_all[layer], ctx_row, norm_ffn_g[layer].reshape(1, d), router_w(layer), u_tri,
                    layer, moe_w_gate, moe_w_up, moe_w_down, fg, final)

    mod = mod_all[0]
    gn = norm_mix_g[0].reshape(1, d)
    ew = _even_weights(even_w_in[0], mla_q_norm_g[0], mla_w_uq[0], mla_kv_norm_g[0], mla_w_ukv[0])
    qm_l, km_l, vm_l, qw_l, kw_l, vw_l = _even_in(x, mod, None, gn, ew, _even_tables(n, True), tm)
    qm_c, km_c, vm_c, qw_c, kw_c, vw_c = _even_in(ctx, mod, ctx_row, gn, ew, _even_tables(lc, False), lc)
    w_out = even_w_out[0].astype(BF16)
    sink = win_sink[0]
    oa_l = _mla_attn(qm_l, [(km_l, vm_l), (km_c, vm_c)], tq)
    ob_l = _gqa(qw_l, kw_l, vw_l, kw_c, vw_c, sink, True)
    xl = _even_out(x, oa_l, ob_l, w_out, mod, None, tm)
    oa_c = _mla_attn(qm_c, [(km_c, vm_c)], lc)
    ob_c = _gqa(qw_c, None, None, kw_c, vw_c, sink, False)
    xc = _even_out(ctx, oa_c, ob_c, w_out, mod, ctx_row, lc)
    xl, xc = moe(xl, xc, 0, False)

    mod = mod_all[1]
    gn = norm_mix_g[1].reshape(1, d)
    ow = _odd_weights(odd_w_in[0], gla_w_g2[0], gla_b_g[0], sg_ln_g[0], sg_ln_b[0], sg_w_s[0], sg_b_s[0])
    q_l, k_l, v_l, la_l, r_l, dl_l = _odd_in(xl, mod, None, gn, ow, 256)
    q_c, k_c, v_c, la_c, _, _ = _odd_in(xc, mod, ctx_row, gn, ow, lc)
    cumq_np, cumkt_np, pm_np, nlev = _gla_tables()
    cumq, cumkt = jnp.asarray(cumq_np, BF16), jnp.asarray(cumkt_np, BF16)
    pm = jnp.asarray(pm_np, F32)
    s0 = jnp.zeros((b, 2, GLA_HEADS, GLA_DV, LANES), F32)
    _, _, s_ctx = _gla(q_c, k_c, v_c, la_c, s0, cumq, cumkt, pm, nlev)
    o_fwd, o_bwd, _ = _gla(q_l, k_l, v_l, la_l, s_ctx, cumq, cumkt, pm, nlev)
    xl = _odd_out(xl, o_fwd, o_bwd, r_l, dl_l, gla_norm_g[0].reshape(1, -1), odd_w_out[0].astype(BF16), mod, tm)
    return moe(xl, None, 1, True)[0]
```

```python
import functools

import numpy as np
import jax
import jax.numpy as jnp
from jax import lax
from jax.experimental import pallas as pl
from jax.experimental.pallas import tpu as pltpu
from jax.experimental.pallas import tpu_sc as plsc

F32 = jnp.float32
BF16 = jnp.bfloat16
I32 = jnp.int32

D_MODEL = 1024
GRID_W = 64
EPS = 1e-6
ROPE_BASE = 10000.0
MLA_HEADS = 8
MLA_Q_RANK = 256
MLA_KV_RANK = 128
MLA_NOPE = 64
MLA_ROPE = 32
MLA_V = 64
WIN_HEADS = 8
WIN_KV_HEADS = 2
WIN_HEAD_DIM = 64
WIN_BLOCK = 128
GLA_HEADS = 4
GLA_DK = 64
GLA_DV = 128
GLA_GATE_RANK = 16
GLA_TAU = 16.0
GLA_K = GLA_HEADS * GLA_DK
GLA_V = GLA_HEADS * GLA_DV
SG_GROUPS = 4
SG_CHUNK = 128
SG_WIDTH = 512
MOE_GROUPS = 4
MOE_PER_GROUP = 8
MOE_EXPERTS = 32
MOE_HIDDEN = 512

LANES = 128
GLA_BLOCK = 128
MOE_TILE = 256
MLA_KEY_CHUNK = 512
SC_CHUNK = 64
NEG = -1e30
LOG2E = 1.4426950408889634
VMEM_LIMIT = 56 * 1024 * 1024


def _cparams(sem):
    return pltpu.CompilerParams(dimension_semantics=sem, vmem_limit_bytes=VMEM_LIMIT)


def _dot(a, b):
    return jnp.dot(a, b, preferred_element_type=F32)


def _dot_nt(a, b):
    return lax.dot_general(a, b, (((1,), (1,)), ((), ())), preferred_element_type=F32)


def _split2(a):
    hi = a.astype(BF16)
    lo = (a - hi.astype(F32)).astype(BF16)
    return hi, lo


def _split3(a):
    hi = a.astype(BF16)
    r = a - hi.astype(F32)
    mid = r.astype(BF16)
    lo = (r - mid.astype(F32)).astype(BF16)
    return hi, mid, lo


def _rms(x, g):
    ms = jnp.mean(x * x, axis=-1, keepdims=True)
    return x * lax.rsqrt(ms + EPS) * g


def _lane_tile(t, reps):
    return t if reps == 1 else jnp.concatenate([t] * reps, axis=1)


def _rope(t, cos, sin, quarter):
    n = t.shape[1]
    lane = lax.broadcasted_iota(I32, t.shape, 1)
    first = (lane & (2 * quarter - 1)) < quarter
    rot = jnp.where(first, -pltpu.roll(t, n - quarter, 1), pltpu.roll(t, quarter, 1))
    return t * cos + rot * sin


def _adaln_kernel(c_ref, w_ref, b_ref, o_ref):
    c = c_ref[...]
    s_hi, s_lo = _split2(c * jax.nn.sigmoid(c))
    w_hi, w_lo = _split2(w_ref[...])
    o_ref[...] = _dot(s_hi, w_hi) + _dot(s_lo, w_hi) + _dot(s_hi, w_lo) + b_ref[...]


def _adaln(cond8, ada_w, ada_b):
    depth, d, n6 = ada_w.shape
    tn = 1536
    return pl.pallas_call(
        _adaln_kernel,
        out_shape=jax.ShapeDtypeStruct((depth, 8, n6), F32),
        grid=(depth, n6 // tn),
        in_specs=[
            pl.BlockSpec((8, d), lambda l, j: (0, 0)),
            pl.BlockSpec((None, d, tn), lambda l, j: (l, 0, j)),
            pl.BlockSpec((None, 1, tn), lambda l, j: (l, 0, j)),
        ],
        out_specs=pl.BlockSpec((None, 8, tn), lambda l, j: (l, 0, j)),
        compiler_params=_cparams(("parallel", "parallel")),
    )(cond8, ada_w, ada_b.reshape(depth, 1, n6))


def _even_in_kernel(x_ref, mod_ref, gn_ref, win_ref, qg_ref, wuq_ref, kvg_ref, wukk_ref, wukv_ref,
                    cq_ref, sq_ref, cw_ref, sw_ref,
                    qm_ref, km_ref, vm_ref, qw_ref, kw_ref, vw_ref):
    d = D_MODEL
    mod = mod_ref[...]
    h = _rms(x_ref[...], gn_ref[...]) * (1.0 + mod[:, d:2 * d]) + mod[:, 0:d]
    z = _dot(h.astype(BF16), win_ref[...])
    cq, sq, cw, sw = cq_ref[...], sq_ref[...], cw_ref[...], sw_ref[...]
    cqn = _rms(z[:, 0:256], qg_ref[...]).astype(BF16)
    q = _dot(cqn, wuq_ref[...])
    q = _rope(q, _lane_tile(cq, 8), _lane_tile(sq, 8), MLA_ROPE // 4)
    qm_ref[...] = (q * (LOG2E * (MLA_NOPE + MLA_ROPE) ** -0.5)).astype(BF16)
    ckvn = _rms(z[:, 256:384], kvg_ref[...]).astype(BF16)
    kn = _dot(ckvn, wukk_ref[...])
    kr = _rope(z[:, 384:512], cq, sq, MLA_ROPE // 4)
    km_ref[...] = (kn + _lane_tile(kr, 8)).astype(BF16)
    vm_ref[...] = _dot_nt(wukv_ref[...], ckvn).astype(BF16)
    qw = _rope(z[:, 512:1024], _lane_tile(cw, 4), _lane_tile(sw, 4), WIN_HEAD_DIM // 4)
    qw_ref[...] = (qw * (WIN_HEAD_DIM ** -0.5)).astype(BF16)
    kw = _rope(z[:, 1024:1280], _lane_tile(cw, 2), _lane_tile(sw, 2), WIN_HEAD_DIM // 4)
    kw_ref[...] = kw.astype(BF16)
    vw_ref[...] = z[:, 1280:1536].astype(BF16)


def _even_in(x, mod, mod_row, gn, wts, tabs, tm):
    b, n, d = x.shape
    win, qg, wuq, kvg, wukk, wukv = wts
    nt = n // tm
    row = (lambda bi, i: (bi, 0, 0)) if mod_row is None else (lambda bi, i: (mod_row, 0, 0))
    full = lambda a: pl.BlockSpec(a.shape, lambda bi, i: (0,) * a.ndim)
    tab = pl.BlockSpec((tm, LANES), lambda bi, i: (i, 0))
    outw = (1024, 1024, None, 512, 256, 256)
    rowspec = lambda w: pl.BlockSpec((None, tm, w), lambda bi, i: (bi, i, 0))
    colspec = pl.BlockSpec((None, 512, tm), lambda bi, i: (bi, 0, i))
    return pl.pallas_call(
        _even_in_kernel,
        out_shape=[jax.ShapeDtypeStruct((b, 512, n) if w is None else (b, n, w), BF16) for w in outw],
        grid=(b, nt),
        in_specs=[pl.BlockSpec((None, tm, d), lambda bi, i: (bi, i, 0)),
                  pl.BlockSpec((None, 1, 6 * d), row),
                  full(gn), full(win), full(qg), full(wuq), full(kvg), full(wukk), full(wukv),
                  tab, tab, tab, tab],
        out_specs=[colspec if w is None else rowspec(w) for w in outw],
        compiler_params=_cparams(("parallel", "parallel")),
    )(x, mod, gn, win, qg, wuq, kvg, wukk, wukv, *tabs)


def _mla_attn_kernel(nseg, q_ref, *refs):
    ks, vts = refs[0:2 * nseg:2], refs[1:2 * nseg:2]
    o_ref = refs[2 * nseg]
    s_bufs = refs[2 * nseg + 1:2 * nseg + 3]
    p_bufs = refs[2 * nseg + 3:2 * nseg + 5]
    pieces, base = [], 0
    for k in ks:
        n = k.shape[0]
        pieces += [(k, c0, min(n, c0 + MLA_KEY_CHUNK), base + c0) for c0 in range(0, n, MLA_KEY_CHUNK)]
        base += n

    def score_chunk(h, piece, buf):
        k, c0, c1, g0 = piece
        hs = slice(h * LANES, (h + 1) * LANES)
        half = (c1 - c0) // 2
        maxes = []
        for a in (c0, c0 + half):
            s = _dot_nt(k[a:a + half, hs], q_ref[:, hs])
            buf[g0 + a - c0:g0 + a - c0 + half, :] = s
            maxes.append(jnp.max(s, axis=0, keepdims=True))
        return jnp.maximum(*maxes)

    def prob_chunk(piece, sbuf, pbuf, m):
        _, c0, c1, g0 = piece
        p = jnp.exp2(sbuf[g0:g0 + c1 - c0, :] - m)
        pbuf[g0:g0 + c1 - c0, :] = p.astype(BF16)
        return jnp.sum(p, axis=0, keepdims=True)

    m_next = functools.reduce(jnp.maximum, [score_chunk(0, pc, s_bufs[0]) for pc in pieces])
    outs = []
    for h in range(MLA_HEADS):
        m_cur, maxes, sums = m_next, [], []
        for pc in pieces:
            if h + 1 < MLA_HEADS:
                maxes.append(score_chunk(h + 1, pc, s_bufs[(h + 1) % 2]))
            sums.append(prob_chunk(pc, s_bufs[h % 2], p_bufs[h % 2], m_cur))
        if h + 1 < MLA_HEADS:
            m_next = functools.reduce(jnp.maximum, maxes)
        l = functools.reduce(jnp.add, sums)
        vrows = slice(h * MLA_V, (h + 1) * MLA_V)
        ot, base = None, 0
        for k, vt in zip(ks, vts):
            n = k.shape[0]
            part = _dot(vt[vrows, :], p_bufs[h % 2][base:base + n, :])
            ot = part if ot is None else ot + part
            base += n
        outs.append(ot * (1.0 / l))
    o_ref[...] = jnp.concatenate(outs, axis=0).T.astype(BF16)


def _mla_attn(q, segs, tq):
    b, n, _ = q.shape
    in_specs = [pl.BlockSpec((None, tq, 1024), lambda bi, i: (bi, i, 0))]
    args = [q]
    keys = 0
    for k, vt in segs:
        lk = k.shape[1]
        keys += lk
        in_specs += [pl.BlockSpec((None, lk, 1024), lambda bi, i: (bi, 0, 0)),
                     pl.BlockSpec((None, 512, lk), lambda bi, i: (bi, 0, 0))]
        args += [k, vt]
    return pl.pallas_call(
        functools.partial(_mla_attn_kernel, len(segs)),
        out_shape=jax.ShapeDtypeStruct((b, n, 512), BF16),
        grid=(b, n // tq),
        in_specs=in_specs,
        out_specs=pl.BlockSpec((None, tq, 512), lambda bi, i: (bi, i, 0)),
        scratch_shapes=[pltpu.VMEM((keys, tq), F32), pltpu.VMEM((keys, tq), F32),
                        pltpu.VMEM((keys, tq), BF16), pltpu.VMEM((keys, tq), BF16)],
        compiler_params=_cparams(("parallel", "parallel")),
    )(*args)


def _gqa_kernel(has_win, nb, sink_ref, q_ref, *refs):
    if has_win:
        kp, kc, kn, vp, vc, vn, kx, vx, o_ref = refs
    else:
        kx, vx, o_ref = refs
    tq = q_ref.shape[0]
    i = pl.program_id(1)
    lane = lax.broadcasted_iota(I32, (tq, LANES), 1)
    row2 = lax.broadcasted_iota(I32, (2 * tq, 1), 0)
    half = WIN_HEAD_DIM
    npair = WIN_HEADS // 2
    kcats, vcats = [], []
    for g in range(WIN_KV_HEADS):
        gs = slice(g * LANES, (g + 1) * LANES)
        if has_win:
            kcats.append(jnp.concatenate([kp[:, gs], kc[:, gs], kn[:, gs], kx[:, gs]], axis=0))
            vcats.append(jnp.concatenate([vp[:, gs], vc[:, gs], vn[:, gs], vx[:, gs]], axis=0))
        else:
            kcats.append(kx[:, gs])
            vcats.append(vx[:, gs])
    scores = []
    for j in range(npair):
        qp = q_ref[:, j * LANES:(j + 1) * LANES]
        zero = jnp.zeros_like(qp)
        q2 = jnp.concatenate([jnp.where(lane < half, qp, zero), jnp.where(lane >= half, qp, zero)], axis=0)
        scores.append(_dot_nt(q2, kcats[j // 2]))
    if has_win:
        w = WIN_BLOCK
        r = lax.broadcasted_iota(I32, scores[0].shape, 0) & (tq - 1)
        c = lax.broadcasted_iota(I32, scores[0].shape, 1)
        big = jnp.int32(1 << 20)
        no_prev = jnp.where(i > 0, 0, big)
        no_next = jnp.where(i < nb - 1, 0, big)
        ok_prev = c >= r + no_prev
        ok_next = (c - 2 * w) <= r - no_next
        valid = ((c >= w) | ok_prev) & ((c < 2 * w) | (c >= 3 * w) | ok_next)
    probs, inv = [], []
    for j in range(npair):
        s = jnp.where(valid, scores[j], NEG) if has_win else scores[j]
        sk = jnp.where(row2 < tq, sink_ref[2 * j], sink_ref[2 * j + 1])
        m = jnp.maximum(jnp.max(s, axis=-1, keepdims=True), sk)
        p = jnp.exp(s - m)
        inv.append(1.0 / (jnp.sum(p, axis=-1, keepdims=True) + jnp.exp(sk - m)))
        probs.append(p.astype(BF16))
    for j in range(npair):
        o2 = _dot(probs[j], vcats[j // 2]) * inv[j]
        o_ref[:, j * LANES:(j + 1) * LANES] = jnp.where(lane < half, o2[:tq], o2[tq:]).astype(BF16)


def _gqa(q, k, v, kx, vx, sink, has_win):
    b, n, _ = q.shape
    lc = kx.shape[1]
    smem = pl.BlockSpec(memory_space=pltpu.SMEM)
    ctxs = pl.BlockSpec((None, lc, 256), lambda bi, i: (bi, 0, 0))
    if has_win:
        tq = WIN_BLOCK
        nb = n // tq
        blk = lambda f: pl.BlockSpec((None, tq, 256), f)
        prev = lambda bi, i: (bi, jnp.maximum(i - 1, 0), 0)
        cur = lambda bi, i: (bi, i, 0)
        nxt = lambda bi, i: (bi, jnp.minimum(i + 1, nb - 1), 0)
        in_specs = [smem, pl.BlockSpec((None, tq, 512), cur),
                    blk(prev), blk(cur), blk(nxt), blk(prev), blk(cur), blk(nxt), ctxs, ctxs]
        args = (sink, q, k, k, k, v, v, v, kx, vx)
    else:
        tq, nb = n, 1
        in_specs = [smem, pl.BlockSpec((None, tq, 512), lambda bi, i: (bi, i, 0)), ctxs, ctxs]
        args = (sink, q, kx, vx)
    return pl.pallas_call(
        functools.partial(_gqa_kernel, has_win, nb),
        out_shape=jax.ShapeDtypeStruct((b, n, 512), BF16),
        grid=(b, nb),
        in_specs=in_specs,
        out_specs=pl.BlockSpec((None, tq, 512), lambda bi, i: (bi, i, 0)),
        compiler_params=_cparams(("parallel", "parallel")),
    )(*args)


def _even_out_kernel(x_ref, a_ref, b_ref, w_ref, mod_ref, o_ref):
    d = D_MODEL
    y = _dot(a_ref[...], w_ref[0:512, :]) + _dot(b_ref[...], w_ref[512:1024, :])
    o_ref[...] = x_ref[...] + mod_ref[:, 2 * d:3 * d] * y


def _even_out(x, oa, ob, w, mod, mod_row, tm):
    b, n, d = x.shape
    row = (lambda bi, i: (bi, 0, 0)) if mod_row is None else (lambda bi, i: (mod_row, 0, 0))
    act = lambda wd: pl.BlockSpec((None, tm, wd), lambda bi, i: (bi, i, 0))
    return pl.pallas_call(
        _even_out_kernel,
        out_shape=jax.ShapeDtypeStruct((b, n, d), F32),
        grid=(b, n // tm),
        in_specs=[act(d), act(512), act(512), pl.BlockSpec(w.shape, lambda bi, i: (0, 0)),
                  pl.BlockSpec((None, 1, 6 * d), row)],
        out_specs=act(d),
        compiler_params=_cparams(("parallel", "parallel")),
    )(x, oa, ob, w, mod)


def _log_sigmoid(z):
    return jnp.minimum(z, 0.0) - jnp.log(1.0 + jnp.exp(-jnp.abs(z)))


def _odd_in_kernel(x_ref, mod_ref, gn_ref, win_ref, wg_ref, bg_ref, lng_ref, lnb_ref, ws_ref, bst_ref,
                   q_ref, k_ref, v_ref, la_ref, r_ref, dl_ref):
    d = D_MODEL
    tm = x_ref.shape[0]
    mod = mod_ref[...]
    h = (_rms(x_ref[...], gn_ref[...]) * (1.0 + mod[:, d:2 * d]) + mod[:, 0:d]).astype(BF16)
    z = _dot(h, win_ref[...])
    q_ref[...] = z[:, 0:256] * (GLA_DK ** -0.5)
    k_ref[...] = z[:, 256:512]
    v_ref[...] = z[:, 512:1024].astype(BF16)
    g_hi, g_lo = _split2(z[:, 1024:1152])
    w_hi, w_lo = _split2(wg_ref[...])
    zg = _dot(g_hi, w_hi) + _dot(g_lo, w_hi) + _dot(g_hi, w_lo) + bg_ref[...]
    la_ref[...] = _log_sigmoid(zg) / GLA_TAU
    r_ref[...] = z[:, 1152:1664]
    u = jax.nn.gelu(z[:, 1664:2176])
    vg = jax.nn.gelu(z[:, 2176:2688])
    mu = jnp.mean(vg, axis=-1, keepdims=True)
    vc = vg - mu
    var = jnp.mean(vc * vc, axis=-1, keepdims=True)
    vn = (vc * lax.rsqrt(var + EPS) * lng_ref[...] + lnb_ref[...]).astype(BF16)
    bst = bst_ref[...]
    for c in range(tm // SG_CHUNK):
        rows = slice(c * SG_CHUNK, (c + 1) * SG_CHUNK)
        parts = []
        for g in range(SG_GROUPS):
            cols = slice(g * LANES, (g + 1) * LANES)
            parts.append(_dot(ws_ref[g], vn[rows, cols]) + bst[:, g:g + 1])
        dl_ref[rows, :] = (u[rows, :] * jnp.concatenate(parts, axis=1)).astype(BF16)


def _odd_in(x, mod, mod_row, gn, wts, tm):
    b, n, d = x.shape
    win, wg, bg, lng, lnb, ws, bst = wts
    row = (lambda bi, i: (bi, 0, 0)) if mod_row is None else (lambda bi, i: (mod_row, 0, 0))
    full = lambda a: pl.BlockSpec(a.shape, lambda bi, i: (0,) * a.ndim)
    act = lambda wd: pl.BlockSpec((None, tm, wd), lambda bi, i: (bi, i, 0))
    outs = [((b, n, 256), F32, act(256)), ((b, n, 256), F32, act(256)), ((b, n, 512), BF16, act(512)),
            ((b, n, 512), F32, act(512)), ((b, n, 512), F32, act(512)), ((b, n, 512), BF16, act(512))]
    return pl.pallas_call(
        _odd_in_kernel,
        out_shape=[jax.ShapeDtypeStruct(s, t) for s, t, _ in outs],
        grid=(b, n // tm),
        in_specs=[act(d), pl.BlockSpec((None, 1, 6 * d), row), full(gn), full(win), full(wg),
                  full(bg), full(lng), full(lnb), full(ws), full(bst)],
        out_specs=[sp for _, _, sp in outs],
        compiler_params=_cparams(("parallel", "parallel")),
    )(x, mod, gn, win, wg, bg, lng, lnb, ws, bst)


def _gla_tables():
    c = GLA_BLOCK
    t = np.arange(c)[:, None]
    u = np.arange(c)[None, :]
    levels = [c >> i for i in range(int(np.log2(c)) + 1)]
    cum = np.zeros((2, 2 * len(levels), c, c), np.float32)
    pair = np.zeros((2, len(levels), c, c), np.float32)
    for li, m in enumerate(levels):
        same = (t // m) == (u // m)
        cum[0, 2 * li] = same & (u <= t)
        cum[0, 2 * li + 1] = same & (u > t)
        cum[1, 2 * li] = same & (u >= t)
        cum[1, 2 * li + 1] = same & (u < t)
        if li > 0:
            pair[0, li] = ((t // m) % 2 == 1) & ((u // m) == (t // m) - 1)
            pair[1, li] = ((t // m) % 2 == 0) & ((u // m) == (t // m) + 1)
    pair[:, 0] = np.eye(c, dtype=np.float32)
    nlev = len(levels)
    m1 = cum[:, 0::2].reshape(2, nlev * c, c)
    m2t = np.concatenate([cum[:, 2 * li + 1].transpose(0, 2, 1) for li in range(nlev)], axis=2)
    return np.concatenate([m1, m1], axis=2), np.concatenate([m2t, m2t], axis=1), pair, nlev


def _gla_chain(nlev, q, k, la, v_ref, cumq, cumkt, pm_ref, st_ref, o_ref):
    c = GLA_BLOCK
    lat, kt = la.T, k.T
    l_hi, l_mid = _split2(la)
    t_hi, t_mid = _split2(lat)
    exq = jnp.exp(_dot(cumq, jnp.concatenate([l_hi, l_mid], axis=0)))
    exk = jnp.exp(_dot(jnp.concatenate([t_hi, t_mid], axis=1), cumkt))
    gcol = jnp.exp(jnp.sum(lat, axis=1, keepdims=True))
    yield
    qe = [(q * exq[li * c:(li + 1) * c]).astype(BF16) for li in range(nlev)]
    ke = [(kt * exk[:, li * c:(li + 1) * c]).astype(BF16) for li in range(nlev)]
    qb, kb = q.astype(BF16), kt.astype(BF16)
    states = [st_ref[hd] for hd in range(GLA_HEADS)]
    yield
    outs, new_states = [], []
    lane = lax.broadcasted_iota(I32, (c, LANES), 1)
    srow = lax.broadcasted_iota(I32, (LANES, 1), 0)
    zero = jnp.zeros((c, LANES), BF16)
    for hd in range(GLA_HEADS):
        ps = slice((hd // 2) * LANES, (hd // 2 + 1) * LANES)
        vs = slice(hd * GLA_DV, (hd + 1) * GLA_DV)
        mine = (lane < GLA_DK) if hd % 2 == 0 else (lane >= GLA_DK)
        mine_row = (srow < GLA_DK) if hd % 2 == 0 else (srow >= GLA_DK)
        pick = lambda t: jnp.where(mine, t[:, ps], zero)
        a = pm_ref[0] * _dot(pick(qb), kb[ps, :])
        for li in range(1, nlev):
            a = a + pm_ref[li] * _dot(pick(qe[li]), ke[li][ps, :])
        v_h = v_ref[:, vs]
        outs.append(_dot(qe[0][:, ps], states[hd].astype(BF16)) + _dot(a.astype(BF16), v_h))
        new_states.append(states[hd] * gcol[ps, :] + jnp.where(mine_row, _dot(ke[0][ps, :], v_h), 0.0))
        yield
    o_ref[...] = jnp.concatenate(outs, axis=1)
    for hd in range(GLA_HEADS):
        st_ref[hd] = new_states[hd]
    yield


def _gla_kernel(nlev, nb, *refs):
    ins_f, ins_b = refs[0:4], refs[4:8]
    cumq_ref, cumkt_ref, pm_ref, s0_ref, of_ref, ob_ref, sf_ref = refs[8:15]
    st_refs = refs[15:]
    step = pl.program_id(1)

    @pl.when(step == 0)
    def _():
        for bb in range(nb):
            for d_ in range(2):
                st_refs[2 * bb + d_][...] = s0_ref[bb, d_]

    chains = []
    for bb in range(nb):
        for d_, (ins, o_ref) in enumerate(((ins_f, of_ref), (ins_b, ob_ref))):
            q_ref, k_ref, v_ref, la_ref = ins
            chains.append(_gla_chain(nlev, q_ref[bb], k_ref[bb], la_ref[bb], v_ref.at[bb], cumq_ref[d_],
                                     cumkt_ref[d_], pm_ref.at[d_], st_refs[2 * bb + d_], o_ref.at[bb]))
    for _ in range(GLA_HEADS + 3):
        for ch in chains:
            next(ch)
    for bb in range(nb):
        for d_ in range(2):
            sf_ref[bb, d_] = st_refs[2 * bb + d_][...]


def _gla(q, k, v, la, s0, cumq, cumkt, pm, nlev):
    b, n, _ = q.shape
    c = GLA_BLOCK
    nc = n // c
    nb = 2 if b % 2 == 0 else 1
    specs = []
    for d_ in range(2):
        pos = (lambda s_: s_) if d_ == 0 else (lambda s_: nc - 1 - s_)
        specs += [pl.BlockSpec((nb, c, 256), lambda bi, s_, pos=pos: (bi, pos(s_), 0)),
                  pl.BlockSpec((nb, c, 256), lambda bi, s_, pos=pos: (bi, pos(s_), 0)),
                  pl.BlockSpec((nb, c, 512), lambda bi, s_, pos=pos: (bi, pos(s_), 0)),
                  pl.BlockSpec((nb, c, 256), lambda bi, s_, pos=pos, d_=d_: (bi, pos(s_), d_))]
    st_spec = pl.BlockSpec((nb, 2, GLA_HEADS, GLA_DV, LANES), lambda bi, s_: (bi, 0, 0, 0, 0))
    full = lambda a: pl.BlockSpec(a.shape, lambda bi, s_: (0,) * a.ndim)
    return pl.pallas_call(
        functools.partial(_gla_kernel, nlev, nb),
        out_shape=[jax.ShapeDtypeStruct((b, n, GLA_V), F32), jax.ShapeDtypeStruct((b, n, GLA_V), F32),
                   jax.ShapeDtypeStruct((b, 2, GLA_HEADS, GLA_DV, LANES), F32)],
        grid=(b // nb, nc),
        in_specs=specs + [full(cumq), full(cumkt), full(pm), st_spec],
        out_specs=[pl.BlockSpec((nb, c, GLA_V), lambda bi, s_: (bi, s_, 0)),
                   pl.BlockSpec((nb, c, GLA_V), lambda bi, s_: (bi, nc - 1 - s_, 0)), st_spec],
        scratch_shapes=[pltpu.VMEM((GLA_HEADS, GLA_DV, LANES), F32) for _ in range(2 * nb)],
        compiler_params=_cparams(("parallel", "arbitrary")),
    )(q, k, v, la, q, k, v, la, cumq, cumkt, pm, s0)


def _odd_out_kernel(x_ref, of_ref, ob_ref, r_ref, dl_ref, gg_ref, w_ref, mod_ref, o_ref):
    d = D_MODEL
    o = of_ref[...] + ob_ref[...]
    gg = gg_ref[...]
    r = r_ref[...]
    parts = []
    for hd in range(GLA_HEADS):
        vs = slice(hd * GLA_DV, (hd + 1) * GLA_DV)
        oh = o[:, vs]
        parts.append(oh * lax.rsqrt(jnp.mean(oh * oh, axis=-1, keepdims=True) + EPS) * gg[:, vs])
    cl = (jnp.concatenate(parts, axis=1) * (r * jax.nn.sigmoid(r))).astype(BF16)
    y = _dot(cl, w_ref[0:512, :]) + _dot(dl_ref[...], w_ref[512:1024, :])
    o_ref[...] = x_ref[...] + mod_ref[:, 2 * d:3 * d] * y


def _odd_out(x, o_fwd, o_bwd, r, dl, gg, w, mod, tm):
    b, n, d = x.shape
    act = lambda wd: pl.BlockSpec((None, tm, wd), lambda bi, i: (bi, i, 0))
    return pl.pallas_call(
        _odd_out_kernel,
        out_shape=jax.ShapeDtypeStruct((b, n, d), F32),
        grid=(b, n // tm),
        in_specs=[act(d), act(GLA_V), act(GLA_V), act(512), act(512),
                  pl.BlockSpec(gg.shape, lambda bi, i: (0, 0)), pl.BlockSpec(w.shape, lambda bi, i: (0, 0)),
                  pl.BlockSpec((None, 1, 6 * d), lambda bi, i: (bi, 0, 0))],
        out_specs=act(d),
        compiler_params=_cparams(("parallel", "parallel")),
    )(x, o_fwd, o_bwd, r, dl, gg, w, mod)


def _router_kernel(nlat, x_ref, xc_ref, mod_ref, gn_ref, wr_ref, u_ref, h_ref, e_ref, wt_ref, r_ref, cnt_ref,
                   carry_ref):
    d = D_MODEL
    tm = x_ref.shape[0]
    i = pl.program_id(0)

    @pl.when(i == 0)
    def _():
        carry_ref[...] = jnp.zeros_like(carry_ref)

    mod = mod_ref[...]
    x = x_ref[...] if xc_ref is None else jnp.where(i < nlat, x_ref[...], xc_ref[...])
    h = _rms(x, gn_ref[...]) * (1.0 + mod[:, 4 * d:5 * d]) + mod[:, 3 * d:4 * d]
    h_ref[...] = h
    h_hi, h_lo = _split2(h)
    w_hi, w_lo = _split2(wr_ref[...])
    lg = _dot_nt(w_hi, h_hi) + _dot_nt(w_lo, h_hi) + _dot_nt(w_hi, h_lo)
    rid = lax.broadcasted_iota(I32, (8, tm), 0)
    gl = jnp.where(rid < MOE_GROUPS, lg[0:8], NEG)
    gmax = jnp.max(gl, axis=0, keepdims=True)
    gsel = jnp.min(jnp.where(gl == gmax, rid, 8), axis=0, keepdims=True)
    pmax = 1.0 / jnp.sum(jnp.where(rid < MOE_GROUPS, jnp.exp(gl - gmax), 0.0), axis=0, keepdims=True)
    e_in = jnp.zeros((MOE_PER_GROUP, tm), F32)
    for g in range(MOE_GROUPS):
        e_in = e_in + jnp.where(gsel == g, lg[8 + 8 * g:16 + 8 * g], 0.0)
    v1 = jnp.max(e_in, axis=0, keepdims=True)
    i1 = jnp.min(jnp.where(e_in == v1, rid, 8), axis=0, keepdims=True)
    e_rest = jnp.where(rid == i1, -jnp.inf, e_in)
    v2 = jnp.max(e_rest, axis=0, keepdims=True)
    i2 = jnp.min(jnp.where(e_rest == v2, rid, 8), axis=0, keepdims=True)
    t = jnp.exp(v2 - v1)
    w1 = pmax / (1.0 + t)
    w2 = pmax * t / (1.0 + t)
    e1 = gsel * MOE_PER_GROUP + i1
    e2 = gsel * MOE_PER_GROUP + i2
    eid = lax.broadcasted_iota(I32, (MOE_EXPERTS, tm), 0)
    oh1 = jnp.where(eid == e1, 1.0, 0.0)
    oh2 = jnp.where(eid == e2, 1.0, 0.0)
    ohs = oh1 + oh2
    base = carry_ref[:, 0:1] + _dot(ohs.astype(BF16), u_ref[...])
    r1 = jnp.sum(oh1 * base, axis=0, keepdims=True)
    r2 = jnp.sum(oh2 * base, axis=0, keepdims=True)
    carry_ref[...] = carry_ref[...] + jnp.sum(ohs, axis=1, keepdims=True)
    cnt_ref[...] = carry_ref[...]
    e_ref[...] = jnp.concatenate([e1, e2], axis=0)
    r_ref[...] = jnp.concatenate([r1, r2], axis=0).astype(I32)
    w8 = jnp.concatenate([w1, w2, jnp.zeros((6, tm), F32)], axis=0)
    wt_ref[...] = w8.T


def _stream_specs(x2, xc2, rows_per_mod, ctx_row):
    tm = MOE_TILE
    d = x2.shape[1]
    nlat = x2.shape[0] // tm
    lat = lambda i, *_: (jnp.minimum(i, nlat - 1), 0)
    ctx = lambda i, *_: (jnp.maximum(i - nlat, 0), 0)
    if xc2 is None:
        modrow = lambda i, *_: (i // (rows_per_mod // tm), 0, 0)
    else:
        modrow = lambda i, *_: (jnp.where(i < nlat, i // (rows_per_mod // tm), ctx_row), 0, 0)
    specs = [pl.BlockSpec((tm, d), lat)] + ([] if xc2 is None else [pl.BlockSpec((tm, d), ctx)])
    return nlat, specs, pl.BlockSpec((None, 1, 6 * d), modrow)


def _router(x2, xc2, mod, rows_per_mod, ctx_row, gn, wr, u):
    d = x2.shape[1]
    tm = MOE_TILE
    n = x2.shape[0] + (0 if xc2 is None else xc2.shape[0])
    nlat, xspecs, modspec = _stream_specs(x2, xc2, rows_per_mod, ctx_row)
    body = functools.partial(_router_kernel, nlat)
    if xc2 is None:
        body = lambda x_ref, *rest: _router_kernel(nlat, x_ref, None, *rest)
    return pl.pallas_call(
        body,
        out_shape=[jax.ShapeDtypeStruct((n, d), F32), jax.ShapeDtypeStruct((2, n), I32),
                   jax.ShapeDtypeStruct((n, 8), F32), jax.ShapeDtypeStruct((2, n), I32),
                   jax.ShapeDtypeStruct((MOE_EXPERTS, LANES), F32)],
        grid=(n // tm,),
        in_specs=xspecs + [modspec,
                  pl.BlockSpec(gn.shape, lambda i: (0, 0)), pl.BlockSpec(wr.shape, lambda i: (0, 0)),
                  pl.BlockSpec(u.shape, lambda i: (0, 0))],
        out_specs=[pl.BlockSpec((tm, d), lambda i: (i, 0)), pl.BlockSpec((2, tm), lambda i: (0, i)),
                   pl.BlockSpec((tm, 8), lambda i: (i, 0)), pl.BlockSpec((2, tm), lambda i: (0, i)),
                   pl.BlockSpec((MOE_EXPERTS, LANES), lambda i: (0, 0))],
        scratch_shapes=[pltpu.VMEM((MOE_EXPERTS, LANES), F32)],
        compiler_params=_cparams(("arbitrary",)),
    )(*([x2] if xc2 is None else [x2, xc2]), mod, gn, wr, u)


def _sc_permute_rows(src, dest, scatter):
    rows, d = dest.shape[0], src.shape[1]
    n = rows // 2
    info = plsc.get_sparse_core_info()
    workers = info.num_cores * info.num_subcores
    per_worker = rows // workers
    assert rows == per_worker * workers and per_worker % SC_CHUNK == 0 and n % per_worker == 0
    mesh = plsc.VectorSubcoreMesh(core_axis_name="c", subcore_axis_name="s")

    def body(src_hbm, dest_hbm, out_hbm, idx_v, rows_v, sem):
        base = (lax.axis_index("s") * info.num_cores + lax.axis_index("c")) * per_worker

        @pl.loop(0, per_worker // SC_CHUNK)
        def _(j):
            a0 = base + j * SC_CHUNK
            pltpu.sync_copy(dest_hbm.at[pl.ds(a0, SC_CHUNK)], idx_v)
            if scatter:
                t0 = jnp.where(a0 >= n, a0 - n, a0)
                pltpu.sync_copy(src_hbm.at[pl.ds(t0, SC_CHUNK)], rows_v)
                pltpu.async_copy(rows_v, out_hbm.at[idx_v], sem).wait()
            else:
                pltpu.async_copy(src_hbm.at[idx_v], rows_v, sem).wait()
                pltpu.sync_copy(rows_v, out_hbm.at[pl.ds(a0, SC_CHUNK)])

    return pl.kernel(
        body, out_type=jax.ShapeDtypeStruct((rows, d), F32), mesh=mesh,
        scratch_types=[pltpu.VMEM((SC_CHUNK,), I32), pltpu.VMEM((SC_CHUNK, d), F32), pltpu.SemaphoreType.DMA],
    )(src, dest)


def _gmm_kernel(vt_ref, ve_ref, vlo_ref, vhi_ref, vfirst_ref, nv_ref, xs_ref, wg_ref, wu_ref, wd_ref, ys_ref,
                wgb_ref, wub_ref, wdb_ref):
    del vt_ref
    v = pl.program_id(0)

    @pl.when(v < nv_ref[0])
    def _():
        @pl.when((v == 0) | (ve_ref[v] != ve_ref[jnp.maximum(v - 1, 0)]))
        def _():
            wgb_ref[...] = wg_ref[...].astype(BF16)
            wub_ref[...] = wu_ref[...].astype(BF16)
            wdb_ref[...] = wd_ref[...].astype(BF16)

        x = xs_ref[...].astype(BF16)
        g = _dot(x, wgb_ref[...])
        u = _dot(x, wub_ref[...])
        y = _dot((g * jax.nn.sigmoid(g) * u).astype(BF16), wdb_ref[...])
        row = lax.broadcasted_iota(I32, (y.shape[0], 1), 0)
        mine = (row >= vlo_ref[v]) & (row < vhi_ref[v])

        @pl.when(vfirst_ref[v] == 1)
        def _():
            ys_ref[...] = jnp.where(mine, y, 0.0)

        @pl.when(vfirst_ref[v] == 0)
        def _():
            ys_ref[...] = jnp.where(mine, y, ys_ref[...])


def _gmm(xs, visits, layer, wg, wu, wd):
    rows, d = xs.shape
    tm = MOE_TILE
    hid = wg.shape[-1]
    nvis = rows // tm + MOE_EXPERTS - 1
    tile = lambda v, vt, *_: (vt[v], 0)
    wspec = lambda r, c: pl.BlockSpec((None, None, r, c), lambda v, vt, ve, *_: (layer, ve[v], 0, 0))
    return pl.pallas_call(
        _gmm_kernel,
        out_shape=jax.ShapeDtypeStruct((rows, d), F32),
        grid_spec=pltpu.PrefetchScalarGridSpec(
            num_scalar_prefetch=6, grid=(nvis,),
            in_specs=[pl.BlockSpec((tm, d), tile), wspec(d, hid), wspec(d, hid), wspec(hid, d)],
            out_specs=pl.BlockSpec((tm, d), tile),
            scratch_shapes=[pltpu.VMEM((d, hid), BF16), pltpu.VMEM((d, hid), BF16), pltpu.VMEM((hid, d), BF16)]),
        compiler_params=_cparams(("arbitrary",)),
    )(*visits, xs, wg, wu, wd)


def _combine_kernel(nlat, final, two, *refs):
    if two:
        x_ref, xc_ref, wt_ref, mod_ref, fg_ref, y0_ref, y1_ref, o_ref, oc_ref = refs
    else:
        x_ref, wt_ref, mod_ref, fg_ref, y0_ref, y1_ref, o_ref = refs
    d = D_MODEL
    step = pl.program_id(0)
    wt = wt_ref[...]
    y = wt[:, 0:1] * y0_ref[...] + wt[:, 1:2] * y1_ref[...]
    x = jnp.where(step < nlat, x_ref[...], xc_ref[...]) if two else x_ref[...]
    out = x + mod_ref[:, 5 * d:6 * d] * y
    if final:
        out = _rms(out, fg_ref[...])
    if two:
        @pl.when(step < nlat)
        def _():
            o_ref[...] = out

        @pl.when(step >= nlat)
        def _():
            oc_ref[...] = out
    else:
        o_ref[...] = out


def _combine(x2, xc2, wt, mod, rows_per_mod, ctx_row, fg, yt, final):
    d = x2.shape[1]
    tm = MOE_TILE
    two = xc2 is not None
    n = x2.shape[0] + (xc2.shape[0] if two else 0)
    ntiles = n // tm
    nlat, xspecs, modspec = _stream_specs(x2, xc2, rows_per_mod, ctx_row)
    out_shape = [jax.ShapeDtypeStruct(a.shape, F32) for a in ([x2, xc2] if two else [x2])]
    return pl.pallas_call(
        functools.partial(_combine_kernel, nlat, final, two),
        out_shape=out_shape,
        grid=(ntiles,),
        in_specs=xspecs + [pl.BlockSpec((tm, 8), lambda i: (i, 0)), modspec, pl.BlockSpec(fg.shape, lambda i: (0, 0)),
                           pl.BlockSpec((tm, d), lambda i: (i, 0)), pl.BlockSpec((tm, d), lambda i: (ntiles + i, 0))],
        out_specs=xspecs,
        compiler_params=_cparams(("arbitrary",)),
    )(*([x2, xc2] if two else [x2]), wt, mod, fg, yt, yt)


def _pick(table, idx):
    hot = idx[..., None] == jnp.arange(table.shape[0], dtype=I32)
    return jnp.sum(jnp.where(hot, table, 0), axis=-1)


def _moe_plan(counts, e, r, rows):
    tm = MOE_TILE
    ends = jnp.cumsum(counts)
    starts = ends - counts
    dest = (_pick(starts, e) + r).reshape(-1)
    first_tile = starts // tm
    nvis = jnp.where(counts > 0, (ends - 1) // tm - first_tile + 1, 0)
    vend = jnp.cumsum(nvis)
    nv = vend[-1:]
    v = jnp.minimum(jnp.arange(rows // tm + MOE_EXPERTS - 1, dtype=I32), nv[0] - 1)
    ve = jnp.sum((vend[None, :] <= v[:, None]).astype(I32), axis=1)
    vt = _pick(first_tile, ve) + v - _pick(vend - nvis, ve)
    vlo = jnp.maximum(_pick(starts, ve) - vt * tm, 0)
    vhi = jnp.minimum(_pick(ends, ve) - vt * tm, tm)
    vfirst = jnp.concatenate([jnp.ones((1,), I32), (vt[1:] != vt[:-1]).astype(I32)])
    return dest, (vt, ve, vlo, vhi, vfirst, nv)


def _moe(x, xc, mod, ctx_row, gn, wr, u, layer, wg, wu, wd, fg, final):
    b, n, d = x.shape
    x2 = x.reshape(b * n, d)
    xc2 = None if xc is None else xc.reshape(-1, d)
    nt = x2.shape[0] + (0 if xc is None else xc2.shape[0])
    h, e, wt, r, cnt = _router(x2, xc2, mod, n, ctx_row, gn, wr, u)
    dest, visits = _moe_plan(cnt[:, 0].astype(I32), e, r, 2 * nt)
    xs = _sc_permute_rows(h, dest, scatter=True)
    ys = _gmm(xs, visits, layer, wg, wu, wd)
    yt = _sc_permute_rows(ys, dest, scatter=False)
    outs = _combine(x2, xc2, wt, mod, n, ctx_row, fg, yt, final)
    if xc is None:
        return outs[0].reshape(b, n, d), None
    return outs[0].reshape(b, n, d), outs[1].reshape(xc.shape)


def _rope_tables(rows, dim):
    row = jnp.repeat(jnp.arange(rows, dtype=F32), GRID_W)
    col = jnp.tile(jnp.arange(GRID_W, dtype=F32), rows)
    half = dim // 2
    inv = jnp.power(ROPE_BASE, -jnp.arange(0, half, 2, dtype=F32) / half)
    ar = row[:, None] * inv[None, :]
    ac = col[:, None] * inv[None, :]
    ang = jnp.concatenate([ar, ar, ac, ac], axis=-1)
    return jnp.cos(ang), jnp.sin(ang)


def _even_tables(n, with_rope):
    if with_rope:
        cm, sm = _rope_tables(n // GRID_W, MLA_ROPE)
        cwin, swin = _rope_tables(n // GRID_W, WIN_HEAD_DIM)
    else:
        cm, sm = jnp.ones((n, MLA_ROPE), F32), jnp.zeros((n, MLA_ROPE), F32)
        cwin, swin = jnp.ones((n, WIN_HEAD_DIM), F32), jnp.zeros((n, WIN_HEAD_DIM), F32)
    one, zero = jnp.ones((n, MLA_NOPE), F32), jnp.zeros((n, MLA_NOPE), F32)
    pad = jnp.zeros((n, LANES - MLA_NOPE - MLA_ROPE), F32)
    return (jnp.concatenate([one, cm, pad], axis=1), jnp.concatenate([zero, sm, pad], axis=1),
            jnp.concatenate([cwin, cwin], axis=1), jnp.concatenate([swin, swin], axis=1))


def _even_weights(w_in, qg, w_uq, kvg, w_ukv):
    d = w_in.shape[0]
    o = np.cumsum([0, MLA_Q_RANK, MLA_KV_RANK, MLA_ROPE, 512, 128, 128])
    cq, ckv, kr, qw, kw, vw = [w_in[:, o[i]:o[i + 1]] for i in range(6)]
    z = lambda c: jnp.zeros((d, c), F32)
    kr128 = jnp.concatenate([z(MLA_NOPE), kr, z(LANES - MLA_NOPE - MLA_ROPE)], axis=1)
    dup = lambda t: jnp.concatenate([t[:, 0:64], t[:, 0:64], t[:, 64:128], t[:, 64:128]], axis=1)
    win = jnp.concatenate([cq, ckv, kr128, qw, dup(kw), dup(vw)], axis=1).astype(BF16)
    uq = w_uq.reshape(MLA_Q_RANK, MLA_HEADS, MLA_NOPE + MLA_ROPE)
    uq = jnp.pad(uq, ((0, 0), (0, 0), (0, LANES - MLA_NOPE - MLA_ROPE))).reshape(MLA_Q_RANK, MLA_HEADS * LANES)
    ukv = w_ukv.reshape(MLA_KV_RANK, MLA_HEADS, MLA_NOPE + MLA_V)
    ukk = jnp.pad(ukv[:, :, :MLA_NOPE], ((0, 0), (0, 0), (0, LANES - MLA_NOPE))).reshape(MLA_KV_RANK, MLA_HEADS * LANES)
    ukvv = ukv[:, :, MLA_NOPE:].reshape(MLA_KV_RANK, MLA_HEADS * MLA_V)
    return (win, qg.reshape(1, -1), uq.astype(BF16), kvg.reshape(1, -1), ukk.astype(BF16), ukvv.T.astype(BF16))


def _odd_weights(w_in, w_g2, b_g, ln_g, ln_b, w_s, b_s):
    d = w_in.shape[0]
    o = np.cumsum([0, GLA_K, GLA_K, GLA_V, 2 * GLA_GATE_RANK, GLA_V, SG_WIDTH, SG_WIDTH])
    q, k, v, g, r, u, vg = [w_in[:, o[i]:o[i + 1]] for i in range(7)]
    g128 = jnp.concatenate([g, jnp.zeros((d, LANES - 2 * GLA_GATE_RANK), F32)], axis=1)
    win = jnp.concatenate([q, k, v, g128, r, u, vg], axis=1).astype(BF16)
    zr = jnp.zeros((GLA_GATE_RANK, GLA_K), F32)
    pad = jnp.zeros((LANES - 2 * GLA_GATE_RANK, GLA_K), F32)
    wg = jnp.concatenate([jnp.concatenate([w_g2[0], zr, pad], axis=0),
                          jnp.concatenate([zr, w_g2[1], pad], axis=0)], axis=1)
    bg = b_g.reshape(1, 2 * GLA_K)
    return (win, wg, bg, ln_g.reshape(1, -1), ln_b.reshape(1, -1), w_s.astype(BF16), b_s.T)


def kernel(x, c, ctx, c_ctx, ada_w, ada_b, norm_mix_g, norm_ffn_g, even_w_in, mla_q_norm_g, mla_w_uq, mla_kv_norm_g, mla_w_ukv, win_sink, even_w_out, odd_w_in, gla_w_g2, gla_b_g, gla_norm_g, sg_ln_g, sg_ln_b, sg_w_s, sg_b_s, odd_w_out, moe_w_rg, moe_w_re, moe_w_gate, moe_w_up, moe_w_down, final_norm_g):
    b, n, d = x.shape
    lc = ctx.shape[1]
    depth = ada_w.shape[0]
    assert depth == 2 and d == D_MODEL and b < 8
    assert n % 512 == 0 and lc % MOE_TILE == 0 and n % GRID_W == 0
    tm = 512 if n % 512 == 0 else 256
    tq = 256

    cond8 = jnp.concatenate([c, c_ctx[None, :], jnp.zeros((8 - b - 1, d), F32)], axis=0)
    mod_all = _adaln(cond8, ada_w, ada_b).reshape(depth, 8, 1, 6 * d)
    ctx_row = b
    u_tri = jnp.asarray(np.triu(np.ones((MOE_TILE, MOE_TILE), np.float32), 1), BF16)
    fg = final_norm_g.reshape(1, d)

    def router_w(layer):
        return jnp.concatenate([moe_w_rg[layer].T, jnp.zeros((8 - MOE_GROUPS, d), F32), moe_w_re[layer].T], axis=0)

    def moe(xx, xx_ctx, layer, final):
        return _moe(xx, xx_ctx, mod_all[layer], ctx_row, norm_ffn_g[layer].reshape(1, d), router_w(layer), u_tri,
                    layer, moe_w_gate, moe_w_up, moe_w_down, fg, final)

    mod = mod_all[0]
    gn = norm_mix_g[0].reshape(1, d)
    ew = _even_weights(even_w_in[0], mla_q_norm_g[0], mla_w_uq[0], mla_kv_norm_g[0], mla_w_ukv[0])
    qm_l, km_l, vm_l, qw_l, kw_l, vw_l = _even_in(x, mod, None, gn, ew, _even_tables(n, True), tm)
    qm_c, km_c, vm_c, qw_c, kw_c, vw_c = _even_in(ctx, mod, ctx_row, gn, ew, _even_tables(lc, False), lc)
    w_out = even_w_out[0].astype(BF16)
    sink = win_sink[0]
    oa_l = _mla_attn(qm_l, [(km_l, vm_l), (km_c, vm_c)], tq)
    ob_l = _gqa(qw_l, kw_l, vw_l, kw_c, vw_c, sink, True)
    xl = _even_out(x, oa_l, ob_l, w_out, mod, None, tm)
    oa_c = _mla_attn(qm_c, [(km_c, vm_c)], lc)
    ob_c = _gqa(qw_c, None, None, kw_c, vw_c, sink, False)
    xc = _even_out(ctx, oa_c, ob_c, w_out, mod, ctx_row, lc)
    xl, xc = moe(xl, xc, 0, False)

    mod = mod_all[1]
    gn = norm_mix_g[1].reshape(1, d)
    ow = _odd_weights(odd_w_in[0], gla_w_g2[0], gla_b_g[0], sg_ln_g[0], sg_ln_b[0], sg_w_s[0], sg_b_s[0])
    q_l, k_l, v_l, la_l, r_l, dl_l = _odd_in(xl, mod, None, gn, ow, 256)
    q_c, k_c, v_c, la_c, _, _ = _odd_in(xc, mod, ctx_row, gn, ow, lc)
    cumq_np, cumkt_np, pm_np, nlev = _gla_tables()
    cumq, cumkt = jnp.asarray(cumq_np, BF16), jnp.asarray(cumkt_np, BF16)
    pm = jnp.asarray(pm_np, F32)
    s0 = jnp.zeros((b, 2, GLA_HEADS, GLA_DV, LANES), F32)
    _, _, s_ctx = _gla(q_c, k_c, v_c, la_c, s0, cumq, cumkt, pm, nlev)
    o_fwd, o_bwd, _ = _gla(q_l, k_l, v_l, la_l, s_ctx, cumq, cumkt, pm, nlev)
    xl = _odd_out(xl, o_fwd, o_bwd, r_l, dl_l, gla_norm_g[0].reshape(1, -1), odd_w_out[0].astype(BF16), mod, tm)
    return moe(xl, None, 1, True)[0]
```

```python
import functools

import numpy as np
import jax
import jax.numpy as jnp
from jax import lax
from jax.experimental import pallas as pl
from jax.experimental.pallas import tpu as pltpu
from jax.experimental.pallas import tpu_sc as plsc

F32 = jnp.float32
BF16 = jnp.bfloat16
I32 = jnp.int32

D_MODEL = 1024
GRID_W = 64
EPS = 1e-6
ROPE_BASE = 10000.0
MLA_HEADS = 8
MLA_Q_RANK = 256
MLA_KV_RANK = 128
MLA_NOPE = 64
MLA_ROPE = 32
MLA_V = 64
WIN_HEADS = 8
WIN_KV_HEADS = 2
WIN_HEAD_DIM = 64
WIN_BLOCK = 128
GLA_HEADS = 4
GLA_DK = 64
GLA_DV = 128
GLA_GATE_RANK = 16
GLA_TAU = 16.0
GLA_K = GLA_HEADS * GLA_DK
GLA_V = GLA_HEADS * GLA_DV
SG_GROUPS = 4
SG_CHUNK = 128
SG_WIDTH = 512
MOE_GROUPS = 4
MOE_PER_GROUP = 8
MOE_EXPERTS = 32
MOE_HIDDEN = 512

LANES = 128
GLA_BLOCK = 128
MOE_TILE = 256
MLA_KEY_CHUNK = 512
SC_CHUNK = 64
NEG = -1e30
LOG2E = 1.4426950408889634
VMEM_LIMIT = 56 * 1024 * 1024


def _cparams(sem):
    return pltpu.CompilerParams(dimension_semantics=sem, vmem_limit_bytes=VMEM_LIMIT)


def _dot(a, b):
    return jnp.dot(a, b, preferred_element_type=F32)


def _dot_nt(a, b):
    return lax.dot_general(a, b, (((1,), (1,)), ((), ())), preferred_element_type=F32)


def _split2(a):
    hi = a.astype(BF16)
    lo = (a - hi.astype(F32)).astype(BF16)
    return hi, lo


def _split3(a):
    hi = a.astype(BF16)
    r = a - hi.astype(F32)
    mid = r.astype(BF16)
    lo = (r - mid.astype(F32)).astype(BF16)
    return hi, mid, lo


def _rms(x, g):
    ms = jnp.mean(x * x, axis=-1, keepdims=True)
    return x * lax.rsqrt(ms + EPS) * g


def _lane_tile(t, reps):
    return t if reps == 1 else jnp.concatenate([t] * reps, axis=1)


def _rope(t, cos, sin, quarter):
    n = t.shape[1]
    lane = lax.broadcasted_iota(I32, t.shape, 1)
    first = (lane & (2 * quarter - 1)) < quarter
    rot = jnp.where(first, -pltpu.roll(t, n - quarter, 1), pltpu.roll(t, quarter, 1))
    return t * cos + rot * sin


def _adaln_kernel(c_ref, w_ref, b_ref, o_ref):
    c = c_ref[...]
    s_hi, s_lo = _split2(c * jax.nn.sigmoid(c))
    w_hi, w_lo = _split2(w_ref[...])
    o_ref[...] = _dot(s_hi, w_hi) + _dot(s_lo, w_hi) + _dot(s_hi, w_lo) + b_ref[...]


def _adaln(cond8, ada_w, ada_b):
    depth, d, n6 = ada_w.shape
    tn = 1536
    return pl.pallas_call(
        _adaln_kernel,
        out_shape=jax.ShapeDtypeStruct((depth, 8, n6), F32),
        grid=(depth, n6 // tn),
        in_specs=[
            pl.BlockSpec((8, d), lambda l, j: (0, 0)),
            pl.BlockSpec((None, d, tn), lambda l, j: (l, 0, j)),
            pl.BlockSpec((None, 1, tn), lambda l, j: (l, 0, j)),
        ],
        out_specs=pl.BlockSpec((None, 8, tn), lambda l, j: (l, 0, j)),
        compiler_params=_cparams(("parallel", "parallel")),
    )(cond8, ada_w, ada_b.reshape(depth, 1, n6))


def _even_in_kernel(x_ref, mod_ref, gn_ref, win_ref, qg_ref, wuq_ref, kvg_ref, wukk_ref, wukv_ref,
                    cq_ref, sq_ref, cw_ref, sw_ref,
                    qm_ref, km_ref, vm_ref, qw_ref, kw_ref, vw_ref):
    d = D_MODEL
    mod = mod_ref[...]
    h = _rms(x_ref[...], gn_ref[...]) * (1.0 + mod[:, d:2 * d]) + mod[:, 0:d]
    z = _dot(h.astype(BF16), win_ref[...])
    cq, sq, cw, sw = cq_ref[...], sq_ref[...], cw_ref[...], sw_ref[...]
    cqn = _rms(z[:, 0:256], qg_ref[...]).astype(BF16)
    q = _dot(cqn, wuq_ref[...])
    q = _rope(q, _lane_tile(cq, 8), _lane_tile(sq, 8), MLA_ROPE // 4)
    qm_ref[...] = (q * (LOG2E * (MLA_NOPE + MLA_ROPE) ** -0.5)).astype(BF16)
    ckvn = _rms(z[:, 256:384], kvg_ref[...]).astype(BF16)
    kn = _dot(ckvn, wukk_ref[...])
    kr = _rope(z[:, 384:512], cq, sq, MLA_ROPE // 4)
    km_ref[...] = (kn + _lane_tile(kr, 8)).astype(BF16)
    vm_ref[...] = _dot_nt(wukv_ref[...], ckvn).astype(BF16)
    qw = _rope(z[:, 512:1024], _lane_tile(cw, 4), _lane_tile(sw, 4), WIN_HEAD_DIM // 4)
    qw_ref[...] = (qw * (WIN_HEAD_DIM ** -0.5)).astype(BF16)
    kw = _rope(z[:, 1024:1280], _lane_tile(cw, 2), _lane_tile(sw, 2), WIN_HEAD_DIM // 4)
    kw_ref[...] = kw.astype(BF16)
    vw_ref[...] = z[:, 1280:1536].astype(BF16)


def _even_in(x, mod, mod_row, gn, wts, tabs, tm):
    b, n, d = x.shape
    win, qg, wuq, kvg, wukk, wukv = wts
    nt = n // tm
    row = (lambda bi, i: (bi, 0, 0)) if mod_row is None else (lambda bi, i: (mod_row, 0, 0))
    full = lambda a: pl.BlockSpec(a.shape, lambda bi, i: (0,) * a.ndim)
    tab = pl.BlockSpec((tm, LANES), lambda bi, i: (i, 0))
    outw = (1024, 1024, None, 512, 256, 256)
    rowspec = lambda w: pl.BlockSpec((None, tm, w), lambda bi, i: (bi, i, 0))
    colspec = pl.BlockSpec((None, 512, tm), lambda bi, i: (bi, 0, i))
    return pl.pallas_call(
        _even_in_kernel,
        out_shape=[jax.ShapeDtypeStruct((b, 512, n) if w is None else (b, n, w), BF16) for w in outw],
        grid=(b, nt),
        in_specs=[pl.BlockSpec((None, tm, d), lambda bi, i: (bi, i, 0)),
                  pl.BlockSpec((None, 1, 6 * d), row),
                  full(gn), full(win), full(qg), full(wuq), full(kvg), full(wukk), full(wukv),
                  tab, tab, tab, tab],
        out_specs=[colspec if w is None else rowspec(w) for w in outw],
        compiler_params=_cparams(("parallel", "parallel")),
    )(x, mod, gn, win, qg, wuq, kvg, wukk, wukv, *tabs)


def _mla_attn_kernel(nseg, q_ref, *refs):
    ks, vts = refs[0:2 * nseg:2], refs[1:2 * nseg:2]
    o_ref = refs[2 * nseg]
    s_bufs = refs[2 * nseg + 1:2 * nseg + 3]
    p_bufs = refs[2 * nseg + 3:2 * nseg + 5]
    pieces, base = [], 0
    for k in ks:
        n = k.shape[0]
        pieces += [(k, c0, min(n, c0 + MLA_KEY_CHUNK), base + c0) for c0 in range(0, n, MLA_KEY_CHUNK)]
        base += n

    def score_chunk(h, piece, buf):
        k, c0, c1, g0 = piece
        hs = slice(h * LANES, (h + 1) * LANES)
        half = (c1 - c0) // 2
        maxes = []
        for a in (c0, c0 + half):
            s = _dot_nt(k[a:a + half, hs], q_ref[:, hs])
            buf[g0 + a - c0:g0 + a - c0 + half, :] = s
            maxes.append(jnp.max(s, axis=0, keepdims=True))
        return jnp.maximum(*maxes)

    def prob_chunk(piece, sbuf, pbuf, m):
        _, c0, c1, g0 = piece
        p = jnp.exp2(sbuf[g0:g0 + c1 - c0, :] - m)
        pbuf[g0:g0 + c1 - c0, :] = p.astype(BF16)
        return jnp.sum(p, axis=0, keepdims=True)

    m_next = functools.reduce(jnp.maximum, [score_chunk(0, pc, s_bufs[0]) for pc in pieces])
    outs = []
    for h in range(MLA_HEADS):
        m_cur, maxes, sums = m_next, [], []
        for pc in pieces:
            if h + 1 < MLA_HEADS:
                maxes.append(score_chunk(h + 1, pc, s_bufs[(h + 1) % 2]))
            sums.append(prob_chunk(pc, s_bufs[h % 2], p_bufs[h % 2], m_cur))
        if h + 1 < MLA_HEADS:
            m_next = functools.reduce(jnp.maximum, maxes)
        l = functools.reduce(jnp.add, sums)
        vrows = slice(h * MLA_V, (h + 1) * MLA_V)
        ot, base = None, 0
        for k, vt in zip(ks, vts):
            n = k.shape[0]
            part = _dot(vt[vrows, :], p_bufs[h % 2][base:base + n, :])
            ot = part if ot is None else ot + part
            base += n
        outs.append(ot * (1.0 / l))
    o_ref[...] = jnp.concatenate(outs, axis=0).T.astype(BF16)


def _mla_attn(q, segs, tq):
    b, n, _ = q.shape
    in_specs = [pl.BlockSpec((None, tq, 1024), lambda bi, i: (bi, i, 0))]
    args = [q]
    keys = 0
    for k, vt in segs:
        lk = k.shape[1]
        keys += lk
        in_specs += [pl.BlockSpec((None, lk, 1024), lambda bi, i: (bi, 0, 0)),
                     pl.BlockSpec((None, 512, lk), lambda bi, i: (bi, 0, 0))]
        args += [k, vt]
    return pl.pallas_call(
        functools.partial(_mla_attn_kernel, len(segs)),
        out_shape=jax.ShapeDtypeStruct((b, n, 512), BF16),
        grid=(b, n // tq),
        in_specs=in_specs,
        out_specs=pl.BlockSpec((None, tq, 512), lambda bi, i: (bi, i, 0)),
        scratch_shapes=[pltpu.VMEM((keys, tq), F32), pltpu.VMEM((keys, tq), F32),
                        pltpu.VMEM((keys, tq), BF16), pltpu.VMEM((keys, tq), BF16)],
        compiler_params=_cparams(("parallel", "parallel")),
    )(*args)


def _gqa_kernel(has_win, nb, sink_ref, q_ref, *refs):
    if has_win:
        kp, kc, kn, vp, vc, vn, kx, vx, o_ref = refs
    else:
        kx, vx, o_ref = refs
    tq = q_ref.shape[0]
    i = pl.program_id(1)
    lane = lax.broadcasted_iota(I32, (tq, LANES), 1)
    row2 = lax.broadcasted_iota(I32, (2 * tq, 1), 0)
    half = WIN_HEAD_DIM
    npair = WIN_HEADS // 2
    kcats, vcats = [], []
    for g in range(WIN_KV_HEADS):
        gs = slice(g * LANES, (g + 1) * LANES)
        if has_win:
            kcats.append(jnp.concatenate([kp[:, gs], kc[:, gs], kn[:, gs], kx[:, gs]], axis=0))
            vcats.append(jnp.concatenate([vp[:, gs], vc[:, gs], vn[:, gs], vx[:, gs]], axis=0))
        else:
            kcats.append(kx[:, gs])
            vcats.append(vx[:, gs])
    scores = []
    for j in range(npair):
        qp = q_ref[:, j * LANES:(j + 1) * LANES]
        zero = jnp.zeros_like(qp)
        q2 = jnp.concatenate([jnp.where(lane < half, qp, zero), jnp.where(lane >= half, qp, zero)], axis=0)
        scores.append(_dot_nt(q2, kcats[j // 2]))
    if has_win:
        w = WIN_BLOCK
        r = lax.broadcasted_iota(I32, scores[0].shape, 0) & (tq - 1)
        c = lax.broadcasted_iota(I32, scores[0].shape, 1)
        big = jnp.int32(1 << 20)
        no_prev = jnp.where(i > 0, 0, big)
        no_next = jnp.where(i < nb - 1, 0, big)
        ok_prev = c >= r + no_prev
        ok_next = (c - 2 * w) <= r - no_next
        valid = ((c >= w) | ok_prev) & ((c < 2 * w) | (c >= 3 * w) | ok_next)
    probs, inv = [], []
    for j in range(npair):
        s = jnp.where(valid, scores[j], NEG) if has_win else scores[j]
        sk = jnp.where(row2 < tq, sink_ref[2 * j], sink_ref[2 * j + 1])
        m = jnp.maximum(jnp.max(s, axis=-1, keepdims=True), sk)
        p = jnp.exp(s - m)
        inv.append(1.0 / (jnp.sum(p, axis=-1, keepdims=True) + jnp.exp(sk - m)))
        probs.append(p.astype(BF16))
    for j in range(npair):
        o2 = _dot(probs[j], vcats[j // 2]) * inv[j]
        o_ref[:, j * LANES:(j + 1) * LANES] = jnp.where(lane < half, o2[:tq], o2[tq:]).astype(BF16)


def _gqa(q, k, v, kx, vx, sink, has_win):
    b, n, _ = q.shape
    lc = kx.shape[1]
    smem = pl.BlockSpec(memory_space=pltpu.SMEM)
    ctxs = pl.BlockSpec((None, lc, 256), lambda bi, i: (bi, 0, 0))
    if has_win:
        tq = WIN_BLOCK
        nb = n // tq
        blk = lambda f: pl.BlockSpec((None, tq, 256), f)
        prev = lambda bi, i: (bi, jnp.maximum(i - 1, 0), 0)
        cur = lambda bi, i: (bi, i, 0)
        nxt = lambda bi, i: (bi, jnp.minimum(i + 1, nb - 1), 0)
        in_specs = [smem, pl.BlockSpec((None, tq, 512), cur),
                    blk(prev), blk(cur), blk(nxt), blk(prev), blk(cur), blk(nxt), ctxs, ctxs]
        args = (sink, q, k, k, k, v, v, v, kx, vx)
    else:
        tq, nb = n, 1
        in_specs = [smem, pl.BlockSpec((None, tq, 512), lambda bi, i: (bi, i, 0)), ctxs, ctxs]
        args = (sink, q, kx, vx)
    return pl.pallas_call(
        functools.partial(_gqa_kernel, has_win, nb),
        out_shape=jax.ShapeDtypeStruct((b, n, 512), BF16),
        grid=(b, nb),
        in_specs=in_specs,
        out_specs=pl.BlockSpec((None, tq, 512), lambda bi, i: (bi, i, 0)),
        compiler_params=_cparams(("parallel", "parallel")),
    )(*args)


def _even_out_kernel(x_ref, a_ref, b_ref, w_ref, mod_ref, o_ref):
    d = D_MODEL
    y = _dot(a_ref[...], w_ref[0:512, :]) + _dot(b_ref[...], w_ref[512:1024, :])
    o_ref[...] = x_ref[...] + mod_ref[:, 2 * d:3 * d] * y


def _even_out(x, oa, ob, w, mod, mod_row, tm):
    b, n, d = x.shape
    row = (lambda bi, i: (bi, 0, 0)) if mod_row is None else (lambda bi, i: (mod_row, 0, 0))
    act = lambda wd: pl.BlockSpec((None, tm, wd), lambda bi, i: (bi, i, 0))
    return pl.pallas_call(
        _even_out_kernel,
        out_shape=jax.ShapeDtypeStruct((b, n, d), F32),
        grid=(b, n // tm),
        in_specs=[act(d), act(512), act(512), pl.BlockSpec(w.shape, lambda bi, i: (0, 0)),
                  pl.BlockSpec((None, 1, 6 * d), row)],
        out_specs=act(d),
        compiler_params=_cparams(("parallel", "parallel")),
    )(x, oa, ob, w, mod)


def _log_sigmoid(z):
    return jnp.minimum(z, 0.0) - jnp.log(1.0 + jnp.exp(-jnp.abs(z)))


def _odd_in_kernel(x_ref, y0_ref, y1_ref, wt_ref, modp_ref, mod_ref, gn_ref, win_ref, wg_ref, bg_ref, lng_ref,
                   lnb_ref, ws_ref, bst_ref, xn_ref, q_ref, k_ref, v_ref, la_ref, r_ref, dl_ref):
    d = D_MODEL
    tm = x_ref.shape[0]
    wt = wt_ref[...]
    x = x_ref[...] + modp_ref[:, 5 * d:6 * d] * (wt[:, 0:1] * y0_ref[...] + wt[:, 1:2] * y1_ref[...])
    xn_ref[...] = x
    mod = mod_ref[...]
    h = (_rms(x, gn_ref[...]) * (1.0 + mod[:, d:2 * d]) + mod[:, 0:d]).astype(BF16)
    z = _dot(h, win_ref[...])
    q_ref[...] = z[:, 0:256] * (GLA_DK ** -0.5)
    k_ref[...] = z[:, 256:512]
    v_ref[...] = z[:, 512:1024].astype(BF16)
    g_hi, g_lo = _split2(z[:, 1024:1152])
    w_hi, w_lo = _split2(wg_ref[...])
    zg = _dot(g_hi, w_hi) + _dot(g_lo, w_hi) + _dot(g_hi, w_lo) + bg_ref[...]
    la_ref[...] = _log_sigmoid(zg) / GLA_TAU
    r_ref[...] = z[:, 1152:1664]
    u = jax.nn.gelu(z[:, 1664:2176])
    vg = jax.nn.gelu(z[:, 2176:2688])
    mu = jnp.mean(vg, axis=-1, keepdims=True)
    vc = vg - mu
    var = jnp.mean(vc * vc, axis=-1, keepdims=True)
    vn = (vc * lax.rsqrt(var + EPS) * lng_ref[...] + lnb_ref[...]).astype(BF16)
    bst = bst_ref[...]
    for c in range(tm // SG_CHUNK):
        rows = slice(c * SG_CHUNK, (c + 1) * SG_CHUNK)
        parts = []
        for g in range(SG_GROUPS):
            cols = slice(g * LANES, (g + 1) * LANES)
            parts.append(_dot(ws_ref[g], vn[rows, cols]) + bst[:, g:g + 1])
        dl_ref[rows, :] = (u[rows, :] * jnp.concatenate(parts, axis=1)).astype(BF16)


def _odd_in(x, pending, tile0, modp, mod, mod_row, gn, wts):
    b, n, d = x.shape
    tm = MOE_TILE
    yt, wt = pending
    ntiles = wt.shape[0] // tm
    win, wg, bg, lng, lnb, ws, bst = wts
    row = (lambda bi, i: (bi, 0, 0)) if mod_row is None else (lambda bi, i: (mod_row, 0, 0))
    full = lambda a: pl.BlockSpec(a.shape, lambda bi, i: (0,) * a.ndim)
    act = lambda wd: pl.BlockSpec((None, tm, wd), lambda bi, i: (bi, i, 0))
    tok = lambda bi, i: tile0 + bi * (n // tm) + i
    outs = [((b, n, d), F32, act(d)),
            ((b, n, 256), F32, act(256)), ((b, n, 256), F32, act(256)), ((b, n, 512), BF16, act(512)),
            ((b, n, 512), F32, act(512)), ((b, n, 512), F32, act(512)), ((b, n, 512), BF16, act(512))]
    return pl.pallas_call(
        _odd_in_kernel,
        out_shape=[jax.ShapeDtypeStruct(s, t) for s, t, _ in outs],
        grid=(b, n // tm),
        in_specs=[act(d), pl.BlockSpec((tm, d), lambda bi, i: (tok(bi, i), 0)),
                  pl.BlockSpec((tm, d), lambda bi, i: (ntiles + tok(bi, i), 0)),
                  pl.BlockSpec((tm, 8), lambda bi, i: (tok(bi, i), 0)),
                  pl.BlockSpec((None, 1, 6 * d), row), pl.BlockSpec((None, 1, 6 * d), row),
                  full(gn), full(win), full(wg), full(bg), full(lng), full(lnb), full(ws), full(bst)],
        out_specs=[sp for _, _, sp in outs],
        compiler_params=_cparams(("parallel", "parallel")),
    )(x, yt, yt, wt, modp, mod, gn, win, wg, bg, lng, lnb, ws, bst)


def _gla_tables():
    c = GLA_BLOCK
    t = np.arange(c)[:, None]
    u = np.arange(c)[None, :]
    levels = [c >> i for i in range(int(np.log2(c)) + 1)]
    cum = np.zeros((2, 2 * len(levels), c, c), np.float32)
    pair = np.zeros((2, len(levels), c, c), np.float32)
    for li, m in enumerate(levels):
        same = (t // m) == (u // m)
        cum[0, 2 * li] = same & (u <= t)
        cum[0, 2 * li + 1] = same & (u > t)
        cum[1, 2 * li] = same & (u >= t)
        cum[1, 2 * li + 1] = same & (u < t)
        if li > 0:
            pair[0, li] = ((t // m) % 2 == 1) & ((u // m) == (t // m) - 1)
            pair[1, li] = ((t // m) % 2 == 0) & ((u // m) == (t // m) + 1)
    pair[:, 0] = np.eye(c, dtype=np.float32)
    nlev = len(levels)
    m1 = cum[:, 0::2].reshape(2, nlev * c, c)
    m2t = np.concatenate([cum[:, 2 * li + 1].transpose(0, 2, 1) for li in range(nlev)], axis=2)
    return np.concatenate([m1, m1], axis=2), np.concatenate([m2t, m2t], axis=1), pair, nlev


def _gla_chain(nlev, q, k, la, v_ref, cumq, cumkt, pm_ref, st_ref, o_ref):
    c = GLA_BLOCK
    lat, kt = la.T, k.T
    l_hi, l_mid = _split2(la)
    t_hi, t_mid = _split2(lat)
    exq = jnp.exp(_dot(cumq, jnp.concatenate([l_hi, l_mid], axis=0)))
    exk = jnp.exp(_dot(jnp.concatenate([t_hi, t_mid], axis=1), cumkt))
    gcol = jnp.exp(jnp.sum(lat, axis=1, keepdims=True))
    yield
    qe = [(q * exq[li * c:(li + 1) * c]).astype(BF16) for li in range(nlev)]
    ke = [(kt * exk[:, li * c:(li + 1) * c]).astype(BF16) for li in range(nlev)]
    qb, kb = q.astype(BF16), kt.astype(BF16)
    states = [st_ref[hd] for hd in range(GLA_HEADS)]
    yield
    outs, new_states = [], []
    lane = lax.broadcasted_iota(I32, (c, LANES), 1)
    srow = lax.broadcasted_iota(I32, (LANES, 1), 0)
    zero = jnp.zeros((c, LANES), BF16)
    for hd in range(GLA_HEADS):
        ps = slice((hd // 2) * LANES, (hd // 2 + 1) * LANES)
        vs = slice(hd * GLA_DV, (hd + 1) * GLA_DV)
        mine = (lane < GLA_DK) if hd % 2 == 0 else (lane >= GLA_DK)
        mine_row = (srow < GLA_DK) if hd % 2 == 0 else (srow >= GLA_DK)
        pick = lambda t: jnp.where(mine, t[:, ps], zero)
        a = pm_ref[0] * _dot(pick(qb), kb[ps, :])
        for li in range(1, nlev):
            a = a + pm_ref[li] * _dot(pick(qe[li]), ke[li][ps, :])
        v_h = v_ref[:, vs]
        outs.append(_dot(qe[0][:, ps], states[hd].astype(BF16)) + _dot(a.astype(BF16), v_h))
        new_states.append(states[hd] * gcol[ps, :] + jnp.where(mine_row, _dot(ke[0][ps, :], v_h), 0.0))
        yield
    o_ref[...] = jnp.concatenate(outs, axis=1)
    for hd in range(GLA_HEADS):
        st_ref[hd] = new_states[hd]
    yield


def _gla_kernel(nlev, nb, *refs):
    ins_f, ins_b = refs[0:4], refs[4:8]
    cumq_ref, cumkt_ref, pm_ref, s0_ref, of_ref, ob_ref, sf_ref = refs[8:15]
    st_refs = refs[15:]
    step = pl.program_id(1)

    @pl.when(step == 0)
    def _():
        for bb in range(nb):
            for d_ in range(2):
                st_refs[2 * bb + d_][...] = s0_ref[bb, d_]

    chains = []
    for bb in range(nb):
        for d_, (ins, o_ref) in enumerate(((ins_f, of_ref), (ins_b, ob_ref))):
            q_ref, k_ref, v_ref, la_ref = ins
            chains.append(_gla_chain(nlev, q_ref[bb], k_ref[bb], la_ref[bb], v_ref.at[bb], cumq_ref[d_],
                                     cumkt_ref[d_], pm_ref.at[d_], st_refs[2 * bb + d_], o_ref.at[bb]))
    for _ in range(GLA_HEADS + 3):
        for ch in chains:
            next(ch)
    for bb in range(nb):
        for d_ in range(2):
            sf_ref[bb, d_] = st_refs[2 * bb + d_][...]


def _gla(q, k, v, la, s0, cumq, cumkt, pm, nlev):
    b, n, _ = q.shape
    c = GLA_BLOCK
    nc = n // c
    nb = 2 if b % 2 == 0 else 1
    specs = []
    for d_ in range(2):
        pos = (lambda s_: s_) if d_ == 0 else (lambda s_: nc - 1 - s_)
        specs += [pl.BlockSpec((nb, c, 256), lambda bi, s_, pos=pos: (bi, pos(s_), 0)),
                  pl.BlockSpec((nb, c, 256), lambda bi, s_, pos=pos: (bi, pos(s_), 0)),
                  pl.BlockSpec((nb, c, 512), lambda bi, s_, pos=pos: (bi, pos(s_), 0)),
                  pl.BlockSpec((nb, c, 256), lambda bi, s_, pos=pos, d_=d_: (bi, pos(s_), d_))]
    st_spec = pl.BlockSpec((nb, 2, GLA_HEADS, GLA_DV, LANES), lambda bi, s_: (bi, 0, 0, 0, 0))
    full = lambda a: pl.BlockSpec(a.shape, lambda bi, s_: (0,) * a.ndim)
    return pl.pallas_call(
        functools.partial(_gla_kernel, nlev, nb),
        out_shape=[jax.ShapeDtypeStruct((b, n, GLA_V), F32), jax.ShapeDtypeStruct((b, n, GLA_V), F32),
                   jax.ShapeDtypeStruct((b, 2, GLA_HEADS, GLA_DV, LANES), F32)],
        grid=(b // nb, nc),
        in_specs=specs + [full(cumq), full(cumkt), full(pm), st_spec],
        out_specs=[pl.BlockSpec((nb, c, GLA_V), lambda bi, s_: (bi, s_, 0)),
                   pl.BlockSpec((nb, c, GLA_V), lambda bi, s_: (bi, nc - 1 - s_, 0)), st_spec],
        scratch_shapes=[pltpu.VMEM((GLA_HEADS, GLA_DV, LANES), F32) for _ in range(2 * nb)],
        compiler_params=_cparams(("parallel", "arbitrary")),
    )(q, k, v, la, q, k, v, la, cumq, cumkt, pm, s0)


def _odd_out_kernel(x_ref, of_ref, ob_ref, r_ref, dl_ref, gg_ref, w_ref, mod_ref, o_ref):
    d = D_MODEL
    o = of_ref[...] + ob_ref[...]
    gg = gg_ref[...]
    r = r_ref[...]
    parts = []
    for hd in range(GLA_HEADS):
        vs = slice(hd * GLA_DV, (hd + 1) * GLA_DV)
        oh = o[:, vs]
        parts.append(oh * lax.rsqrt(jnp.mean(oh * oh, axis=-1, keepdims=True) + EPS) * gg[:, vs])
    cl = (jnp.concatenate(parts, axis=1) * (r * jax.nn.sigmoid(r))).astype(BF16)
    y = _dot(cl, w_ref[0:512, :]) + _dot(dl_ref[...], w_ref[512:1024, :])
    o_ref[...] = x_ref[...] + mod_ref[:, 2 * d:3 * d] * y


def _odd_out(x, o_fwd, o_bwd, r, dl, gg, w, mod, tm):
    b, n, d = x.shape
    act = lambda wd: pl.BlockSpec((None, tm, wd), lambda bi, i: (bi, i, 0))
    return pl.pallas_call(
        _odd_out_kernel,
        out_shape=jax.ShapeDtypeStruct((b, n, d), F32),
        grid=(b, n // tm),
        in_specs=[act(d), act(GLA_V), act(GLA_V), act(512), act(512),
                  pl.BlockSpec(gg.shape, lambda bi, i: (0, 0)), pl.BlockSpec(w.shape, lambda bi, i: (0, 0)),
                  pl.BlockSpec((None, 1, 6 * d), lambda bi, i: (bi, 0, 0))],
        out_specs=act(d),
        compiler_params=_cparams(("parallel", "parallel")),
    )(x, o_fwd, o_bwd, r, dl, gg, w, mod)


def _router_kernel(nlat, x_ref, xc_ref, mod_ref, gn_ref, wr_ref, u_ref, h_ref, e_ref, wt_ref, r_ref, cnt_ref,
                   carry_ref):
    d = D_MODEL
    tm = x_ref.shape[0]
    i = pl.program_id(0)

    @pl.when(i == 0)
    def _():
        carry_ref[...] = jnp.zeros_like(carry_ref)

    mod = mod_ref[...]
    x = x_ref[...] if xc_ref is None else jnp.where(i < nlat, x_ref[...], xc_ref[...])
    h = _rms(x, gn_ref[...]) * (1.0 + mod[:, 4 * d:5 * d]) + mod[:, 3 * d:4 * d]
    h_ref[...] = h
    h_hi, h_lo = _split2(h)
    w_hi, w_lo = _split2(wr_ref[...])
    lg = _dot_nt(w_hi, h_hi) + _dot_nt(w_lo, h_hi) + _dot_nt(w_hi, h_lo)
    rid = lax.broadcasted_iota(I32, (8, tm), 0)
    gl = jnp.where(rid < MOE_GROUPS, lg[0:8], NEG)
    gmax = jnp.max(gl, axis=0, keepdims=True)
    gsel = jnp.min(jnp.where(gl == gmax, rid, 8), axis=0, keepdims=True)
    pmax = 1.0 / jnp.sum(jnp.where(rid < MOE_GROUPS, jnp.exp(gl - gmax), 0.0), axis=0, keepdims=True)
    e_in = jnp.zeros((MOE_PER_GROUP, tm), F32)
    for g in range(MOE_GROUPS):
        e_in = e_in + jnp.where(gsel == g, lg[8 + 8 * g:16 + 8 * g], 0.0)
    v1 = jnp.max(e_in, axis=0, keepdims=True)
    i1 = jnp.min(jnp.where(e_in == v1, rid, 8), axis=0, keepdims=True)
    e_rest = jnp.where(rid == i1, -jnp.inf, e_in)
    v2 = jnp.max(e_rest, axis=0, keepdims=True)
    i2 = jnp.min(jnp.where(e_rest == v2, rid, 8), axis=0, keepdims=True)
    t = jnp.exp(v2 - v1)
    w1 = pmax / (1.0 + t)
    w2 = pmax * t / (1.0 + t)
    e1 = gsel * MOE_PER_GROUP + i1
    e2 = gsel * MOE_PER_GROUP + i2
    eid = lax.broadcasted_iota(I32, (MOE_EXPERTS, tm), 0)
    oh1 = jnp.where(eid == e1, 1.0, 0.0)
    oh2 = jnp.where(eid == e2, 1.0, 0.0)
    ohs = oh1 + oh2
    base = carry_ref[:, 0:1] + _dot(ohs.astype(BF16), u_ref[...])
    r1 = jnp.sum(oh1 * base, axis=0, keepdims=True)
    r2 = jnp.sum(oh2 * base, axis=0, keepdims=True)
    carry_ref[...] = carry_ref[...] + jnp.sum(ohs, axis=1, keepdims=True)
    cnt_ref[...] = carry_ref[...]
    e_ref[...] = jnp.concatenate([e1, e2], axis=0)
    r_ref[...] = jnp.concatenate([r1, r2], axis=0).astype(I32)
    w8 = jnp.concatenate([w1, w2, jnp.zeros((6, tm), F32)], axis=0)
    wt_ref[...] = w8.T


def _stream_specs(x2, xc2, rows_per_mod, ctx_row):
    tm = MOE_TILE
    d = x2.shape[1]
    nlat = x2.shape[0] // tm
    lat = lambda i, *_: (jnp.minimum(i, nlat - 1), 0)
    ctx = lambda i, *_: (jnp.maximum(i - nlat, 0), 0)
    if xc2 is None:
        modrow = lambda i, *_: (i // (rows_per_mod // tm), 0, 0)
    else:
        modrow = lambda i, *_: (jnp.where(i < nlat, i // (rows_per_mod // tm), ctx_row), 0, 0)
    specs = [pl.BlockSpec((tm, d), lat)] + ([] if xc2 is None else [pl.BlockSpec((tm, d), ctx)])
    return nlat, specs, pl.BlockSpec((None, 1, 6 * d), modrow)


def _router(x2, xc2, mod, rows_per_mod, ctx_row, gn, wr, u):
    d = x2.shape[1]
    tm = MOE_TILE
    n = x2.shape[0] + (0 if xc2 is None else xc2.shape[0])
    nlat, xspecs, modspec = _stream_specs(x2, xc2, rows_per_mod, ctx_row)
    body = functools.partial(_router_kernel, nlat)
    if xc2 is None:
        body = lambda x_ref, *rest: _router_kernel(nlat, x_ref, None, *rest)
    return pl.pallas_call(
        body,
        out_shape=[jax.ShapeDtypeStruct((n, d), F32), jax.ShapeDtypeStruct((2, n), I32),
                   jax.ShapeDtypeStruct((n, 8), F32), jax.ShapeDtypeStruct((2, n), I32),
                   jax.ShapeDtypeStruct((MOE_EXPERTS, LANES), F32)],
        grid=(n // tm,),
        in_specs=xspecs + [modspec,
                  pl.BlockSpec(gn.shape, lambda i: (0, 0)), pl.BlockSpec(wr.shape, lambda i: (0, 0)),
                  pl.BlockSpec(u.shape, lambda i: (0, 0))],
        out_specs=[pl.BlockSpec((tm, d), lambda i: (i, 0)), pl.BlockSpec((2, tm), lambda i: (0, i)),
                   pl.BlockSpec((tm, 8), lambda i: (i, 0)), pl.BlockSpec((2, tm), lambda i: (0, i)),
                   pl.BlockSpec((MOE_EXPERTS, LANES), lambda i: (0, 0))],
        scratch_shapes=[pltpu.VMEM((MOE_EXPERTS, LANES), F32)],
        compiler_params=_cparams(("arbitrary",)),
    )(*([x2] if xc2 is None else [x2, xc2]), mod, gn, wr, u)


def _sc_permute_rows(src, dest, scatter):
    rows, d = dest.shape[0], src.shape[1]
    n = rows // 2
    info = plsc.get_sparse_core_info()
    workers = info.num_cores * info.num_subcores
    per_worker = rows // workers
    assert rows == per_worker * workers and per_worker % SC_CHUNK == 0 and n % per_worker == 0
    mesh = plsc.VectorSubcoreMesh(core_axis_name="c", subcore_axis_name="s")

    def body(src_hbm, dest_hbm, out_hbm, idx_v, rows_v, sem):
        base = (lax.axis_index("s") * info.num_cores + lax.axis_index("c")) * per_worker

        @pl.loop(0, per_worker // SC_CHUNK)
        def _(j):
            a0 = base + j * SC_CHUNK
            pltpu.sync_copy(dest_hbm.at[pl.ds(a0, SC_CHUNK)], idx_v)
            if scatter:
                t0 = jnp.where(a0 >= n, a0 - n, a0)
                pltpu.sync_copy(src_hbm.at[pl.ds(t0, SC_CHUNK)], rows_v)
                pltpu.async_copy(rows_v, out_hbm.at[idx_v], sem).wait()
            else:
                pltpu.async_copy(src_hbm.at[idx_v], rows_v, sem).wait()
                pltpu.sync_copy(rows_v, out_hbm.at[pl.ds(a0, SC_CHUNK)])

    return pl.kernel(
        body, out_type=jax.ShapeDtypeStruct((rows, d), F32), mesh=mesh,
        scratch_types=[pltpu.VMEM((SC_CHUNK,), I32), pltpu.VMEM((SC_CHUNK, d), F32), pltpu.SemaphoreType.DMA],
    )(src, dest)


def _gmm_kernel(vt_ref, ve_ref, vlo_ref, vhi_ref, vfirst_ref, nv_ref, xs_ref, wg_ref, wu_ref, wd_ref, ys_ref,
                wgb_ref, wub_ref, wdb_ref):
    del vt_ref
    v = pl.program_id(0)

    @pl.when(v < nv_ref[0])
    def _():
        @pl.when((v == 0) | (ve_ref[v] != ve_ref[jnp.maximum(v - 1, 0)]))
        def _():
            wgb_ref[...] = wg_ref[...].astype(BF16)
            wub_ref[...] = wu_ref[...].astype(BF16)
            wdb_ref[...] = wd_ref[...].astype(BF16)

        x = xs_ref[...].astype(BF16)
        g = _dot(x, wgb_ref[...])
        u = _dot(x, wub_ref[...])
        y = _dot((g * jax.nn.sigmoid(g) * u).astype(BF16), wdb_ref[...])
        row = lax.broadcasted_iota(I32, (y.shape[0], 1), 0)
        mine = (row >= vlo_ref[v]) & (row < vhi_ref[v])

        @pl.when(vfirst_ref[v] == 1)
        def _():
            ys_ref[...] = jnp.where(mine, y, 0.0)

        @pl.when(vfirst_ref[v] == 0)
        def _():
            ys_ref[...] = jnp.where(mine, y, ys_ref[...])


def _gmm(xs, visits, layer, wg, wu, wd):
    rows, d = xs.shape
    tm = MOE_TILE
    hid = wg.shape[-1]
    nvis = rows // tm + MOE_EXPERTS - 1
    tile = lambda v, vt, *_: (vt[v], 0)
    wspec = lambda r, c: pl.BlockSpec((None, None, r, c), lambda v, vt, ve, *_: (layer, ve[v], 0, 0))
    return pl.pallas_call(
        _gmm_kernel,
        out_shape=jax.ShapeDtypeStruct((rows, d), F32),
        grid_spec=pltpu.PrefetchScalarGridSpec(
            num_scalar_prefetch=6, grid=(nvis,),
            in_specs=[pl.BlockSpec((tm, d), tile), wspec(d, hid), wspec(d, hid), wspec(hid, d)],
            out_specs=pl.BlockSpec((tm, d), tile),
            scratch_shapes=[pltpu.VMEM((d, hid), BF16), pltpu.VMEM((d, hid), BF16), pltpu.VMEM((hid, d), BF16)]),
        compiler_params=_cparams(("arbitrary",)),
    )(*visits, xs, wg, wu, wd)


def _combine_kernel(x_ref, wt_ref, mod_ref, fg_ref, y0_ref, y1_ref, o_ref):
    d = D_MODEL
    wt = wt_ref[...]
    y = wt[:, 0:1] * y0_ref[...] + wt[:, 1:2] * y1_ref[...]
    o_ref[...] = _rms(x_ref[...] + mod_ref[:, 5 * d:6 * d] * y, fg_ref[...])


def _combine(x2, wt, mod, rows_per_mod, fg, yt):
    n, d = x2.shape
    tm = MOE_TILE
    ntiles = n // tm
    return pl.pallas_call(
        _combine_kernel,
        out_shape=jax.ShapeDtypeStruct((n, d), F32),
        grid=(ntiles,),
        in_specs=[pl.BlockSpec((tm, d), lambda i: (i, 0)), pl.BlockSpec((tm, 8), lambda i: (i, 0)),
                  pl.BlockSpec((None, 1, 6 * d), lambda i: (i // (rows_per_mod // tm), 0, 0)),
                  pl.BlockSpec(fg.shape, lambda i: (0, 0)),
                  pl.BlockSpec((tm, d), lambda i: (i, 0)), pl.BlockSpec((tm, d), lambda i: (ntiles + i, 0))],
        out_specs=pl.BlockSpec((tm, d), lambda i: (i, 0)),
        compiler_params=_cparams(("parallel",)),
    )(x2, wt, mod, fg, yt, yt)


def _pick(table, idx):
    hot = idx[..., None] == jnp.arange(table.shape[0], dtype=I32)
    return jnp.sum(jnp.where(hot, table, 0), axis=-1)


def _moe_plan(counts, e, r, rows):
    tm = MOE_TILE
    ends = jnp.cumsum(counts)
    starts = ends - counts
    dest = (_pick(starts, e) + r).reshape(-1)
    first_tile = starts // tm
    nvis = jnp.where(counts > 0, (ends - 1) // tm - first_tile + 1, 0)
    vend = jnp.cumsum(nvis)
    nv = vend[-1:]
    v = jnp.minimum(jnp.arange(rows // tm + MOE_EXPERTS - 1, dtype=I32), nv[0] - 1)
    ve = jnp.sum((vend[None, :] <= v[:, None]).astype(I32), axis=1)
    vt = _pick(first_tile, ve) + v - _pick(vend - nvis, ve)
    vlo = jnp.maximum(_pick(starts, ve) - vt * tm, 0)
    vhi = jnp.minimum(_pick(ends, ve) - vt * tm, tm)
    vfirst = jnp.concatenate([jnp.ones((1,), I32), (vt[1:] != vt[:-1]).astype(I32)])
    return dest, (vt, ve, vlo, vhi, vfirst, nv)


def _moe_experts(x, xc, mod, ctx_row, gn, wr, u, layer, wg, wu, wd):
    b, n, d = x.shape
    x2 = x.reshape(b * n, d)
    xc2 = None if xc is None else xc.reshape(-1, d)
    nt = x2.shape[0] + (0 if xc is None else xc2.shape[0])
    h, e, wt, r, cnt = _router(x2, xc2, mod, n, ctx_row, gn, wr, u)
    dest, visits = _moe_plan(cnt[:, 0].astype(I32), e, r, 2 * nt)
    xs = _sc_permute_rows(h, dest, scatter=True)
    ys = _gmm(xs, visits, layer, wg, wu, wd)
    return _sc_permute_rows(ys, dest, scatter=False), wt


def _rope_tables(rows, dim):
    row = jnp.repeat(jnp.arange(rows, dtype=F32), GRID_W)
    col = jnp.tile(jnp.arange(GRID_W, dtype=F32), rows)
    half = dim // 2
    inv = jnp.power(ROPE_BASE, -jnp.arange(0, half, 2, dtype=F32) / half)
    ar = row[:, None] * inv[None, :]
    ac = col[:, None] * inv[None, :]
    ang = jnp.concatenate([ar, ar, ac, ac], axis=-1)
    return jnp.cos(ang), jnp.sin(ang)


def _even_tables(n, with_rope):
    if with_rope:
        cm, sm = _rope_tables(n // GRID_W, MLA_ROPE)
        cwin, swin = _rope_tables(n // GRID_W, WIN_HEAD_DIM)
    else:
        cm, sm = jnp.ones((n, MLA_ROPE), F32), jnp.zeros((n, MLA_ROPE), F32)
        cwin, swin = jnp.ones((n, WIN_HEAD_DIM), F32), jnp.zeros((n, WIN_HEAD_DIM), F32)
    one, zero = jnp.ones((n, MLA_NOPE), F32), jnp.zeros((n, MLA_NOPE), F32)
    pad = jnp.zeros((n, LANES - MLA_NOPE - MLA_ROPE), F32)
    return (jnp.concatenate([one, cm, pad], axis=1), jnp.concatenate([zero, sm, pad], axis=1),
            jnp.concatenate([cwin, cwin], axis=1), jnp.concatenate([swin, swin], axis=1))


def _even_weights(w_in, qg, w_uq, kvg, w_ukv):
    d = w_in.shape[0]
    o = np.cumsum([0, MLA_Q_RANK, MLA_KV_RANK, MLA_ROPE, 512, 128, 128])
    cq, ckv, kr, qw, kw, vw = [w_in[:, o[i]:o[i + 1]] for i in range(6)]
    z = lambda c: jnp.zeros((d, c), F32)
    kr128 = jnp.concatenate([z(MLA_NOPE), kr, z(LANES - MLA_NOPE - MLA_ROPE)], axis=1)
    dup = lambda t: jnp.concatenate([t[:, 0:64], t[:, 0:64], t[:, 64:128], t[:, 64:128]], axis=1)
    win = jnp.concatenate([cq, ckv, kr128, qw, dup(kw), dup(vw)], axis=1).astype(BF16)
    uq = w_uq.reshape(MLA_Q_RANK, MLA_HEADS, MLA_NOPE + MLA_ROPE)
    uq = jnp.pad(uq, ((0, 0), (0, 0), (0, LANES - MLA_NOPE - MLA_ROPE))).reshape(MLA_Q_RANK, MLA_HEADS * LANES)
    ukv = w_ukv.reshape(MLA_KV_RANK, MLA_HEADS, MLA_NOPE + MLA_V)
    ukk = jnp.pad(ukv[:, :, :MLA_NOPE], ((0, 0), (0, 0), (0, LANES - MLA_NOPE))).reshape(MLA_KV_RANK, MLA_HEADS * LANES)
    ukvv = ukv[:, :, MLA_NOPE:].reshape(MLA_KV_RANK, MLA_HEADS * MLA_V)
    return (win, qg.reshape(1, -1), uq.astype(BF16), kvg.reshape(1, -1), ukk.astype(BF16), ukvv.T.astype(BF16))


def _odd_weights(w_in, w_g2, b_g, ln_g, ln_b, w_s, b_s):
    d = w_in.shape[0]
    o = np.cumsum([0, GLA_K, GLA_K, GLA_V, 2 * GLA_GATE_RANK, GLA_V, SG_WIDTH, SG_WIDTH])
    q, k, v, g, r, u, vg = [w_in[:, o[i]:o[i + 1]] for i in range(7)]
    g128 = jnp.concatenate([g, jnp.zeros((d, LANES - 2 * GLA_GATE_RANK), F32)], axis=1)
    win = jnp.concatenate([q, k, v, g128, r, u, vg], axis=1).astype(BF16)
    zr = jnp.zeros((GLA_GATE_RANK, GLA_K), F32)
    pad = jnp.zeros((LANES - 2 * GLA_GATE_RANK, GLA_K), F32)
    wg = jnp.concatenate([jnp.concatenate([w_g2[0], zr, pad], axis=0),
                          jnp.concatenate([zr, w_g2[1], pad], axis=0)], axis=1)
    bg = b_g.reshape(1, 2 * GLA_K)
    return (win, wg, bg, ln_g.reshape(1, -1), ln_b.reshape(1, -1), w_s.astype(BF16), b_s.T)


def kernel(x, c, ctx, c_ctx, ada_w, ada_b, norm_mix_g, norm_ffn_g, even_w_in, mla_q_norm_g, mla_w_uq, mla_kv_norm_g, mla_w_ukv, win_sink, even_w_out, odd_w_in, gla_w_g2, gla_b_g, gla_norm_g, sg_ln_g, sg_ln_b, sg_w_s, sg_b_s, odd_w_out, moe_w_rg, moe_w_re, moe_w_gate, moe_w_up, moe_w_down, final_norm_g):
    b, n, d = x.shape
    lc = ctx.shape[1]
    depth = ada_w.shape[0]
    assert depth == 2 and d == D_MODEL and b < 8
    assert n % 512 == 0 and lc % MOE_TILE == 0 and n % GRID_W == 0
    tm = 512 if n % 512 == 0 else 256
    tq = 256

    cond8 = jnp.concatenate([c, c_ctx[None, :], jnp.zeros((8 - b - 1, d), F32)], axis=0)
    mod_all = _adaln(cond8, ada_w, ada_b).reshape(depth, 8, 1, 6 * d)
    ctx_row = b
    u_tri = jnp.asarray(np.triu(np.ones((MOE_TILE, MOE_TILE), np.float32), 1), BF16)
    fg = final_norm_g.reshape(1, d)

    def router_w(layer):
        return jnp.concatenate([moe_w_rg[layer].T, jnp.zeros((8 - MOE_GROUPS, d), F32), moe_w_re[layer].T], axis=0)

    def moe_experts(xx, xx_ctx, layer):
        return _moe_experts(xx, xx_ctx, mod_all[layer], ctx_row, norm_ffn_g[layer].reshape(1, d), router_w(layer),
                            u_tri, layer, moe_w_gate, moe_w_up, moe_w_down)

    mod = mod_all[0]
    gn = norm_mix_g[0].reshape(1, d)
    ew = _even_weights(even_w_in[0], mla_q_norm_g[0], mla_w_uq[0], mla_kv_norm_g[0], mla_w_ukv[0])
    qm_l, km_l, vm_l, qw_l, kw_l, vw_l = _even_in(x, mod, None, gn, ew, _even_tables(n, True), tm)
    qm_c, km_c, vm_c, qw_c, kw_c, vw_c = _even_in(ctx, mod, ctx_row, gn, ew, _even_tables(lc, False), lc)
    w_out = even_w_out[0].astype(BF16)
    sink = win_sink[0]
    oa_l = _mla_attn(qm_l, [(km_l, vm_l), (km_c, vm_c)], tq)
    ob_l = _gqa(qw_l, kw_l, vw_l, kw_c, vw_c, sink, True)
    xl = _even_out(x, oa_l, ob_l, w_out, mod, None, tm)
    oa_c = _mla_attn(qm_c, [(km_c, vm_c)], lc)
    ob_c = _gqa(qw_c, None, None, kw_c, vw_c, sink, False)
    xc = _even_out(ctx, oa_c, ob_c, w_out, mod, ctx_row, lc)
    pending = moe_experts(xl, xc, 0)

    mod = mod_all[1]
    gn = norm_mix_g[1].reshape(1, d)
    ow = _odd_weights(odd_w_in[0], gla_w_g2[0], gla_b_g[0], sg_ln_g[0], sg_ln_b[0], sg_w_s[0], sg_b_s[0])
    xl, q_l, k_l, v_l, la_l, r_l, dl_l = _odd_in(xl, pending, 0, mod_all[0], mod, None, gn, ow)
    _, q_c, k_c, v_c, la_c, _, _ = _odd_in(xc, pending, b * n // MOE_TILE, mod_all[0], mod, ctx_row, gn, ow)
    cumq_np, cumkt_np, pm_np, nlev = _gla_tables()
    cumq, cumkt = jnp.asarray(cumq_np, BF16), jnp.asarray(cumkt_np, BF16)
    pm = jnp.asarray(pm_np, F32)
    s0 = jnp.zeros((b, 2, GLA_HEADS, GLA_DV, LANES), F32)
    _, _, s_ctx = _gla(q_c, k_c, v_c, la_c, s0, cumq, cumkt, pm, nlev)
    o_fwd, o_bwd, _ = _gla(q_l, k_l, v_l, la_l, s_ctx, cumq, cumkt, pm, nlev)
    xl = _odd_out(xl, o_fwd, o_bwd, r_l, dl_l, gla_norm_g[0].reshape(1, -1), odd_w_out[0].astype(BF16), mod, tm)
    yt, wt = moe_experts(xl, None, 1)
    return _combine(xl.reshape(b * n, d), wt, mod, n, fg, yt).reshape(b, n, d)
```

```python
import functools

import numpy as np
import jax
import jax.numpy as jnp
from jax import lax
from jax.experimental import pallas as pl
from jax.experimental.pallas import tpu as pltpu
from jax.experimental.pallas import tpu_sc as plsc

F32 = jnp.float32
BF16 = jnp.bfloat16
I32 = jnp.int32

D_MODEL = 1024
GRID_W = 64
EPS = 1e-6
ROPE_BASE = 10000.0
MLA_HEADS = 8
MLA_Q_RANK = 256
MLA_KV_RANK = 128
MLA_NOPE = 64
MLA_ROPE = 32
MLA_V = 64
WIN_HEADS = 8
WIN_KV_HEADS = 2
WIN_HEAD_DIM = 64
WIN_BLOCK = 128
GLA_HEADS = 4
GLA_DK = 64
GLA_DV = 128
GLA_GATE_RANK = 16
GLA_TAU = 16.0
GLA_K = GLA_HEADS * GLA_DK
GLA_V = GLA_HEADS * GLA_DV
SG_GROUPS = 4
SG_CHUNK = 128
SG_WIDTH = 512
MOE_GROUPS = 4
MOE_PER_GROUP = 8
MOE_EXPERTS = 32
MOE_HIDDEN = 512

LANES = 128
GLA_BLOCK = 128
GLA_BATCHES_PER_STEP = 2
MOE_TILE = 256
MLA_KEY_CHUNK = 512
SC_CHUNK = 64
NEG = -1e30
LOG2E = 1.4426950408889634
VMEM_LIMIT = 56 * 1024 * 1024


def _cparams(sem):
    return pltpu.CompilerParams(dimension_semantics=sem, vmem_limit_bytes=VMEM_LIMIT)


def _dot(a, b):
    return jnp.dot(a, b, preferred_element_type=F32)


def _dot_nt(a, b):
    return lax.dot_general(a, b, (((1,), (1,)), ((), ())), preferred_element_type=F32)


def _split2(a):
    hi = a.astype(BF16)
    lo = (a - hi.astype(F32)).astype(BF16)
    return hi, lo


def _split3(a):
    hi = a.astype(BF16)
    r = a - hi.astype(F32)
    mid = r.astype(BF16)
    lo = (r - mid.astype(F32)).astype(BF16)
    return hi, mid, lo


def _rms(x, g):
    ms = jnp.mean(x * x, axis=-1, keepdims=True)
    return x * lax.rsqrt(ms + EPS) * g


def _lane_tile(t, reps):
    return t if reps == 1 else jnp.concatenate([t] * reps, axis=1)


def _rope(t, cos, sin, quarter):
    n = t.shape[1]
    lane = lax.broadcasted_iota(I32, t.shape, 1)
    first = (lane & (2 * quarter - 1)) < quarter
    rot = jnp.where(first, -pltpu.roll(t, n - quarter, 1), pltpu.roll(t, quarter, 1))
    return t * cos + rot * sin


def _adaln_kernel(c_ref, w_ref, b_ref, o_ref):
    c = c_ref[...]
    s_hi, s_lo = _split2(c * jax.nn.sigmoid(c))
    w_hi, w_lo = _split2(w_ref[...])
    o_ref[...] = _dot(s_hi, w_hi) + _dot(s_lo, w_hi) + _dot(s_hi, w_lo) + b_ref[...]


def _adaln(cond8, ada_w, ada_b):
    depth, d, n6 = ada_w.shape
    tn = 1536
    return pl.pallas_call(
        _adaln_kernel,
        out_shape=jax.ShapeDtypeStruct((depth, 8, n6), F32),
        grid=(depth, n6 // tn),
        in_specs=[
            pl.BlockSpec((8, d), lambda l, j: (0, 0)),
            pl.BlockSpec((None, d, tn), lambda l, j: (l, 0, j)),
            pl.BlockSpec((None, 1, tn), lambda l, j: (l, 0, j)),
        ],
        out_specs=pl.BlockSpec((None, 8, tn), lambda l, j: (l, 0, j)),
        compiler_params=_cparams(("parallel", "parallel")),
    )(cond8, ada_w, ada_b.reshape(depth, 1, n6))


def _even_in_kernel(x_ref, mod_ref, gn_ref, win_ref, qg_ref, wuq_ref, kvg_ref, wukk_ref, wukv_ref,
                    cq_ref, sq_ref, cw_ref, sw_ref,
                    qm_ref, km_ref, vm_ref, qw_ref, kw_ref, vw_ref):
    d = D_MODEL
    mod = mod_ref[...]
    h = _rms(x_ref[...], gn_ref[...]) * (1.0 + mod[:, d:2 * d]) + mod[:, 0:d]
    z = _dot(h.astype(BF16), win_ref[...])
    cq, sq, cw, sw = cq_ref[...], sq_ref[...], cw_ref[...], sw_ref[...]
    cqn = _rms(z[:, 0:256], qg_ref[...]).astype(BF16)
    q = _dot(cqn, wuq_ref[...])
    q = _rope(q, _lane_tile(cq, 8), _lane_tile(sq, 8), MLA_ROPE // 4)
    qm_ref[...] = (q * (LOG2E * (MLA_NOPE + MLA_ROPE) ** -0.5)).astype(BF16)
    ckvn = _rms(z[:, 256:384], kvg_ref[...]).astype(BF16)
    kn = _dot(ckvn, wukk_ref[...])
    kr = _rope(z[:, 384:512], cq, sq, MLA_ROPE // 4)
    km_ref[...] = (kn + _lane_tile(kr, 8)).astype(BF16)
    vm_ref[...] = _dot_nt(wukv_ref[...], ckvn).astype(BF16)
    qw = _rope(z[:, 512:1024], _lane_tile(cw, 4), _lane_tile(sw, 4), WIN_HEAD_DIM // 4)
    qw_ref[...] = (qw * (WIN_HEAD_DIM ** -0.5)).astype(BF16)
    kw = _rope(z[:, 1024:1280], _lane_tile(cw, 2), _lane_tile(sw, 2), WIN_HEAD_DIM // 4)
    kw_ref[...] = kw.astype(BF16)
    vw_ref[...] = z[:, 1280:1536].astype(BF16)


def _even_in(x, mod, mod_row, gn, wts, tabs, tm):
    b, n, d = x.shape
    win, qg, wuq, kvg, wukk, wukv = wts
    nt = n // tm
    row = (lambda bi, i: (bi, 0, 0)) if mod_row is None else (lambda bi, i: (mod_row, 0, 0))
    full = lambda a: pl.BlockSpec(a.shape, lambda bi, i: (0,) * a.ndim)
    tab = pl.BlockSpec((tm, LANES), lambda bi, i: (i, 0))
    outw = (1024, 1024, None, 512, 256, 256)
    rowspec = lambda w: pl.BlockSpec((None, tm, w), lambda bi, i: (bi, i, 0))
    colspec = pl.BlockSpec((None, 512, tm), lambda bi, i: (bi, 0, i))
    return pl.pallas_call(
        _even_in_kernel,
        out_shape=[jax.ShapeDtypeStruct((b, 512, n) if w is None else (b, n, w), BF16) for w in outw],
        grid=(b, nt),
        in_specs=[pl.BlockSpec((None, tm, d), lambda bi, i: (bi, i, 0)),
                  pl.BlockSpec((None, 1, 6 * d), row),
                  full(gn), full(win), full(qg), full(wuq), full(kvg), full(wukk), full(wukv),
                  tab, tab, tab, tab],
        out_specs=[colspec if w is None else rowspec(w) for w in outw],
        compiler_params=_cparams(("parallel", "parallel")),
    )(x, mod, gn, win, qg, wuq, kvg, wukk, wukv, *tabs)


def _mla_attn_kernel(nseg, q_ref, *refs):
    ks, vts = refs[0:2 * nseg:2], refs[1:2 * nseg:2]
    o_ref = refs[2 * nseg]
    s_bufs = refs[2 * nseg + 1:2 * nseg + 3]
    p_bufs = refs[2 * nseg + 3:2 * nseg + 5]
    pieces, base = [], 0
    for k in ks:
        n = k.shape[0]
        pieces += [(k, c0, min(n, c0 + MLA_KEY_CHUNK), base + c0) for c0 in range(0, n, MLA_KEY_CHUNK)]
        base += n

    def score_chunk(h, piece, buf):
        k, c0, c1, g0 = piece
        hs = slice(h * LANES, (h + 1) * LANES)
        half = (c1 - c0) // 2
        maxes = []
        for a in (c0, c0 + half):
            s = _dot_nt(k[a:a + half, hs], q_ref[:, hs])
            buf[g0 + a - c0:g0 + a - c0 + half, :] = s
            maxes.append(jnp.max(s, axis=0, keepdims=True))
        return jnp.maximum(*maxes)

    def prob_chunk(piece, sbuf, pbuf, m):
        _, c0, c1, g0 = piece
        p = jnp.exp2(sbuf[g0:g0 + c1 - c0, :] - m)
        pbuf[g0:g0 + c1 - c0, :] = p.astype(BF16)
        return jnp.sum(p, axis=0, keepdims=True)

    m_next = functools.reduce(jnp.maximum, [score_chunk(0, pc, s_bufs[0]) for pc in pieces])
    outs = []
    for h in range(MLA_HEADS):
        m_cur, maxes, sums = m_next, [], []
        for pc in pieces:
            if h + 1 < MLA_HEADS:
                maxes.append(score_chunk(h + 1, pc, s_bufs[(h + 1) % 2]))
            sums.append(prob_chunk(pc, s_bufs[h % 2], p_bufs[h % 2], m_cur))
        if h + 1 < MLA_HEADS:
            m_next = functools.reduce(jnp.maximum, maxes)
        l = functools.reduce(jnp.add, sums)
        vrows = slice(h * MLA_V, (h + 1) * MLA_V)
        ot, base = None, 0
        for k, vt in zip(ks, vts):
            n = k.shape[0]
            part = _dot(vt[vrows, :], p_bufs[h % 2][base:base + n, :])
            ot = part if ot is None else ot + part
            base += n
        outs.append(ot * (1.0 / l))
    o_ref[...] = jnp.concatenate(outs, axis=0).T.astype(BF16)


def _mla_attn(q, segs, tq):
    b, n, _ = q.shape
    in_specs = [pl.BlockSpec((None, tq, 1024), lambda bi, i: (bi, i, 0))]
    args = [q]
    keys = 0
    for k, vt in segs:
        lk = k.shape[1]
        keys += lk
        in_specs += [pl.BlockSpec((None, lk, 1024), lambda bi, i: (bi, 0, 0)),
                     pl.BlockSpec((None, 512, lk), lambda bi, i: (bi, 0, 0))]
        args += [k, vt]
    return pl.pallas_call(
        functools.partial(_mla_attn_kernel, len(segs)),
        out_shape=jax.ShapeDtypeStruct((b, n, 512), BF16),
        grid=(b, n // tq),
        in_specs=in_specs,
        out_specs=pl.BlockSpec((None, tq, 512), lambda bi, i: (bi, i, 0)),
        scratch_shapes=[pltpu.VMEM((keys, tq), F32), pltpu.VMEM((keys, tq), F32),
                        pltpu.VMEM((keys, tq), BF16), pltpu.VMEM((keys, tq), BF16)],
        compiler_params=_cparams(("parallel", "parallel")),
    )(*args)


def _gqa_kernel(has_win, nb, sink_ref, q_ref, *refs):
    if has_win:
        kp, kc, kn, vp, vc, vn, kx, vx, o_ref = refs
    else:
        kx, vx, o_ref = refs
    tq = q_ref.shape[0]
    i = pl.program_id(1)
    lane = lax.broadcasted_iota(I32, (tq, LANES), 1)
    row2 = lax.broadcasted_iota(I32, (2 * tq, 1), 0)
    half = WIN_HEAD_DIM
    npair = WIN_HEADS // 2
    kcats, vcats = [], []
    for g in range(WIN_KV_HEADS):
        gs = slice(g * LANES, (g + 1) * LANES)
        if has_win:
            kcats.append(jnp.concatenate([kp[:, gs], kc[:, gs], kn[:, gs], kx[:, gs]], axis=0))
            vcats.append(jnp.concatenate([vp[:, gs], vc[:, gs], vn[:, gs], vx[:, gs]], axis=0))
        else:
            kcats.append(kx[:, gs])
            vcats.append(vx[:, gs])
    scores = []
    for j in range(npair):
        qp = q_ref[:, j * LANES:(j + 1) * LANES]
        zero = jnp.zeros_like(qp)
        q2 = jnp.concatenate([jnp.where(lane < half, qp, zero), jnp.where(lane >= half, qp, zero)], axis=0)
        scores.append(_dot_nt(q2, kcats[j // 2]))
    if has_win:
        w = WIN_BLOCK
        r = lax.broadcasted_iota(I32, scores[0].shape, 0) & (tq - 1)
        c = lax.broadcasted_iota(I32, scores[0].shape, 1)
        big = jnp.int32(1 << 20)
        no_prev = jnp.where(i > 0, 0, big)
        no_next = jnp.where(i < nb - 1, 0, big)
        ok_prev = c >= r + no_prev
        ok_next = (c - 2 * w) <= r - no_next
        valid = ((c >= w) | ok_prev) & ((c < 2 * w) | (c >= 3 * w) | ok_next)
    probs, inv = [], []
    for j in range(npair):
        s = jnp.where(valid, scores[j], NEG) if has_win else scores[j]
        sk = jnp.where(row2 < tq, sink_ref[2 * j], sink_ref[2 * j + 1])
        m = jnp.maximum(jnp.max(s, axis=-1, keepdims=True), sk)
        p = jnp.exp(s - m)
        inv.append(1.0 / (jnp.sum(p, axis=-1, keepdims=True) + jnp.exp(sk - m)))
        probs.append(p.astype(BF16))
    for j in range(npair):
        o2 = _dot(probs[j], vcats[j // 2]) * inv[j]
        o_ref[:, j * LANES:(j + 1) * LANES] = jnp.where(lane < half, o2[:tq], o2[tq:]).astype(BF16)


def _gqa(q, k, v, kx, vx, sink, has_win):
    b, n, _ = q.shape
    lc = kx.shape[1]
    smem = pl.BlockSpec(memory_space=pltpu.SMEM)
    ctxs = pl.BlockSpec((None, lc, 256), lambda bi, i: (bi, 0, 0))
    if has_win:
        tq = WIN_BLOCK
        nb = n // tq
        blk = lambda f: pl.BlockSpec((None, tq, 256), f)
        prev = lambda bi, i: (bi, jnp.maximum(i - 1, 0), 0)
        cur = lambda bi, i: (bi, i, 0)
        nxt = lambda bi, i: (bi, jnp.minimum(i + 1, nb - 1), 0)
        in_specs = [smem, pl.BlockSpec((None, tq, 512), cur),
                    blk(prev), blk(cur), blk(nxt), blk(prev), blk(cur), blk(nxt), ctxs, ctxs]
        args = (sink, q, k, k, k, v, v, v, kx, vx)
    else:
        tq, nb = n, 1
        in_specs = [smem, pl.BlockSpec((None, tq, 512), lambda bi, i: (bi, i, 0)), ctxs, ctxs]
        args = (sink, q, kx, vx)
    return pl.pallas_call(
        functools.partial(_gqa_kernel, has_win, nb),
        out_shape=jax.ShapeDtypeStruct((b, n, 512), BF16),
        grid=(b, nb),
        in_specs=in_specs,
        out_specs=pl.BlockSpec((None, tq, 512), lambda bi, i: (bi, i, 0)),
        compiler_params=_cparams(("parallel", "parallel")),
    )(*args)


def _even_out_kernel(x_ref, a_ref, b_ref, w_ref, mod_ref, o_ref):
    d = D_MODEL
    y = _dot(a_ref[...], w_ref[0:512, :]) + _dot(b_ref[...], w_ref[512:1024, :])
    o_ref[...] = x_ref[...] + mod_ref[:, 2 * d:3 * d] * y


def _even_out(x, oa, ob, w, mod, mod_row, tm):
    b, n, d = x.shape
    row = (lambda bi, i: (bi, 0, 0)) if mod_row is None else (lambda bi, i: (mod_row, 0, 0))
    act = lambda wd: pl.BlockSpec((None, tm, wd), lambda bi, i: (bi, i, 0))
    return pl.pallas_call(
        _even_out_kernel,
        out_shape=jax.ShapeDtypeStruct((b, n, d), F32),
        grid=(b, n // tm),
        in_specs=[act(d), act(512), act(512), pl.BlockSpec(w.shape, lambda bi, i: (0, 0)),
                  pl.BlockSpec((None, 1, 6 * d), row)],
        out_specs=act(d),
        compiler_params=_cparams(("parallel", "parallel")),
    )(x, oa, ob, w, mod)


def _log_sigmoid(z):
    return jnp.minimum(z, 0.0) - jnp.log(1.0 + jnp.exp(-jnp.abs(z)))


def _odd_in_kernel(x_ref, y0_ref, y1_ref, wt_ref, modp_ref, mod_ref, gn_ref, win_ref, wg_ref, bg_ref, lng_ref,
                   lnb_ref, ws_ref, bst_ref, xn_ref, q_ref, k_ref, v_ref, la_ref, r_ref, dl_ref):
    d = D_MODEL
    tm = x_ref.shape[0]
    wt = wt_ref[...]
    x = x_ref[...] + modp_ref[:, 5 * d:6 * d] * (wt[:, 0:1] * y0_ref[...] + wt[:, 1:2] * y1_ref[...])
    xn_ref[...] = x
    mod = mod_ref[...]
    h = (_rms(x, gn_ref[...]) * (1.0 + mod[:, d:2 * d]) + mod[:, 0:d]).astype(BF16)
    z = _dot(h, win_ref[...])
    q_ref[...] = z[:, 0:256] * (GLA_DK ** -0.5)
    k_ref[...] = z[:, 256:512]
    v_ref[...] = z[:, 512:1024].astype(BF16)
    g_hi, g_lo = _split2(z[:, 1024:1152])
    w_hi, w_lo = _split2(wg_ref[...])
    zg = _dot(g_hi, w_hi) + _dot(g_lo, w_hi) + _dot(g_hi, w_lo) + bg_ref[...]
    la_ref[...] = _log_sigmoid(zg) / GLA_TAU
    r_ref[...] = z[:, 1152:1664]
    u = jax.nn.gelu(z[:, 1664:2176])
    vg = jax.nn.gelu(z[:, 2176:2688])
    mu = jnp.mean(vg, axis=-1, keepdims=True)
    vc = vg - mu
    var = jnp.mean(vc * vc, axis=-1, keepdims=True)
    vn = (vc * lax.rsqrt(var + EPS) * lng_ref[...] + lnb_ref[...]).astype(BF16)
    bst = bst_ref[...]
    for c in range(tm // SG_CHUNK):
        rows = slice(c * SG_CHUNK, (c + 1) * SG_CHUNK)
        parts = []
        for g in range(SG_GROUPS):
            cols = slice(g * LANES, (g + 1) * LANES)
            parts.append(_dot(ws_ref[g], vn[rows, cols]) + bst[:, g:g + 1])
        dl_ref[rows, :] = (u[rows, :] * jnp.concatenate(parts, axis=1)).astype(BF16)


def _odd_in(x, pending, tile0, modp, mod, mod_row, gn, wts):
    b, n, d = x.shape
    tm = MOE_TILE
    yt, wt = pending
    ntiles = wt.shape[0] // tm
    win, wg, bg, lng, lnb, ws, bst = wts
    row = (lambda bi, i: (bi, 0, 0)) if mod_row is None else (lambda bi, i: (mod_row, 0, 0))
    full = lambda a: pl.BlockSpec(a.shape, lambda bi, i: (0,) * a.ndim)
    act = lambda wd: pl.BlockSpec((None, tm, wd), lambda bi, i: (bi, i, 0))
    tok = lambda bi, i: tile0 + bi * (n // tm) + i
    outs = [((b, n, d), F32, act(d)),
            ((b, n, 256), F32, act(256)), ((b, n, 256), F32, act(256)), ((b, n, 512), BF16, act(512)),
            ((b, n, 512), F32, act(512)), ((b, n, 512), F32, act(512)), ((b, n, 512), BF16, act(512))]
    return pl.pallas_call(
        _odd_in_kernel,
        out_shape=[jax.ShapeDtypeStruct(s, t) for s, t, _ in outs],
        grid=(b, n // tm),
        in_specs=[act(d), pl.BlockSpec((tm, d), lambda bi, i: (tok(bi, i), 0)),
                  pl.BlockSpec((tm, d), lambda bi, i: (ntiles + tok(bi, i), 0)),
                  pl.BlockSpec((tm, 8), lambda bi, i: (tok(bi, i), 0)),
                  pl.BlockSpec((None, 1, 6 * d), row), pl.BlockSpec((None, 1, 6 * d), row),
                  full(gn), full(win), full(wg), full(bg), full(lng), full(lnb), full(ws), full(bst)],
        out_specs=[sp for _, _, sp in outs],
        compiler_params=_cparams(("parallel", "parallel")),
    )(x, yt, yt, wt, modp, mod, gn, win, wg, bg, lng, lnb, ws, bst)


def _gla_tables():
    c = GLA_BLOCK
    t = np.arange(c)[:, None]
    u = np.arange(c)[None, :]
    levels = [c >> i for i in range(int(np.log2(c)) + 1)]
    cum = np.zeros((2, 2 * len(levels), c, c), np.float32)
    pair = np.zeros((2, len(levels), c, c), np.float32)
    for li, m in enumerate(levels):
        same = (t // m) == (u // m)
        cum[0, 2 * li] = same & (u <= t)
        cum[0, 2 * li + 1] = same & (u > t)
        cum[1, 2 * li] = same & (u >= t)
        cum[1, 2 * li + 1] = same & (u < t)
        if li > 0:
            pair[0, li] = ((t // m) % 2 == 1) & ((u // m) == (t // m) - 1)
            pair[1, li] = ((t // m) % 2 == 0) & ((u // m) == (t // m) + 1)
    pair[:, 0] = np.eye(c, dtype=np.float32)
    nlev = len(levels)
    m1 = cum[:, 0::2].reshape(2, nlev * c, c)
    m2t = np.concatenate([cum[:, 2 * li + 1].transpose(0, 2, 1) for li in range(nlev)], axis=2)
    return np.concatenate([m1, m1], axis=2), np.concatenate([m2t, m2t], axis=1), pair, nlev


def _gla_chain(nlev, q, k, la, v_ref, cumq, cumkt, pm_ref, st_ref, o_ref):
    c = GLA_BLOCK
    lat, kt = la.T, k.T
    l_hi, l_mid = _split2(la)
    t_hi, t_mid = _split2(lat)
    exq = jnp.exp(_dot(cumq, jnp.concatenate([l_hi, l_mid], axis=0)))
    exk = jnp.exp(_dot(jnp.concatenate([t_hi, t_mid], axis=1), cumkt))
    gcol = jnp.exp(jnp.sum(lat, axis=1, keepdims=True))
    yield
    qe = [(q * exq[li * c:(li + 1) * c]).astype(BF16) for li in range(nlev)]
    ke = [(kt * exk[:, li * c:(li + 1) * c]).astype(BF16) for li in range(nlev)]
    qb, kb = q.astype(BF16), kt.astype(BF16)
    states = [st_ref[hd] for hd in range(GLA_HEADS)]
    yield
    outs, new_states = [], []
    lane = lax.broadcasted_iota(I32, (c, LANES), 1)
    srow = lax.broadcasted_iota(I32, (LANES, 1), 0)
    zero = jnp.zeros((c, LANES), BF16)
    for hd in range(GLA_HEADS):
        ps = slice((hd // 2) * LANES, (hd // 2 + 1) * LANES)
        vs = slice(hd * GLA_DV, (hd + 1) * GLA_DV)
        mine = (lane < GLA_DK) if hd % 2 == 0 else (lane >= GLA_DK)
        mine_row = (srow < GLA_DK) if hd % 2 == 0 else (srow >= GLA_DK)
        pick = lambda t: jnp.where(mine, t[:, ps], zero)
        a = pm_ref[0] * _dot(pick(qb), kb[ps, :])
        for li in range(1, nlev):
            a = a + pm_ref[li] * _dot(pick(qe[li]), ke[li][ps, :])
        v_h = v_ref[:, vs]
        outs.append(_dot(qe[0][:, ps], states[hd].astype(BF16)) + _dot(a.astype(BF16), v_h))
        new_states.append(states[hd] * gcol[ps, :] + jnp.where(mine_row, _dot(ke[0][ps, :], v_h), 0.0))
        yield
    o_ref[...] = jnp.concatenate(outs, axis=1)
    for hd in range(GLA_HEADS):
        st_ref[hd] = new_states[hd]
    yield


def _gla_kernel(nlev, nb, *refs):
    ins_f, ins_b = refs[0:4], refs[4:8]
    cumq_ref, cumkt_ref, pm_ref, s0_ref, of_ref, ob_ref, sf_ref = refs[8:15]
    st_refs = refs[15:]
    step = pl.program_id(1)

    @pl.when(step == 0)
    def _():
        for bb in range(nb):
            for d_ in range(2):
                st_refs[2 * bb + d_][...] = s0_ref[bb, d_]

    chains = []
    for bb in range(nb):
        for d_, (ins, o_ref) in enumerate(((ins_f, of_ref), (ins_b, ob_ref))):
            q_ref, k_ref, v_ref, la_ref = ins
            chains.append(_gla_chain(nlev, q_ref[bb], k_ref[bb], la_ref[bb], v_ref.at[bb], cumq_ref[d_],
                                     cumkt_ref[d_], pm_ref.at[d_], st_refs[2 * bb + d_], o_ref.at[bb]))
    for _ in range(GLA_HEADS + 3):
        for ch in chains:
            next(ch)
    for bb in range(nb):
        for d_ in range(2):
            sf_ref[bb, d_] = st_refs[2 * bb + d_][...]


def _gla(q, k, v, la, s0, cumq, cumkt, pm, nlev):
    b, n, _ = q.shape
    c = GLA_BLOCK
    nc = n // c
    nb = next(c for c in (GLA_BATCHES_PER_STEP, 2, 1) if b % c == 0)
    specs = []
    for d_ in range(2):
        pos = (lambda s_: s_) if d_ == 0 else (lambda s_: nc - 1 - s_)
        specs += [pl.BlockSpec((nb, c, 256), lambda bi, s_, pos=pos: (bi, pos(s_), 0)),
                  pl.BlockSpec((nb, c, 256), lambda bi, s_, pos=pos: (bi, pos(s_), 0)),
                  pl.BlockSpec((nb, c, 512), lambda bi, s_, pos=pos: (bi, pos(s_), 0)),
                  pl.BlockSpec((nb, c, 256), lambda bi, s_, pos=pos, d_=d_: (bi, pos(s_), d_))]
    st_spec = pl.BlockSpec((nb, 2, GLA_HEADS, GLA_DV, LANES), lambda bi, s_: (bi, 0, 0, 0, 0))
    full = lambda a: pl.BlockSpec(a.shape, lambda bi, s_: (0,) * a.ndim)
    return pl.pallas_call(
        functools.partial(_gla_kernel, nlev, nb),
        out_shape=[jax.ShapeDtypeStruct((b, n, GLA_V), F32), jax.ShapeDtypeStruct((b, n, GLA_V), F32),
                   jax.ShapeDtypeStruct((b, 2, GLA_HEADS, GLA_DV, LANES), F32)],
        grid=(b // nb, nc),
        in_specs=specs + [full(cumq), full(cumkt), full(pm), st_spec],
        out_specs=[pl.BlockSpec((nb, c, GLA_V), lambda bi, s_: (bi, s_, 0)),
                   pl.BlockSpec((nb, c, GLA_V), lambda bi, s_: (bi, nc - 1 - s_, 0)), st_spec],
        scratch_shapes=[pltpu.VMEM((GLA_HEADS, GLA_DV, LANES), F32) for _ in range(2 * nb)],
        compiler_params=_cparams(("parallel", "arbitrary")),
    )(q, k, v, la, q, k, v, la, cumq, cumkt, pm, s0)


def _odd_out_kernel(x_ref, of_ref, ob_ref, r_ref, dl_ref, gg_ref, w_ref, mod_ref, o_ref):
    d = D_MODEL
    o = of_ref[...] + ob_ref[...]
    gg = gg_ref[...]
    r = r_ref[...]
    parts = []
    for hd in range(GLA_HEADS):
        vs = slice(hd * GLA_DV, (hd + 1) * GLA_DV)
        oh = o[:, vs]
        parts.append(oh * lax.rsqrt(jnp.mean(oh * oh, axis=-1, keepdims=True) + EPS) * gg[:, vs])
    cl = (jnp.concatenate(parts, axis=1) * (r * jax.nn.sigmoid(r))).astype(BF16)
    y = _dot(cl, w_ref[0:512, :]) + _dot(dl_ref[...], w_ref[512:1024, :])
    o_ref[...] = x_ref[...] + mod_ref[:, 2 * d:3 * d] * y


def _odd_out(x, o_fwd, o_bwd, r, dl, gg, w, mod, tm):
    b, n, d = x.shape
    act = lambda wd: pl.BlockSpec((None, tm, wd), lambda bi, i: (bi, i, 0))
    return pl.pallas_call(
        _odd_out_kernel,
        out_shape=jax.ShapeDtypeStruct((b, n, d), F32),
        grid=(b, n // tm),
        in_specs=[act(d), act(GLA_V), act(GLA_V), act(512), act(512),
                  pl.BlockSpec(gg.shape, lambda bi, i: (0, 0)), pl.BlockSpec(w.shape, lambda bi, i: (0, 0)),
                  pl.BlockSpec((None, 1, 6 * d), lambda bi, i: (bi, 0, 0))],
        out_specs=act(d),
        compiler_params=_cparams(("parallel", "parallel")),
    )(x, o_fwd, o_bwd, r, dl, gg, w, mod)


def _router_kernel(nlat, x_ref, xc_ref, mod_ref, gn_ref, wr_ref, u_ref, h_ref, e_ref, wt_ref, r_ref, cnt_ref,
                   carry_ref):
    d = D_MODEL
    tm = x_ref.shape[0]
    i = pl.program_id(0)

    @pl.when(i == 0)
    def _():
        carry_ref[...] = jnp.zeros_like(carry_ref)

    mod = mod_ref[...]
    x = x_ref[...] if xc_ref is None else jnp.where(i < nlat, x_ref[...], xc_ref[...])
    h = _rms(x, gn_ref[...]) * (1.0 + mod[:, 4 * d:5 * d]) + mod[:, 3 * d:4 * d]
    h_ref[...] = h
    h_hi, h_lo = _split2(h)
    w_hi, w_lo = _split2(wr_ref[...])
    lg = _dot_nt(w_hi, h_hi) + _dot_nt(w_lo, h_hi) + _dot_nt(w_hi, h_lo)
    rid = lax.broadcasted_iota(I32, (8, tm), 0)
    gl = jnp.where(rid < MOE_GROUPS, lg[0:8], NEG)
    gmax = jnp.max(gl, axis=0, keepdims=True)
    gsel = jnp.min(jnp.where(gl == gmax, rid, 8), axis=0, keepdims=True)
    pmax = 1.0 / jnp.sum(jnp.where(rid < MOE_GROUPS, jnp.exp(gl - gmax), 0.0), axis=0, keepdims=True)
    e_in = jnp.zeros((MOE_PER_GROUP, tm), F32)
    for g in range(MOE_GROUPS):
        e_in = e_in + jnp.where(gsel == g, lg[8 + 8 * g:16 + 8 * g], 0.0)
    v1 = jnp.max(e_in, axis=0, keepdims=True)
    i1 = jnp.min(jnp.where(e_in == v1, rid, 8), axis=0, keepdims=True)
    e_rest = jnp.where(rid == i1, -jnp.inf, e_in)
    v2 = jnp.max(e_rest, axis=0, keepdims=True)
    i2 = jnp.min(jnp.where(e_rest == v2, rid, 8), axis=0, keepdims=True)
    t = jnp.exp(v2 - v1)
    w1 = pmax / (1.0 + t)
    w2 = pmax * t / (1.0 + t)
    e1 = gsel * MOE_PER_GROUP + i1
    e2 = gsel * MOE_PER_GROUP + i2
    eid = lax.broadcasted_iota(I32, (MOE_EXPERTS, tm), 0)
    oh1 = jnp.where(eid == e1, 1.0, 0.0)
    oh2 = jnp.where(eid == e2, 1.0, 0.0)
    ohs = oh1 + oh2
    base = carry_ref[:, 0:1] + _dot(ohs.astype(BF16), u_ref[...])
    r1 = jnp.sum(oh1 * base, axis=0, keepdims=True)
    r2 = jnp.sum(oh2 * base, axis=0, keepdims=True)
    carry_ref[...] = carry_ref[...] + jnp.sum(ohs, axis=1, keepdims=True)
    cnt_ref[...] = carry_ref[...]
    e_ref[...] = jnp.concatenate([e1, e2], axis=0)
    r_ref[...] = jnp.concatenate([r1, r2], axis=0).astype(I32)
    w8 = jnp.concatenate([w1, w2, jnp.zeros((6, tm), F32)], axis=0)
    wt_ref[...] = w8.T


def _stream_specs(x2, xc2, rows_per_mod, ctx_row):
    tm = MOE_TILE
    d = x2.shape[1]
    nlat = x2.shape[0] // tm
    lat = lambda i, *_: (jnp.minimum(i, nlat - 1), 0)
    ctx = lambda i, *_: (jnp.maximum(i - nlat, 0), 0)
    if xc2 is None:
        modrow = lambda i, *_: (i // (rows_per_mod // tm), 0, 0)
    else:
        modrow = lambda i, *_: (jnp.where(i < nlat, i // (rows_per_mod // tm), ctx_row), 0, 0)
    specs = [pl.BlockSpec((tm, d), lat)] + ([] if xc2 is None else [pl.BlockSpec((tm, d), ctx)])
    return nlat, specs, pl.BlockSpec((None, 1, 6 * d), modrow)


def _router(x2, xc2, mod, rows_per_mod, ctx_row, gn, wr, u):
    d = x2.shape[1]
    tm = MOE_TILE
    n = x2.shape[0] + (0 if xc2 is None else xc2.shape[0])
    nlat, xspecs, modspec = _stream_specs(x2, xc2, rows_per_mod, ctx_row)
    body = functools.partial(_router_kernel, nlat)
    if xc2 is None:
        body = lambda x_ref, *rest: _router_kernel(nlat, x_ref, None, *rest)
    return pl.pallas_call(
        body,
        out_shape=[jax.ShapeDtypeStruct((n, d), F32), jax.ShapeDtypeStruct((2, n), I32),
                   jax.ShapeDtypeStruct((n, 8), F32), jax.ShapeDtypeStruct((2, n), I32),
                   jax.ShapeDtypeStruct((MOE_EXPERTS, LANES), F32)],
        grid=(n // tm,),
        in_specs=xspecs + [modspec,
                  pl.BlockSpec(gn.shape, lambda i: (0, 0)), pl.BlockSpec(wr.shape, lambda i: (0, 0)),
                  pl.BlockSpec(u.shape, lambda i: (0, 0))],
        out_specs=[pl.BlockSpec((tm, d), lambda i: (i, 0)), pl.BlockSpec((2, tm), lambda i: (0, i)),
                   pl.BlockSpec((tm, 8), lambda i: (i, 0)), pl.BlockSpec((2, tm), lambda i: (0, i)),
                   pl.BlockSpec((MOE_EXPERTS, LANES), lambda i: (0, 0))],
        scratch_shapes=[pltpu.VMEM((MOE_EXPERTS, LANES), F32)],
        compiler_params=_cparams(("arbitrary",)),
    )(*([x2] if xc2 is None else [x2, xc2]), mod, gn, wr, u)


def _sc_permute_rows(src, dest, scatter):
    rows, d = dest.shape[0], src.shape[1]
    n = rows // 2
    info = plsc.get_sparse_core_info()
    workers = info.num_cores * info.num_subcores
    per_worker = rows // workers
    assert rows == per_worker * workers and per_worker % SC_CHUNK == 0 and n % per_worker == 0
    mesh = plsc.VectorSubcoreMesh(core_axis_name="c", subcore_axis_name="s")

    def body(src_hbm, dest_hbm, out_hbm, idx_v, rows_v, sem):
        base = (lax.axis_index("s") * info.num_cores + lax.axis_index("c")) * per_worker

        @pl.loop(0, per_worker // SC_CHUNK)
        def _(j):
            a0 = base + j * SC_CHUNK
            pltpu.sync_copy(dest_hbm.at[pl.ds(a0, SC_CHUNK)], idx_v)
            if scatter:
                t0 = jnp.where(a0 >= n, a0 - n, a0)
                pltpu.sync_copy(src_hbm.at[pl.ds(t0, SC_CHUNK)], rows_v)
                pltpu.async_copy(rows_v, out_hbm.at[idx_v], sem).wait()
            else:
                pltpu.async_copy(src_hbm.at[idx_v], rows_v, sem).wait()
                pltpu.sync_copy(rows_v, out_hbm.at[pl.ds(a0, SC_CHUNK)])

    return pl.kernel(
        body, out_type=jax.ShapeDtypeStruct((rows, d), F32), mesh=mesh,
        scratch_types=[pltpu.VMEM((SC_CHUNK,), I32), pltpu.VMEM((SC_CHUNK, d), F32), pltpu.SemaphoreType.DMA],
    )(src, dest)


def _gmm_kernel(layer, vt_ref, ve_ref, vlo_ref, vhi_ref, vfirst_ref, vslot_ref, vnext_ref, nv_ref,
                xs_ref, wg_hbm, wu_hbm, wd_hbm, ys_ref, wgs_ref, wus_ref, wds_ref, wgb_ref, wub_ref, wdb_ref, sem):
    del vt_ref
    v = pl.program_id(0)

    def fetch(e, slot):
        return [pltpu.make_async_copy(w.at[layer, e], s.at[slot], sem.at[slot, i])
                for i, (w, s) in enumerate(((wg_hbm, wgs_ref), (wu_hbm, wus_ref), (wd_hbm, wds_ref)))]

    @pl.when(v < nv_ref[0])
    def _():
        @pl.when((v == 0) | (ve_ref[v] != ve_ref[jnp.maximum(v - 1, 0)]))
        def _():
            slot = vslot_ref[v]

            @pl.when(v == 0)
            def _():
                for c in fetch(ve_ref[0], 0):
                    c.start()

            for c in fetch(ve_ref[v], slot):
                c.wait()
            wgb_ref[...] = wgs_ref[slot].astype(BF16)
            wub_ref[...] = wus_ref[slot].astype(BF16)
            wdb_ref[...] = wds_ref[slot].astype(BF16)

            @pl.when(vnext_ref[v] >= 0)
            def _():
                for c in fetch(vnext_ref[v], 1 - slot):
                    c.start()

        x = xs_ref[...].astype(BF16)
        g = _dot(x, wgb_ref[...])
        u = _dot(x, wub_ref[...])
        y = _dot((g * jax.nn.sigmoid(g) * u).astype(BF16), wdb_ref[...])
        row = lax.broadcasted_iota(I32, (y.shape[0], 1), 0)
        mine = (row >= vlo_ref[v]) & (row < vhi_ref[v])

        @pl.when(vfirst_ref[v] == 1)
        def _():
            ys_ref[...] = jnp.where(mine, y, 0.0)

        @pl.when(vfirst_ref[v] == 0)
        def _():
            ys_ref[...] = jnp.where(mine, y, ys_ref[...])


def _gmm(xs, visits, layer, wg, wu, wd):
    rows, d = xs.shape
    tm = MOE_TILE
    hid = wg.shape[-1]
    nvis = rows // tm + MOE_EXPERTS - 1
    tile = lambda v, vt, *_: (vt[v], 0)
    anyspec = pl.BlockSpec(memory_space=pl.ANY)
    return pl.pallas_call(
        functools.partial(_gmm_kernel, layer),
        out_shape=jax.ShapeDtypeStruct((rows, d), F32),
        grid_spec=pltpu.PrefetchScalarGridSpec(
            num_scalar_prefetch=8, grid=(nvis,),
            in_specs=[pl.BlockSpec((tm, d), tile), anyspec, anyspec, anyspec],
            out_specs=pl.BlockSpec((tm, d), tile),
            scratch_shapes=[pltpu.VMEM((2, d, hid), F32), pltpu.VMEM((2, d, hid), F32), pltpu.VMEM((2, hid, d), F32),
                            pltpu.VMEM((d, hid), BF16), pltpu.VMEM((d, hid), BF16), pltpu.VMEM((hid, d), BF16),
                            pltpu.SemaphoreType.DMA((2, 3))]),
        compiler_params=_cparams(("arbitrary",)),
    )(*visits, xs, wg, wu, wd)


def _combine_kernel(x_ref, wt_ref, mod_ref, fg_ref, y0_ref, y1_ref, o_ref):
    d = D_MODEL
    wt = wt_ref[...]
    y = wt[:, 0:1] * y0_ref[...] + wt[:, 1:2] * y1_ref[...]
    o_ref[...] = _rms(x_ref[...] + mod_ref[:, 5 * d:6 * d] * y, fg_ref[...])


def _combine(x2, wt, mod, rows_per_mod, fg, yt):
    n, d = x2.shape
    tm = MOE_TILE
    ntiles = n // tm
    return pl.pallas_call(
        _combine_kernel,
        out_shape=jax.ShapeDtypeStruct((n, d), F32),
        grid=(ntiles,),
        in_specs=[pl.BlockSpec((tm, d), lambda i: (i, 0)), pl.BlockSpec((tm, 8), lambda i: (i, 0)),
                  pl.BlockSpec((None, 1, 6 * d), lambda i: (i // (rows_per_mod // tm), 0, 0)),
                  pl.BlockSpec(fg.shape, lambda i: (0, 0)),
                  pl.BlockSpec((tm, d), lambda i: (i, 0)), pl.BlockSpec((tm, d), lambda i: (ntiles + i, 0))],
        out_specs=pl.BlockSpec((tm, d), lambda i: (i, 0)),
        compiler_params=_cparams(("parallel",)),
    )(x2, wt, mod, fg, yt, yt)


def _pick(table, idx):
    hot = idx[..., None] == jnp.arange(table.shape[0], dtype=I32)
    return jnp.sum(jnp.where(hot, table, 0), axis=-1)


def _moe_plan(counts, e, r, rows):
    tm = MOE_TILE
    ends = jnp.cumsum(counts)
    starts = ends - counts
    dest = (_pick(starts, e) + r).reshape(-1)
    first_tile = starts // tm
    nvis = jnp.where(counts > 0, (ends - 1) // tm - first_tile + 1, 0)
    vend = jnp.cumsum(nvis)
    nv = vend[-1:]
    v = jnp.minimum(jnp.arange(rows // tm + MOE_EXPERTS - 1, dtype=I32), nv[0] - 1)
    ve = jnp.sum((vend[None, :] <= v[:, None]).astype(I32), axis=1)
    vt = _pick(first_tile, ve) + v - _pick(vend - nvis, ve)
    vlo = jnp.maximum(_pick(starts, ve) - vt * tm, 0)
    vhi = jnp.minimum(_pick(ends, ve) - vt * tm, tm)
    vfirst = jnp.concatenate([jnp.ones((1,), I32), (vt[1:] != vt[:-1]).astype(I32)])
    changed = jnp.concatenate([jnp.ones((1,), I32), (ve[1:] != ve[:-1]).astype(I32)])
    vslot = (jnp.cumsum(changed) - 1) % 2
    eid = jnp.arange(MOE_EXPERTS, dtype=I32)
    later = (eid[None, :] > eid[:, None]) & (counts[None, :] > 0)
    nxt = jnp.min(jnp.where(later, eid[None, :], MOE_EXPERTS), axis=1)
    vnext = _pick(jnp.where(nxt < MOE_EXPERTS, nxt, -1), ve)
    return dest, (vt, ve, vlo, vhi, vfirst, vslot.astype(I32), vnext.astype(I32), nv)


def _moe_experts(x, xc, mod, ctx_row, gn, wr, u, layer, wg, wu, wd):
    b, n, d = x.shape
    x2 = x.reshape(b * n, d)
    xc2 = None if xc is None else xc.reshape(-1, d)
    nt = x2.shape[0] + (0 if xc is None else xc2.shape[0])
    h, e, wt, r, cnt = _router(x2, xc2, mod, n, ctx_row, gn, wr, u)
    dest, visits = _moe_plan(cnt[:, 0].astype(I32), e, r, 2 * nt)
    xs = _sc_permute_rows(h, dest, scatter=True)
    ys = _gmm(xs, visits, layer, wg, wu, wd)
    return _sc_permute_rows(ys, dest, scatter=False), wt


def _rope_tables(rows, dim):
    row = jnp.repeat(jnp.arange(rows, dtype=F32), GRID_W)
    col = jnp.tile(jnp.arange(GRID_W, dtype=F32), rows)
    half = dim // 2
    inv = jnp.power(ROPE_BASE, -jnp.arange(0, half, 2, dtype=F32) / half)
    ar = row[:, None] * inv[None, :]
    ac = col[:, None] * inv[None, :]
    ang = jnp.concatenate([ar, ar, ac, ac], axis=-1)
    return jnp.cos(ang), jnp.sin(ang)


def _even_tables(n, with_rope):
    if with_rope:
        cm, sm = _rope_tables(n // GRID_W, MLA_ROPE)
        cwin, swin = _rope_tables(n // GRID_W, WIN_HEAD_DIM)
    else:
        cm, sm = jnp.ones((n, MLA_ROPE), F32), jnp.zeros((n, MLA_ROPE), F32)
        cwin, swin = jnp.ones((n, WIN_HEAD_DIM), F32), jnp.zeros((n, WIN_HEAD_DIM), F32)
    one, zero = jnp.ones((n, MLA_NOPE), F32), jnp.zeros((n, MLA_NOPE), F32)
    pad = jnp.zeros((n, LANES - MLA_NOPE - MLA_ROPE), F32)
    return (jnp.concatenate([one, cm, pad], axis=1), jnp.concatenate([zero, sm, pad], axis=1),
            jnp.concatenate([cwin, cwin], axis=1), jnp.concatenate([swin, swin], axis=1))


def _even_weights(w_in, qg, w_uq, kvg, w_ukv):
    d = w_in.shape[0]
    o = np.cumsum([0, MLA_Q_RANK, MLA_KV_RANK, MLA_ROPE, 512, 128, 128])
    cq, ckv, kr, qw, kw, vw = [w_in[:, o[i]:o[i + 1]] for i in range(6)]
    z = lambda c: jnp.zeros((d, c), F32)
    kr128 = jnp.concatenate([z(MLA_NOPE), kr, z(LANES - MLA_NOPE - MLA_ROPE)], axis=1)
    dup = lambda t: jnp.concatenate([t[:, 0:64], t[:, 0:64], t[:, 64:128], t[:, 64:128]], axis=1)
    win = jnp.concatenate([cq, ckv, kr128, qw, dup(kw), dup(vw)], axis=1).astype(BF16)
    uq = w_uq.reshape(MLA_Q_RANK, MLA_HEADS, MLA_NOPE + MLA_ROPE)
    uq = jnp.pad(uq, ((0, 0), (0, 0), (0, LANES - MLA_NOPE - MLA_ROPE))).reshape(MLA_Q_RANK, MLA_HEADS * LANES)
    ukv = w_ukv.reshape(MLA_KV_RANK, MLA_HEADS, MLA_NOPE + MLA_V)
    ukk = jnp.pad(ukv[:, :, :MLA_NOPE], ((0, 0), (0, 0), (0, LANES - MLA_NOPE))).reshape(MLA_KV_RANK, MLA_HEADS * LANES)
    ukvv = ukv[:, :, MLA_NOPE:].reshape(MLA_KV_RANK, MLA_HEADS * MLA_V)
    return (win, qg.reshape(1, -1), uq.astype(BF16), kvg.reshape(1, -1), ukk.astype(BF16), ukvv.T.astype(BF16))


def _odd_weights(w_in, w_g2, b_g, ln_g, ln_b, w_s, b_s):
    d = w_in.shape[0]
    o = np.cumsum([0, GLA_K, GLA_K, GLA_V, 2 * GLA_GATE_RANK, GLA_V, SG_WIDTH, SG_WIDTH])
    q, k, v, g, r, u, vg = [w_in[:, o[i]:o[i + 1]] for i in range(7)]
    g128 = jnp.concatenate([g, jnp.zeros((d, LANES - 2 * GLA_GATE_RANK), F32)], axis=1)
    win = jnp.concatenate([q, k, v, g128, r, u, vg], axis=1).astype(BF16)
    zr = jnp.zeros((GLA_GATE_RANK, GLA_K), F32)
    pad = jnp.zeros((LANES - 2 * GLA_GATE_RANK, GLA_K), F32)
    wg = jnp.concatenate([jnp.concatenate([w_g2[0], zr, pad], axis=0),
                          jnp.concatenate([zr, w_g2[1], pad], axis=0)], axis=1)
    bg = b_g.reshape(1, 2 * GLA_K)
    return (win, wg, bg, ln_g.reshape(1, -1), ln_b.reshape(1, -1), w_s.astype(BF16), b_s.T)


def kernel(x, c, ctx, c_ctx, ada_w, ada_b, norm_mix_g, norm_ffn_g, even_w_in, mla_q_norm_g, mla_w_uq, mla_kv_norm_g, mla_w_ukv, win_sink, even_w_out, odd_w_in, gla_w_g2, gla_b_g, gla_norm_g, sg_ln_g, sg_ln_b, sg_w_s, sg_b_s, odd_w_out, moe_w_rg, moe_w_re, moe_w_gate, moe_w_up, moe_w_down, final_norm_g):
    b, n, d = x.shape
    lc = ctx.shape[1]
    depth = ada_w.shape[0]
    assert depth == 2 and d == D_MODEL and b < 8
    assert n % 512 == 0 and lc % MOE_TILE == 0 and n % GRID_W == 0
    tm = 512 if n % 512 == 0 else 256
    tq = 256

    cond8 = jnp.concatenate([c, c_ctx[None, :], jnp.zeros((8 - b - 1, d), F32)], axis=0)
    mod_all = _adaln(cond8, ada_w, ada_b).reshape(depth, 8, 1, 6 * d)
    ctx_row = b
    u_tri = jnp.asarray(np.triu(np.ones((MOE_TILE, MOE_TILE), np.float32), 1), BF16)
    fg = final_norm_g.reshape(1, d)

    def router_w(layer):
        return jnp.concatenate([moe_w_rg[layer].T, jnp.zeros((8 - MOE_GROUPS, d), F32), moe_w_re[layer].T], axis=0)

    def moe_experts(xx, xx_ctx, layer):
        return _moe_experts(xx, xx_ctx, mod_all[layer], ctx_row, norm_ffn_g[layer].reshape(1, d), router_w(layer),
                            u_tri, layer, moe_w_gate, moe_w_up, moe_w_down)

    mod = mod_all[0]
    gn = norm_mix_g[0].reshape(1, d)
    ew = _even_weights(even_w_in[0], mla_q_norm_g[0], mla_w_uq[0], mla_kv_norm_g[0], mla_w_ukv[0])
    qm_l, km_l, vm_l, qw_l, kw_l, vw_l = _even_in(x, mod, None, gn, ew, _even_tables(n, True), tm)
    qm_c, km_c, vm_c, qw_c, kw_c, vw_c = _even_in(ctx, mod, ctx_row, gn, ew, _even_tables(lc, False), lc)
    w_out = even_w_out[0].astype(BF16)
    sink = win_sink[0]
    oa_l = _mla_attn(qm_l, [(km_l, vm_l), (km_c, vm_c)], tq)
    ob_l = _gqa(qw_l, kw_l, vw_l, kw_c, vw_c, sink, True)
    xl = _even_out(x, oa_l, ob_l, w_out, mod, None, tm)
    oa_c = _mla_attn(qm_c, [(km_c, vm_c)], lc)
    ob_c = _gqa(qw_c, None, None, kw_c, vw_c, sink, False)
    xc = _even_out(ctx, oa_c, ob_c, w_out, mod, ctx_row, lc)
    pending = moe_experts(xl, xc, 0)

    mod = mod_all[1]
    gn = norm_mix_g[1].reshape(1, d)
    ow = _odd_weights(odd_w_in[0], gla_w_g2[0], gla_b_g[0], sg_ln_g[0], sg_ln_b[0], sg_w_s[0], sg_b_s[0])
    xl, q_l, k_l, v_l, la_l, r_l, dl_l = _odd_in(xl, pending, 0, mod_all[0], mod, None, gn, ow)
    _, q_c, k_c, v_c, la_c, _, _ = _odd_in(xc, pending, b * n // MOE_TILE, mod_all[0], mod, ctx_row, gn, ow)
    cumq_np, cumkt_np, pm_np, nlev = _gla_tables()
    cumq, cumkt = jnp.asarray(cumq_np, BF16), jnp.asarray(cumkt_np, BF16)
    pm = jnp.asarray(pm_np, F32)
    s0 = jnp.zeros((b, 2, GLA_HEADS, GLA_DV, LANES), F32)
    _, _, s_ctx = _gla(q_c, k_c, v_c, la_c, s0, cumq, cumkt, pm, nlev)
    o_fwd, o_bwd, _ = _gla(q_l, k_l, v_l, la_l, s_ctx, cumq, cumkt, pm, nlev)
    xl = _odd_out(xl, o_fwd, o_bwd, r_l, dl_l, gla_norm_g[0].reshape(1, -1), odd_w_out[0].astype(BF16), mod, tm)
    yt, wt = moe_experts(xl, None, 1)
    return _combine(xl.reshape(b * n, d), wt, mod, n, fg, yt).reshape(b, n, d)
```

```python
import functools

import numpy as np
import jax
import jax.numpy as jnp
from jax import lax
from jax.experimental import pallas as pl
from jax.experimental.pallas import tpu as pltpu
from jax.experimental.pallas import tpu_sc as plsc

F32 = jnp.float32
BF16 = jnp.bfloat16
I32 = jnp.int32

D_MODEL = 1024
GRID_W = 64
EPS = 1e-6
ROPE_BASE = 10000.0
MLA_HEADS = 8
MLA_Q_RANK = 256
MLA_KV_RANK = 128
MLA_NOPE = 64
MLA_ROPE = 32
MLA_V = 64
WIN_HEADS = 8
WIN_KV_HEADS = 2
WIN_HEAD_DIM = 64
WIN_BLOCK = 128
GLA_HEADS = 4
GLA_DK = 64
GLA_DV = 128
GLA_GATE_RANK = 16
GLA_TAU = 16.0
GLA_K = GLA_HEADS * GLA_DK
GLA_V = GLA_HEADS * GLA_DV
SG_GROUPS = 4
SG_CHUNK = 128
SG_WIDTH = 512
MOE_GROUPS = 4
MOE_PER_GROUP = 8
MOE_EXPERTS = 32
MOE_HIDDEN = 512

LANES = 128
GLA_BLOCK = 128
GLA_BATCHES_PER_STEP = 2
MOE_TILE = 256
MLA_KEY_CHUNK = 1024
SC_CHUNKS = (128, 64, 32)
NEG = -1e30
LOG2E = 1.4426950408889634
VMEM_LIMIT = 56 * 1024 * 1024


def _cparams(sem):
    return pltpu.CompilerParams(dimension_semantics=sem, vmem_limit_bytes=VMEM_LIMIT)


def _dot(a, b):
    return jnp.dot(a, b, preferred_element_type=F32)


def _dot_nt(a, b):
    return lax.dot_general(a, b, (((1,), (1,)), ((), ())), preferred_element_type=F32)


def _split2(a):
    hi = a.astype(BF16)
    lo = (a - hi.astype(F32)).astype(BF16)
    return hi, lo


def _split3(a):
    hi = a.astype(BF16)
    r = a - hi.astype(F32)
    mid = r.astype(BF16)
    lo = (r - mid.astype(F32)).astype(BF16)
    return hi, mid, lo


def _pack_bf16_pairs(x):
    k = x.shape[1] // 2
    bits = lax.bitcast_convert_type(x.astype(BF16).astype(F32), jnp.uint32)
    return lax.bitcast_convert_type(bits[:, :k] | (bits[:, k:] >> 16), I32)


def _unpack_bf16_pairs(w):
    bits = lax.bitcast_convert_type(w, jnp.uint32)
    hi = lax.bitcast_convert_type(bits & jnp.uint32(0xFFFF0000), F32)
    lo = lax.bitcast_convert_type(bits << 16, F32)
    return jnp.concatenate([hi, lo], axis=1)


def _rms(x, g):
    ms = jnp.mean(x * x, axis=-1, keepdims=True)
    return x * lax.rsqrt(ms + EPS) * g


def _lane_tile(t, reps):
    return t if reps == 1 else jnp.concatenate([t] * reps, axis=1)


def _rope(t, cos, sin, quarter):
    n = t.shape[1]
    lane = lax.broadcasted_iota(I32, t.shape, 1)
    first = (lane & (2 * quarter - 1)) < quarter
    rot = jnp.where(first, -pltpu.roll(t, n - quarter, 1), pltpu.roll(t, quarter, 1))
    return t * cos + rot * sin


def _adaln_kernel(c_ref, w_ref, b_ref, o_ref):
    c = c_ref[...]
    s_hi, s_lo = _split2(c * jax.nn.sigmoid(c))
    w_hi, w_lo = _split2(w_ref[...])
    o_ref[...] = _dot(s_hi, w_hi) + _dot(s_lo, w_hi) + _dot(s_hi, w_lo) + b_ref[...]


def _adaln(cond8, ada_w, ada_b):
    depth, d, n6 = ada_w.shape
    tn = 1536
    return pl.pallas_call(
        _adaln_kernel,
        out_shape=jax.ShapeDtypeStruct((depth, 8, n6), F32),
        grid=(depth, n6 // tn),
        in_specs=[
            pl.BlockSpec((8, d), lambda l, j: (0, 0)),
            pl.BlockSpec((None, d, tn), lambda l, j: (l, 0, j)),
            pl.BlockSpec((None, 1, tn), lambda l, j: (l, 0, j)),
        ],
        out_specs=pl.BlockSpec((None, 8, tn), lambda l, j: (l, 0, j)),
        compiler_params=_cparams(("parallel", "parallel")),
    )(cond8, ada_w, ada_b.reshape(depth, 1, n6))


def _even_in_kernel(x_ref, mod_ref, gn_ref, win_ref, qg_ref, wuq_ref, kvg_ref, wukk_ref, wukv_ref,
                    cq_ref, sq_ref, cw_ref, sw_ref,
                    qm_ref, km_ref, vm_ref, qw_ref, kw_ref, vw_ref):
    d = D_MODEL
    mod = mod_ref[...]
    h = _rms(x_ref[...], gn_ref[...]) * (1.0 + mod[:, d:2 * d]) + mod[:, 0:d]
    z = _dot(h.astype(BF16), win_ref[...])
    cq, sq, cw, sw = cq_ref[...], sq_ref[...], cw_ref[...], sw_ref[...]
    cqn = _rms(z[:, 0:256], qg_ref[...]).astype(BF16)
    q = _dot(cqn, wuq_ref[...])
    q = _rope(q, _lane_tile(cq, 8), _lane_tile(sq, 8), MLA_ROPE // 4)
    qm_ref[...] = (q * (LOG2E * (MLA_NOPE + MLA_ROPE) ** -0.5)).astype(BF16)
    ckvn = _rms(z[:, 256:384], kvg_ref[...]).astype(BF16)
    kn = _dot(ckvn, wukk_ref[...])
    kr = _rope(z[:, 384:512], cq, sq, MLA_ROPE // 4)
    km_ref[...] = (kn + _lane_tile(kr, 8)).astype(BF16)
    vm_ref[...] = _dot_nt(wukv_ref[...], ckvn).astype(BF16)
    qw = _rope(z[:, 512:1024], _lane_tile(cw, 4), _lane_tile(sw, 4), WIN_HEAD_DIM // 4)
    qw_ref[...] = (qw * (WIN_HEAD_DIM ** -0.5)).astype(BF16)
    kw = _rope(z[:, 1024:1280], _lane_tile(cw, 2), _lane_tile(sw, 2), WIN_HEAD_DIM // 4)
    kw_ref[...] = kw.astype(BF16)
    vw_ref[...] = z[:, 1280:1536].astype(BF16)


def _even_in(x, mod, mod_row, gn, wts, tabs, tm):
    b, n, d = x.shape
    win, qg, wuq, kvg, wukk, wukv = wts
    nt = n // tm
    row = (lambda bi, i: (bi, 0, 0)) if mod_row is None else (lambda bi, i: (mod_row, 0, 0))
    full = lambda a: pl.BlockSpec(a.shape, lambda bi, i: (0,) * a.ndim)
    tab = pl.BlockSpec((tm, LANES), lambda bi, i: (i, 0))
    outw = (1024, 1024, None, 512, 256, 256)
    rowspec = lambda w: pl.BlockSpec((None, tm, w), lambda bi, i: (bi, i, 0))
    colspec = pl.BlockSpec((None, 512, tm), lambda bi, i: (bi, 0, i))
    return pl.pallas_call(
        _even_in_kernel,
        out_shape=[jax.ShapeDtypeStruct((b, 512, n) if w is None else (b, n, w), BF16) for w in outw],
        grid=(b, nt),
        in_specs=[pl.BlockSpec((None, tm, d), lambda bi, i: (bi, i, 0)),
                  pl.BlockSpec((None, 1, 6 * d), row),
                  full(gn), full(win), full(qg), full(wuq), full(kvg), full(wukk), full(wukv),
                  tab, tab, tab, tab],
        out_specs=[colspec if w is None else rowspec(w) for w in outw],
        compiler_params=_cparams(("parallel", "parallel")),
    )(x, mod, gn, win, qg, wuq, kvg, wukk, wukv, *tabs)


def _mla_attn_kernel(nseg, q_ref, *refs):
    ks, vts = refs[0:2 * nseg:2], refs[1:2 * nseg:2]
    o_ref = refs[2 * nseg]
    s_bufs = refs[2 * nseg + 1:2 * nseg + 3]
    p_bufs = refs[2 * nseg + 3:2 * nseg + 5]
    pieces, base = [], 0
    for k in ks:
        n = k.shape[0]
        pieces += [(k, c0, min(n, c0 + MLA_KEY_CHUNK), base + c0) for c0 in range(0, n, MLA_KEY_CHUNK)]
        base += n

    def score_chunk(h, piece, buf):
        k, c0, c1, g0 = piece
        hs = slice(h * LANES, (h + 1) * LANES)
        s = _dot_nt(k[c0:c1, hs], q_ref[:, hs])
        buf[g0:g0 + c1 - c0, :] = s
        return jnp.max(s, axis=0, keepdims=True)

    def prob_chunk(piece, sbuf, pbuf, m):
        _, c0, c1, g0 = piece
        p = jnp.exp2(sbuf[g0:g0 + c1 - c0, :] - m)
        pbuf[g0:g0 + c1 - c0, :] = p.astype(BF16)
        return jnp.sum(p, axis=0, keepdims=True)

    m_next = functools.reduce(jnp.maximum, [score_chunk(0, pc, s_bufs[0]) for pc in pieces])
    outs = []
    for h in range(MLA_HEADS):
        m_cur, maxes, sums = m_next, [], []
        for pc in pieces:
            if h + 1 < MLA_HEADS:
                maxes.append(score_chunk(h + 1, pc, s_bufs[(h + 1) % 2]))
            sums.append(prob_chunk(pc, s_bufs[h % 2], p_bufs[h % 2], m_cur))
        if h + 1 < MLA_HEADS:
            m_next = functools.reduce(jnp.maximum, maxes)
        l = functools.reduce(jnp.add, sums)
        vrows = slice(h * MLA_V, (h + 1) * MLA_V)
        ot, base = None, 0
        for k, vt in zip(ks, vts):
            n = k.shape[0]
            part = _dot(vt[vrows, :], p_bufs[h % 2][base:base + n, :])
            ot = part if ot is None else ot + part
            base += n
        outs.append(ot * (1.0 / l))
    o_ref[...] = jnp.concatenate(outs, axis=0).T.astype(BF16)


def _mla_attn(q, segs, tq):
    b, n, _ = q.shape
    in_specs = [pl.BlockSpec((None, tq, 1024), lambda bi, i: (bi, i, 0))]
    args = [q]
    keys = 0
    for k, vt in segs:
        lk = k.shape[1]
        keys += lk
        in_specs += [pl.BlockSpec((None, lk, 1024), lambda bi, i: (bi, 0, 0)),
                     pl.BlockSpec((None, 512, lk), lambda bi, i: (bi, 0, 0))]
        args += [k, vt]
    return pl.pallas_call(
        functools.partial(_mla_attn_kernel, len(segs)),
        out_shape=jax.ShapeDtypeStruct((b, n, 512), BF16),
        grid=(b, n // tq),
        in_specs=in_specs,
        out_specs=pl.BlockSpec((None, tq, 512), lambda bi, i: (bi, i, 0)),
        scratch_shapes=[pltpu.VMEM((keys, tq), F32), pltpu.VMEM((keys, tq), F32),
                        pltpu.VMEM((keys, tq), BF16), pltpu.VMEM((keys, tq), BF16)],
        compiler_params=_cparams(("parallel", "parallel")),
    )(*args)


def _gqa_kernel(has_win, nb, sink_ref, q_ref, *refs):
    if has_win:
        kp, kc, kn, vp, vc, vn, kx, vx, o_ref = refs
    else:
        kx, vx, o_ref = refs
    tq = q_ref.shape[0]
    i = pl.program_id(1)
    lane = lax.broadcasted_iota(I32, (tq, LANES), 1)
    row2 = lax.broadcasted_iota(I32, (2 * tq, 1), 0)
    half = WIN_HEAD_DIM
    npair = WIN_HEADS // 2
    kcats, vcats = [], []
    for g in range(WIN_KV_HEADS):
        gs = slice(g * LANES, (g + 1) * LANES)
        if has_win:
            kcats.append(jnp.concatenate([kp[:, gs], kc[:, gs], kn[:, gs], kx[:, gs]], axis=0))
            vcats.append(jnp.concatenate([vp[:, gs], vc[:, gs], vn[:, gs], vx[:, gs]], axis=0))
        else:
            kcats.append(kx[:, gs])
            vcats.append(vx[:, gs])
    scores = []
    for j in range(npair):
        qp = q_ref[:, j * LANES:(j + 1) * LANES]
        zero = jnp.zeros_like(qp)
        q2 = jnp.concatenate([jnp.where(lane < half, qp, zero), jnp.where(lane >= half, qp, zero)], axis=0)
        scores.append(_dot_nt(q2, kcats[j // 2]))
    if has_win:
        w = WIN_BLOCK
        r = lax.broadcasted_iota(I32, scores[0].shape, 0) & (tq - 1)
        c = lax.broadcasted_iota(I32, scores[0].shape, 1)
        big = jnp.int32(1 << 20)
        no_prev = jnp.where(i > 0, 0, big)
        no_next = jnp.where(i < nb - 1, 0, big)
        ok_prev = c >= r + no_prev
        ok_next = (c - 2 * w) <= r - no_next
        valid = ((c >= w) | ok_prev) & ((c < 2 * w) | (c >= 3 * w) | ok_next)
    probs, inv = [], []
    for j in range(npair):
        s = jnp.where(valid, scores[j], NEG) if has_win else scores[j]
        sk = jnp.where(row2 < tq, sink_ref[2 * j], sink_ref[2 * j + 1])
        m = jnp.maximum(jnp.max(s, axis=-1, keepdims=True), sk)
        p = jnp.exp(s - m)
        inv.append(1.0 / (jnp.sum(p, axis=-1, keepdims=True) + jnp.exp(sk - m)))
        probs.append(p.astype(BF16))
    for j in range(npair):
        o2 = _dot(probs[j], vcats[j // 2]) * inv[j]
        o_ref[:, j * LANES:(j + 1) * LANES] = jnp.where(lane < half, o2[:tq], o2[tq:]).astype(BF16)


def _gqa(q, k, v, kx, vx, sink, has_win):
    b, n, _ = q.shape
    lc = kx.shape[1]
    smem = pl.BlockSpec(memory_space=pltpu.SMEM)
    ctxs = pl.BlockSpec((None, lc, 256), lambda bi, i: (bi, 0, 0))
    if has_win:
        tq = WIN_BLOCK
        nb = n // tq
        blk = lambda f: pl.BlockSpec((None, tq, 256), f)
        prev = lambda bi, i: (bi, jnp.maximum(i - 1, 0), 0)
        cur = lambda bi, i: (bi, i, 0)
        nxt = lambda bi, i: (bi, jnp.minimum(i + 1, nb - 1), 0)
        in_specs = [smem, pl.BlockSpec((None, tq, 512), cur),
                    blk(prev), blk(cur), blk(nxt), blk(prev), blk(cur), blk(nxt), ctxs, ctxs]
        args = (sink, q, k, k, k, v, v, v, kx, vx)
    else:
        tq, nb = n, 1
        in_specs = [smem, pl.BlockSpec((None, tq, 512), lambda bi, i: (bi, i, 0)), ctxs, ctxs]
        args = (sink, q, kx, vx)
    return pl.pallas_call(
        functools.partial(_gqa_kernel, has_win, nb),
        out_shape=jax.ShapeDtypeStruct((b, n, 512), BF16),
        grid=(b, nb),
        in_specs=in_specs,
        out_specs=pl.BlockSpec((None, tq, 512), lambda bi, i: (bi, i, 0)),
        compiler_params=_cparams(("parallel", "parallel")),
    )(*args)


def _even_out_kernel(x_ref, a_ref, b_ref, w_ref, mod_ref, o_ref):
    d = D_MODEL
    y = _dot(a_ref[...], w_ref[0:512, :]) + _dot(b_ref[...], w_ref[512:1024, :])
    o_ref[...] = x_ref[...] + mod_ref[:, 2 * d:3 * d] * y


def _even_out(x, oa, ob, w, mod, mod_row, tm):
    b, n, d = x.shape
    row = (lambda bi, i: (bi, 0, 0)) if mod_row is None else (lambda bi, i: (mod_row, 0, 0))
    act = lambda wd: pl.BlockSpec((None, tm, wd), lambda bi, i: (bi, i, 0))
    return pl.pallas_call(
        _even_out_kernel,
        out_shape=jax.ShapeDtypeStruct((b, n, d), F32),
        grid=(b, n // tm),
        in_specs=[act(d), act(512), act(512), pl.BlockSpec(w.shape, lambda bi, i: (0, 0)),
                  pl.BlockSpec((None, 1, 6 * d), row)],
        out_specs=act(d),
        compiler_params=_cparams(("parallel", "parallel")),
    )(x, oa, ob, w, mod)


def _log_sigmoid(z):
    return jnp.minimum(z, 0.0) - jnp.log(1.0 + jnp.exp(-jnp.abs(z)))


def _odd_in_kernel(x_ref, y0_ref, y1_ref, wt_ref, modp_ref, mod_ref, gn_ref, win_ref, wg_ref, bg_ref, lng_ref,
                   lnb_ref, ws_ref, bst_ref, xn_ref, q_ref, k_ref, v_ref, la_ref, r_ref, dl_ref):
    d = D_MODEL
    tm = x_ref.shape[0]
    wt = wt_ref[...]
    y = wt[:, 0:1] * _unpack_bf16_pairs(y0_ref[...]) + wt[:, 1:2] * _unpack_bf16_pairs(y1_ref[...])
    x = x_ref[...] + modp_ref[:, 5 * d:6 * d] * y
    xn_ref[...] = x
    mod = mod_ref[...]
    h = (_rms(x, gn_ref[...]) * (1.0 + mod[:, d:2 * d]) + mod[:, 0:d]).astype(BF16)
    z = _dot(h, win_ref[...])
    q_ref[...] = z[:, 0:256] * (GLA_DK ** -0.5)
    k_ref[...] = z[:, 256:512]
    v_ref[...] = z[:, 512:1024].astype(BF16)
    g_hi, g_lo = _split2(z[:, 1024:1152])
    w_hi, w_lo = _split2(wg_ref[...])
    zg = _dot(g_hi, w_hi) + _dot(g_lo, w_hi) + _dot(g_hi, w_lo) + bg_ref[...]
    la_ref[...] = _log_sigmoid(zg) / GLA_TAU
    r_ref[...] = z[:, 1152:1664]
    u = jax.nn.gelu(z[:, 1664:2176])
    vg = jax.nn.gelu(z[:, 2176:2688])
    mu = jnp.mean(vg, axis=-1, keepdims=True)
    vc = vg - mu
    var = jnp.mean(vc * vc, axis=-1, keepdims=True)
    vn = (vc * lax.rsqrt(var + EPS) * lng_ref[...] + lnb_ref[...]).astype(BF16)
    bst = bst_ref[...]
    for c in range(tm // SG_CHUNK):
        rows = slice(c * SG_CHUNK, (c + 1) * SG_CHUNK)
        parts = []
        for g in range(SG_GROUPS):
            cols = slice(g * LANES, (g + 1) * LANES)
            parts.append(_dot(ws_ref[g], vn[rows, cols]) + bst[:, g:g + 1])
        dl_ref[rows, :] = (u[rows, :] * jnp.concatenate(parts, axis=1)).astype(BF16)


def _odd_in(x, pending, tile0, modp, mod, mod_row, gn, wts):
    b, n, d = x.shape
    tm = MOE_TILE
    yt, wt = pending
    ntiles = wt.shape[0] // tm
    win, wg, bg, lng, lnb, ws, bst = wts
    row = (lambda bi, i: (bi, 0, 0)) if mod_row is None else (lambda bi, i: (mod_row, 0, 0))
    full = lambda a: pl.BlockSpec(a.shape, lambda bi, i: (0,) * a.ndim)
    act = lambda wd: pl.BlockSpec((None, tm, wd), lambda bi, i: (bi, i, 0))
    tok = lambda bi, i: tile0 + bi * (n // tm) + i
    outs = [((b, n, d), F32, act(d)),
            ((b, n, 256), F32, act(256)), ((b, n, 256), F32, act(256)), ((b, n, 512), BF16, act(512)),
            ((b, n, 512), F32, act(512)), ((b, n, 512), F32, act(512)), ((b, n, 512), BF16, act(512))]
    return pl.pallas_call(
        _odd_in_kernel,
        out_shape=[jax.ShapeDtypeStruct(s, t) for s, t, _ in outs],
        grid=(b, n // tm),
        in_specs=[act(d), pl.BlockSpec((tm, d // 2), lambda bi, i: (tok(bi, i), 0)),
                  pl.BlockSpec((tm, d // 2), lambda bi, i: (ntiles + tok(bi, i), 0)),
                  pl.BlockSpec((tm, 8), lambda bi, i: (tok(bi, i), 0)),
                  pl.BlockSpec((None, 1, 6 * d), row), pl.BlockSpec((None, 1, 6 * d), row),
                  full(gn), full(win), full(wg), full(bg), full(lng), full(lnb), full(ws), full(bst)],
        out_specs=[sp for _, _, sp in outs],
        compiler_params=_cparams(("parallel", "parallel")),
    )(x, yt, yt, wt, modp, mod, gn, win, wg, bg, lng, lnb, ws, bst)


def _gla_tables():
    c = GLA_BLOCK
    t = np.arange(c)[:, None]
    u = np.arange(c)[None, :]
    levels = [c >> i for i in range(int(np.log2(c)) + 1)]
    cum = np.zeros((2, 2 * len(levels), c, c), np.float32)
    pair = np.zeros((2, len(levels), c, c), np.float32)
    for li, m in enumerate(levels):
        same = (t // m) == (u // m)
        cum[0, 2 * li] = same & (u <= t)
        cum[0, 2 * li + 1] = same & (u > t)
        cum[1, 2 * li] = same & (u >= t)
        cum[1, 2 * li + 1] = same & (u < t)
        if li > 0:
            pair[0, li] = ((t // m) % 2 == 1) & ((u // m) == (t // m) - 1)
            pair[1, li] = ((t // m) % 2 == 0) & ((u // m) == (t // m) + 1)
    pair[:, 0] = np.eye(c, dtype=np.float32)
    nlev = len(levels)
    m1 = cum[:, 0::2].reshape(2, nlev * c, c)
    m2t = np.concatenate([cum[:, 2 * li + 1].transpose(0, 2, 1) for li in range(nlev)], axis=2)
    return np.concatenate([m1, m1], axis=2), np.concatenate([m2t, m2t], axis=1), pair, nlev


def _gla_chain(nlev, q, k, la, v_ref, cumq, cumkt, pm_ref, st_ref, o_ref):
    c = GLA_BLOCK
    lat, kt = la.T, k.T
    l_hi, l_mid = _split2(la)
    t_hi, t_mid = _split2(lat)
    exq = jnp.exp(_dot(cumq, jnp.concatenate([l_hi, l_mid], axis=0)))
    exk = jnp.exp(_dot(jnp.concatenate([t_hi, t_mid], axis=1), cumkt))
    gcol = jnp.exp(jnp.sum(lat, axis=1, keepdims=True))
    yield
    qe = [(q * exq[li * c:(li + 1) * c]).astype(BF16) for li in range(nlev)]
    ke = [(kt * exk[:, li * c:(li + 1) * c]).astype(BF16) for li in range(nlev)]
    qb, kb = q.astype(BF16), kt.astype(BF16)
    states = [st_ref[hd] for hd in range(GLA_HEADS)]
    yield
    outs, new_states = [], []
    lane = lax.broadcasted_iota(I32, (c, LANES), 1)
    srow = lax.broadcasted_iota(I32, (LANES, 1), 0)
    zero = jnp.zeros((c, LANES), BF16)
    for hd in range(GLA_HEADS):
        ps = slice((hd // 2) * LANES, (hd // 2 + 1) * LANES)
        vs = slice(hd * GLA_DV, (hd + 1) * GLA_DV)
        mine = (lane < GLA_DK) if hd % 2 == 0 else (lane >= GLA_DK)
        mine_row = (srow < GLA_DK) if hd % 2 == 0 else (srow >= GLA_DK)
        pick = lambda t: jnp.where(mine, t[:, ps], zero)
        a = pm_ref[0] * _dot(pick(qb), kb[ps, :])
        for li in range(1, nlev):
            a = a + pm_ref[li] * _dot(pick(qe[li]), ke[li][ps, :])
        v_h = v_ref[:, vs]
        outs.append(_dot(qe[0][:, ps], states[hd].astype(BF16)) + _dot(a.astype(BF16), v_h))
        new_states.append(states[hd] * gcol[ps, :] + jnp.where(mine_row, _dot(ke[0][ps, :], v_h), 0.0))
        yield
    o_ref[...] = jnp.concatenate(outs, axis=1)
    for hd in range(GLA_HEADS):
        st_ref[hd] = new_states[hd]
    yield


def _gla_kernel(nlev, nb, *refs):
    ins_f, ins_b = refs[0:4], refs[4:8]
    cumq_ref, cumkt_ref, pm_ref, s0_ref, of_ref, ob_ref, sf_ref = refs[8:15]
    st_refs = refs[15:]
    step = pl.program_id(1)

    @pl.when(step == 0)
    def _():
        for bb in range(nb):
            for d_ in range(2):
                st_refs[2 * bb + d_][...] = s0_ref[bb, d_]

    chains = []
    for bb in range(nb):
        for d_, (ins, o_ref) in enumerate(((ins_f, of_ref), (ins_b, ob_ref))):
            q_ref, k_ref, v_ref, la_ref = ins
            chains.append(_gla_chain(nlev, q_ref[bb], k_ref[bb], la_ref[bb], v_ref.at[bb], cumq_ref[d_],
                                     cumkt_ref[d_], pm_ref.at[d_], st_refs[2 * bb + d_], o_ref.at[bb]))
    for _ in range(GLA_HEADS + 3):
        for ch in chains:
            next(ch)
    for bb in range(nb):
        for d_ in range(2):
            sf_ref[bb, d_] = st_refs[2 * bb + d_][...]


def _gla(q, k, v, la, s0, cumq, cumkt, pm, nlev):
    b, n, _ = q.shape
    c = GLA_BLOCK
    nc = n // c
    nb = next(c for c in (GLA_BATCHES_PER_STEP, 2, 1) if b % c == 0)
    specs = []
    for d_ in range(2):
        pos = (lambda s_: s_) if d_ == 0 else (lambda s_: nc - 1 - s_)
        specs += [pl.BlockSpec((nb, c, 256), lambda bi, s_, pos=pos: (bi, pos(s_), 0)),
                  pl.BlockSpec((nb, c, 256), lambda bi, s_, pos=pos: (bi, pos(s_), 0)),
                  pl.BlockSpec((nb, c, 512), lambda bi, s_, pos=pos: (bi, pos(s_), 0)),
                  pl.BlockSpec((nb, c, 256), lambda bi, s_, pos=pos, d_=d_: (bi, pos(s_), d_))]
    st_spec = pl.BlockSpec((nb, 2, GLA_HEADS, GLA_DV, LANES), lambda bi, s_: (bi, 0, 0, 0, 0))
    full = lambda a: pl.BlockSpec(a.shape, lambda bi, s_: (0,) * a.ndim)
    return pl.pallas_call(
        functools.partial(_gla_kernel, nlev, nb),
        out_shape=[jax.ShapeDtypeStruct((b, n, GLA_V), F32), jax.ShapeDtypeStruct((b, n, GLA_V), F32),
                   jax.ShapeDtypeStruct((b, 2, GLA_HEADS, GLA_DV, LANES), F32)],
        grid=(b // nb, nc),
        in_specs=specs + [full(cumq), full(cumkt), full(pm), st_spec],
        out_specs=[pl.BlockSpec((nb, c, GLA_V), lambda bi, s_: (bi, s_, 0)),
                   pl.BlockSpec((nb, c, GLA_V), lambda bi, s_: (bi, nc - 1 - s_, 0)), st_spec],
        scratch_shapes=[pltpu.VMEM((GLA_HEADS, GLA_DV, LANES), F32) for _ in range(2 * nb)],
        compiler_params=_cparams(("parallel", "arbitrary")),
    )(q, k, v, la, q, k, v, la, cumq, cumkt, pm, s0)


def _odd_out_kernel(x_ref, of_ref, ob_ref, r_ref, dl_ref, gg_ref, w_ref, mod_ref, o_ref):
    d = D_MODEL
    o = of_ref[...] + ob_ref[...]
    gg = gg_ref[...]
    r = r_ref[...]
    parts = []
    for hd in range(GLA_HEADS):
        vs = slice(hd * GLA_DV, (hd + 1) * GLA_DV)
        oh = o[:, vs]
        parts.append(oh * lax.rsqrt(jnp.mean(oh * oh, axis=-1, keepdims=True) + EPS) * gg[:, vs])
    cl = (jnp.concatenate(parts, axis=1) * (r * jax.nn.sigmoid(r))).astype(BF16)
    y = _dot(cl, w_ref[0:512, :]) + _dot(dl_ref[...], w_ref[512:1024, :])
    o_ref[...] = x_ref[...] + mod_ref[:, 2 * d:3 * d] * y


def _odd_out(x, o_fwd, o_bwd, r, dl, gg, w, mod, tm):
    b, n, d = x.shape
    act = lambda wd: pl.BlockSpec((None, tm, wd), lambda bi, i: (bi, i, 0))
    return pl.pallas_call(
        _odd_out_kernel,
        out_shape=jax.ShapeDtypeStruct((b, n, d), F32),
        grid=(b, n // tm),
        in_specs=[act(d), act(GLA_V), act(GLA_V), act(512), act(512),
                  pl.BlockSpec(gg.shape, lambda bi, i: (0, 0)), pl.BlockSpec(w.shape, lambda bi, i: (0, 0)),
                  pl.BlockSpec((None, 1, 6 * d), lambda bi, i: (bi, 0, 0))],
        out_specs=act(d),
        compiler_params=_cparams(("parallel", "parallel")),
    )(x, o_fwd, o_bwd, r, dl, gg, w, mod)


def _router_kernel(nlat, x_ref, xc_ref, mod_ref, gn_ref, wr_ref, u_ref, h_ref, e_ref, wt_ref, r_ref, cnt_ref,
                   carry_ref):
    d = D_MODEL
    tm = x_ref.shape[0]
    i = pl.program_id(0)

    @pl.when(i == 0)
    def _():
        carry_ref[...] = jnp.zeros_like(carry_ref)

    mod = mod_ref[...]
    x = x_ref[...] if xc_ref is None else jnp.where(i < nlat, x_ref[...], xc_ref[...])
    h = _rms(x, gn_ref[...]) * (1.0 + mod[:, 4 * d:5 * d]) + mod[:, 3 * d:4 * d]
    h_ref[...] = _pack_bf16_pairs(h)
    h_hi, h_lo = _split2(h)
    w_hi, w_lo = _split2(wr_ref[...])
    lg = _dot_nt(w_hi, h_hi) + _dot_nt(w_lo, h_hi) + _dot_nt(w_hi, h_lo)
    rid = lax.broadcasted_iota(I32, (8, tm), 0)
    gl = jnp.where(rid < MOE_GROUPS, lg[0:8], NEG)
    gmax = jnp.max(gl, axis=0, keepdims=True)
    gsel = jnp.min(jnp.where(gl == gmax, rid, 8), axis=0, keepdims=True)
    pmax = 1.0 / jnp.sum(jnp.where(rid < MOE_GROUPS, jnp.exp(gl - gmax), 0.0), axis=0, keepdims=True)
    e_in = jnp.zeros((MOE_PER_GROUP, tm), F32)
    for g in range(MOE_GROUPS):
        e_in = e_in + jnp.where(gsel == g, lg[8 + 8 * g:16 + 8 * g], 0.0)
    v1 = jnp.max(e_in, axis=0, keepdims=True)
    i1 = jnp.min(jnp.where(e_in == v1, rid, 8), axis=0, keepdims=True)
    e_rest = jnp.where(rid == i1, -jnp.inf, e_in)
    v2 = jnp.max(e_rest, axis=0, keepdims=True)
    i2 = jnp.min(jnp.where(e_rest == v2, rid, 8), axis=0, keepdims=True)
    t = jnp.exp(v2 - v1)
    w1 = pmax / (1.0 + t)
    w2 = pmax * t / (1.0 + t)
    e1 = gsel * MOE_PER_GROUP + i1
    e2 = gsel * MOE_PER_GROUP + i2
    eid = lax.broadcasted_iota(I32, (MOE_EXPERTS, tm), 0)
    oh1 = jnp.where(eid == e1, 1.0, 0.0)
    oh2 = jnp.where(eid == e2, 1.0, 0.0)
    ohs = oh1 + oh2
    base = carry_ref[:, 0:1] + _dot(ohs.astype(BF16), u_ref[...])
    r1 = jnp.sum(oh1 * base, axis=0, keepdims=True)
    r2 = jnp.sum(oh2 * base, axis=0, keepdims=True)
    carry_ref[...] = carry_ref[...] + jnp.sum(ohs, axis=1, keepdims=True)
    cnt_ref[...] = carry_ref[...]
    e_ref[...] = jnp.concatenate([e1, e2], axis=0)
    r_ref[...] = jnp.concatenate([r1, r2], axis=0).astype(I32)
    w8 = jnp.concatenate([w1, w2, jnp.zeros((6, tm), F32)], axis=0)
    wt_ref[...] = w8.T


def _stream_specs(x2, xc2, rows_per_mod, ctx_row):
    tm = MOE_TILE
    d = x2.shape[1]
    nlat = x2.shape[0] // tm
    lat = lambda i, *_: (jnp.minimum(i, nlat - 1), 0)
    ctx = lambda i, *_: (jnp.maximum(i - nlat, 0), 0)
    if xc2 is None:
        modrow = lambda i, *_: (i // (rows_per_mod // tm), 0, 0)
    else:
        modrow = lambda i, *_: (jnp.where(i < nlat, i // (rows_per_mod // tm), ctx_row), 0, 0)
    specs = [pl.BlockSpec((tm, d), lat)] + ([] if xc2 is None else [pl.BlockSpec((tm, d), ctx)])
    return nlat, specs, pl.BlockSpec((None, 1, 6 * d), modrow)


def _router(x2, xc2, mod, rows_per_mod, ctx_row, gn, wr, u):
    d = x2.shape[1]
    tm = MOE_TILE
    n = x2.shape[0] + (0 if xc2 is None else xc2.shape[0])
    nlat, xspecs, modspec = _stream_specs(x2, xc2, rows_per_mod, ctx_row)
    body = functools.partial(_router_kernel, nlat)
    if xc2 is None:
        body = lambda x_ref, *rest: _router_kernel(nlat, x_ref, None, *rest)
    return pl.pallas_call(
        body,
        out_shape=[jax.ShapeDtypeStruct((n, d // 2), I32), jax.ShapeDtypeStruct((2, n), I32),
                   jax.ShapeDtypeStruct((n, 8), F32), jax.ShapeDtypeStruct((2, n), I32),
                   jax.ShapeDtypeStruct((MOE_EXPERTS, LANES), F32)],
        grid=(n // tm,),
        in_specs=xspecs + [modspec,
                  pl.BlockSpec(gn.shape, lambda i: (0, 0)), pl.BlockSpec(wr.shape, lambda i: (0, 0)),
                  pl.BlockSpec(u.shape, lambda i: (0, 0))],
        out_specs=[pl.BlockSpec((tm, d // 2), lambda i: (i, 0)), pl.BlockSpec((2, tm), lambda i: (0, i)),
                   pl.BlockSpec((tm, 8), lambda i: (i, 0)), pl.BlockSpec((2, tm), lambda i: (0, i)),
                   pl.BlockSpec((MOE_EXPERTS, LANES), lambda i: (0, 0))],
        scratch_shapes=[pltpu.VMEM((MOE_EXPERTS, LANES), F32)],
        compiler_params=_cparams(("arbitrary",)),
    )(*([x2] if xc2 is None else [x2, xc2]), mod, gn, wr, u)


def _sc_permute_rows(src, dest, scatter):
    rows, d = dest.shape[0], src.shape[1]
    n = rows // 2
    info = plsc.get_sparse_core_info()
    workers = info.num_cores * info.num_subcores
    per_worker = rows // workers
    chunk = next(c for c in SC_CHUNKS if per_worker % c == 0)
    assert rows == per_worker * workers and n % per_worker == 0
    mesh = plsc.VectorSubcoreMesh(core_axis_name="c", subcore_axis_name="s")

    def body(src_hbm, dest_hbm, out_hbm, idx_v, rows_v, sem):
        base = (lax.axis_index("s") * info.num_cores + lax.axis_index("c")) * per_worker

        @pl.loop(0, per_worker // chunk)
        def _(j):
            a0 = base + j * chunk
            pltpu.sync_copy(dest_hbm.at[pl.ds(a0, chunk)], idx_v)
            if scatter:
                t0 = jnp.where(a0 >= n, a0 - n, a0)
                pltpu.sync_copy(src_hbm.at[pl.ds(t0, chunk)], rows_v)
                pltpu.async_copy(rows_v, out_hbm.at[idx_v], sem).wait()
            else:
                pltpu.async_copy(src_hbm.at[idx_v], rows_v, sem).wait()
                pltpu.sync_copy(rows_v, out_hbm.at[pl.ds(a0, chunk)])

    return pl.kernel(
        body, out_type=jax.ShapeDtypeStruct((rows, d), src.dtype), mesh=mesh,
        scratch_types=[pltpu.VMEM((chunk,), I32), pltpu.VMEM((chunk, d), src.dtype), pltpu.SemaphoreType.DMA],
    )(src, dest)


def _gmm_kernel(layer, vt_ref, ve_ref, vlo_ref, vhi_ref, vfirst_ref, vslot_ref, vnext_ref, nv_ref,
                xs_ref, wg_hbm, wu_hbm, wd_hbm, ys_ref, wgs_ref, wus_ref, wds_ref, wgb_ref, wub_ref, wdb_ref, sem):
    del vt_ref
    v = pl.program_id(0)

    def fetch(e, slot):
        return [pltpu.make_async_copy(w.at[layer, e], s.at[slot], sem.at[slot, i])
                for i, (w, s) in enumerate(((wg_hbm, wgs_ref), (wu_hbm, wus_ref), (wd_hbm, wds_ref)))]

    @pl.when(v < nv_ref[0])
    def _():
        @pl.when((v == 0) | (ve_ref[v] != ve_ref[jnp.maximum(v - 1, 0)]))
        def _():
            slot = vslot_ref[v]

            @pl.when(v == 0)
            def _():
                for c in fetch(ve_ref[0], 0):
                    c.start()

            for c in fetch(ve_ref[v], slot):
                c.wait()
            wgb_ref[...] = wgs_ref[slot].astype(BF16)
            wub_ref[...] = wus_ref[slot].astype(BF16)
            wdb_ref[...] = wds_ref[slot].astype(BF16)

            @pl.when(vnext_ref[v] >= 0)
            def _():
                for c in fetch(vnext_ref[v], 1 - slot):
                    c.start()

        x = _unpack_bf16_pairs(xs_ref[...]).astype(BF16)
        g = _dot(x, wgb_ref[...])
        u = _dot(x, wub_ref[...])
        y = _pack_bf16_pairs(_dot((g * jax.nn.sigmoid(g) * u).astype(BF16), wdb_ref[...]))
        row = lax.broadcasted_iota(I32, (y.shape[0], 1), 0)
        mine = (row >= vlo_ref[v]) & (row < vhi_ref[v])

        @pl.when(vfirst_ref[v] == 1)
        def _():
            ys_ref[...] = jnp.where(mine, y, 0)

        @pl.when(vfirst_ref[v] == 0)
        def _():
            ys_ref[...] = jnp.where(mine, y, ys_ref[...])


def _gmm(xs, visits, layer, wg, wu, wd):
    rows, dw = xs.shape
    d = 2 * dw
    tm = MOE_TILE
    hid = wg.shape[-1]
    nvis = rows // tm + MOE_EXPERTS - 1
    tile = lambda v, vt, *_: (vt[v], 0)
    anyspec = pl.BlockSpec(memory_space=pl.ANY)
    return pl.pallas_call(
        functools.partial(_gmm_kernel, layer),
        out_shape=jax.ShapeDtypeStruct((rows, dw), I32),
        grid_spec=pltpu.PrefetchScalarGridSpec(
            num_scalar_prefetch=8, grid=(nvis,),
            in_specs=[pl.BlockSpec((tm, dw), tile), anyspec, anyspec, anyspec],
            out_specs=pl.BlockSpec((tm, dw), tile),
            scratch_shapes=[pltpu.VMEM((2, d, hid), F32), pltpu.VMEM((2, d, hid), F32), pltpu.VMEM((2, hid, d), F32),
                            pltpu.VMEM((d, hid), BF16), pltpu.VMEM((d, hid), BF16), pltpu.VMEM((hid, d), BF16),
                            pltpu.SemaphoreType.DMA((2, 3))]),
        compiler_params=_cparams(("arbitrary",)),
    )(*visits, xs, wg, wu, wd)


def _combine_kernel(x_ref, wt_ref, mod_ref, fg_ref, y0_ref, y1_ref, o_ref):
    d = D_MODEL
    wt = wt_ref[...]
    y = wt[:, 0:1] * _unpack_bf16_pairs(y0_ref[...]) + wt[:, 1:2] * _unpack_bf16_pairs(y1_ref[...])
    o_ref[...] = _rms(x_ref[...] + mod_ref[:, 5 * d:6 * d] * y, fg_ref[...])


def _combine(x2, wt, mod, rows_per_mod, fg, yt):
    n, d = x2.shape
    tm = MOE_TILE
    ntiles = n // tm
    return pl.pallas_call(
        _combine_kernel,
        out_shape=jax.ShapeDtypeStruct((n, d), F32),
        grid=(ntiles,),
        in_specs=[pl.BlockSpec((tm, d), lambda i: (i, 0)), pl.BlockSpec((tm, 8), lambda i: (i, 0)),
                  pl.BlockSpec((None, 1, 6 * d), lambda i: (i // (rows_per_mod // tm), 0, 0)),
                  pl.BlockSpec(fg.shape, lambda i: (0, 0)),
                  pl.BlockSpec((tm, d // 2), lambda i: (i, 0)), pl.BlockSpec((tm, d // 2), lambda i: (ntiles + i, 0))],
        out_specs=pl.BlockSpec((tm, d), lambda i: (i, 0)),
        compiler_params=_cparams(("parallel",)),
    )(x2, wt, mod, fg, yt, yt)


def _pick(table, idx):
    hot = idx[..., None] == jnp.arange(table.shape[0], dtype=I32)
    return jnp.sum(jnp.where(hot, table, 0), axis=-1)


def _moe_plan(counts, e, r, rows):
    tm = MOE_TILE
    ends = jnp.cumsum(counts)
    starts = ends - counts
    dest = (_pick(starts, e) + r).reshape(-1)
    first_tile = starts // tm
    nvis = jnp.where(counts > 0, (ends - 1) // tm - first_tile + 1, 0)
    vend = jnp.cumsum(nvis)
    nv = vend[-1:]
    v = jnp.minimum(jnp.arange(rows // tm + MOE_EXPERTS - 1, dtype=I32), nv[0] - 1)
    ve = jnp.sum((vend[None, :] <= v[:, None]).astype(I32), axis=1)
    vt = _pick(first_tile, ve) + v - _pick(vend - nvis, ve)
    vlo = jnp.maximum(_pick(starts, ve) - vt * tm, 0)
    vhi = jnp.minimum(_pick(ends, ve) - vt * tm, tm)
    vfirst = jnp.concatenate([jnp.ones((1,), I32), (vt[1:] != vt[:-1]).astype(I32)])
    changed = jnp.concatenate([jnp.ones((1,), I32), (ve[1:] != ve[:-1]).astype(I32)])
    vslot = (jnp.cumsum(changed) - 1) % 2
    eid = jnp.arange(MOE_EXPERTS, dtype=I32)
    later = (eid[None, :] > eid[:, None]) & (counts[None, :] > 0)
    nxt = jnp.min(jnp.where(later, eid[None, :], MOE_EXPERTS), axis=1)
    vnext = _pick(jnp.where(nxt < MOE_EXPERTS, nxt, -1), ve)
    return dest, (vt, ve, vlo, vhi, vfirst, vslot.astype(I32), vnext.astype(I32), nv)


def _moe_experts(x, xc, mod, ctx_row, gn, wr, u, layer, wg, wu, wd):
    b, n, d = x.shape
    x2 = x.reshape(b * n, d)
    xc2 = None if xc is None else xc.reshape(-1, d)
    nt = x2.shape[0] + (0 if xc is None else xc2.shape[0])
    h, e, wt, r, cnt = _router(x2, xc2, mod, n, ctx_row, gn, wr, u)
    dest, visits = _moe_plan(cnt[:, 0].astype(I32), e, r, 2 * nt)
    xs = _sc_permute_rows(h, dest, scatter=True)
    ys = _gmm(xs, visits, layer, wg, wu, wd)
    return _sc_permute_rows(ys, dest, scatter=False), wt


def _rope_tables(rows, dim):
    row = jnp.repeat(jnp.arange(rows, dtype=F32), GRID_W)
    col = jnp.tile(jnp.arange(GRID_W, dtype=F32), rows)
    half = dim // 2
    inv = jnp.power(ROPE_BASE, -jnp.arange(0, half, 2, dtype=F32) / half)
    ar = row[:, None] * inv[None, :]
    ac = col[:, None] * inv[None, :]
    ang = jnp.concatenate([ar, ar, ac, ac], axis=-1)
    return jnp.cos(ang), jnp.sin(ang)


def _even_tables(n, with_rope):
    if with_rope:
        cm, sm = _rope_tables(n // GRID_W, MLA_ROPE)
        cwin, swin = _rope_tables(n // GRID_W, WIN_HEAD_DIM)
    else:
        cm, sm = jnp.ones((n, MLA_ROPE), F32), jnp.zeros((n, MLA_ROPE), F32)
        cwin, swin = jnp.ones((n, WIN_HEAD_DIM), F32), jnp.zeros((n, WIN_HEAD_DIM), F32)
    one, zero = jnp.ones((n, MLA_NOPE), F32), jnp.zeros((n, MLA_NOPE), F32)
    pad = jnp.zeros((n, LANES - MLA_NOPE - MLA_ROPE), F32)
    return (jnp.concatenate([one, cm, pad], axis=1), jnp.concatenate([zero, sm, pad], axis=1),
            jnp.concatenate([cwin, cwin], axis=1), jnp.concatenate([swin, swin], axis=1))


def _even_weights(w_in, qg, w_uq, kvg, w_ukv):
    d = w_in.shape[0]
    o = np.cumsum([0, MLA_Q_RANK, MLA_KV_RANK, MLA_ROPE, 512, 128, 128])
    cq, ckv, kr, qw, kw, vw = [w_in[:, o[i]:o[i + 1]] for i in range(6)]
    z = lambda c: jnp.zeros((d, c), F32)
    kr128 = jnp.concatenate([z(MLA_NOPE), kr, z(LANES - MLA_NOPE - MLA_ROPE)], axis=1)
    dup = lambda t: jnp.concatenate([t[:, 0:64], t[:, 0:64], t[:, 64:128], t[:, 64:128]], axis=1)
    win = jnp.concatenate([cq, ckv, kr128, qw, dup(kw), dup(vw)], axis=1).astype(BF16)
    uq = w_uq.reshape(MLA_Q_RANK, MLA_HEADS, MLA_NOPE + MLA_ROPE)
    uq = jnp.pad(uq, ((0, 0), (0, 0), (0, LANES - MLA_NOPE - MLA_ROPE))).reshape(MLA_Q_RANK, MLA_HEADS * LANES)
    ukv = w_ukv.reshape(MLA_KV_RANK, MLA_HEADS, MLA_NOPE + MLA_V)
    ukk = jnp.pad(ukv[:, :, :MLA_NOPE], ((0, 0), (0, 0), (0, LANES - MLA_NOPE))).reshape(MLA_KV_RANK, MLA_HEADS * LANES)
    ukvv = ukv[:, :, MLA_NOPE:].reshape(MLA_KV_RANK, MLA_HEADS * MLA_V)
    return (win, qg.reshape(1, -1), uq.astype(BF16), kvg.reshape(1, -1), ukk.astype(BF16), ukvv.T.astype(BF16))


def _odd_weights(w_in, w_g2, b_g, ln_g, ln_b, w_s, b_s):
    d = w_in.shape[0]
    o = np.cumsum([0, GLA_K, GLA_K, GLA_V, 2 * GLA_GATE_RANK, GLA_V, SG_WIDTH, SG_WIDTH])
    q, k, v, g, r, u, vg = [w_in[:, o[i]:o[i + 1]] for i in range(7)]
    g128 = jnp.concatenate([g, jnp.zeros((d, LANES - 2 * GLA_GATE_RANK), F32)], axis=1)
    win = jnp.concatenate([q, k, v, g128, r, u, vg], axis=1).astype(BF16)
    zr = jnp.zeros((GLA_GATE_RANK, GLA_K), F32)
    pad = jnp.zeros((LANES - 2 * GLA_GATE_RANK, GLA_K), F32)
    wg = jnp.concatenate([jnp.concatenate([w_g2[0], zr, pad], axis=0),
                          jnp.concatenate([zr, w_g2[1], pad], axis=0)], axis=1)
    bg = b_g.reshape(1, 2 * GLA_K)
    return (win, wg, bg, ln_g.reshape(1, -1), ln_b.reshape(1, -1), w_s.astype(BF16), b_s.T)


def kernel(x, c, ctx, c_ctx, ada_w, ada_b, norm_mix_g, norm_ffn_g, even_w_in, mla_q_norm_g, mla_w_uq, mla_kv_norm_g, mla_w_ukv, win_sink, even_w_out, odd_w_in, gla_w_g2, gla_b_g, gla_norm_g, sg_ln_g, sg_ln_b, sg_w_s, sg_b_s, odd_w_out, moe_w_rg, moe_w_re, moe_w_gate, moe_w_up, moe_w_down, final_norm_g):
    b, n, d = x.shape
    lc = ctx.shape[1]
    depth = ada_w.shape[0]
    assert depth == 2 and d == D_MODEL and b < 8
    assert n % 512 == 0 and lc % MOE_TILE == 0 and n % GRID_W == 0
    tm = 512 if n % 512 == 0 else 256
    tq = 256

    cond8 = jnp.concatenate([c, c_ctx[None, :], jnp.zeros((8 - b - 1, d), F32)], axis=0)
    mod_all = _adaln(cond8, ada_w, ada_b).reshape(depth, 8, 1, 6 * d)
    ctx_row = b
    u_tri = jnp.asarray(np.triu(np.ones((MOE_TILE, MOE_TILE), np.float32), 1), BF16)
    fg = final_norm_g.reshape(1, d)

    def router_w(layer):
        return jnp.concatenate([moe_w_rg[layer].T, jnp.zeros((8 - MOE_GROUPS, d), F32), moe_w_re[layer].T], axis=0)

    def moe_experts(xx, xx_ctx, layer):
        return _moe_experts(xx, xx_ctx, mod_all[layer], ctx_row, norm_ffn_g[layer].reshape(1, d), router_w(layer),
                            u_tri, layer, moe_w_gate, moe_w_up, moe_w_down)

    mod = mod_all[0]
    gn = norm_mix_g[0].reshape(1, d)
    ew = _even_weights(even_w_in[0], mla_q_norm_g[0], mla_w_uq[0], mla_kv_norm_g[0], mla_w_ukv[0])
    qm_l, km_l, vm_l, qw_l, kw_l, vw_l = _even_in(x, mod, None, gn, ew, _even_tables(n, True), tm)
    qm_c, km_c, vm_c, qw_c, kw_c, vw_c = _even_in(ctx, mod, ctx_row, gn, ew, _even_tables(lc, False), lc)
    w_out = even_w_out[0].astype(BF16)
    sink = win_sink[0]
    oa_l = _mla_attn(qm_l, [(km_l, vm_l), (km_c, vm_c)], tq)
    ob_l = _gqa(qw_l, kw_l, vw_l, kw_c, vw_c, sink, True)
    xl = _even_out(x, oa_l, ob_l, w_out, mod, None, tm)
    oa_c = _mla_attn(qm_c, [(km_c, vm_c)], lc)
    ob_c = _gqa(qw_c, None, None, kw_c, vw_c, sink, False)
    xc = _even_out(ctx, oa_c, ob_c, w_out, mod, ctx_row, lc)
    pending = moe_experts(xl, xc, 0)

    mod = mod_all[1]
    gn = norm_mix_g[1].reshape(1, d)
    ow = _odd_weights(odd_w_in[0], gla_w_g2[0], gla_b_g[0], sg_ln_g[0], sg_ln_b[0], sg_w_s[0], sg_b_s[0])
    xl, q_l, k_l, v_l, la_l, r_l, dl_l = _odd_in(xl, pending, 0, mod_all[0], mod, None, gn, ow)
    _, q_c, k_c, v_c, la_c, _, _ = _odd_in(xc, pending, b * n // MOE_TILE, mod_all[0], mod, ctx_row, gn, ow)
    cumq_np, cumkt_np, pm_np, nlev = _gla_tables()
    cumq, cumkt = jnp.asarray(cumq_np, BF16), jnp.asarray(cumkt_np, BF16)
    pm = jnp.asarray(pm_np, F32)
    s0 = jnp.zeros((b, 2, GLA_HEADS, GLA_DV, LANES), F32)
    _, _, s_ctx = _gla(q_c, k_c, v_c, la_c, s0, cumq, cumkt, pm, nlev)
    o_fwd, o_bwd, _ = _gla(q_l, k_l, v_l, la_l, s_ctx, cumq, cumkt, pm, nlev)
    xl = _odd_out(xl, o_fwd, o_bwd, r_l, dl_l, gla_norm_g[0].reshape(1, -1), odd_w_out[0].astype(BF16), mod, tm)
    yt, wt = moe_experts(xl, None, 1)
    return _combine(xl.reshape(b * n, d), wt, mod, n, fg, yt).reshape(b, n, d)
```

```python
import functools

import numpy as np
import jax
import jax.numpy as jnp
from jax import lax
from jax.experimental import pallas as pl
from jax.experimental.pallas import tpu as pltpu
from jax.experimental.pallas import tpu_sc as plsc

F32 = jnp.float32
BF16 = jnp.bfloat16
I32 = jnp.int32

D_MODEL = 1024
GRID_W = 64
EPS = 1e-6
ROPE_BASE = 10000.0
MLA_HEADS = 8
MLA_Q_RANK = 256
MLA_KV_RANK = 128
MLA_NOPE = 64
MLA_ROPE = 32
MLA_V = 64
WIN_HEADS = 8
WIN_KV_HEADS = 2
WIN_HEAD_DIM = 64
WIN_BLOCK = 128
GLA_HEADS = 4
GLA_DK = 64
GLA_DV = 128
GLA_GATE_RANK = 16
GLA_TAU = 16.0
GLA_K = GLA_HEADS * GLA_DK
GLA_V = GLA_HEADS * GLA_DV
SG_GROUPS = 4
SG_CHUNK = 128
SG_WIDTH = 512
MOE_GROUPS = 4
MOE_PER_GROUP = 8
MOE_EXPERTS = 32
MOE_HIDDEN = 512

LANES = 128
GLA_BLOCK = 128
GLA_BATCHES_PER_STEP = 2
MOE_TILE = 256
MLA_KEY_CHUNK = 1024
SC_CHUNKS = (128, 64, 32)
NEG = -1e30
LOG2E = 1.4426950408889634
VMEM_LIMIT = 56 * 1024 * 1024


def _cparams(sem):
    return pltpu.CompilerParams(dimension_semantics=sem, vmem_limit_bytes=VMEM_LIMIT)


def _dot(a, b):
    return jnp.dot(a, b, preferred_element_type=F32)


def _dot_nt(a, b):
    return lax.dot_general(a, b, (((1,), (1,)), ((), ())), preferred_element_type=F32)


def _split2(a):
    hi = a.astype(BF16)
    lo = (a - hi.astype(F32)).astype(BF16)
    return hi, lo


def _split3(a):
    hi = a.astype(BF16)
    r = a - hi.astype(F32)
    mid = r.astype(BF16)
    lo = (r - mid.astype(F32)).astype(BF16)
    return hi, mid, lo


def _pack_bf16_pairs(x):
    k = x.shape[1] // 2
    bits = lax.bitcast_convert_type(x.astype(BF16).astype(F32), jnp.uint32)
    return lax.bitcast_convert_type(bits[:, :k] | (bits[:, k:] >> 16), I32)


def _unpack_bf16_pairs(w):
    bits = lax.bitcast_convert_type(w, jnp.uint32)
    hi = lax.bitcast_convert_type(bits & jnp.uint32(0xFFFF0000), F32)
    lo = lax.bitcast_convert_type(bits << 16, F32)
    return jnp.concatenate([hi, lo], axis=1)


def _rms(x, g):
    ms = jnp.mean(x * x, axis=-1, keepdims=True)
    return x * lax.rsqrt(ms + EPS) * g


def _lane_tile(t, reps):
    return t if reps == 1 else jnp.concatenate([t] * reps, axis=1)


def _rope(t, cos, sin, quarter):
    n = t.shape[1]
    lane = lax.broadcasted_iota(I32, t.shape, 1)
    first = (lane & (2 * quarter - 1)) < quarter
    rot = jnp.where(first, -pltpu.roll(t, n - quarter, 1), pltpu.roll(t, quarter, 1))
    return t * cos + rot * sin


def _adaln_kernel(c_ref, w_ref, b_ref, o_ref):
    c = c_ref[...]
    s_hi, s_lo = _split2(c * jax.nn.sigmoid(c))
    w_hi, w_lo = _split2(w_ref[...])
    o_ref[...] = _dot(s_hi, w_hi) + _dot(s_lo, w_hi) + _dot(s_hi, w_lo) + b_ref[...]


def _adaln(cond8, ada_w, ada_b):
    depth, d, n6 = ada_w.shape
    tn = 1536
    return pl.pallas_call(
        _adaln_kernel,
        out_shape=jax.ShapeDtypeStruct((depth, 8, n6), F32),
        grid=(depth, n6 // tn),
        in_specs=[
            pl.BlockSpec((8, d), lambda l, j: (0, 0)),
            pl.BlockSpec((None, d, tn), lambda l, j: (l, 0, j)),
            pl.BlockSpec((None, 1, tn), lambda l, j: (l, 0, j)),
        ],
        out_specs=pl.BlockSpec((None, 8, tn), lambda l, j: (l, 0, j)),
        compiler_params=_cparams(("parallel", "parallel")),
    )(cond8, ada_w, ada_b.reshape(depth, 1, n6))


def _even_in_kernel(x_ref, mod_ref, gn_ref, win_ref, qg_ref, wuq_ref, kvg_ref, wukk_ref, wukv_ref,
                    cq_ref, sq_ref, cw_ref, sw_ref,
                    qm_ref, km_ref, vm_ref, qw_ref, kw_ref, vw_ref):
    d = D_MODEL
    mod = mod_ref[...]
    h = _rms(x_ref[...], gn_ref[...]) * (1.0 + mod[:, d:2 * d]) + mod[:, 0:d]
    z = _dot(h.astype(BF16), win_ref[...])
    cq, sq, cw, sw = cq_ref[...], sq_ref[...], cw_ref[...], sw_ref[...]
    cqn = _rms(z[:, 0:256], qg_ref[...]).astype(BF16)
    q = _dot(cqn, wuq_ref[...])
    q = _rope(q, _lane_tile(cq, 8), _lane_tile(sq, 8), MLA_ROPE // 4)
    qm_ref[...] = (q * (LOG2E * (MLA_NOPE + MLA_ROPE) ** -0.5)).astype(BF16)
    ckvn = _rms(z[:, 256:384], kvg_ref[...]).astype(BF16)
    kn = _dot(ckvn, wukk_ref[...])
    kr = _rope(z[:, 384:512], cq, sq, MLA_ROPE // 4)
    km_ref[...] = (kn + _lane_tile(kr, 8)).astype(BF16)
    vm_ref[...] = _dot_nt(wukv_ref[...], ckvn).astype(BF16)
    qw = _rope(z[:, 512:1024], _lane_tile(cw, 4), _lane_tile(sw, 4), WIN_HEAD_DIM // 4)
    qw_ref[...] = (qw * (WIN_HEAD_DIM ** -0.5)).astype(BF16)
    kw = _rope(z[:, 1024:1280], _lane_tile(cw, 2), _lane_tile(sw, 2), WIN_HEAD_DIM // 4)
    kw_ref[...] = kw.astype(BF16)
    vw_ref[...] = z[:, 1280:1536].astype(BF16)


def _even_in(x, mod, mod_row, gn, wts, tabs, tm):
    b, n, d = x.shape
    win, qg, wuq, kvg, wukk, wukv = wts
    nt = n // tm
    row = (lambda bi, i: (bi, 0, 0)) if mod_row is None else (lambda bi, i: (mod_row, 0, 0))
    full = lambda a: pl.BlockSpec(a.shape, lambda bi, i: (0,) * a.ndim)
    tab = pl.BlockSpec((tm, LANES), lambda bi, i: (i, 0))
    outw = (1024, 1024, None, 512, 256, 256)
    rowspec = lambda w: pl.BlockSpec((None, tm, w), lambda bi, i: (bi, i, 0))
    colspec = pl.BlockSpec((None, 512, tm), lambda bi, i: (bi, 0, i))
    return pl.pallas_call(
        _even_in_kernel,
        out_shape=[jax.ShapeDtypeStruct((b, 512, n) if w is None else (b, n, w), BF16) for w in outw],
        grid=(b, nt),
        in_specs=[pl.BlockSpec((None, tm, d), lambda bi, i: (bi, i, 0)),
                  pl.BlockSpec((None, 1, 6 * d), row),
                  full(gn), full(win), full(qg), full(wuq), full(kvg), full(wukk), full(wukv),
                  tab, tab, tab, tab],
        out_specs=[colspec if w is None else rowspec(w) for w in outw],
        compiler_params=_cparams(("parallel", "parallel")),
    )(x, mod, gn, win, qg, wuq, kvg, wukk, wukv, *tabs)


def _mla_attn_kernel(nseg, q_ref, *refs):
    ks, vts = refs[0:2 * nseg:2], refs[1:2 * nseg:2]
    o_ref = refs[2 * nseg]
    s_bufs = refs[2 * nseg + 1:2 * nseg + 3]
    p_bufs = refs[2 * nseg + 3:2 * nseg + 5]
    pieces, base = [], 0
    for k in ks:
        n = k.shape[0]
        pieces += [(k, c0, min(n, c0 + MLA_KEY_CHUNK), base + c0) for c0 in range(0, n, MLA_KEY_CHUNK)]
        base += n

    def score_chunk(h, piece, buf):
        k, c0, c1, g0 = piece
        hs = slice(h * LANES, (h + 1) * LANES)
        s = _dot_nt(k[c0:c1, hs], q_ref[:, hs])
        buf[g0:g0 + c1 - c0, :] = s
        return jnp.max(s, axis=0, keepdims=True)

    def prob_chunk(piece, sbuf, pbuf, m):
        _, c0, c1, g0 = piece
        p = jnp.exp2(sbuf[g0:g0 + c1 - c0, :] - m)
        pbuf[g0:g0 + c1 - c0, :] = p.astype(BF16)
        return jnp.sum(p, axis=0, keepdims=True)

    m_next = functools.reduce(jnp.maximum, [score_chunk(0, pc, s_bufs[0]) for pc in pieces])
    outs = []
    for h in range(MLA_HEADS):
        m_cur, maxes, sums = m_next, [], []
        for pc in pieces:
            if h + 1 < MLA_HEADS:
                maxes.append(score_chunk(h + 1, pc, s_bufs[(h + 1) % 2]))
            sums.append(prob_chunk(pc, s_bufs[h % 2], p_bufs[h % 2], m_cur))
        if h + 1 < MLA_HEADS:
            m_next = functools.reduce(jnp.maximum, maxes)
        l = functools.reduce(jnp.add, sums)
        vrows = slice(h * MLA_V, (h + 1) * MLA_V)
        ot, base = None, 0
        for k, vt in zip(ks, vts):
            n = k.shape[0]
            part = _dot(vt[vrows, :], p_bufs[h % 2][base:base + n, :])
            ot = part if ot is None else ot + part
            base += n
        outs.append(ot * (1.0 / l))
    o_ref[...] = jnp.concatenate(outs, axis=0).T.astype(BF16)


def _mla_attn(q, segs, tq):
    b, n, _ = q.shape
    in_specs = [pl.BlockSpec((None, tq, 1024), lambda bi, i: (bi, i, 0))]
    args = [q]
    keys = 0
    for k, vt in segs:
        lk = k.shape[1]
        keys += lk
        in_specs += [pl.BlockSpec((None, lk, 1024), lambda bi, i: (bi, 0, 0)),
                     pl.BlockSpec((None, 512, lk), lambda bi, i: (bi, 0, 0))]
        args += [k, vt]
    return pl.pallas_call(
        functools.partial(_mla_attn_kernel, len(segs)),
        out_shape=jax.ShapeDtypeStruct((b, n, 512), BF16),
        grid=(b, n // tq),
        in_specs=in_specs,
        out_specs=pl.BlockSpec((None, tq, 512), lambda bi, i: (bi, i, 0)),
        scratch_shapes=[pltpu.VMEM((keys, tq), F32), pltpu.VMEM((keys, tq), F32),
                        pltpu.VMEM((keys, tq), BF16), pltpu.VMEM((keys, tq), BF16)],
        compiler_params=_cparams(("parallel", "parallel")),
    )(*args)


def _gqa_kernel(has_win, nb, sink_ref, q_ref, *refs):
    if has_win:
        kp, kc, kn, vp, vc, vn, kx, vx, o_ref = refs
    else:
        kx, vx, o_ref = refs
    tq = q_ref.shape[0]
    i = pl.program_id(1)
    lane = lax.broadcasted_iota(I32, (tq, LANES), 1)
    row2 = lax.broadcasted_iota(I32, (2 * tq, 1), 0)
    half = WIN_HEAD_DIM
    npair = WIN_HEADS // 2
    kcats, vcats = [], []
    for g in range(WIN_KV_HEADS):
        gs = slice(g * LANES, (g + 1) * LANES)
        if has_win:
            kcats.append(jnp.concatenate([kp[:, gs], kc[:, gs], kn[:, gs], kx[:, gs]], axis=0))
            vcats.append(jnp.concatenate([vp[:, gs], vc[:, gs], vn[:, gs], vx[:, gs]], axis=0))
        else:
            kcats.append(kx[:, gs])
            vcats.append(vx[:, gs])
    scores = []
    for j in range(npair):
        qp = q_ref[:, j * LANES:(j + 1) * LANES]
        zero = jnp.zeros_like(qp)
        q2 = jnp.concatenate([jnp.where(lane < half, qp, zero), jnp.where(lane >= half, qp, zero)], axis=0)
        scores.append(_dot_nt(q2, kcats[j // 2]))
    if has_win:
        w = WIN_BLOCK
        r = lax.broadcasted_iota(I32, scores[0].shape, 0) & (tq - 1)
        c = lax.broadcasted_iota(I32, scores[0].shape, 1)
        big = jnp.int32(1 << 20)
        no_prev = jnp.where(i > 0, 0, big)
        no_next = jnp.where(i < nb - 1, 0, big)
        ok_prev = c >= r + no_prev
        ok_next = (c - 2 * w) <= r - no_next
        valid = ((c >= w) | ok_prev) & ((c < 2 * w) | (c >= 3 * w) | ok_next)
    probs, inv = [], []
    for j in range(npair):
        s = jnp.where(valid, scores[j], NEG) if has_win else scores[j]
        sk = jnp.where(row2 < tq, sink_ref[2 * j], sink_ref[2 * j + 1])
        m = jnp.maximum(jnp.max(s, axis=-1, keepdims=True), sk)
        p = jnp.exp(s - m)
        inv.append(1.0 / (jnp.sum(p, axis=-1, keepdims=True) + jnp.exp(sk - m)))
        probs.append(p.astype(BF16))
    for j in range(npair):
        o2 = _dot(probs[j], vcats[j // 2]) * inv[j]
        o_ref[:, j * LANES:(j + 1) * LANES] = jnp.where(lane < half, o2[:tq], o2[tq:]).astype(BF16)


def _gqa(q, k, v, kx, vx, sink, has_win):
    b, n, _ = q.shape
    lc = kx.shape[1]
    smem = pl.BlockSpec(memory_space=pltpu.SMEM)
    ctxs = pl.BlockSpec((None, lc, 256), lambda bi, i: (bi, 0, 0))
    if has_win:
        tq = WIN_BLOCK
        nb = n // tq
        blk = lambda f: pl.BlockSpec((None, tq, 256), f)
        prev = lambda bi, i: (bi, jnp.maximum(i - 1, 0), 0)
        cur = lambda bi, i: (bi, i, 0)
        nxt = lambda bi, i: (bi, jnp.minimum(i + 1, nb - 1), 0)
        in_specs = [smem, pl.BlockSpec((None, tq, 512), cur),
                    blk(prev), blk(cur), blk(nxt), blk(prev), blk(cur), blk(nxt), ctxs, ctxs]
        args = (sink, q, k, k, k, v, v, v, kx, vx)
    else:
        tq, nb = n, 1
        in_specs = [smem, pl.BlockSpec((None, tq, 512), lambda bi, i: (bi, i, 0)), ctxs, ctxs]
        args = (sink, q, kx, vx)
    return pl.pallas_call(
        functools.partial(_gqa_kernel, has_win, nb),
        out_shape=jax.ShapeDtypeStruct((b, n, 512), BF16),
        grid=(b, nb),
        in_specs=in_specs,
        out_specs=pl.BlockSpec((None, tq, 512), lambda bi, i: (bi, i, 0)),
        compiler_params=_cparams(("parallel", "parallel")),
    )(*args)


def _log_sigmoid(z):
    return jnp.minimum(z, 0.0) - jnp.log(1.0 + jnp.exp(-jnp.abs(z)))


def _odd_in_kernel(x_ref, y0_ref, y1_ref, wt_ref, modp_ref, mod_ref, gn_ref, win_ref, wg_ref, bg_ref, lng_ref,
                   lnb_ref, ws_ref, bst_ref, xn_ref, q_ref, k_ref, v_ref, la_ref, r_ref, dl_ref):
    d = D_MODEL
    tm = x_ref.shape[0]
    wt = wt_ref[...]
    y = wt[:, 0:1] * _unpack_bf16_pairs(y0_ref[...]) + wt[:, 1:2] * _unpack_bf16_pairs(y1_ref[...])
    x = x_ref[...] + modp_ref[:, 5 * d:6 * d] * y
    xn_ref[...] = x
    mod = mod_ref[...]
    h = (_rms(x, gn_ref[...]) * (1.0 + mod[:, d:2 * d]) + mod[:, 0:d]).astype(BF16)
    z = _dot(h, win_ref[...])
    q_ref[...] = z[:, 0:256] * (GLA_DK ** -0.5)
    k_ref[...] = z[:, 256:512]
    v_ref[...] = z[:, 512:1024].astype(BF16)
    g_hi, g_lo = _split2(z[:, 1024:1152])
    w_hi, w_lo = _split2(wg_ref[...])
    zg = _dot(g_hi, w_hi) + _dot(g_lo, w_hi) + _dot(g_hi, w_lo) + bg_ref[...]
    la_ref[...] = _log_sigmoid(zg) / GLA_TAU
    r_ref[...] = z[:, 1152:1664]
    u = jax.nn.gelu(z[:, 1664:2176])
    vg = jax.nn.gelu(z[:, 2176:2688])
    mu = jnp.mean(vg, axis=-1, keepdims=True)
    vc = vg - mu
    var = jnp.mean(vc * vc, axis=-1, keepdims=True)
    vn = (vc * lax.rsqrt(var + EPS) * lng_ref[...] + lnb_ref[...]).astype(BF16)
    bst = bst_ref[...]
    for c in range(tm // SG_CHUNK):
        rows = slice(c * SG_CHUNK, (c + 1) * SG_CHUNK)
        parts = []
        for g in range(SG_GROUPS):
            cols = slice(g * LANES, (g + 1) * LANES)
            parts.append(_dot(ws_ref[g], vn[rows, cols]) + bst[:, g:g + 1])
        dl_ref[rows, :] = (u[rows, :] * jnp.concatenate(parts, axis=1)).astype(BF16)


def _odd_in(x, pending, tile0, modp, mod, mod_row, gn, wts):
    b, n, d = x.shape
    tm = MOE_TILE
    yt, wt = pending
    ntiles = wt.shape[0] // tm
    win, wg, bg, lng, lnb, ws, bst = wts
    row = (lambda bi, i: (bi, 0, 0)) if mod_row is None else (lambda bi, i: (mod_row, 0, 0))
    full = lambda a: pl.BlockSpec(a.shape, lambda bi, i: (0,) * a.ndim)
    act = lambda wd: pl.BlockSpec((None, tm, wd), lambda bi, i: (bi, i, 0))
    tok = lambda bi, i: tile0 + bi * (n // tm) + i
    outs = [((b, n, d), F32, act(d)),
            ((b, n, 256), F32, act(256)), ((b, n, 256), F32, act(256)), ((b, n, 512), BF16, act(512)),
            ((b, n, 512), F32, act(512)), ((b, n, 512), F32, act(512)), ((b, n, 512), BF16, act(512))]
    return pl.pallas_call(
        _odd_in_kernel,
        out_shape=[jax.ShapeDtypeStruct(s, t) for s, t, _ in outs],
        grid=(b, n // tm),
        in_specs=[act(d), pl.BlockSpec((tm, d // 2), lambda bi, i: (tok(bi, i), 0)),
                  pl.BlockSpec((tm, d // 2), lambda bi, i: (ntiles + tok(bi, i), 0)),
                  pl.BlockSpec((tm, 8), lambda bi, i: (tok(bi, i), 0)),
                  pl.BlockSpec((None, 1, 6 * d), row), pl.BlockSpec((None, 1, 6 * d), row),
                  full(gn), full(win), full(wg), full(bg), full(lng), full(lnb), full(ws), full(bst)],
        out_specs=[sp for _, _, sp in outs],
        compiler_params=_cparams(("parallel", "parallel")),
    )(x, yt, yt, wt, modp, mod, gn, win, wg, bg, lng, lnb, ws, bst)


def _gla_tables():
    c = GLA_BLOCK
    t = np.arange(c)[:, None]
    u = np.arange(c)[None, :]
    levels = [c >> i for i in range(int(np.log2(c)) + 1)]
    cum = np.zeros((2, 2 * len(levels), c, c), np.float32)
    pair = np.zeros((2, len(levels), c, c), np.float32)
    for li, m in enumerate(levels):
        same = (t // m) == (u // m)
        cum[0, 2 * li] = same & (u <= t)
        cum[0, 2 * li + 1] = same & (u > t)
        cum[1, 2 * li] = same & (u >= t)
        cum[1, 2 * li + 1] = same & (u < t)
        if li > 0:
            pair[0, li] = ((t // m) % 2 == 1) & ((u // m) == (t // m) - 1)
            pair[1, li] = ((t // m) % 2 == 0) & ((u // m) == (t // m) + 1)
    pair[:, 0] = np.eye(c, dtype=np.float32)
    nlev = len(levels)
    m1 = cum[:, 0::2].reshape(2, nlev * c, c)
    m2t = np.concatenate([cum[:, 2 * li + 1].transpose(0, 2, 1) for li in range(nlev)], axis=2)
    return np.concatenate([m1, m1], axis=2), np.concatenate([m2t, m2t], axis=1), pair, nlev


def _gla_chain(nlev, q, k, la, v_ref, cumq, cumkt, pm_ref, st_ref, o_ref):
    c = GLA_BLOCK
    lat, kt = la.T, k.T
    l_hi, l_mid = _split2(la)
    t_hi, t_mid = _split2(lat)
    exq = jnp.exp(_dot(cumq, jnp.concatenate([l_hi, l_mid], axis=0)))
    exk = jnp.exp(_dot(jnp.concatenate([t_hi, t_mid], axis=1), cumkt))
    gcol = jnp.exp(jnp.sum(lat, axis=1, keepdims=True))
    yield
    qe = [(q * exq[li * c:(li + 1) * c]).astype(BF16) for li in range(nlev)]
    ke = [(kt * exk[:, li * c:(li + 1) * c]).astype(BF16) for li in range(nlev)]
    qb, kb = q.astype(BF16), kt.astype(BF16)
    states = [st_ref[hd] for hd in range(GLA_HEADS)]
    yield
    outs, new_states = [], []
    lane = lax.broadcasted_iota(I32, (c, LANES), 1)
    srow = lax.broadcasted_iota(I32, (LANES, 1), 0)
    zero = jnp.zeros((c, LANES), BF16)
    for hd in range(GLA_HEADS):
        ps = slice((hd // 2) * LANES, (hd // 2 + 1) * LANES)
        vs = slice(hd * GLA_DV, (hd + 1) * GLA_DV)
        mine = (lane < GLA_DK) if hd % 2 == 0 else (lane >= GLA_DK)
        mine_row = (srow < GLA_DK) if hd % 2 == 0 else (srow >= GLA_DK)
        pick = lambda t: jnp.where(mine, t[:, ps], zero)
        a = pm_ref[0] * _dot(pick(qb), kb[ps, :])
        for li in range(1, nlev):
            a = a + pm_ref[li] * _dot(pick(qe[li]), ke[li][ps, :])
        v_h = v_ref[:, vs]
        outs.append(_dot(qe[0][:, ps], states[hd].astype(BF16)) + _dot(a.astype(BF16), v_h))
        new_states.append(states[hd] * gcol[ps, :] + jnp.where(mine_row, _dot(ke[0][ps, :], v_h), 0.0))
        yield
    o_ref[...] = jnp.concatenate(outs, axis=1)
    for hd in range(GLA_HEADS):
        st_ref[hd] = new_states[hd]
    yield


def _gla_kernel(nlev, nb, *refs):
    ins_f, ins_b = refs[0:4], refs[4:8]
    cumq_ref, cumkt_ref, pm_ref, s0_ref, of_ref, ob_ref, sf_ref = refs[8:15]
    st_refs = refs[15:]
    step = pl.program_id(1)

    @pl.when(step == 0)
    def _():
        for bb in range(nb):
            for d_ in range(2):
                st_refs[2 * bb + d_][...] = s0_ref[bb, d_]

    chains = []
    for bb in range(nb):
        for d_, (ins, o_ref) in enumerate(((ins_f, of_ref), (ins_b, ob_ref))):
            q_ref, k_ref, v_ref, la_ref = ins
            chains.append(_gla_chain(nlev, q_ref[bb], k_ref[bb], la_ref[bb], v_ref.at[bb], cumq_ref[d_],
                                     cumkt_ref[d_], pm_ref.at[d_], st_refs[2 * bb + d_], o_ref.at[bb]))
    for _ in range(GLA_HEADS + 3):
        for ch in chains:
            next(ch)
    for bb in range(nb):
        for d_ in range(2):
            sf_ref[bb, d_] = st_refs[2 * bb + d_][...]


def _gla(q, k, v, la, s0, cumq, cumkt, pm, nlev):
    b, n, _ = q.shape
    c = GLA_BLOCK
    nc = n // c
    nb = next(c for c in (GLA_BATCHES_PER_STEP, 2, 1) if b % c == 0)
    specs = []
    for d_ in range(2):
        pos = (lambda s_: s_) if d_ == 0 else (lambda s_: nc - 1 - s_)
        specs += [pl.BlockSpec((nb, c, 256), lambda bi, s_, pos=pos: (bi, pos(s_), 0)),
                  pl.BlockSpec((nb, c, 256), lambda bi, s_, pos=pos: (bi, pos(s_), 0)),
                  pl.BlockSpec((nb, c, 512), lambda bi, s_, pos=pos: (bi, pos(s_), 0)),
                  pl.BlockSpec((nb, c, 256), lambda bi, s_, pos=pos, d_=d_: (bi, pos(s_), d_))]
    st_spec = pl.BlockSpec((nb, 2, GLA_HEADS, GLA_DV, LANES), lambda bi, s_: (bi, 0, 0, 0, 0))
    full = lambda a: pl.BlockSpec(a.shape, lambda bi, s_: (0,) * a.ndim)
    return pl.pallas_call(
        functools.partial(_gla_kernel, nlev, nb),
        out_shape=[jax.ShapeDtypeStruct((b, n, GLA_V), F32), jax.ShapeDtypeStruct((b, n, GLA_V), F32),
                   jax.ShapeDtypeStruct((b, 2, GLA_HEADS, GLA_DV, LANES), F32)],
        grid=(b // nb, nc),
        in_specs=specs + [full(cumq), full(cumkt), full(pm), st_spec],
        out_specs=[pl.BlockSpec((nb, c, GLA_V), lambda bi, s_: (bi, s_, 0)),
                   pl.BlockSpec((nb, c, GLA_V), lambda bi, s_: (bi, nc - 1 - s_, 0)), st_spec],
        scratch_shapes=[pltpu.VMEM((GLA_HEADS, GLA_DV, LANES), F32) for _ in range(2 * nb)],
        compiler_params=_cparams(("parallel", "arbitrary")),
    )(q, k, v, la, q, k, v, la, cumq, cumkt, pm, s0)


def _odd_out_router_kernel(x_ref, of_ref, ob_ref, r_ref, dl_ref, gg_ref, w_ref, mod_ref, gn_ref, wr_ref, u_ref,
                           o_ref, *route_refs):
    d = D_MODEL
    o = of_ref[...] + ob_ref[...]
    gg = gg_ref[...]
    r = r_ref[...]
    parts = []
    for hd in range(GLA_HEADS):
        vs = slice(hd * GLA_DV, (hd + 1) * GLA_DV)
        oh = o[:, vs]
        parts.append(oh * lax.rsqrt(jnp.mean(oh * oh, axis=-1, keepdims=True) + EPS) * gg[:, vs])
    cl = (jnp.concatenate(parts, axis=1) * (r * jax.nn.sigmoid(r))).astype(BF16)
    y = _dot(cl, w_ref[0:512, :]) + _dot(dl_ref[...], w_ref[512:1024, :])
    x = x_ref[...] + mod_ref[:, 2 * d:3 * d] * y
    o_ref[...] = x
    _route_tile(x, mod_ref, gn_ref, wr_ref, u_ref, *route_refs)


def _odd_out_router(x, o_fwd, o_bwd, r, dl, gg, w, mod, gn, wr, u):
    b, n, d = x.shape
    tm = MOE_TILE
    flat = lambda t: t.reshape(-1, t.shape[-1])
    act = lambda wd: pl.BlockSpec((tm, wd), lambda i: (i, 0))
    full = lambda t: pl.BlockSpec(t.shape, lambda i: (0,) * t.ndim)
    rshapes, rspecs, rscratch = _route_out(b * n, d)
    outs = pl.pallas_call(
        _odd_out_router_kernel,
        out_shape=[jax.ShapeDtypeStruct((b * n, d), F32)] + rshapes,
        grid=(b * n // tm,),
        in_specs=[act(d), act(GLA_V), act(GLA_V), act(512), act(512), full(gg), full(w),
                  pl.BlockSpec((None, 1, 6 * d), lambda i: (i // (n // tm), 0, 0)), full(gn), full(wr), full(u)],
        out_specs=[act(d)] + rspecs,
        scratch_shapes=rscratch,
        compiler_params=_cparams(("arbitrary",)),
    )(flat(x), flat(o_fwd), flat(o_bwd), flat(r), flat(dl), gg, w, mod, gn, wr, u)
    return outs[0].reshape(x.shape), outs[1:]


def _route_tile(x, mod_ref, gn_ref, wr_ref, u_ref, h_ref, e_ref, wt_ref, r_ref, cnt_ref, carry_ref):
    d = D_MODEL
    tm = x.shape[0]

    @pl.when(pl.program_id(0) == 0)
    def _():
        carry_ref[...] = jnp.zeros_like(carry_ref)

    mod = mod_ref[...]
    h = _rms(x, gn_ref[...]) * (1.0 + mod[:, 4 * d:5 * d]) + mod[:, 3 * d:4 * d]
    h_ref[...] = _pack_bf16_pairs(h)
    h_hi, h_lo = _split2(h)
    w_hi, w_lo = _split2(wr_ref[...])
    lg = _dot_nt(w_hi, h_hi) + _dot_nt(w_lo, h_hi) + _dot_nt(w_hi, h_lo)
    rid = lax.broadcasted_iota(I32, (8, tm), 0)
    gl = jnp.where(rid < MOE_GROUPS, lg[0:8], NEG)
    gmax = jnp.max(gl, axis=0, keepdims=True)
    gsel = jnp.min(jnp.where(gl == gmax, rid, 8), axis=0, keepdims=True)
    pmax = 1.0 / jnp.sum(jnp.where(rid < MOE_GROUPS, jnp.exp(gl - gmax), 0.0), axis=0, keepdims=True)
    e_in = jnp.zeros((MOE_PER_GROUP, tm), F32)
    for g in range(MOE_GROUPS):
        e_in = e_in + jnp.where(gsel == g, lg[8 + 8 * g:16 + 8 * g], 0.0)
    v1 = jnp.max(e_in, axis=0, keepdims=True)
    i1 = jnp.min(jnp.where(e_in == v1, rid, 8), axis=0, keepdims=True)
    e_rest = jnp.where(rid == i1, -jnp.inf, e_in)
    v2 = jnp.max(e_rest, axis=0, keepdims=True)
    i2 = jnp.min(jnp.where(e_rest == v2, rid, 8), axis=0, keepdims=True)
    t = jnp.exp(v2 - v1)
    w1 = pmax / (1.0 + t)
    w2 = pmax * t / (1.0 + t)
    e1 = gsel * MOE_PER_GROUP + i1
    e2 = gsel * MOE_PER_GROUP + i2
    eid = lax.broadcasted_iota(I32, (MOE_EXPERTS, tm), 0)
    oh1 = jnp.where(eid == e1, 1.0, 0.0)
    oh2 = jnp.where(eid == e2, 1.0, 0.0)
    ohs = oh1 + oh2
    base = carry_ref[:, 0:1] + _dot(ohs.astype(BF16), u_ref[...])
    r1 = jnp.sum(oh1 * base, axis=0, keepdims=True)
    r2 = jnp.sum(oh2 * base, axis=0, keepdims=True)
    carry_ref[...] = carry_ref[...] + jnp.sum(ohs, axis=1, keepdims=True)
    cnt_ref[...] = carry_ref[...]
    e_ref[...] = jnp.concatenate([e1, e2], axis=0)
    r_ref[...] = jnp.concatenate([r1, r2], axis=0).astype(I32)
    w8 = jnp.concatenate([w1, w2, jnp.zeros((6, tm), F32)], axis=0)
    wt_ref[...] = w8.T


def _route_out(n, d):
    tm = MOE_TILE
    shapes = [jax.ShapeDtypeStruct((n, d // 2), I32), jax.ShapeDtypeStruct((2, n), I32),
              jax.ShapeDtypeStruct((n, 8), F32), jax.ShapeDtypeStruct((2, n), I32),
              jax.ShapeDtypeStruct((MOE_EXPERTS, LANES), F32)]
    specs = [pl.BlockSpec((tm, d // 2), lambda i: (i, 0)), pl.BlockSpec((2, tm), lambda i: (0, i)),
             pl.BlockSpec((tm, 8), lambda i: (i, 0)), pl.BlockSpec((2, tm), lambda i: (0, i)),
             pl.BlockSpec((MOE_EXPERTS, LANES), lambda i: (0, 0))]
    return shapes, specs, [pltpu.VMEM((MOE_EXPERTS, LANES), F32)]


def _even_out_router_kernel(nlat, x_ref, xc_ref, a_ref, ac_ref, b_ref, bc_ref, w_ref, mod_ref, gn_ref, wr_ref, u_ref,
                            xo_ref, xoc_ref, *route_refs):
    d = D_MODEL
    lat = pl.program_id(0) < nlat
    a = jnp.where(lat, a_ref[...], ac_ref[...])
    b = jnp.where(lat, b_ref[...], bc_ref[...])
    y = _dot(a, w_ref[0:512, :]) + _dot(b, w_ref[512:1024, :])
    x = jnp.where(lat, x_ref[...], xc_ref[...]) + mod_ref[:, 2 * d:3 * d] * y

    @pl.when(lat)
    def _():
        xo_ref[...] = x

    @pl.when(jnp.logical_not(lat))
    def _():
        xoc_ref[...] = x

    _route_tile(x, mod_ref, gn_ref, wr_ref, u_ref, *route_refs)


def _even_out_router(x, xc, oa, oac, ob, obc, w, mod, ctx_row, gn, wr, u):
    b, n, d = x.shape
    tm = MOE_TILE
    flat = lambda t: t.reshape(-1, t.shape[-1])
    nlat = b * n // tm
    ntok = b * n + xc.shape[0] * xc.shape[1]
    lat = lambda wd: pl.BlockSpec((tm, wd), lambda i: (jnp.minimum(i, nlat - 1), 0))
    ctx = lambda wd: pl.BlockSpec((tm, wd), lambda i: (jnp.maximum(i - nlat, 0), 0))
    full = lambda t: pl.BlockSpec(t.shape, lambda i: (0,) * t.ndim)
    modrow = lambda i: (jnp.where(i < nlat, i // (n // tm), ctx_row), 0, 0)
    rshapes, rspecs, rscratch = _route_out(ntok, d)
    outs = pl.pallas_call(
        functools.partial(_even_out_router_kernel, nlat),
        out_shape=[jax.ShapeDtypeStruct((b * n, d), F32), jax.ShapeDtypeStruct((ntok - b * n, d), F32)] + rshapes,
        grid=(ntok // tm,),
        in_specs=[lat(d), ctx(d), lat(512), ctx(512), lat(512), ctx(512), full(w),
                  pl.BlockSpec((None, 1, 6 * d), modrow), full(gn), full(wr), full(u)],
        out_specs=[lat(d), ctx(d)] + rspecs,
        scratch_shapes=rscratch,
        compiler_params=_cparams(("arbitrary",)),
    )(flat(x), flat(xc), flat(oa), flat(oac), flat(ob), flat(obc), w, mod, gn, wr, u)
    return outs[0].reshape(x.shape), outs[1].reshape(xc.shape), outs[2:]


def _sc_permute_rows(src, dest, scatter):
    rows, d = dest.shape[0], src.shape[1]
    n = rows // 2
    info = plsc.get_sparse_core_info()
    workers = info.num_cores * info.num_subcores
    per_worker = rows // workers
    chunk = next(c for c in SC_CHUNKS if per_worker % c == 0)
    assert rows == per_worker * workers and n % per_worker == 0
    mesh = plsc.VectorSubcoreMesh(core_axis_name="c", subcore_axis_name="s")

    def body(src_hbm, dest_hbm, out_hbm, idx_v, rows_v, sem):
        base = (lax.axis_index("s") * info.num_cores + lax.axis_index("c")) * per_worker

        @pl.loop(0, per_worker // chunk)
        def _(j):
            a0 = base + j * chunk
            pltpu.sync_copy(dest_hbm.at[pl.ds(a0, chunk)], idx_v)
            if scatter:
                t0 = jnp.where(a0 >= n, a0 - n, a0)
                pltpu.sync_copy(src_hbm.at[pl.ds(t0, chunk)], rows_v)
                pltpu.async_copy(rows_v, out_hbm.at[idx_v], sem).wait()
            else:
                pltpu.async_copy(src_hbm.at[idx_v], rows_v, sem).wait()
                pltpu.sync_copy(rows_v, out_hbm.at[pl.ds(a0, chunk)])

    return pl.kernel(
        body, out_type=jax.ShapeDtypeStruct((rows, d), src.dtype), mesh=mesh,
        scratch_types=[pltpu.VMEM((chunk,), I32), pltpu.VMEM((chunk, d), src.dtype), pltpu.SemaphoreType.DMA],
    )(src, dest)


def _gmm_kernel(layer, vt_ref, ve_ref, vlo_ref, vhi_ref, vfirst_ref, vslot_ref, vnext_ref, nv_ref,
                xs_ref, wg_hbm, wu_hbm, wd_hbm, ys_ref, wgs_ref, wus_ref, wds_ref, wgb_ref, wub_ref, wdb_ref, sem):
    del vt_ref
    v = pl.program_id(0)

    def fetch(e, slot):
        return [pltpu.make_async_copy(w.at[layer, e], s.at[slot], sem.at[slot, i])
                for i, (w, s) in enumerate(((wg_hbm, wgs_ref), (wu_hbm, wus_ref), (wd_hbm, wds_ref)))]

    @pl.when(v < nv_ref[0])
    def _():
        @pl.when((v == 0) | (ve_ref[v] != ve_ref[jnp.maximum(v - 1, 0)]))
        def _():
            slot = vslot_ref[v]

            @pl.when(v == 0)
            def _():
                for c in fetch(ve_ref[0], 0):
                    c.start()

            for c in fetch(ve_ref[v], slot):
                c.wait()
            wgb_ref[...] = wgs_ref[slot].astype(BF16)
            wub_ref[...] = wus_ref[slot].astype(BF16)
            wdb_ref[...] = wds_ref[slot].astype(BF16)

            @pl.when(vnext_ref[v] >= 0)
            def _():
                for c in fetch(vnext_ref[v], 1 - slot):
                    c.start()

        x = _unpack_bf16_pairs(xs_ref[...]).astype(BF16)
        g = _dot(x, wgb_ref[...])
        u = _dot(x, wub_ref[...])
        y = _pack_bf16_pairs(_dot((g * jax.nn.sigmoid(g) * u).astype(BF16), wdb_ref[...]))
        row = lax.broadcasted_iota(I32, (y.shape[0], 1), 0)
        mine = (row >= vlo_ref[v]) & (row < vhi_ref[v])

        @pl.when(vfirst_ref[v] == 1)
        def _():
            ys_ref[...] = jnp.where(mine, y, 0)

        @pl.when(vfirst_ref[v] == 0)
        def _():
            ys_ref[...] = jnp.where(mine, y, ys_ref[...])


def _gmm(xs, visits, layer, wg, wu, wd):
    rows, dw = xs.shape
    d = 2 * dw
    tm = MOE_TILE
    hid = wg.shape[-1]
    nvis = rows // tm + MOE_EXPERTS - 1
    tile = lambda v, vt, *_: (vt[v], 0)
    anyspec = pl.BlockSpec(memory_space=pl.ANY)
    return pl.pallas_call(
        functools.partial(_gmm_kernel, layer),
        out_shape=jax.ShapeDtypeStruct((rows, dw), I32),
        grid_spec=pltpu.PrefetchScalarGridSpec(
            num_scalar_prefetch=8, grid=(nvis,),
            in_specs=[pl.BlockSpec((tm, dw), tile), anyspec, anyspec, anyspec],
            out_specs=pl.BlockSpec((tm, dw), tile),
            scratch_shapes=[pltpu.VMEM((2, d, hid), F32), pltpu.VMEM((2, d, hid), F32), pltpu.VMEM((2, hid, d), F32),
                            pltpu.VMEM((d, hid), BF16), pltpu.VMEM((d, hid), BF16), pltpu.VMEM((hid, d), BF16),
                            pltpu.SemaphoreType.DMA((2, 3))]),
        compiler_params=_cparams(("arbitrary",)),
    )(*visits, xs, wg, wu, wd)


def _combine_kernel(x_ref, wt_ref, mod_ref, fg_ref, y0_ref, y1_ref, o_ref):
    d = D_MODEL
    wt = wt_ref[...]
    y = wt[:, 0:1] * _unpack_bf16_pairs(y0_ref[...]) + wt[:, 1:2] * _unpack_bf16_pairs(y1_ref[...])
    o_ref[...] = _rms(x_ref[...] + mod_ref[:, 5 * d:6 * d] * y, fg_ref[...])


def _combine(x2, wt, mod, rows_per_mod, fg, yt):
    n, d = x2.shape
    tm = MOE_TILE
    ntiles = n // tm
    return pl.pallas_call(
        _combine_kernel,
        out_shape=jax.ShapeDtypeStruct((n, d), F32),
        grid=(ntiles,),
        in_specs=[pl.BlockSpec((tm, d), lambda i: (i, 0)), pl.BlockSpec((tm, 8), lambda i: (i, 0)),
                  pl.BlockSpec((None, 1, 6 * d), lambda i: (i // (rows_per_mod // tm), 0, 0)),
                  pl.BlockSpec(fg.shape, lambda i: (0, 0)),
                  pl.BlockSpec((tm, d // 2), lambda i: (i, 0)), pl.BlockSpec((tm, d // 2), lambda i: (ntiles + i, 0))],
        out_specs=pl.BlockSpec((tm, d), lambda i: (i, 0)),
        compiler_params=_cparams(("parallel",)),
    )(x2, wt, mod, fg, yt, yt)


def _pick(table, idx):
    hot = idx[..., None] == jnp.arange(table.shape[0], dtype=I32)
    return jnp.sum(jnp.where(hot, table, 0), axis=-1)


def _moe_plan(counts, e, r, rows):
    tm = MOE_TILE
    ends = jnp.cumsum(counts)
    starts = ends - counts
    dest = (_pick(starts, e) + r).reshape(-1)
    first_tile = starts // tm
    nvis = jnp.where(counts > 0, (ends - 1) // tm - first_tile + 1, 0)
    vend = jnp.cumsum(nvis)
    nv = vend[-1:]
    v = jnp.minimum(jnp.arange(rows // tm + MOE_EXPERTS - 1, dtype=I32), nv[0] - 1)
    ve = jnp.sum((vend[None, :] <= v[:, None]).astype(I32), axis=1)
    vt = _pick(first_tile, ve) + v - _pick(vend - nvis, ve)
    vlo = jnp.maximum(_pick(starts, ve) - vt * tm, 0)
    vhi = jnp.minimum(_pick(ends, ve) - vt * tm, tm)
    vfirst = jnp.concatenate([jnp.ones((1,), I32), (vt[1:] != vt[:-1]).astype(I32)])
    changed = jnp.concatenate([jnp.ones((1,), I32), (ve[1:] != ve[:-1]).astype(I32)])
    vslot = (jnp.cumsum(changed) - 1) % 2
    eid = jnp.arange(MOE_EXPERTS, dtype=I32)
    later = (eid[None, :] > eid[:, None]) & (counts[None, :] > 0)
    nxt = jnp.min(jnp.where(later, eid[None, :], MOE_EXPERTS), axis=1)
    vnext = _pick(jnp.where(nxt < MOE_EXPERTS, nxt, -1), ve)
    return dest, (vt, ve, vlo, vhi, vfirst, vslot.astype(I32), vnext.astype(I32), nv)


def _moe_experts(routed, layer, wg, wu, wd):
    h, e, wt, r, cnt = routed
    dest, visits = _moe_plan(cnt[:, 0].astype(I32), e, r, 2 * h.shape[0])
    xs = _sc_permute_rows(h, dest, scatter=True)
    ys = _gmm(xs, visits, layer, wg, wu, wd)
    return _sc_permute_rows(ys, dest, scatter=False), wt


def _rope_tables(rows, dim):
    row = jnp.repeat(jnp.arange(rows, dtype=F32), GRID_W)
    col = jnp.tile(jnp.arange(GRID_W, dtype=F32), rows)
    half = dim // 2
    inv = jnp.power(ROPE_BASE, -jnp.arange(0, half, 2, dtype=F32) / half)
    ar = row[:, None] * inv[None, :]
    ac = col[:, None] * inv[None, :]
    ang = jnp.concatenate([ar, ar, ac, ac], axis=-1)
    return jnp.cos(ang), jnp.sin(ang)


def _even_tables(n, with_rope):
    if with_rope:
        cm, sm = _rope_tables(n // GRID_W, MLA_ROPE)
        cwin, swin = _rope_tables(n // GRID_W, WIN_HEAD_DIM)
    else:
        cm, sm = jnp.ones((n, MLA_ROPE), F32), jnp.zeros((n, MLA_ROPE), F32)
        cwin, swin = jnp.ones((n, WIN_HEAD_DIM), F32), jnp.zeros((n, WIN_HEAD_DIM), F32)
    one, zero = jnp.ones((n, MLA_NOPE), F32), jnp.zeros((n, MLA_NOPE), F32)
    pad = jnp.zeros((n, LANES - MLA_NOPE - MLA_ROPE), F32)
    return (jnp.concatenate([one, cm, pad], axis=1), jnp.concatenate([zero, sm, pad], axis=1),
            jnp.concatenate([cwin, cwin], axis=1), jnp.concatenate([swin, swin], axis=1))


def _even_weights(w_in, qg, w_uq, kvg, w_ukv):
    d = w_in.shape[0]
    o = np.cumsum([0, MLA_Q_RANK, MLA_KV_RANK, MLA_ROPE, 512, 128, 128])
    cq, ckv, kr, qw, kw, vw = [w_in[:, o[i]:o[i + 1]] for i in range(6)]
    z = lambda c: jnp.zeros((d, c), F32)
    kr128 = jnp.concatenate([z(MLA_NOPE), kr, z(LANES - MLA_NOPE - MLA_ROPE)], axis=1)
    dup = lambda t: jnp.concatenate([t[:, 0:64], t[:, 0:64], t[:, 64:128], t[:, 64:128]], axis=1)
    win = jnp.concatenate([cq, ckv, kr128, qw, dup(kw), dup(vw)], axis=1).astype(BF16)
    uq = w_uq.reshape(MLA_Q_RANK, MLA_HEADS, MLA_NOPE + MLA_ROPE)
    uq = jnp.pad(uq, ((0, 0), (0, 0), (0, LANES - MLA_NOPE - MLA_ROPE))).reshape(MLA_Q_RANK, MLA_HEADS * LANES)
    ukv = w_ukv.reshape(MLA_KV_RANK, MLA_HEADS, MLA_NOPE + MLA_V)
    ukk = jnp.pad(ukv[:, :, :MLA_NOPE], ((0, 0), (0, 0), (0, LANES - MLA_NOPE))).reshape(MLA_KV_RANK, MLA_HEADS * LANES)
    ukvv = ukv[:, :, MLA_NOPE:].reshape(MLA_KV_RANK, MLA_HEADS * MLA_V)
    return (win, qg.reshape(1, -1), uq.astype(BF16), kvg.reshape(1, -1), ukk.astype(BF16), ukvv.T.astype(BF16))


def _odd_weights(w_in, w_g2, b_g, ln_g, ln_b, w_s, b_s):
    d = w_in.shape[0]
    o = np.cumsum([0, GLA_K, GLA_K, GLA_V, 2 * GLA_GATE_RANK, GLA_V, SG_WIDTH, SG_WIDTH])
    q, k, v, g, r, u, vg = [w_in[:, o[i]:o[i + 1]] for i in range(7)]
    g128 = jnp.concatenate([g, jnp.zeros((d, LANES - 2 * GLA_GATE_RANK), F32)], axis=1)
    win = jnp.concatenate([q, k, v, g128, r, u, vg], axis=1).astype(BF16)
    zr = jnp.zeros((GLA_GATE_RANK, GLA_K), F32)
    pad = jnp.zeros((LANES - 2 * GLA_GATE_RANK, GLA_K), F32)
    wg = jnp.concatenate([jnp.concatenate([w_g2[0], zr, pad], axis=0),
                          jnp.concatenate([zr, w_g2[1], pad], axis=0)], axis=1)
    bg = b_g.reshape(1, 2 * GLA_K)
    return (win, wg, bg, ln_g.reshape(1, -1), ln_b.reshape(1, -1), w_s.astype(BF16), b_s.T)


def kernel(x, c, ctx, c_ctx, ada_w, ada_b, norm_mix_g, norm_ffn_g, even_w_in, mla_q_norm_g, mla_w_uq, mla_kv_norm_g, mla_w_ukv, win_sink, even_w_out, odd_w_in, gla_w_g2, gla_b_g, gla_norm_g, sg_ln_g, sg_ln_b, sg_w_s, sg_b_s, odd_w_out, moe_w_rg, moe_w_re, moe_w_gate, moe_w_up, moe_w_down, final_norm_g):
    b, n, d = x.shape
    lc = ctx.shape[1]
    depth = ada_w.shape[0]
    assert depth == 2 and d == D_MODEL and b < 8
    assert n % 512 == 0 and lc % MOE_TILE == 0 and n % GRID_W == 0
    tm = 512 if n % 512 == 0 else 256
    tq = 256

    cond8 = jnp.concatenate([c, c_ctx[None, :], jnp.zeros((8 - b - 1, d), F32)], axis=0)
    mod_all = _adaln(cond8, ada_w, ada_b).reshape(depth, 8, 1, 6 * d)
    ctx_row = b
    u_tri = jnp.asarray(np.triu(np.ones((MOE_TILE, MOE_TILE), np.float32), 1), BF16)
    fg = final_norm_g.reshape(1, d)

    def router_w(layer):
        return jnp.concatenate([moe_w_rg[layer].T, jnp.zeros((8 - MOE_GROUPS, d), F32), moe_w_re[layer].T], axis=0)

    def moe_experts(routed, layer):
        return _moe_experts(routed, layer, moe_w_gate, moe_w_up, moe_w_down)

    mod = mod_all[0]
    gn = norm_mix_g[0].reshape(1, d)
    ew = _even_weights(even_w_in[0], mla_q_norm_g[0], mla_w_uq[0], mla_kv_norm_g[0], mla_w_ukv[0])
    qm_l, km_l, vm_l, qw_l, kw_l, vw_l = _even_in(x, mod, None, gn, ew, _even_tables(n, True), tm)
    qm_c, km_c, vm_c, qw_c, kw_c, vw_c = _even_in(ctx, mod, ctx_row, gn, ew, _even_tables(lc, False), lc)
    w_out = even_w_out[0].astype(BF16)
    sink = win_sink[0]
    oa_l = _mla_attn(qm_l, [(km_l, vm_l), (km_c, vm_c)], tq)
    ob_l = _gqa(qw_l, kw_l, vw_l, kw_c, vw_c, sink, True)
    oa_c = _mla_attn(qm_c, [(km_c, vm_c)], lc)
    ob_c = _gqa(qw_c, None, None, kw_c, vw_c, sink, False)
    xl, xc, routed = _even_out_router(x, ctx, oa_l, oa_c, ob_l, ob_c, w_out, mod, ctx_row,
                                      norm_ffn_g[0].reshape(1, d), router_w(0), u_tri)
    pending = moe_experts(routed, 0)

    mod = mod_all[1]
    gn = norm_mix_g[1].reshape(1, d)
    ow = _odd_weights(odd_w_in[0], gla_w_g2[0], gla_b_g[0], sg_ln_g[0], sg_ln_b[0], sg_w_s[0], sg_b_s[0])
    xl, q_l, k_l, v_l, la_l, r_l, dl_l = _odd_in(xl, pending, 0, mod_all[0], mod, None, gn, ow)
    _, q_c, k_c, v_c, la_c, _, _ = _odd_in(xc, pending, b * n // MOE_TILE, mod_all[0], mod, ctx_row, gn, ow)
    cumq_np, cumkt_np, pm_np, nlev = _gla_tables()
    cumq, cumkt = jnp.asarray(cumq_np, BF16), jnp.asarray(cumkt_np, BF16)
    pm = jnp.asarray(pm_np, F32)
    s0 = jnp.zeros((b, 2, GLA_HEADS, GLA_DV, LANES), F32)
    _, _, s_ctx = _gla(q_c, k_c, v_c, la_c, s0, cumq, cumkt, pm, nlev)
    o_fwd, o_bwd, _ = _gla(q_l, k_l, v_l, la_l, s_ctx, cumq, cumkt, pm, nlev)
    xl, routed = _odd_out_router(xl, o_fwd, o_bwd, r_l, dl_l, gla_norm_g[0].reshape(1, -1),
                                 odd_w_out[0].astype(BF16), mod, norm_ffn_g[1].reshape(1, d), router_w(1), u_tri)
    yt, wt = moe_experts(routed, 1)
    return _combine(xl.reshape(b * n, d), wt, mod, n, fg, yt).reshape(b, n, d)
```

```python
import functools

import numpy as np
import jax
import jax.numpy as jnp
from jax import lax
from jax.experimental import pallas as pl
from jax.experimental.pallas import tpu as pltpu
from jax.experimental.pallas import tpu_sc as plsc

F32 = jnp.float32
BF16 = jnp.bfloat16
I32 = jnp.int32

D_MODEL = 1024
GRID_W = 64
EPS = 1e-6
ROPE_BASE = 10000.0
MLA_HEADS = 8
MLA_Q_RANK = 256
MLA_KV_RANK = 128
MLA_NOPE = 64
MLA_ROPE = 32
MLA_V = 64
WIN_HEADS = 8
WIN_KV_HEADS = 2
WIN_HEAD_DIM = 64
WIN_BLOCK = 128
GLA_HEADS = 4
GLA_DK = 64
GLA_DV = 128
GLA_GATE_RANK = 16
GLA_TAU = 16.0
GLA_K = GLA_HEADS * GLA_DK
GLA_V = GLA_HEADS * GLA_DV
SG_GROUPS = 4
SG_CHUNK = 128
SG_WIDTH = 512
MOE_GROUPS = 4
MOE_PER_GROUP = 8
MOE_EXPERTS = 32
MOE_HIDDEN = 512

LANES = 128
GLA_BLOCK = 128
GLA_BATCHES_PER_STEP = 2
MOE_TILE = 256
COMBINE_TILE = 1024
MLA_KEY_CHUNK = 1024
SC_CHUNKS = (128, 64, 32)
NEG = -1e30
LOG2E = 1.4426950408889634
VMEM_LIMIT = 56 * 1024 * 1024


def _cparams(sem):
    return pltpu.CompilerParams(dimension_semantics=sem, vmem_limit_bytes=VMEM_LIMIT)


def _dot(a, b):
    return jnp.dot(a, b, preferred_element_type=F32)


def _dot_nt(a, b):
    return lax.dot_general(a, b, (((1,), (1,)), ((), ())), preferred_element_type=F32)


def _split2(a):
    hi = a.astype(BF16)
    lo = (a - hi.astype(F32)).astype(BF16)
    return hi, lo


def _split3(a):
    hi = a.astype(BF16)
    r = a - hi.astype(F32)
    mid = r.astype(BF16)
    lo = (r - mid.astype(F32)).astype(BF16)
    return hi, mid, lo


def _pack_bf16_pairs(x):
    k = x.shape[1] // 2
    bits = lax.bitcast_convert_type(x.astype(BF16).astype(F32), jnp.uint32)
    return lax.bitcast_convert_type(bits[:, :k] | (bits[:, k:] >> 16), I32)


def _unpack_bf16_pairs(w):
    bits = lax.bitcast_convert_type(w, jnp.uint32)
    hi = lax.bitcast_convert_type(bits & jnp.uint32(0xFFFF0000), F32)
    lo = lax.bitcast_convert_type(bits << 16, F32)
    return jnp.concatenate([hi, lo], axis=1)


def _rms(x, g):
    ms = jnp.mean(x * x, axis=-1, keepdims=True)
    return x * lax.rsqrt(ms + EPS) * g


def _lane_tile(t, reps):
    return t if reps == 1 else jnp.concatenate([t] * reps, axis=1)


def _rope(t, cos, sin, quarter):
    n = t.shape[1]
    lane = lax.broadcasted_iota(I32, t.shape, 1)
    first = (lane & (2 * quarter - 1)) < quarter
    rot = jnp.where(first, -pltpu.roll(t, n - quarter, 1), pltpu.roll(t, quarter, 1))
    return t * cos + rot * sin


def _adaln_kernel(c_ref, w_ref, b_ref, o_ref):
    c = c_ref[...]
    s_hi, s_lo = _split2(c * jax.nn.sigmoid(c))
    w_hi, w_lo = _split2(w_ref[...])
    o_ref[...] = _dot(s_hi, w_hi) + _dot(s_lo, w_hi) + _dot(s_hi, w_lo) + b_ref[...]


def _adaln(cond8, ada_w, ada_b):
    depth, d, n6 = ada_w.shape
    tn = 1536
    return pl.pallas_call(
        _adaln_kernel,
        out_shape=jax.ShapeDtypeStruct((depth, 8, n6), F32),
        grid=(depth, n6 // tn),
        in_specs=[
            pl.BlockSpec((8, d), lambda l, j: (0, 0)),
            pl.BlockSpec((None, d, tn), lambda l, j: (l, 0, j)),
            pl.BlockSpec((None, 1, tn), lambda l, j: (l, 0, j)),
        ],
        out_specs=pl.BlockSpec((None, 8, tn), lambda l, j: (l, 0, j)),
        compiler_params=_cparams(("parallel", "parallel")),
    )(cond8, ada_w, ada_b.reshape(depth, 1, n6))


def _even_in_kernel(x_ref, mod_ref, gn_ref, win_ref, qg_ref, wuq_ref, kvg_ref, wukk_ref, wukv_ref,
                    cq_ref, sq_ref, cw_ref, sw_ref,
                    qm_ref, km_ref, vm_ref, qw_ref, kw_ref, vw_ref):
    d = D_MODEL
    mod = mod_ref[...]
    h = _rms(x_ref[...], gn_ref[...]) * (1.0 + mod[:, d:2 * d]) + mod[:, 0:d]
    z = _dot(h.astype(BF16), win_ref[...])
    cq, sq, cw, sw = cq_ref[...], sq_ref[...], cw_ref[...], sw_ref[...]
    cqn = _rms(z[:, 0:256], qg_ref[...]).astype(BF16)
    q = _dot(cqn, wuq_ref[...])
    q = _rope(q, _lane_tile(cq, 8), _lane_tile(sq, 8), MLA_ROPE // 4)
    qm_ref[...] = (q * (LOG2E * (MLA_NOPE + MLA_ROPE) ** -0.5)).astype(BF16)
    ckvn = _rms(z[:, 256:384], kvg_ref[...]).astype(BF16)
    kn = _dot(ckvn, wukk_ref[...])
    kr = _rope(z[:, 384:512], cq, sq, MLA_ROPE // 4)
    km_ref[...] = (kn + _lane_tile(kr, 8)).astype(BF16)
    vm_ref[...] = _dot_nt(wukv_ref[...], ckvn).astype(BF16)
    qw = _rope(z[:, 512:1024], _lane_tile(cw, 4), _lane_tile(sw, 4), WIN_HEAD_DIM // 4)
    qw_ref[...] = (qw * (WIN_HEAD_DIM ** -0.5)).astype(BF16)
    kw = _rope(z[:, 1024:1280], _lane_tile(cw, 2), _lane_tile(sw, 2), WIN_HEAD_DIM // 4)
    kw_ref[...] = kw.astype(BF16)
    vw_ref[...] = z[:, 1280:1536].astype(BF16)


def _even_in(x, mod, mod_row, gn, wts, tabs, tm):
    b, n, d = x.shape
    win, qg, wuq, kvg, wukk, wukv = wts
    nt = n // tm
    row = (lambda bi, i: (bi, 0, 0)) if mod_row is None else (lambda bi, i: (mod_row, 0, 0))
    full = lambda a: pl.BlockSpec(a.shape, lambda bi, i: (0,) * a.ndim)
    tab = pl.BlockSpec((tm, LANES), lambda bi, i: (i, 0))
    outw = (1024, 1024, None, 512, 256, 256)
    rowspec = lambda w: pl.BlockSpec((None, tm, w), lambda bi, i: (bi, i, 0))
    colspec = pl.BlockSpec((None, 512, tm), lambda bi, i: (bi, 0, i))
    return pl.pallas_call(
        _even_in_kernel,
        out_shape=[jax.ShapeDtypeStruct((b, 512, n) if w is None else (b, n, w), BF16) for w in outw],
        grid=(b, nt),
        in_specs=[pl.BlockSpec((None, tm, d), lambda bi, i: (bi, i, 0)),
                  pl.BlockSpec((None, 1, 6 * d), row),
                  full(gn), full(win), full(qg), full(wuq), full(kvg), full(wukk), full(wukv),
                  tab, tab, tab, tab],
        out_specs=[colspec if w is None else rowspec(w) for w in outw],
        compiler_params=_cparams(("parallel", "parallel")),
    )(x, mod, gn, win, qg, wuq, kvg, wukk, wukv, *tabs)


def _mla_attn_kernel(nseg, q_ref, *refs):
    ks, vts = refs[0:2 * nseg:2], refs[1:2 * nseg:2]
    o_ref = refs[2 * nseg]
    s_bufs = refs[2 * nseg + 1:2 * nseg + 3]
    p_bufs = refs[2 * nseg + 3:2 * nseg + 5]
    pieces, base = [], 0
    for k in ks:
        n = k.shape[0]
        pieces += [(k, c0, min(n, c0 + MLA_KEY_CHUNK), base + c0) for c0 in range(0, n, MLA_KEY_CHUNK)]
        base += n

    def score_chunk(h, piece, buf):
        k, c0, c1, g0 = piece
        hs = slice(h * LANES, (h + 1) * LANES)
        s = _dot_nt(k[c0:c1, hs], q_ref[:, hs])
        buf[g0:g0 + c1 - c0, :] = s
        return jnp.max(s, axis=0, keepdims=True)

    def prob_chunk(piece, sbuf, pbuf, m):
        _, c0, c1, g0 = piece
        p = jnp.exp2(sbuf[g0:g0 + c1 - c0, :] - m)
        pbuf[g0:g0 + c1 - c0, :] = p.astype(BF16)
        return jnp.sum(p, axis=0, keepdims=True)

    m_next = functools.reduce(jnp.maximum, [score_chunk(0, pc, s_bufs[0]) for pc in pieces])
    outs = []
    for h in range(MLA_HEADS):
        m_cur, maxes, sums = m_next, [], []
        for pc in pieces:
            if h + 1 < MLA_HEADS:
                maxes.append(score_chunk(h + 1, pc, s_bufs[(h + 1) % 2]))
            sums.append(prob_chunk(pc, s_bufs[h % 2], p_bufs[h % 2], m_cur))
        if h + 1 < MLA_HEADS:
            m_next = functools.reduce(jnp.maximum, maxes)
        l = functools.reduce(jnp.add, sums)
        vrows = slice(h * MLA_V, (h + 1) * MLA_V)
        ot, base = None, 0
        for k, vt in zip(ks, vts):
            n = k.shape[0]
            part = _dot(vt[vrows, :], p_bufs[h % 2][base:base + n, :])
            ot = part if ot is None else ot + part
            base += n
        outs.append(ot * (1.0 / l))
    o_ref[...] = jnp.concatenate(outs, axis=0).T.astype(BF16)


def _mla_attn(q, segs, tq):
    b, n, _ = q.shape
    in_specs = [pl.BlockSpec((None, tq, 1024), lambda bi, i: (bi, i, 0))]
    args = [q]
    keys = 0
    for k, vt in segs:
        lk = k.shape[1]
        keys += lk
        in_specs += [pl.BlockSpec((None, lk, 1024), lambda bi, i: (bi, 0, 0)),
                     pl.BlockSpec((None, 512, lk), lambda bi, i: (bi, 0, 0))]
        args += [k, vt]
    return pl.pallas_call(
        functools.partial(_mla_attn_kernel, len(segs)),
        out_shape=jax.ShapeDtypeStruct((b, n, 512), BF16),
        grid=(b, n // tq),
        in_specs=in_specs,
        out_specs=pl.BlockSpec((None, tq, 512), lambda bi, i: (bi, i, 0)),
        scratch_shapes=[pltpu.VMEM((keys, tq), F32), pltpu.VMEM((keys, tq), F32),
                        pltpu.VMEM((keys, tq), BF16), pltpu.VMEM((keys, tq), BF16)],
        compiler_params=_cparams(("parallel", "parallel")),
    )(*args)


def _gqa_kernel(has_win, nb, sink_ref, q_ref, *refs):
    if has_win:
        kp, kc, kn, vp, vc, vn, kx, vx, o_ref = refs
    else:
        kx, vx, o_ref = refs
    tq = q_ref.shape[0]
    i = pl.program_id(1)
    lane = lax.broadcasted_iota(I32, (tq, LANES), 1)
    row2 = lax.broadcasted_iota(I32, (2 * tq, 1), 0)
    half = WIN_HEAD_DIM
    npair = WIN_HEADS // 2
    kcats, vcats = [], []
    for g in range(WIN_KV_HEADS):
        gs = slice(g * LANES, (g + 1) * LANES)
        if has_win:
            kcats.append(jnp.concatenate([kp[:, gs], kc[:, gs], kn[:, gs], kx[:, gs]], axis=0))
            vcats.append(jnp.concatenate([vp[:, gs], vc[:, gs], vn[:, gs], vx[:, gs]], axis=0))
        else:
            kcats.append(kx[:, gs])
            vcats.append(vx[:, gs])
    scores = []
    for j in range(npair):
        qp = q_ref[:, j * LANES:(j + 1) * LANES]
        zero = jnp.zeros_like(qp)
        q2 = jnp.concatenate([jnp.where(lane < half, qp, zero), jnp.where(lane >= half, qp, zero)], axis=0)
        scores.append(_dot_nt(q2, kcats[j // 2]))
    if has_win:
        w = WIN_BLOCK
        r = lax.broadcasted_iota(I32, scores[0].shape, 0) & (tq - 1)
        c = lax.broadcasted_iota(I32, scores[0].shape, 1)
        big = jnp.int32(1 << 20)
        no_prev = jnp.where(i > 0, 0, big)
        no_next = jnp.where(i < nb - 1, 0, big)
        ok_prev = c >= r + no_prev
        ok_next = (c - 2 * w) <= r - no_next
        valid = ((c >= w) | ok_prev) & ((c < 2 * w) | (c >= 3 * w) | ok_next)
    probs, inv = [], []
    for j in range(npair):
        s = jnp.where(valid, scores[j], NEG) if has_win else scores[j]
        sk = jnp.where(row2 < tq, sink_ref[2 * j], sink_ref[2 * j + 1])
        m = jnp.maximum(jnp.max(s, axis=-1, keepdims=True), sk)
        p = jnp.exp(s - m)
        inv.append(1.0 / (jnp.sum(p, axis=-1, keepdims=True) + jnp.exp(sk - m)))
        probs.append(p.astype(BF16))
    for j in range(npair):
        o2 = _dot(probs[j], vcats[j // 2]) * inv[j]
        o_ref[:, j * LANES:(j + 1) * LANES] = jnp.where(lane < half, o2[:tq], o2[tq:]).astype(BF16)


def _gqa(q, k, v, kx, vx, sink, has_win):
    b, n, _ = q.shape
    lc = kx.shape[1]
    smem = pl.BlockSpec(memory_space=pltpu.SMEM)
    ctxs = pl.BlockSpec((None, lc, 256), lambda bi, i: (bi, 0, 0))
    if has_win:
        tq = WIN_BLOCK
        nb = n // tq
        blk = lambda f: pl.BlockSpec((None, tq, 256), f)
        prev = lambda bi, i: (bi, jnp.maximum(i - 1, 0), 0)
        cur = lambda bi, i: (bi, i, 0)
        nxt = lambda bi, i: (bi, jnp.minimum(i + 1, nb - 1), 0)
        in_specs = [smem, pl.BlockSpec((None, tq, 512), cur),
                    blk(prev), blk(cur), blk(nxt), blk(prev), blk(cur), blk(nxt), ctxs, ctxs]
        args = (sink, q, k, k, k, v, v, v, kx, vx)
    else:
        tq, nb = n, 1
        in_specs = [smem, pl.BlockSpec((None, tq, 512), lambda bi, i: (bi, i, 0)), ctxs, ctxs]
        args = (sink, q, kx, vx)
    return pl.pallas_call(
        functools.partial(_gqa_kernel, has_win, nb),
        out_shape=jax.ShapeDtypeStruct((b, n, 512), BF16),
        grid=(b, nb),
        in_specs=in_specs,
        out_specs=pl.BlockSpec((None, tq, 512), lambda bi, i: (bi, i, 0)),
        compiler_params=_cparams(("parallel", "parallel")),
    )(*args)


def _log_sigmoid(z):
    return jnp.minimum(z, 0.0) - jnp.log(1.0 + jnp.exp(-jnp.abs(z)))


def _odd_in_kernel(x_ref, y0_ref, y1_ref, wt_ref, modp_ref, mod_ref, gn_ref, win_ref, wg_ref, bg_ref, lng_ref,
                   lnb_ref, ws_ref, bst_ref, xn_ref, q_ref, k_ref, v_ref, la_ref, r_ref, dl_ref):
    d = D_MODEL
    tm = x_ref.shape[0]
    wt = wt_ref[...]
    y = wt[:, 0:1] * _unpack_bf16_pairs(y0_ref[...]) + wt[:, 1:2] * _unpack_bf16_pairs(y1_ref[...])
    x = x_ref[...] + modp_ref[:, 5 * d:6 * d] * y
    xn_ref[...] = x
    mod = mod_ref[...]
    h = (_rms(x, gn_ref[...]) * (1.0 + mod[:, d:2 * d]) + mod[:, 0:d]).astype(BF16)
    z = _dot(h, win_ref[...])
    q_ref[...] = z[:, 0:256] * (GLA_DK ** -0.5)
    k_ref[...] = z[:, 256:512]
    v_ref[...] = z[:, 512:1024].astype(BF16)
    g_hi, g_lo = _split2(z[:, 1024:1152])
    w_hi, w_lo = _split2(wg_ref[...])
    zg = _dot(g_hi, w_hi) + _dot(g_lo, w_hi) + _dot(g_hi, w_lo) + bg_ref[...]
    la_ref[...] = _log_sigmoid(zg) / GLA_TAU
    r_ref[...] = z[:, 1152:1664]
    u = jax.nn.gelu(z[:, 1664:2176])
    vg = jax.nn.gelu(z[:, 2176:2688])
    mu = jnp.mean(vg, axis=-1, keepdims=True)
    vc = vg - mu
    var = jnp.mean(vc * vc, axis=-1, keepdims=True)
    vn = (vc * lax.rsqrt(var + EPS) * lng_ref[...] + lnb_ref[...]).astype(BF16)
    bst = bst_ref[...]
    for c in range(tm // SG_CHUNK):
        rows = slice(c * SG_CHUNK, (c + 1) * SG_CHUNK)
        parts = []
        for g in range(SG_GROUPS):
            cols = slice(g * LANES, (g + 1) * LANES)
            parts.append(_dot(ws_ref[g], vn[rows, cols]) + bst[:, g:g + 1])
        dl_ref[rows, :] = (u[rows, :] * jnp.concatenate(parts, axis=1)).astype(BF16)


def _odd_in(x, pending, tile0, modp, mod, mod_row, gn, wts):
    b, n, d = x.shape
    tm = MOE_TILE
    yt, wt = pending
    ntiles = wt.shape[0] // tm
    win, wg, bg, lng, lnb, ws, bst = wts
    row = (lambda bi, i: (bi, 0, 0)) if mod_row is None else (lambda bi, i: (mod_row, 0, 0))
    full = lambda a: pl.BlockSpec(a.shape, lambda bi, i: (0,) * a.ndim)
    act = lambda wd: pl.BlockSpec((None, tm, wd), lambda bi, i: (bi, i, 0))
    tok = lambda bi, i: tile0 + bi * (n // tm) + i
    outs = [((b, n, d), F32, act(d)),
            ((b, n, 256), F32, act(256)), ((b, n, 256), F32, act(256)), ((b, n, 512), BF16, act(512)),
            ((b, n, 512), F32, act(512)), ((b, n, 512), F32, act(512)), ((b, n, 512), BF16, act(512))]
    return pl.pallas_call(
        _odd_in_kernel,
        out_shape=[jax.ShapeDtypeStruct(s, t) for s, t, _ in outs],
        grid=(b, n // tm),
        in_specs=[act(d), pl.BlockSpec((tm, d // 2), lambda bi, i: (tok(bi, i), 0)),
                  pl.BlockSpec((tm, d // 2), lambda bi, i: (ntiles + tok(bi, i), 0)),
                  pl.BlockSpec((tm, 8), lambda bi, i: (tok(bi, i), 0)),
                  pl.BlockSpec((None, 1, 6 * d), row), pl.BlockSpec((None, 1, 6 * d), row),
                  full(gn), full(win), full(wg), full(bg), full(lng), full(lnb), full(ws), full(bst)],
        out_specs=[sp for _, _, sp in outs],
        compiler_params=_cparams(("parallel", "parallel")),
    )(x, yt, yt, wt, modp, mod, gn, win, wg, bg, lng, lnb, ws, bst)


def _gla_tables():
    c = GLA_BLOCK
    t = np.arange(c)[:, None]
    u = np.arange(c)[None, :]
    levels = [c >> i for i in range(int(np.log2(c)) + 1)]
    cum = np.zeros((2, 2 * len(levels), c, c), np.float32)
    pair = np.zeros((2, len(levels), c, c), np.float32)
    for li, m in enumerate(levels):
        same = (t // m) == (u // m)
        cum[0, 2 * li] = same & (u <= t)
        cum[0, 2 * li + 1] = same & (u > t)
        cum[1, 2 * li] = same & (u >= t)
        cum[1, 2 * li + 1] = same & (u < t)
        if li > 0:
            pair[0, li] = ((t // m) % 2 == 1) & ((u // m) == (t // m) - 1)
            pair[1, li] = ((t // m) % 2 == 0) & ((u // m) == (t // m) + 1)
    pair[:, 0] = np.eye(c, dtype=np.float32)
    nlev = len(levels)
    m1 = cum[:, 0::2].reshape(2, nlev * c, c)
    m2t = np.concatenate([cum[:, 2 * li + 1].transpose(0, 2, 1) for li in range(nlev)], axis=2)
    return np.concatenate([m1, m1], axis=2), np.concatenate([m2t, m2t], axis=1), pair, nlev


def _gla_chain(nlev, q, k, la, v_ref, cumq, cumkt, pm_ref, st_ref, o_ref):
    c = GLA_BLOCK
    lat, kt = la.T, k.T
    l_hi, l_mid = _split2(la)
    t_hi, t_mid = _split2(lat)
    exq = jnp.exp(_dot(cumq, jnp.concatenate([l_hi, l_mid], axis=0)))
    exk = jnp.exp(_dot(jnp.concatenate([t_hi, t_mid], axis=1), cumkt))
    gcol = jnp.exp(jnp.sum(lat, axis=1, keepdims=True))
    yield
    qe = [(q * exq[li * c:(li + 1) * c]).astype(BF16) for li in range(nlev)]
    ke = [(kt * exk[:, li * c:(li + 1) * c]).astype(BF16) for li in range(nlev)]
    qb, kb = q.astype(BF16), kt.astype(BF16)
    states = [st_ref[hd] for hd in range(GLA_HEADS)]
    yield
    outs, new_states = [], []
    lane = lax.broadcasted_iota(I32, (c, LANES), 1)
    srow = lax.broadcasted_iota(I32, (LANES, 1), 0)
    zero = jnp.zeros((c, LANES), BF16)
    for hd in range(GLA_HEADS):
        ps = slice((hd // 2) * LANES, (hd // 2 + 1) * LANES)
        vs = slice(hd * GLA_DV, (hd + 1) * GLA_DV)
        mine = (lane < GLA_DK) if hd % 2 == 0 else (lane >= GLA_DK)
        mine_row = (srow < GLA_DK) if hd % 2 == 0 else (srow >= GLA_DK)
        pick = lambda t: jnp.where(mine, t[:, ps], zero)
        a = pm_ref[0] * _dot(pick(qb), kb[ps, :])
        for li in range(1, nlev):
            a = a + pm_ref[li] * _dot(pick(qe[li]), ke[li][ps, :])
        v_h = v_ref[:, vs]
        outs.append(_dot(qe[0][:, ps], states[hd].astype(BF16)) + _dot(a.astype(BF16), v_h))
        new_states.append(states[hd] * gcol[ps, :] + jnp.where(mine_row, _dot(ke[0][ps, :], v_h), 0.0))
        yield
    o_ref[...] = jnp.concatenate(outs, axis=1)
    for hd in range(GLA_HEADS):
        st_ref[hd] = new_states[hd]
    yield


def _gla_kernel(nlev, nb, *refs):
    ins_f, ins_b = refs[0:4], refs[4:8]
    cumq_ref, cumkt_ref, pm_ref, s0_ref, of_ref, ob_ref, sf_ref = refs[8:15]
    st_refs = refs[15:]
    step = pl.program_id(1)

    @pl.when(step == 0)
    def _():
        for bb in range(nb):
            for d_ in range(2):
                st_refs[2 * bb + d_][...] = s0_ref[bb, d_]

    chains = []
    for bb in range(nb):
        for d_, (ins, o_ref) in enumerate(((ins_f, of_ref), (ins_b, ob_ref))):
            q_ref, k_ref, v_ref, la_ref = ins
            chains.append(_gla_chain(nlev, q_ref[bb], k_ref[bb], la_ref[bb], v_ref.at[bb], cumq_ref[d_],
                                     cumkt_ref[d_], pm_ref.at[d_], st_refs[2 * bb + d_], o_ref.at[bb]))
    for _ in range(GLA_HEADS + 3):
        for ch in chains:
            next(ch)
    for bb in range(nb):
        for d_ in range(2):
            sf_ref[bb, d_] = st_refs[2 * bb + d_][...]


def _gla(q, k, v, la, s0, cumq, cumkt, pm, nlev):
    b, n, _ = q.shape
    c = GLA_BLOCK
    nc = n // c
    nb = next(c for c in (GLA_BATCHES_PER_STEP, 2, 1) if b % c == 0)
    specs = []
    for d_ in range(2):
        pos = (lambda s_: s_) if d_ == 0 else (lambda s_: nc - 1 - s_)
        specs += [pl.BlockSpec((nb, c, 256), lambda bi, s_, pos=pos: (bi, pos(s_), 0)),
                  pl.BlockSpec((nb, c, 256), lambda bi, s_, pos=pos: (bi, pos(s_), 0)),
                  pl.BlockSpec((nb, c, 512), lambda bi, s_, pos=pos: (bi, pos(s_), 0)),
                  pl.BlockSpec((nb, c, 256), lambda bi, s_, pos=pos, d_=d_: (bi, pos(s_), d_))]
    st_spec = pl.BlockSpec((nb, 2, GLA_HEADS, GLA_DV, LANES), lambda bi, s_: (bi, 0, 0, 0, 0))
    full = lambda a: pl.BlockSpec(a.shape, lambda bi, s_: (0,) * a.ndim)
    return pl.pallas_call(
        functools.partial(_gla_kernel, nlev, nb),
        out_shape=[jax.ShapeDtypeStruct((b, n, GLA_V), F32), jax.ShapeDtypeStruct((b, n, GLA_V), F32),
                   jax.ShapeDtypeStruct((b, 2, GLA_HEADS, GLA_DV, LANES), F32)],
        grid=(b // nb, nc),
        in_specs=specs + [full(cumq), full(cumkt), full(pm), st_spec],
        out_specs=[pl.BlockSpec((nb, c, GLA_V), lambda bi, s_: (bi, s_, 0)),
                   pl.BlockSpec((nb, c, GLA_V), lambda bi, s_: (bi, nc - 1 - s_, 0)), st_spec],
        scratch_shapes=[pltpu.VMEM((GLA_HEADS, GLA_DV, LANES), F32) for _ in range(2 * nb)],
        compiler_params=_cparams(("parallel", "arbitrary")),
    )(q, k, v, la, q, k, v, la, cumq, cumkt, pm, s0)


def _odd_out_router_kernel(x_ref, of_ref, ob_ref, r_ref, dl_ref, gg_ref, w_ref, mod_ref, gn_ref, wr_ref, u_ref,
                           o_ref, *route_refs):
    d = D_MODEL
    o = of_ref[...] + ob_ref[...]
    gg = gg_ref[...]
    r = r_ref[...]
    parts = []
    for hd in range(GLA_HEADS):
        vs = slice(hd * GLA_DV, (hd + 1) * GLA_DV)
        oh = o[:, vs]
        parts.append(oh * lax.rsqrt(jnp.mean(oh * oh, axis=-1, keepdims=True) + EPS) * gg[:, vs])
    cl = (jnp.concatenate(parts, axis=1) * (r * jax.nn.sigmoid(r))).astype(BF16)
    y = _dot(cl, w_ref[0:512, :]) + _dot(dl_ref[...], w_ref[512:1024, :])
    x = x_ref[...] + mod_ref[:, 2 * d:3 * d] * y
    o_ref[...] = x
    _route_tile(x, mod_ref, gn_ref, wr_ref, u_ref, *route_refs)


def _odd_out_router(x, o_fwd, o_bwd, r, dl, gg, w, mod, gn, wr, u):
    b, n, d = x.shape
    tm = MOE_TILE
    flat = lambda t: t.reshape(-1, t.shape[-1])
    act = lambda wd: pl.BlockSpec((tm, wd), lambda i: (i, 0))
    full = lambda t: pl.BlockSpec(t.shape, lambda i: (0,) * t.ndim)
    rshapes, rspecs, rscratch = _route_out(b * n, d)
    outs = pl.pallas_call(
        _odd_out_router_kernel,
        out_shape=[jax.ShapeDtypeStruct((b * n, d), F32)] + rshapes,
        grid=(b * n // tm,),
        in_specs=[act(d), act(GLA_V), act(GLA_V), act(512), act(512), full(gg), full(w),
                  pl.BlockSpec((None, 1, 6 * d), lambda i: (i // (n // tm), 0, 0)), full(gn), full(wr), full(u)],
        out_specs=[act(d)] + rspecs,
        scratch_shapes=rscratch,
        compiler_params=_cparams(("arbitrary",)),
    )(flat(x), flat(o_fwd), flat(o_bwd), flat(r), flat(dl), gg, w, mod, gn, wr, u)
    return outs[0].reshape(x.shape), outs[1:]


def _route_tile(x, mod_ref, gn_ref, wr_ref, u_ref, h_ref, e_ref, wt_ref, r_ref, cnt_ref, carry_ref):
    d = D_MODEL
    tm = x.shape[0]

    @pl.when(pl.program_id(0) == 0)
    def _():
        carry_ref[...] = jnp.zeros_like(carry_ref)

    mod = mod_ref[...]
    h = _rms(x, gn_ref[...]) * (1.0 + mod[:, 4 * d:5 * d]) + mod[:, 3 * d:4 * d]
    h_ref[...] = _pack_bf16_pairs(h)
    h_hi, h_lo = _split2(h)
    w_hi, w_lo = _split2(wr_ref[...])
    lg = _dot_nt(w_hi, h_hi) + _dot_nt(w_lo, h_hi) + _dot_nt(w_hi, h_lo)
    rid = lax.broadcasted_iota(I32, (8, tm), 0)
    gl = jnp.where(rid < MOE_GROUPS, lg[0:8], NEG)
    gmax = jnp.max(gl, axis=0, keepdims=True)
    gsel = jnp.min(jnp.where(gl == gmax, rid, 8), axis=0, keepdims=True)
    pmax = 1.0 / jnp.sum(jnp.where(rid < MOE_GROUPS, jnp.exp(gl - gmax), 0.0), axis=0, keepdims=True)
    e_in = jnp.zeros((MOE_PER_GROUP, tm), F32)
    for g in range(MOE_GROUPS):
        e_in = e_in + jnp.where(gsel == g, lg[8 + 8 * g:16 + 8 * g], 0.0)
    v1 = jnp.max(e_in, axis=0, keepdims=True)
    i1 = jnp.min(jnp.where(e_in == v1, rid, 8), axis=0, keepdims=True)
    e_rest = jnp.where(rid == i1, -jnp.inf, e_in)
    v2 = jnp.max(e_rest, axis=0, keepdims=True)
    i2 = jnp.min(jnp.where(e_rest == v2, rid, 8), axis=0, keepdims=True)
    t = jnp.exp(v2 - v1)
    w1 = pmax / (1.0 + t)
    w2 = pmax * t / (1.0 + t)
    e1 = gsel * MOE_PER_GROUP + i1
    e2 = gsel * MOE_PER_GROUP + i2
    eid = lax.broadcasted_iota(I32, (MOE_EXPERTS, tm), 0)
    oh1 = jnp.where(eid == e1, 1.0, 0.0)
    oh2 = jnp.where(eid == e2, 1.0, 0.0)
    ohs = oh1 + oh2
    base = carry_ref[:, 0:1] + _dot(ohs.astype(BF16), u_ref[...])
    r1 = jnp.sum(oh1 * base, axis=0, keepdims=True)
    r2 = jnp.sum(oh2 * base, axis=0, keepdims=True)
    carry_ref[...] = carry_ref[...] + jnp.sum(ohs, axis=1, keepdims=True)
    cnt_ref[...] = carry_ref[...]
    e_ref[...] = jnp.concatenate([e1, e2], axis=0)
    r_ref[...] = jnp.concatenate([r1, r2], axis=0).astype(I32)
    w8 = jnp.concatenate([w1, w2, jnp.zeros((6, tm), F32)], axis=0)
    wt_ref[...] = w8.T


def _route_out(n, d):
    tm = MOE_TILE
    shapes = [jax.ShapeDtypeStruct((n, d // 2), I32), jax.ShapeDtypeStruct((2, n), I32),
              jax.ShapeDtypeStruct((n, 8), F32), jax.ShapeDtypeStruct((2, n), I32),
              jax.ShapeDtypeStruct((MOE_EXPERTS, LANES), F32)]
    specs = [pl.BlockSpec((tm, d // 2), lambda i: (i, 0)), pl.BlockSpec((2, tm), lambda i: (0, i)),
             pl.BlockSpec((tm, 8), lambda i: (i, 0)), pl.BlockSpec((2, tm), lambda i: (0, i)),
             pl.BlockSpec((MOE_EXPERTS, LANES), lambda i: (0, 0))]
    return shapes, specs, [pltpu.VMEM((MOE_EXPERTS, LANES), F32)]


def _even_out_router_kernel(nlat, x_ref, xc_ref, a_ref, ac_ref, b_ref, bc_ref, w_ref, mod_ref, gn_ref, wr_ref, u_ref,
                            xo_ref, xoc_ref, *route_refs):
    d = D_MODEL
    lat = pl.program_id(0) < nlat
    a = jnp.where(lat, a_ref[...], ac_ref[...])
    b = jnp.where(lat, b_ref[...], bc_ref[...])
    y = _dot(a, w_ref[0:512, :]) + _dot(b, w_ref[512:1024, :])
    x = jnp.where(lat, x_ref[...], xc_ref[...]) + mod_ref[:, 2 * d:3 * d] * y

    @pl.when(lat)
    def _():
        xo_ref[...] = x

    @pl.when(jnp.logical_not(lat))
    def _():
        xoc_ref[...] = x

    _route_tile(x, mod_ref, gn_ref, wr_ref, u_ref, *route_refs)


def _even_out_router(x, xc, oa, oac, ob, obc, w, mod, ctx_row, gn, wr, u):
    b, n, d = x.shape
    tm = MOE_TILE
    flat = lambda t: t.reshape(-1, t.shape[-1])
    nlat = b * n // tm
    ntok = b * n + xc.shape[0] * xc.shape[1]
    lat = lambda wd: pl.BlockSpec((tm, wd), lambda i: (jnp.minimum(i, nlat - 1), 0))
    ctx = lambda wd: pl.BlockSpec((tm, wd), lambda i: (jnp.maximum(i - nlat, 0), 0))
    full = lambda t: pl.BlockSpec(t.shape, lambda i: (0,) * t.ndim)
    modrow = lambda i: (jnp.where(i < nlat, i // (n // tm), ctx_row), 0, 0)
    rshapes, rspecs, rscratch = _route_out(ntok, d)
    outs = pl.pallas_call(
        functools.partial(_even_out_router_kernel, nlat),
        out_shape=[jax.ShapeDtypeStruct((b * n, d), F32), jax.ShapeDtypeStruct((ntok - b * n, d), F32)] + rshapes,
        grid=(ntok // tm,),
        in_specs=[lat(d), ctx(d), lat(512), ctx(512), lat(512), ctx(512), full(w),
                  pl.BlockSpec((None, 1, 6 * d), modrow), full(gn), full(wr), full(u)],
        out_specs=[lat(d), ctx(d)] + rspecs,
        scratch_shapes=rscratch,
        compiler_params=_cparams(("arbitrary",)),
    )(flat(x), flat(xc), flat(oa), flat(oac), flat(ob), flat(obc), w, mod, gn, wr, u)
    return outs[0].reshape(x.shape), outs[1].reshape(xc.shape), outs[2:]


def _sc_permute_rows(src, dest, scatter):
    rows, d = dest.shape[0], src.shape[1]
    n = rows // 2
    info = plsc.get_sparse_core_info()
    workers = info.num_cores * info.num_subcores
    per_worker = rows // workers
    chunk = next(c for c in SC_CHUNKS if per_worker % c == 0)
    assert rows == per_worker * workers and n % per_worker == 0
    mesh = plsc.VectorSubcoreMesh(core_axis_name="c", subcore_axis_name="s")

    def body(src_hbm, dest_hbm, out_hbm, idx_v, rows_v, sem):
        base = (lax.axis_index("s") * info.num_cores + lax.axis_index("c")) * per_worker

        @pl.loop(0, per_worker // chunk)
        def _(j):
            a0 = base + j * chunk
            pltpu.sync_copy(dest_hbm.at[pl.ds(a0, chunk)], idx_v)
            if scatter:
                t0 = jnp.where(a0 >= n, a0 - n, a0)
                pltpu.sync_copy(src_hbm.at[pl.ds(t0, chunk)], rows_v)
                pltpu.async_copy(rows_v, out_hbm.at[idx_v], sem).wait()
            else:
                pltpu.async_copy(src_hbm.at[idx_v], rows_v, sem).wait()
                pltpu.sync_copy(rows_v, out_hbm.at[pl.ds(a0, chunk)])

    return pl.kernel(
        body, out_type=jax.ShapeDtypeStruct((rows, d), src.dtype), mesh=mesh,
        scratch_types=[pltpu.VMEM((chunk,), I32), pltpu.VMEM((chunk, d), src.dtype), pltpu.SemaphoreType.DMA],
    )(src, dest)


def _gmm_kernel(layer, vt_ref, ve_ref, vlo_ref, vhi_ref, vfirst_ref, vslot_ref, vnext_ref, nv_ref,
                xs_ref, wg_hbm, wu_hbm, wd_hbm, ys_ref, wgs_ref, wus_ref, wds_ref, wgb_ref, wub_ref, wdb_ref, sem):
    del vt_ref
    v = pl.program_id(0)

    def fetch(e, slot):
        return [pltpu.make_async_copy(w.at[layer, e], s.at[slot], sem.at[slot, i])
                for i, (w, s) in enumerate(((wg_hbm, wgs_ref), (wu_hbm, wus_ref), (wd_hbm, wds_ref)))]

    @pl.when(v < nv_ref[0])
    def _():
        @pl.when((v == 0) | (ve_ref[v] != ve_ref[jnp.maximum(v - 1, 0)]))
        def _():
            slot = vslot_ref[v]

            @pl.when(v == 0)
            def _():
                for c in fetch(ve_ref[0], 0):
                    c.start()

            for c in fetch(ve_ref[v], slot):
                c.wait()
            wgb_ref[...] = wgs_ref[slot].astype(BF16)
            wub_ref[...] = wus_ref[slot].astype(BF16)
            wdb_ref[...] = wds_ref[slot].astype(BF16)

            @pl.when(vnext_ref[v] >= 0)
            def _():
                for c in fetch(vnext_ref[v], 1 - slot):
                    c.start()

        x = _unpack_bf16_pairs(xs_ref[...]).astype(BF16)
        g = _dot(x, wgb_ref[...])
        u = _dot(x, wub_ref[...])
        y = _pack_bf16_pairs(_dot((g * jax.nn.sigmoid(g) * u).astype(BF16), wdb_ref[...]))
        row = lax.broadcasted_iota(I32, (y.shape[0], 1), 0)
        mine = (row >= vlo_ref[v]) & (row < vhi_ref[v])

        @pl.when(vfirst_ref[v] == 1)
        def _():
            ys_ref[...] = jnp.where(mine, y, 0)

        @pl.when(vfirst_ref[v] == 0)
        def _():
            ys_ref[...] = jnp.where(mine, y, ys_ref[...])


def _gmm(xs, visits, layer, wg, wu, wd):
    rows, dw = xs.shape
    d = 2 * dw
    tm = MOE_TILE
    hid = wg.shape[-1]
    nvis = rows // tm + MOE_EXPERTS - 1
    tile = lambda v, vt, *_: (vt[v], 0)
    anyspec = pl.BlockSpec(memory_space=pl.ANY)
    return pl.pallas_call(
        functools.partial(_gmm_kernel, layer),
        out_shape=jax.ShapeDtypeStruct((rows, dw), I32),
        grid_spec=pltpu.PrefetchScalarGridSpec(
            num_scalar_prefetch=8, grid=(nvis,),
            in_specs=[pl.BlockSpec((tm, dw), tile), anyspec, anyspec, anyspec],
            out_specs=pl.BlockSpec((tm, dw), tile),
            scratch_shapes=[pltpu.VMEM((2, d, hid), F32), pltpu.VMEM((2, d, hid), F32), pltpu.VMEM((2, hid, d), F32),
                            pltpu.VMEM((d, hid), BF16), pltpu.VMEM((d, hid), BF16), pltpu.VMEM((hid, d), BF16),
                            pltpu.SemaphoreType.DMA((2, 3))]),
        compiler_params=_cparams(("arbitrary",)),
    )(*visits, xs, wg, wu, wd)


def _combine_kernel(x_ref, wt_ref, mod_ref, fg_ref, y0_ref, y1_ref, o_ref):
    d = D_MODEL
    wt = wt_ref[...]
    y = wt[:, 0:1] * _unpack_bf16_pairs(y0_ref[...]) + wt[:, 1:2] * _unpack_bf16_pairs(y1_ref[...])
    o_ref[...] = _rms(x_ref[...] + mod_ref[:, 5 * d:6 * d] * y, fg_ref[...])


def _combine(x2, wt, mod, rows_per_mod, fg, yt, tm=COMBINE_TILE):
    n, d = x2.shape
    tm = min(tm, rows_per_mod)
    assert rows_per_mod % tm == 0
    ntiles = n // tm
    return pl.pallas_call(
        _combine_kernel,
        out_shape=jax.ShapeDtypeStruct((n, d), F32),
        grid=(ntiles,),
        in_specs=[pl.BlockSpec((tm, d), lambda i: (i, 0)), pl.BlockSpec((tm, 8), lambda i: (i, 0)),
                  pl.BlockSpec((None, 1, 6 * d), lambda i: (i // (rows_per_mod // tm), 0, 0)),
                  pl.BlockSpec(fg.shape, lambda i: (0, 0)),
                  pl.BlockSpec((tm, d // 2), lambda i: (i, 0)), pl.BlockSpec((tm, d // 2), lambda i: (ntiles + i, 0))],
        out_specs=pl.BlockSpec((tm, d), lambda i: (i, 0)),
        compiler_params=_cparams(("parallel",)),
    )(x2, wt, mod, fg, yt, yt)


def _pick(table, idx):
    hot = idx[..., None] == jnp.arange(table.shape[0], dtype=I32)
    return jnp.sum(jnp.where(hot, table, 0), axis=-1)


def _moe_plan(counts, e, r, rows):
    tm = MOE_TILE
    ends = jnp.cumsum(counts)
    starts = ends - counts
    dest = (_pick(starts, e) + r).reshape(-1)
    first_tile = starts // tm
    nvis = jnp.where(counts > 0, (ends - 1) // tm - first_tile + 1, 0)
    vend = jnp.cumsum(nvis)
    nv = vend[-1:]
    v = jnp.minimum(jnp.arange(rows // tm + MOE_EXPERTS - 1, dtype=I32), nv[0] - 1)
    ve = jnp.sum((vend[None, :] <= v[:, None]).astype(I32), axis=1)
    vt = _pick(first_tile, ve) + v - _pick(vend - nvis, ve)
    vlo = jnp.maximum(_pick(starts, ve) - vt * tm, 0)
    vhi = jnp.minimum(_pick(ends, ve) - vt * tm, tm)
    vfirst = jnp.concatenate([jnp.ones((1,), I32), (vt[1:] != vt[:-1]).astype(I32)])
    changed = jnp.concatenate([jnp.ones((1,), I32), (ve[1:] != ve[:-1]).astype(I32)])
    vslot = (jnp.cumsum(changed) - 1) % 2
    eid = jnp.arange(MOE_EXPERTS, dtype=I32)
    later = (eid[None, :] > eid[:, None]) & (counts[None, :] > 0)
    nxt = jnp.min(jnp.where(later, eid[None, :], MOE_EXPERTS), axis=1)
    vnext = _pick(jnp.where(nxt < MOE_EXPERTS, nxt, -1), ve)
    return dest, (vt, ve, vlo, vhi, vfirst, vslot.astype(I32), vnext.astype(I32), nv)


def _moe_experts(routed, layer, wg, wu, wd):
    h, e, wt, r, cnt = routed
    dest, visits = _moe_plan(cnt[:, 0].astype(I32), e, r, 2 * h.shape[0])
    xs = _sc_permute_rows(h, dest, scatter=True)
    ys = _gmm(xs, visits, layer, wg, wu, wd)
    return _sc_permute_rows(ys, dest, scatter=False), wt


def _rope_tables(rows, dim):
    row = jnp.repeat(jnp.arange(rows, dtype=F32), GRID_W)
    col = jnp.tile(jnp.arange(GRID_W, dtype=F32), rows)
    half = dim // 2
    inv = jnp.power(ROPE_BASE, -jnp.arange(0, half, 2, dtype=F32) / half)
    ar = row[:, None] * inv[None, :]
    ac = col[:, None] * inv[None, :]
    ang = jnp.concatenate([ar, ar, ac, ac], axis=-1)
    return jnp.cos(ang), jnp.sin(ang)


def _even_tables(n, with_rope):
    if with_rope:
        cm, sm = _rope_tables(n // GRID_W, MLA_ROPE)
        cwin, swin = _rope_tables(n // GRID_W, WIN_HEAD_DIM)
    else:
        cm, sm = jnp.ones((n, MLA_ROPE), F32), jnp.zeros((n, MLA_ROPE), F32)
        cwin, swin = jnp.ones((n, WIN_HEAD_DIM), F32), jnp.zeros((n, WIN_HEAD_DIM), F32)
    one, zero = jnp.ones((n, MLA_NOPE), F32), jnp.zeros((n, MLA_NOPE), F32)
    pad = jnp.zeros((n, LANES - MLA_NOPE - MLA_ROPE), F32)
    return (jnp.concatenate([one, cm, pad], axis=1), jnp.concatenate([zero, sm, pad], axis=1),
            jnp.concatenate([cwin, cwin], axis=1), jnp.concatenate([swin, swin], axis=1))


def _even_weights(w_in, qg, w_uq, kvg, w_ukv):
    d = w_in.shape[0]
    o = np.cumsum([0, MLA_Q_RANK, MLA_KV_RANK, MLA_ROPE, 512, 128, 128])
    cq, ckv, kr, qw, kw, vw = [w_in[:, o[i]:o[i + 1]] for i in range(6)]
    z = lambda c: jnp.zeros((d, c), F32)
    kr128 = jnp.concatenate([z(MLA_NOPE), kr, z(LANES - MLA_NOPE - MLA_ROPE)], axis=1)
    dup = lambda t: jnp.concatenate([t[:, 0:64], t[:, 0:64], t[:, 64:128], t[:, 64:128]], axis=1)
    win = jnp.concatenate([cq, ckv, kr128, qw, dup(kw), dup(vw)], axis=1).astype(BF16)
    uq = w_uq.reshape(MLA_Q_RANK, MLA_HEADS, MLA_NOPE + MLA_ROPE)
    uq = jnp.pad(uq, ((0, 0), (0, 0), (0, LANES - MLA_NOPE - MLA_ROPE))).reshape(MLA_Q_RANK, MLA_HEADS * LANES)
    ukv = w_ukv.reshape(MLA_KV_RANK, MLA_HEADS, MLA_NOPE + MLA_V)
    ukk = jnp.pad(ukv[:, :, :MLA_NOPE], ((0, 0), (0, 0), (0, LANES - MLA_NOPE))).reshape(MLA_KV_RANK, MLA_HEADS * LANES)
    ukvv = ukv[:, :, MLA_NOPE:].reshape(MLA_KV_RANK, MLA_HEADS * MLA_V)
    return (win, qg.reshape(1, -1), uq.astype(BF16), kvg.reshape(1, -1), ukk.astype(BF16), ukvv.T.astype(BF16))


def _odd_weights(w_in, w_g2, b_g, ln_g, ln_b, w_s, b_s):
    d = w_in.shape[0]
    o = np.cumsum([0, GLA_K, GLA_K, GLA_V, 2 * GLA_GATE_RANK, GLA_V, SG_WIDTH, SG_WIDTH])
    q, k, v, g, r, u, vg = [w_in[:, o[i]:o[i + 1]] for i in range(7)]
    g128 = jnp.concatenate([g, jnp.zeros((d, LANES - 2 * GLA_GATE_RANK), F32)], axis=1)
    win = jnp.concatenate([q, k, v, g128, r, u, vg], axis=1).astype(BF16)
    zr = jnp.zeros((GLA_GATE_RANK, GLA_K), F32)
    pad = jnp.zeros((LANES - 2 * GLA_GATE_RANK, GLA_K), F32)
    wg = jnp.concatenate([jnp.concatenate([w_g2[0], zr, pad], axis=0),
                          jnp.concatenate([zr, w_g2[1], pad], axis=0)], axis=1)
    bg = b_g.reshape(1, 2 * GLA_K)
    return (win, wg, bg, ln_g.reshape(1, -1), ln_b.reshape(1, -1), w_s.astype(BF16), b_s.T)


def kernel(x, c, ctx, c_ctx, ada_w, ada_b, norm_mix_g, norm_ffn_g, even_w_in, mla_q_norm_g, mla_w_uq, mla_kv_norm_g, mla_w_ukv, win_sink, even_w_out, odd_w_in, gla_w_g2, gla_b_g, gla_norm_g, sg_ln_g, sg_ln_b, sg_w_s, sg_b_s, odd_w_out, moe_w_rg, moe_w_re, moe_w_gate, moe_w_up, moe_w_down, final_norm_g):
    b, n, d = x.shape
    lc = ctx.shape[1]
    depth = ada_w.shape[0]
    assert depth == 2 and d == D_MODEL and b < 8
    assert n % 512 == 0 and lc % MOE_TILE == 0 and n % GRID_W == 0
    tm = 512 if n % 512 == 0 else 256
    tq = 256

    cond8 = jnp.concatenate([c, c_ctx[None, :], jnp.zeros((8 - b - 1, d), F32)], axis=0)
    mod_all = _adaln(cond8, ada_w, ada_b).reshape(depth, 8, 1, 6 * d)
    ctx_row = b
    u_tri = jnp.asarray(np.triu(np.ones((MOE_TILE, MOE_TILE), np.float32), 1), BF16)
    fg = final_norm_g.reshape(1, d)

    def router_w(layer):
        return jnp.concatenate([moe_w_rg[layer].T, jnp.zeros((8 - MOE_GROUPS, d), F32), moe_w_re[layer].T], axis=0)

    def moe_experts(routed, layer):
        return _moe_experts(routed, layer, moe_w_gate, moe_w_up, moe_w_down)

    mod = mod_all[0]
    gn = norm_mix_g[0].reshape(1, d)
    ew = _even_weights(even_w_in[0], mla_q_norm_g[0], mla_w_uq[0], mla_kv_norm_g[0], mla_w_ukv[0])
    qm_l, km_l, vm_l, qw_l, kw_l, vw_l = _even_in(x, mod, None, gn, ew, _even_tables(n, True), tm)
    qm_c, km_c, vm_c, qw_c, kw_c, vw_c = _even_in(ctx, mod, ctx_row, gn, ew, _even_tables(lc, False), lc)
    w_out = even_w_out[0].astype(BF16)
    sink = win_sink[0]
    oa_l = _mla_attn(qm_l, [(km_l, vm_l), (km_c, vm_c)], tq)
    ob_l = _gqa(qw_l, kw_l, vw_l, kw_c, vw_c, sink, True)
    oa_c = _mla_attn(qm_c, [(km_c, vm_c)], lc)
    ob_c = _gqa(qw_c, None, None, kw_c, vw_c, sink, False)
    xl, xc, routed = _even_out_router(x, ctx, oa_l, oa_c, ob_l, ob_c, w_out, mod, ctx_row,
                                      norm_ffn_g[0].reshape(1, d), router_w(0), u_tri)
    pending = moe_experts(routed, 0)

    mod = mod_all[1]
    gn = norm_mix_g[1].reshape(1, d)
    ow = _odd_weights(odd_w_in[0], gla_w_g2[0], gla_b_g[0], sg_ln_g[0], sg_ln_b[0], sg_w_s[0], sg_b_s[0])
    xl, q_l, k_l, v_l, la_l, r_l, dl_l = _odd_in(xl, pending, 0, mod_all[0], mod, None, gn, ow)
    _, q_c, k_c, v_c, la_c, _, _ = _odd_in(xc, pending, b * n // MOE_TILE, mod_all[0], mod, ctx_row, gn, ow)
    cumq_np, cumkt_np, pm_np, nlev = _gla_tables()
    cumq, cumkt = jnp.asarray(cumq_np, BF16), jnp.asarray(cumkt_np, BF16)
    pm = jnp.asarray(pm_np, F32)
    s0 = jnp.zeros((b, 2, GLA_HEADS, GLA_DV, LANES), F32)
    _, _, s_ctx = _gla(q_c, k_c, v_c, la_c, s0, cumq, cumkt, pm, nlev)
    o_fwd, o_bwd, _ = _gla(q_l, k_l, v_l, la_l, s_ctx, cumq, cumkt, pm, nlev)
    xl, routed = _odd_out_router(xl, o_fwd, o_bwd, r_l, dl_l, gla_norm_g[0].reshape(1, -1),
                                 odd_w_out[0].astype(BF16), mod, norm_ffn_g[1].reshape(1, d), router_w(1), u_tri)
    yt, wt = moe_experts(routed, 1)
    return _combine(xl.reshape(b * n, d), wt, mod, n, fg, yt).reshape(b, n, d)
```

```python
import functools

import numpy as np
import jax
import jax.numpy as jnp
from jax import lax
from jax.experimental import pallas as pl
from jax.experimental.pallas import tpu as pltpu
from jax.experimental.pallas import tpu_sc as plsc

F32 = jnp.float32
BF16 = jnp.bfloat16
I32 = jnp.int32

D_MODEL = 1024
GRID_W = 64
EPS = 1e-6
ROPE_BASE = 10000.0
MLA_HEADS = 8
MLA_Q_RANK = 256
MLA_KV_RANK = 128
MLA_NOPE = 64
MLA_ROPE = 32
MLA_V = 64
WIN_HEADS = 8
WIN_KV_HEADS = 2
WIN_HEAD_DIM = 64
WIN_BLOCK = 128
GLA_HEADS = 4
GLA_DK = 64
GLA_DV = 128
GLA_GATE_RANK = 16
GLA_TAU = 16.0
GLA_K = GLA_HEADS * GLA_DK
GLA_V = GLA_HEADS * GLA_DV
SG_GROUPS = 4
SG_CHUNK = 128
SG_WIDTH = 512
MOE_GROUPS = 4
MOE_PER_GROUP = 8
MOE_EXPERTS = 32
MOE_HIDDEN = 512

LANES = 128
GLA_BLOCK = 128
GLA_LEVELS_PER_PHASE = 2
GLA_BATCHES_PER_STEP = 2
MOE_TILE = 256
COMBINE_TILE = 1024
MLA_KEY_CHUNK = 1024
SC_CHUNKS = (128, 64, 32)
NEG = -1e30
LOG2E = 1.4426950408889634
VMEM_LIMIT = 56 * 1024 * 1024


def _cparams(sem):
    return pltpu.CompilerParams(dimension_semantics=sem, vmem_limit_bytes=VMEM_LIMIT)


def _dot(a, b):
    return jnp.dot(a, b, preferred_element_type=F32)


def _dot_nt(a, b):
    return lax.dot_general(a, b, (((1,), (1,)), ((), ())), preferred_element_type=F32)


def _split2(a):
    hi = a.astype(BF16)
    lo = (a - hi.astype(F32)).astype(BF16)
    return hi, lo


def _split3(a):
    hi = a.astype(BF16)
    r = a - hi.astype(F32)
    mid = r.astype(BF16)
    lo = (r - mid.astype(F32)).astype(BF16)
    return hi, mid, lo


def _pack_bf16_pairs(x):
    k = x.shape[1] // 2
    bits = lax.bitcast_convert_type(x.astype(BF16).astype(F32), jnp.uint32)
    return lax.bitcast_convert_type(bits[:, :k] | (bits[:, k:] >> 16), I32)


def _unpack_bf16_pairs(w):
    bits = lax.bitcast_convert_type(w, jnp.uint32)
    hi = lax.bitcast_convert_type(bits & jnp.uint32(0xFFFF0000), F32)
    lo = lax.bitcast_convert_type(bits << 16, F32)
    return jnp.concatenate([hi, lo], axis=1)


def _rms(x, g):
    ms = jnp.mean(x * x, axis=-1, keepdims=True)
    return x * lax.rsqrt(ms + EPS) * g


def _lane_tile(t, reps):
    return t if reps == 1 else jnp.concatenate([t] * reps, axis=1)


def _rope(t, cos, sin, quarter):
    n = t.shape[1]
    lane = lax.broadcasted_iota(I32, t.shape, 1)
    first = (lane & (2 * quarter - 1)) < quarter
    rot = jnp.where(first, -pltpu.roll(t, n - quarter, 1), pltpu.roll(t, quarter, 1))
    return t * cos + rot * sin


def _adaln_kernel(c_ref, w_ref, b_ref, o_ref):
    c = c_ref[...]
    s_hi, s_lo = _split2(c * jax.nn.sigmoid(c))
    w_hi, w_lo = _split2(w_ref[...])
    o_ref[...] = _dot(s_hi, w_hi) + _dot(s_lo, w_hi) + _dot(s_hi, w_lo) + b_ref[...]


def _adaln(cond8, ada_w, ada_b):
    depth, d, n6 = ada_w.shape
    tn = 1536
    return pl.pallas_call(
        _adaln_kernel,
        out_shape=jax.ShapeDtypeStruct((depth, 8, n6), F32),
        grid=(depth, n6 // tn),
        in_specs=[
            pl.BlockSpec((8, d), lambda l, j: (0, 0)),
            pl.BlockSpec((None, d, tn), lambda l, j: (l, 0, j)),
            pl.BlockSpec((None, 1, tn), lambda l, j: (l, 0, j)),
        ],
        out_specs=pl.BlockSpec((None, 8, tn), lambda l, j: (l, 0, j)),
        compiler_params=_cparams(("parallel", "parallel")),
    )(cond8, ada_w, ada_b.reshape(depth, 1, n6))


def _even_in_kernel(x_ref, mod_ref, gn_ref, win_ref, qg_ref, wuq_ref, kvg_ref, wukk_ref, wukv_ref,
                    cq_ref, sq_ref, cw_ref, sw_ref,
                    qm_ref, km_ref, vm_ref, qw_ref, kw_ref, vw_ref):
    d = D_MODEL
    mod = mod_ref[...]
    h = _rms(x_ref[...], gn_ref[...]) * (1.0 + mod[:, d:2 * d]) + mod[:, 0:d]
    z = _dot(h.astype(BF16), win_ref[...])
    cq, sq, cw, sw = cq_ref[...], sq_ref[...], cw_ref[...], sw_ref[...]
    cqn = _rms(z[:, 0:256], qg_ref[...]).astype(BF16)
    q = _dot(cqn, wuq_ref[...])
    q = _rope(q, _lane_tile(cq, 8), _lane_tile(sq, 8), MLA_ROPE // 4)
    qm_ref[...] = (q * (LOG2E * (MLA_NOPE + MLA_ROPE) ** -0.5)).astype(BF16)
    ckvn = _rms(z[:, 256:384], kvg_ref[...]).astype(BF16)
    kn = _dot(ckvn, wukk_ref[...])
    kr = _rope(z[:, 384:512], cq, sq, MLA_ROPE // 4)
    km_ref[...] = (kn + _lane_tile(kr, 8)).astype(BF16)
    vm_ref[...] = _dot_nt(wukv_ref[...], ckvn).astype(BF16)
    qw = _rope(z[:, 512:1024], _lane_tile(cw, 4), _lane_tile(sw, 4), WIN_HEAD_DIM // 4)
    qw_ref[...] = (qw * (WIN_HEAD_DIM ** -0.5)).astype(BF16)
    kw = _rope(z[:, 1024:1280], _lane_tile(cw, 2), _lane_tile(sw, 2), WIN_HEAD_DIM // 4)
    kw_ref[...] = kw.astype(BF16)
    vw_ref[...] = z[:, 1280:1536].astype(BF16)


def _even_in(x, mod, mod_row, gn, wts, tabs, tm):
    b, n, d = x.shape
    win, qg, wuq, kvg, wukk, wukv = wts
    nt = n // tm
    row = (lambda bi, i: (bi, 0, 0)) if mod_row is None else (lambda bi, i: (mod_row, 0, 0))
    full = lambda a: pl.BlockSpec(a.shape, lambda bi, i: (0,) * a.ndim)
    tab = pl.BlockSpec((tm, LANES), lambda bi, i: (i, 0))
    outw = (1024, 1024, None, 512, 256, 256)
    rowspec = lambda w: pl.BlockSpec((None, tm, w), lambda bi, i: (bi, i, 0))
    colspec = pl.BlockSpec((None, 512, tm), lambda bi, i: (bi, 0, i))
    return pl.pallas_call(
        _even_in_kernel,
        out_shape=[jax.ShapeDtypeStruct((b, 512, n) if w is None else (b, n, w), BF16) for w in outw],
        grid=(b, nt),
        in_specs=[pl.BlockSpec((None, tm, d), lambda bi, i: (bi, i, 0)),
                  pl.BlockSpec((None, 1, 6 * d), row),
                  full(gn), full(win), full(qg), full(wuq), full(kvg), full(wukk), full(wukv),
                  tab, tab, tab, tab],
        out_specs=[colspec if w is None else rowspec(w) for w in outw],
        compiler_params=_cparams(("parallel", "parallel")),
    )(x, mod, gn, win, qg, wuq, kvg, wukk, wukv, *tabs)


def _mla_attn_kernel(nseg, q_ref, *refs):
    ks, vts = refs[0:2 * nseg:2], refs[1:2 * nseg:2]
    o_ref = refs[2 * nseg]
    s_bufs = refs[2 * nseg + 1:2 * nseg + 3]
    p_bufs = refs[2 * nseg + 3:2 * nseg + 5]
    pieces, base = [], 0
    for k in ks:
        n = k.shape[0]
        pieces += [(k, c0, min(n, c0 + MLA_KEY_CHUNK), base + c0) for c0 in range(0, n, MLA_KEY_CHUNK)]
        base += n

    def score_chunk(h, piece, buf):
        k, c0, c1, g0 = piece
        hs = slice(h * LANES, (h + 1) * LANES)
        s = _dot_nt(k[c0:c1, hs], q_ref[:, hs])
        buf[g0:g0 + c1 - c0, :] = s
        return jnp.max(s, axis=0, keepdims=True)

    def prob_chunk(piece, sbuf, pbuf, m):
        _, c0, c1, g0 = piece
        p = jnp.exp2(sbuf[g0:g0 + c1 - c0, :] - m)
        pbuf[g0:g0 + c1 - c0, :] = p.astype(BF16)
        return jnp.sum(p, axis=0, keepdims=True)

    m_next = functools.reduce(jnp.maximum, [score_chunk(0, pc, s_bufs[0]) for pc in pieces])
    outs = []
    for h in range(MLA_HEADS):
        m_cur, maxes, sums = m_next, [], []
        for pc in pieces:
            if h + 1 < MLA_HEADS:
                maxes.append(score_chunk(h + 1, pc, s_bufs[(h + 1) % 2]))
            sums.append(prob_chunk(pc, s_bufs[h % 2], p_bufs[h % 2], m_cur))
        if h + 1 < MLA_HEADS:
            m_next = functools.reduce(jnp.maximum, maxes)
        l = functools.reduce(jnp.add, sums)
        vrows = slice(h * MLA_V, (h + 1) * MLA_V)
        ot, base = None, 0
        for k, vt in zip(ks, vts):
            n = k.shape[0]
            part = _dot(vt[vrows, :], p_bufs[h % 2][base:base + n, :])
            ot = part if ot is None else ot + part
            base += n
        outs.append(ot * (1.0 / l))
    o_ref[...] = jnp.concatenate(outs, axis=0).T.astype(BF16)


def _mla_attn(q, segs, tq):
    b, n, _ = q.shape
    in_specs = [pl.BlockSpec((None, tq, 1024), lambda bi, i: (bi, i, 0))]
    args = [q]
    keys = 0
    for k, vt in segs:
        lk = k.shape[1]
        keys += lk
        in_specs += [pl.BlockSpec((None, lk, 1024), lambda bi, i: (bi, 0, 0)),
                     pl.BlockSpec((None, 512, lk), lambda bi, i: (bi, 0, 0))]
        args += [k, vt]
    return pl.pallas_call(
        functools.partial(_mla_attn_kernel, len(segs)),
        out_shape=jax.ShapeDtypeStruct((b, n, 512), BF16),
        grid=(b, n // tq),
        in_specs=in_specs,
        out_specs=pl.BlockSpec((None, tq, 512), lambda bi, i: (bi, i, 0)),
        scratch_shapes=[pltpu.VMEM((keys, tq), F32), pltpu.VMEM((keys, tq), F32),
                        pltpu.VMEM((keys, tq), BF16), pltpu.VMEM((keys, tq), BF16)],
        compiler_params=_cparams(("parallel", "parallel")),
    )(*args)


def _gqa_kernel(has_win, nb, sink_ref, q_ref, *refs):
    if has_win:
        kp, kc, kn, vp, vc, vn, kx, vx, o_ref = refs
    else:
        kx, vx, o_ref = refs
    tq = q_ref.shape[0]
    i = pl.program_id(1)
    lane = lax.broadcasted_iota(I32, (tq, LANES), 1)
    row2 = lax.broadcasted_iota(I32, (2 * tq, 1), 0)
    half = WIN_HEAD_DIM
    npair = WIN_HEADS // 2
    kcats, vcats = [], []
    for g in range(WIN_KV_HEADS):
        gs = slice(g * LANES, (g + 1) * LANES)
        if has_win:
            kcats.append(jnp.concatenate([kp[:, gs], kc[:, gs], kn[:, gs], kx[:, gs]], axis=0))
            vcats.append(jnp.concatenate([vp[:, gs], vc[:, gs], vn[:, gs], vx[:, gs]], axis=0))
        else:
            kcats.append(kx[:, gs])
            vcats.append(vx[:, gs])
    if has_win:
        w = WIN_BLOCK
        shape = (2 * tq, kcats[0].shape[0])
        r = lax.broadcasted_iota(I32, shape, 0) & (tq - 1)
        c = lax.broadcasted_iota(I32, shape, 1)
        big = jnp.int32(1 << 20)
        no_prev = jnp.where(i > 0, 0, big)
        no_next = jnp.where(i < nb - 1, 0, big)
        ok_prev = c >= r + no_prev
        ok_next = (c - 2 * w) <= r - no_next
        valid = ((c >= w) | ok_prev) & ((c < 2 * w) | (c >= 3 * w) | ok_next)

    def score(j):
        qp = q_ref[:, j * LANES:(j + 1) * LANES]
        zero = jnp.zeros_like(qp)
        q2 = jnp.concatenate([jnp.where(lane < half, qp, zero), jnp.where(lane >= half, qp, zero)], axis=0)
        return _dot_nt(q2, kcats[j // 2])

    def softmax(j, s):
        s = jnp.where(valid, s, NEG) if has_win else s
        sk = jnp.where(row2 < tq, sink_ref[2 * j], sink_ref[2 * j + 1])
        m = jnp.maximum(jnp.max(s, axis=-1, keepdims=True), sk)
        p = jnp.exp(s - m)
        return p.astype(BF16), 1.0 / (jnp.sum(p, axis=-1, keepdims=True) + jnp.exp(sk - m))

    def values(j, p, inv):
        o2 = _dot(p, vcats[j // 2]) * inv
        o_ref[:, j * LANES:(j + 1) * LANES] = jnp.where(lane < half, o2[:tq], o2[tq:]).astype(BF16)

    s_val, p_val = {}, {}
    for t in range(npair + 2):
        if t < npair:
            s_val[t] = score(t)
        if 0 <= t - 1 < npair:
            p_val[t - 1] = softmax(t - 1, s_val.pop(t - 1))
        if 0 <= t - 2 < npair:
            values(t - 2, *p_val.pop(t - 2))


def _gqa(q, k, v, kx, vx, sink, has_win):
    b, n, _ = q.shape
    lc = kx.shape[1]
    smem = pl.BlockSpec(memory_space=pltpu.SMEM)
    ctxs = pl.BlockSpec((None, lc, 256), lambda bi, i: (bi, 0, 0))
    if has_win:
        tq = WIN_BLOCK
        nb = n // tq
        blk = lambda f: pl.BlockSpec((None, tq, 256), f)
        prev = lambda bi, i: (bi, jnp.maximum(i - 1, 0), 0)
        cur = lambda bi, i: (bi, i, 0)
        nxt = lambda bi, i: (bi, jnp.minimum(i + 1, nb - 1), 0)
        in_specs = [smem, pl.BlockSpec((None, tq, 512), cur),
                    blk(prev), blk(cur), blk(nxt), blk(prev), blk(cur), blk(nxt), ctxs, ctxs]
        args = (sink, q, k, k, k, v, v, v, kx, vx)
    else:
        tq, nb = n, 1
        in_specs = [smem, pl.BlockSpec((None, tq, 512), lambda bi, i: (bi, i, 0)), ctxs, ctxs]
        args = (sink, q, kx, vx)
    return pl.pallas_call(
        functools.partial(_gqa_kernel, has_win, nb),
        out_shape=jax.ShapeDtypeStruct((b, n, 512), BF16),
        grid=(b, nb),
        in_specs=in_specs,
        out_specs=pl.BlockSpec((None, tq, 512), lambda bi, i: (bi, i, 0)),
        compiler_params=_cparams(("parallel", "parallel")),
    )(*args)


def _log_sigmoid(z):
    return jnp.minimum(z, 0.0) - jnp.log(1.0 + jnp.exp(-jnp.abs(z)))


def _odd_in_kernel(x_ref, y0_ref, y1_ref, wt_ref, modp_ref, mod_ref, gn_ref, win_ref, wg_ref, bg_ref, lng_ref,
                   lnb_ref, ws_ref, bst_ref, xn_ref, q_ref, k_ref, v_ref, la_ref, r_ref, dl_ref):
    d = D_MODEL
    tm = x_ref.shape[0]
    wt = wt_ref[...]
    y = wt[:, 0:1] * _unpack_bf16_pairs(y0_ref[...]) + wt[:, 1:2] * _unpack_bf16_pairs(y1_ref[...])
    x = x_ref[...] + modp_ref[:, 5 * d:6 * d] * y
    xn_ref[...] = x
    mod = mod_ref[...]
    h = (_rms(x, gn_ref[...]) * (1.0 + mod[:, d:2 * d]) + mod[:, 0:d]).astype(BF16)
    z = _dot(h, win_ref[...])
    q_ref[...] = z[:, 0:256] * (GLA_DK ** -0.5)
    k_ref[...] = z[:, 256:512]
    v_ref[...] = z[:, 512:1024].astype(BF16)
    g_hi, g_lo = _split2(z[:, 1024:1152])
    w_hi, w_lo = _split2(wg_ref[...])
    zg = _dot(g_hi, w_hi) + _dot(g_lo, w_hi) + _dot(g_hi, w_lo) + bg_ref[...]
    la_ref[...] = _log_sigmoid(zg) / GLA_TAU
    r_ref[...] = z[:, 1152:1664]
    u = jax.nn.gelu(z[:, 1664:2176])
    vg = jax.nn.gelu(z[:, 2176:2688])
    mu = jnp.mean(vg, axis=-1, keepdims=True)
    vc = vg - mu
    var = jnp.mean(vc * vc, axis=-1, keepdims=True)
    vn = (vc * lax.rsqrt(var + EPS) * lng_ref[...] + lnb_ref[...]).astype(BF16)
    bst = bst_ref[...]
    for c in range(tm // SG_CHUNK):
        rows = slice(c * SG_CHUNK, (c + 1) * SG_CHUNK)
        parts = []
        for g in range(SG_GROUPS):
            cols = slice(g * LANES, (g + 1) * LANES)
            parts.append(_dot(ws_ref[g], vn[rows, cols]) + bst[:, g:g + 1])
        dl_ref[rows, :] = (u[rows, :] * jnp.concatenate(parts, axis=1)).astype(BF16)


def _odd_in(x, pending, tile0, modp, mod, mod_row, gn, wts):
    b, n, d = x.shape
    tm = MOE_TILE
    yt, wt = pending
    ntiles = wt.shape[0] // tm
    win, wg, bg, lng, lnb, ws, bst = wts
    row = (lambda bi, i: (bi, 0, 0)) if mod_row is None else (lambda bi, i: (mod_row, 0, 0))
    full = lambda a: pl.BlockSpec(a.shape, lambda bi, i: (0,) * a.ndim)
    act = lambda wd: pl.BlockSpec((None, tm, wd), lambda bi, i: (bi, i, 0))
    tok = lambda bi, i: tile0 + bi * (n // tm) + i
    outs = [((b, n, d), F32, act(d)),
            ((b, n, 256), F32, act(256)), ((b, n, 256), F32, act(256)), ((b, n, 512), BF16, act(512)),
            ((b, n, 512), F32, act(512)), ((b, n, 512), F32, act(512)), ((b, n, 512), BF16, act(512))]
    return pl.pallas_call(
        _odd_in_kernel,
        out_shape=[jax.ShapeDtypeStruct(s, t) for s, t, _ in outs],
        grid=(b, n // tm),
        in_specs=[act(d), pl.BlockSpec((tm, d // 2), lambda bi, i: (tok(bi, i), 0)),
                  pl.BlockSpec((tm, d // 2), lambda bi, i: (ntiles + tok(bi, i), 0)),
                  pl.BlockSpec((tm, 8), lambda bi, i: (tok(bi, i), 0)),
                  pl.BlockSpec((None, 1, 6 * d), row), pl.BlockSpec((None, 1, 6 * d), row),
                  full(gn), full(win), full(wg), full(bg), full(lng), full(lnb), full(ws), full(bst)],
        out_specs=[sp for _, _, sp in outs],
        compiler_params=_cparams(("parallel", "parallel")),
    )(x, yt, yt, wt, modp, mod, gn, win, wg, bg, lng, lnb, ws, bst)


def _gla_tables():
    c = GLA_BLOCK
    t = np.arange(c)[:, None]
    u = np.arange(c)[None, :]
    levels = [c >> i for i in range(int(np.log2(c)) + 1)]
    cum = np.zeros((2, 2 * len(levels), c, c), np.float32)
    pair = np.zeros((2, len(levels), c, c), np.float32)
    for li, m in enumerate(levels):
        same = (t // m) == (u // m)
        cum[0, 2 * li] = same & (u <= t)
        cum[0, 2 * li + 1] = same & (u > t)
        cum[1, 2 * li] = same & (u >= t)
        cum[1, 2 * li + 1] = same & (u < t)
        if li > 0:
            pair[0, li] = ((t // m) % 2 == 1) & ((u // m) == (t // m) - 1)
            pair[1, li] = ((t // m) % 2 == 0) & ((u // m) == (t // m) + 1)
    pair[:, 0] = np.eye(c, dtype=np.float32)
    nlev = len(levels)
    m1 = cum[:, 0::2].reshape(2, nlev * c, c)
    m2t = np.concatenate([cum[:, 2 * li + 1].transpose(0, 2, 1) for li in range(nlev)], axis=2)
    return np.concatenate([m1, m1], axis=2), np.concatenate([m2t, m2t], axis=1), pair, nlev


def _gla_chain(nlev, q, k, la, v_ref, cumq, cumkt, pm_ref, st_ref, o_ref):
    c = GLA_BLOCK
    lat, kt = la.T, k.T
    l_hi, l_mid = _split2(la)
    t_hi, t_mid = _split2(lat)
    exq = jnp.exp(_dot(cumq, jnp.concatenate([l_hi, l_mid], axis=0)))
    exk = jnp.exp(_dot(jnp.concatenate([t_hi, t_mid], axis=1), cumkt))
    gcol = jnp.exp(jnp.sum(lat, axis=1, keepdims=True))
    yield
    qe, ke = [], []
    for li in range(nlev):
        qe.append((q * exq[li * c:(li + 1) * c]).astype(BF16))
        ke.append((kt * exk[:, li * c:(li + 1) * c]).astype(BF16))
        if li % 2 == 1:
            yield
    qb, kb = q.astype(BF16), kt.astype(BF16)
    states = [st_ref[hd] for hd in range(GLA_HEADS)]
    yield
    outs, new_states = [], []
    lane = lax.broadcasted_iota(I32, (c, LANES), 1)
    srow = lax.broadcasted_iota(I32, (LANES, 1), 0)
    zero = jnp.zeros((c, LANES), BF16)
    for hd in range(GLA_HEADS):
        ps = slice((hd // 2) * LANES, (hd // 2 + 1) * LANES)
        vs = slice(hd * GLA_DV, (hd + 1) * GLA_DV)
        mine = (lane < GLA_DK) if hd % 2 == 0 else (lane >= GLA_DK)
        mine_row = (srow < GLA_DK) if hd % 2 == 0 else (srow >= GLA_DK)
        pick = lambda t: jnp.where(mine, t[:, ps], zero)
        a = pm_ref[0] * _dot(pick(qb), kb[ps, :])
        for li in range(1, nlev):
            a = a + pm_ref[li] * _dot(pick(qe[li]), ke[li][ps, :])
            if li % GLA_LEVELS_PER_PHASE == 0:
                yield
        v_h = v_ref[:, vs]
        outs.append(_dot(qe[0][:, ps], states[hd].astype(BF16)) + _dot(a.astype(BF16), v_h))
        new_states.append(states[hd] * gcol[ps, :] + jnp.where(mine_row, _dot(ke[0][ps, :], v_h), 0.0))
        yield
    o_ref[...] = jnp.concatenate(outs, axis=1)
    for hd in range(GLA_HEADS):
        st_ref[hd] = new_states[hd]
    yield


def _gla_kernel(nlev, nb, *refs):
    ins_f, ins_b = refs[0:4], refs[4:8]
    cumq_ref, cumkt_ref, pm_ref, s0_ref, of_ref, ob_ref, sf_ref = refs[8:15]
    st_refs = refs[15:]
    step = pl.program_id(1)

    @pl.when(step == 0)
    def _():
        for bb in range(nb):
            for d_ in range(2):
                st_refs[2 * bb + d_][...] = s0_ref[bb, d_]

    chains = []
    for bb in range(nb):
        for d_, (ins, o_ref) in enumerate(((ins_f, of_ref), (ins_b, ob_ref))):
            q_ref, k_ref, v_ref, la_ref = ins
            chains.append(_gla_chain(nlev, q_ref[bb], k_ref[bb], la_ref[bb], v_ref.at[bb], cumq_ref[d_],
                                     cumkt_ref[d_], pm_ref.at[d_], st_refs[2 * bb + d_], o_ref.at[bb]))
    while chains:
        chains = [ch for ch in chains if next(ch, "done") != "done"]
    for bb in range(nb):
        for d_ in range(2):
            sf_ref[bb, d_] = st_refs[2 * bb + d_][...]


def _gla(q, k, v, la, s0, cumq, cumkt, pm, nlev):
    b, n, _ = q.shape
    c = GLA_BLOCK
    nc = n // c
    nb = next(c for c in (GLA_BATCHES_PER_STEP, 2, 1) if b % c == 0)
    specs = []
    for d_ in range(2):
        pos = (lambda s_: s_) if d_ == 0 else (lambda s_: nc - 1 - s_)
        specs += [pl.BlockSpec((nb, c, 256), lambda bi, s_, pos=pos: (bi, pos(s_), 0)),
                  pl.BlockSpec((nb, c, 256), lambda bi, s_, pos=pos: (bi, pos(s_), 0)),
                  pl.BlockSpec((nb, c, 512), lambda bi, s_, pos=pos: (bi, pos(s_), 0)),
                  pl.BlockSpec((nb, c, 256), lambda bi, s_, pos=pos, d_=d_: (bi, pos(s_), d_))]
    st_spec = pl.BlockSpec((nb, 2, GLA_HEADS, GLA_DV, LANES), lambda bi, s_: (bi, 0, 0, 0, 0))
    full = lambda a: pl.BlockSpec(a.shape, lambda bi, s_: (0,) * a.ndim)
    return pl.pallas_call(
        functools.partial(_gla_kernel, nlev, nb),
        out_shape=[jax.ShapeDtypeStruct((b, n, GLA_V), F32), jax.ShapeDtypeStruct((b, n, GLA_V), F32),
                   jax.ShapeDtypeStruct((b, 2, GLA_HEADS, GLA_DV, LANES), F32)],
        grid=(b // nb, nc),
        in_specs=specs + [full(cumq), full(cumkt), full(pm), st_spec],
        out_specs=[pl.BlockSpec((nb, c, GLA_V), lambda bi, s_: (bi, s_, 0)),
                   pl.BlockSpec((nb, c, GLA_V), lambda bi, s_: (bi, nc - 1 - s_, 0)), st_spec],
        scratch_shapes=[pltpu.VMEM((GLA_HEADS, GLA_DV, LANES), F32) for _ in range(2 * nb)],
        compiler_params=_cparams(("parallel", "arbitrary")),
    )(q, k, v, la, q, k, v, la, cumq, cumkt, pm, s0)


def _odd_out_router_kernel(x_ref, of_ref, ob_ref, r_ref, dl_ref, gg_ref, w_ref, mod_ref, gn_ref, wr_ref, u_ref,
                           o_ref, *route_refs):
    d = D_MODEL
    o = of_ref[...] + ob_ref[...]
    gg = gg_ref[...]
    r = r_ref[...]
    parts = []
    for hd in range(GLA_HEADS):
        vs = slice(hd * GLA_DV, (hd + 1) * GLA_DV)
        oh = o[:, vs]
        parts.append(oh * lax.rsqrt(jnp.mean(oh * oh, axis=-1, keepdims=True) + EPS) * gg[:, vs])
    cl = (jnp.concatenate(parts, axis=1) * (r * jax.nn.sigmoid(r))).astype(BF16)
    y = _dot(cl, w_ref[0:512, :]) + _dot(dl_ref[...], w_ref[512:1024, :])
    x = x_ref[...] + mod_ref[:, 2 * d:3 * d] * y
    o_ref[...] = x
    _route_tile(x, mod_ref, gn_ref, wr_ref, u_ref, *route_refs)


def _odd_out_router(x, o_fwd, o_bwd, r, dl, gg, w, mod, gn, wr, u):
    b, n, d = x.shape
    tm = MOE_TILE
    flat = lambda t: t.reshape(-1, t.shape[-1])
    act = lambda wd: pl.BlockSpec((tm, wd), lambda i: (i, 0))
    full = lambda t: pl.BlockSpec(t.shape, lambda i: (0,) * t.ndim)
    rshapes, rspecs, rscratch = _route_out(b * n, d)
    outs = pl.pallas_call(
        _odd_out_router_kernel,
        out_shape=[jax.ShapeDtypeStruct((b * n, d), F32)] + rshapes,
        grid=(b * n // tm,),
        in_specs=[act(d), act(GLA_V), act(GLA_V), act(512), act(512), full(gg), full(w),
                  pl.BlockSpec((None, 1, 6 * d), lambda i: (i // (n // tm), 0, 0)), full(gn), full(wr), full(u)],
        out_specs=[act(d)] + rspecs,
        scratch_shapes=rscratch,
        compiler_params=_cparams(("arbitrary",)),
    )(flat(x), flat(o_fwd), flat(o_bwd), flat(r), flat(dl), gg, w, mod, gn, wr, u)
    return outs[0].reshape(x.shape), outs[1:]


def _route_tile(x, mod_ref, gn_ref, wr_ref, u_ref, h_ref, e_ref, wt_ref, r_ref, cnt_ref, carry_ref):
    d = D_MODEL
    tm = x.shape[0]

    @pl.when(pl.program_id(0) == 0)
    def _():
        carry_ref[...] = jnp.zeros_like(carry_ref)

    mod = mod_ref[...]
    h = _rms(x, gn_ref[...]) * (1.0 + mod[:, 4 * d:5 * d]) + mod[:, 3 * d:4 * d]
    h_ref[...] = _pack_bf16_pairs(h)
    h_hi, h_lo = _split2(h)
    w_hi, w_lo = _split2(wr_ref[...])
    lg = _dot_nt(w_hi, h_hi) + _dot_nt(w_lo, h_hi) + _dot_nt(w_hi, h_lo)
    rid = lax.broadcasted_iota(I32, (8, tm), 0)
    gl = jnp.where(rid < MOE_GROUPS, lg[0:8], NEG)
    gmax = jnp.max(gl, axis=0, keepdims=True)
    gsel = jnp.min(jnp.where(gl == gmax, rid, 8), axis=0, keepdims=True)
    pmax = 1.0 / jnp.sum(jnp.where(rid < MOE_GROUPS, jnp.exp(gl - gmax), 0.0), axis=0, keepdims=True)
    e_in = jnp.zeros((MOE_PER_GROUP, tm), F32)
    for g in range(MOE_GROUPS):
        e_in = e_in + jnp.where(gsel == g, lg[8 + 8 * g:16 + 8 * g], 0.0)
    v1 = jnp.max(e_in, axis=0, keepdims=True)
    i1 = jnp.min(jnp.where(e_in == v1, rid, 8), axis=0, keepdims=True)
    e_rest = jnp.where(rid == i1, -jnp.inf, e_in)
    v2 = jnp.max(e_rest, axis=0, keepdims=True)
    i2 = jnp.min(jnp.where(e_rest == v2, rid, 8), axis=0, keepdims=True)
    t = jnp.exp(v2 - v1)
    w1 = pmax / (1.0 + t)
    w2 = pmax * t / (1.0 + t)
    e1 = gsel * MOE_PER_GROUP + i1
    e2 = gsel * MOE_PER_GROUP + i2
    eid = lax.broadcasted_iota(I32, (MOE_EXPERTS, tm), 0)
    oh1 = jnp.where(eid == e1, 1.0, 0.0)
    oh2 = jnp.where(eid == e2, 1.0, 0.0)
    ohs = oh1 + oh2
    base = carry_ref[:, 0:1] + _dot(ohs.astype(BF16), u_ref[...])
    r1 = jnp.sum(oh1 * base, axis=0, keepdims=True)
    r2 = jnp.sum(oh2 * base, axis=0, keepdims=True)
    carry_ref[...] = carry_ref[...] + jnp.sum(ohs, axis=1, keepdims=True)
    cnt_ref[...] = carry_ref[...]
    e_ref[...] = jnp.concatenate([e1, e2], axis=0)
    r_ref[...] = jnp.concatenate([r1, r2], axis=0).astype(I32)
    w8 = jnp.concatenate([w1, w2, jnp.zeros((6, tm), F32)], axis=0)
    wt_ref[...] = w8.T


def _route_out(n, d):
    tm = MOE_TILE
    shapes = [jax.ShapeDtypeStruct((n, d // 2), I32), jax.ShapeDtypeStruct((2, n), I32),
              jax.ShapeDtypeStruct((n, 8), F32), jax.ShapeDtypeStruct((2, n), I32),
              jax.ShapeDtypeStruct((MOE_EXPERTS, LANES), F32)]
    specs = [pl.BlockSpec((tm, d // 2), lambda i: (i, 0)), pl.BlockSpec((2, tm), lambda i: (0, i)),
             pl.BlockSpec((tm, 8), lambda i: (i, 0)), pl.BlockSpec((2, tm), lambda i: (0, i)),
             pl.BlockSpec((MOE_EXPERTS, LANES), lambda i: (0, 0))]
    return shapes, specs, [pltpu.VMEM((MOE_EXPERTS, LANES), F32)]


def _even_out_router_kernel(nlat, x_ref, xc_ref, a_ref, ac_ref, b_ref, bc_ref, w_ref, mod_ref, gn_ref, wr_ref, u_ref,
                            xo_ref, xoc_ref, *route_refs):
    d = D_MODEL
    lat = pl.program_id(0) < nlat
    a = jnp.where(lat, a_ref[...], ac_ref[...])
    b = jnp.where(lat, b_ref[...], bc_ref[...])
    y = _dot(a, w_ref[0:512, :]) + _dot(b, w_ref[512:1024, :])
    x = jnp.where(lat, x_ref[...], xc_ref[...]) + mod_ref[:, 2 * d:3 * d] * y

    @pl.when(lat)
    def _():
        xo_ref[...] = x

    @pl.when(jnp.logical_not(lat))
    def _():
        xoc_ref[...] = x

    _route_tile(x, mod_ref, gn_ref, wr_ref, u_ref, *route_refs)


def _even_out_router(x, xc, oa, oac, ob, obc, w, mod, ctx_row, gn, wr, u):
    b, n, d = x.shape
    tm = MOE_TILE
    flat = lambda t: t.reshape(-1, t.shape[-1])
    nlat = b * n // tm
    ntok = b * n + xc.shape[0] * xc.shape[1]
    lat = lambda wd: pl.BlockSpec((tm, wd), lambda i: (jnp.minimum(i, nlat - 1), 0))
    ctx = lambda wd: pl.BlockSpec((tm, wd), lambda i: (jnp.maximum(i - nlat, 0), 0))
    full = lambda t: pl.BlockSpec(t.shape, lambda i: (0,) * t.ndim)
    modrow = lambda i: (jnp.where(i < nlat, i // (n // tm), ctx_row), 0, 0)
    rshapes, rspecs, rscratch = _route_out(ntok, d)
    outs = pl.pallas_call(
        functools.partial(_even_out_router_kernel, nlat),
        out_shape=[jax.ShapeDtypeStruct((b * n, d), F32), jax.ShapeDtypeStruct((ntok - b * n, d), F32)] + rshapes,
        grid=(ntok // tm,),
        in_specs=[lat(d), ctx(d), lat(512), ctx(512), lat(512), ctx(512), full(w),
                  pl.BlockSpec((None, 1, 6 * d), modrow), full(gn), full(wr), full(u)],
        out_specs=[lat(d), ctx(d)] + rspecs,
        scratch_shapes=rscratch,
        compiler_params=_cparams(("arbitrary",)),
    )(flat(x), flat(xc), flat(oa), flat(oac), flat(ob), flat(obc), w, mod, gn, wr, u)
    return outs[0].reshape(x.shape), outs[1].reshape(xc.shape), outs[2:]


def _sc_permute_rows(src, dest, scatter):
    rows, d = dest.shape[0], src.shape[1]
    n = rows // 2
    info = plsc.get_sparse_core_info()
    workers = info.num_cores * info.num_subcores
    per_worker = rows // workers
    chunk = next(c for c in SC_CHUNKS if per_worker % c == 0)
    assert rows == per_worker * workers and n % per_worker == 0
    mesh = plsc.VectorSubcoreMesh(core_axis_name="c", subcore_axis_name="s")

    def body(src_hbm, dest_hbm, out_hbm, idx_v, rows_v, sem):
        base = (lax.axis_index("s") * info.num_cores + lax.axis_index("c")) * per_worker

        @pl.loop(0, per_worker // chunk)
        def _(j):
            a0 = base + j * chunk
            pltpu.sync_copy(dest_hbm.at[pl.ds(a0, chunk)], idx_v)
            if scatter:
                t0 = jnp.where(a0 >= n, a0 - n, a0)
                pltpu.sync_copy(src_hbm.at[pl.ds(t0, chunk)], rows_v)
                pltpu.async_copy(rows_v, out_hbm.at[idx_v], sem).wait()
            else:
                pltpu.async_copy(src_hbm.at[idx_v], rows_v, sem).wait()
                pltpu.sync_copy(rows_v, out_hbm.at[pl.ds(a0, chunk)])

    return pl.kernel(
        body, out_type=jax.ShapeDtypeStruct((rows, d), src.dtype), mesh=mesh,
        scratch_types=[pltpu.VMEM((chunk,), I32), pltpu.VMEM((chunk, d), src.dtype), pltpu.SemaphoreType.DMA],
    )(src, dest)


def _gmm_kernel(layer, vt_ref, ve_ref, vlo_ref, vhi_ref, vfirst_ref, vslot_ref, vnext_ref, nv_ref,
                xs_ref, wg_hbm, wu_hbm, wd_hbm, ys_ref, wgs_ref, wus_ref, wds_ref, wgb_ref, wub_ref, wdb_ref, sem):
    del vt_ref
    v = pl.program_id(0)

    def fetch(e, slot):
        return [pltpu.make_async_copy(w.at[layer, e], s.at[slot], sem.at[slot, i])
                for i, (w, s) in enumerate(((wg_hbm, wgs_ref), (wu_hbm, wus_ref), (wd_hbm, wds_ref)))]

    @pl.when(v < nv_ref[0])
    def _():
        @pl.when((v == 0) | (ve_ref[v] != ve_ref[jnp.maximum(v - 1, 0)]))
        def _():
            slot = vslot_ref[v]

            @pl.when(v == 0)
            def _():
                for c in fetch(ve_ref[0], 0):
                    c.start()

            for c in fetch(ve_ref[v], slot):
                c.wait()
            wgb_ref[...] = wgs_ref[slot].astype(BF16)
            wub_ref[...] = wus_ref[slot].astype(BF16)
            wdb_ref[...] = wds_ref[slot].astype(BF16)

            @pl.when(vnext_ref[v] >= 0)
            def _():
                for c in fetch(vnext_ref[v], 1 - slot):
                    c.start()

        x = _unpack_bf16_pairs(xs_ref[...]).astype(BF16)
        g = _dot(x, wgb_ref[...])
        u = _dot(x, wub_ref[...])
        y = _pack_bf16_pairs(_dot((g * jax.nn.sigmoid(g) * u).astype(BF16), wdb_ref[...]))
        row = lax.broadcasted_iota(I32, (y.shape[0], 1), 0)
        mine = (row >= vlo_ref[v]) & (row < vhi_ref[v])

        @pl.when(vfirst_ref[v] == 1)
        def _():
            ys_ref[...] = jnp.where(mine, y, 0)

        @pl.when(vfirst_ref[v] == 0)
        def _():
            ys_ref[...] = jnp.where(mine, y, ys_ref[...])


def _gmm(xs, visits, layer, wg, wu, wd):
    rows, dw = xs.shape
    d = 2 * dw
    tm = MOE_TILE
    hid = wg.shape[-1]
    nvis = rows // tm + MOE_EXPERTS - 1
    tile = lambda v, vt, *_: (vt[v], 0)
    anyspec = pl.BlockSpec(memory_space=pl.ANY)
    return pl.pallas_call(
        functools.partial(_gmm_kernel, layer),
        out_shape=jax.ShapeDtypeStruct((rows, dw), I32),
        grid_spec=pltpu.PrefetchScalarGridSpec(
            num_scalar_prefetch=8, grid=(nvis,),
            in_specs=[pl.BlockSpec((tm, dw), tile), anyspec, anyspec, anyspec],
            out_specs=pl.BlockSpec((tm, dw), tile),
            scratch_shapes=[pltpu.VMEM((2, d, hid), F32), pltpu.VMEM((2, d, hid), F32), pltpu.VMEM((2, hid, d), F32),
                            pltpu.VMEM((d, hid), BF16), pltpu.VMEM((d, hid), BF16), pltpu.VMEM((hid, d), BF16),
                            pltpu.SemaphoreType.DMA((2, 3))]),
        compiler_params=_cparams(("arbitrary",)),
    )(*visits, xs, wg, wu, wd)


def _combine_kernel(x_ref, wt_ref, mod_ref, fg_ref, y0_ref, y1_ref, o_ref):
    d = D_MODEL
    wt = wt_ref[...]
    y = wt[:, 0:1] * _unpack_bf16_pairs(y0_ref[...]) + wt[:, 1:2] * _unpack_bf16_pairs(y1_ref[...])
    o_ref[...] = _rms(x_ref[...] + mod_ref[:, 5 * d:6 * d] * y, fg_ref[...])


def _combine(x2, wt, mod, rows_per_mod, fg, yt, tm=COMBINE_TILE):
    n, d = x2.shape
    tm = min(tm, rows_per_mod)
    assert rows_per_mod % tm == 0
    ntiles = n // tm
    return pl.pallas_call(
        _combine_kernel,
        out_shape=jax.ShapeDtypeStruct((n, d), F32),
        grid=(ntiles,),
        in_specs=[pl.BlockSpec((tm, d), lambda i: (i, 0)), pl.BlockSpec((tm, 8), lambda i: (i, 0)),
                  pl.BlockSpec((None, 1, 6 * d), lambda i: (i // (rows_per_mod // tm), 0, 0)),
                  pl.BlockSpec(fg.shape, lambda i: (0, 0)),
                  pl.BlockSpec((tm, d // 2), lambda i: (i, 0)), pl.BlockSpec((tm, d // 2), lambda i: (ntiles + i, 0))],
        out_specs=pl.BlockSpec((tm, d), lambda i: (i, 0)),
        compiler_params=_cparams(("parallel",)),
    )(x2, wt, mod, fg, yt, yt)


def _pick(table, idx):
    hot = idx[..., None] == jnp.arange(table.shape[0], dtype=I32)
    return jnp.sum(jnp.where(hot, table, 0), axis=-1)


def _moe_plan(counts, e, r, rows):
    tm = MOE_TILE
    ends = jnp.cumsum(counts)
    starts = ends - counts
    dest = (_pick(starts, e) + r).reshape(-1)
    first_tile = starts // tm
    nvis = jnp.where(counts > 0, (ends - 1) // tm - first_tile + 1, 0)
    vend = jnp.cumsum(nvis)
    nv = vend[-1:]
    v = jnp.minimum(jnp.arange(rows // tm + MOE_EXPERTS - 1, dtype=I32), nv[0] - 1)
    ve = jnp.sum((vend[None, :] <= v[:, None]).astype(I32), axis=1)
    vt = _pick(first_tile, ve) + v - _pick(vend - nvis, ve)
    vlo = jnp.maximum(_pick(starts, ve) - vt * tm, 0)
    vhi = jnp.minimum(_pick(ends, ve) - vt * tm, tm)
    vfirst = jnp.concatenate([jnp.ones((1,), I32), (vt[1:] != vt[:-1]).astype(I32)])
    changed = jnp.concatenate([jnp.ones((1,), I32), (ve[1:] != ve[:-1]).astype(I32)])
    vslot = (jnp.cumsum(changed) - 1) % 2
    eid = jnp.arange(MOE_EXPERTS, dtype=I32)
    later = (eid[None, :] > eid[:, None]) & (counts[None, :] > 0)
    nxt = jnp.min(jnp.where(later, eid[None, :], MOE_EXPERTS), axis=1)
    vnext = _pick(jnp.where(nxt < MOE_EXPERTS, nxt, -1), ve)
    return dest, (vt, ve, vlo, vhi, vfirst, vslot.astype(I32), vnext.astype(I32), nv)


def _moe_experts(routed, layer, wg, wu, wd):
    h, e, wt, r, cnt = routed
    dest, visits = _moe_plan(cnt[:, 0].astype(I32), e, r, 2 * h.shape[0])
    xs = _sc_permute_rows(h, dest, scatter=True)
    ys = _gmm(xs, visits, layer, wg, wu, wd)
    return _sc_permute_rows(ys, dest, scatter=False), wt


def _rope_tables(rows, dim):
    row = jnp.repeat(jnp.arange(rows, dtype=F32), GRID_W)
    col = jnp.tile(jnp.arange(GRID_W, dtype=F32), rows)
    half = dim // 2
    inv = jnp.power(ROPE_BASE, -jnp.arange(0, half, 2, dtype=F32) / half)
    ar = row[:, None] * inv[None, :]
    ac = col[:, None] * inv[None, :]
    ang = jnp.concatenate([ar, ar, ac, ac], axis=-1)
    return jnp.cos(ang), jnp.sin(ang)


def _even_tables(n, with_rope):
    if with_rope:
        cm, sm = _rope_tables(n // GRID_W, MLA_ROPE)
        cwin, swin = _rope_tables(n // GRID_W, WIN_HEAD_DIM)
    else:
        cm, sm = jnp.ones((n, MLA_ROPE), F32), jnp.zeros((n, MLA_ROPE), F32)
        cwin, swin = jnp.ones((n, WIN_HEAD_DIM), F32), jnp.zeros((n, WIN_HEAD_DIM), F32)
    one, zero = jnp.ones((n, MLA_NOPE), F32), jnp.zeros((n, MLA_NOPE), F32)
    pad = jnp.zeros((n, LANES - MLA_NOPE - MLA_ROPE), F32)
    return (jnp.concatenate([one, cm, pad], axis=1), jnp.concatenate([zero, sm, pad], axis=1),
            jnp.concatenate([cwin, cwin], axis=1), jnp.concatenate([swin, swin], axis=1))


def _even_weights(w_in, qg, w_uq, kvg, w_ukv):
    d = w_in.shape[0]
    o = np.cumsum([0, MLA_Q_RANK, MLA_KV_RANK, MLA_ROPE, 512, 128, 128])
    cq, ckv, kr, qw, kw, vw = [w_in[:, o[i]:o[i + 1]] for i in range(6)]
    z = lambda c: jnp.zeros((d, c), F32)
    kr128 = jnp.concatenate([z(MLA_NOPE), kr, z(LANES - MLA_NOPE - MLA_ROPE)], axis=1)
    dup = lambda t: jnp.concatenate([t[:, 0:64], t[:, 0:64], t[:, 64:128], t[:, 64:128]], axis=1)
    win = jnp.concatenate([cq, ckv, kr128, qw, dup(kw), dup(vw)], axis=1).astype(BF16)
    uq = w_uq.reshape(MLA_Q_RANK, MLA_HEADS, MLA_NOPE + MLA_ROPE)
    uq = jnp.pad(uq, ((0, 0), (0, 0), (0, LANES - MLA_NOPE - MLA_ROPE))).reshape(MLA_Q_RANK, MLA_HEADS * LANES)
    ukv = w_ukv.reshape(MLA_KV_RANK, MLA_HEADS, MLA_NOPE + MLA_V)
    ukk = jnp.pad(ukv[:, :, :MLA_NOPE], ((0, 0), (0, 0), (0, LANES - MLA_NOPE))).reshape(MLA_KV_RANK, MLA_HEADS * LANES)
    ukvv = ukv[:, :, MLA_NOPE:].reshape(MLA_KV_RANK, MLA_HEADS * MLA_V)
    return (win, qg.reshape(1, -1), uq.astype(BF16), kvg.reshape(1, -1), ukk.astype(BF16), ukvv.T.astype(BF16))


def _odd_weights(w_in, w_g2, b_g, ln_g, ln_b, w_s, b_s):
    d = w_in.shape[0]
    o = np.cumsum([0, GLA_K, GLA_K, GLA_V, 2 * GLA_GATE_RANK, GLA_V, SG_WIDTH, SG_WIDTH])
    q, k, v, g, r, u, vg = [w_in[:, o[i]:o[i + 1]] for i in range(7)]
    g128 = jnp.concatenate([g, jnp.zeros((d, LANES - 2 * GLA_GATE_RANK), F32)], axis=1)
    win = jnp.concatenate([q, k, v, g128, r, u, vg], axis=1).astype(BF16)
    zr = jnp.zeros((GLA_GATE_RANK, GLA_K), F32)
    pad = jnp.zeros((LANES - 2 * GLA_GATE_RANK, GLA_K), F32)
    wg = jnp.concatenate([jnp.concatenate([w_g2[0], zr, pad], axis=0),
                          jnp.concatenate([zr, w_g2[1], pad], axis=0)], axis=1)
    bg = b_g.reshape(1, 2 * GLA_K)
    return (win, wg, bg, ln_g.reshape(1, -1), ln_b.reshape(1, -1), w_s.astype(BF16), b_s.T)


def kernel(x, c, ctx, c_ctx, ada_w, ada_b, norm_mix_g, norm_ffn_g, even_w_in, mla_q_norm_g, mla_w_uq, mla_kv_norm_g, mla_w_ukv, win_sink, even_w_out, odd_w_in, gla_w_g2, gla_b_g, gla_norm_g, sg_ln_g, sg_ln_b, sg_w_s, sg_b_s, odd_w_out, moe_w_rg, moe_w_re, moe_w_gate, moe_w_up, moe_w_down, final_norm_g):
    b, n, d = x.shape
    lc = ctx.shape[1]
    depth = ada_w.shape[0]
    assert depth == 2 and d == D_MODEL and b < 8
    assert n % 512 == 0 and lc % MOE_TILE == 0 and n % GRID_W == 0
    tm = 512 if n % 512 == 0 else 256
    tq = 256

    cond8 = jnp.concatenate([c, c_ctx[None, :], jnp.zeros((8 - b - 1, d), F32)], axis=0)
    mod_all = _adaln(cond8, ada_w, ada_b).reshape(depth, 8, 1, 6 * d)
    ctx_row = b
    u_tri = jnp.asarray(np.triu(np.ones((MOE_TILE, MOE_TILE), np.float32), 1), BF16)
    fg = final_norm_g.reshape(1, d)

    def router_w(layer):
        return jnp.concatenate([moe_w_rg[layer].T, jnp.zeros((8 - MOE_GROUPS, d), F32), moe_w_re[layer].T], axis=0)

    def moe_experts(routed, layer):
        return _moe_experts(routed, layer, moe_w_gate, moe_w_up, moe_w_down)

    mod = mod_all[0]
    gn = norm_mix_g[0].reshape(1, d)
    ew = _even_weights(even_w_in[0], mla_q_norm_g[0], mla_w_uq[0], mla_kv_norm_g[0], mla_w_ukv[0])
    qm_l, km_l, vm_l, qw_l, kw_l, vw_l = _even_in(x, mod, None, gn, ew, _even_tables(n, True), tm)
    qm_c, km_c, vm_c, qw_c, kw_c, vw_c = _even_in(ctx, mod, ctx_row, gn, ew, _even_tables(lc, False), lc)
    w_out = even_w_out[0].astype(BF16)
    sink = win_sink[0]
    oa_l = _mla_attn(qm_l, [(km_l, vm_l), (km_c, vm_c)], tq)
    ob_l = _gqa(qw_l, kw_l, vw_l, kw_c, vw_c, sink, True)
    oa_c = _mla_attn(qm_c, [(km_c, vm_c)], lc)
    ob_c = _gqa(qw_c, None, None, kw_c, vw_c, sink, False)
    xl, xc, routed = _even_out_router(x, ctx, oa_l, oa_c, ob_l, ob_c, w_out, mod, ctx_row,
                                      norm_ffn_g[0].reshape(1, d), router_w(0), u_tri)
    pending = moe_experts(routed, 0)

    mod = mod_all[1]
    gn = norm_mix_g[1].reshape(1, d)
    ow = _odd_weights(odd_w_in[0], gla_w_g2[0], gla_b_g[0], sg_ln_g[0], sg_ln_b[0], sg_w_s[0], sg_b_s[0])
    xl, q_l, k_l, v_l, la_l, r_l, dl_l = _odd_in(xl, pending, 0, mod_all[0], mod, None, gn, ow)
    _, q_c, k_c, v_c, la_c, _, _ = _odd_in(xc, pending, b * n // MOE_TILE, mod_all[0], mod, ctx_row, gn, ow)
    cumq_np, cumkt_np, pm_np, nlev = _gla_tables()
    cumq, cumkt = jnp.asarray(cumq_np, BF16), jnp.asarray(cumkt_np, BF16)
    pm = jnp.asarray(pm_np, F32)
    s0 = jnp.zeros((b, 2, GLA_HEADS, GLA_DV, LANES), F32)
    _, _, s_ctx = _gla(q_c, k_c, v_c, la_c, s0, cumq, cumkt, pm, nlev)
    o_fwd, o_bwd, _ = _gla(q_l, k_l, v_l, la_l, s_ctx, cumq, cumkt, pm, nlev)
    xl, routed = _odd_out_router(xl, o_fwd, o_bwd, r_l, dl_l, gla_norm_g[0].reshape(1, -1),
                                 odd_w_out[0].astype(BF16), mod, norm_ffn_g[1].reshape(1, d), router_w(1), u_tri)
    yt, wt = moe_experts(routed, 1)
    return _combine(xl.reshape(b * n, d), wt, mod, n, fg, yt).reshape(b, n, d)
```

```python
import functools

import numpy as np
import jax
import jax.numpy as jnp
from jax import lax
from jax.experimental import pallas as pl
from jax.experimental.pallas import tpu as pltpu
from jax.experimental.pallas import tpu_sc as plsc

F32 = jnp.float32
BF16 = jnp.bfloat16
I32 = jnp.int32

D_MODEL = 1024
GRID_W = 64
EPS = 1e-6
ROPE_BASE = 10000.0
MLA_HEADS = 8
MLA_Q_RANK = 256
MLA_KV_RANK = 128
MLA_NOPE = 64
MLA_ROPE = 32
MLA_V = 64
WIN_HEADS = 8
WIN_KV_HEADS = 2
WIN_HEAD_DIM = 64
WIN_BLOCK = 128
GLA_HEADS = 4
GLA_DK = 64
GLA_DV = 128
GLA_GATE_RANK = 16
GLA_TAU = 16.0
GLA_K = GLA_HEADS * GLA_DK
GLA_V = GLA_HEADS * GLA_DV
SG_GROUPS = 4
SG_CHUNK = 128
SG_WIDTH = 512
MOE_GROUPS = 4
MOE_PER_GROUP = 8
MOE_EXPERTS = 32
MOE_HIDDEN = 512

LANES = 128
GLA_BLOCK = 128
GLA_LEVELS_PER_PHASE = 2
GLA_BATCHES_PER_STEP = 4
MOE_TILE = 256
COMBINE_TILE = 1024
MLA_KEY_CHUNK = 1024
SC_CHUNKS = (128, 64, 32)
NEG = -1e30
LOG2E = 1.4426950408889634
VMEM_LIMIT = 56 * 1024 * 1024


def _cparams(sem):
    return pltpu.CompilerParams(dimension_semantics=sem, vmem_limit_bytes=VMEM_LIMIT)


def _dot(a, b):
    return jnp.dot(a, b, preferred_element_type=F32)


def _dot_nt(a, b):
    return lax.dot_general(a, b, (((1,), (1,)), ((), ())), preferred_element_type=F32)


def _split2(a):
    hi = a.astype(BF16)
    lo = (a - hi.astype(F32)).astype(BF16)
    return hi, lo


def _pack_bf16_pairs(x):
    k = x.shape[1] // 2
    bits = lax.bitcast_convert_type(x.astype(BF16).astype(F32), jnp.uint32)
    return lax.bitcast_convert_type(bits[:, :k] | (bits[:, k:] >> 16), I32)


def _unpack_bf16_pairs(w):
    bits = lax.bitcast_convert_type(w, jnp.uint32)
    hi = lax.bitcast_convert_type(bits & jnp.uint32(0xFFFF0000), F32)
    lo = lax.bitcast_convert_type(bits << 16, F32)
    return jnp.concatenate([hi, lo], axis=1)


def _rms(x, g):
    ms = jnp.mean(x * x, axis=-1, keepdims=True)
    return x * lax.rsqrt(ms + EPS) * g


def _lane_tile(t, reps):
    return t if reps == 1 else jnp.concatenate([t] * reps, axis=1)


def _rope(t, cos, sin, quarter):
    n = t.shape[1]
    lane = lax.broadcasted_iota(I32, t.shape, 1)
    first = (lane & (2 * quarter - 1)) < quarter
    rot = jnp.where(first, -pltpu.roll(t, n - quarter, 1), pltpu.roll(t, quarter, 1))
    return t * cos + rot * sin


def _adaln_kernel(c_ref, w_ref, b_ref, o_ref):
    c = c_ref[...]
    s_hi, s_lo = _split2(c * jax.nn.sigmoid(c))
    w_hi, w_lo = _split2(w_ref[...])
    o_ref[...] = _dot(s_hi, w_hi) + _dot(s_lo, w_hi) + _dot(s_hi, w_lo) + b_ref[...]


def _adaln(cond8, ada_w, ada_b):
    depth, d, n6 = ada_w.shape
    tn = 1536
    return pl.pallas_call(
        _adaln_kernel,
        out_shape=jax.ShapeDtypeStruct((depth, 8, n6), F32),
        grid=(depth, n6 // tn),
        in_specs=[
            pl.BlockSpec((8, d), lambda l, j: (0, 0)),
            pl.BlockSpec((None, d, tn), lambda l, j: (l, 0, j)),
            pl.BlockSpec((None, 1, tn), lambda l, j: (l, 0, j)),
        ],
        out_specs=pl.BlockSpec((None, 8, tn), lambda l, j: (l, 0, j)),
        compiler_params=_cparams(("parallel", "parallel")),
    )(cond8, ada_w, ada_b.reshape(depth, 1, n6))


def _even_in_kernel(x_ref, mod_ref, gn_ref, win_ref, qg_ref, wuq_ref, kvg_ref, wukk_ref, wukv_ref,
                    cq_ref, sq_ref, cw_ref, sw_ref,
                    qm_ref, km_ref, vm_ref, qw_ref, kw_ref, vw_ref):
    d = D_MODEL
    mod = mod_ref[...]
    h = _rms(x_ref[...], gn_ref[...]) * (1.0 + mod[:, d:2 * d]) + mod[:, 0:d]
    z = _dot(h.astype(BF16), win_ref[...])
    cq, sq, cw, sw = cq_ref[...], sq_ref[...], cw_ref[...], sw_ref[...]
    cqn = _rms(z[:, 0:256], qg_ref[...]).astype(BF16)
    q = _dot(cqn, wuq_ref[...])
    q = _rope(q, _lane_tile(cq, 8), _lane_tile(sq, 8), MLA_ROPE // 4)
    qm_ref[...] = (q * (LOG2E * (MLA_NOPE + MLA_ROPE) ** -0.5)).astype(BF16)
    ckvn = _rms(z[:, 256:384], kvg_ref[...]).astype(BF16)
    kn = _dot(ckvn, wukk_ref[...])
    kr = _rope(z[:, 384:512], cq, sq, MLA_ROPE // 4)
    km_ref[...] = (kn + _lane_tile(kr, 8)).astype(BF16)
    vm_ref[...] = _dot_nt(wukv_ref[...], ckvn).astype(BF16)
    qw = _rope(z[:, 512:1024], _lane_tile(cw, 4), _lane_tile(sw, 4), WIN_HEAD_DIM // 4)
    qw_ref[...] = (qw * (WIN_HEAD_DIM ** -0.5)).astype(BF16)
    kw = _rope(z[:, 1024:1280], _lane_tile(cw, 2), _lane_tile(sw, 2), WIN_HEAD_DIM // 4)
    kw_ref[...] = kw.astype(BF16)
    vw_ref[...] = z[:, 1280:1536].astype(BF16)


def _even_in(x, mod, mod_row, gn, wts, tabs, tm):
    b, n, d = x.shape
    win, qg, wuq, kvg, wukk, wukv = wts
    nt = n // tm
    row = (lambda bi, i: (bi, 0, 0)) if mod_row is None else (lambda bi, i: (mod_row, 0, 0))
    full = lambda a: pl.BlockSpec(a.shape, lambda bi, i: (0,) * a.ndim)
    tab = pl.BlockSpec((tm, LANES), lambda bi, i: (i, 0))
    outw = (1024, 1024, None, 512, 256, 256)
    rowspec = lambda w: pl.BlockSpec((None, tm, w), lambda bi, i: (bi, i, 0))
    colspec = pl.BlockSpec((None, 512, tm), lambda bi, i: (bi, 0, i))
    return pl.pallas_call(
        _even_in_kernel,
        out_shape=[jax.ShapeDtypeStruct((b, 512, n) if w is None else (b, n, w), BF16) for w in outw],
        grid=(b, nt),
        in_specs=[pl.BlockSpec((None, tm, d), lambda bi, i: (bi, i, 0)),
                  pl.BlockSpec((None, 1, 6 * d), row),
                  full(gn), full(win), full(qg), full(wuq), full(kvg), full(wukk), full(wukv),
                  tab, tab, tab, tab],
        out_specs=[colspec if w is None else rowspec(w) for w in outw],
        compiler_params=_cparams(("parallel", "parallel")),
    )(x, mod, gn, win, qg, wuq, kvg, wukk, wukv, *tabs)


def _mla_attn_kernel(nseg, q_ref, *refs):
    ks, vts = refs[0:2 * nseg:2], refs[1:2 * nseg:2]
    o_ref = refs[2 * nseg]
    s_bufs = refs[2 * nseg + 1:2 * nseg + 3]
    p_bufs = refs[2 * nseg + 3:2 * nseg + 5]
    pieces, base = [], 0
    for k in ks:
        n = k.shape[0]
        pieces += [(k, c0, min(n, c0 + MLA_KEY_CHUNK), base + c0) for c0 in range(0, n, MLA_KEY_CHUNK)]
        base += n

    def score_chunk(h, piece, buf):
        k, c0, c1, g0 = piece
        hs = slice(h * LANES, (h + 1) * LANES)
        s = _dot_nt(k[c0:c1, hs], q_ref[:, hs])
        buf[g0:g0 + c1 - c0, :] = s
        return jnp.max(s, axis=0, keepdims=True)

    def prob_chunk(piece, sbuf, pbuf, m):
        _, c0, c1, g0 = piece
        p = jnp.exp2(sbuf[g0:g0 + c1 - c0, :] - m)
        pbuf[g0:g0 + c1 - c0, :] = p.astype(BF16)
        return jnp.sum(p, axis=0, keepdims=True)

    m_next = functools.reduce(jnp.maximum, [score_chunk(0, pc, s_bufs[0]) for pc in pieces])
    outs = []
    for h in range(MLA_HEADS):
        m_cur, maxes, sums = m_next, [], []
        for pc in pieces:
            if h + 1 < MLA_HEADS:
                maxes.append(score_chunk(h + 1, pc, s_bufs[(h + 1) % 2]))
            sums.append(prob_chunk(pc, s_bufs[h % 2], p_bufs[h % 2], m_cur))
        if h + 1 < MLA_HEADS:
            m_next = functools.reduce(jnp.maximum, maxes)
        l = functools.reduce(jnp.add, sums)
        vrows = slice(h * MLA_V, (h + 1) * MLA_V)
        ot, base = None, 0
        for k, vt in zip(ks, vts):
            n = k.shape[0]
            part = _dot(vt[vrows, :], p_bufs[h % 2][base:base + n, :])
            ot = part if ot is None else ot + part
            base += n
        outs.append(ot * (1.0 / l))
    o_ref[...] = jnp.concatenate(outs, axis=0).T.astype(BF16)


def _mla_attn(q, segs, tq):
    b, n, _ = q.shape
    in_specs = [pl.BlockSpec((None, tq, 1024), lambda bi, i: (bi, i, 0))]
    args = [q]
    keys = 0
    for k, vt in segs:
        lk = k.shape[1]
        keys += lk
        in_specs += [pl.BlockSpec((None, lk, 1024), lambda bi, i: (bi, 0, 0)),
                     pl.BlockSpec((None, 512, lk), lambda bi, i: (bi, 0, 0))]
        args += [k, vt]
    return pl.pallas_call(
        functools.partial(_mla_attn_kernel, len(segs)),
        out_shape=jax.ShapeDtypeStruct((b, n, 512), BF16),
        grid=(b, n // tq),
        in_specs=in_specs,
        out_specs=pl.BlockSpec((None, tq, 512), lambda bi, i: (bi, i, 0)),
        scratch_shapes=[pltpu.VMEM((keys, tq), F32), pltpu.VMEM((keys, tq), F32),
                        pltpu.VMEM((keys, tq), BF16), pltpu.VMEM((keys, tq), BF16)],
        compiler_params=_cparams(("parallel", "parallel")),
    )(*args)


def _gqa_kernel(has_win, nb, sink_ref, q_ref, *refs):
    if has_win:
        kp, kc, kn, vp, vc, vn, kx, vx, o_ref = refs
    else:
        kx, vx, o_ref = refs
    tq = q_ref.shape[0]
    i = pl.program_id(1)
    lane = lax.broadcasted_iota(I32, (tq, LANES), 1)
    row2 = lax.broadcasted_iota(I32, (2 * tq, 1), 0)
    half = WIN_HEAD_DIM
    npair = WIN_HEADS // 2
    kcats, vcats = [], []
    for g in range(WIN_KV_HEADS):
        gs = slice(g * LANES, (g + 1) * LANES)
        if has_win:
            kcats.append(jnp.concatenate([kp[:, gs], kc[:, gs], kn[:, gs], kx[:, gs]], axis=0))
            vcats.append(jnp.concatenate([vp[:, gs], vc[:, gs], vn[:, gs], vx[:, gs]], axis=0))
        else:
            kcats.append(kx[:, gs])
            vcats.append(vx[:, gs])
    if has_win:
        w = WIN_BLOCK
        shape = (2 * tq, kcats[0].shape[0])
        r = lax.broadcasted_iota(I32, shape, 0) & (tq - 1)
        c = lax.broadcasted_iota(I32, shape, 1)
        big = jnp.int32(1 << 20)
        no_prev = jnp.where(i > 0, 0, big)
        no_next = jnp.where(i < nb - 1, 0, big)
        ok_prev = c >= r + no_prev
        ok_next = (c - 2 * w) <= r - no_next
        valid = ((c >= w) | ok_prev) & ((c < 2 * w) | (c >= 3 * w) | ok_next)

    def score(j):
        qp = q_ref[:, j * LANES:(j + 1) * LANES]
        zero = jnp.zeros_like(qp)
        q2 = jnp.concatenate([jnp.where(lane < half, qp, zero), jnp.where(lane >= half, qp, zero)], axis=0)
        return _dot_nt(q2, kcats[j // 2])

    def softmax(j, s):
        s = jnp.where(valid, s, NEG) if has_win else s
        sk = jnp.where(row2 < tq, sink_ref[2 * j], sink_ref[2 * j + 1])
        m = jnp.maximum(jnp.max(s, axis=-1, keepdims=True), sk)
        p = jnp.exp(s - m)
        return p.astype(BF16), 1.0 / (jnp.sum(p, axis=-1, keepdims=True) + jnp.exp(sk - m))

    def values(j, p, inv):
        o2 = _dot(p, vcats[j // 2]) * inv
        o_ref[:, j * LANES:(j + 1) * LANES] = jnp.where(lane < half, o2[:tq], o2[tq:]).astype(BF16)

    s_val, p_val = {}, {}
    for t in range(npair + 2):
        if t < npair:
            s_val[t] = score(t)
        if 0 <= t - 1 < npair:
            p_val[t - 1] = softmax(t - 1, s_val.pop(t - 1))
        if 0 <= t - 2 < npair:
            values(t - 2, *p_val.pop(t - 2))


def _gqa(q, k, v, kx, vx, sink, has_win):
    b, n, _ = q.shape
    lc = kx.shape[1]
    smem = pl.BlockSpec(memory_space=pltpu.SMEM)
    ctxs = pl.BlockSpec((None, lc, 256), lambda bi, i: (bi, 0, 0))
    if has_win:
        tq = WIN_BLOCK
        nb = n // tq
        blk = lambda f: pl.BlockSpec((None, tq, 256), f)
        prev = lambda bi, i: (bi, jnp.maximum(i - 1, 0), 0)
        cur = lambda bi, i: (bi, i, 0)
        nxt = lambda bi, i: (bi, jnp.minimum(i + 1, nb - 1), 0)
        in_specs = [smem, pl.BlockSpec((None, tq, 512), cur),
                    blk(prev), blk(cur), blk(nxt), blk(prev), blk(cur), blk(nxt), ctxs, ctxs]
        args = (sink, q, k, k, k, v, v, v, kx, vx)
    else:
        tq, nb = n, 1
        in_specs = [smem, pl.BlockSpec((None, tq, 512), lambda bi, i: (bi, i, 0)), ctxs, ctxs]
        args = (sink, q, kx, vx)
    return pl.pallas_call(
        functools.partial(_gqa_kernel, has_win, nb),
        out_shape=jax.ShapeDtypeStruct((b, n, 512), BF16),
        grid=(b, nb),
        in_specs=in_specs,
        out_specs=pl.BlockSpec((None, tq, 512), lambda bi, i: (bi, i, 0)),
        compiler_params=_cparams(("parallel", "parallel")),
    )(*args)


def _log_sigmoid(z):
    return jnp.minimum(z, 0.0) - jnp.log(1.0 + jnp.exp(-jnp.abs(z)))


def _odd_in_kernel(x_ref, y0_ref, y1_ref, wt_ref, modp_ref, mod_ref, gn_ref, win_ref, wg_ref, bg_ref, lng_ref,
                   lnb_ref, ws_ref, bst_ref, xn_ref, q_ref, k_ref, v_ref, la_ref, r_ref, dl_ref):
    d = D_MODEL
    tm = x_ref.shape[0]
    wt = wt_ref[...]
    y = wt[:, 0:1] * _unpack_bf16_pairs(y0_ref[...]) + wt[:, 1:2] * _unpack_bf16_pairs(y1_ref[...])
    x = x_ref[...] + modp_ref[:, 5 * d:6 * d] * y
    xn_ref[...] = x
    mod = mod_ref[...]
    h = (_rms(x, gn_ref[...]) * (1.0 + mod[:, d:2 * d]) + mod[:, 0:d]).astype(BF16)
    z = _dot(h, win_ref[...])
    q_ref[...] = z[:, 0:256] * (GLA_DK ** -0.5)
    k_ref[...] = z[:, 256:512]
    v_ref[...] = z[:, 512:1024].astype(BF16)
    g_hi, g_lo = _split2(z[:, 1024:1152])
    w_hi, w_lo = _split2(wg_ref[...])
    zg = _dot(g_hi, w_hi) + _dot(g_lo, w_hi) + _dot(g_hi, w_lo) + bg_ref[...]
    la_ref[...] = _log_sigmoid(zg) / GLA_TAU
    r_ref[...] = z[:, 1152:1664]
    u = jax.nn.gelu(z[:, 1664:2176])
    vg = jax.nn.gelu(z[:, 2176:2688])
    mu = jnp.mean(vg, axis=-1, keepdims=True)
    vc = vg - mu
    var = jnp.mean(vc * vc, axis=-1, keepdims=True)
    vn = (vc * lax.rsqrt(var + EPS) * lng_ref[...] + lnb_ref[...]).astype(BF16)
    bst = bst_ref[...]
    for c in range(tm // SG_CHUNK):
        rows = slice(c * SG_CHUNK, (c + 1) * SG_CHUNK)
        parts = []
        for g in range(SG_GROUPS):
            cols = slice(g * LANES, (g + 1) * LANES)
            parts.append(_dot(ws_ref[g], vn[rows, cols]) + bst[:, g:g + 1])
        dl_ref[rows, :] = (u[rows, :] * jnp.concatenate(parts, axis=1)).astype(BF16)


def _odd_in(x, pending, tile0, modp, mod, mod_row, gn, wts):
    b, n, d = x.shape
    tm = MOE_TILE
    yt, wt = pending
    ntiles = wt.shape[0] // tm
    win, wg, bg, lng, lnb, ws, bst = wts
    row = (lambda bi, i: (bi, 0, 0)) if mod_row is None else (lambda bi, i: (mod_row, 0, 0))
    full = lambda a: pl.BlockSpec(a.shape, lambda bi, i: (0,) * a.ndim)
    act = lambda wd: pl.BlockSpec((None, tm, wd), lambda bi, i: (bi, i, 0))
    tok = lambda bi, i: tile0 + bi * (n // tm) + i
    outs = [((b, n, d), F32, act(d)),
            ((b, n, 256), F32, act(256)), ((b, n, 256), F32, act(256)), ((b, n, 512), BF16, act(512)),
            ((b, n, 512), F32, act(512)), ((b, n, 512), F32, act(512)), ((b, n, 512), BF16, act(512))]
    return pl.pallas_call(
        _odd_in_kernel,
        out_shape=[jax.ShapeDtypeStruct(s, t) for s, t, _ in outs],
        grid=(b, n // tm),
        in_specs=[act(d), pl.BlockSpec((tm, d // 2), lambda bi, i: (tok(bi, i), 0)),
                  pl.BlockSpec((tm, d // 2), lambda bi, i: (ntiles + tok(bi, i), 0)),
                  pl.BlockSpec((tm, 8), lambda bi, i: (tok(bi, i), 0)),
                  pl.BlockSpec((None, 1, 6 * d), row), pl.BlockSpec((None, 1, 6 * d), row),
                  full(gn), full(win), full(wg), full(bg), full(lng), full(lnb), full(ws), full(bst)],
        out_specs=[sp for _, _, sp in outs],
        compiler_params=_cparams(("parallel", "parallel")),
    )(x, yt, yt, wt, modp, mod, gn, win, wg, bg, lng, lnb, ws, bst)


def _gla_tables():
    c = GLA_BLOCK
    t = np.arange(c)[:, None]
    u = np.arange(c)[None, :]
    levels = [c >> i for i in range(int(np.log2(c)) + 1)]
    cum = np.zeros((2, 2 * len(levels), c, c), np.float32)
    pair = np.zeros((2, len(levels), c, c), np.float32)
    for li, m in enumerate(levels):
        same = (t // m) == (u // m)
        cum[0, 2 * li] = same & (u <= t)
        cum[0, 2 * li + 1] = same & (u > t)
        cum[1, 2 * li] = same & (u >= t)
        cum[1, 2 * li + 1] = same & (u < t)
        if li > 0:
            pair[0, li] = ((t // m) % 2 == 1) & ((u // m) == (t // m) - 1)
            pair[1, li] = ((t // m) % 2 == 0) & ((u // m) == (t // m) + 1)
    pair[:, 0] = np.eye(c, dtype=np.float32)
    nlev = len(levels)
    m1 = cum[:, 0::2].reshape(2, nlev * c, c)
    m2t = np.concatenate([cum[:, 2 * li + 1].transpose(0, 2, 1) for li in range(nlev)], axis=2)
    return np.concatenate([m1, m1], axis=2), np.concatenate([m2t, m2t], axis=1), pair, nlev


def _gla_chain(nlev, q, k, la, v_ref, cumq, cumkt, pm_ref, st_ref, o_ref):
    c = GLA_BLOCK
    lat, kt = la.T, k.T
    l_hi, l_mid = _split2(la)
    t_hi, t_mid = _split2(lat)
    exq = jnp.exp(_dot(cumq, jnp.concatenate([l_hi, l_mid], axis=0)))
    exk = jnp.exp(_dot(jnp.concatenate([t_hi, t_mid], axis=1), cumkt))
    gcol = jnp.exp(jnp.sum(lat, axis=1, keepdims=True))
    yield
    qe, ke = [], []
    for li in range(nlev):
        qe.append((q * exq[li * c:(li + 1) * c]).astype(BF16))
        ke.append((kt * exk[:, li * c:(li + 1) * c]).astype(BF16))
        if li % 2 == 1:
            yield
    qb, kb = q.astype(BF16), kt.astype(BF16)
    states = [st_ref[hd] for hd in range(GLA_HEADS)]
    yield
    outs, new_states = [], []
    lane = lax.broadcasted_iota(I32, (c, LANES), 1)
    srow = lax.broadcasted_iota(I32, (LANES, 1), 0)
    zero = jnp.zeros((c, LANES), BF16)
    for hd in range(GLA_HEADS):
        ps = slice((hd // 2) * LANES, (hd // 2 + 1) * LANES)
        vs = slice(hd * GLA_DV, (hd + 1) * GLA_DV)
        mine = (lane < GLA_DK) if hd % 2 == 0 else (lane >= GLA_DK)
        mine_row = (srow < GLA_DK) if hd % 2 == 0 else (srow >= GLA_DK)
        pick = lambda t: jnp.where(mine, t[:, ps], zero)
        a = pm_ref[0] * _dot(pick(qb), kb[ps, :])
        for li in range(1, nlev):
            a = a + pm_ref[li] * _dot(pick(qe[li]), ke[li][ps, :])
            if li % GLA_LEVELS_PER_PHASE == 0:
                yield
        v_h = v_ref[:, vs]
        outs.append(_dot(qe[0][:, ps], states[hd].astype(BF16)) + _dot(a.astype(BF16), v_h))
        new_states.append(states[hd] * gcol[ps, :] + jnp.where(mine_row, _dot(ke[0][ps, :], v_h), 0.0))
        yield
    o_ref[...] = jnp.concatenate(outs, axis=1)
    for hd in range(GLA_HEADS):
        st_ref[hd] = new_states[hd]
    yield


def _gla_kernel(nlev, nb, *refs):
    ins_f, ins_b = refs[0:4], refs[4:8]
    cumq_ref, cumkt_ref, pm_ref, s0_ref, of_ref, ob_ref, sf_ref = refs[8:15]
    st_refs = refs[15:]
    step = pl.program_id(1)

    @pl.when(step == 0)
    def _():
        for bb in range(nb):
            for d_ in range(2):
                st_refs[2 * bb + d_][...] = s0_ref[bb, d_]

    chains = []
    for bb in range(nb):
        for d_, (ins, o_ref) in enumerate(((ins_f, of_ref), (ins_b, ob_ref))):
            q_ref, k_ref, v_ref, la_ref = ins
            chains.append(_gla_chain(nlev, q_ref[bb], k_ref[bb], la_ref[bb], v_ref.at[bb], cumq_ref[d_],
                                     cumkt_ref[d_], pm_ref.at[d_], st_refs[2 * bb + d_], o_ref.at[bb]))
    while chains:
        chains = [ch for ch in chains if next(ch, "done") != "done"]
    for bb in range(nb):
        for d_ in range(2):
            sf_ref[bb, d_] = st_refs[2 * bb + d_][...]


def _gla(q, k, v, la, s0, cumq, cumkt, pm, nlev):
    b, n, _ = q.shape
    c = GLA_BLOCK
    nc = n // c
    nb = next(c for c in (GLA_BATCHES_PER_STEP, 2, 1) if b % c == 0)
    specs = []
    for d_ in range(2):
        pos = (lambda s_: s_) if d_ == 0 else (lambda s_: nc - 1 - s_)
        specs += [pl.BlockSpec((nb, c, 256), lambda bi, s_, pos=pos: (bi, pos(s_), 0)),
                  pl.BlockSpec((nb, c, 256), lambda bi, s_, pos=pos: (bi, pos(s_), 0)),
                  pl.BlockSpec((nb, c, 512), lambda bi, s_, pos=pos: (bi, pos(s_), 0)),
                  pl.BlockSpec((nb, c, 256), lambda bi, s_, pos=pos, d_=d_: (bi, pos(s_), d_))]
    st_spec = pl.BlockSpec((nb, 2, GLA_HEADS, GLA_DV, LANES), lambda bi, s_: (bi, 0, 0, 0, 0))
    full = lambda a: pl.BlockSpec(a.shape, lambda bi, s_: (0,) * a.ndim)
    return pl.pallas_call(
        functools.partial(_gla_kernel, nlev, nb),
        out_shape=[jax.ShapeDtypeStruct((b, n, GLA_V), F32), jax.ShapeDtypeStruct((b, n, GLA_V), F32),
                   jax.ShapeDtypeStruct((b, 2, GLA_HEADS, GLA_DV, LANES), F32)],
        grid=(b // nb, nc),
        in_specs=specs + [full(cumq), full(cumkt), full(pm), st_spec],
        out_specs=[pl.BlockSpec((nb, c, GLA_V), lambda bi, s_: (bi, s_, 0)),
                   pl.BlockSpec((nb, c, GLA_V), lambda bi, s_: (bi, nc - 1 - s_, 0)), st_spec],
        scratch_shapes=[pltpu.VMEM((GLA_HEADS, GLA_DV, LANES), F32) for _ in range(2 * nb)],
        compiler_params=_cparams(("parallel", "arbitrary")),
    )(q, k, v, la, q, k, v, la, cumq, cumkt, pm, s0)


def _odd_out_router_kernel(x_ref, of_ref, ob_ref, r_ref, dl_ref, gg_ref, w_ref, mod_ref, gn_ref, wr_ref, u_ref,
                           o_ref, *route_refs):
    d = D_MODEL
    o = of_ref[...] + ob_ref[...]
    gg = gg_ref[...]
    r = r_ref[...]
    parts = []
    for hd in range(GLA_HEADS):
        vs = slice(hd * GLA_DV, (hd + 1) * GLA_DV)
        oh = o[:, vs]
        parts.append(oh * lax.rsqrt(jnp.mean(oh * oh, axis=-1, keepdims=True) + EPS) * gg[:, vs])
    cl = (jnp.concatenate(parts, axis=1) * (r * jax.nn.sigmoid(r))).astype(BF16)
    y = _dot(cl, w_ref[0:512, :]) + _dot(dl_ref[...], w_ref[512:1024, :])
    x = x_ref[...] + mod_ref[:, 2 * d:3 * d] * y
    o_ref[...] = x
    _route_tile(x, mod_ref, gn_ref, wr_ref, u_ref, *route_refs)


def _odd_out_router(x, o_fwd, o_bwd, r, dl, gg, w, mod, gn, wr, u):
    b, n, d = x.shape
    tm = MOE_TILE
    flat = lambda t: t.reshape(-1, t.shape[-1])
    act = lambda wd: pl.BlockSpec((tm, wd), lambda i: (i, 0))
    full = lambda t: pl.BlockSpec(t.shape, lambda i: (0,) * t.ndim)
    rshapes, rspecs, rscratch = _route_out(b * n, d)
    outs = pl.pallas_call(
        _odd_out_router_kernel,
        out_shape=[jax.ShapeDtypeStruct((b * n, d), F32)] + rshapes,
        grid=(b * n // tm,),
        in_specs=[act(d), act(GLA_V), act(GLA_V), act(512), act(512), full(gg), full(w),
                  pl.BlockSpec((None, 1, 6 * d), lambda i: (i // (n // tm), 0, 0)), full(gn), full(wr), full(u)],
        out_specs=[act(d)] + rspecs,
        scratch_shapes=rscratch,
        compiler_params=_cparams(("arbitrary",)),
    )(flat(x), flat(o_fwd), flat(o_bwd), flat(r), flat(dl), gg, w, mod, gn, wr, u)
    return outs[0].reshape(x.shape), outs[1:]


def _route_tile(x, mod_ref, gn_ref, wr_ref, u_ref, h_ref, e_ref, wt_ref, r_ref, cnt_ref, carry_ref):
    d = D_MODEL
    tm = x.shape[0]

    @pl.when(pl.program_id(0) == 0)
    def _():
        carry_ref[...] = jnp.zeros_like(carry_ref)

    mod = mod_ref[...]
    h = _rms(x, gn_ref[...]) * (1.0 + mod[:, 4 * d:5 * d]) + mod[:, 3 * d:4 * d]
    h_ref[...] = _pack_bf16_pairs(h)
    h_hi, h_lo = _split2(h)
    w_hi, w_lo = _split2(wr_ref[...])
    lg = _dot_nt(w_hi, h_hi) + _dot_nt(w_lo, h_hi) + _dot_nt(w_hi, h_lo)
    rid = lax.broadcasted_iota(I32, (8, tm), 0)
    gl = jnp.where(rid < MOE_GROUPS, lg[0:8], NEG)
    gmax = jnp.max(gl, axis=0, keepdims=True)
    gsel = jnp.min(jnp.where(gl == gmax, rid, 8), axis=0, keepdims=True)
    pmax = 1.0 / jnp.sum(jnp.where(rid < MOE_GROUPS, jnp.exp(gl - gmax), 0.0), axis=0, keepdims=True)
    e_in = jnp.zeros((MOE_PER_GROUP, tm), F32)
    for g in range(MOE_GROUPS):
        e_in = e_in + jnp.where(gsel == g, lg[8 + 8 * g:16 + 8 * g], 0.0)
    v1 = jnp.max(e_in, axis=0, keepdims=True)
    i1 = jnp.min(jnp.where(e_in == v1, rid, 8), axis=0, keepdims=True)
    e_rest = jnp.where(rid == i1, -jnp.inf, e_in)
    v2 = jnp.max(e_rest, axis=0, keepdims=True)
    i2 = jnp.min(jnp.where(e_rest == v2, rid, 8), axis=0, keepdims=True)
    t = jnp.exp(v2 - v1)
    w1 = pmax / (1.0 + t)
    w2 = pmax * t / (1.0 + t)
    e1 = gsel * MOE_PER_GROUP + i1
    e2 = gsel * MOE_PER_GROUP + i2
    eid = lax.broadcasted_iota(I32, (MOE_EXPERTS, tm), 0)
    oh1 = jnp.where(eid == e1, 1.0, 0.0)
    oh2 = jnp.where(eid == e2, 1.0, 0.0)
    ohs = oh1 + oh2
    base = carry_ref[:, 0:1] + _dot(ohs.astype(BF16), u_ref[...])
    r1 = jnp.sum(oh1 * base, axis=0, keepdims=True)
    r2 = jnp.sum(oh2 * base, axis=0, keepdims=True)
    carry_ref[...] = carry_ref[...] + jnp.sum(ohs, axis=1, keepdims=True)
    cnt_ref[...] = carry_ref[...]
    e_ref[...] = jnp.concatenate([e1, e2], axis=0)
    r_ref[...] = jnp.concatenate([r1, r2], axis=0).astype(I32)
    w8 = jnp.concatenate([w1, w2, jnp.zeros((6, tm), F32)], axis=0)
    wt_ref[...] = w8.T


def _route_out(n, d):
    tm = MOE_TILE
    shapes = [jax.ShapeDtypeStruct((n, d // 2), I32), jax.ShapeDtypeStruct((2, n), I32),
              jax.ShapeDtypeStruct((n, 8), F32), jax.ShapeDtypeStruct((2, n), I32),
              jax.ShapeDtypeStruct((MOE_EXPERTS, LANES), F32)]
    specs = [pl.BlockSpec((tm, d // 2), lambda i: (i, 0)), pl.BlockSpec((2, tm), lambda i: (0, i)),
             pl.BlockSpec((tm, 8), lambda i: (i, 0)), pl.BlockSpec((2, tm), lambda i: (0, i)),
             pl.BlockSpec((MOE_EXPERTS, LANES), lambda i: (0, 0))]
    return shapes, specs, [pltpu.VMEM((MOE_EXPERTS, LANES), F32)]


def _even_out_router_kernel(nlat, x_ref, xc_ref, a_ref, ac_ref, b_ref, bc_ref, w_ref, mod_ref, gn_ref, wr_ref, u_ref,
                            xo_ref, xoc_ref, *route_refs):
    d = D_MODEL
    lat = pl.program_id(0) < nlat
    a = jnp.where(lat, a_ref[...], ac_ref[...])
    b = jnp.where(lat, b_ref[...], bc_ref[...])
    y = _dot(a, w_ref[0:512, :]) + _dot(b, w_ref[512:1024, :])
    x = jnp.where(lat, x_ref[...], xc_ref[...]) + mod_ref[:, 2 * d:3 * d] * y

    @pl.when(lat)
    def _():
        xo_ref[...] = x

    @pl.when(jnp.logical_not(lat))
    def _():
        xoc_ref[...] = x

    _route_tile(x, mod_ref, gn_ref, wr_ref, u_ref, *route_refs)


def _even_out_router(x, xc, oa, oac, ob, obc, w, mod, ctx_row, gn, wr, u):
    b, n, d = x.shape
    tm = MOE_TILE
    flat = lambda t: t.reshape(-1, t.shape[-1])
    nlat = b * n // tm
    ntok = b * n + xc.shape[0] * xc.shape[1]
    lat = lambda wd: pl.BlockSpec((tm, wd), lambda i: (jnp.minimum(i, nlat - 1), 0))
    ctx = lambda wd: pl.BlockSpec((tm, wd), lambda i: (jnp.maximum(i - nlat, 0), 0))
    full = lambda t: pl.BlockSpec(t.shape, lambda i: (0,) * t.ndim)
    modrow = lambda i: (jnp.where(i < nlat, i // (n // tm), ctx_row), 0, 0)
    rshapes, rspecs, rscratch = _route_out(ntok, d)
    outs = pl.pallas_call(
        functools.partial(_even_out_router_kernel, nlat),
        out_shape=[jax.ShapeDtypeStruct((b * n, d), F32), jax.ShapeDtypeStruct((ntok - b * n, d), F32)] + rshapes,
        grid=(ntok // tm,),
        in_specs=[lat(d), ctx(d), lat(512), ctx(512), lat(512), ctx(512), full(w),
                  pl.BlockSpec((None, 1, 6 * d), modrow), full(gn), full(wr), full(u)],
        out_specs=[lat(d), ctx(d)] + rspecs,
        scratch_shapes=rscratch,
        compiler_params=_cparams(("arbitrary",)),
    )(flat(x), flat(xc), flat(oa), flat(oac), flat(ob), flat(obc), w, mod, gn, wr, u)
    return outs[0].reshape(x.shape), outs[1].reshape(xc.shape), outs[2:]


def _sc_permute_rows(src, dest, scatter):
    rows, d = dest.shape[0], src.shape[1]
    n = rows // 2
    info = plsc.get_sparse_core_info()
    workers = info.num_cores * info.num_subcores
    per_worker = rows // workers
    chunk = next(c for c in SC_CHUNKS if per_worker % c == 0)
    assert rows == per_worker * workers and n % per_worker == 0
    mesh = plsc.VectorSubcoreMesh(core_axis_name="c", subcore_axis_name="s")

    def body(src_hbm, dest_hbm, out_hbm, idx_v, rows_v, sem):
        base = (lax.axis_index("s") * info.num_cores + lax.axis_index("c")) * per_worker

        @pl.loop(0, per_worker // chunk)
        def _(j):
            a0 = base + j * chunk
            pltpu.sync_copy(dest_hbm.at[pl.ds(a0, chunk)], idx_v)
            if scatter:
                t0 = jnp.where(a0 >= n, a0 - n, a0)
                pltpu.sync_copy(src_hbm.at[pl.ds(t0, chunk)], rows_v)
                pltpu.async_copy(rows_v, out_hbm.at[idx_v], sem).wait()
            else:
                pltpu.async_copy(src_hbm.at[idx_v], rows_v, sem).wait()
                pltpu.sync_copy(rows_v, out_hbm.at[pl.ds(a0, chunk)])

    return pl.kernel(
        body, out_type=jax.ShapeDtypeStruct((rows, d), src.dtype), mesh=mesh,
        scratch_types=[pltpu.VMEM((chunk,), I32), pltpu.VMEM((chunk, d), src.dtype), pltpu.SemaphoreType.DMA],
    )(src, dest)


def _gmm_kernel(layer, vt_ref, ve_ref, vlo_ref, vhi_ref, vfirst_ref, vslot_ref, vnext_ref, nv_ref,
                xs_ref, wg_hbm, wu_hbm, wd_hbm, ys_ref, wgs_ref, wus_ref, wds_ref, wgb_ref, wub_ref, wdb_ref, sem):
    del vt_ref
    v = pl.program_id(0)

    def fetch(e, slot):
        return [pltpu.make_async_copy(w.at[layer, e], s.at[slot], sem.at[slot, i])
                for i, (w, s) in enumerate(((wg_hbm, wgs_ref), (wu_hbm, wus_ref), (wd_hbm, wds_ref)))]

    @pl.when(v < nv_ref[0])
    def _():
        @pl.when((v == 0) | (ve_ref[v] != ve_ref[jnp.maximum(v - 1, 0)]))
        def _():
            slot = vslot_ref[v]

            @pl.when(v == 0)
            def _():
                for c in fetch(ve_ref[0], 0):
                    c.start()

            for c in fetch(ve_ref[v], slot):
                c.wait()
            wgb_ref[...] = wgs_ref[slot].astype(BF16)
            wub_ref[...] = wus_ref[slot].astype(BF16)
            wdb_ref[...] = wds_ref[slot].astype(BF16)

            @pl.when(vnext_ref[v] >= 0)
            def _():
                for c in fetch(vnext_ref[v], 1 - slot):
                    c.start()

        x = _unpack_bf16_pairs(xs_ref[...]).astype(BF16)
        g = _dot(x, wgb_ref[...])
        u = _dot(x, wub_ref[...])
        y = _pack_bf16_pairs(_dot((g * jax.nn.sigmoid(g) * u).astype(BF16), wdb_ref[...]))
        row = lax.broadcasted_iota(I32, (y.shape[0], 1), 0)
        mine = (row >= vlo_ref[v]) & (row < vhi_ref[v])

        @pl.when(vfirst_ref[v] == 1)
        def _():
            ys_ref[...] = jnp.where(mine, y, 0)

        @pl.when(vfirst_ref[v] == 0)
        def _():
            ys_ref[...] = jnp.where(mine, y, ys_ref[...])


def _gmm(xs, visits, layer, wg, wu, wd):
    rows, dw = xs.shape
    d = 2 * dw
    tm = MOE_TILE
    hid = wg.shape[-1]
    nvis = rows // tm + MOE_EXPERTS - 1
    tile = lambda v, vt, *_: (vt[v], 0)
    anyspec = pl.BlockSpec(memory_space=pl.ANY)
    return pl.pallas_call(
        functools.partial(_gmm_kernel, layer),
        out_shape=jax.ShapeDtypeStruct((rows, dw), I32),
        grid_spec=pltpu.PrefetchScalarGridSpec(
            num_scalar_prefetch=8, grid=(nvis,),
            in_specs=[pl.BlockSpec((tm, dw), tile), anyspec, anyspec, anyspec],
            out_specs=pl.BlockSpec((tm, dw), tile),
            scratch_shapes=[pltpu.VMEM((2, d, hid), F32), pltpu.VMEM((2, d, hid), F32), pltpu.VMEM((2, hid, d), F32),
                            pltpu.VMEM((d, hid), BF16), pltpu.VMEM((d, hid), BF16), pltpu.VMEM((hid, d), BF16),
                            pltpu.SemaphoreType.DMA((2, 3))]),
        compiler_params=_cparams(("arbitrary",)),
    )(*visits, xs, wg, wu, wd)


def _combine_kernel(x_ref, wt_ref, mod_ref, fg_ref, y0_ref, y1_ref, o_ref):
    d = D_MODEL
    wt = wt_ref[...]
    y = wt[:, 0:1] * _unpack_bf16_pairs(y0_ref[...]) + wt[:, 1:2] * _unpack_bf16_pairs(y1_ref[...])
    o_ref[...] = _rms(x_ref[...] + mod_ref[:, 5 * d:6 * d] * y, fg_ref[...])


def _combine(x2, wt, mod, rows_per_mod, fg, yt, tm=COMBINE_TILE):
    n, d = x2.shape
    tm = min(tm, rows_per_mod)
    assert rows_per_mod % tm == 0
    ntiles = n // tm
    return pl.pallas_call(
        _combine_kernel,
        out_shape=jax.ShapeDtypeStruct((n, d), F32),
        grid=(ntiles,),
        in_specs=[pl.BlockSpec((tm, d), lambda i: (i, 0)), pl.BlockSpec((tm, 8), lambda i: (i, 0)),
                  pl.BlockSpec((None, 1, 6 * d), lambda i: (i // (rows_per_mod // tm), 0, 0)),
                  pl.BlockSpec(fg.shape, lambda i: (0, 0)),
                  pl.BlockSpec((tm, d // 2), lambda i: (i, 0)), pl.BlockSpec((tm, d // 2), lambda i: (ntiles + i, 0))],
        out_specs=pl.BlockSpec((tm, d), lambda i: (i, 0)),
        compiler_params=_cparams(("parallel",)),
    )(x2, wt, mod, fg, yt, yt)


def _pick(table, idx):
    hot = idx[..., None] == jnp.arange(table.shape[0], dtype=I32)
    return jnp.sum(jnp.where(hot, table, 0), axis=-1)


def _moe_plan(counts, e, r, rows):
    tm = MOE_TILE
    ends = jnp.cumsum(counts)
    starts = ends - counts
    dest = (_pick(starts, e) + r).reshape(-1)
    first_tile = starts // tm
    nvis = jnp.where(counts > 0, (ends - 1) // tm - first_tile + 1, 0)
    vend = jnp.cumsum(nvis)
    nv = vend[-1:]
    v = jnp.minimum(jnp.arange(rows // tm + MOE_EXPERTS - 1, dtype=I32), nv[0] - 1)
    ve = jnp.sum((vend[None, :] <= v[:, None]).astype(I32), axis=1)
    vt = _pick(first_tile, ve) + v - _pick(vend - nvis, ve)
    vlo = jnp.maximum(_pick(starts, ve) - vt * tm, 0)
    vhi = jnp.minimum(_pick(ends, ve) - vt * tm, tm)
    vfirst = jnp.concatenate([jnp.ones((1,), I32), (vt[1:] != vt[:-1]).astype(I32)])
    changed = jnp.concatenate([jnp.ones((1,), I32), (ve[1:] != ve[:-1]).astype(I32)])
    vslot = (jnp.cumsum(changed) - 1) % 2
    eid = jnp.arange(MOE_EXPERTS, dtype=I32)
    later = (eid[None, :] > eid[:, None]) & (counts[None, :] > 0)
    nxt = jnp.min(jnp.where(later, eid[None, :], MOE_EXPERTS), axis=1)
    vnext = _pick(jnp.where(nxt < MOE_EXPERTS, nxt, -1), ve)
    return dest, (vt, ve, vlo, vhi, vfirst, vslot.astype(I32), vnext.astype(I32), nv)


def _moe_experts(routed, layer, wg, wu, wd):
    h, e, wt, r, cnt = routed
    dest, visits = _moe_plan(cnt[:, 0].astype(I32), e, r, 2 * h.shape[0])
    xs = _sc_permute_rows(h, dest, scatter=True)
    ys = _gmm(xs, visits, layer, wg, wu, wd)
    return _sc_permute_rows(ys, dest, scatter=False), wt


def _rope_tables(rows, dim):
    row = jnp.repeat(jnp.arange(rows, dtype=F32), GRID_W)
    col = jnp.tile(jnp.arange(GRID_W, dtype=F32), rows)
    half = dim // 2
    inv = jnp.power(ROPE_BASE, -jnp.arange(0, half, 2, dtype=F32) / half)
    ar = row[:, None] * inv[None, :]
    ac = col[:, None] * inv[None, :]
    ang = jnp.concatenate([ar, ar, ac, ac], axis=-1)
    return jnp.cos(ang), jnp.sin(ang)


def _even_tables(n, with_rope):
    if with_rope:
        cm, sm = _rope_tables(n // GRID_W, MLA_ROPE)
        cwin, swin = _rope_tables(n // GRID_W, WIN_HEAD_DIM)
    else:
        cm, sm = jnp.ones((n, MLA_ROPE), F32), jnp.zeros((n, MLA_ROPE), F32)
        cwin, swin = jnp.ones((n, WIN_HEAD_DIM), F32), jnp.zeros((n, WIN_HEAD_DIM), F32)
    one, zero = jnp.ones((n, MLA_NOPE), F32), jnp.zeros((n, MLA_NOPE), F32)
    pad = jnp.zeros((n, LANES - MLA_NOPE - MLA_ROPE), F32)
    return (jnp.concatenate([one, cm, pad], axis=1), jnp.concatenate([zero, sm, pad], axis=1),
            jnp.concatenate([cwin, cwin], axis=1), jnp.concatenate([swin, swin], axis=1))


def _even_weights(w_in, qg, w_uq, kvg, w_ukv):
    d = w_in.shape[0]
    o = np.cumsum([0, MLA_Q_RANK, MLA_KV_RANK, MLA_ROPE, 512, 128, 128])
    cq, ckv, kr, qw, kw, vw = [w_in[:, o[i]:o[i + 1]] for i in range(6)]
    z = lambda c: jnp.zeros((d, c), F32)
    kr128 = jnp.concatenate([z(MLA_NOPE), kr, z(LANES - MLA_NOPE - MLA_ROPE)], axis=1)
    dup = lambda t: jnp.concatenate([t[:, 0:64], t[:, 0:64], t[:, 64:128], t[:, 64:128]], axis=1)
    win = jnp.concatenate([cq, ckv, kr128, qw, dup(kw), dup(vw)], axis=1).astype(BF16)
    uq = w_uq.reshape(MLA_Q_RANK, MLA_HEADS, MLA_NOPE + MLA_ROPE)
    uq = jnp.pad(uq, ((0, 0), (0, 0), (0, LANES - MLA_NOPE - MLA_ROPE))).reshape(MLA_Q_RANK, MLA_HEADS * LANES)
    ukv = w_ukv.reshape(MLA_KV_RANK, MLA_HEADS, MLA_NOPE + MLA_V)
    ukk = jnp.pad(ukv[:, :, :MLA_NOPE], ((0, 0), (0, 0), (0, LANES - MLA_NOPE))).reshape(MLA_KV_RANK, MLA_HEADS * LANES)
    ukvv = ukv[:, :, MLA_NOPE:].reshape(MLA_KV_RANK, MLA_HEADS * MLA_V)
    return (win, qg.reshape(1, -1), uq.astype(BF16), kvg.reshape(1, -1), ukk.astype(BF16), ukvv.T.astype(BF16))


def _odd_weights(w_in, w_g2, b_g, ln_g, ln_b, w_s, b_s):
    d = w_in.shape[0]
    o = np.cumsum([0, GLA_K, GLA_K, GLA_V, 2 * GLA_GATE_RANK, GLA_V, SG_WIDTH, SG_WIDTH])
    q, k, v, g, r, u, vg = [w_in[:, o[i]:o[i + 1]] for i in range(7)]
    g128 = jnp.concatenate([g, jnp.zeros((d, LANES - 2 * GLA_GATE_RANK), F32)], axis=1)
    win = jnp.concatenate([q, k, v, g128, r, u, vg], axis=1).astype(BF16)
    zr = jnp.zeros((GLA_GATE_RANK, GLA_K), F32)
    pad = jnp.zeros((LANES - 2 * GLA_GATE_RANK, GLA_K), F32)
    wg = jnp.concatenate([jnp.concatenate([w_g2[0], zr, pad], axis=0),
                          jnp.concatenate([zr, w_g2[1], pad], axis=0)], axis=1)
    bg = b_g.reshape(1, 2 * GLA_K)
    return (win, wg, bg, ln_g.reshape(1, -1), ln_b.reshape(1, -1), w_s.astype(BF16), b_s.T)


def kernel(x, c, ctx, c_ctx, ada_w, ada_b, norm_mix_g, norm_ffn_g, even_w_in, mla_q_norm_g, mla_w_uq, mla_kv_norm_g, mla_w_ukv, win_sink, even_w_out, odd_w_in, gla_w_g2, gla_b_g, gla_norm_g, sg_ln_g, sg_ln_b, sg_w_s, sg_b_s, odd_w_out, moe_w_rg, moe_w_re, moe_w_gate, moe_w_up, moe_w_down, final_norm_g):
    b, n, d = x.shape
    lc = ctx.shape[1]
    depth = ada_w.shape[0]
    assert depth == 2 and d == D_MODEL and b < 8
    assert n % 512 == 0 and lc % MOE_TILE == 0 and n % GRID_W == 0
    tm = 512 if n % 512 == 0 else 256
    tq = 256

    cond8 = jnp.concatenate([c, c_ctx[None, :], jnp.zeros((8 - b - 1, d), F32)], axis=0)
    mod_all = _adaln(cond8, ada_w, ada_b).reshape(depth, 8, 1, 6 * d)
    ctx_row = b
    u_tri = jnp.asarray(np.triu(np.ones((MOE_TILE, MOE_TILE), np.float32), 1), BF16)
    fg = final_norm_g.reshape(1, d)

    def router_w(layer):
        return jnp.concatenate([moe_w_rg[layer].T, jnp.zeros((8 - MOE_GROUPS, d), F32), moe_w_re[layer].T], axis=0)

    def moe_experts(routed, layer):
        return _moe_experts(routed, layer, moe_w_gate, moe_w_up, moe_w_down)

    mod = mod_all[0]
    gn = norm_mix_g[0].reshape(1, d)
    ew = _even_weights(even_w_in[0], mla_q_norm_g[0], mla_w_uq[0], mla_kv_norm_g[0], mla_w_ukv[0])
    qm_l, km_l, vm_l, qw_l, kw_l, vw_l = _even_in(x, mod, None, gn, ew, _even_tables(n, True), tm)
    qm_c, km_c, vm_c, qw_c, kw_c, vw_c = _even_in(ctx, mod, ctx_row, gn, ew, _even_tables(lc, False), lc)
    w_out = even_w_out[0].astype(BF16)
    sink = win_sink[0]
    oa_l = _mla_attn(qm_l, [(km_l, vm_l), (km_c, vm_c)], tq)
    ob_l = _gqa(qw_l, kw_l, vw_l, kw_c, vw_c, sink, True)
    oa_c = _mla_attn(qm_c, [(km_c, vm_c)], lc)
    ob_c = _gqa(qw_c, None, None, kw_c, vw_c, sink, False)
    xl, xc, routed = _even_out_router(x, ctx, oa_l, oa_c, ob_l, ob_c, w_out, mod, ctx_row,
                                      norm_ffn_g[0].reshape(1, d), router_w(0), u_tri)
    pending = moe_experts(routed, 0)

    mod = mod_all[1]
    gn = norm_mix_g[1].reshape(1, d)
    ow = _odd_weights(odd_w_in[0], gla_w_g2[0], gla_b_g[0], sg_ln_g[0], sg_ln_b[0], sg_w_s[0], sg_b_s[0])
    xl, q_l, k_l, v_l, la_l, r_l, dl_l = _odd_in(xl, pending, 0, mod_all[0], mod, None, gn, ow)
    _, q_c, k_c, v_c, la_c, _, _ = _odd_in(xc, pending, b * n // MOE_TILE, mod_all[0], mod, ctx_row, gn, ow)
    cumq_np, cumkt_np, pm_np, nlev = _gla_tables()
    cumq, cumkt = jnp.asarray(cumq_np, BF16), jnp.asarray(cumkt_np, BF16)
    pm = jnp.asarray(pm_np, F32)
    s0 = jnp.zeros((b, 2, GLA_HEADS, GLA_DV, LANES), F32)
    _, _, s_ctx = _gla(q_c, k_c, v_c, la_c, s0, cumq, cumkt, pm, nlev)
    o_fwd, o_bwd, _ = _gla(q_l, k_l, v_l, la_l, s_ctx, cumq, cumkt, pm, nlev)
    xl, routed = _odd_out_router(xl, o_fwd, o_bwd, r_l, dl_l, gla_norm_g[0].reshape(1, -1),
                                 odd_w_out[0].astype(BF16), mod, norm_ffn_g[1].reshape(1, d), router_w(1), u_tri)
    yt, wt = moe_experts(routed, 1)
    return _combine(xl.reshape(b * n, d), wt, mod, n, fg, yt).reshape(b, n, d)
```

```python
import functools

import numpy as np
import jax
import jax.numpy as jnp
from jax import lax
from jax.experimental import pallas as pl
from jax.experimental.pallas import tpu as pltpu
from jax.experimental.pallas import tpu_sc as plsc

F32 = jnp.float32
BF16 = jnp.bfloat16
I32 = jnp.int32

D_MODEL = 1024
GRID_W = 64
EPS = 1e-6
ROPE_BASE = 10000.0
MLA_HEADS = 8
MLA_Q_RANK = 256
MLA_KV_RANK = 128
MLA_NOPE = 64
MLA_ROPE = 32
MLA_V = 64
WIN_HEADS = 8
WIN_KV_HEADS = 2
WIN_HEAD_DIM = 64
WIN_BLOCK = 128
GLA_HEADS = 4
GLA_DK = 64
GLA_DV = 128
GLA_GATE_RANK = 16
GLA_TAU = 16.0
GLA_K = GLA_HEADS * GLA_DK
GLA_V = GLA_HEADS * GLA_DV
SG_GROUPS = 4
SG_CHUNK = 128
SG_WIDTH = 512
MOE_GROUPS = 4
MOE_PER_GROUP = 8
MOE_EXPERTS = 32
MOE_HIDDEN = 512

LANES = 128
GLA_BLOCK = 128
GLA_LEVELS_PER_PHASE = 2
GLA_BATCHES_PER_STEP = 2
MOE_TILE = 256
COMBINE_TILE = 1024
MLA_KEY_CHUNK = 1024
SC_CHUNKS = (128, 64, 32)
NEG = -1e30
LOG2E = 1.4426950408889634
VMEM_LIMIT = 56 * 1024 * 1024


def _cparams(sem):
    return pltpu.CompilerParams(dimension_semantics=sem, vmem_limit_bytes=VMEM_LIMIT)


def _dot(a, b):
    return jnp.dot(a, b, preferred_element_type=F32)


def _dot_nt(a, b):
    return lax.dot_general(a, b, (((1,), (1,)), ((), ())), preferred_element_type=F32)


def _split2(a):
    hi = a.astype(BF16)
    lo = (a - hi.astype(F32)).astype(BF16)
    return hi, lo


def _pack_bf16_pairs(x):
    k = x.shape[1] // 2
    bits = lax.bitcast_convert_type(x.astype(BF16).astype(F32), jnp.uint32)
    return lax.bitcast_convert_type(bits[:, :k] | (bits[:, k:] >> 16), I32)


def _unpack_bf16_pairs(w):
    bits = lax.bitcast_convert_type(w, jnp.uint32)
    hi = lax.bitcast_convert_type(bits & jnp.uint32(0xFFFF0000), F32)
    lo = lax.bitcast_convert_type(bits << 16, F32)
    return jnp.concatenate([hi, lo], axis=1)


def _rms(x, g):
    ms = jnp.mean(x * x, axis=-1, keepdims=True)
    return x * lax.rsqrt(ms + EPS) * g


def _lane_tile(t, reps):
    return t if reps == 1 else jnp.concatenate([t] * reps, axis=1)


def _rope(t, cos, sin, quarter):
    n = t.shape[1]
    lane = lax.broadcasted_iota(I32, t.shape, 1)
    first = (lane & (2 * quarter - 1)) < quarter
    rot = jnp.where(first, -pltpu.roll(t, n - quarter, 1), pltpu.roll(t, quarter, 1))
    return t * cos + rot * sin


def _adaln_kernel(c_ref, w_ref, b_ref, o_ref):
    c = c_ref[...]
    s_hi, s_lo = _split2(c * jax.nn.sigmoid(c))
    w_hi, w_lo = _split2(w_ref[...])
    o_ref[...] = _dot(s_hi, w_hi) + _dot(s_lo, w_hi) + _dot(s_hi, w_lo) + b_ref[...]


def _adaln(cond8, ada_w, ada_b):
    depth, d, n6 = ada_w.shape
    tn = 1536
    return pl.pallas_call(
        _adaln_kernel,
        out_shape=jax.ShapeDtypeStruct((depth, 8, n6), F32),
        grid=(depth, n6 // tn),
        in_specs=[
            pl.BlockSpec((8, d), lambda l, j: (0, 0)),
            pl.BlockSpec((None, d, tn), lambda l, j: (l, 0, j)),
            pl.BlockSpec((None, 1, tn), lambda l, j: (l, 0, j)),
        ],
        out_specs=pl.BlockSpec((None, 8, tn), lambda l, j: (l, 0, j)),
        compiler_params=_cparams(("parallel", "parallel")),
    )(cond8, ada_w, ada_b.reshape(depth, 1, n6))


def _even_in_kernel(x_ref, mod_ref, gn_ref, win_ref, qg_ref, wuq_ref, kvg_ref, wukk_ref, wukv_ref,
                    cq_ref, sq_ref, cw_ref, sw_ref,
                    qm_ref, km_ref, vm_ref, qw_ref, kw_ref, vw_ref):
    d = D_MODEL
    mod = mod_ref[...]
    h = _rms(x_ref[...], gn_ref[...]) * (1.0 + mod[:, d:2 * d]) + mod[:, 0:d]
    z = _dot(h.astype(BF16), win_ref[...])
    cq, sq, cw, sw = cq_ref[...], sq_ref[...], cw_ref[...], sw_ref[...]
    cqn = _rms(z[:, 0:256], qg_ref[...]).astype(BF16)
    q = _dot(cqn, wuq_ref[...])
    q = _rope(q, _lane_tile(cq, 8), _lane_tile(sq, 8), MLA_ROPE // 4)
    qm_ref[...] = (q * (LOG2E * (MLA_NOPE + MLA_ROPE) ** -0.5)).astype(BF16)
    ckvn = _rms(z[:, 256:384], kvg_ref[...]).astype(BF16)
    kn = _dot(ckvn, wukk_ref[...])
    kr = _rope(z[:, 384:512], cq, sq, MLA_ROPE // 4)
    km_ref[...] = (kn + _lane_tile(kr, 8)).astype(BF16)
    vm_ref[...] = _dot_nt(wukv_ref[...], ckvn).astype(BF16)
    qw = _rope(z[:, 512:1024], _lane_tile(cw, 4), _lane_tile(sw, 4), WIN_HEAD_DIM // 4)
    qw_ref[...] = (qw * (WIN_HEAD_DIM ** -0.5)).astype(BF16)
    kw = _rope(z[:, 1024:1280], _lane_tile(cw, 2), _lane_tile(sw, 2), WIN_HEAD_DIM // 4)
    kw_ref[...] = kw.astype(BF16)
    vw_ref[...] = z[:, 1280:1536].astype(BF16)


def _even_in(x, mod, mod_row, gn, wts, tabs, tm):
    b, n, d = x.shape
    win, qg, wuq, kvg, wukk, wukv = wts
    nt = n // tm
    row = (lambda bi, i: (bi, 0, 0)) if mod_row is None else (lambda bi, i: (mod_row, 0, 0))
    full = lambda a: pl.BlockSpec(a.shape, lambda bi, i: (0,) * a.ndim)
    tab = pl.BlockSpec((tm, LANES), lambda bi, i: (i, 0))
    outw = (1024, 1024, None, 512, 256, 256)
    rowspec = lambda w: pl.BlockSpec((None, tm, w), lambda bi, i: (bi, i, 0))
    colspec = pl.BlockSpec((None, 512, tm), lambda bi, i: (bi, 0, i))
    return pl.pallas_call(
        _even_in_kernel,
        out_shape=[jax.ShapeDtypeStruct((b, 512, n) if w is None else (b, n, w), BF16) for w in outw],
        grid=(b, nt),
        in_specs=[pl.BlockSpec((None, tm, d), lambda bi, i: (bi, i, 0)),
                  pl.BlockSpec((None, 1, 6 * d), row),
                  full(gn), full(win), full(qg), full(wuq), full(kvg), full(wukk), full(wukv),
                  tab, tab, tab, tab],
        out_specs=[colspec if w is None else rowspec(w) for w in outw],
        compiler_params=_cparams(("parallel", "parallel")),
    )(x, mod, gn, win, qg, wuq, kvg, wukk, wukv, *tabs)


def _mla_attn_kernel(nseg, q_ref, *refs):
    ks, vts = refs[0:2 * nseg:2], refs[1:2 * nseg:2]
    o_ref = refs[2 * nseg]
    s_bufs = refs[2 * nseg + 1:2 * nseg + 3]
    p_bufs = refs[2 * nseg + 3:2 * nseg + 5]
    pieces, base = [], 0
    for k in ks:
        n = k.shape[0]
        pieces += [(k, c0, min(n, c0 + MLA_KEY_CHUNK), base + c0) for c0 in range(0, n, MLA_KEY_CHUNK)]
        base += n

    def score_chunk(h, piece, buf):
        k, c0, c1, g0 = piece
        hs = slice(h * LANES, (h + 1) * LANES)
        s = _dot_nt(k[c0:c1, hs], q_ref[:, hs])
        buf[g0:g0 + c1 - c0, :] = s
        return jnp.max(s, axis=0, keepdims=True)

    def prob_chunk(piece, sbuf, pbuf, m):
        _, c0, c1, g0 = piece
        p = jnp.exp2(sbuf[g0:g0 + c1 - c0, :] - m)
        pbuf[g0:g0 + c1 - c0, :] = p.astype(BF16)
        return jnp.sum(p, axis=0, keepdims=True)

    m_next = functools.reduce(jnp.maximum, [score_chunk(0, pc, s_bufs[0]) for pc in pieces])
    outs = []
    for h in range(MLA_HEADS):
        m_cur, maxes, sums = m_next, [], []
        for pc in pieces:
            if h + 1 < MLA_HEADS:
                maxes.append(score_chunk(h + 1, pc, s_bufs[(h + 1) % 2]))
            sums.append(prob_chunk(pc, s_bufs[h % 2], p_bufs[h % 2], m_cur))
        if h + 1 < MLA_HEADS:
            m_next = functools.reduce(jnp.maximum, maxes)
        l = functools.reduce(jnp.add, sums)
        vrows = slice(h * MLA_V, (h + 1) * MLA_V)
        ot, base = None, 0
        for k, vt in zip(ks, vts):
            n = k.shape[0]
            part = _dot(vt[vrows, :], p_bufs[h % 2][base:base + n, :])
            ot = part if ot is None else ot + part
            base += n
        outs.append(ot * (1.0 / l))
    o_ref[...] = jnp.concatenate(outs, axis=0).T.astype(BF16)


def _mla_attn(q, segs, tq):
    b, n, _ = q.shape
    in_specs = [pl.BlockSpec((None, tq, 1024), lambda bi, i: (bi, i, 0))]
    args = [q]
    keys = 0
    for k, vt in segs:
        lk = k.shape[1]
        keys += lk
        in_specs += [pl.BlockSpec((None, lk, 1024), lambda bi, i: (bi, 0, 0)),
                     pl.BlockSpec((None, 512, lk), lambda bi, i: (bi, 0, 0))]
        args += [k, vt]
    return pl.pallas_call(
        functools.partial(_mla_attn_kernel, len(segs)),
        out_shape=jax.ShapeDtypeStruct((b, n, 512), BF16),
        grid=(b, n // tq),
        in_specs=in_specs,
        out_specs=pl.BlockSpec((None, tq, 512), lambda bi, i: (bi, i, 0)),
        scratch_shapes=[pltpu.VMEM((keys, tq), F32), pltpu.VMEM((keys, tq), F32),
                        pltpu.VMEM((keys, tq), BF16), pltpu.VMEM((keys, tq), BF16)],
        compiler_params=_cparams(("parallel", "parallel")),
    )(*args)


def _gqa_kernel(has_win, nb, sink_ref, q_ref, *refs):
    if has_win:
        kp, kc, kn, vp, vc, vn, kx, vx, o_ref = refs
    else:
        kx, vx, o_ref = refs
    tq = q_ref.shape[0]
    i = pl.program_id(1)
    lane = lax.broadcasted_iota(I32, (tq, LANES), 1)
    row2 = lax.broadcasted_iota(I32, (2 * tq, 1), 0)
    half = WIN_HEAD_DIM
    npair = WIN_HEADS // 2
    kcats, vcats = [], []
    for g in range(WIN_KV_HEADS):
        gs = slice(g * LANES, (g + 1) * LANES)
        if has_win:
            kcats.append(jnp.concatenate([kp[:, gs], kc[:, gs], kn[:, gs], kx[:, gs]], axis=0))
            vcats.append(jnp.concatenate([vp[:, gs], vc[:, gs], vn[:, gs], vx[:, gs]], axis=0))
        else:
            kcats.append(kx[:, gs])
            vcats.append(vx[:, gs])
    if has_win:
        w = WIN_BLOCK
        shape = (2 * tq, kcats[0].shape[0])
        r = lax.broadcasted_iota(I32, shape, 0) & (tq - 1)
        c = lax.broadcasted_iota(I32, shape, 1)
        big = jnp.int32(1 << 20)
        no_prev = jnp.where(i > 0, 0, big)
        no_next = jnp.where(i < nb - 1, 0, big)
        ok_prev = c >= r + no_prev
        ok_next = (c - 2 * w) <= r - no_next
        valid = ((c >= w) | ok_prev) & ((c < 2 * w) | (c >= 3 * w) | ok_next)

    def score(j):
        qp = q_ref[:, j * LANES:(j + 1) * LANES]
        zero = jnp.zeros_like(qp)
        q2 = jnp.concatenate([jnp.where(lane < half, qp, zero), jnp.where(lane >= half, qp, zero)], axis=0)
        return _dot_nt(q2, kcats[j // 2])

    def softmax(j, s):
        s = jnp.where(valid, s, NEG) if has_win else s
        sk = jnp.where(row2 < tq, sink_ref[2 * j], sink_ref[2 * j + 1])
        m = jnp.maximum(jnp.max(s, axis=-1, keepdims=True), sk)
        p = jnp.exp(s - m)
        return p.astype(BF16), 1.0 / (jnp.sum(p, axis=-1, keepdims=True) + jnp.exp(sk - m))

    def values(j, p, inv):
        o2 = _dot(p, vcats[j // 2]) * inv
        o_ref[:, j * LANES:(j + 1) * LANES] = jnp.where(lane < half, o2[:tq], o2[tq:]).astype(BF16)

    s_val, p_val = {}, {}
    for t in range(npair + 2):
        if t < npair:
            s_val[t] = score(t)
        if 0 <= t - 1 < npair:
            p_val[t - 1] = softmax(t - 1, s_val.pop(t - 1))
        if 0 <= t - 2 < npair:
            values(t - 2, *p_val.pop(t - 2))


def _gqa(q, k, v, kx, vx, sink, has_win):
    b, n, _ = q.shape
    lc = kx.shape[1]
    smem = pl.BlockSpec(memory_space=pltpu.SMEM)
    ctxs = pl.BlockSpec((None, lc, 256), lambda bi, i: (bi, 0, 0))
    if has_win:
        tq = WIN_BLOCK
        nb = n // tq
        blk = lambda f: pl.BlockSpec((None, tq, 256), f)
        prev = lambda bi, i: (bi, jnp.maximum(i - 1, 0), 0)
        cur = lambda bi, i: (bi, i, 0)
        nxt = lambda bi, i: (bi, jnp.minimum(i + 1, nb - 1), 0)
        in_specs = [smem, pl.BlockSpec((None, tq, 512), cur),
                    blk(prev), blk(cur), blk(nxt), blk(prev), blk(cur), blk(nxt), ctxs, ctxs]
        args = (sink, q, k, k, k, v, v, v, kx, vx)
    else:
        tq, nb = n, 1
        in_specs = [smem, pl.BlockSpec((None, tq, 512), lambda bi, i: (bi, i, 0)), ctxs, ctxs]
        args = (sink, q, kx, vx)
    return pl.pallas_call(
        functools.partial(_gqa_kernel, has_win, nb),
        out_shape=jax.ShapeDtypeStruct((b, n, 512), BF16),
        grid=(b, nb),
        in_specs=in_specs,
        out_specs=pl.BlockSpec((None, tq, 512), lambda bi, i: (bi, i, 0)),
        compiler_params=_cparams(("parallel", "parallel")),
    )(*args)


def _log_sigmoid(z):
    return jnp.minimum(z, 0.0) - jnp.log(1.0 + jnp.exp(-jnp.abs(z)))


def _odd_in_kernel(x_ref, y0_ref, y1_ref, wt_ref, modp_ref, mod_ref, gn_ref, win_ref, wg_ref, bg_ref, lng_ref,
                   lnb_ref, ws_ref, bst_ref, xn_ref, q_ref, k_ref, v_ref, la_ref, r_ref, dl_ref):
    d = D_MODEL
    tm = x_ref.shape[0]
    wt = wt_ref[...]
    y = wt[:, 0:1] * _unpack_bf16_pairs(y0_ref[...]) + wt[:, 1:2] * _unpack_bf16_pairs(y1_ref[...])
    x = x_ref[...] + modp_ref[:, 5 * d:6 * d] * y
    xn_ref[...] = x
    mod = mod_ref[...]
    h = (_rms(x, gn_ref[...]) * (1.0 + mod[:, d:2 * d]) + mod[:, 0:d]).astype(BF16)
    z = _dot(h, win_ref[...])
    q_ref[...] = z[:, 0:256] * (GLA_DK ** -0.5)
    k_ref[...] = z[:, 256:512]
    v_ref[...] = z[:, 512:1024].astype(BF16)
    g_hi, g_lo = _split2(z[:, 1024:1152])
    w_hi, w_lo = _split2(wg_ref[...])
    zg = _dot(g_hi, w_hi) + _dot(g_lo, w_hi) + _dot(g_hi, w_lo) + bg_ref[...]
    la_ref[...] = _log_sigmoid(zg) / GLA_TAU
    r_ref[...] = z[:, 1152:1664]
    u = jax.nn.gelu(z[:, 1664:2176])
    vg = jax.nn.gelu(z[:, 2176:2688])
    mu = jnp.mean(vg, axis=-1, keepdims=True)
    vc = vg - mu
    var = jnp.mean(vc * vc, axis=-1, keepdims=True)
    vn = (vc * lax.rsqrt(var + EPS) * lng_ref[...] + lnb_ref[...]).astype(BF16)
    bst = bst_ref[...]
    for c in range(tm // SG_CHUNK):
        rows = slice(c * SG_CHUNK, (c + 1) * SG_CHUNK)
        parts = []
        for g in range(SG_GROUPS):
            cols = slice(g * LANES, (g + 1) * LANES)
            parts.append(_dot(ws_ref[g], vn[rows, cols]) + bst[:, g:g + 1])
        dl_ref[rows, :] = (u[rows, :] * jnp.concatenate(parts, axis=1)).astype(BF16)


def _odd_in(x, pending, tile0, modp, mod, mod_row, gn, wts):
    b, n, d = x.shape
    tm = MOE_TILE
    yt, wt = pending
    ntiles = wt.shape[0] // tm
    win, wg, bg, lng, lnb, ws, bst = wts
    row = (lambda bi, i: (bi, 0, 0)) if mod_row is None else (lambda bi, i: (mod_row, 0, 0))
    full = lambda a: pl.BlockSpec(a.shape, lambda bi, i: (0,) * a.ndim)
    act = lambda wd: pl.BlockSpec((None, tm, wd), lambda bi, i: (bi, i, 0))
    tok = lambda bi, i: tile0 + bi * (n // tm) + i
    outs = [((b, n, d), F32, act(d)),
            ((b, n, 256), F32, act(256)), ((b, n, 256), F32, act(256)), ((b, n, 512), BF16, act(512)),
            ((b, n, 512), F32, act(512)), ((b, n, 512), F32, act(512)), ((b, n, 512), BF16, act(512))]
    return pl.pallas_call(
        _odd_in_kernel,
        out_shape=[jax.ShapeDtypeStruct(s, t) for s, t, _ in outs],
        grid=(b, n // tm),
        in_specs=[act(d), pl.BlockSpec((tm, d // 2), lambda bi, i: (tok(bi, i), 0)),
                  pl.BlockSpec((tm, d // 2), lambda bi, i: (ntiles + tok(bi, i), 0)),
                  pl.BlockSpec((tm, 8), lambda bi, i: (tok(bi, i), 0)),
                  pl.BlockSpec((None, 1, 6 * d), row), pl.BlockSpec((None, 1, 6 * d), row),
                  full(gn), full(win), full(wg), full(bg), full(lng), full(lnb), full(ws), full(bst)],
        out_specs=[sp for _, _, sp in outs],
        compiler_params=_cparams(("parallel", "parallel")),
    )(x, yt, yt, wt, modp, mod, gn, win, wg, bg, lng, lnb, ws, bst)


def _gla_tables():
    c = GLA_BLOCK
    t = np.arange(c)[:, None]
    u = np.arange(c)[None, :]
    levels = [c >> i for i in range(int(np.log2(c)) + 1)]
    cum = np.zeros((2, 2 * len(levels), c, c), np.float32)
    pair = np.zeros((2, len(levels), c, c), np.float32)
    for li, m in enumerate(levels):
        same = (t // m) == (u // m)
        cum[0, 2 * li] = same & (u <= t)
        cum[0, 2 * li + 1] = same & (u > t)
        cum[1, 2 * li] = same & (u >= t)
        cum[1, 2 * li + 1] = same & (u < t)
        if li > 0:
            pair[0, li] = ((t // m) % 2 == 1) & ((u // m) == (t // m) - 1)
            pair[1, li] = ((t // m) % 2 == 0) & ((u // m) == (t // m) + 1)
    pair[:, 0] = np.eye(c, dtype=np.float32)
    nlev = len(levels)
    m1 = cum[:, 0::2].reshape(2, nlev * c, c)
    m2t = np.concatenate([cum[:, 2 * li + 1].transpose(0, 2, 1) for li in range(nlev)], axis=2)
    return np.concatenate([m1, m1], axis=2), np.concatenate([m2t, m2t], axis=1), pair, nlev


def _gla_chain(nlev, q, k, la, v_ref, cumq, cumkt, pm_ref, st_ref, o_ref):
    c = GLA_BLOCK
    lat, kt = la.T, k.T
    l_hi, l_mid = _split2(la)
    t_hi, t_mid = _split2(lat)
    exq = jnp.exp(_dot(cumq, jnp.concatenate([l_hi, l_mid], axis=0)))
    exk = jnp.exp(_dot(jnp.concatenate([t_hi, t_mid], axis=1), cumkt))
    gcol = jnp.exp(jnp.sum(lat, axis=1, keepdims=True))
    yield
    qe, ke = [], []
    for li in range(nlev):
        qe.append((q * exq[li * c:(li + 1) * c]).astype(BF16))
        ke.append((kt * exk[:, li * c:(li + 1) * c]).astype(BF16))
        if li % 2 == 1:
            yield
    qb, kb = q.astype(BF16), kt.astype(BF16)
    states = [st_ref[hd] for hd in range(GLA_HEADS)]
    yield
    outs, new_states = [], []
    lane = lax.broadcasted_iota(I32, (c, LANES), 1)
    srow = lax.broadcasted_iota(I32, (LANES, 1), 0)
    zero = jnp.zeros((c, LANES), BF16)
    for hd in range(GLA_HEADS):
        ps = slice((hd // 2) * LANES, (hd // 2 + 1) * LANES)
        vs = slice(hd * GLA_DV, (hd + 1) * GLA_DV)
        mine = (lane < GLA_DK) if hd % 2 == 0 else (lane >= GLA_DK)
        mine_row = (srow < GLA_DK) if hd % 2 == 0 else (srow >= GLA_DK)
        pick = lambda t: jnp.where(mine, t[:, ps], zero)
        a = pm_ref[0] * _dot(pick(qb), kb[ps, :])
        for li in range(1, nlev):
            a = a + pm_ref[li] * _dot(pick(qe[li]), ke[li][ps, :])
            if li % GLA_LEVELS_PER_PHASE == 0:
                yield
        v_h = v_ref[:, vs]
        outs.append(_dot(qe[0][:, ps], states[hd].astype(BF16)) + _dot(a.astype(BF16), v_h))
        new_states.append(states[hd] * gcol[ps, :] + jnp.where(mine_row, _dot(ke[0][ps, :], v_h), 0.0))
        yield
    o_ref[...] = jnp.concatenate(outs, axis=1)
    for hd in range(GLA_HEADS):
        st_ref[hd] = new_states[hd]
    yield


def _gla_kernel(nlev, nb, *refs):
    ins_f, ins_b = refs[0:4], refs[4:8]
    cumq_ref, cumkt_ref, pm_ref, s0_ref, of_ref, ob_ref, sf_ref = refs[8:15]
    st_refs = refs[15:]
    step = pl.program_id(1)

    @pl.when(step == 0)
    def _():
        for bb in range(nb):
            for d_ in range(2):
                st_refs[2 * bb + d_][...] = s0_ref[bb, d_]

    chains = []
    for bb in range(nb):
        for d_, (ins, o_ref) in enumerate(((ins_f, of_ref), (ins_b, ob_ref))):
            q_ref, k_ref, v_ref, la_ref = ins
            chains.append(_gla_chain(nlev, q_ref[bb], k_ref[bb], la_ref[bb], v_ref.at[bb], cumq_ref[d_],
                                     cumkt_ref[d_], pm_ref.at[d_], st_refs[2 * bb + d_], o_ref.at[bb]))
    while chains:
        chains = [ch for ch in chains if next(ch, "done") != "done"]
    for bb in range(nb):
        for d_ in range(2):
            sf_ref[bb, d_] = st_refs[2 * bb + d_][...]


def _gla(q, k, v, la, s0, cumq, cumkt, pm, nlev):
    b, n, _ = q.shape
    c = GLA_BLOCK
    nc = n // c
    nb = next(c for c in (GLA_BATCHES_PER_STEP, 2, 1) if b % c == 0)
    specs = []
    for d_ in range(2):
        pos = (lambda s_: s_) if d_ == 0 else (lambda s_: nc - 1 - s_)
        specs += [pl.BlockSpec((nb, c, 256), lambda bi, s_, pos=pos: (bi, pos(s_), 0)),
                  pl.BlockSpec((nb, c, 256), lambda bi, s_, pos=pos: (bi, pos(s_), 0)),
                  pl.BlockSpec((nb, c, 512), lambda bi, s_, pos=pos: (bi, pos(s_), 0)),
                  pl.BlockSpec((nb, c, 256), lambda bi, s_, pos=pos, d_=d_: (bi, pos(s_), d_))]
    st_spec = pl.BlockSpec((nb, 2, GLA_HEADS, GLA_DV, LANES), lambda bi, s_: (bi, 0, 0, 0, 0))
    full = lambda a: pl.BlockSpec(a.shape, lambda bi, s_: (0,) * a.ndim)
    return pl.pallas_call(
        functools.partial(_gla_kernel, nlev, nb),
        out_shape=[jax.ShapeDtypeStruct((b, n, GLA_V), F32), jax.ShapeDtypeStruct((b, n, GLA_V), F32),
                   jax.ShapeDtypeStruct((b, 2, GLA_HEADS, GLA_DV, LANES), F32)],
        grid=(b // nb, nc),
        in_specs=specs + [full(cumq), full(cumkt), full(pm), st_spec],
        out_specs=[pl.BlockSpec((nb, c, GLA_V), lambda bi, s_: (bi, s_, 0)),
                   pl.BlockSpec((nb, c, GLA_V), lambda bi, s_: (bi, nc - 1 - s_, 0)), st_spec],
        scratch_shapes=[pltpu.VMEM((GLA_HEADS, GLA_DV, LANES), F32) for _ in range(2 * nb)],
        compiler_params=_cparams(("parallel", "arbitrary")),
    )(q, k, v, la, q, k, v, la, cumq, cumkt, pm, s0)


def _odd_out_router_kernel(x_ref, of_ref, ob_ref, r_ref, dl_ref, gg_ref, w_ref, mod_ref, gn_ref, wr_ref, u_ref,
                           o_ref, *route_refs):
    d = D_MODEL
    o = of_ref[...] + ob_ref[...]
    gg = gg_ref[...]
    r = r_ref[...]
    parts = []
    for hd in range(GLA_HEADS):
        vs = slice(hd * GLA_DV, (hd + 1) * GLA_DV)
        oh = o[:, vs]
        parts.append(oh * lax.rsqrt(jnp.mean(oh * oh, axis=-1, keepdims=True) + EPS) * gg[:, vs])
    cl = (jnp.concatenate(parts, axis=1) * (r * jax.nn.sigmoid(r))).astype(BF16)
    y = _dot(cl, w_ref[0:512, :]) + _dot(dl_ref[...], w_ref[512:1024, :])
    x = x_ref[...] + mod_ref[:, 2 * d:3 * d] * y
    o_ref[...] = x
    _route_tile(x, mod_ref, gn_ref, wr_ref, u_ref, *route_refs)


def _odd_out_router(x, o_fwd, o_bwd, r, dl, gg, w, mod, gn, wr, u):
    b, n, d = x.shape
    tm = MOE_TILE
    flat = lambda t: t.reshape(-1, t.shape[-1])
    act = lambda wd: pl.BlockSpec((tm, wd), lambda i: (i, 0))
    full = lambda t: pl.BlockSpec(t.shape, lambda i: (0,) * t.ndim)
    rshapes, rspecs, rscratch = _route_out(b * n, d)
    outs = pl.pallas_call(
        _odd_out_router_kernel,
        out_shape=[jax.ShapeDtypeStruct((b * n, d), F32)] + rshapes,
        grid=(b * n // tm,),
        in_specs=[act(d), act(GLA_V), act(GLA_V), act(512), act(512), full(gg), full(w),
                  pl.BlockSpec((None, 1, 6 * d), lambda i: (i // (n // tm), 0, 0)), full(gn), full(wr), full(u)],
        out_specs=[act(d)] + rspecs,
        scratch_shapes=rscratch,
        compiler_params=_cparams(("arbitrary",)),
    )(flat(x), flat(o_fwd), flat(o_bwd), flat(r), flat(dl), gg, w, mod, gn, wr, u)
    return outs[0].reshape(x.shape), outs[1:]


def _route_tile(x, mod_ref, gn_ref, wr_ref, u_ref, h_ref, e_ref, wt_ref, r_ref, cnt_ref, carry_ref):
    d = D_MODEL
    tm = x.shape[0]

    @pl.when(pl.program_id(0) == 0)
    def _():
        carry_ref[...] = jnp.zeros_like(carry_ref)

    mod = mod_ref[...]
    h = _rms(x, gn_ref[...]) * (1.0 + mod[:, 4 * d:5 * d]) + mod[:, 3 * d:4 * d]
    h_ref[...] = _pack_bf16_pairs(h)
    h_hi, h_lo = _split2(h)
    w_hi, w_lo = _split2(wr_ref[...])
    lg = _dot_nt(w_hi, h_hi) + _dot_nt(w_lo, h_hi) + _dot_nt(w_hi, h_lo)
    rid = lax.broadcasted_iota(I32, (8, tm), 0)
    gl = jnp.where(rid < MOE_GROUPS, lg[0:8], NEG)
    gmax = jnp.max(gl, axis=0, keepdims=True)
    gsel = jnp.min(jnp.where(gl == gmax, rid, 8), axis=0, keepdims=True)
    pmax = 1.0 / jnp.sum(jnp.where(rid < MOE_GROUPS, jnp.exp(gl - gmax), 0.0), axis=0, keepdims=True)
    e_in = jnp.zeros((MOE_PER_GROUP, tm), F32)
    for g in range(MOE_GROUPS):
        e_in = e_in + jnp.where(gsel == g, lg[8 + 8 * g:16 + 8 * g], 0.0)
    v1 = jnp.max(e_in, axis=0, keepdims=True)
    i1 = jnp.min(jnp.where(e_in == v1, rid, 8), axis=0, keepdims=True)
    e_rest = jnp.where(rid == i1, -jnp.inf, e_in)
    v2 = jnp.max(e_rest, axis=0, keepdims=True)
    i2 = jnp.min(jnp.where(e_rest == v2, rid, 8), axis=0, keepdims=True)
    t = jnp.exp(v2 - v1)
    w1 = pmax / (1.0 + t)
    w2 = pmax * t / (1.0 + t)
    e1 = gsel * MOE_PER_GROUP + i1
    e2 = gsel * MOE_PER_GROUP + i2
    eid = lax.broadcasted_iota(I32, (MOE_EXPERTS, tm), 0)
    oh1 = jnp.where(eid == e1, 1.0, 0.0)
    oh2 = jnp.where(eid == e2, 1.0, 0.0)
    ohs = oh1 + oh2
    base = carry_ref[:, 0:1] + _dot(ohs.astype(BF16), u_ref[...])
    r1 = jnp.sum(oh1 * base, axis=0, keepdims=True)
    r2 = jnp.sum(oh2 * base, axis=0, keepdims=True)
    carry_ref[...] = carry_ref[...] + jnp.sum(ohs, axis=1, keepdims=True)
    cnt_ref[...] = carry_ref[...]
    e_ref[...] = jnp.concatenate([e1, e2], axis=0)
    r_ref[...] = jnp.concatenate([r1, r2], axis=0).astype(I32)
    w8 = jnp.concatenate([w1, w2, jnp.zeros((6, tm), F32)], axis=0)
    wt_ref[...] = w8.T


def _route_out(n, d):
    tm = MOE_TILE
    shapes = [jax.ShapeDtypeStruct((n, d // 2), I32), jax.ShapeDtypeStruct((2, n), I32),
              jax.ShapeDtypeStruct((n, 8), F32), jax.ShapeDtypeStruct((2, n), I32),
              jax.ShapeDtypeStruct((MOE_EXPERTS, LANES), F32)]
    specs = [pl.BlockSpec((tm, d // 2), lambda i: (i, 0)), pl.BlockSpec((2, tm), lambda i: (0, i)),
             pl.BlockSpec((tm, 8), lambda i: (i, 0)), pl.BlockSpec((2, tm), lambda i: (0, i)),
             pl.BlockSpec((MOE_EXPERTS, LANES), lambda i: (0, 0))]
    return shapes, specs, [pltpu.VMEM((MOE_EXPERTS, LANES), F32)]


def _even_out_router_kernel(nlat, x_ref, xc_ref, a_ref, ac_ref, b_ref, bc_ref, w_ref, mod_ref, gn_ref, wr_ref, u_ref,
                            xo_ref, xoc_ref, *route_refs):
    d = D_MODEL
    lat = pl.program_id(0) < nlat
    a = jnp.where(lat, a_ref[...], ac_ref[...])
    b = jnp.where(lat, b_ref[...], bc_ref[...])
    y = _dot(a, w_ref[0:512, :]) + _dot(b, w_ref[512:1024, :])
    x = jnp.where(lat, x_ref[...], xc_ref[...]) + mod_ref[:, 2 * d:3 * d] * y

    @pl.when(lat)
    def _():
        xo_ref[...] = x

    @pl.when(jnp.logical_not(lat))
    def _():
        xoc_ref[...] = x

    _route_tile(x, mod_ref, gn_ref, wr_ref, u_ref, *route_refs)


def _even_out_router(x, xc, oa, oac, ob, obc, w, mod, ctx_row, gn, wr, u):
    b, n, d = x.shape
    tm = MOE_TILE
    flat = lambda t: t.reshape(-1, t.shape[-1])
    nlat = b * n // tm
    ntok = b * n + xc.shape[0] * xc.shape[1]
    lat = lambda wd: pl.BlockSpec((tm, wd), lambda i: (jnp.minimum(i, nlat - 1), 0))
    ctx = lambda wd: pl.BlockSpec((tm, wd), lambda i: (jnp.maximum(i - nlat, 0), 0))
    full = lambda t: pl.BlockSpec(t.shape, lambda i: (0,) * t.ndim)
    modrow = lambda i: (jnp.where(i < nlat, i // (n // tm), ctx_row), 0, 0)
    rshapes, rspecs, rscratch = _route_out(ntok, d)
    outs = pl.pallas_call(
        functools.partial(_even_out_router_kernel, nlat),
        out_shape=[jax.ShapeDtypeStruct((b * n, d), F32), jax.ShapeDtypeStruct((ntok - b * n, d), F32)] + rshapes,
        grid=(ntok // tm,),
        in_specs=[lat(d), ctx(d), lat(512), ctx(512), lat(512), ctx(512), full(w),
                  pl.BlockSpec((None, 1, 6 * d), modrow), full(gn), full(wr), full(u)],
        out_specs=[lat(d), ctx(d)] + rspecs,
        scratch_shapes=rscratch,
        compiler_params=_cparams(("arbitrary",)),
    )(flat(x), flat(xc), flat(oa), flat(oac), flat(ob), flat(obc), w, mod, gn, wr, u)
    return outs[0].reshape(x.shape), outs[1].reshape(xc.shape), outs[2:]


def _sc_permute_rows(src, dest, scatter):
    rows, d = dest.shape[0], src.shape[1]
    n = rows // 2
    info = plsc.get_sparse_core_info()
    workers = info.num_cores * info.num_subcores
    per_worker = rows // workers
    chunk = next(c for c in SC_CHUNKS if per_worker % c == 0)
    assert rows == per_worker * workers and n % per_worker == 0
    mesh = plsc.VectorSubcoreMesh(core_axis_name="c", subcore_axis_name="s")

    def body(src_hbm, dest_hbm, out_hbm, idx_v, rows_v, sem):
        base = (lax.axis_index("s") * info.num_cores + lax.axis_index("c")) * per_worker

        @pl.loop(0, per_worker // chunk)
        def _(j):
            a0 = base + j * chunk
            pltpu.sync_copy(dest_hbm.at[pl.ds(a0, chunk)], idx_v)
            if scatter:
                t0 = jnp.where(a0 >= n, a0 - n, a0)
                pltpu.sync_copy(src_hbm.at[pl.ds(t0, chunk)], rows_v)
                pltpu.async_copy(rows_v, out_hbm.at[idx_v], sem).wait()
            else:
                pltpu.async_copy(src_hbm.at[idx_v], rows_v, sem).wait()
                pltpu.sync_copy(rows_v, out_hbm.at[pl.ds(a0, chunk)])

    return pl.kernel(
        body, out_type=jax.ShapeDtypeStruct((rows, d), src.dtype), mesh=mesh,
        scratch_types=[pltpu.VMEM((chunk,), I32), pltpu.VMEM((chunk, d), src.dtype), pltpu.SemaphoreType.DMA],
    )(src, dest)


def _gmm_kernel(layer, vt_ref, ve_ref, vlo_ref, vhi_ref, vfirst_ref, vslot_ref, vnext_ref, nv_ref,
                xs_ref, wg_hbm, wu_hbm, wd_hbm, ys_ref, wgs_ref, wus_ref, wds_ref, wgb_ref, wub_ref, wdb_ref, sem):
    del vt_ref
    v = pl.program_id(0)

    def fetch(e, slot):
        return [pltpu.make_async_copy(w.at[layer, e], s.at[slot], sem.at[slot, i])
                for i, (w, s) in enumerate(((wg_hbm, wgs_ref), (wu_hbm, wus_ref), (wd_hbm, wds_ref)))]

    @pl.when(v < nv_ref[0])
    def _():
        @pl.when((v == 0) | (ve_ref[v] != ve_ref[jnp.maximum(v - 1, 0)]))
        def _():
            slot = vslot_ref[v]

            @pl.when(v == 0)
            def _():
                for c in fetch(ve_ref[0], 0):
                    c.start()

            for c in fetch(ve_ref[v], slot):
                c.wait()
            wgb_ref[...] = wgs_ref[slot].astype(BF16)
            wub_ref[...] = wus_ref[slot].astype(BF16)
            wdb_ref[...] = wds_ref[slot].astype(BF16)

            @pl.when(vnext_ref[v] >= 0)
            def _():
                for c in fetch(vnext_ref[v], 1 - slot):
                    c.start()

        x = _unpack_bf16_pairs(xs_ref[...]).astype(BF16)
        g = _dot(x, wgb_ref[...])
        u = _dot(x, wub_ref[...])
        y = _pack_bf16_pairs(_dot((g * jax.nn.sigmoid(g) * u).astype(BF16), wdb_ref[...]))
        row = lax.broadcasted_iota(I32, (y.shape[0], 1), 0)
        mine = (row >= vlo_ref[v]) & (row < vhi_ref[v])

        @pl.when(vfirst_ref[v] == 1)
        def _():
            ys_ref[...] = jnp.where(mine, y, 0)

        @pl.when(vfirst_ref[v] == 0)
        def _():
            ys_ref[...] = jnp.where(mine, y, ys_ref[...])


def _gmm(xs, visits, layer, wg, wu, wd):
    rows, dw = xs.shape
    d = 2 * dw
    tm = MOE_TILE
    hid = wg.shape[-1]
    nvis = rows // tm + MOE_EXPERTS - 1
    tile = lambda v, vt, *_: (vt[v], 0)
    anyspec = pl.BlockSpec(memory_space=pl.ANY)
    return pl.pallas_call(
        functools.partial(_gmm_kernel, layer),
        out_shape=jax.ShapeDtypeStruct((rows, dw), I32),
        grid_spec=pltpu.PrefetchScalarGridSpec(
            num_scalar_prefetch=8, grid=(nvis,),
            in_specs=[pl.BlockSpec((tm, dw), tile), anyspec, anyspec, anyspec],
            out_specs=pl.BlockSpec((tm, dw), tile),
            scratch_shapes=[pltpu.VMEM((2, d, hid), F32), pltpu.VMEM((2, d, hid), F32), pltpu.VMEM((2, hid, d), F32),
                            pltpu.VMEM((d, hid), BF16), pltpu.VMEM((d, hid), BF16), pltpu.VMEM((hid, d), BF16),
                            pltpu.SemaphoreType.DMA((2, 3))]),
        compiler_params=_cparams(("arbitrary",)),
    )(*visits, xs, wg, wu, wd)


def _combine_kernel(x_ref, wt_ref, mod_ref, fg_ref, y0_ref, y1_ref, o_ref):
    d = D_MODEL
    wt = wt_ref[...]
    y = wt[:, 0:1] * _unpack_bf16_pairs(y0_ref[...]) + wt[:, 1:2] * _unpack_bf16_pairs(y1_ref[...])
    o_ref[...] = _rms(x_ref[...] + mod_ref[:, 5 * d:6 * d] * y, fg_ref[...])


def _combine(x2, wt, mod, rows_per_mod, fg, yt, tm=COMBINE_TILE):
    n, d = x2.shape
    tm = min(tm, rows_per_mod)
    assert rows_per_mod % tm == 0
    ntiles = n // tm
    return pl.pallas_call(
        _combine_kernel,
        out_shape=jax.ShapeDtypeStruct((n, d), F32),
        grid=(ntiles,),
        in_specs=[pl.BlockSpec((tm, d), lambda i: (i, 0)), pl.BlockSpec((tm, 8), lambda i: (i, 0)),
                  pl.BlockSpec((None, 1, 6 * d), lambda i: (i // (rows_per_mod // tm), 0, 0)),
                  pl.BlockSpec(fg.shape, lambda i: (0, 0)),
                  pl.BlockSpec((tm, d // 2), lambda i: (i, 0)), pl.BlockSpec((tm, d // 2), lambda i: (ntiles + i, 0))],
        out_specs=pl.BlockSpec((tm, d), lambda i: (i, 0)),
        compiler_params=_cparams(("parallel",)),
    )(x2, wt, mod, fg, yt, yt)


def _pick(table, idx):
    hot = idx[..., None] == jnp.arange(table.shape[0], dtype=I32)
    return jnp.sum(jnp.where(hot, table, 0), axis=-1)


def _moe_plan(counts, e, r, rows):
    tm = MOE_TILE
    ends = jnp.cumsum(counts)
    starts = ends - counts
    dest = (_pick(starts, e) + r).reshape(-1)
    first_tile = starts // tm
    nvis = jnp.where(counts > 0, (ends - 1) // tm - first_tile + 1, 0)
    vend = jnp.cumsum(nvis)
    nv = vend[-1:]
    v = jnp.minimum(jnp.arange(rows // tm + MOE_EXPERTS - 1, dtype=I32), nv[0] - 1)
    ve = jnp.sum((vend[None, :] <= v[:, None]).astype(I32), axis=1)
    vt = _pick(first_tile, ve) + v - _pick(vend - nvis, ve)
    vlo = jnp.maximum(_pick(starts, ve) - vt * tm, 0)
    vhi = jnp.minimum(_pick(ends, ve) - vt * tm, tm)
    vfirst = jnp.concatenate([jnp.ones((1,), I32), (vt[1:] != vt[:-1]).astype(I32)])
    changed = jnp.concatenate([jnp.ones((1,), I32), (ve[1:] != ve[:-1]).astype(I32)])
    vslot = (jnp.cumsum(changed) - 1) % 2
    eid = jnp.arange(MOE_EXPERTS, dtype=I32)
    later = (eid[None, :] > eid[:, None]) & (counts[None, :] > 0)
    nxt = jnp.min(jnp.where(later, eid[None, :], MOE_EXPERTS), axis=1)
    vnext = _pick(jnp.where(nxt < MOE_EXPERTS, nxt, -1), ve)
    return dest, (vt, ve, vlo, vhi, vfirst, vslot.astype(I32), vnext.astype(I32), nv)


def _moe_experts(routed, layer, wg, wu, wd):
    h, e, wt, r, cnt = routed
    dest, visits = _moe_plan(cnt[:, 0].astype(I32), e, r, 2 * h.shape[0])
    xs = _sc_permute_rows(h, dest, scatter=True)
    ys = _gmm(xs, visits, layer, wg, wu, wd)
    return _sc_permute_rows(ys, dest, scatter=False), wt


def _rope_tables(rows, dim):
    row = jnp.repeat(jnp.arange(rows, dtype=F32), GRID_W)
    col = jnp.tile(jnp.arange(GRID_W, dtype=F32), rows)
    half = dim // 2
    inv = jnp.power(ROPE_BASE, -jnp.arange(0, half, 2, dtype=F32) / half)
    ar = row[:, None] * inv[None, :]
    ac = col[:, None] * inv[None, :]
    ang = jnp.concatenate([ar, ar, ac, ac], axis=-1)
    return jnp.cos(ang), jnp.sin(ang)


def _even_tables(n, with_rope):
    if with_rope:
        cm, sm = _rope_tables(n // GRID_W, MLA_ROPE)
        cwin, swin = _rope_tables(n // GRID_W, WIN_HEAD_DIM)
    else:
        cm, sm = jnp.ones((n, MLA_ROPE), F32), jnp.zeros((n, MLA_ROPE), F32)
        cwin, swin = jnp.ones((n, WIN_HEAD_DIM), F32), jnp.zeros((n, WIN_HEAD_DIM), F32)
    one, zero = jnp.ones((n, MLA_NOPE), F32), jnp.zeros((n, MLA_NOPE), F32)
    pad = jnp.zeros((n, LANES - MLA_NOPE - MLA_ROPE), F32)
    return (jnp.concatenate([one, cm, pad], axis=1), jnp.concatenate([zero, sm, pad], axis=1),
            jnp.concatenate([cwin, cwin], axis=1), jnp.concatenate([swin, swin], axis=1))


def _even_weights(w_in, qg, w_uq, kvg, w_ukv):
    d = w_in.shape[0]
    o = np.cumsum([0, MLA_Q_RANK, MLA_KV_RANK, MLA_ROPE, 512, 128, 128])
    cq, ckv, kr, qw, kw, vw = [w_in[:, o[i]:o[i + 1]] for i in range(6)]
    z = lambda c: jnp.zeros((d, c), F32)
    kr128 = jnp.concatenate([z(MLA_NOPE), kr, z(LANES - MLA_NOPE - MLA_ROPE)], axis=1)
    dup = lambda t: jnp.concatenate([t[:, 0:64], t[:, 0:64], t[:, 64:128], t[:, 64:128]], axis=1)
    win = jnp.concatenate([cq, ckv, kr128, qw, dup(kw), dup(vw)], axis=1).astype(BF16)
    uq = w_uq.reshape(MLA_Q_RANK, MLA_HEADS, MLA_NOPE + MLA_ROPE)
    uq = jnp.pad(uq, ((0, 0), (0, 0), (0, LANES - MLA_NOPE - MLA_ROPE))).reshape(MLA_Q_RANK, MLA_HEADS * LANES)
    ukv = w_ukv.reshape(MLA_KV_RANK, MLA_HEADS, MLA_NOPE + MLA_V)
    ukk = jnp.pad(ukv[:, :, :MLA_NOPE], ((0, 0), (0, 0), (0, LANES - MLA_NOPE))).reshape(MLA_KV_RANK, MLA_HEADS * LANES)
    ukvv = ukv[:, :, MLA_NOPE:].reshape(MLA_KV_RANK, MLA_HEADS * MLA_V)
    return (win, qg.reshape(1, -1), uq.astype(BF16), kvg.reshape(1, -1), ukk.astype(BF16), ukvv.T.astype(BF16))


def _odd_weights(w_in, w_g2, b_g, ln_g, ln_b, w_s, b_s):
    d = w_in.shape[0]
    o = np.cumsum([0, GLA_K, GLA_K, GLA_V, 2 * GLA_GATE_RANK, GLA_V, SG_WIDTH, SG_WIDTH])
    q, k, v, g, r, u, vg = [w_in[:, o[i]:o[i + 1]] for i in range(7)]
    g128 = jnp.concatenate([g, jnp.zeros((d, LANES - 2 * GLA_GATE_RANK), F32)], axis=1)
    win = jnp.concatenate([q, k, v, g128, r, u, vg], axis=1).astype(BF16)
    zr = jnp.zeros((GLA_GATE_RANK, GLA_K), F32)
    pad = jnp.zeros((LANES - 2 * GLA_GATE_RANK, GLA_K), F32)
    wg = jnp.concatenate([jnp.concatenate([w_g2[0], zr, pad], axis=0),
                          jnp.concatenate([zr, w_g2[1], pad], axis=0)], axis=1)
    bg = b_g.reshape(1, 2 * GLA_K)
    return (win, wg, bg, ln_g.reshape(1, -1), ln_b.reshape(1, -1), w_s.astype(BF16), b_s.T)


def kernel(x, c, ctx, c_ctx, ada_w, ada_b, norm_mix_g, norm_ffn_g, even_w_in, mla_q_norm_g, mla_w_uq, mla_kv_norm_g, mla_w_ukv, win_sink, even_w_out, odd_w_in, gla_w_g2, gla_b_g, gla_norm_g, sg_ln_g, sg_ln_b, sg_w_s, sg_b_s, odd_w_out, moe_w_rg, moe_w_re, moe_w_gate, moe_w_up, moe_w_down, final_norm_g):
    b, n, d = x.shape
    lc = ctx.shape[1]
    depth = ada_w.shape[0]
    assert depth == 2 and d == D_MODEL and b < 8
    assert n % 512 == 0 and lc % MOE_TILE == 0 and n % GRID_W == 0
    tm = 512 if n % 512 == 0 else 256
    tq = 256

    cond8 = jnp.concatenate([c, c_ctx[None, :], jnp.zeros((8 - b - 1, d), F32)], axis=0)
    mod_all = _adaln(cond8, ada_w, ada_b).reshape(depth, 8, 1, 6 * d)
    ctx_row = b
    u_tri = jnp.asarray(np.triu(np.ones((MOE_TILE, MOE_TILE), np.float32), 1), BF16)
    fg = final_norm_g.reshape(1, d)

    def router_w(layer):
        return jnp.concatenate([moe_w_rg[layer].T, jnp.zeros((8 - MOE_GROUPS, d), F32), moe_w_re[layer].T], axis=0)

    def moe_experts(routed, layer):
        return _moe_experts(routed, layer, moe_w_gate, moe_w_up, moe_w_down)

    mod = mod_all[0]
    gn = norm_mix_g[0].reshape(1, d)
    ew = _even_weights(even_w_in[0], mla_q_norm_g[0], mla_w_uq[0], mla_kv_norm_g[0], mla_w_ukv[0])
    qm_l, km_l, vm_l, qw_l, kw_l, vw_l = _even_in(x, mod, None, gn, ew, _even_tables(n, True), tm)
    qm_c, km_c, vm_c, qw_c, kw_c, vw_c = _even_in(ctx, mod, ctx_row, gn, ew, _even_tables(lc, False), lc)
    w_out = even_w_out[0].astype(BF16)
    sink = win_sink[0]
    oa_l = _mla_attn(qm_l, [(km_l, vm_l), (km_c, vm_c)], tq)
    ob_l = _gqa(qw_l, kw_l, vw_l, kw_c, vw_c, sink, True)
    oa_c = _mla_attn(qm_c, [(km_c, vm_c)], lc)
    ob_c = _gqa(qw_c, None, None, kw_c, vw_c, sink, False)
    xl, xc, routed = _even_out_router(x, ctx, oa_l, oa_c, ob_l, ob_c, w_out, mod, ctx_row,
                                      norm_ffn_g[0].reshape(1, d), router_w(0), u_tri)
    pending = moe_experts(routed, 0)

    mod = mod_all[1]
    gn = norm_mix_g[1].reshape(1, d)
    ow = _odd_weights(odd_w_in[0], gla_w_g2[0], gla_b_g[0], sg_ln_g[0], sg_ln_b[0], sg_w_s[0], sg_b_s[0])
    xl, q_l, k_l, v_l, la_l, r_l, dl_l = _odd_in(xl, pending, 0, mod_all[0], mod, None, gn, ow)
    _, q_c, k_c, v_c, la_c, _, _ = _odd_in(xc, pending, b * n // MOE_TILE, mod_all[0], mod, ctx_row, gn, ow)
    cumq_np, cumkt_np, pm_np, nlev = _gla_tables()
    cumq, cumkt = jnp.asarray(cumq_np, BF16), jnp.asarray(cumkt_np, BF16)
    pm = jnp.asarray(pm_np, F32)
    s0 = jnp.zeros((b, 2, GLA_HEADS, GLA_DV, LANES), F32)
    _, _, s_ctx = _gla(q_c, k_c, v_c, la_c, s0, cumq, cumkt, pm, nlev)
    o_fwd, o_bwd, _ = _gla(q_l, k_l, v_l, la_l, s_ctx, cumq, cumkt, pm, nlev)
    xl, routed = _odd_out_router(xl, o_fwd, o_bwd, r_l, dl_l, gla_norm_g[0].reshape(1, -1),
                                 odd_w_out[0].astype(BF16), mod, norm_ffn_g[1].reshape(1, d), router_w(1), u_tri)
    yt, wt = moe_experts(routed, 1)
    return _combine(xl.reshape(b * n, d), wt, mod, n, fg, yt).reshape(b, n, d)
```

```python
import functools

import numpy as np
import jax
import jax.numpy as jnp
from jax import lax
from jax.experimental import pallas as pl
from jax.experimental.pallas import tpu as pltpu
from jax.experimental.pallas import tpu_sc as plsc

F32 = jnp.float32
BF16 = jnp.bfloat16
I32 = jnp.int32

D_MODEL = 1024
GRID_W = 64
EPS = 1e-6
ROPE_BASE = 10000.0
MLA_HEADS = 8
MLA_Q_RANK = 256
MLA_KV_RANK = 128
MLA_NOPE = 64
MLA_ROPE = 32
MLA_V = 64
WIN_HEADS = 8
WIN_KV_HEADS = 2
WIN_HEAD_DIM = 64
WIN_BLOCK = 128
GLA_HEADS = 4
GLA_DK = 64
GLA_DV = 128
GLA_GATE_RANK = 16
GLA_TAU = 16.0
GLA_K = GLA_HEADS * GLA_DK
GLA_V = GLA_HEADS * GLA_DV
SG_GROUPS = 4
SG_CHUNK = 128
SG_WIDTH = 512
MOE_GROUPS = 4
MOE_PER_GROUP = 8
MOE_EXPERTS = 32
MOE_HIDDEN = 512

LANES = 128
GLA_BLOCK = 128
GLA_LEVELS_PER_PHASE = 2
GLA_BATCHES_PER_STEP = 2
MOE_TILE = 256
ROUTE_STEP = 1024
COMBINE_TILE = 1024
MLA_KEY_CHUNK = 1024
SC_CHUNKS = (128, 64, 32)
NEG = -1e30
LOG2E = 1.4426950408889634
VMEM_LIMIT = 56 * 1024 * 1024


def _cparams(sem):
    return pltpu.CompilerParams(dimension_semantics=sem, vmem_limit_bytes=VMEM_LIMIT)


def _dot(a, b):
    return jnp.dot(a, b, preferred_element_type=F32)


def _dot_nt(a, b):
    return lax.dot_general(a, b, (((1,), (1,)), ((), ())), preferred_element_type=F32)


def _split2(a):
    hi = a.astype(BF16)
    lo = (a - hi.astype(F32)).astype(BF16)
    return hi, lo


def _pack_bf16_pairs(x):
    k = x.shape[1] // 2
    bits = lax.bitcast_convert_type(x.astype(BF16).astype(F32), jnp.uint32)
    return lax.bitcast_convert_type(bits[:, :k] | (bits[:, k:] >> 16), I32)


def _unpack_bf16_pairs(w):
    bits = lax.bitcast_convert_type(w, jnp.uint32)
    hi = lax.bitcast_convert_type(bits & jnp.uint32(0xFFFF0000), F32)
    lo = lax.bitcast_convert_type(bits << 16, F32)
    return jnp.concatenate([hi, lo], axis=1)


def _rms(x, g):
    ms = jnp.mean(x * x, axis=-1, keepdims=True)
    return x * lax.rsqrt(ms + EPS) * g


def _lane_tile(t, reps):
    return t if reps == 1 else jnp.concatenate([t] * reps, axis=1)


def _rope(t, cos, sin, quarter):
    n = t.shape[1]
    lane = lax.broadcasted_iota(I32, t.shape, 1)
    first = (lane & (2 * quarter - 1)) < quarter
    rot = jnp.where(first, -pltpu.roll(t, n - quarter, 1), pltpu.roll(t, quarter, 1))
    return t * cos + rot * sin


def _adaln_kernel(c_ref, w_ref, b_ref, o_ref):
    c = c_ref[...]
    s_hi, s_lo = _split2(c * jax.nn.sigmoid(c))
    w_hi, w_lo = _split2(w_ref[...])
    o_ref[...] = _dot(s_hi, w_hi) + _dot(s_lo, w_hi) + _dot(s_hi, w_lo) + b_ref[...]


def _adaln(cond8, ada_w, ada_b):
    depth, d, n6 = ada_w.shape
    tn = 1536
    return pl.pallas_call(
        _adaln_kernel,
        out_shape=jax.ShapeDtypeStruct((depth, 8, n6), F32),
        grid=(depth, n6 // tn),
        in_specs=[
            pl.BlockSpec((8, d), lambda l, j: (0, 0)),
            pl.BlockSpec((None, d, tn), lambda l, j: (l, 0, j)),
            pl.BlockSpec((None, 1, tn), lambda l, j: (l, 0, j)),
        ],
        out_specs=pl.BlockSpec((None, 8, tn), lambda l, j: (l, 0, j)),
        compiler_params=_cparams(("parallel", "parallel")),
    )(cond8, ada_w, ada_b.reshape(depth, 1, n6))


def _even_in_kernel(x_ref, mod_ref, gn_ref, win_ref, qg_ref, wuq_ref, kvg_ref, wukk_ref, wukv_ref,
                    cq_ref, sq_ref, cw_ref, sw_ref,
                    qm_ref, km_ref, vm_ref, qw_ref, kw_ref, vw_ref):
    d = D_MODEL
    mod = mod_ref[...]
    h = _rms(x_ref[...], gn_ref[...]) * (1.0 + mod[:, d:2 * d]) + mod[:, 0:d]
    z = _dot(h.astype(BF16), win_ref[...])
    cq, sq, cw, sw = cq_ref[...], sq_ref[...], cw_ref[...], sw_ref[...]
    cqn = _rms(z[:, 0:256], qg_ref[...]).astype(BF16)
    q = _dot(cqn, wuq_ref[...])
    q = _rope(q, _lane_tile(cq, 8), _lane_tile(sq, 8), MLA_ROPE // 4)
    qm_ref[...] = (q * (LOG2E * (MLA_NOPE + MLA_ROPE) ** -0.5)).astype(BF16)
    ckvn = _rms(z[:, 256:384], kvg_ref[...]).astype(BF16)
    kn = _dot(ckvn, wukk_ref[...])
    kr = _rope(z[:, 384:512], cq, sq, MLA_ROPE // 4)
    km_ref[...] = (kn + _lane_tile(kr, 8)).astype(BF16)
    vm_ref[...] = _dot_nt(wukv_ref[...], ckvn).astype(BF16)
    qw = _rope(z[:, 512:1024], _lane_tile(cw, 4), _lane_tile(sw, 4), WIN_HEAD_DIM // 4)
    qw_ref[...] = (qw * (WIN_HEAD_DIM ** -0.5)).astype(BF16)
    kw = _rope(z[:, 1024:1280], _lane_tile(cw, 2), _lane_tile(sw, 2), WIN_HEAD_DIM // 4)
    kw_ref[...] = kw.astype(BF16)
    vw_ref[...] = z[:, 1280:1536].astype(BF16)


def _even_in(x, mod, mod_row, gn, wts, tabs, tm):
    b, n, d = x.shape
    win, qg, wuq, kvg, wukk, wukv = wts
    nt = n // tm
    row = (lambda bi, i: (bi, 0, 0)) if mod_row is None else (lambda bi, i: (mod_row, 0, 0))
    full = lambda a: pl.BlockSpec(a.shape, lambda bi, i: (0,) * a.ndim)
    tab = pl.BlockSpec((tm, LANES), lambda bi, i: (i, 0))
    outw = (1024, 1024, None, 512, 256, 256)
    rowspec = lambda w: pl.BlockSpec((None, tm, w), lambda bi, i: (bi, i, 0))
    colspec = pl.BlockSpec((None, 512, tm), lambda bi, i: (bi, 0, i))
    return pl.pallas_call(
        _even_in_kernel,
        out_shape=[jax.ShapeDtypeStruct((b, 512, n) if w is None else (b, n, w), BF16) for w in outw],
        grid=(b, nt),
        in_specs=[pl.BlockSpec((None, tm, d), lambda bi, i: (bi, i, 0)),
                  pl.BlockSpec((None, 1, 6 * d), row),
                  full(gn), full(win), full(qg), full(wuq), full(kvg), full(wukk), full(wukv),
                  tab, tab, tab, tab],
        out_specs=[colspec if w is None else rowspec(w) for w in outw],
        compiler_params=_cparams(("parallel", "parallel")),
    )(x, mod, gn, win, qg, wuq, kvg, wukk, wukv, *tabs)


def _mla_attn_kernel(nseg, q_ref, *refs):
    ks, vts = refs[0:2 * nseg:2], refs[1:2 * nseg:2]
    o_ref = refs[2 * nseg]
    s_bufs = refs[2 * nseg + 1:2 * nseg + 3]
    p_bufs = refs[2 * nseg + 3:2 * nseg + 5]
    pieces, base = [], 0
    for k in ks:
        n = k.shape[0]
        pieces += [(k, c0, min(n, c0 + MLA_KEY_CHUNK), base + c0) for c0 in range(0, n, MLA_KEY_CHUNK)]
        base += n

    def score_chunk(h, piece, buf):
        k, c0, c1, g0 = piece
        hs = slice(h * LANES, (h + 1) * LANES)
        s = _dot_nt(k[c0:c1, hs], q_ref[:, hs])
        buf[g0:g0 + c1 - c0, :] = s
        return jnp.max(s, axis=0, keepdims=True)

    def prob_chunk(piece, sbuf, pbuf, m):
        _, c0, c1, g0 = piece
        p = jnp.exp2(sbuf[g0:g0 + c1 - c0, :] - m)
        pbuf[g0:g0 + c1 - c0, :] = p.astype(BF16)
        return jnp.sum(p, axis=0, keepdims=True)

    m_next = functools.reduce(jnp.maximum, [score_chunk(0, pc, s_bufs[0]) for pc in pieces])
    outs = []
    for h in range(MLA_HEADS):
        m_cur, maxes, sums = m_next, [], []
        for pc in pieces:
            if h + 1 < MLA_HEADS:
                maxes.append(score_chunk(h + 1, pc, s_bufs[(h + 1) % 2]))
            sums.append(prob_chunk(pc, s_bufs[h % 2], p_bufs[h % 2], m_cur))
        if h + 1 < MLA_HEADS:
            m_next = functools.reduce(jnp.maximum, maxes)
        l = functools.reduce(jnp.add, sums)
        vrows = slice(h * MLA_V, (h + 1) * MLA_V)
        ot, base = None, 0
        for k, vt in zip(ks, vts):
            n = k.shape[0]
            part = _dot(vt[vrows, :], p_bufs[h % 2][base:base + n, :])
            ot = part if ot is None else ot + part
            base += n
        outs.append(ot * (1.0 / l))
    o_ref[...] = jnp.concatenate(outs, axis=0).T.astype(BF16)


def _mla_attn(q, segs, tq):
    b, n, _ = q.shape
    in_specs = [pl.BlockSpec((None, tq, 1024), lambda bi, i: (bi, i, 0))]
    args = [q]
    keys = 0
    for k, vt in segs:
        lk = k.shape[1]
        keys += lk
        in_specs += [pl.BlockSpec((None, lk, 1024), lambda bi, i: (bi, 0, 0)),
                     pl.BlockSpec((None, 512, lk), lambda bi, i: (bi, 0, 0))]
        args += [k, vt]
    return pl.pallas_call(
        functools.partial(_mla_attn_kernel, len(segs)),
        out_shape=jax.ShapeDtypeStruct((b, n, 512), BF16),
        grid=(b, n // tq),
        in_specs=in_specs,
        out_specs=pl.BlockSpec((None, tq, 512), lambda bi, i: (bi, i, 0)),
        scratch_shapes=[pltpu.VMEM((keys, tq), F32), pltpu.VMEM((keys, tq), F32),
                        pltpu.VMEM((keys, tq), BF16), pltpu.VMEM((keys, tq), BF16)],
        compiler_params=_cparams(("parallel", "parallel")),
    )(*args)


def _gqa_kernel(has_win, nb, sink_ref, q_ref, *refs):
    if has_win:
        kp, kc, kn, vp, vc, vn, kx, vx, o_ref = refs
    else:
        kx, vx, o_ref = refs
    tq = q_ref.shape[0]
    i = pl.program_id(1)
    lane = lax.broadcasted_iota(I32, (tq, LANES), 1)
    row2 = lax.broadcasted_iota(I32, (2 * tq, 1), 0)
    half = WIN_HEAD_DIM
    npair = WIN_HEADS // 2
    kcats, vcats = [], []
    for g in range(WIN_KV_HEADS):
        gs = slice(g * LANES, (g + 1) * LANES)
        if has_win:
            kcats.append(jnp.concatenate([kp[:, gs], kc[:, gs], kn[:, gs], kx[:, gs]], axis=0))
            vcats.append(jnp.concatenate([vp[:, gs], vc[:, gs], vn[:, gs], vx[:, gs]], axis=0))
        else:
            kcats.append(kx[:, gs])
            vcats.append(vx[:, gs])
    if has_win:
        w = WIN_BLOCK
        shape = (2 * tq, kcats[0].shape[0])
        r = lax.broadcasted_iota(I32, shape, 0) & (tq - 1)
        c = lax.broadcasted_iota(I32, shape, 1)
        big = jnp.int32(1 << 20)
        no_prev = jnp.where(i > 0, 0, big)
        no_next = jnp.where(i < nb - 1, 0, big)
        ok_prev = c >= r + no_prev
        ok_next = (c - 2 * w) <= r - no_next
        valid = ((c >= w) | ok_prev) & ((c < 2 * w) | (c >= 3 * w) | ok_next)

    def score(j):
        qp = q_ref[:, j * LANES:(j + 1) * LANES]
        zero = jnp.zeros_like(qp)
        q2 = jnp.concatenate([jnp.where(lane < half, qp, zero), jnp.where(lane >= half, qp, zero)], axis=0)
        return _dot_nt(q2, kcats[j // 2])

    def softmax(j, s):
        s = jnp.where(valid, s, NEG) if has_win else s
        sk = jnp.where(row2 < tq, sink_ref[2 * j], sink_ref[2 * j + 1])
        m = jnp.maximum(jnp.max(s, axis=-1, keepdims=True), sk)
        p = jnp.exp(s - m)
        return p.astype(BF16), 1.0 / (jnp.sum(p, axis=-1, keepdims=True) + jnp.exp(sk - m))

    def values(j, p, inv):
        o2 = _dot(p, vcats[j // 2]) * inv
        o_ref[:, j * LANES:(j + 1) * LANES] = jnp.where(lane < half, o2[:tq], o2[tq:]).astype(BF16)

    s_val, p_val = {}, {}
    for t in range(npair + 2):
        if t < npair:
            s_val[t] = score(t)
        if 0 <= t - 1 < npair:
            p_val[t - 1] = softmax(t - 1, s_val.pop(t - 1))
        if 0 <= t - 2 < npair:
            values(t - 2, *p_val.pop(t - 2))


def _gqa(q, k, v, kx, vx, sink, has_win):
    b, n, _ = q.shape
    lc = kx.shape[1]
    smem = pl.BlockSpec(memory_space=pltpu.SMEM)
    ctxs = pl.BlockSpec((None, lc, 256), lambda bi, i: (bi, 0, 0))
    if has_win:
        tq = WIN_BLOCK
        nb = n // tq
        blk = lambda f: pl.BlockSpec((None, tq, 256), f)
        prev = lambda bi, i: (bi, jnp.maximum(i - 1, 0), 0)
        cur = lambda bi, i: (bi, i, 0)
        nxt = lambda bi, i: (bi, jnp.minimum(i + 1, nb - 1), 0)
        in_specs = [smem, pl.BlockSpec((None, tq, 512), cur),
                    blk(prev), blk(cur), blk(nxt), blk(prev), blk(cur), blk(nxt), ctxs, ctxs]
        args = (sink, q, k, k, k, v, v, v, kx, vx)
    else:
        tq, nb = n, 1
        in_specs = [smem, pl.BlockSpec((None, tq, 512), lambda bi, i: (bi, i, 0)), ctxs, ctxs]
        args = (sink, q, kx, vx)
    return pl.pallas_call(
        functools.partial(_gqa_kernel, has_win, nb),
        out_shape=jax.ShapeDtypeStruct((b, n, 512), BF16),
        grid=(b, nb),
        in_specs=in_specs,
        out_specs=pl.BlockSpec((None, tq, 512), lambda bi, i: (bi, i, 0)),
        compiler_params=_cparams(("parallel", "parallel")),
    )(*args)


def _log_sigmoid(z):
    return jnp.minimum(z, 0.0) - jnp.log(1.0 + jnp.exp(-jnp.abs(z)))


def _odd_in_kernel(x_ref, y0_ref, y1_ref, wt_ref, modp_ref, mod_ref, gn_ref, win_ref, wg_ref, bg_ref, lng_ref,
                   lnb_ref, ws_ref, bst_ref, xn_ref, q_ref, k_ref, v_ref, la_ref, r_ref, dl_ref):
    d = D_MODEL
    tm = x_ref.shape[0]
    wt = wt_ref[...]
    y = wt[:, 0:1] * _unpack_bf16_pairs(y0_ref[...]) + wt[:, 1:2] * _unpack_bf16_pairs(y1_ref[...])
    x = x_ref[...] + modp_ref[:, 5 * d:6 * d] * y
    xn_ref[...] = x
    mod = mod_ref[...]
    h = (_rms(x, gn_ref[...]) * (1.0 + mod[:, d:2 * d]) + mod[:, 0:d]).astype(BF16)
    z = _dot(h, win_ref[...])
    q_ref[...] = z[:, 0:256] * (GLA_DK ** -0.5)
    k_ref[...] = z[:, 256:512]
    v_ref[...] = z[:, 512:1024].astype(BF16)
    g_hi, g_lo = _split2(z[:, 1024:1152])
    w_hi, w_lo = _split2(wg_ref[...])
    zg = _dot(g_hi, w_hi) + _dot(g_lo, w_hi) + _dot(g_hi, w_lo) + bg_ref[...]
    la_ref[...] = _log_sigmoid(zg) / GLA_TAU
    r_ref[...] = z[:, 1152:1664]
    u = jax.nn.gelu(z[:, 1664:2176])
    vg = jax.nn.gelu(z[:, 2176:2688])
    mu = jnp.mean(vg, axis=-1, keepdims=True)
    vc = vg - mu
    var = jnp.mean(vc * vc, axis=-1, keepdims=True)
    vn = (vc * lax.rsqrt(var + EPS) * lng_ref[...] + lnb_ref[...]).astype(BF16)
    bst = bst_ref[...]
    for c in range(tm // SG_CHUNK):
        rows = slice(c * SG_CHUNK, (c + 1) * SG_CHUNK)
        parts = []
        for g in range(SG_GROUPS):
            cols = slice(g * LANES, (g + 1) * LANES)
            parts.append(_dot(ws_ref[g], vn[rows, cols]) + bst[:, g:g + 1])
        dl_ref[rows, :] = (u[rows, :] * jnp.concatenate(parts, axis=1)).astype(BF16)


def _odd_in(x, pending, tile0, modp, mod, mod_row, gn, wts):
    b, n, d = x.shape
    tm = MOE_TILE
    yt, wt = pending
    ntiles = wt.shape[0] // tm
    win, wg, bg, lng, lnb, ws, bst = wts
    row = (lambda bi, i: (bi, 0, 0)) if mod_row is None else (lambda bi, i: (mod_row, 0, 0))
    full = lambda a: pl.BlockSpec(a.shape, lambda bi, i: (0,) * a.ndim)
    act = lambda wd: pl.BlockSpec((None, tm, wd), lambda bi, i: (bi, i, 0))
    tok = lambda bi, i: tile0 + bi * (n // tm) + i
    outs = [((b, n, d), F32, act(d)),
            ((b, n, 256), F32, act(256)), ((b, n, 256), F32, act(256)), ((b, n, 512), BF16, act(512)),
            ((b, n, 512), F32, act(512)), ((b, n, 512), F32, act(512)), ((b, n, 512), BF16, act(512))]
    return pl.pallas_call(
        _odd_in_kernel,
        out_shape=[jax.ShapeDtypeStruct(s, t) for s, t, _ in outs],
        grid=(b, n // tm),
        in_specs=[act(d), pl.BlockSpec((tm, d // 2), lambda bi, i: (tok(bi, i), 0)),
                  pl.BlockSpec((tm, d // 2), lambda bi, i: (ntiles + tok(bi, i), 0)),
                  pl.BlockSpec((tm, 8), lambda bi, i: (tok(bi, i), 0)),
                  pl.BlockSpec((None, 1, 6 * d), row), pl.BlockSpec((None, 1, 6 * d), row),
                  full(gn), full(win), full(wg), full(bg), full(lng), full(lnb), full(ws), full(bst)],
        out_specs=[sp for _, _, sp in outs],
        compiler_params=_cparams(("parallel", "parallel")),
    )(x, yt, yt, wt, modp, mod, gn, win, wg, bg, lng, lnb, ws, bst)


def _gla_tables():
    c = GLA_BLOCK
    t = np.arange(c)[:, None]
    u = np.arange(c)[None, :]
    levels = [c >> i for i in range(int(np.log2(c)) + 1)]
    cum = np.zeros((2, 2 * len(levels), c, c), np.float32)
    pair = np.zeros((2, len(levels), c, c), np.float32)
    for li, m in enumerate(levels):
        same = (t // m) == (u // m)
        cum[0, 2 * li] = same & (u <= t)
        cum[0, 2 * li + 1] = same & (u > t)
        cum[1, 2 * li] = same & (u >= t)
        cum[1, 2 * li + 1] = same & (u < t)
        if li > 0:
            pair[0, li] = ((t // m) % 2 == 1) & ((u // m) == (t // m) - 1)
            pair[1, li] = ((t // m) % 2 == 0) & ((u // m) == (t // m) + 1)
    pair[:, 0] = np.eye(c, dtype=np.float32)
    nlev = len(levels)
    m1 = cum[:, 0::2].reshape(2, nlev * c, c)
    m2t = np.concatenate([cum[:, 2 * li + 1].transpose(0, 2, 1) for li in range(nlev)], axis=2)
    return np.concatenate([m1, m1], axis=2), np.concatenate([m2t, m2t], axis=1), pair, nlev


def _gla_chain(nlev, q, k, la, v_ref, cumq, cumkt, pm_ref, st_ref, o_ref):
    c = GLA_BLOCK
    lat, kt = la.T, k.T
    l_hi, l_mid = _split2(la)
    t_hi, t_mid = _split2(lat)
    exq = jnp.exp(_dot(cumq, jnp.concatenate([l_hi, l_mid], axis=0)))
    exk = jnp.exp(_dot(jnp.concatenate([t_hi, t_mid], axis=1), cumkt))
    gcol = jnp.exp(jnp.sum(lat, axis=1, keepdims=True))
    yield
    qe, ke = [], []
    for li in range(nlev):
        qe.append((q * exq[li * c:(li + 1) * c]).astype(BF16))
        ke.append((kt * exk[:, li * c:(li + 1) * c]).astype(BF16))
        if li % 2 == 1:
            yield
    qb, kb = q.astype(BF16), kt.astype(BF16)
    states = [st_ref[hd] for hd in range(GLA_HEADS)]
    yield
    outs, new_states = [], []
    lane = lax.broadcasted_iota(I32, (c, LANES), 1)
    srow = lax.broadcasted_iota(I32, (LANES, 1), 0)
    zero = jnp.zeros((c, LANES), BF16)
    for hd in range(GLA_HEADS):
        ps = slice((hd // 2) * LANES, (hd // 2 + 1) * LANES)
        vs = slice(hd * GLA_DV, (hd + 1) * GLA_DV)
        mine = (lane < GLA_DK) if hd % 2 == 0 else (lane >= GLA_DK)
        mine_row = (srow < GLA_DK) if hd % 2 == 0 else (srow >= GLA_DK)
        pick = lambda t: jnp.where(mine, t[:, ps], zero)
        a = pm_ref[0] * _dot(pick(qb), kb[ps, :])
        for li in range(1, nlev):
            a = a + pm_ref[li] * _dot(pick(qe[li]), ke[li][ps, :])
            if li % GLA_LEVELS_PER_PHASE == 0:
                yield
        v_h = v_ref[:, vs]
        outs.append(_dot(qe[0][:, ps], states[hd].astype(BF16)) + _dot(a.astype(BF16), v_h))
        new_states.append(states[hd] * gcol[ps, :] + jnp.where(mine_row, _dot(ke[0][ps, :], v_h), 0.0))
        yield
    o_ref[...] = jnp.concatenate(outs, axis=1)
    for hd in range(GLA_HEADS):
        st_ref[hd] = new_states[hd]
    yield


def _gla_kernel(nlev, nb, *refs):
    ins_f, ins_b = refs[0:4], refs[4:8]
    cumq_ref, cumkt_ref, pm_ref, s0_ref, of_ref, ob_ref, sf_ref = refs[8:15]
    st_refs = refs[15:]
    step = pl.program_id(1)

    @pl.when(step == 0)
    def _():
        for bb in range(nb):
            for d_ in range(2):
                st_refs[2 * bb + d_][...] = s0_ref[bb, d_]

    chains = []
    for bb in range(nb):
        for d_, (ins, o_ref) in enumerate(((ins_f, of_ref), (ins_b, ob_ref))):
            q_ref, k_ref, v_ref, la_ref = ins
            chains.append(_gla_chain(nlev, q_ref[bb], k_ref[bb], la_ref[bb], v_ref.at[bb], cumq_ref[d_],
                                     cumkt_ref[d_], pm_ref.at[d_], st_refs[2 * bb + d_], o_ref.at[bb]))
    while chains:
        chains = [ch for ch in chains if next(ch, "done") != "done"]
    for bb in range(nb):
        for d_ in range(2):
            sf_ref[bb, d_] = st_refs[2 * bb + d_][...]


def _gla(q, k, v, la, s0, cumq, cumkt, pm, nlev):
    b, n, _ = q.shape
    c = GLA_BLOCK
    nc = n // c
    nb = next(c for c in (GLA_BATCHES_PER_STEP, 2, 1) if b % c == 0)
    specs = []
    for d_ in range(2):
        pos = (lambda s_: s_) if d_ == 0 else (lambda s_: nc - 1 - s_)
        specs += [pl.BlockSpec((nb, c, 256), lambda bi, s_, pos=pos: (bi, pos(s_), 0)),
                  pl.BlockSpec((nb, c, 256), lambda bi, s_, pos=pos: (bi, pos(s_), 0)),
                  pl.BlockSpec((nb, c, 512), lambda bi, s_, pos=pos: (bi, pos(s_), 0)),
                  pl.BlockSpec((nb, c, 256), lambda bi, s_, pos=pos, d_=d_: (bi, pos(s_), d_))]
    st_spec = pl.BlockSpec((nb, 2, GLA_HEADS, GLA_DV, LANES), lambda bi, s_: (bi, 0, 0, 0, 0))
    full = lambda a: pl.BlockSpec(a.shape, lambda bi, s_: (0,) * a.ndim)
    return pl.pallas_call(
        functools.partial(_gla_kernel, nlev, nb),
        out_shape=[jax.ShapeDtypeStruct((b, n, GLA_V), F32), jax.ShapeDtypeStruct((b, n, GLA_V), F32),
                   jax.ShapeDtypeStruct((b, 2, GLA_HEADS, GLA_DV, LANES), F32)],
        grid=(b // nb, nc),
        in_specs=specs + [full(cumq), full(cumkt), full(pm), st_spec],
        out_specs=[pl.BlockSpec((nb, c, GLA_V), lambda bi, s_: (bi, s_, 0)),
                   pl.BlockSpec((nb, c, GLA_V), lambda bi, s_: (bi, nc - 1 - s_, 0)), st_spec],
        scratch_shapes=[pltpu.VMEM((GLA_HEADS, GLA_DV, LANES), F32) for _ in range(2 * nb)],
        compiler_params=_cparams(("parallel", "arbitrary")),
    )(q, k, v, la, q, k, v, la, cumq, cumkt, pm, s0)


def _odd_out_router_kernel(x_ref, of_ref, ob_ref, r_ref, dl_ref, gg_ref, w_ref, mod_ref, gn_ref, wr_ref, u_ref,
                           o_ref, *route_refs):
    d = D_MODEL
    gg = gg_ref[...]

    def tile(rs):
        o = of_ref[rs, :] + ob_ref[rs, :]
        r = r_ref[rs, :]
        parts = []
        for hd in range(GLA_HEADS):
            vs = slice(hd * GLA_DV, (hd + 1) * GLA_DV)
            oh = o[:, vs]
            parts.append(oh * lax.rsqrt(jnp.mean(oh * oh, axis=-1, keepdims=True) + EPS) * gg[:, vs])
        cl = (jnp.concatenate(parts, axis=1) * (r * jax.nn.sigmoid(r))).astype(BF16)
        y = _dot(cl, w_ref[0:512, :]) + _dot(dl_ref[rs, :], w_ref[512:1024, :])
        x = x_ref[rs, :] + mod_ref[:, 2 * d:3 * d] * y
        o_ref[rs, :] = x
        yield
        yield from _route_tile(x, rs, mod_ref, gn_ref, wr_ref, u_ref, *route_refs)

    _run_tiles(route_refs[-1], [tile(slice(t * MOE_TILE, (t + 1) * MOE_TILE)) for t in range(ROUTE_STEP // MOE_TILE)])


def _odd_out_router(x, o_fwd, o_bwd, r, dl, gg, w, mod, gn, wr, u):
    b, n, d = x.shape
    tm = ROUTE_STEP
    flat = lambda t: t.reshape(-1, t.shape[-1])
    act = lambda wd: pl.BlockSpec((tm, wd), lambda i: (i, 0))
    full = lambda t: pl.BlockSpec(t.shape, lambda i: (0,) * t.ndim)
    rshapes, rspecs, rscratch = _route_out(b * n, d)
    outs = pl.pallas_call(
        _odd_out_router_kernel,
        out_shape=[jax.ShapeDtypeStruct((b * n, d), F32)] + rshapes,
        grid=(b * n // tm,),
        in_specs=[act(d), act(GLA_V), act(GLA_V), act(512), act(512), full(gg), full(w),
                  pl.BlockSpec((None, 1, 6 * d), lambda i: (i // (n // tm), 0, 0)), full(gn), full(wr), full(u)],
        out_specs=[act(d)] + rspecs,
        scratch_shapes=rscratch,
        compiler_params=_cparams(("arbitrary",)),
    )(flat(x), flat(o_fwd), flat(o_bwd), flat(r), flat(dl), gg, w, mod, gn, wr, u)
    return outs[0].reshape(x.shape), outs[1:]


def _route_tile(x, rs, mod_ref, gn_ref, wr_ref, u_ref, h_ref, e_ref, wt_ref, r_ref, cnt_ref, carry_ref):
    d = D_MODEL
    tm = x.shape[0]
    mod = mod_ref[...]
    h = _rms(x, gn_ref[...]) * (1.0 + mod[:, 4 * d:5 * d]) + mod[:, 3 * d:4 * d]
    h_ref[rs, :] = _pack_bf16_pairs(h)
    h_hi, h_lo = _split2(h)
    w_hi, w_lo = _split2(wr_ref[...])
    lg = _dot_nt(w_hi, h_hi) + _dot_nt(w_lo, h_hi) + _dot_nt(w_hi, h_lo)
    yield
    rid = lax.broadcasted_iota(I32, (8, tm), 0)
    gl = jnp.where(rid < MOE_GROUPS, lg[0:8], NEG)
    gmax = jnp.max(gl, axis=0, keepdims=True)
    gsel = jnp.min(jnp.where(gl == gmax, rid, 8), axis=0, keepdims=True)
    pmax = 1.0 / jnp.sum(jnp.where(rid < MOE_GROUPS, jnp.exp(gl - gmax), 0.0), axis=0, keepdims=True)
    e_in = jnp.zeros((MOE_PER_GROUP, tm), F32)
    for g in range(MOE_GROUPS):
        e_in = e_in + jnp.where(gsel == g, lg[8 + 8 * g:16 + 8 * g], 0.0)
    v1 = jnp.max(e_in, axis=0, keepdims=True)
    i1 = jnp.min(jnp.where(e_in == v1, rid, 8), axis=0, keepdims=True)
    e_rest = jnp.where(rid == i1, -jnp.inf, e_in)
    v2 = jnp.max(e_rest, axis=0, keepdims=True)
    i2 = jnp.min(jnp.where(e_rest == v2, rid, 8), axis=0, keepdims=True)
    t = jnp.exp(v2 - v1)
    w1 = pmax / (1.0 + t)
    w2 = pmax * t / (1.0 + t)
    e1 = gsel * MOE_PER_GROUP + i1
    e2 = gsel * MOE_PER_GROUP + i2
    eid = lax.broadcasted_iota(I32, (MOE_EXPERTS, tm), 0)
    oh1 = jnp.where(eid == e1, 1.0, 0.0)
    oh2 = jnp.where(eid == e2, 1.0, 0.0)
    ohs = oh1 + oh2
    prefix = _dot(ohs.astype(BF16), u_ref[...])
    e_ref[:, rs] = jnp.concatenate([e1, e2], axis=0)
    w8 = jnp.concatenate([w1, w2, jnp.zeros((6, tm), F32)], axis=0)
    wt_ref[rs, :] = w8.T
    yield
    base = carry_ref[:, 0:1] + prefix
    r1 = jnp.sum(oh1 * base, axis=0, keepdims=True)
    r2 = jnp.sum(oh2 * base, axis=0, keepdims=True)
    carry_ref[...] = carry_ref[...] + jnp.sum(ohs, axis=1, keepdims=True)
    cnt_ref[...] = carry_ref[...]
    r_ref[:, rs] = jnp.concatenate([r1, r2], axis=0).astype(I32)


def _run_tiles(carry_ref, chains):
    @pl.when(pl.program_id(0) == 0)
    def _():
        carry_ref[...] = jnp.zeros_like(carry_ref)

    while chains:
        chains = [ch for ch in chains if next(ch, "done") != "done"]


def _route_out(n, d):
    tm = ROUTE_STEP
    shapes = [jax.ShapeDtypeStruct((n, d // 2), I32), jax.ShapeDtypeStruct((2, n), I32),
              jax.ShapeDtypeStruct((n, 8), F32), jax.ShapeDtypeStruct((2, n), I32),
              jax.ShapeDtypeStruct((MOE_EXPERTS, LANES), F32)]
    specs = [pl.BlockSpec((tm, d // 2), lambda i: (i, 0)), pl.BlockSpec((2, tm), lambda i: (0, i)),
             pl.BlockSpec((tm, 8), lambda i: (i, 0)), pl.BlockSpec((2, tm), lambda i: (0, i)),
             pl.BlockSpec((MOE_EXPERTS, LANES), lambda i: (0, 0))]
    return shapes, specs, [pltpu.VMEM((MOE_EXPERTS, LANES), F32)]


def _even_out_router_kernel(nlat, x_ref, xc_ref, a_ref, ac_ref, b_ref, bc_ref, w_ref, mod_ref, gn_ref, wr_ref, u_ref,
                            xo_ref, xoc_ref, *route_refs):
    d = D_MODEL
    lat = pl.program_id(0) < nlat

    def tile(rs):
        a = jnp.where(lat, a_ref[rs, :], ac_ref[rs, :])
        b = jnp.where(lat, b_ref[rs, :], bc_ref[rs, :])
        y = _dot(a, w_ref[0:512, :]) + _dot(b, w_ref[512:1024, :])
        x = jnp.where(lat, x_ref[rs, :], xc_ref[rs, :]) + mod_ref[:, 2 * d:3 * d] * y

        @pl.when(lat)
        def _():
            xo_ref[rs, :] = x

        @pl.when(jnp.logical_not(lat))
        def _():
            xoc_ref[rs, :] = x

        yield
        yield from _route_tile(x, rs, mod_ref, gn_ref, wr_ref, u_ref, *route_refs)

    _run_tiles(route_refs[-1], [tile(slice(t * MOE_TILE, (t + 1) * MOE_TILE)) for t in range(ROUTE_STEP // MOE_TILE)])


def _even_out_router(x, xc, oa, oac, ob, obc, w, mod, ctx_row, gn, wr, u):
    b, n, d = x.shape
    tm = ROUTE_STEP
    flat = lambda t: t.reshape(-1, t.shape[-1])
    nlat = b * n // tm
    ntok = b * n + xc.shape[0] * xc.shape[1]
    lat = lambda wd: pl.BlockSpec((tm, wd), lambda i: (jnp.minimum(i, nlat - 1), 0))
    ctx = lambda wd: pl.BlockSpec((tm, wd), lambda i: (jnp.maximum(i - nlat, 0), 0))
    full = lambda t: pl.BlockSpec(t.shape, lambda i: (0,) * t.ndim)
    modrow = lambda i: (jnp.where(i < nlat, i // (n // tm), ctx_row), 0, 0)
    rshapes, rspecs, rscratch = _route_out(ntok, d)
    outs = pl.pallas_call(
        functools.partial(_even_out_router_kernel, nlat),
        out_shape=[jax.ShapeDtypeStruct((b * n, d), F32), jax.ShapeDtypeStruct((ntok - b * n, d), F32)] + rshapes,
        grid=(ntok // tm,),
        in_specs=[lat(d), ctx(d), lat(512), ctx(512), lat(512), ctx(512), full(w),
                  pl.BlockSpec((None, 1, 6 * d), modrow), full(gn), full(wr), full(u)],
        out_specs=[lat(d), ctx(d)] + rspecs,
        scratch_shapes=rscratch,
        compiler_params=_cparams(("arbitrary",)),
    )(flat(x), flat(xc), flat(oa), flat(oac), flat(ob), flat(obc), w, mod, gn, wr, u)
    return outs[0].reshape(x.shape), outs[1].reshape(xc.shape), outs[2:]


def _sc_permute_rows(src, dest, scatter):
    rows, d = dest.shape[0], src.shape[1]
    n = rows // 2
    info = plsc.get_sparse_core_info()
    workers = info.num_cores * info.num_subcores
    per_worker = rows // workers
    chunk = next(c for c in SC_CHUNKS if per_worker % c == 0)
    assert rows == per_worker * workers and n % per_worker == 0
    mesh = plsc.VectorSubcoreMesh(core_axis_name="c", subcore_axis_name="s")

    def body(src_hbm, dest_hbm, out_hbm, idx_v, rows_v, sem):
        base = (lax.axis_index("s") * info.num_cores + lax.axis_index("c")) * per_worker

        @pl.loop(0, per_worker // chunk)
        def _(j):
            a0 = base + j * chunk
            pltpu.sync_copy(dest_hbm.at[pl.ds(a0, chunk)], idx_v)
            if scatter:
                t0 = jnp.where(a0 >= n, a0 - n, a0)
                pltpu.sync_copy(src_hbm.at[pl.ds(t0, chunk)], rows_v)
                pltpu.async_copy(rows_v, out_hbm.at[idx_v], sem).wait()
            else:
                pltpu.async_copy(src_hbm.at[idx_v], rows_v, sem).wait()
                pltpu.sync_copy(rows_v, out_hbm.at[pl.ds(a0, chunk)])

    return pl.kernel(
        body, out_type=jax.ShapeDtypeStruct((rows, d), src.dtype), mesh=mesh,
        scratch_types=[pltpu.VMEM((chunk,), I32), pltpu.VMEM((chunk, d), src.dtype), pltpu.SemaphoreType.DMA],
    )(src, dest)


def _gmm_kernel(layer, vt_ref, ve_ref, vlo_ref, vhi_ref, vfirst_ref, vslot_ref, vnext_ref, nv_ref,
                xs_ref, wg_hbm, wu_hbm, wd_hbm, ys_ref, wgs_ref, wus_ref, wds_ref, wgb_ref, wub_ref, wdb_ref, sem):
    del vt_ref
    v = pl.program_id(0)

    def fetch(e, slot):
        return [pltpu.make_async_copy(w.at[layer, e], s.at[slot], sem.at[slot, i])
                for i, (w, s) in enumerate(((wg_hbm, wgs_ref), (wu_hbm, wus_ref), (wd_hbm, wds_ref)))]

    @pl.when(v < nv_ref[0])
    def _():
        @pl.when((v == 0) | (ve_ref[v] != ve_ref[jnp.maximum(v - 1, 0)]))
        def _():
            slot = vslot_ref[v]

            @pl.when(v == 0)
            def _():
                for c in fetch(ve_ref[0], 0):
                    c.start()

            for c in fetch(ve_ref[v], slot):
                c.wait()
            wgb_ref[...] = wgs_ref[slot].astype(BF16)
            wub_ref[...] = wus_ref[slot].astype(BF16)
            wdb_ref[...] = wds_ref[slot].astype(BF16)

            @pl.when(vnext_ref[v] >= 0)
            def _():
                for c in fetch(vnext_ref[v], 1 - slot):
                    c.start()

        x = _unpack_bf16_pairs(xs_ref[...]).astype(BF16)
        g = _dot(x, wgb_ref[...])
        u = _dot(x, wub_ref[...])
        y = _pack_bf16_pairs(_dot((g * jax.nn.sigmoid(g) * u).astype(BF16), wdb_ref[...]))
        row = lax.broadcasted_iota(I32, (y.shape[0], 1), 0)
        mine = (row >= vlo_ref[v]) & (row < vhi_ref[v])

        @pl.when(vfirst_ref[v] == 1)
        def _():
            ys_ref[...] = jnp.where(mine, y, 0)

        @pl.when(vfirst_ref[v] == 0)
        def _():
            ys_ref[...] = jnp.where(mine, y, ys_ref[...])


def _gmm(xs, visits, layer, wg, wu, wd):
    rows, dw = xs.shape
    d = 2 * dw
    tm = MOE_TILE
    hid = wg.shape[-1]
    nvis = rows // tm + MOE_EXPERTS - 1
    tile = lambda v, vt, *_: (vt[v], 0)
    anyspec = pl.BlockSpec(memory_space=pl.ANY)
    return pl.pallas_call(
        functools.partial(_gmm_kernel, layer),
        out_shape=jax.ShapeDtypeStruct((rows, dw), I32),
        grid_spec=pltpu.PrefetchScalarGridSpec(
            num_scalar_prefetch=8, grid=(nvis,),
            in_specs=[pl.BlockSpec((tm, dw), tile), anyspec, anyspec, anyspec],
            out_specs=pl.BlockSpec((tm, dw), tile),
            scratch_shapes=[pltpu.VMEM((2, d, hid), F32), pltpu.VMEM((2, d, hid), F32), pltpu.VMEM((2, hid, d), F32),
                            pltpu.VMEM((d, hid), BF16), pltpu.VMEM((d, hid), BF16), pltpu.VMEM((hid, d), BF16),
                            pltpu.SemaphoreType.DMA((2, 3))]),
        compiler_params=_cparams(("arbitrary",)),
    )(*visits, xs, wg, wu, wd)


def _combine_kernel(x_ref, wt_ref, mod_ref, fg_ref, y0_ref, y1_ref, o_ref):
    d = D_MODEL
    wt = wt_ref[...]
    y = wt[:, 0:1] * _unpack_bf16_pairs(y0_ref[...]) + wt[:, 1:2] * _unpack_bf16_pairs(y1_ref[...])
    o_ref[...] = _rms(x_ref[...] + mod_ref[:, 5 * d:6 * d] * y, fg_ref[...])


def _combine(x2, wt, mod, rows_per_mod, fg, yt, tm=COMBINE_TILE):
    n, d = x2.shape
    tm = min(tm, rows_per_mod)
    assert rows_per_mod % tm == 0
    ntiles = n // tm
    return pl.pallas_call(
        _combine_kernel,
        out_shape=jax.ShapeDtypeStruct((n, d), F32),
        grid=(ntiles,),
        in_specs=[pl.BlockSpec((tm, d), lambda i: (i, 0)), pl.BlockSpec((tm, 8), lambda i: (i, 0)),
                  pl.BlockSpec((None, 1, 6 * d), lambda i: (i // (rows_per_mod // tm), 0, 0)),
                  pl.BlockSpec(fg.shape, lambda i: (0, 0)),
                  pl.BlockSpec((tm, d // 2), lambda i: (i, 0)), pl.BlockSpec((tm, d // 2), lambda i: (ntiles + i, 0))],
        out_specs=pl.BlockSpec((tm, d), lambda i: (i, 0)),
        compiler_params=_cparams(("parallel",)),
    )(x2, wt, mod, fg, yt, yt)


def _pick(table, idx):
    hot = idx[..., None] == jnp.arange(table.shape[0], dtype=I32)
    return jnp.sum(jnp.where(hot, table, 0), axis=-1)


def _moe_plan(counts, e, r, rows):
    tm = MOE_TILE
    ends = jnp.cumsum(counts)
    starts = ends - counts
    dest = (_pick(starts, e) + r).reshape(-1)
    first_tile = starts // tm
    nvis = jnp.where(counts > 0, (ends - 1) // tm - first_tile + 1, 0)
    vend = jnp.cumsum(nvis)
    nv = vend[-1:]
    v = jnp.minimum(jnp.arange(rows // tm + MOE_EXPERTS - 1, dtype=I32), nv[0] - 1)
    ve = jnp.sum((vend[None, :] <= v[:, None]).astype(I32), axis=1)
    vt = _pick(first_tile, ve) + v - _pick(vend - nvis, ve)
    vlo = jnp.maximum(_pick(starts, ve) - vt * tm, 0)
    vhi = jnp.minimum(_pick(ends, ve) - vt * tm, tm)
    vfirst = jnp.concatenate([jnp.ones((1,), I32), (vt[1:] != vt[:-1]).astype(I32)])
    changed = jnp.concatenate([jnp.ones((1,), I32), (ve[1:] != ve[:-1]).astype(I32)])
    vslot = (jnp.cumsum(changed) - 1) % 2
    eid = jnp.arange(MOE_EXPERTS, dtype=I32)
    later = (eid[None, :] > eid[:, None]) & (counts[None, :] > 0)
    nxt = jnp.min(jnp.where(later, eid[None, :], MOE_EXPERTS), axis=1)
    vnext = _pick(jnp.where(nxt < MOE_EXPERTS, nxt, -1), ve)
    return dest, (vt, ve, vlo, vhi, vfirst, vslot.astype(I32), vnext.astype(I32), nv)


def _moe_experts(routed, layer, wg, wu, wd):
    h, e, wt, r, cnt = routed
    dest, visits = _moe_plan(cnt[:, 0].astype(I32), e, r, 2 * h.shape[0])
    xs = _sc_permute_rows(h, dest, scatter=True)
    ys = _gmm(xs, visits, layer, wg, wu, wd)
    return _sc_permute_rows(ys, dest, scatter=False), wt


def _rope_tables(rows, dim):
    row = jnp.repeat(jnp.arange(rows, dtype=F32), GRID_W)
    col = jnp.tile(jnp.arange(GRID_W, dtype=F32), rows)
    half = dim // 2
    inv = jnp.power(ROPE_BASE, -jnp.arange(0, half, 2, dtype=F32) / half)
    ar = row[:, None] * inv[None, :]
    ac = col[:, None] * inv[None, :]
    ang = jnp.concatenate([ar, ar, ac, ac], axis=-1)
    return jnp.cos(ang), jnp.sin(ang)


def _even_tables(n, with_rope):
    if with_rope:
        cm, sm = _rope_tables(n // GRID_W, MLA_ROPE)
        cwin, swin = _rope_tables(n // GRID_W, WIN_HEAD_DIM)
    else:
        cm, sm = jnp.ones((n, MLA_ROPE), F32), jnp.zeros((n, MLA_ROPE), F32)
        cwin, swin = jnp.ones((n, WIN_HEAD_DIM), F32), jnp.zeros((n, WIN_HEAD_DIM), F32)
    one, zero = jnp.ones((n, MLA_NOPE), F32), jnp.zeros((n, MLA_NOPE), F32)
    pad = jnp.zeros((n, LANES - MLA_NOPE - MLA_ROPE), F32)
    return (jnp.concatenate([one, cm, pad], axis=1), jnp.concatenate([zero, sm, pad], axis=1),
            jnp.concatenate([cwin, cwin], axis=1), jnp.concatenate([swin, swin], axis=1))


def _even_weights(w_in, qg, w_uq, kvg, w_ukv):
    d = w_in.shape[0]
    o = np.cumsum([0, MLA_Q_RANK, MLA_KV_RANK, MLA_ROPE, 512, 128, 128])
    cq, ckv, kr, qw, kw, vw = [w_in[:, o[i]:o[i + 1]] for i in range(6)]
    z = lambda c: jnp.zeros((d, c), F32)
    kr128 = jnp.concatenate([z(MLA_NOPE), kr, z(LANES - MLA_NOPE - MLA_ROPE)], axis=1)
    dup = lambda t: jnp.concatenate([t[:, 0:64], t[:, 0:64], t[:, 64:128], t[:, 64:128]], axis=1)
    win = jnp.concatenate([cq, ckv, kr128, qw, dup(kw), dup(vw)], axis=1).astype(BF16)
    uq = w_uq.reshape(MLA_Q_RANK, MLA_HEADS, MLA_NOPE + MLA_ROPE)
    uq = jnp.pad(uq, ((0, 0), (0, 0), (0, LANES - MLA_NOPE - MLA_ROPE))).reshape(MLA_Q_RANK, MLA_HEADS * LANES)
    ukv = w_ukv.reshape(MLA_KV_RANK, MLA_HEADS, MLA_NOPE + MLA_V)
    ukk = jnp.pad(ukv[:, :, :MLA_NOPE], ((0, 0), (0, 0), (0, LANES - MLA_NOPE))).reshape(MLA_KV_RANK, MLA_HEADS * LANES)
    ukvv = ukv[:, :, MLA_NOPE:].reshape(MLA_KV_RANK, MLA_HEADS * MLA_V)
    return (win, qg.reshape(1, -1), uq.astype(BF16), kvg.reshape(1, -1), ukk.astype(BF16), ukvv.T.astype(BF16))


def _odd_weights(w_in, w_g2, b_g, ln_g, ln_b, w_s, b_s):
    d = w_in.shape[0]
    o = np.cumsum([0, GLA_K, GLA_K, GLA_V, 2 * GLA_GATE_RANK, GLA_V, SG_WIDTH, SG_WIDTH])
    q, k, v, g, r, u, vg = [w_in[:, o[i]:o[i + 1]] for i in range(7)]
    g128 = jnp.concatenate([g, jnp.zeros((d, LANES - 2 * GLA_GATE_RANK), F32)], axis=1)
    win = jnp.concatenate([q, k, v, g128, r, u, vg], axis=1).astype(BF16)
    zr = jnp.zeros((GLA_GATE_RANK, GLA_K), F32)
    pad = jnp.zeros((LANES - 2 * GLA_GATE_RANK, GLA_K), F32)
    wg = jnp.concatenate([jnp.concatenate([w_g2[0], zr, pad], axis=0),
                          jnp.concatenate([zr, w_g2[1], pad], axis=0)], axis=1)
    bg = b_g.reshape(1, 2 * GLA_K)
    return (win, wg, bg, ln_g.reshape(1, -1), ln_b.reshape(1, -1), w_s.astype(BF16), b_s.T)


def kernel(x, c, ctx, c_ctx, ada_w, ada_b, norm_mix_g, norm_ffn_g, even_w_in, mla_q_norm_g, mla_w_uq, mla_kv_norm_g, mla_w_ukv, win_sink, even_w_out, odd_w_in, gla_w_g2, gla_b_g, gla_norm_g, sg_ln_g, sg_ln_b, sg_w_s, sg_b_s, odd_w_out, moe_w_rg, moe_w_re, moe_w_gate, moe_w_up, moe_w_down, final_norm_g):
    b, n, d = x.shape
    lc = ctx.shape[1]
    depth = ada_w.shape[0]
    assert depth == 2 and d == D_MODEL and b < 8
    assert n % 512 == 0 and lc % MOE_TILE == 0 and n % GRID_W == 0
    tm = 512 if n % 512 == 0 else 256
    tq = 256

    cond8 = jnp.concatenate([c, c_ctx[None, :], jnp.zeros((8 - b - 1, d), F32)], axis=0)
    mod_all = _adaln(cond8, ada_w, ada_b).reshape(depth, 8, 1, 6 * d)
    ctx_row = b
    u_tri = jnp.asarray(np.triu(np.ones((MOE_TILE, MOE_TILE), np.float32), 1), BF16)
    fg = final_norm_g.reshape(1, d)

    def router_w(layer):
        return jnp.concatenate([moe_w_rg[layer].T, jnp.zeros((8 - MOE_GROUPS, d), F32), moe_w_re[layer].T], axis=0)

    def moe_experts(routed, layer):
        return _moe_experts(routed, layer, moe_w_gate, moe_w_up, moe_w_down)

    mod = mod_all[0]
    gn = norm_mix_g[0].reshape(1, d)
    ew = _even_weights(even_w_in[0], mla_q_norm_g[0], mla_w_uq[0], mla_kv_norm_g[0], mla_w_ukv[0])
    qm_l, km_l, vm_l, qw_l, kw_l, vw_l = _even_in(x, mod, None, gn, ew, _even_tables(n, True), tm)
    qm_c, km_c, vm_c, qw_c, kw_c, vw_c = _even_in(ctx, mod, ctx_row, gn, ew, _even_tables(lc, False), lc)
    w_out = even_w_out[0].astype(BF16)
    sink = win_sink[0]
    oa_l = _mla_attn(qm_l, [(km_l, vm_l), (km_c, vm_c)], tq)
    ob_l = _gqa(qw_l, kw_l, vw_l, kw_c, vw_c, sink, True)
    oa_c = _mla_attn(qm_c, [(km_c, vm_c)], lc)
    ob_c = _gqa(qw_c, None, None, kw_c, vw_c, sink, False)
    xl, xc, routed = _even_out_router(x, ctx, oa_l, oa_c, ob_l, ob_c, w_out, mod, ctx_row,
                                      norm_ffn_g[0].reshape(1, d), router_w(0), u_tri)
    pending = moe_experts(routed, 0)

    mod = mod_all[1]
    gn = norm_mix_g[1].reshape(1, d)
    ow = _odd_weights(odd_w_in[0], gla_w_g2[0], gla_b_g[0], sg_ln_g[0], sg_ln_b[0], sg_w_s[0], sg_b_s[0])
    xl, q_l, k_l, v_l, la_l, r_l, dl_l = _odd_in(xl, pending, 0, mod_all[0], mod, None, gn, ow)
    _, q_c, k_c, v_c, la_c, _, _ = _odd_in(xc, pending, b * n // MOE_TILE, mod_all[0], mod, ctx_row, gn, ow)
    cumq_np, cumkt_np, pm_np, nlev = _gla_tables()
    cumq, cumkt = jnp.asarray(cumq_np, BF16), jnp.asarray(cumkt_np, BF16)
    pm = jnp.asarray(pm_np, F32)
    s0 = jnp.zeros((b, 2, GLA_HEADS, GLA_DV, LANES), F32)
    _, _, s_ctx = _gla(q_c, k_c, v_c, la_c, s0, cumq, cumkt, pm, nlev)
    o_fwd, o_bwd, _ = _gla(q_l, k_l, v_l, la_l, s_ctx, cumq, cumkt, pm, nlev)
    xl, routed = _odd_out_router(xl, o_fwd, o_bwd, r_l, dl_l, gla_norm_g[0].reshape(1, -1),
                                 odd_w_out[0].astype(BF16), mod, norm_ffn_g[1].reshape(1, d), router_w(1), u_tri)
    yt, wt = moe_experts(routed, 1)
    return _combine(xl.reshape(b * n, d), wt, mod, n, fg, yt).reshape(b, n, d)
```

```python
import functools

import numpy as np
import jax
import jax.numpy as jnp
from jax import lax
from jax.experimental import pallas as pl
from jax.experimental.pallas import tpu as pltpu
from jax.experimental.pallas import tpu_sc as plsc

F32 = jnp.float32
BF16 = jnp.bfloat16
I32 = jnp.int32

D_MODEL = 1024
GRID_W = 64
EPS = 1e-6
ROPE_BASE = 10000.0
MLA_HEADS = 8
MLA_Q_RANK = 256
MLA_KV_RANK = 128
MLA_NOPE = 64
MLA_ROPE = 32
MLA_V = 64
WIN_HEADS = 8
WIN_KV_HEADS = 2
WIN_HEAD_DIM = 64
WIN_BLOCK = 128
GLA_HEADS = 4
GLA_DK = 64
GLA_DV = 128
GLA_GATE_RANK = 16
GLA_TAU = 16.0
GLA_K = GLA_HEADS * GLA_DK
GLA_V = GLA_HEADS * GLA_DV
SG_GROUPS = 4
SG_CHUNK = 128
SG_WIDTH = 512
MOE_GROUPS = 4
MOE_PER_GROUP = 8
MOE_EXPERTS = 32
MOE_HIDDEN = 512

LANES = 128
GLA_BLOCK = 128
GLA_LEVELS_PER_PHASE = 2
GLA_BATCHES_PER_STEP = 2
MOE_TILE = 256
ODD_IN_STEP = 1024
ROUTE_STEP = 1024
COMBINE_TILE = 1024
MLA_KEY_CHUNK = 1024
SC_CHUNKS = (128, 64, 32)
NEG = -1e30
LOG2E = 1.4426950408889634
VMEM_LIMIT = 56 * 1024 * 1024


def _cparams(sem):
    return pltpu.CompilerParams(dimension_semantics=sem, vmem_limit_bytes=VMEM_LIMIT)


def _dot(a, b):
    return jnp.dot(a, b, preferred_element_type=F32)


def _dot_nt(a, b):
    return lax.dot_general(a, b, (((1,), (1,)), ((), ())), preferred_element_type=F32)


def _split2(a):
    hi = a.astype(BF16)
    lo = (a - hi.astype(F32)).astype(BF16)
    return hi, lo


def _pack_bf16_pairs(x):
    k = x.shape[1] // 2
    bits = lax.bitcast_convert_type(x.astype(BF16).astype(F32), jnp.uint32)
    return lax.bitcast_convert_type(bits[:, :k] | (bits[:, k:] >> 16), I32)


def _unpack_bf16_pairs(w):
    bits = lax.bitcast_convert_type(w, jnp.uint32)
    hi = lax.bitcast_convert_type(bits & jnp.uint32(0xFFFF0000), F32)
    lo = lax.bitcast_convert_type(bits << 16, F32)
    return jnp.concatenate([hi, lo], axis=1)


def _rms(x, g):
    ms = jnp.mean(x * x, axis=-1, keepdims=True)
    return x * lax.rsqrt(ms + EPS) * g


def _lane_tile(t, reps):
    return t if reps == 1 else jnp.concatenate([t] * reps, axis=1)


def _rope(t, cos, sin, quarter):
    n = t.shape[1]
    lane = lax.broadcasted_iota(I32, t.shape, 1)
    first = (lane & (2 * quarter - 1)) < quarter
    rot = jnp.where(first, -pltpu.roll(t, n - quarter, 1), pltpu.roll(t, quarter, 1))
    return t * cos + rot * sin


def _adaln_kernel(c_ref, w_ref, b_ref, o_ref):
    c = c_ref[...]
    s_hi, s_lo = _split2(c * jax.nn.sigmoid(c))
    w_hi, w_lo = _split2(w_ref[...])
    o_ref[...] = _dot(s_hi, w_hi) + _dot(s_lo, w_hi) + _dot(s_hi, w_lo) + b_ref[...]


def _adaln(cond8, ada_w, ada_b):
    depth, d, n6 = ada_w.shape
    tn = 1536
    return pl.pallas_call(
        _adaln_kernel,
        out_shape=jax.ShapeDtypeStruct((depth, 8, n6), F32),
        grid=(depth, n6 // tn),
        in_specs=[
            pl.BlockSpec((8, d), lambda l, j: (0, 0)),
            pl.BlockSpec((None, d, tn), lambda l, j: (l, 0, j)),
            pl.BlockSpec((None, 1, tn), lambda l, j: (l, 0, j)),
        ],
        out_specs=pl.BlockSpec((None, 8, tn), lambda l, j: (l, 0, j)),
        compiler_params=_cparams(("parallel", "parallel")),
    )(cond8, ada_w, ada_b.reshape(depth, 1, n6))


def _even_in_kernel(x_ref, mod_ref, gn_ref, win_ref, qg_ref, wuq_ref, kvg_ref, wukk_ref, wukv_ref,
                    cq_ref, sq_ref, cw_ref, sw_ref,
                    qm_ref, km_ref, vm_ref, qw_ref, kw_ref, vw_ref):
    d = D_MODEL
    mod = mod_ref[...]
    h = _rms(x_ref[...], gn_ref[...]) * (1.0 + mod[:, d:2 * d]) + mod[:, 0:d]
    z = _dot(h.astype(BF16), win_ref[...])
    cq, sq, cw, sw = cq_ref[...], sq_ref[...], cw_ref[...], sw_ref[...]
    cqn = _rms(z[:, 0:256], qg_ref[...]).astype(BF16)
    q = _dot(cqn, wuq_ref[...])
    q = _rope(q, _lane_tile(cq, 8), _lane_tile(sq, 8), MLA_ROPE // 4)
    qm_ref[...] = (q * (LOG2E * (MLA_NOPE + MLA_ROPE) ** -0.5)).astype(BF16)
    ckvn = _rms(z[:, 256:384], kvg_ref[...]).astype(BF16)
    kn = _dot(ckvn, wukk_ref[...])
    kr = _rope(z[:, 384:512], cq, sq, MLA_ROPE // 4)
    km_ref[...] = (kn + _lane_tile(kr, 8)).astype(BF16)
    vm_ref[...] = _dot_nt(wukv_ref[...], ckvn).astype(BF16)
    qw = _rope(z[:, 512:1024], _lane_tile(cw, 4), _lane_tile(sw, 4), WIN_HEAD_DIM // 4)
    qw_ref[...] = (qw * (WIN_HEAD_DIM ** -0.5)).astype(BF16)
    kw = _rope(z[:, 1024:1280], _lane_tile(cw, 2), _lane_tile(sw, 2), WIN_HEAD_DIM // 4)
    kw_ref[...] = kw.astype(BF16)
    vw_ref[...] = z[:, 1280:1536].astype(BF16)


def _even_in(x, mod, mod_row, gn, wts, tabs, tm):
    b, n, d = x.shape
    win, qg, wuq, kvg, wukk, wukv = wts
    nt = n // tm
    row = (lambda bi, i: (bi, 0, 0)) if mod_row is None else (lambda bi, i: (mod_row, 0, 0))
    full = lambda a: pl.BlockSpec(a.shape, lambda bi, i: (0,) * a.ndim)
    tab = pl.BlockSpec((tm, LANES), lambda bi, i: (i, 0))
    outw = (1024, 1024, None, 512, 256, 256)
    rowspec = lambda w: pl.BlockSpec((None, tm, w), lambda bi, i: (bi, i, 0))
    colspec = pl.BlockSpec((None, 512, tm), lambda bi, i: (bi, 0, i))
    return pl.pallas_call(
        _even_in_kernel,
        out_shape=[jax.ShapeDtypeStruct((b, 512, n) if w is None else (b, n, w), BF16) for w in outw],
        grid=(b, nt),
        in_specs=[pl.BlockSpec((None, tm, d), lambda bi, i: (bi, i, 0)),
                  pl.BlockSpec((None, 1, 6 * d), row),
                  full(gn), full(win), full(qg), full(wuq), full(kvg), full(wukk), full(wukv),
                  tab, tab, tab, tab],
        out_specs=[colspec if w is None else rowspec(w) for w in outw],
        compiler_params=_cparams(("parallel", "parallel")),
    )(x, mod, gn, win, qg, wuq, kvg, wukk, wukv, *tabs)


def _mla_attn_kernel(nseg, q_ref, *refs):
    ks, vts = refs[0:2 * nseg:2], refs[1:2 * nseg:2]
    o_ref = refs[2 * nseg]
    s_bufs = refs[2 * nseg + 1:2 * nseg + 3]
    p_bufs = refs[2 * nseg + 3:2 * nseg + 5]
    pieces, base = [], 0
    for k in ks:
        n = k.shape[0]
        pieces += [(k, c0, min(n, c0 + MLA_KEY_CHUNK), base + c0) for c0 in range(0, n, MLA_KEY_CHUNK)]
        base += n

    def score_chunk(h, piece, buf):
        k, c0, c1, g0 = piece
        hs = slice(h * LANES, (h + 1) * LANES)
        s = _dot_nt(k[c0:c1, hs], q_ref[:, hs])
        buf[g0:g0 + c1 - c0, :] = s
        return jnp.max(s, axis=0, keepdims=True)

    def prob_chunk(piece, sbuf, pbuf, m):
        _, c0, c1, g0 = piece
        p = jnp.exp2(sbuf[g0:g0 + c1 - c0, :] - m)
        pbuf[g0:g0 + c1 - c0, :] = p.astype(BF16)
        return jnp.sum(p, axis=0, keepdims=True)

    m_next = functools.reduce(jnp.maximum, [score_chunk(0, pc, s_bufs[0]) for pc in pieces])
    outs = []
    for h in range(MLA_HEADS):
        m_cur, maxes, sums = m_next, [], []
        for pc in pieces:
            if h + 1 < MLA_HEADS:
                maxes.append(score_chunk(h + 1, pc, s_bufs[(h + 1) % 2]))
            sums.append(prob_chunk(pc, s_bufs[h % 2], p_bufs[h % 2], m_cur))
        if h + 1 < MLA_HEADS:
            m_next = functools.reduce(jnp.maximum, maxes)
        l = functools.reduce(jnp.add, sums)
        vrows = slice(h * MLA_V, (h + 1) * MLA_V)
        ot, base = None, 0
        for k, vt in zip(ks, vts):
            n = k.shape[0]
            part = _dot(vt[vrows, :], p_bufs[h % 2][base:base + n, :])
            ot = part if ot is None else ot + part
            base += n
        outs.append(ot * (1.0 / l))
    o_ref[...] = jnp.concatenate(outs, axis=0).T.astype(BF16)


def _mla_attn(q, segs, tq):
    b, n, _ = q.shape
    in_specs = [pl.BlockSpec((None, tq, 1024), lambda bi, i: (bi, i, 0))]
    args = [q]
    keys = 0
    for k, vt in segs:
        lk = k.shape[1]
        keys += lk
        in_specs += [pl.BlockSpec((None, lk, 1024), lambda bi, i: (bi, 0, 0)),
                     pl.BlockSpec((None, 512, lk), lambda bi, i: (bi, 0, 0))]
        args += [k, vt]
    return pl.pallas_call(
        functools.partial(_mla_attn_kernel, len(segs)),
        out_shape=jax.ShapeDtypeStruct((b, n, 512), BF16),
        grid=(b, n // tq),
        in_specs=in_specs,
        out_specs=pl.BlockSpec((None, tq, 512), lambda bi, i: (bi, i, 0)),
        scratch_shapes=[pltpu.VMEM((keys, tq), F32), pltpu.VMEM((keys, tq), F32),
                        pltpu.VMEM((keys, tq), BF16), pltpu.VMEM((keys, tq), BF16)],
        compiler_params=_cparams(("parallel", "parallel")),
    )(*args)


def _gqa_kernel(has_win, nb, sink_ref, q_ref, *refs):
    if has_win:
        kp, kc, kn, vp, vc, vn, kx, vx, o_ref = refs
    else:
        kx, vx, o_ref = refs
    tq = q_ref.shape[0]
    i = pl.program_id(1)
    lane = lax.broadcasted_iota(I32, (tq, LANES), 1)
    row2 = lax.broadcasted_iota(I32, (2 * tq, 1), 0)
    half = WIN_HEAD_DIM
    npair = WIN_HEADS // 2
    kcats, vcats = [], []
    for g in range(WIN_KV_HEADS):
        gs = slice(g * LANES, (g + 1) * LANES)
        if has_win:
            kcats.append(jnp.concatenate([kp[:, gs], kc[:, gs], kn[:, gs], kx[:, gs]], axis=0))
            vcats.append(jnp.concatenate([vp[:, gs], vc[:, gs], vn[:, gs], vx[:, gs]], axis=0))
        else:
            kcats.append(kx[:, gs])
            vcats.append(vx[:, gs])
    if has_win:
        w = WIN_BLOCK
        shape = (2 * tq, kcats[0].shape[0])
        r = lax.broadcasted_iota(I32, shape, 0) & (tq - 1)
        c = lax.broadcasted_iota(I32, shape, 1)
        big = jnp.int32(1 << 20)
        no_prev = jnp.where(i > 0, 0, big)
        no_next = jnp.where(i < nb - 1, 0, big)
        ok_prev = c >= r + no_prev
        ok_next = (c - 2 * w) <= r - no_next
        valid = ((c >= w) | ok_prev) & ((c < 2 * w) | (c >= 3 * w) | ok_next)

    def score(j):
        qp = q_ref[:, j * LANES:(j + 1) * LANES]
        zero = jnp.zeros_like(qp)
        q2 = jnp.concatenate([jnp.where(lane < half, qp, zero), jnp.where(lane >= half, qp, zero)], axis=0)
        return _dot_nt(q2, kcats[j // 2])

    def softmax(j, s):
        s = jnp.where(valid, s, NEG) if has_win else s
        sk = jnp.where(row2 < tq, sink_ref[2 * j], sink_ref[2 * j + 1])
        m = jnp.maximum(jnp.max(s, axis=-1, keepdims=True), sk)
        p = jnp.exp(s - m)
        return p.astype(BF16), 1.0 / (jnp.sum(p, axis=-1, keepdims=True) + jnp.exp(sk - m))

    def values(j, p, inv):
        o2 = _dot(p, vcats[j // 2]) * inv
        o_ref[:, j * LANES:(j + 1) * LANES] = jnp.where(lane < half, o2[:tq], o2[tq:]).astype(BF16)

    s_val, p_val = {}, {}
    for t in range(npair + 2):
        if t < npair:
            s_val[t] = score(t)
        if 0 <= t - 1 < npair:
            p_val[t - 1] = softmax(t - 1, s_val.pop(t - 1))
        if 0 <= t - 2 < npair:
            values(t - 2, *p_val.pop(t - 2))


def _gqa(q, k, v, kx, vx, sink, has_win):
    b, n, _ = q.shape
    lc = kx.shape[1]
    smem = pl.BlockSpec(memory_space=pltpu.SMEM)
    ctxs = pl.BlockSpec((None, lc, 256), lambda bi, i: (bi, 0, 0))
    if has_win:
        tq = WIN_BLOCK
        nb = n // tq
        blk = lambda f: pl.BlockSpec((None, tq, 256), f)
        prev = lambda bi, i: (bi, jnp.maximum(i - 1, 0), 0)
        cur = lambda bi, i: (bi, i, 0)
        nxt = lambda bi, i: (bi, jnp.minimum(i + 1, nb - 1), 0)
        in_specs = [smem, pl.BlockSpec((None, tq, 512), cur),
                    blk(prev), blk(cur), blk(nxt), blk(prev), blk(cur), blk(nxt), ctxs, ctxs]
        args = (sink, q, k, k, k, v, v, v, kx, vx)
    else:
        tq, nb = n, 1
        in_specs = [smem, pl.BlockSpec((None, tq, 512), lambda bi, i: (bi, i, 0)), ctxs, ctxs]
        args = (sink, q, kx, vx)
    return pl.pallas_call(
        functools.partial(_gqa_kernel, has_win, nb),
        out_shape=jax.ShapeDtypeStruct((b, n, 512), BF16),
        grid=(b, nb),
        in_specs=in_specs,
        out_specs=pl.BlockSpec((None, tq, 512), lambda bi, i: (bi, i, 0)),
        compiler_params=_cparams(("parallel", "parallel")),
    )(*args)


def _log_sigmoid(z):
    return jnp.minimum(z, 0.0) - jnp.log(1.0 + jnp.exp(-jnp.abs(z)))


def _odd_in_kernel(x_ref, y0_ref, y1_ref, wt_ref, modp_ref, mod_ref, gn_ref, win_ref, wg_ref, bg_ref, lng_ref,
                   lnb_ref, ws_ref, bst_ref, xn_ref, q_ref, k_ref, v_ref, la_ref, r_ref, dl_ref):
    d = D_MODEL
    mod = mod_ref[...]
    bst = bst_ref[...]

    def tile(r0, tm):
        rs = slice(r0, r0 + tm)
        wt = wt_ref[rs, :]
        y = wt[:, 0:1] * _unpack_bf16_pairs(y0_ref[rs, :]) + wt[:, 1:2] * _unpack_bf16_pairs(y1_ref[rs, :])
        x = x_ref[rs, :] + modp_ref[:, 5 * d:6 * d] * y
        xn_ref[rs, :] = x
        h = (_rms(x, gn_ref[...]) * (1.0 + mod[:, d:2 * d]) + mod[:, 0:d]).astype(BF16)
        z = _dot(h, win_ref[...])
        yield
        q_ref[rs, :] = z[:, 0:256] * (GLA_DK ** -0.5)
        k_ref[rs, :] = z[:, 256:512]
        v_ref[rs, :] = z[:, 512:1024].astype(BF16)
        g_hi, g_lo = _split2(z[:, 1024:1152])
        w_hi, w_lo = _split2(wg_ref[...])
        zg = _dot(g_hi, w_hi) + _dot(g_lo, w_hi) + _dot(g_hi, w_lo) + bg_ref[...]
        la_ref[rs, :] = _log_sigmoid(zg) / GLA_TAU
        r_ref[rs, :] = z[:, 1152:1664]
        yield
        u = jax.nn.gelu(z[:, 1664:2176])
        vg = jax.nn.gelu(z[:, 2176:2688])
        mu = jnp.mean(vg, axis=-1, keepdims=True)
        vc = vg - mu
        var = jnp.mean(vc * vc, axis=-1, keepdims=True)
        vn = (vc * lax.rsqrt(var + EPS) * lng_ref[...] + lnb_ref[...]).astype(BF16)
        yield
        for c in range(tm // SG_CHUNK):
            rows = slice(c * SG_CHUNK, (c + 1) * SG_CHUNK)
            parts = []
            for g in range(SG_GROUPS):
                cols = slice(g * LANES, (g + 1) * LANES)
                parts.append(_dot(ws_ref[g], vn[rows, cols]) + bst[:, g:g + 1])
            dl_ref[r0 + c * SG_CHUNK:r0 + (c + 1) * SG_CHUNK, :] = (
                u[rows, :] * jnp.concatenate(parts, axis=1)).astype(BF16)

    tm = min(MOE_TILE, x_ref.shape[0])
    chains = [tile(r0, tm) for r0 in range(0, x_ref.shape[0], tm)]
    while chains:
        chains = [ch for ch in chains if next(ch, "done") != "done"]


def _odd_in(x, pending, tile0, modp, mod, mod_row, gn, wts):
    b, n, d = x.shape
    tm = min(ODD_IN_STEP, n)
    tile0 = tile0 * MOE_TILE // tm
    yt, wt = pending
    ntiles = wt.shape[0] // tm
    win, wg, bg, lng, lnb, ws, bst = wts
    row = (lambda bi, i: (bi, 0, 0)) if mod_row is None else (lambda bi, i: (mod_row, 0, 0))
    full = lambda a: pl.BlockSpec(a.shape, lambda bi, i: (0,) * a.ndim)
    act = lambda wd: pl.BlockSpec((None, tm, wd), lambda bi, i: (bi, i, 0))
    tok = lambda bi, i: tile0 + bi * (n // tm) + i
    outs = [((b, n, d), F32, act(d)),
            ((b, n, 256), F32, act(256)), ((b, n, 256), F32, act(256)), ((b, n, 512), BF16, act(512)),
            ((b, n, 512), F32, act(512)), ((b, n, 512), F32, act(512)), ((b, n, 512), BF16, act(512))]
    return pl.pallas_call(
        _odd_in_kernel,
        out_shape=[jax.ShapeDtypeStruct(s, t) for s, t, _ in outs],
        grid=(b, n // tm),
        in_specs=[act(d), pl.BlockSpec((tm, d // 2), lambda bi, i: (tok(bi, i), 0)),
                  pl.BlockSpec((tm, d // 2), lambda bi, i: (ntiles + tok(bi, i), 0)),
                  pl.BlockSpec((tm, 8), lambda bi, i: (tok(bi, i), 0)),
                  pl.BlockSpec((None, 1, 6 * d), row), pl.BlockSpec((None, 1, 6 * d), row),
                  full(gn), full(win), full(wg), full(bg), full(lng), full(lnb), full(ws), full(bst)],
        out_specs=[sp for _, _, sp in outs],
        compiler_params=_cparams(("parallel", "parallel")),
    )(x, yt, yt, wt, modp, mod, gn, win, wg, bg, lng, lnb, ws, bst)


def _gla_tables():
    c = GLA_BLOCK
    t = np.arange(c)[:, None]
    u = np.arange(c)[None, :]
    levels = [c >> i for i in range(int(np.log2(c)) + 1)]
    cum = np.zeros((2, 2 * len(levels), c, c), np.float32)
    pair = np.zeros((2, len(levels), c, c), np.float32)
    for li, m in enumerate(levels):
        same = (t // m) == (u // m)
        cum[0, 2 * li] = same & (u <= t)
        cum[0, 2 * li + 1] = same & (u > t)
        cum[1, 2 * li] = same & (u >= t)
        cum[1, 2 * li + 1] = same & (u < t)
        if li > 0:
            pair[0, li] = ((t // m) % 2 == 1) & ((u // m) == (t // m) - 1)
            pair[1, li] = ((t // m) % 2 == 0) & ((u // m) == (t // m) + 1)
    pair[:, 0] = np.eye(c, dtype=np.float32)
    nlev = len(levels)
    m1 = cum[:, 0::2].reshape(2, nlev * c, c)
    m2t = np.concatenate([cum[:, 2 * li + 1].transpose(0, 2, 1) for li in range(nlev)], axis=2)
    return np.concatenate([m1, m1], axis=2), np.concatenate([m2t, m2t], axis=1), pair, nlev


def _gla_chain(nlev, q, k, la, v_ref, cumq, cumkt, pm_ref, st_ref, o_ref):
    c = GLA_BLOCK
    lat, kt = la.T, k.T
    l_hi, l_mid = _split2(la)
    t_hi, t_mid = _split2(lat)
    exq = jnp.exp(_dot(cumq, jnp.concatenate([l_hi, l_mid], axis=0)))
    exk = jnp.exp(_dot(jnp.concatenate([t_hi, t_mid], axis=1), cumkt))
    gcol = jnp.exp(jnp.sum(lat, axis=1, keepdims=True))
    yield
    qe, ke = [], []
    for li in range(nlev):
        qe.append((q * exq[li * c:(li + 1) * c]).astype(BF16))
        ke.append((kt * exk[:, li * c:(li + 1) * c]).astype(BF16))
        if li % 2 == 1:
            yield
    qb, kb = q.astype(BF16), kt.astype(BF16)
    states = [st_ref[hd] for hd in range(GLA_HEADS)]
    yield
    outs, new_states = [], []
    lane = lax.broadcasted_iota(I32, (c, LANES), 1)
    srow = lax.broadcasted_iota(I32, (LANES, 1), 0)
    zero = jnp.zeros((c, LANES), BF16)
    for hd in range(GLA_HEADS):
        ps = slice((hd // 2) * LANES, (hd // 2 + 1) * LANES)
        vs = slice(hd * GLA_DV, (hd + 1) * GLA_DV)
        mine = (lane < GLA_DK) if hd % 2 == 0 else (lane >= GLA_DK)
        mine_row = (srow < GLA_DK) if hd % 2 == 0 else (srow >= GLA_DK)
        pick = lambda t: jnp.where(mine, t[:, ps], zero)
        a = pm_ref[0] * _dot(pick(qb), kb[ps, :])
        for li in range(1, nlev):
            a = a + pm_ref[li] * _dot(pick(qe[li]), ke[li][ps, :])
            if li % GLA_LEVELS_PER_PHASE == 0:
                yield
        v_h = v_ref[:, vs]
        outs.append(_dot(qe[0][:, ps], states[hd].astype(BF16)) + _dot(a.astype(BF16), v_h))
        new_states.append(states[hd] * gcol[ps, :] + jnp.where(mine_row, _dot(ke[0][ps, :], v_h), 0.0))
        yield
    o_ref[...] = jnp.concatenate(outs, axis=1)
    for hd in range(GLA_HEADS):
        st_ref[hd] = new_states[hd]
    yield


def _gla_kernel(nlev, nb, *refs):
    ins_f, ins_b = refs[0:4], refs[4:8]
    cumq_ref, cumkt_ref, pm_ref, s0_ref, of_ref, ob_ref, sf_ref = refs[8:15]
    st_refs = refs[15:]
    step = pl.program_id(1)

    @pl.when(step == 0)
    def _():
        for bb in range(nb):
            for d_ in range(2):
                st_refs[2 * bb + d_][...] = s0_ref[bb, d_]

    chains = []
    for bb in range(nb):
        for d_, (ins, o_ref) in enumerate(((ins_f, of_ref), (ins_b, ob_ref))):
            q_ref, k_ref, v_ref, la_ref = ins
            chains.append(_gla_chain(nlev, q_ref[bb], k_ref[bb], la_ref[bb], v_ref.at[bb], cumq_ref[d_],
                                     cumkt_ref[d_], pm_ref.at[d_], st_refs[2 * bb + d_], o_ref.at[bb]))
    while chains:
        chains = [ch for ch in chains if next(ch, "done") != "done"]
    for bb in range(nb):
        for d_ in range(2):
            sf_ref[bb, d_] = st_refs[2 * bb + d_][...]


def _gla(q, k, v, la, s0, cumq, cumkt, pm, nlev):
    b, n, _ = q.shape
    c = GLA_BLOCK
    nc = n // c
    nb = next(c for c in (GLA_BATCHES_PER_STEP, 2, 1) if b % c == 0)
    specs = []
    for d_ in range(2):
        pos = (lambda s_: s_) if d_ == 0 else (lambda s_: nc - 1 - s_)
        specs += [pl.BlockSpec((nb, c, 256), lambda bi, s_, pos=pos: (bi, pos(s_), 0)),
                  pl.BlockSpec((nb, c, 256), lambda bi, s_, pos=pos: (bi, pos(s_), 0)),
                  pl.BlockSpec((nb, c, 512), lambda bi, s_, pos=pos: (bi, pos(s_), 0)),
                  pl.BlockSpec((nb, c, 256), lambda bi, s_, pos=pos, d_=d_: (bi, pos(s_), d_))]
    st_spec = pl.BlockSpec((nb, 2, GLA_HEADS, GLA_DV, LANES), lambda bi, s_: (bi, 0, 0, 0, 0))
    full = lambda a: pl.BlockSpec(a.shape, lambda bi, s_: (0,) * a.ndim)
    return pl.pallas_call(
        functools.partial(_gla_kernel, nlev, nb),
        out_shape=[jax.ShapeDtypeStruct((b, n, GLA_V), F32), jax.ShapeDtypeStruct((b, n, GLA_V), F32),
                   jax.ShapeDtypeStruct((b, 2, GLA_HEADS, GLA_DV, LANES), F32)],
        grid=(b // nb, nc),
        in_specs=specs + [full(cumq), full(cumkt), full(pm), st_spec],
        out_specs=[pl.BlockSpec((nb, c, GLA_V), lambda bi, s_: (bi, s_, 0)),
                   pl.BlockSpec((nb, c, GLA_V), lambda bi, s_: (bi, nc - 1 - s_, 0)), st_spec],
        scratch_shapes=[pltpu.VMEM((GLA_HEADS, GLA_DV, LANES), F32) for _ in range(2 * nb)],
        compiler_params=_cparams(("parallel", "arbitrary")),
    )(q, k, v, la, q, k, v, la, cumq, cumkt, pm, s0)


def _odd_out_router_kernel(x_ref, of_ref, ob_ref, r_ref, dl_ref, gg_ref, w_ref, mod_ref, gn_ref, wr_ref, u_ref,
                           o_ref, *route_refs):
    d = D_MODEL
    gg = gg_ref[...]

    def tile(rs):
        o = of_ref[rs, :] + ob_ref[rs, :]
        r = r_ref[rs, :]
        parts = []
        for hd in range(GLA_HEADS):
            vs = slice(hd * GLA_DV, (hd + 1) * GLA_DV)
            oh = o[:, vs]
            parts.append(oh * lax.rsqrt(jnp.mean(oh * oh, axis=-1, keepdims=True) + EPS) * gg[:, vs])
        cl = (jnp.concatenate(parts, axis=1) * (r * jax.nn.sigmoid(r))).astype(BF16)
        y = _dot(cl, w_ref[0:512, :]) + _dot(dl_ref[rs, :], w_ref[512:1024, :])
        x = x_ref[rs, :] + mod_ref[:, 2 * d:3 * d] * y
        o_ref[rs, :] = x
        yield
        yield from _route_tile(x, rs, mod_ref, gn_ref, wr_ref, u_ref, *route_refs)

    _run_tiles(route_refs[-1], [tile(slice(t * MOE_TILE, (t + 1) * MOE_TILE)) for t in range(ROUTE_STEP // MOE_TILE)])


def _odd_out_router(x, o_fwd, o_bwd, r, dl, gg, w, mod, gn, wr, u):
    b, n, d = x.shape
    tm = ROUTE_STEP
    flat = lambda t: t.reshape(-1, t.shape[-1])
    act = lambda wd: pl.BlockSpec((tm, wd), lambda i: (i, 0))
    full = lambda t: pl.BlockSpec(t.shape, lambda i: (0,) * t.ndim)
    rshapes, rspecs, rscratch = _route_out(b * n, d)
    outs = pl.pallas_call(
        _odd_out_router_kernel,
        out_shape=[jax.ShapeDtypeStruct((b * n, d), F32)] + rshapes,
        grid=(b * n // tm,),
        in_specs=[act(d), act(GLA_V), act(GLA_V), act(512), act(512), full(gg), full(w),
                  pl.BlockSpec((None, 1, 6 * d), lambda i: (i // (n // tm), 0, 0)), full(gn), full(wr), full(u)],
        out_specs=[act(d)] + rspecs,
        scratch_shapes=rscratch,
        compiler_params=_cparams(("arbitrary",)),
    )(flat(x), flat(o_fwd), flat(o_bwd), flat(r), flat(dl), gg, w, mod, gn, wr, u)
    return outs[0].reshape(x.shape), outs[1:]


def _route_tile(x, rs, mod_ref, gn_ref, wr_ref, u_ref, h_ref, e_ref, wt_ref, r_ref, cnt_ref, carry_ref):
    d = D_MODEL
    tm = x.shape[0]
    mod = mod_ref[...]
    h = _rms(x, gn_ref[...]) * (1.0 + mod[:, 4 * d:5 * d]) + mod[:, 3 * d:4 * d]
    h_ref[rs, :] = _pack_bf16_pairs(h)
    h_hi, h_lo = _split2(h)
    w_hi, w_lo = _split2(wr_ref[...])
    lg = _dot_nt(w_hi, h_hi) + _dot_nt(w_lo, h_hi) + _dot_nt(w_hi, h_lo)
    yield
    rid = lax.broadcasted_iota(I32, (8, tm), 0)
    gl = jnp.where(rid < MOE_GROUPS, lg[0:8], NEG)
    gmax = jnp.max(gl, axis=0, keepdims=True)
    gsel = jnp.min(jnp.where(gl == gmax, rid, 8), axis=0, keepdims=True)
    pmax = 1.0 / jnp.sum(jnp.where(rid < MOE_GROUPS, jnp.exp(gl - gmax), 0.0), axis=0, keepdims=True)
    e_in = jnp.zeros((MOE_PER_GROUP, tm), F32)
    for g in range(MOE_GROUPS):
        e_in = e_in + jnp.where(gsel == g, lg[8 + 8 * g:16 + 8 * g], 0.0)
    v1 = jnp.max(e_in, axis=0, keepdims=True)
    i1 = jnp.min(jnp.where(e_in == v1, rid, 8), axis=0, keepdims=True)
    e_rest = jnp.where(rid == i1, -jnp.inf, e_in)
    v2 = jnp.max(e_rest, axis=0, keepdims=True)
    i2 = jnp.min(jnp.where(e_rest == v2, rid, 8), axis=0, keepdims=True)
    t = jnp.exp(v2 - v1)
    w1 = pmax / (1.0 + t)
    w2 = pmax * t / (1.0 + t)
    e1 = gsel * MOE_PER_GROUP + i1
    e2 = gsel * MOE_PER_GROUP + i2
    eid = lax.broadcasted_iota(I32, (MOE_EXPERTS, tm), 0)
    oh1 = jnp.where(eid == e1, 1.0, 0.0)
    oh2 = jnp.where(eid == e2, 1.0, 0.0)
    ohs = oh1 + oh2
    prefix = _dot(ohs.astype(BF16), u_ref[...])
    e_ref[:, rs] = jnp.concatenate([e1, e2], axis=0)
    w8 = jnp.concatenate([w1, w2, jnp.zeros((6, tm), F32)], axis=0)
    wt_ref[rs, :] = w8.T
    yield
    base = carry_ref[:, 0:1] + prefix
    r1 = jnp.sum(oh1 * base, axis=0, keepdims=True)
    r2 = jnp.sum(oh2 * base, axis=0, keepdims=True)
    carry_ref[...] = carry_ref[...] + jnp.sum(ohs, axis=1, keepdims=True)
    cnt_ref[...] = carry_ref[...]
    r_ref[:, rs] = jnp.concatenate([r1, r2], axis=0).astype(I32)


def _run_tiles(carry_ref, chains):
    @pl.when(pl.program_id(0) == 0)
    def _():
        carry_ref[...] = jnp.zeros_like(carry_ref)

    while chains:
        chains = [ch for ch in chains if next(ch, "done") != "done"]


def _route_out(n, d):
    tm = ROUTE_STEP
    shapes = [jax.ShapeDtypeStruct((n, d // 2), I32), jax.ShapeDtypeStruct((2, n), I32),
              jax.ShapeDtypeStruct((n, 8), F32), jax.ShapeDtypeStruct((2, n), I32),
              jax.ShapeDtypeStruct((MOE_EXPERTS, LANES), F32)]
    specs = [pl.BlockSpec((tm, d // 2), lambda i: (i, 0)), pl.BlockSpec((2, tm), lambda i: (0, i)),
             pl.BlockSpec((tm, 8), lambda i: (i, 0)), pl.BlockSpec((2, tm), lambda i: (0, i)),
             pl.BlockSpec((MOE_EXPERTS, LANES), lambda i: (0, 0))]
    return shapes, specs, [pltpu.VMEM((MOE_EXPERTS, LANES), F32)]


def _even_out_router_kernel(nlat, x_ref, xc_ref, a_ref, ac_ref, b_ref, bc_ref, w_ref, mod_ref, gn_ref, wr_ref, u_ref,
                            xo_ref, xoc_ref, *route_refs):
    d = D_MODEL
    lat = pl.program_id(0) < nlat

    def tile(rs):
        a = jnp.where(lat, a_ref[rs, :], ac_ref[rs, :])
        b = jnp.where(lat, b_ref[rs, :], bc_ref[rs, :])
        y = _dot(a, w_ref[0:512, :]) + _dot(b, w_ref[512:1024, :])
        x = jnp.where(lat, x_ref[rs, :], xc_ref[rs, :]) + mod_ref[:, 2 * d:3 * d] * y

        @pl.when(lat)
        def _():
            xo_ref[rs, :] = x

        @pl.when(jnp.logical_not(lat))
        def _():
            xoc_ref[rs, :] = x

        yield
        yield from _route_tile(x, rs, mod_ref, gn_ref, wr_ref, u_ref, *route_refs)

    _run_tiles(route_refs[-1], [tile(slice(t * MOE_TILE, (t + 1) * MOE_TILE)) for t in range(ROUTE_STEP // MOE_TILE)])


def _even_out_router(x, xc, oa, oac, ob, obc, w, mod, ctx_row, gn, wr, u):
    b, n, d = x.shape
    tm = ROUTE_STEP
    flat = lambda t: t.reshape(-1, t.shape[-1])
    nlat = b * n // tm
    ntok = b * n + xc.shape[0] * xc.shape[1]
    lat = lambda wd: pl.BlockSpec((tm, wd), lambda i: (jnp.minimum(i, nlat - 1), 0))
    ctx = lambda wd: pl.BlockSpec((tm, wd), lambda i: (jnp.maximum(i - nlat, 0), 0))
    full = lambda t: pl.BlockSpec(t.shape, lambda i: (0,) * t.ndim)
    modrow = lambda i: (jnp.where(i < nlat, i // (n // tm), ctx_row), 0, 0)
    rshapes, rspecs, rscratch = _route_out(ntok, d)
    outs = pl.pallas_call(
        functools.partial(_even_out_router_kernel, nlat),
        out_shape=[jax.ShapeDtypeStruct((b * n, d), F32), jax.ShapeDtypeStruct((ntok - b * n, d), F32)] + rshapes,
        grid=(ntok // tm,),
        in_specs=[lat(d), ctx(d), lat(512), ctx(512), lat(512), ctx(512), full(w),
                  pl.BlockSpec((None, 1, 6 * d), modrow), full(gn), full(wr), full(u)],
        out_specs=[lat(d), ctx(d)] + rspecs,
        scratch_shapes=rscratch,
        compiler_params=_cparams(("arbitrary",)),
    )(flat(x), flat(xc), flat(oa), flat(oac), flat(ob), flat(obc), w, mod, gn, wr, u)
    return outs[0].reshape(x.shape), outs[1].reshape(xc.shape), outs[2:]


def _sc_permute_rows(src, dest, scatter):
    rows, d = dest.shape[0], src.shape[1]
    n = rows // 2
    info = plsc.get_sparse_core_info()
    workers = info.num_cores * info.num_subcores
    per_worker = rows // workers
    chunk = next(c for c in SC_CHUNKS if per_worker % c == 0)
    assert rows == per_worker * workers and n % per_worker == 0
    mesh = plsc.VectorSubcoreMesh(core_axis_name="c", subcore_axis_name="s")

    def body(src_hbm, dest_hbm, out_hbm, idx_v, rows_v, sem):
        base = (lax.axis_index("s") * info.num_cores + lax.axis_index("c")) * per_worker

        @pl.loop(0, per_worker // chunk)
        def _(j):
            a0 = base + j * chunk
            pltpu.sync_copy(dest_hbm.at[pl.ds(a0, chunk)], idx_v)
            if scatter:
                t0 = jnp.where(a0 >= n, a0 - n, a0)
                pltpu.sync_copy(src_hbm.at[pl.ds(t0, chunk)], rows_v)
                pltpu.async_copy(rows_v, out_hbm.at[idx_v], sem).wait()
            else:
                pltpu.async_copy(src_hbm.at[idx_v], rows_v, sem).wait()
                pltpu.sync_copy(rows_v, out_hbm.at[pl.ds(a0, chunk)])

    return pl.kernel(
        body, out_type=jax.ShapeDtypeStruct((rows, d), src.dtype), mesh=mesh,
        scratch_types=[pltpu.VMEM((chunk,), I32), pltpu.VMEM((chunk, d), src.dtype), pltpu.SemaphoreType.DMA],
    )(src, dest)


def _gmm_kernel(layer, vt_ref, ve_ref, vlo_ref, vhi_ref, vfirst_ref, vslot_ref, vnext_ref, nv_ref,
                xs_ref, wg_hbm, wu_hbm, wd_hbm, ys_ref, wgs_ref, wus_ref, wds_ref, wgb_ref, wub_ref, wdb_ref, sem):
    del vt_ref
    v = pl.program_id(0)

    def fetch(e, slot):
        return [pltpu.make_async_copy(w.at[layer, e], s.at[slot], sem.at[slot, i])
                for i, (w, s) in enumerate(((wg_hbm, wgs_ref), (wu_hbm, wus_ref), (wd_hbm, wds_ref)))]

    @pl.when(v < nv_ref[0])
    def _():
        @pl.when((v == 0) | (ve_ref[v] != ve_ref[jnp.maximum(v - 1, 0)]))
        def _():
            slot = vslot_ref[v]

            @pl.when(v == 0)
            def _():
                for c in fetch(ve_ref[0], 0):
                    c.start()

            for c in fetch(ve_ref[v], slot):
                c.wait()
            wgb_ref[...] = wgs_ref[slot].astype(BF16)
            wub_ref[...] = wus_ref[slot].astype(BF16)
            wdb_ref[...] = wds_ref[slot].astype(BF16)

            @pl.when(vnext_ref[v] >= 0)
            def _():
                for c in fetch(vnext_ref[v], 1 - slot):
                    c.start()

        x = _unpack_bf16_pairs(xs_ref[...]).astype(BF16)
        g = _dot(x, wgb_ref[...])
        u = _dot(x, wub_ref[...])
        y = _pack_bf16_pairs(_dot((g * jax.nn.sigmoid(g) * u).astype(BF16), wdb_ref[...]))
        row = lax.broadcasted_iota(I32, (y.shape[0], 1), 0)
        mine = (row >= vlo_ref[v]) & (row < vhi_ref[v])

        @pl.when(vfirst_ref[v] == 1)
        def _():
            ys_ref[...] = jnp.where(mine, y, 0)

        @pl.when(vfirst_ref[v] == 0)
        def _():
            ys_ref[...] = jnp.where(mine, y, ys_ref[...])


def _gmm(xs, visits, layer, wg, wu, wd):
    rows, dw = xs.shape
    d = 2 * dw
    tm = MOE_TILE
    hid = wg.shape[-1]
    nvis = rows // tm + MOE_EXPERTS - 1
    tile = lambda v, vt, *_: (vt[v], 0)
    anyspec = pl.BlockSpec(memory_space=pl.ANY)
    return pl.pallas_call(
        functools.partial(_gmm_kernel, layer),
        out_shape=jax.ShapeDtypeStruct((rows, dw), I32),
        grid_spec=pltpu.PrefetchScalarGridSpec(
            num_scalar_prefetch=8, grid=(nvis,),
            in_specs=[pl.BlockSpec((tm, dw), tile), anyspec, anyspec, anyspec],
            out_specs=pl.BlockSpec((tm, dw), tile),
            scratch_shapes=[pltpu.VMEM((2, d, hid), F32), pltpu.VMEM((2, d, hid), F32), pltpu.VMEM((2, hid, d), F32),
                            pltpu.VMEM((d, hid), BF16), pltpu.VMEM((d, hid), BF16), pltpu.VMEM((hid, d), BF16),
                            pltpu.SemaphoreType.DMA((2, 3))]),
        compiler_params=_cparams(("arbitrary",)),
    )(*visits, xs, wg, wu, wd)


def _combine_kernel(x_ref, wt_ref, mod_ref, fg_ref, y0_ref, y1_ref, o_ref):
    d = D_MODEL
    wt = wt_ref[...]
    y = wt[:, 0:1] * _unpack_bf16_pairs(y0_ref[...]) + wt[:, 1:2] * _unpack_bf16_pairs(y1_ref[...])
    o_ref[...] = _rms(x_ref[...] + mod_ref[:, 5 * d:6 * d] * y, fg_ref[...])


def _combine(x2, wt, mod, rows_per_mod, fg, yt, tm=COMBINE_TILE):
    n, d = x2.shape
    tm = min(tm, rows_per_mod)
    assert rows_per_mod % tm == 0
    ntiles = n // tm
    return pl.pallas_call(
        _combine_kernel,
        out_shape=jax.ShapeDtypeStruct((n, d), F32),
        grid=(ntiles,),
        in_specs=[pl.BlockSpec((tm, d), lambda i: (i, 0)), pl.BlockSpec((tm, 8), lambda i: (i, 0)),
                  pl.BlockSpec((None, 1, 6 * d), lambda i: (i // (rows_per_mod // tm), 0, 0)),
                  pl.BlockSpec(fg.shape, lambda i: (0, 0)),
                  pl.BlockSpec((tm, d // 2), lambda i: (i, 0)), pl.BlockSpec((tm, d // 2), lambda i: (ntiles + i, 0))],
        out_specs=pl.BlockSpec((tm, d), lambda i: (i, 0)),
        compiler_params=_cparams(("parallel",)),
    )(x2, wt, mod, fg, yt, yt)


def _pick(table, idx):
    hot = idx[..., None] == jnp.arange(table.shape[0], dtype=I32)
    return jnp.sum(jnp.where(hot, table, 0), axis=-1)


def _moe_plan(counts, e, r, rows):
    tm = MOE_TILE
    ends = jnp.cumsum(counts)
    starts = ends - counts
    dest = (_pick(starts, e) + r).reshape(-1)
    first_tile = starts // tm
    nvis = jnp.where(counts > 0, (ends - 1) // tm - first_tile + 1, 0)
    vend = jnp.cumsum(nvis)
    nv = vend[-1:]
    v = jnp.minimum(jnp.arange(rows // tm + MOE_EXPERTS - 1, dtype=I32), nv[0] - 1)
    ve = jnp.sum((vend[None, :] <= v[:, None]).astype(I32), axis=1)
    vt = _pick(first_tile, ve) + v - _pick(vend - nvis, ve)
    vlo = jnp.maximum(_pick(starts, ve) - vt * tm, 0)
    vhi = jnp.minimum(_pick(ends, ve) - vt * tm, tm)
    vfirst = jnp.concatenate([jnp.ones((1,), I32), (vt[1:] != vt[:-1]).astype(I32)])
    changed = jnp.concatenate([jnp.ones((1,), I32), (ve[1:] != ve[:-1]).astype(I32)])
    vslot = (jnp.cumsum(changed) - 1) % 2
    eid = jnp.arange(MOE_EXPERTS, dtype=I32)
    later = (eid[None, :] > eid[:, None]) & (counts[None, :] > 0)
    nxt = jnp.min(jnp.where(later, eid[None, :], MOE_EXPERTS), axis=1)
    vnext = _pick(jnp.where(nxt < MOE_EXPERTS, nxt, -1), ve)
    return dest, (vt, ve, vlo, vhi, vfirst, vslot.astype(I32), vnext.astype(I32), nv)


def _moe_experts(routed, layer, wg, wu, wd):
    h, e, wt, r, cnt = routed
    dest, visits = _moe_plan(cnt[:, 0].astype(I32), e, r, 2 * h.shape[0])
    xs = _sc_permute_rows(h, dest, scatter=True)
    ys = _gmm(xs, visits, layer, wg, wu, wd)
    return _sc_permute_rows(ys, dest, scatter=False), wt


def _rope_tables(rows, dim):
    row = jnp.repeat(jnp.arange(rows, dtype=F32), GRID_W)
    col = jnp.tile(jnp.arange(GRID_W, dtype=F32), rows)
    half = dim // 2
    inv = jnp.power(ROPE_BASE, -jnp.arange(0, half, 2, dtype=F32) / half)
    ar = row[:, None] * inv[None, :]
    ac = col[:, None] * inv[None, :]
    ang = jnp.concatenate([ar, ar, ac, ac], axis=-1)
    return jnp.cos(ang), jnp.sin(ang)


def _even_tables(n, with_rope):
    if with_rope:
        cm, sm = _rope_tables(n // GRID_W, MLA_ROPE)
        cwin, swin = _rope_tables(n // GRID_W, WIN_HEAD_DIM)
    else:
        cm, sm = jnp.ones((n, MLA_ROPE), F32), jnp.zeros((n, MLA_ROPE), F32)
        cwin, swin = jnp.ones((n, WIN_HEAD_DIM), F32), jnp.zeros((n, WIN_HEAD_DIM), F32)
    one, zero = jnp.ones((n, MLA_NOPE), F32), jnp.zeros((n, MLA_NOPE), F32)
    pad = jnp.zeros((n, LANES - MLA_NOPE - MLA_ROPE), F32)
    return (jnp.concatenate([one, cm, pad], axis=1), jnp.concatenate([zero, sm, pad], axis=1),
            jnp.concatenate([cwin, cwin], axis=1), jnp.concatenate([swin, swin], axis=1))


def _even_weights(w_in, qg, w_uq, kvg, w_ukv):
    d = w_in.shape[0]
    o = np.cumsum([0, MLA_Q_RANK, MLA_KV_RANK, MLA_ROPE, 512, 128, 128])
    cq, ckv, kr, qw, kw, vw = [w_in[:, o[i]:o[i + 1]] for i in range(6)]
    z = lambda c: jnp.zeros((d, c), F32)
    kr128 = jnp.concatenate([z(MLA_NOPE), kr, z(LANES - MLA_NOPE - MLA_ROPE)], axis=1)
    dup = lambda t: jnp.concatenate([t[:, 0:64], t[:, 0:64], t[:, 64:128], t[:, 64:128]], axis=1)
    win = jnp.concatenate([cq, ckv, kr128, qw, dup(kw), dup(vw)], axis=1).astype(BF16)
    uq = w_uq.reshape(MLA_Q_RANK, MLA_HEADS, MLA_NOPE + MLA_ROPE)
    uq = jnp.pad(uq, ((0, 0), (0, 0), (0, LANES - MLA_NOPE - MLA_ROPE))).reshape(MLA_Q_RANK, MLA_HEADS * LANES)
    ukv = w_ukv.reshape(MLA_KV_RANK, MLA_HEADS, MLA_NOPE + MLA_V)
    ukk = jnp.pad(ukv[:, :, :MLA_NOPE], ((0, 0), (0, 0), (0, LANES - MLA_NOPE))).reshape(MLA_KV_RANK, MLA_HEADS * LANES)
    ukvv = ukv[:, :, MLA_NOPE:].reshape(MLA_KV_RANK, MLA_HEADS * MLA_V)
    return (win, qg.reshape(1, -1), uq.astype(BF16), kvg.reshape(1, -1), ukk.astype(BF16), ukvv.T.astype(BF16))


def _odd_weights(w_in, w_g2, b_g, ln_g, ln_b, w_s, b_s):
    d = w_in.shape[0]
    o = np.cumsum([0, GLA_K, GLA_K, GLA_V, 2 * GLA_GATE_RANK, GLA_V, SG_WIDTH, SG_WIDTH])
    q, k, v, g, r, u, vg = [w_in[:, o[i]:o[i + 1]] for i in range(7)]
    g128 = jnp.concatenate([g, jnp.zeros((d, LANES - 2 * GLA_GATE_RANK), F32)], axis=1)
    win = jnp.concatenate([q, k, v, g128, r, u, vg], axis=1).astype(BF16)
    zr = jnp.zeros((GLA_GATE_RANK, GLA_K), F32)
    pad = jnp.zeros((LANES - 2 * GLA_GATE_RANK, GLA_K), F32)
    wg = jnp.concatenate([jnp.concatenate([w_g2[0], zr, pad], axis=0),
                          jnp.concatenate([zr, w_g2[1], pad], axis=0)], axis=1)
    bg = b_g.reshape(1, 2 * GLA_K)
    return (win, wg, bg, ln_g.reshape(1, -1), ln_b.reshape(1, -1), w_s.astype(BF16), b_s.T)


def kernel(x, c, ctx, c_ctx, ada_w, ada_b, norm_mix_g, norm_ffn_g, even_w_in, mla_q_norm_g, mla_w_uq, mla_kv_norm_g, mla_w_ukv, win_sink, even_w_out, odd_w_in, gla_w_g2, gla_b_g, gla_norm_g, sg_ln_g, sg_ln_b, sg_w_s, sg_b_s, odd_w_out, moe_w_rg, moe_w_re, moe_w_gate, moe_w_up, moe_w_down, final_norm_g):
    b, n, d = x.shape
    lc = ctx.shape[1]
    depth = ada_w.shape[0]
    assert depth == 2 and d == D_MODEL and b < 8
    assert n % 512 == 0 and lc % MOE_TILE == 0 and n % GRID_W == 0
    tm = 512 if n % 512 == 0 else 256
    tq = 256

    cond8 = jnp.concatenate([c, c_ctx[None, :], jnp.zeros((8 - b - 1, d), F32)], axis=0)
    mod_all = _adaln(cond8, ada_w, ada_b).reshape(depth, 8, 1, 6 * d)
    ctx_row = b
    u_tri = jnp.asarray(np.triu(np.ones((MOE_TILE, MOE_TILE), np.float32), 1), BF16)
    fg = final_norm_g.reshape(1, d)

    def router_w(layer):
        return jnp.concatenate([moe_w_rg[layer].T, jnp.zeros((8 - MOE_GROUPS, d), F32), moe_w_re[layer].T], axis=0)

    def moe_experts(routed, layer):
        return _moe_experts(routed, layer, moe_w_gate, moe_w_up, moe_w_down)

    mod = mod_all[0]
    gn = norm_mix_g[0].reshape(1, d)
    ew = _even_weights(even_w_in[0], mla_q_norm_g[0], mla_w_uq[0], mla_kv_norm_g[0], mla_w_ukv[0])
    qm_l, km_l, vm_l, qw_l, kw_l, vw_l = _even_in(x, mod, None, gn, ew, _even_tables(n, True), tm)
    qm_c, km_c, vm_c, qw_c, kw_c, vw_c = _even_in(ctx, mod, ctx_row, gn, ew, _even_tables(lc, False), lc)
    w_out = even_w_out[0].astype(BF16)
    sink = win_sink[0]
    oa_l = _mla_attn(qm_l, [(km_l, vm_l), (km_c, vm_c)], tq)
    ob_l = _gqa(qw_l, kw_l, vw_l, kw_c, vw_c, sink, True)
    oa_c = _mla_attn(qm_c, [(km_c, vm_c)], lc)
    ob_c = _gqa(qw_c, None, None, kw_c, vw_c, sink, False)
    xl, xc, routed = _even_out_router(x, ctx, oa_l, oa_c, ob_l, ob_c, w_out, mod, ctx_row,
                                      norm_ffn_g[0].reshape(1, d), router_w(0), u_tri)
    pending = moe_experts(routed, 0)

    mod = mod_all[1]
    gn = norm_mix_g[1].reshape(1, d)
    ow = _odd_weights(odd_w_in[0], gla_w_g2[0], gla_b_g[0], sg_ln_g[0], sg_ln_b[0], sg_w_s[0], sg_b_s[0])
    xl, q_l, k_l, v_l, la_l, r_l, dl_l = _odd_in(xl, pending, 0, mod_all[0], mod, None, gn, ow)
    _, q_c, k_c, v_c, la_c, _, _ = _odd_in(xc, pending, b * n // MOE_TILE, mod_all[0], mod, ctx_row, gn, ow)
    cumq_np, cumkt_np, pm_np, nlev = _gla_tables()
    cumq, cumkt = jnp.asarray(cumq_np, BF16), jnp.asarray(cumkt_np, BF16)
    pm = jnp.asarray(pm_np, F32)
    s0 = jnp.zeros((b, 2, GLA_HEADS, GLA_DV, LANES), F32)
    _, _, s_ctx = _gla(q_c, k_c, v_c, la_c, s0, cumq, cumkt, pm, nlev)
    o_fwd, o_bwd, _ = _gla(q_l, k_l, v_l, la_l, s_ctx, cumq, cumkt, pm, nlev)
    xl, routed = _odd_out_router(xl, o_fwd, o_bwd, r_l, dl_l, gla_norm_g[0].reshape(1, -1),
                                 odd_w_out[0].astype(BF16), mod, norm_ffn_g[1].reshape(1, d), router_w(1), u_tri)
    yt, wt = moe_experts(routed, 1)
    return _combine(xl.reshape(b * n, d), wt, mod, n, fg, yt).reshape(b, n, d)
```

```python
import functools

import numpy as np
import jax
import jax.numpy as jnp
from jax import lax
from jax.experimental import pallas as pl
from jax.experimental.pallas import tpu as pltpu
from jax.experimental.pallas import tpu_sc as plsc

F32 = jnp.float32
BF16 = jnp.bfloat16
I32 = jnp.int32

D_MODEL = 1024
GRID_W = 64
EPS = 1e-6
ROPE_BASE = 10000.0
MLA_HEADS = 8
MLA_Q_RANK = 256
MLA_KV_RANK = 128
MLA_NOPE = 64
MLA_ROPE = 32
MLA_V = 64
WIN_HEADS = 8
WIN_KV_HEADS = 2
WIN_HEAD_DIM = 64
WIN_BLOCK = 128
GLA_HEADS = 4
GLA_DK = 64
GLA_DV = 128
GLA_GATE_RANK = 16
GLA_TAU = 16.0
GLA_K = GLA_HEADS * GLA_DK
GLA_V = GLA_HEADS * GLA_DV
SG_GROUPS = 4
SG_CHUNK = 128
SG_WIDTH = 512
MOE_GROUPS = 4
MOE_PER_GROUP = 8
MOE_EXPERTS = 32
MOE_HIDDEN = 512

LANES = 128
GLA_BLOCK = 128
GLA_LEVELS_PER_PHASE = 2
GLA_BATCHES_PER_STEP = 2
MOE_TILE = 256
EVEN_IN_STEP = 1024
ODD_IN_STEP = 1024
ROUTE_STEP = 1024
COMBINE_TILE = 1024
MLA_KEY_CHUNK = 1024
SC_CHUNKS = (128, 64, 32)
NEG = -1e30
LOG2E = 1.4426950408889634
VMEM_LIMIT = 56 * 1024 * 1024


def _cparams(sem):
    return pltpu.CompilerParams(dimension_semantics=sem, vmem_limit_bytes=VMEM_LIMIT)


def _dot(a, b):
    return jnp.dot(a, b, preferred_element_type=F32)


def _dot_nt(a, b):
    return lax.dot_general(a, b, (((1,), (1,)), ((), ())), preferred_element_type=F32)


def _split2(a):
    hi = a.astype(BF16)
    lo = (a - hi.astype(F32)).astype(BF16)
    return hi, lo


def _pack_bf16_pairs(x):
    k = x.shape[1] // 2
    bits = lax.bitcast_convert_type(x.astype(BF16).astype(F32), jnp.uint32)
    return lax.bitcast_convert_type(bits[:, :k] | (bits[:, k:] >> 16), I32)


def _unpack_bf16_pairs(w):
    bits = lax.bitcast_convert_type(w, jnp.uint32)
    hi = lax.bitcast_convert_type(bits & jnp.uint32(0xFFFF0000), F32)
    lo = lax.bitcast_convert_type(bits << 16, F32)
    return jnp.concatenate([hi, lo], axis=1)


def _rms(x, g):
    ms = jnp.mean(x * x, axis=-1, keepdims=True)
    return x * lax.rsqrt(ms + EPS) * g


def _lane_tile(t, reps):
    return t if reps == 1 else jnp.concatenate([t] * reps, axis=1)


def _rope(t, cos, sin, quarter):
    n = t.shape[1]
    lane = lax.broadcasted_iota(I32, t.shape, 1)
    first = (lane & (2 * quarter - 1)) < quarter
    rot = jnp.where(first, -pltpu.roll(t, n - quarter, 1), pltpu.roll(t, quarter, 1))
    return t * cos + rot * sin


def _adaln_kernel(c_ref, w_ref, b_ref, o_ref):
    c = c_ref[...]
    s_hi, s_lo = _split2(c * jax.nn.sigmoid(c))
    w_hi, w_lo = _split2(w_ref[...])
    o_ref[...] = _dot(s_hi, w_hi) + _dot(s_lo, w_hi) + _dot(s_hi, w_lo) + b_ref[...]


def _adaln(cond8, ada_w, ada_b):
    depth, d, n6 = ada_w.shape
    tn = 1536
    return pl.pallas_call(
        _adaln_kernel,
        out_shape=jax.ShapeDtypeStruct((depth, 8, n6), F32),
        grid=(depth, n6 // tn),
        in_specs=[
            pl.BlockSpec((8, d), lambda l, j: (0, 0)),
            pl.BlockSpec((None, d, tn), lambda l, j: (l, 0, j)),
            pl.BlockSpec((None, 1, tn), lambda l, j: (l, 0, j)),
        ],
        out_specs=pl.BlockSpec((None, 8, tn), lambda l, j: (l, 0, j)),
        compiler_params=_cparams(("parallel", "parallel")),
    )(cond8, ada_w, ada_b.reshape(depth, 1, n6))


def _even_in_kernel(x_ref, mod_ref, gn_ref, win_ref, qg_ref, wuq_ref, kvg_ref, wukk_ref, wukv_ref,
                    cq_ref, sq_ref, cw_ref, sw_ref,
                    qm_ref, km_ref, vm_ref, qw_ref, kw_ref, vw_ref):
    d = D_MODEL
    mod = mod_ref[...]

    def tile(rs):
        h = _rms(x_ref[rs, :], gn_ref[...]) * (1.0 + mod[:, d:2 * d]) + mod[:, 0:d]
        z = _dot(h.astype(BF16), win_ref[...])
        cq, sq, cw, sw = cq_ref[rs, :], sq_ref[rs, :], cw_ref[rs, :], sw_ref[rs, :]
        yield
        cqn = _rms(z[:, 0:256], qg_ref[...]).astype(BF16)
        q = _dot(cqn, wuq_ref[...])
        ckvn = _rms(z[:, 256:384], kvg_ref[...]).astype(BF16)
        kn = _dot(ckvn, wukk_ref[...])
        vm_ref[:, rs] = _dot_nt(wukv_ref[...], ckvn).astype(BF16)
        yield
        q = _rope(q, _lane_tile(cq, 8), _lane_tile(sq, 8), MLA_ROPE // 4)
        qm_ref[rs, :] = (q * (LOG2E * (MLA_NOPE + MLA_ROPE) ** -0.5)).astype(BF16)
        kr = _rope(z[:, 384:512], cq, sq, MLA_ROPE // 4)
        km_ref[rs, :] = (kn + _lane_tile(kr, 8)).astype(BF16)
        yield
        qw = _rope(z[:, 512:1024], _lane_tile(cw, 4), _lane_tile(sw, 4), WIN_HEAD_DIM // 4)
        qw_ref[rs, :] = (qw * (WIN_HEAD_DIM ** -0.5)).astype(BF16)
        kw = _rope(z[:, 1024:1280], _lane_tile(cw, 2), _lane_tile(sw, 2), WIN_HEAD_DIM // 4)
        kw_ref[rs, :] = kw.astype(BF16)
        vw_ref[rs, :] = z[:, 1280:1536].astype(BF16)

    rows = min(MOE_TILE, x_ref.shape[0])
    chains = [tile(slice(r0, r0 + rows)) for r0 in range(0, x_ref.shape[0], rows)]
    while chains:
        chains = [ch for ch in chains if next(ch, "done") != "done"]


def _even_in(x, mod, mod_row, gn, wts, tabs, tm):
    b, n, d = x.shape
    win, qg, wuq, kvg, wukk, wukv = wts
    nt = n // tm
    row = (lambda bi, i: (bi, 0, 0)) if mod_row is None else (lambda bi, i: (mod_row, 0, 0))
    full = lambda a: pl.BlockSpec(a.shape, lambda bi, i: (0,) * a.ndim)
    tab = pl.BlockSpec((tm, LANES), lambda bi, i: (i, 0))
    outw = (1024, 1024, None, 512, 256, 256)
    rowspec = lambda w: pl.BlockSpec((None, tm, w), lambda bi, i: (bi, i, 0))
    colspec = pl.BlockSpec((None, 512, tm), lambda bi, i: (bi, 0, i))
    return pl.pallas_call(
        _even_in_kernel,
        out_shape=[jax.ShapeDtypeStruct((b, 512, n) if w is None else (b, n, w), BF16) for w in outw],
        grid=(b, nt),
        in_specs=[pl.BlockSpec((None, tm, d), lambda bi, i: (bi, i, 0)),
                  pl.BlockSpec((None, 1, 6 * d), row),
                  full(gn), full(win), full(qg), full(wuq), full(kvg), full(wukk), full(wukv),
                  tab, tab, tab, tab],
        out_specs=[colspec if w is None else rowspec(w) for w in outw],
        compiler_params=_cparams(("parallel", "parallel")),
    )(x, mod, gn, win, qg, wuq, kvg, wukk, wukv, *tabs)


def _mla_attn_kernel(nseg, q_ref, *refs):
    ks, vts = refs[0:2 * nseg:2], refs[1:2 * nseg:2]
    o_ref = refs[2 * nseg]
    s_bufs = refs[2 * nseg + 1:2 * nseg + 3]
    p_bufs = refs[2 * nseg + 3:2 * nseg + 5]
    pieces, base = [], 0
    for k in ks:
        n = k.shape[0]
        pieces += [(k, c0, min(n, c0 + MLA_KEY_CHUNK), base + c0) for c0 in range(0, n, MLA_KEY_CHUNK)]
        base += n

    def score_chunk(h, piece, buf):
        k, c0, c1, g0 = piece
        hs = slice(h * LANES, (h + 1) * LANES)
        s = _dot_nt(k[c0:c1, hs], q_ref[:, hs])
        buf[g0:g0 + c1 - c0, :] = s
        return jnp.max(s, axis=0, keepdims=True)

    def prob_chunk(piece, sbuf, pbuf, m):
        _, c0, c1, g0 = piece
        p = jnp.exp2(sbuf[g0:g0 + c1 - c0, :] - m)
        pbuf[g0:g0 + c1 - c0, :] = p.astype(BF16)
        return jnp.sum(p, axis=0, keepdims=True)

    m_next = functools.reduce(jnp.maximum, [score_chunk(0, pc, s_bufs[0]) for pc in pieces])
    outs = []
    for h in range(MLA_HEADS):
        m_cur, maxes, sums = m_next, [], []
        for pc in pieces:
            if h + 1 < MLA_HEADS:
                maxes.append(score_chunk(h + 1, pc, s_bufs[(h + 1) % 2]))
            sums.append(prob_chunk(pc, s_bufs[h % 2], p_bufs[h % 2], m_cur))
        if h + 1 < MLA_HEADS:
            m_next = functools.reduce(jnp.maximum, maxes)
        l = functools.reduce(jnp.add, sums)
        vrows = slice(h * MLA_V, (h + 1) * MLA_V)
        ot, base = None, 0
        for k, vt in zip(ks, vts):
            n = k.shape[0]
            part = _dot(vt[vrows, :], p_bufs[h % 2][base:base + n, :])
            ot = part if ot is None else ot + part
            base += n
        outs.append(ot * (1.0 / l))
    o_ref[...] = jnp.concatenate(outs, axis=0).T.astype(BF16)


def _mla_attn(q, segs, tq):
    b, n, _ = q.shape
    in_specs = [pl.BlockSpec((None, tq, 1024), lambda bi, i: (bi, i, 0))]
    args = [q]
    keys = 0
    for k, vt in segs:
        lk = k.shape[1]
        keys += lk
        in_specs += [pl.BlockSpec((None, lk, 1024), lambda bi, i: (bi, 0, 0)),
                     pl.BlockSpec((None, 512, lk), lambda bi, i: (bi, 0, 0))]
        args += [k, vt]
    return pl.pallas_call(
        functools.partial(_mla_attn_kernel, len(segs)),
        out_shape=jax.ShapeDtypeStruct((b, n, 512), BF16),
        grid=(b, n // tq),
        in_specs=in_specs,
        out_specs=pl.BlockSpec((None, tq, 512), lambda bi, i: (bi, i, 0)),
        scratch_shapes=[pltpu.VMEM((keys, tq), F32), pltpu.VMEM((keys, tq), F32),
                        pltpu.VMEM((keys, tq), BF16), pltpu.VMEM((keys, tq), BF16)],
        compiler_params=_cparams(("parallel", "parallel")),
    )(*args)


def _gqa_kernel(has_win, nb, sink_ref, q_ref, *refs):
    if has_win:
        kp, kc, kn, vp, vc, vn, kx, vx, o_ref = refs
    else:
        kx, vx, o_ref = refs
    tq = q_ref.shape[0]
    i = pl.program_id(1)
    lane = lax.broadcasted_iota(I32, (tq, LANES), 1)
    row2 = lax.broadcasted_iota(I32, (2 * tq, 1), 0)
    half = WIN_HEAD_DIM
    npair = WIN_HEADS // 2
    kcats, vcats = [], []
    for g in range(WIN_KV_HEADS):
        gs = slice(g * LANES, (g + 1) * LANES)
        if has_win:
            kcats.append(jnp.concatenate([kp[:, gs], kc[:, gs], kn[:, gs], kx[:, gs]], axis=0))
            vcats.append(jnp.concatenate([vp[:, gs], vc[:, gs], vn[:, gs], vx[:, gs]], axis=0))
        else:
            kcats.append(kx[:, gs])
            vcats.append(vx[:, gs])
    if has_win:
        w = WIN_BLOCK
        shape = (2 * tq, kcats[0].shape[0])
        r = lax.broadcasted_iota(I32, shape, 0) & (tq - 1)
        c = lax.broadcasted_iota(I32, shape, 1)
        big = jnp.int32(1 << 20)
        no_prev = jnp.where(i > 0, 0, big)
        no_next = jnp.where(i < nb - 1, 0, big)
        ok_prev = c >= r + no_prev
        ok_next = (c - 2 * w) <= r - no_next
        valid = ((c >= w) | ok_prev) & ((c < 2 * w) | (c >= 3 * w) | ok_next)

    def score(j):
        qp = q_ref[:, j * LANES:(j + 1) * LANES]
        zero = jnp.zeros_like(qp)
        q2 = jnp.concatenate([jnp.where(lane < half, qp, zero), jnp.where(lane >= half, qp, zero)], axis=0)
        return _dot_nt(q2, kcats[j // 2])

    def softmax(j, s):
        s = jnp.where(valid, s, NEG) if has_win else s
        sk = jnp.where(row2 < tq, sink_ref[2 * j], sink_ref[2 * j + 1])
        m = jnp.maximum(jnp.max(s, axis=-1, keepdims=True), sk)
        p = jnp.exp(s - m)
        return p.astype(BF16), 1.0 / (jnp.sum(p, axis=-1, keepdims=True) + jnp.exp(sk - m))

    def values(j, p, inv):
        o2 = _dot(p, vcats[j // 2]) * inv
        o_ref[:, j * LANES:(j + 1) * LANES] = jnp.where(lane < half, o2[:tq], o2[tq:]).astype(BF16)

    s_val, p_val = {}, {}
    for t in range(npair + 2):
        if t < npair:
            s_val[t] = score(t)
        if 0 <= t - 1 < npair:
            p_val[t - 1] = softmax(t - 1, s_val.pop(t - 1))
        if 0 <= t - 2 < npair:
            values(t - 2, *p_val.pop(t - 2))


def _gqa(q, k, v, kx, vx, sink, has_win):
    b, n, _ = q.shape
    lc = kx.shape[1]
    smem = pl.BlockSpec(memory_space=pltpu.SMEM)
    ctxs = pl.BlockSpec((None, lc, 256), lambda bi, i: (bi, 0, 0))
    if has_win:
        tq = WIN_BLOCK
        nb = n // tq
        blk = lambda f: pl.BlockSpec((None, tq, 256), f)
        prev = lambda bi, i: (bi, jnp.maximum(i - 1, 0), 0)
        cur = lambda bi, i: (bi, i, 0)
        nxt = lambda bi, i: (bi, jnp.minimum(i + 1, nb - 1), 0)
        in_specs = [smem, pl.BlockSpec((None, tq, 512), cur),
                    blk(prev), blk(cur), blk(nxt), blk(prev), blk(cur), blk(nxt), ctxs, ctxs]
        args = (sink, q, k, k, k, v, v, v, kx, vx)
    else:
        tq, nb = n, 1
        in_specs = [smem, pl.BlockSpec((None, tq, 512), lambda bi, i: (bi, i, 0)), ctxs, ctxs]
        args = (sink, q, kx, vx)
    return pl.pallas_call(
        functools.partial(_gqa_kernel, has_win, nb),
        out_shape=jax.ShapeDtypeStruct((b, n, 512), BF16),
        grid=(b, nb),
        in_specs=in_specs,
        out_specs=pl.BlockSpec((None, tq, 512), lambda bi, i: (bi, i, 0)),
        compiler_params=_cparams(("parallel", "parallel")),
    )(*args)


def _log_sigmoid(z):
    return jnp.minimum(z, 0.0) - jnp.log(1.0 + jnp.exp(-jnp.abs(z)))


def _odd_in_kernel(x_ref, y0_ref, y1_ref, wt_ref, modp_ref, mod_ref, gn_ref, win_ref, wg_ref, bg_ref, lng_ref,
                   lnb_ref, ws_ref, bst_ref, xn_ref, q_ref, k_ref, v_ref, la_ref, r_ref, dl_ref):
    d = D_MODEL
    mod = mod_ref[...]
    bst = bst_ref[...]

    def tile(r0, tm):
        rs = slice(r0, r0 + tm)
        wt = wt_ref[rs, :]
        y = wt[:, 0:1] * _unpack_bf16_pairs(y0_ref[rs, :]) + wt[:, 1:2] * _unpack_bf16_pairs(y1_ref[rs, :])
        x = x_ref[rs, :] + modp_ref[:, 5 * d:6 * d] * y
        xn_ref[rs, :] = x
        h = (_rms(x, gn_ref[...]) * (1.0 + mod[:, d:2 * d]) + mod[:, 0:d]).astype(BF16)
        z = _dot(h, win_ref[...])
        yield
        q_ref[rs, :] = z[:, 0:256] * (GLA_DK ** -0.5)
        k_ref[rs, :] = z[:, 256:512]
        v_ref[rs, :] = z[:, 512:1024].astype(BF16)
        g_hi, g_lo = _split2(z[:, 1024:1152])
        w_hi, w_lo = _split2(wg_ref[...])
        zg = _dot(g_hi, w_hi) + _dot(g_lo, w_hi) + _dot(g_hi, w_lo) + bg_ref[...]
        la_ref[rs, :] = _log_sigmoid(zg) / GLA_TAU
        r_ref[rs, :] = z[:, 1152:1664]
        yield
        u = jax.nn.gelu(z[:, 1664:2176])
        vg = jax.nn.gelu(z[:, 2176:2688])
        mu = jnp.mean(vg, axis=-1, keepdims=True)
        vc = vg - mu
        var = jnp.mean(vc * vc, axis=-1, keepdims=True)
        vn = (vc * lax.rsqrt(var + EPS) * lng_ref[...] + lnb_ref[...]).astype(BF16)
        yield
        for c in range(tm // SG_CHUNK):
            rows = slice(c * SG_CHUNK, (c + 1) * SG_CHUNK)
            parts = []
            for g in range(SG_GROUPS):
                cols = slice(g * LANES, (g + 1) * LANES)
                parts.append(_dot(ws_ref[g], vn[rows, cols]) + bst[:, g:g + 1])
            dl_ref[r0 + c * SG_CHUNK:r0 + (c + 1) * SG_CHUNK, :] = (
                u[rows, :] * jnp.concatenate(parts, axis=1)).astype(BF16)

    tm = min(MOE_TILE, x_ref.shape[0])
    chains = [tile(r0, tm) for r0 in range(0, x_ref.shape[0], tm)]
    while chains:
        chains = [ch for ch in chains if next(ch, "done") != "done"]


def _odd_in(x, pending, tile0, modp, mod, mod_row, gn, wts):
    b, n, d = x.shape
    tm = min(ODD_IN_STEP, n)
    tile0 = tile0 * MOE_TILE // tm
    yt, wt = pending
    ntiles = wt.shape[0] // tm
    win, wg, bg, lng, lnb, ws, bst = wts
    row = (lambda bi, i: (bi, 0, 0)) if mod_row is None else (lambda bi, i: (mod_row, 0, 0))
    full = lambda a: pl.BlockSpec(a.shape, lambda bi, i: (0,) * a.ndim)
    act = lambda wd: pl.BlockSpec((None, tm, wd), lambda bi, i: (bi, i, 0))
    tok = lambda bi, i: tile0 + bi * (n // tm) + i
    outs = [((b, n, d), F32, act(d)),
            ((b, n, 256), F32, act(256)), ((b, n, 256), F32, act(256)), ((b, n, 512), BF16, act(512)),
            ((b, n, 512), F32, act(512)), ((b, n, 512), F32, act(512)), ((b, n, 512), BF16, act(512))]
    return pl.pallas_call(
        _odd_in_kernel,
        out_shape=[jax.ShapeDtypeStruct(s, t) for s, t, _ in outs],
        grid=(b, n // tm),
        in_specs=[act(d), pl.BlockSpec((tm, d // 2), lambda bi, i: (tok(bi, i), 0)),
                  pl.BlockSpec((tm, d // 2), lambda bi, i: (ntiles + tok(bi, i), 0)),
                  pl.BlockSpec((tm, 8), lambda bi, i: (tok(bi, i), 0)),
                  pl.BlockSpec((None, 1, 6 * d), row), pl.BlockSpec((None, 1, 6 * d), row),
                  full(gn), full(win), full(wg), full(bg), full(lng), full(lnb), full(ws), full(bst)],
        out_specs=[sp for _, _, sp in outs],
        compiler_params=_cparams(("parallel", "parallel")),
    )(x, yt, yt, wt, modp, mod, gn, win, wg, bg, lng, lnb, ws, bst)


def _gla_tables():
    c = GLA_BLOCK
    t = np.arange(c)[:, None]
    u = np.arange(c)[None, :]
    levels = [c >> i for i in range(int(np.log2(c)) + 1)]
    cum = np.zeros((2, 2 * len(levels), c, c), np.float32)
    pair = np.zeros((2, len(levels), c, c), np.float32)
    for li, m in enumerate(levels):
        same = (t // m) == (u // m)
        cum[0, 2 * li] = same & (u <= t)
        cum[0, 2 * li + 1] = same & (u > t)
        cum[1, 2 * li] = same & (u >= t)
        cum[1, 2 * li + 1] = same & (u < t)
        if li > 0:
            pair[0, li] = ((t // m) % 2 == 1) & ((u // m) == (t // m) - 1)
            pair[1, li] = ((t // m) % 2 == 0) & ((u // m) == (t // m) + 1)
    pair[:, 0] = np.eye(c, dtype=np.float32)
    nlev = len(levels)
    m1 = cum[:, 0::2].reshape(2, nlev * c, c)
    m2t = np.concatenate([cum[:, 2 * li + 1].transpose(0, 2, 1) for li in range(nlev)], axis=2)
    return np.concatenate([m1, m1], axis=2), np.concatenate([m2t, m2t], axis=1), pair, nlev


def _gla_chain(nlev, q, k, la, v_ref, cumq, cumkt, pm_ref, st_ref, o_ref):
    c = GLA_BLOCK
    lat, kt = la.T, k.T
    l_hi, l_mid = _split2(la)
    t_hi, t_mid = _split2(lat)
    exq = jnp.exp(_dot(cumq, jnp.concatenate([l_hi, l_mid], axis=0)))
    exk = jnp.exp(_dot(jnp.concatenate([t_hi, t_mid], axis=1), cumkt))
    gcol = jnp.exp(jnp.sum(lat, axis=1, keepdims=True))
    yield
    qe, ke = [], []
    for li in range(nlev):
        qe.append((q * exq[li * c:(li + 1) * c]).astype(BF16))
        ke.append((kt * exk[:, li * c:(li + 1) * c]).astype(BF16))
        if li % 2 == 1:
            yield
    qb, kb = q.astype(BF16), kt.astype(BF16)
    states = [st_ref[hd] for hd in range(GLA_HEADS)]
    yield
    outs, new_states = [], []
    lane = lax.broadcasted_iota(I32, (c, LANES), 1)
    srow = lax.broadcasted_iota(I32, (LANES, 1), 0)
    zero = jnp.zeros((c, LANES), BF16)
    for hd in range(GLA_HEADS):
        ps = slice((hd // 2) * LANES, (hd // 2 + 1) * LANES)
        vs = slice(hd * GLA_DV, (hd + 1) * GLA_DV)
        mine = (lane < GLA_DK) if hd % 2 == 0 else (lane >= GLA_DK)
        mine_row = (srow < GLA_DK) if hd % 2 == 0 else (srow >= GLA_DK)
        pick = lambda t: jnp.where(mine, t[:, ps], zero)
        a = pm_ref[0] * _dot(pick(qb), kb[ps, :])
        for li in range(1, nlev):
            a = a + pm_ref[li] * _dot(pick(qe[li]), ke[li][ps, :])
            if li % GLA_LEVELS_PER_PHASE == 0:
                yield
        v_h = v_ref[:, vs]
        outs.append(_dot(qe[0][:, ps], states[hd].astype(BF16)) + _dot(a.astype(BF16), v_h))
        new_states.append(states[hd] * gcol[ps, :] + jnp.where(mine_row, _dot(ke[0][ps, :], v_h), 0.0))
        yield
    o_ref[...] = jnp.concatenate(outs, axis=1)
    for hd in range(GLA_HEADS):
        st_ref[hd] = new_states[hd]
    yield


def _gla_kernel(nlev, nb, *refs):
    ins_f, ins_b = refs[0:4], refs[4:8]
    cumq_ref, cumkt_ref, pm_ref, s0_ref, of_ref, ob_ref, sf_ref = refs[8:15]
    st_refs = refs[15:]
    step = pl.program_id(1)

    @pl.when(step == 0)
    def _():
        for bb in range(nb):
            for d_ in range(2):
                st_refs[2 * bb + d_][...] = s0_ref[bb, d_]

    chains = []
    for bb in range(nb):
        for d_, (ins, o_ref) in enumerate(((ins_f, of_ref), (ins_b, ob_ref))):
            q_ref, k_ref, v_ref, la_ref = ins
            chains.append(_gla_chain(nlev, q_ref[bb], k_ref[bb], la_ref[bb], v_ref.at[bb], cumq_ref[d_],
                                     cumkt_ref[d_], pm_ref.at[d_], st_refs[2 * bb + d_], o_ref.at[bb]))
    while chains:
        chains = [ch for ch in chains if next(ch, "done") != "done"]
    for bb in range(nb):
        for d_ in range(2):
            sf_ref[bb, d_] = st_refs[2 * bb + d_][...]


def _gla(q, k, v, la, s0, cumq, cumkt, pm, nlev):
    b, n, _ = q.shape
    c = GLA_BLOCK
    nc = n // c
    nb = next(c for c in (GLA_BATCHES_PER_STEP, 2, 1) if b % c == 0)
    specs = []
    for d_ in range(2):
        pos = (lambda s_: s_) if d_ == 0 else (lambda s_: nc - 1 - s_)
        specs += [pl.BlockSpec((nb, c, 256), lambda bi, s_, pos=pos: (bi, pos(s_), 0)),
                  pl.BlockSpec((nb, c, 256), lambda bi, s_, pos=pos: (bi, pos(s_), 0)),
                  pl.BlockSpec((nb, c, 512), lambda bi, s_, pos=pos: (bi, pos(s_), 0)),
                  pl.BlockSpec((nb, c, 256), lambda bi, s_, pos=pos, d_=d_: (bi, pos(s_), d_))]
    st_spec = pl.BlockSpec((nb, 2, GLA_HEADS, GLA_DV, LANES), lambda bi, s_: (bi, 0, 0, 0, 0))
    full = lambda a: pl.BlockSpec(a.shape, lambda bi, s_: (0,) * a.ndim)
    return pl.pallas_call(
        functools.partial(_gla_kernel, nlev, nb),
        out_shape=[jax.ShapeDtypeStruct((b, n, GLA_V), F32), jax.ShapeDtypeStruct((b, n, GLA_V), F32),
                   jax.ShapeDtypeStruct((b, 2, GLA_HEADS, GLA_DV, LANES), F32)],
        grid=(b // nb, nc),
        in_specs=specs + [full(cumq), full(cumkt), full(pm), st_spec],
        out_specs=[pl.BlockSpec((nb, c, GLA_V), lambda bi, s_: (bi, s_, 0)),
                   pl.BlockSpec((nb, c, GLA_V), lambda bi, s_: (bi, nc - 1 - s_, 0)), st_spec],
        scratch_shapes=[pltpu.VMEM((GLA_HEADS, GLA_DV, LANES), F32) for _ in range(2 * nb)],
        compiler_params=_cparams(("parallel", "arbitrary")),
    )(q, k, v, la, q, k, v, la, cumq, cumkt, pm, s0)


def _odd_out_router_kernel(x_ref, of_ref, ob_ref, r_ref, dl_ref, gg_ref, w_ref, mod_ref, gn_ref, wr_ref, u_ref,
                           o_ref, *route_refs):
    d = D_MODEL
    gg = gg_ref[...]

    def tile(rs):
        o = of_ref[rs, :] + ob_ref[rs, :]
        r = r_ref[rs, :]
        parts = []
        for hd in range(GLA_HEADS):
            vs = slice(hd * GLA_DV, (hd + 1) * GLA_DV)
            oh = o[:, vs]
            parts.append(oh * lax.rsqrt(jnp.mean(oh * oh, axis=-1, keepdims=True) + EPS) * gg[:, vs])
        cl = (jnp.concatenate(parts, axis=1) * (r * jax.nn.sigmoid(r))).astype(BF16)
        y = _dot(cl, w_ref[0:512, :]) + _dot(dl_ref[rs, :], w_ref[512:1024, :])
        x = x_ref[rs, :] + mod_ref[:, 2 * d:3 * d] * y
        o_ref[rs, :] = x
        yield
        yield from _route_tile(x, rs, mod_ref, gn_ref, wr_ref, u_ref, *route_refs)

    _run_tiles(route_refs[-1], [tile(slice(t * MOE_TILE, (t + 1) * MOE_TILE)) for t in range(ROUTE_STEP // MOE_TILE)])


def _odd_out_router(x, o_fwd, o_bwd, r, dl, gg, w, mod, gn, wr, u):
    b, n, d = x.shape
    tm = ROUTE_STEP
    flat = lambda t: t.reshape(-1, t.shape[-1])
    act = lambda wd: pl.BlockSpec((tm, wd), lambda i: (i, 0))
    full = lambda t: pl.BlockSpec(t.shape, lambda i: (0,) * t.ndim)
    rshapes, rspecs, rscratch = _route_out(b * n, d)
    outs = pl.pallas_call(
        _odd_out_router_kernel,
        out_shape=[jax.ShapeDtypeStruct((b * n, d), F32)] + rshapes,
        grid=(b * n // tm,),
        in_specs=[act(d), act(GLA_V), act(GLA_V), act(512), act(512), full(gg), full(w),
                  pl.BlockSpec((None, 1, 6 * d), lambda i: (i // (n // tm), 0, 0)), full(gn), full(wr), full(u)],
        out_specs=[act(d)] + rspecs,
        scratch_shapes=rscratch,
        compiler_params=_cparams(("arbitrary",)),
    )(flat(x), flat(o_fwd), flat(o_bwd), flat(r), flat(dl), gg, w, mod, gn, wr, u)
    return outs[0].reshape(x.shape), outs[1:]


def _route_tile(x, rs, mod_ref, gn_ref, wr_ref, u_ref, h_ref, e_ref, wt_ref, r_ref, cnt_ref, carry_ref):
    d = D_MODEL
    tm = x.shape[0]
    mod = mod_ref[...]
    h = _rms(x, gn_ref[...]) * (1.0 + mod[:, 4 * d:5 * d]) + mod[:, 3 * d:4 * d]
    h_ref[rs, :] = _pack_bf16_pairs(h)
    h_hi, h_lo = _split2(h)
    w_hi, w_lo = _split2(wr_ref[...])
    lg = _dot_nt(w_hi, h_hi) + _dot_nt(w_lo, h_hi) + _dot_nt(w_hi, h_lo)
    yield
    rid = lax.broadcasted_iota(I32, (8, tm), 0)
    gl = jnp.where(rid < MOE_GROUPS, lg[0:8], NEG)
    gmax = jnp.max(gl, axis=0, keepdims=True)
    gsel = jnp.min(jnp.where(gl == gmax, rid, 8), axis=0, keepdims=True)
    pmax = 1.0 / jnp.sum(jnp.where(rid < MOE_GROUPS, jnp.exp(gl - gmax), 0.0), axis=0, keepdims=True)
    e_in = jnp.zeros((MOE_PER_GROUP, tm), F32)
    for g in range(MOE_GROUPS):
        e_in = e_in + jnp.where(gsel == g, lg[8 + 8 * g:16 + 8 * g], 0.0)
    v1 = jnp.max(e_in, axis=0, keepdims=True)
    i1 = jnp.min(jnp.where(e_in == v1, rid, 8), axis=0, keepdims=True)
    e_rest = jnp.where(rid == i1, -jnp.inf, e_in)
    v2 = jnp.max(e_rest, axis=0, keepdims=True)
    i2 = jnp.min(jnp.where(e_rest == v2, rid, 8), axis=0, keepdims=True)
    t = jnp.exp(v2 - v1)
    w1 = pmax / (1.0 + t)
    w2 = pmax * t / (1.0 + t)
    e1 = gsel * MOE_PER_GROUP + i1
    e2 = gsel * MOE_PER_GROUP + i2
    eid = lax.broadcasted_iota(I32, (MOE_EXPERTS, tm), 0)
    oh1 = jnp.where(eid == e1, 1.0, 0.0)
    oh2 = jnp.where(eid == e2, 1.0, 0.0)
    ohs = oh1 + oh2
    prefix = _dot(ohs.astype(BF16), u_ref[...])
    e_ref[:, rs] = jnp.concatenate([e1, e2], axis=0)
    w8 = jnp.concatenate([w1, w2, jnp.zeros((6, tm), F32)], axis=0)
    wt_ref[rs, :] = w8.T
    yield
    base = carry_ref[:, 0:1] + prefix
    r1 = jnp.sum(oh1 * base, axis=0, keepdims=True)
    r2 = jnp.sum(oh2 * base, axis=0, keepdims=True)
    carry_ref[...] = carry_ref[...] + jnp.sum(ohs, axis=1, keepdims=True)
    cnt_ref[...] = carry_ref[...]
    r_ref[:, rs] = jnp.concatenate([r1, r2], axis=0).astype(I32)


def _run_tiles(carry_ref, chains):
    @pl.when(pl.program_id(0) == 0)
    def _():
        carry_ref[...] = jnp.zeros_like(carry_ref)

    while chains:
        chains = [ch for ch in chains if next(ch, "done") != "done"]


def _route_out(n, d):
    tm = ROUTE_STEP
    shapes = [jax.ShapeDtypeStruct((n, d // 2), I32), jax.ShapeDtypeStruct((2, n), I32),
              jax.ShapeDtypeStruct((n, 8), F32), jax.ShapeDtypeStruct((2, n), I32),
              jax.ShapeDtypeStruct((MOE_EXPERTS, LANES), F32)]
    specs = [pl.BlockSpec((tm, d // 2), lambda i: (i, 0)), pl.BlockSpec((2, tm), lambda i: (0, i)),
             pl.BlockSpec((tm, 8), lambda i: (i, 0)), pl.BlockSpec((2, tm), lambda i: (0, i)),
             pl.BlockSpec((MOE_EXPERTS, LANES), lambda i: (0, 0))]
    return shapes, specs, [pltpu.VMEM((MOE_EXPERTS, LANES), F32)]


def _even_out_router_kernel(nlat, x_ref, xc_ref, a_ref, ac_ref, b_ref, bc_ref, w_ref, mod_ref, gn_ref, wr_ref, u_ref,
                            xo_ref, xoc_ref, *route_refs):
    d = D_MODEL
    lat = pl.program_id(0) < nlat

    def tile(rs):
        a = jnp.where(lat, a_ref[rs, :], ac_ref[rs, :])
        b = jnp.where(lat, b_ref[rs, :], bc_ref[rs, :])
        y = _dot(a, w_ref[0:512, :]) + _dot(b, w_ref[512:1024, :])
        x = jnp.where(lat, x_ref[rs, :], xc_ref[rs, :]) + mod_ref[:, 2 * d:3 * d] * y

        @pl.when(lat)
        def _():
            xo_ref[rs, :] = x

        @pl.when(jnp.logical_not(lat))
        def _():
            xoc_ref[rs, :] = x

        yield
        yield from _route_tile(x, rs, mod_ref, gn_ref, wr_ref, u_ref, *route_refs)

    _run_tiles(route_refs[-1], [tile(slice(t * MOE_TILE, (t + 1) * MOE_TILE)) for t in range(ROUTE_STEP // MOE_TILE)])


def _even_out_router(x, xc, oa, oac, ob, obc, w, mod, ctx_row, gn, wr, u):
    b, n, d = x.shape
    tm = ROUTE_STEP
    flat = lambda t: t.reshape(-1, t.shape[-1])
    nlat = b * n // tm
    ntok = b * n + xc.shape[0] * xc.shape[1]
    lat = lambda wd: pl.BlockSpec((tm, wd), lambda i: (jnp.minimum(i, nlat - 1), 0))
    ctx = lambda wd: pl.BlockSpec((tm, wd), lambda i: (jnp.maximum(i - nlat, 0), 0))
    full = lambda t: pl.BlockSpec(t.shape, lambda i: (0,) * t.ndim)
    modrow = lambda i: (jnp.where(i < nlat, i // (n // tm), ctx_row), 0, 0)
    rshapes, rspecs, rscratch = _route_out(ntok, d)
    outs = pl.pallas_call(
        functools.partial(_even_out_router_kernel, nlat),
        out_shape=[jax.ShapeDtypeStruct((b * n, d), F32), jax.ShapeDtypeStruct((ntok - b * n, d), F32)] + rshapes,
        grid=(ntok // tm,),
        in_specs=[lat(d), ctx(d), lat(512), ctx(512), lat(512), ctx(512), full(w),
                  pl.BlockSpec((None, 1, 6 * d), modrow), full(gn), full(wr), full(u)],
        out_specs=[lat(d), ctx(d)] + rspecs,
        scratch_shapes=rscratch,
        compiler_params=_cparams(("arbitrary",)),
    )(flat(x), flat(xc), flat(oa), flat(oac), flat(ob), flat(obc), w, mod, gn, wr, u)
    return outs[0].reshape(x.shape), outs[1].reshape(xc.shape), outs[2:]


def _sc_permute_rows(src, dest, scatter):
    rows, d = dest.shape[0], src.shape[1]
    n = rows // 2
    info = plsc.get_sparse_core_info()
    workers = info.num_cores * info.num_subcores
    per_worker = rows // workers
    chunk = next(c for c in SC_CHUNKS if per_worker % c == 0)
    assert rows == per_worker * workers and n % per_worker == 0
    mesh = plsc.VectorSubcoreMesh(core_axis_name="c", subcore_axis_name="s")

    def body(src_hbm, dest_hbm, out_hbm, idx_v, rows_v, sem):
        base = (lax.axis_index("s") * info.num_cores + lax.axis_index("c")) * per_worker

        @pl.loop(0, per_worker // chunk)
        def _(j):
            a0 = base + j * chunk
            pltpu.sync_copy(dest_hbm.at[pl.ds(a0, chunk)], idx_v)
            if scatter:
                t0 = jnp.where(a0 >= n, a0 - n, a0)
                pltpu.sync_copy(src_hbm.at[pl.ds(t0, chunk)], rows_v)
                pltpu.async_copy(rows_v, out_hbm.at[idx_v], sem).wait()
            else:
                pltpu.async_copy(src_hbm.at[idx_v], rows_v, sem).wait()
                pltpu.sync_copy(rows_v, out_hbm.at[pl.ds(a0, chunk)])

    return pl.kernel(
        body, out_type=jax.ShapeDtypeStruct((rows, d), src.dtype), mesh=mesh,
        scratch_types=[pltpu.VMEM((chunk,), I32), pltpu.VMEM((chunk, d), src.dtype), pltpu.SemaphoreType.DMA],
    )(src, dest)


def _gmm_kernel(layer, vt_ref, ve_ref, vlo_ref, vhi_ref, vfirst_ref, vslot_ref, vnext_ref, nv_ref,
                xs_ref, wg_hbm, wu_hbm, wd_hbm, ys_ref, wgs_ref, wus_ref, wds_ref, wgb_ref, wub_ref, wdb_ref, sem):
    del vt_ref
    v = pl.program_id(0)

    def fetch(e, slot):
        return [pltpu.make_async_copy(w.at[layer, e], s.at[slot], sem.at[slot, i])
                for i, (w, s) in enumerate(((wg_hbm, wgs_ref), (wu_hbm, wus_ref), (wd_hbm, wds_ref)))]

    @pl.when(v < nv_ref[0])
    def _():
        @pl.when((v == 0) | (ve_ref[v] != ve_ref[jnp.maximum(v - 1, 0)]))
        def _():
            slot = vslot_ref[v]

            @pl.when(v == 0)
            def _():
                for c in fetch(ve_ref[0], 0):
                    c.start()

            for c in fetch(ve_ref[v], slot):
                c.wait()
            wgb_ref[...] = wgs_ref[slot].astype(BF16)
            wub_ref[...] = wus_ref[slot].astype(BF16)
            wdb_ref[...] = wds_ref[slot].astype(BF16)

            @pl.when(vnext_ref[v] >= 0)
            def _():
                for c in fetch(vnext_ref[v], 1 - slot):
                    c.start()

        x = _unpack_bf16_pairs(xs_ref[...]).astype(BF16)
        g = _dot(x, wgb_ref[...])
        u = _dot(x, wub_ref[...])
        y = _pack_bf16_pairs(_dot((g * jax.nn.sigmoid(g) * u).astype(BF16), wdb_ref[...]))
        row = lax.broadcasted_iota(I32, (y.shape[0], 1), 0)
        mine = (row >= vlo_ref[v]) & (row < vhi_ref[v])

        @pl.when(vfirst_ref[v] == 1)
        def _():
            ys_ref[...] = jnp.where(mine, y, 0)

        @pl.when(vfirst_ref[v] == 0)
        def _():
            ys_ref[...] = jnp.where(mine, y, ys_ref[...])


def _gmm(xs, visits, layer, wg, wu, wd):
    rows, dw = xs.shape
    d = 2 * dw
    tm = MOE_TILE
    hid = wg.shape[-1]
    nvis = rows // tm + MOE_EXPERTS - 1
    tile = lambda v, vt, *_: (vt[v], 0)
    anyspec = pl.BlockSpec(memory_space=pl.ANY)
    return pl.pallas_call(
        functools.partial(_gmm_kernel, layer),
        out_shape=jax.ShapeDtypeStruct((rows, dw), I32),
        grid_spec=pltpu.PrefetchScalarGridSpec(
            num_scalar_prefetch=8, grid=(nvis,),
            in_specs=[pl.BlockSpec((tm, dw), tile), anyspec, anyspec, anyspec],
            out_specs=pl.BlockSpec((tm, dw), tile),
            scratch_shapes=[pltpu.VMEM((2, d, hid), F32), pltpu.VMEM((2, d, hid), F32), pltpu.VMEM((2, hid, d), F32),
                            pltpu.VMEM((d, hid), BF16), pltpu.VMEM((d, hid), BF16), pltpu.VMEM((hid, d), BF16),
                            pltpu.SemaphoreType.DMA((2, 3))]),
        compiler_params=_cparams(("arbitrary",)),
    )(*visits, xs, wg, wu, wd)


def _combine_kernel(x_ref, wt_ref, mod_ref, fg_ref, y0_ref, y1_ref, o_ref):
    d = D_MODEL
    wt = wt_ref[...]
    y = wt[:, 0:1] * _unpack_bf16_pairs(y0_ref[...]) + wt[:, 1:2] * _unpack_bf16_pairs(y1_ref[...])
    o_ref[...] = _rms(x_ref[...] + mod_ref[:, 5 * d:6 * d] * y, fg_ref[...])


def _combine(x2, wt, mod, rows_per_mod, fg, yt, tm=COMBINE_TILE):
    n, d = x2.shape
    tm = min(tm, rows_per_mod)
    assert rows_per_mod % tm == 0
    ntiles = n // tm
    return pl.pallas_call(
        _combine_kernel,
        out_shape=jax.ShapeDtypeStruct((n, d), F32),
        grid=(ntiles,),
        in_specs=[pl.BlockSpec((tm, d), lambda i: (i, 0)), pl.BlockSpec((tm, 8), lambda i: (i, 0)),
                  pl.BlockSpec((None, 1, 6 * d), lambda i: (i // (rows_per_mod // tm), 0, 0)),
                  pl.BlockSpec(fg.shape, lambda i: (0, 0)),
                  pl.BlockSpec((tm, d // 2), lambda i: (i, 0)), pl.BlockSpec((tm, d // 2), lambda i: (ntiles + i, 0))],
        out_specs=pl.BlockSpec((tm, d), lambda i: (i, 0)),
        compiler_params=_cparams(("parallel",)),
    )(x2, wt, mod, fg, yt, yt)


def _pick(table, idx):
    hot = idx[..., None] == jnp.arange(table.shape[0], dtype=I32)
    return jnp.sum(jnp.where(hot, table, 0), axis=-1)


def _moe_plan(counts, e, r, rows):
    tm = MOE_TILE
    ends = jnp.cumsum(counts)
    starts = ends - counts
    dest = (_pick(starts, e) + r).reshape(-1)
    first_tile = starts // tm
    nvis = jnp.where(counts > 0, (ends - 1) // tm - first_tile + 1, 0)
    vend = jnp.cumsum(nvis)
    nv = vend[-1:]
    v = jnp.minimum(jnp.arange(rows // tm + MOE_EXPERTS - 1, dtype=I32), nv[0] - 1)
    ve = jnp.sum((vend[None, :] <= v[:, None]).astype(I32), axis=1)
    vt = _pick(first_tile, ve) + v - _pick(vend - nvis, ve)
    vlo = jnp.maximum(_pick(starts, ve) - vt * tm, 0)
    vhi = jnp.minimum(_pick(ends, ve) - vt * tm, tm)
    vfirst = jnp.concatenate([jnp.ones((1,), I32), (vt[1:] != vt[:-1]).astype(I32)])
    changed = jnp.concatenate([jnp.ones((1,), I32), (ve[1:] != ve[:-1]).astype(I32)])
    vslot = (jnp.cumsum(changed) - 1) % 2
    eid = jnp.arange(MOE_EXPERTS, dtype=I32)
    later = (eid[None, :] > eid[:, None]) & (counts[None, :] > 0)
    nxt = jnp.min(jnp.where(later, eid[None, :], MOE_EXPERTS), axis=1)
    vnext = _pick(jnp.where(nxt < MOE_EXPERTS, nxt, -1), ve)
    return dest, (vt, ve, vlo, vhi, vfirst, vslot.astype(I32), vnext.astype(I32), nv)


def _moe_experts(routed, layer, wg, wu, wd):
    h, e, wt, r, cnt = routed
    dest, visits = _moe_plan(cnt[:, 0].astype(I32), e, r, 2 * h.shape[0])
    xs = _sc_permute_rows(h, dest, scatter=True)
    ys = _gmm(xs, visits, layer, wg, wu, wd)
    return _sc_permute_rows(ys, dest, scatter=False), wt


def _rope_tables(rows, dim):
    row = jnp.repeat(jnp.arange(rows, dtype=F32), GRID_W)
    col = jnp.tile(jnp.arange(GRID_W, dtype=F32), rows)
    half = dim // 2
    inv = jnp.power(ROPE_BASE, -jnp.arange(0, half, 2, dtype=F32) / half)
    ar = row[:, None] * inv[None, :]
    ac = col[:, None] * inv[None, :]
    ang = jnp.concatenate([ar, ar, ac, ac], axis=-1)
    return jnp.cos(ang), jnp.sin(ang)


def _even_tables(n, with_rope):
    if with_rope:
        cm, sm = _rope_tables(n // GRID_W, MLA_ROPE)
        cwin, swin = _rope_tables(n // GRID_W, WIN_HEAD_DIM)
    else:
        cm, sm = jnp.ones((n, MLA_ROPE), F32), jnp.zeros((n, MLA_ROPE), F32)
        cwin, swin = jnp.ones((n, WIN_HEAD_DIM), F32), jnp.zeros((n, WIN_HEAD_DIM), F32)
    one, zero = jnp.ones((n, MLA_NOPE), F32), jnp.zeros((n, MLA_NOPE), F32)
    pad = jnp.zeros((n, LANES - MLA_NOPE - MLA_ROPE), F32)
    return (jnp.concatenate([one, cm, pad], axis=1), jnp.concatenate([zero, sm, pad], axis=1),
            jnp.concatenate([cwin, cwin], axis=1), jnp.concatenate([swin, swin], axis=1))


def _even_weights(w_in, qg, w_uq, kvg, w_ukv):
    d = w_in.shape[0]
    o = np.cumsum([0, MLA_Q_RANK, MLA_KV_RANK, MLA_ROPE, 512, 128, 128])
    cq, ckv, kr, qw, kw, vw = [w_in[:, o[i]:o[i + 1]] for i in range(6)]
    z = lambda c: jnp.zeros((d, c), F32)
    kr128 = jnp.concatenate([z(MLA_NOPE), kr, z(LANES - MLA_NOPE - MLA_ROPE)], axis=1)
    dup = lambda t: jnp.concatenate([t[:, 0:64], t[:, 0:64], t[:, 64:128], t[:, 64:128]], axis=1)
    win = jnp.concatenate([cq, ckv, kr128, qw, dup(kw), dup(vw)], axis=1).astype(BF16)
    uq = w_uq.reshape(MLA_Q_RANK, MLA_HEADS, MLA_NOPE + MLA_ROPE)
    uq = jnp.pad(uq, ((0, 0), (0, 0), (0, LANES - MLA_NOPE - MLA_ROPE))).reshape(MLA_Q_RANK, MLA_HEADS * LANES)
    ukv = w_ukv.reshape(MLA_KV_RANK, MLA_HEADS, MLA_NOPE + MLA_V)
    ukk = jnp.pad(ukv[:, :, :MLA_NOPE], ((0, 0), (0, 0), (0, LANES - MLA_NOPE))).reshape(MLA_KV_RANK, MLA_HEADS * LANES)
    ukvv = ukv[:, :, MLA_NOPE:].reshape(MLA_KV_RANK, MLA_HEADS * MLA_V)
    return (win, qg.reshape(1, -1), uq.astype(BF16), kvg.reshape(1, -1), ukk.astype(BF16), ukvv.T.astype(BF16))


def _odd_weights(w_in, w_g2, b_g, ln_g, ln_b, w_s, b_s):
    d = w_in.shape[0]
    o = np.cumsum([0, GLA_K, GLA_K, GLA_V, 2 * GLA_GATE_RANK, GLA_V, SG_WIDTH, SG_WIDTH])
    q, k, v, g, r, u, vg = [w_in[:, o[i]:o[i + 1]] for i in range(7)]
    g128 = jnp.concatenate([g, jnp.zeros((d, LANES - 2 * GLA_GATE_RANK), F32)], axis=1)
    win = jnp.concatenate([q, k, v, g128, r, u, vg], axis=1).astype(BF16)
    zr = jnp.zeros((GLA_GATE_RANK, GLA_K), F32)
    pad = jnp.zeros((LANES - 2 * GLA_GATE_RANK, GLA_K), F32)
    wg = jnp.concatenate([jnp.concatenate([w_g2[0], zr, pad], axis=0),
                          jnp.concatenate([zr, w_g2[1], pad], axis=0)], axis=1)
    bg = b_g.reshape(1, 2 * GLA_K)
    return (win, wg, bg, ln_g.reshape(1, -1), ln_b.reshape(1, -1), w_s.astype(BF16), b_s.T)


def kernel(x, c, ctx, c_ctx, ada_w, ada_b, norm_mix_g, norm_ffn_g, even_w_in, mla_q_norm_g, mla_w_uq, mla_kv_norm_g, mla_w_ukv, win_sink, even_w_out, odd_w_in, gla_w_g2, gla_b_g, gla_norm_g, sg_ln_g, sg_ln_b, sg_w_s, sg_b_s, odd_w_out, moe_w_rg, moe_w_re, moe_w_gate, moe_w_up, moe_w_down, final_norm_g):
    b, n, d = x.shape
    lc = ctx.shape[1]
    depth = ada_w.shape[0]
    assert depth == 2 and d == D_MODEL and b < 8
    assert n % 512 == 0 and lc % MOE_TILE == 0 and n % GRID_W == 0
    tm = 512 if n % 512 == 0 else 256
    tq = 256

    cond8 = jnp.concatenate([c, c_ctx[None, :], jnp.zeros((8 - b - 1, d), F32)], axis=0)
    mod_all = _adaln(cond8, ada_w, ada_b).reshape(depth, 8, 1, 6 * d)
    ctx_row = b
    u_tri = jnp.asarray(np.triu(np.ones((MOE_TILE, MOE_TILE), np.float32), 1), BF16)
    fg = final_norm_g.reshape(1, d)

    def router_w(layer):
        return jnp.concatenate([moe_w_rg[layer].T, jnp.zeros((8 - MOE_GROUPS, d), F32), moe_w_re[layer].T], axis=0)

    def moe_experts(routed, layer):
        return _moe_experts(routed, layer, moe_w_gate, moe_w_up, moe_w_down)

    mod = mod_all[0]
    gn = norm_mix_g[0].reshape(1, d)
    ew = _even_weights(even_w_in[0], mla_q_norm_g[0], mla_w_uq[0], mla_kv_norm_g[0], mla_w_ukv[0])
    qm_l, km_l, vm_l, qw_l, kw_l, vw_l = _even_in(x, mod, None, gn, ew, _even_tables(n, True), min(EVEN_IN_STEP, n))
    qm_c, km_c, vm_c, qw_c, kw_c, vw_c = _even_in(ctx, mod, ctx_row, gn, ew, _even_tables(lc, False), lc)
    w_out = even_w_out[0].astype(BF16)
    sink = win_sink[0]
    oa_l = _mla_attn(qm_l, [(km_l, vm_l), (km_c, vm_c)], tq)
    ob_l = _gqa(qw_l, kw_l, vw_l, kw_c, vw_c, sink, True)
    oa_c = _mla_attn(qm_c, [(km_c, vm_c)], lc)
    ob_c = _gqa(qw_c, None, None, kw_c, vw_c, sink, False)
    xl, xc, routed = _even_out_router(x, ctx, oa_l, oa_c, ob_l, ob_c, w_out, mod, ctx_row,
                                      norm_ffn_g[0].reshape(1, d), router_w(0), u_tri)
    pending = moe_experts(routed, 0)

    mod = mod_all[1]
    gn = norm_mix_g[1].reshape(1, d)
    ow = _odd_weights(odd_w_in[0], gla_w_g2[0], gla_b_g[0], sg_ln_g[0], sg_ln_b[0], sg_w_s[0], sg_b_s[0])
    xl, q_l, k_l, v_l, la_l, r_l, dl_l = _odd_in(xl, pending, 0, mod_all[0], mod, None, gn, ow)
    _, q_c, k_c, v_c, la_c, _, _ = _odd_in(xc, pending, b * n // MOE_TILE, mod_all[0], mod, ctx_row, gn, ow)
    cumq_np, cumkt_np, pm_np, nlev = _gla_tables()
    cumq, cumkt = jnp.asarray(cumq_np, BF16), jnp.asarray(cumkt_np, BF16)
    pm = jnp.asarray(pm_np, F32)
    s0 = jnp.zeros((b, 2, GLA_HEADS, GLA_DV, LANES), F32)
    _, _, s_ctx = _gla(q_c, k_c, v_c, la_c, s0, cumq, cumkt, pm, nlev)
    o_fwd, o_bwd, _ = _gla(q_l, k_l, v_l, la_l, s_ctx, cumq, cumkt, pm, nlev)
    xl, routed = _odd_out_router(xl, o_fwd, o_bwd, r_l, dl_l, gla_norm_g[0].reshape(1, -1),
                                 odd_w_out[0].astype(BF16), mod, norm_ffn_g[1].reshape(1, d), router_w(1), u_tri)
    yt, wt = moe_experts(routed, 1)
    return _combine(xl.reshape(b * n, d), wt, mod, n, fg, yt).reshape(b, n, d)
```

```python
import functools

import numpy as np
import jax
import jax.numpy as jnp
from jax import lax
from jax.experimental import pallas as pl
from jax.experimental.pallas import tpu as pltpu
from jax.experimental.pallas import tpu_sc as plsc

F32 = jnp.float32
BF16 = jnp.bfloat16
I32 = jnp.int32

D_MODEL = 1024
GRID_W = 64
EPS = 1e-6
ROPE_BASE = 10000.0
MLA_HEADS = 8
MLA_Q_RANK = 256
MLA_KV_RANK = 128
MLA_NOPE = 64
MLA_ROPE = 32
MLA_V = 64
WIN_HEADS = 8
WIN_KV_HEADS = 2
WIN_HEAD_DIM = 64
WIN_BLOCK = 128
GLA_HEADS = 4
GLA_DK = 64
GLA_DV = 128
GLA_GATE_RANK = 16
GLA_TAU = 16.0
GLA_K = GLA_HEADS * GLA_DK
GLA_V = GLA_HEADS * GLA_DV
SG_GROUPS = 4
SG_CHUNK = 128
SG_WIDTH = 512
MOE_GROUPS = 4
MOE_PER_GROUP = 8
MOE_EXPERTS = 32
MOE_HIDDEN = 512

LANES = 128
GLA_BLOCK = 128
GLA_LEVELS_PER_PHASE = 2
GLA_BATCHES_PER_STEP = 2
MOE_TILE = 256
EVEN_IN_STEP = 1024
ODD_IN_STEP = 1024
ROUTE_STEP = 1024
COMBINE_TILE = 1024
MLA_KEY_CHUNK = 1024
SC_CHUNKS = (64, 32)
NEG = -1e30
LOG2E = 1.4426950408889634
VMEM_LIMIT = 56 * 1024 * 1024


def _cparams(sem):
    return pltpu.CompilerParams(dimension_semantics=sem, vmem_limit_bytes=VMEM_LIMIT)


def _dot(a, b):
    return jnp.dot(a, b, preferred_element_type=F32)


def _dot_nt(a, b):
    return lax.dot_general(a, b, (((1,), (1,)), ((), ())), preferred_element_type=F32)


def _split2(a):
    hi = a.astype(BF16)
    lo = (a - hi.astype(F32)).astype(BF16)
    return hi, lo


def _pack_bf16_pairs(x):
    k = x.shape[1] // 2
    bits = lax.bitcast_convert_type(x.astype(BF16).astype(F32), jnp.uint32)
    return lax.bitcast_convert_type(bits[:, :k] | (bits[:, k:] >> 16), I32)


def _unpack_bf16_pairs(w):
    bits = lax.bitcast_convert_type(w, jnp.uint32)
    hi = lax.bitcast_convert_type(bits & jnp.uint32(0xFFFF0000), F32)
    lo = lax.bitcast_convert_type(bits << 16, F32)
    return jnp.concatenate([hi, lo], axis=1)


def _rms(x, g):
    ms = jnp.mean(x * x, axis=-1, keepdims=True)
    return x * lax.rsqrt(ms + EPS) * g


def _lane_tile(t, reps):
    return t if reps == 1 else jnp.concatenate([t] * reps, axis=1)


def _rope(t, cos, sin, quarter):
    n = t.shape[1]
    lane = lax.broadcasted_iota(I32, t.shape, 1)
    first = (lane & (2 * quarter - 1)) < quarter
    rot = jnp.where(first, -pltpu.roll(t, n - quarter, 1), pltpu.roll(t, quarter, 1))
    return t * cos + rot * sin


def _adaln_kernel(c_ref, w_ref, b_ref, o_ref):
    c = c_ref[...]
    s_hi, s_lo = _split2(c * jax.nn.sigmoid(c))
    w_hi, w_lo = _split2(w_ref[...])
    o_ref[...] = _dot(s_hi, w_hi) + _dot(s_lo, w_hi) + _dot(s_hi, w_lo) + b_ref[...]


def _adaln(cond8, ada_w, ada_b):
    depth, d, n6 = ada_w.shape
    tn = 1536
    return pl.pallas_call(
        _adaln_kernel,
        out_shape=jax.ShapeDtypeStruct((depth, 8, n6), F32),
        grid=(depth, n6 // tn),
        in_specs=[
            pl.BlockSpec((8, d), lambda l, j: (0, 0)),
            pl.BlockSpec((None, d, tn), lambda l, j: (l, 0, j)),
            pl.BlockSpec((None, 1, tn), lambda l, j: (l, 0, j)),
        ],
        out_specs=pl.BlockSpec((None, 8, tn), lambda l, j: (l, 0, j)),
        compiler_params=_cparams(("parallel", "parallel")),
    )(cond8, ada_w, ada_b.reshape(depth, 1, n6))


def _even_in_kernel(x_ref, mod_ref, gn_ref, win_ref, qg_ref, wuq_ref, kvg_ref, wukk_ref, wukv_ref,
                    cq_ref, sq_ref, cw_ref, sw_ref,
                    qm_ref, km_ref, vm_ref, qw_ref, kw_ref, vw_ref):
    d = D_MODEL
    mod = mod_ref[...]

    def tile(rs):
        h = _rms(x_ref[rs, :], gn_ref[...]) * (1.0 + mod[:, d:2 * d]) + mod[:, 0:d]
        z = _dot(h.astype(BF16), win_ref[...])
        cq, sq, cw, sw = cq_ref[rs, :], sq_ref[rs, :], cw_ref[rs, :], sw_ref[rs, :]
        yield
        cqn = _rms(z[:, 0:256], qg_ref[...]).astype(BF16)
        q = _dot(cqn, wuq_ref[...])
        ckvn = _rms(z[:, 256:384], kvg_ref[...]).astype(BF16)
        kn = _dot(ckvn, wukk_ref[...])
        vm_ref[:, rs] = _dot_nt(wukv_ref[...], ckvn).astype(BF16)
        yield
        q = _rope(q, _lane_tile(cq, 8), _lane_tile(sq, 8), MLA_ROPE // 4)
        qm_ref[rs, :] = (q * (LOG2E * (MLA_NOPE + MLA_ROPE) ** -0.5)).astype(BF16)
        kr = _rope(z[:, 384:512], cq, sq, MLA_ROPE // 4)
        km_ref[rs, :] = (kn + _lane_tile(kr, 8)).astype(BF16)
        yield
        qw = _rope(z[:, 512:1024], _lane_tile(cw, 4), _lane_tile(sw, 4), WIN_HEAD_DIM // 4)
        qw_ref[rs, :] = (qw * (WIN_HEAD_DIM ** -0.5)).astype(BF16)
        kw = _rope(z[:, 1024:1280], _lane_tile(cw, 2), _lane_tile(sw, 2), WIN_HEAD_DIM // 4)
        kw_ref[rs, :] = kw.astype(BF16)
        vw_ref[rs, :] = z[:, 1280:1536].astype(BF16)

    rows = min(MOE_TILE, x_ref.shape[0])
    chains = [tile(slice(r0, r0 + rows)) for r0 in range(0, x_ref.shape[0], rows)]
    while chains:
        chains = [ch for ch in chains if next(ch, "done") != "done"]


def _even_in(x, mod, mod_row, gn, wts, tabs, tm):
    b, n, d = x.shape
    win, qg, wuq, kvg, wukk, wukv = wts
    nt = n // tm
    row = (lambda bi, i: (bi, 0, 0)) if mod_row is None else (lambda bi, i: (mod_row, 0, 0))
    full = lambda a: pl.BlockSpec(a.shape, lambda bi, i: (0,) * a.ndim)
    tab = pl.BlockSpec((tm, LANES), lambda bi, i: (i, 0))
    outw = (1024, 1024, None, 512, 256, 256)
    rowspec = lambda w: pl.BlockSpec((None, tm, w), lambda bi, i: (bi, i, 0))
    colspec = pl.BlockSpec((None, 512, tm), lambda bi, i: (bi, 0, i))
    return pl.pallas_call(
        _even_in_kernel,
        out_shape=[jax.ShapeDtypeStruct((b, 512, n) if w is None else (b, n, w), BF16) for w in outw],
        grid=(b, nt),
        in_specs=[pl.BlockSpec((None, tm, d), lambda bi, i: (bi, i, 0)),
                  pl.BlockSpec((None, 1, 6 * d), row),
                  full(gn), full(win), full(qg), full(wuq), full(kvg), full(wukk), full(wukv),
                  tab, tab, tab, tab],
        out_specs=[colspec if w is None else rowspec(w) for w in outw],
        compiler_params=_cparams(("parallel", "parallel")),
    )(x, mod, gn, win, qg, wuq, kvg, wukk, wukv, *tabs)


def _mla_attn_kernel(nseg, q_ref, *refs):
    ks, vts = refs[0:2 * nseg:2], refs[1:2 * nseg:2]
    o_ref = refs[2 * nseg]
    s_bufs = refs[2 * nseg + 1:2 * nseg + 3]
    p_bufs = refs[2 * nseg + 3:2 * nseg + 5]
    pieces, base = [], 0
    for k in ks:
        n = k.shape[0]
        pieces += [(k, c0, min(n, c0 + MLA_KEY_CHUNK), base + c0) for c0 in range(0, n, MLA_KEY_CHUNK)]
        base += n

    def score_chunk(h, piece, buf):
        k, c0, c1, g0 = piece
        hs = slice(h * LANES, (h + 1) * LANES)
        s = _dot_nt(k[c0:c1, hs], q_ref[:, hs])
        buf[g0:g0 + c1 - c0, :] = s
        return jnp.max(s, axis=0, keepdims=True)

    def prob_chunk(piece, sbuf, pbuf, m):
        _, c0, c1, g0 = piece
        p = jnp.exp2(sbuf[g0:g0 + c1 - c0, :] - m)
        pbuf[g0:g0 + c1 - c0, :] = p.astype(BF16)
        return jnp.sum(p, axis=0, keepdims=True)

    m_next = functools.reduce(jnp.maximum, [score_chunk(0, pc, s_bufs[0]) for pc in pieces])
    outs = []
    for h in range(MLA_HEADS):
        m_cur, maxes, sums = m_next, [], []
        for pc in pieces:
            if h + 1 < MLA_HEADS:
                maxes.append(score_chunk(h + 1, pc, s_bufs[(h + 1) % 2]))
            sums.append(prob_chunk(pc, s_bufs[h % 2], p_bufs[h % 2], m_cur))
        if h + 1 < MLA_HEADS:
            m_next = functools.reduce(jnp.maximum, maxes)
        l = functools.reduce(jnp.add, sums)
        vrows = slice(h * MLA_V, (h + 1) * MLA_V)
        ot, base = None, 0
        for k, vt in zip(ks, vts):
            n = k.shape[0]
            part = _dot(vt[vrows, :], p_bufs[h % 2][base:base + n, :])
            ot = part if ot is None else ot + part
            base += n
        outs.append(ot * (1.0 / l))
    o_ref[...] = jnp.concatenate(outs, axis=0).T.astype(BF16)


def _mla_attn(q, segs, tq):
    b, n, _ = q.shape
    in_specs = [pl.BlockSpec((None, tq, 1024), lambda bi, i: (bi, i, 0))]
    args = [q]
    keys = 0
    for k, vt in segs:
        lk = k.shape[1]
        keys += lk
        in_specs += [pl.BlockSpec((None, lk, 1024), lambda bi, i: (bi, 0, 0)),
                     pl.BlockSpec((None, 512, lk), lambda bi, i: (bi, 0, 0))]
        args += [k, vt]
    return pl.pallas_call(
        functools.partial(_mla_attn_kernel, len(segs)),
        out_shape=jax.ShapeDtypeStruct((b, n, 512), BF16),
        grid=(b, n // tq),
        in_specs=in_specs,
        out_specs=pl.BlockSpec((None, tq, 512), lambda bi, i: (bi, i, 0)),
        scratch_shapes=[pltpu.VMEM((keys, tq), F32), pltpu.VMEM((keys, tq), F32),
                        pltpu.VMEM((keys, tq), BF16), pltpu.VMEM((keys, tq), BF16)],
        compiler_params=_cparams(("parallel", "parallel")),
    )(*args)


def _gqa_kernel(has_win, nb, sink_ref, q_ref, *refs):
    if has_win:
        kp, kc, kn, vp, vc, vn, kx, vx, o_ref = refs
    else:
        kx, vx, o_ref = refs
    tq = q_ref.shape[0]
    i = pl.program_id(1)
    lane = lax.broadcasted_iota(I32, (tq, LANES), 1)
    row2 = lax.broadcasted_iota(I32, (2 * tq, 1), 0)
    half = WIN_HEAD_DIM
    npair = WIN_HEADS // 2
    kcats, vcats = [], []
    for g in range(WIN_KV_HEADS):
        gs = slice(g * LANES, (g + 1) * LANES)
        if has_win:
            kcats.append(jnp.concatenate([kp[:, gs], kc[:, gs], kn[:, gs], kx[:, gs]], axis=0))
            vcats.append(jnp.concatenate([vp[:, gs], vc[:, gs], vn[:, gs], vx[:, gs]], axis=0))
        else:
            kcats.append(kx[:, gs])
            vcats.append(vx[:, gs])
    if has_win:
        w = WIN_BLOCK
        shape = (2 * tq, kcats[0].shape[0])
        r = lax.broadcasted_iota(I32, shape, 0) & (tq - 1)
        c = lax.broadcasted_iota(I32, shape, 1)
        big = jnp.int32(1 << 20)
        no_prev = jnp.where(i > 0, 0, big)
        no_next = jnp.where(i < nb - 1, 0, big)
        ok_prev = c >= r + no_prev
        ok_next = (c - 2 * w) <= r - no_next
        valid = ((c >= w) | ok_prev) & ((c < 2 * w) | (c >= 3 * w) | ok_next)

    def score(j):
        qp = q_ref[:, j * LANES:(j + 1) * LANES]
        zero = jnp.zeros_like(qp)
        q2 = jnp.concatenate([jnp.where(lane < half, qp, zero), jnp.where(lane >= half, qp, zero)], axis=0)
        return _dot_nt(q2, kcats[j // 2])

    def softmax(j, s):
        s = jnp.where(valid, s, NEG) if has_win else s
        sk = jnp.where(row2 < tq, sink_ref[2 * j], sink_ref[2 * j + 1])
        m = jnp.maximum(jnp.max(s, axis=-1, keepdims=True), sk)
        p = jnp.exp(s - m)
        return p.astype(BF16), 1.0 / (jnp.sum(p, axis=-1, keepdims=True) + jnp.exp(sk - m))

    def values(j, p, inv):
        o2 = _dot(p, vcats[j // 2]) * inv
        o_ref[:, j * LANES:(j + 1) * LANES] = jnp.where(lane < half, o2[:tq], o2[tq:]).astype(BF16)

    s_val, p_val = {}, {}
    for t in range(npair + 2):
        if t < npair:
            s_val[t] = score(t)
        if 0 <= t - 1 < npair:
            p_val[t - 1] = softmax(t - 1, s_val.pop(t - 1))
        if 0 <= t - 2 < npair:
            values(t - 2, *p_val.pop(t - 2))


def _gqa(q, k, v, kx, vx, sink, has_win):
    b, n, _ = q.shape
    lc = kx.shape[1]
    smem = pl.BlockSpec(memory_space=pltpu.SMEM)
    ctxs = pl.BlockSpec((None, lc, 256), lambda bi, i: (bi, 0, 0))
    if has_win:
        tq = WIN_BLOCK
        nb = n // tq
        blk = lambda f: pl.BlockSpec((None, tq, 256), f)
        prev = lambda bi, i: (bi, jnp.maximum(i - 1, 0), 0)
        cur = lambda bi, i: (bi, i, 0)
        nxt = lambda bi, i: (bi, jnp.minimum(i + 1, nb - 1), 0)
        in_specs = [smem, pl.BlockSpec((None, tq, 512), cur),
                    blk(prev), blk(cur), blk(nxt), blk(prev), blk(cur), blk(nxt), ctxs, ctxs]
        args = (sink, q, k, k, k, v, v, v, kx, vx)
    else:
        tq, nb = n, 1
        in_specs = [smem, pl.BlockSpec((None, tq, 512), lambda bi, i: (bi, i, 0)), ctxs, ctxs]
        args = (sink, q, kx, vx)
    return pl.pallas_call(
        functools.partial(_gqa_kernel, has_win, nb),
        out_shape=jax.ShapeDtypeStruct((b, n, 512), BF16),
        grid=(b, nb),
        in_specs=in_specs,
        out_specs=pl.BlockSpec((None, tq, 512), lambda bi, i: (bi, i, 0)),
        compiler_params=_cparams(("parallel", "parallel")),
    )(*args)


def _log_sigmoid(z):
    return jnp.minimum(z, 0.0) - jnp.log(1.0 + jnp.exp(-jnp.abs(z)))


def _odd_in_kernel(x_ref, y0_ref, y1_ref, wt_ref, modp_ref, mod_ref, gn_ref, win_ref, wg_ref, bg_ref, lng_ref,
                   lnb_ref, ws_ref, bst_ref, xn_ref, q_ref, k_ref, v_ref, la_ref, r_ref, dl_ref):
    d = D_MODEL
    mod = mod_ref[...]
    bst = bst_ref[...]

    def tile(r0, tm):
        rs = slice(r0, r0 + tm)
        wt = wt_ref[rs, :]
        y = wt[:, 0:1] * _unpack_bf16_pairs(y0_ref[rs, :]) + wt[:, 1:2] * _unpack_bf16_pairs(y1_ref[rs, :])
        x = x_ref[rs, :] + modp_ref[:, 5 * d:6 * d] * y
        xn_ref[rs, :] = x
        h = (_rms(x, gn_ref[...]) * (1.0 + mod[:, d:2 * d]) + mod[:, 0:d]).astype(BF16)
        z = _dot(h, win_ref[...])
        yield
        q_ref[rs, :] = z[:, 0:256] * (GLA_DK ** -0.5)
        k_ref[rs, :] = z[:, 256:512]
        v_ref[rs, :] = z[:, 512:1024].astype(BF16)
        g_hi, g_lo = _split2(z[:, 1024:1152])
        w_hi, w_lo = _split2(wg_ref[...])
        zg = _dot(g_hi, w_hi) + _dot(g_lo, w_hi) + _dot(g_hi, w_lo) + bg_ref[...]
        la_ref[rs, :] = _log_sigmoid(zg) / GLA_TAU
        r_ref[rs, :] = z[:, 1152:1664]
        yield
        u = jax.nn.gelu(z[:, 1664:2176])
        vg = jax.nn.gelu(z[:, 2176:2688])
        mu = jnp.mean(vg, axis=-1, keepdims=True)
        vc = vg - mu
        var = jnp.mean(vc * vc, axis=-1, keepdims=True)
        vn = (vc * lax.rsqrt(var + EPS) * lng_ref[...] + lnb_ref[...]).astype(BF16)
        yield
        for c in range(tm // SG_CHUNK):
            rows = slice(c * SG_CHUNK, (c + 1) * SG_CHUNK)
            parts = []
            for g in range(SG_GROUPS):
                cols = slice(g * LANES, (g + 1) * LANES)
                parts.append(_dot(ws_ref[g], vn[rows, cols]) + bst[:, g:g + 1])
            dl_ref[r0 + c * SG_CHUNK:r0 + (c + 1) * SG_CHUNK, :] = (
                u[rows, :] * jnp.concatenate(parts, axis=1)).astype(BF16)

    tm = min(MOE_TILE, x_ref.shape[0])
    chains = [tile(r0, tm) for r0 in range(0, x_ref.shape[0], tm)]
    while chains:
        chains = [ch for ch in chains if next(ch, "done") != "done"]


def _odd_in(x, pending, tile0, modp, mod, mod_row, gn, wts):
    b, n, d = x.shape
    tm = min(ODD_IN_STEP, n)
    tile0 = tile0 * MOE_TILE // tm
    yt, wt = pending
    ntiles = wt.shape[0] // tm
    win, wg, bg, lng, lnb, ws, bst = wts
    row = (lambda bi, i: (bi, 0, 0)) if mod_row is None else (lambda bi, i: (mod_row, 0, 0))
    full = lambda a: pl.BlockSpec(a.shape, lambda bi, i: (0,) * a.ndim)
    act = lambda wd: pl.BlockSpec((None, tm, wd), lambda bi, i: (bi, i, 0))
    tok = lambda bi, i: tile0 + bi * (n // tm) + i
    outs = [((b, n, d), F32, act(d)),
            ((b, n, 256), F32, act(256)), ((b, n, 256), F32, act(256)), ((b, n, 512), BF16, act(512)),
            ((b, n, 512), F32, act(512)), ((b, n, 512), F32, act(512)), ((b, n, 512), BF16, act(512))]
    return pl.pallas_call(
        _odd_in_kernel,
        out_shape=[jax.ShapeDtypeStruct(s, t) for s, t, _ in outs],
        grid=(b, n // tm),
        in_specs=[act(d), pl.BlockSpec((tm, d // 2), lambda bi, i: (tok(bi, i), 0)),
                  pl.BlockSpec((tm, d // 2), lambda bi, i: (ntiles + tok(bi, i), 0)),
                  pl.BlockSpec((tm, 8), lambda bi, i: (tok(bi, i), 0)),
                  pl.BlockSpec((None, 1, 6 * d), row), pl.BlockSpec((None, 1, 6 * d), row),
                  full(gn), full(win), full(wg), full(bg), full(lng), full(lnb), full(ws), full(bst)],
        out_specs=[sp for _, _, sp in outs],
        compiler_params=_cparams(("parallel", "parallel")),
    )(x, yt, yt, wt, modp, mod, gn, win, wg, bg, lng, lnb, ws, bst)


def _gla_tables():
    c = GLA_BLOCK
    t = np.arange(c)[:, None]
    u = np.arange(c)[None, :]
    levels = [c >> i for i in range(int(np.log2(c)) + 1)]
    cum = np.zeros((2, 2 * len(levels), c, c), np.float32)
    pair = np.zeros((2, len(levels), c, c), np.float32)
    for li, m in enumerate(levels):
        same = (t // m) == (u // m)
        cum[0, 2 * li] = same & (u <= t)
        cum[0, 2 * li + 1] = same & (u > t)
        cum[1, 2 * li] = same & (u >= t)
        cum[1, 2 * li + 1] = same & (u < t)
        if li > 0:
            pair[0, li] = ((t // m) % 2 == 1) & ((u // m) == (t // m) - 1)
            pair[1, li] = ((t // m) % 2 == 0) & ((u // m) == (t // m) + 1)
    pair[:, 0] = np.eye(c, dtype=np.float32)
    nlev = len(levels)
    m1 = cum[:, 0::2].reshape(2, nlev * c, c)
    m2t = np.concatenate([cum[:, 2 * li + 1].transpose(0, 2, 1) for li in range(nlev)], axis=2)
    return np.concatenate([m1, m1], axis=2), np.concatenate([m2t, m2t], axis=1), pair, nlev


def _gla_chain(nlev, q, k, la, v_ref, cumq, cumkt, pm_ref, st_ref, o_ref):
    c = GLA_BLOCK
    lat, kt = la.T, k.T
    l_hi, l_mid = _split2(la)
    t_hi, t_mid = _split2(lat)
    exq = jnp.exp(_dot(cumq, jnp.concatenate([l_hi, l_mid], axis=0)))
    exk = jnp.exp(_dot(jnp.concatenate([t_hi, t_mid], axis=1), cumkt))
    gcol = jnp.exp(jnp.sum(lat, axis=1, keepdims=True))
    yield
    qe, ke = [], []
    for li in range(nlev):
        qe.append((q * exq[li * c:(li + 1) * c]).astype(BF16))
        ke.append((kt * exk[:, li * c:(li + 1) * c]).astype(BF16))
        if li % 2 == 1:
            yield
    qb, kb = q.astype(BF16), kt.astype(BF16)
    states = [st_ref[hd] for hd in range(GLA_HEADS)]
    yield
    outs, new_states = [], []
    lane = lax.broadcasted_iota(I32, (c, LANES), 1)
    srow = lax.broadcasted_iota(I32, (LANES, 1), 0)
    zero = jnp.zeros((c, LANES), BF16)
    for hd in range(GLA_HEADS):
        ps = slice((hd // 2) * LANES, (hd // 2 + 1) * LANES)
        vs = slice(hd * GLA_DV, (hd + 1) * GLA_DV)
        mine = (lane < GLA_DK) if hd % 2 == 0 else (lane >= GLA_DK)
        mine_row = (srow < GLA_DK) if hd % 2 == 0 else (srow >= GLA_DK)
        pick = lambda t: jnp.where(mine, t[:, ps], zero)
        a = pm_ref[0] * _dot(pick(qb), kb[ps, :])
        for li in range(1, nlev):
            a = a + pm_ref[li] * _dot(pick(qe[li]), ke[li][ps, :])
            if li % GLA_LEVELS_PER_PHASE == 0:
                yield
        v_h = v_ref[:, vs]
        outs.append(_dot(qe[0][:, ps], states[hd].astype(BF16)) + _dot(a.astype(BF16), v_h))
        new_states.append(states[hd] * gcol[ps, :] + jnp.where(mine_row, _dot(ke[0][ps, :], v_h), 0.0))
        yield
    o_ref[...] = jnp.concatenate(outs, axis=1)
    for hd in range(GLA_HEADS):
        st_ref[hd] = new_states[hd]
    yield


def _gla_kernel(nlev, nb, *refs):
    ins_f, ins_b = refs[0:4], refs[4:8]
    cumq_ref, cumkt_ref, pm_ref, s0_ref, of_ref, ob_ref, sf_ref = refs[8:15]
    st_refs = refs[15:]
    step = pl.program_id(1)

    @pl.when(step == 0)
    def _():
        for bb in range(nb):
            for d_ in range(2):
                st_refs[2 * bb + d_][...] = s0_ref[bb, d_]

    chains = []
    for bb in range(nb):
        for d_, (ins, o_ref) in enumerate(((ins_f, of_ref), (ins_b, ob_ref))):
            q_ref, k_ref, v_ref, la_ref = ins
            chains.append(_gla_chain(nlev, q_ref[bb], k_ref[bb], la_ref[bb], v_ref.at[bb], cumq_ref[d_],
                                     cumkt_ref[d_], pm_ref.at[d_], st_refs[2 * bb + d_], o_ref.at[bb]))
    while chains:
        chains = [ch for ch in chains if next(ch, "done") != "done"]
    for bb in range(nb):
        for d_ in range(2):
            sf_ref[bb, d_] = st_refs[2 * bb + d_][...]


def _gla(q, k, v, la, s0, cumq, cumkt, pm, nlev):
    b, n, _ = q.shape
    c = GLA_BLOCK
    nc = n // c
    nb = next(c for c in (GLA_BATCHES_PER_STEP, 2, 1) if b % c == 0)
    specs = []
    for d_ in range(2):
        pos = (lambda s_: s_) if d_ == 0 else (lambda s_: nc - 1 - s_)
        specs += [pl.BlockSpec((nb, c, 256), lambda bi, s_, pos=pos: (bi, pos(s_), 0)),
                  pl.BlockSpec((nb, c, 256), lambda bi, s_, pos=pos: (bi, pos(s_), 0)),
                  pl.BlockSpec((nb, c, 512), lambda bi, s_, pos=pos: (bi, pos(s_), 0)),
                  pl.BlockSpec((nb, c, 256), lambda bi, s_, pos=pos, d_=d_: (bi, pos(s_), d_))]
    st_spec = pl.BlockSpec((nb, 2, GLA_HEADS, GLA_DV, LANES), lambda bi, s_: (bi, 0, 0, 0, 0))
    full = lambda a: pl.BlockSpec(a.shape, lambda bi, s_: (0,) * a.ndim)
    return pl.pallas_call(
        functools.partial(_gla_kernel, nlev, nb),
        out_shape=[jax.ShapeDtypeStruct((b, n, GLA_V), F32), jax.ShapeDtypeStruct((b, n, GLA_V), F32),
                   jax.ShapeDtypeStruct((b, 2, GLA_HEADS, GLA_DV, LANES), F32)],
        grid=(b // nb, nc),
        in_specs=specs + [full(cumq), full(cumkt), full(pm), st_spec],
        out_specs=[pl.BlockSpec((nb, c, GLA_V), lambda bi, s_: (bi, s_, 0)),
                   pl.BlockSpec((nb, c, GLA_V), lambda bi, s_: (bi, nc - 1 - s_, 0)), st_spec],
        scratch_shapes=[pltpu.VMEM((GLA_HEADS, GLA_DV, LANES), F32) for _ in range(2 * nb)],
        compiler_params=_cparams(("parallel", "arbitrary")),
    )(q, k, v, la, q, k, v, la, cumq, cumkt, pm, s0)


def _odd_out_router_kernel(x_ref, of_ref, ob_ref, r_ref, dl_ref, gg_ref, w_ref, mod_ref, gn_ref, wr_ref, u_ref,
                           o_ref, *route_refs):
    d = D_MODEL
    gg = gg_ref[...]

    def tile(rs):
        o = of_ref[rs, :] + ob_ref[rs, :]
        r = r_ref[rs, :]
        parts = []
        for hd in range(GLA_HEADS):
            vs = slice(hd * GLA_DV, (hd + 1) * GLA_DV)
            oh = o[:, vs]
            parts.append(oh * lax.rsqrt(jnp.mean(oh * oh, axis=-1, keepdims=True) + EPS) * gg[:, vs])
        cl = (jnp.concatenate(parts, axis=1) * (r * jax.nn.sigmoid(r))).astype(BF16)
        y = _dot(cl, w_ref[0:512, :]) + _dot(dl_ref[rs, :], w_ref[512:1024, :])
        x = x_ref[rs, :] + mod_ref[:, 2 * d:3 * d] * y
        o_ref[rs, :] = x
        yield
        yield from _route_tile(x, rs, mod_ref, gn_ref, wr_ref, u_ref, *route_refs)

    _run_tiles(route_refs[-1], [tile(slice(t * MOE_TILE, (t + 1) * MOE_TILE)) for t in range(ROUTE_STEP // MOE_TILE)])


def _odd_out_router(x, o_fwd, o_bwd, r, dl, gg, w, mod, gn, wr, u):
    b, n, d = x.shape
    tm = ROUTE_STEP
    flat = lambda t: t.reshape(-1, t.shape[-1])
    act = lambda wd: pl.BlockSpec((tm, wd), lambda i: (i, 0))
    full = lambda t: pl.BlockSpec(t.shape, lambda i: (0,) * t.ndim)
    rshapes, rspecs, rscratch = _route_out(b * n, d)
    outs = pl.pallas_call(
        _odd_out_router_kernel,
        out_shape=[jax.ShapeDtypeStruct((b * n, d), F32)] + rshapes,
        grid=(b * n // tm,),
        in_specs=[act(d), act(GLA_V), act(GLA_V), act(512), act(512), full(gg), full(w),
                  pl.BlockSpec((None, 1, 6 * d), lambda i: (i // (n // tm), 0, 0)), full(gn), full(wr), full(u)],
        out_specs=[act(d)] + rspecs,
        scratch_shapes=rscratch,
        compiler_params=_cparams(("arbitrary",)),
    )(flat(x), flat(o_fwd), flat(o_bwd), flat(r), flat(dl), gg, w, mod, gn, wr, u)
    return outs[0].reshape(x.shape), outs[1:]


def _route_tile(x, rs, mod_ref, gn_ref, wr_ref, u_ref, h_ref, e_ref, wt_ref, r_ref, cnt_ref, carry_ref):
    d = D_MODEL
    tm = x.shape[0]
    mod = mod_ref[...]
    h = _rms(x, gn_ref[...]) * (1.0 + mod[:, 4 * d:5 * d]) + mod[:, 3 * d:4 * d]
    h_ref[rs, :] = _pack_bf16_pairs(h)
    h_hi, h_lo = _split2(h)
    w_hi, w_lo = _split2(wr_ref[...])
    lg = _dot_nt(w_hi, h_hi) + _dot_nt(w_lo, h_hi) + _dot_nt(w_hi, h_lo)
    yield
    rid = lax.broadcasted_iota(I32, (8, tm), 0)
    gl = jnp.where(rid < MOE_GROUPS, lg[0:8], NEG)
    gmax = jnp.max(gl, axis=0, keepdims=True)
    gsel = jnp.min(jnp.where(gl == gmax, rid, 8), axis=0, keepdims=True)
    pmax = 1.0 / jnp.sum(jnp.where(rid < MOE_GROUPS, jnp.exp(gl - gmax), 0.0), axis=0, keepdims=True)
    e_in = jnp.zeros((MOE_PER_GROUP, tm), F32)
    for g in range(MOE_GROUPS):
        e_in = e_in + jnp.where(gsel == g, lg[8 + 8 * g:16 + 8 * g], 0.0)
    v1 = jnp.max(e_in, axis=0, keepdims=True)
    i1 = jnp.min(jnp.where(e_in == v1, rid, 8), axis=0, keepdims=True)
    e_rest = jnp.where(rid == i1, -jnp.inf, e_in)
    v2 = jnp.max(e_rest, axis=0, keepdims=True)
    i2 = jnp.min(jnp.where(e_rest == v2, rid, 8), axis=0, keepdims=True)
    t = jnp.exp(v2 - v1)
    w1 = pmax / (1.0 + t)
    w2 = pmax * t / (1.0 + t)
    e1 = gsel * MOE_PER_GROUP + i1
    e2 = gsel * MOE_PER_GROUP + i2
    eid = lax.broadcasted_iota(I32, (MOE_EXPERTS, tm), 0)
    oh1 = jnp.where(eid == e1, 1.0, 0.0)
    oh2 = jnp.where(eid == e2, 1.0, 0.0)
    ohs = oh1 + oh2
    prefix = _dot(ohs.astype(BF16), u_ref[...])
    e_ref[:, rs] = jnp.concatenate([e1, e2], axis=0)
    w8 = jnp.concatenate([w1, w2, jnp.zeros((6, tm), F32)], axis=0)
    wt_ref[rs, :] = w8.T
    yield
    base = carry_ref[:, 0:1] + prefix
    r1 = jnp.sum(oh1 * base, axis=0, keepdims=True)
    r2 = jnp.sum(oh2 * base, axis=0, keepdims=True)
    carry_ref[...] = carry_ref[...] + jnp.sum(ohs, axis=1, keepdims=True)
    cnt_ref[...] = carry_ref[...]
    r_ref[:, rs] = jnp.concatenate([r1, r2], axis=0).astype(I32)


def _run_tiles(carry_ref, chains):
    @pl.when(pl.program_id(0) == 0)
    def _():
        carry_ref[...] = jnp.zeros_like(carry_ref)

    while chains:
        chains = [ch for ch in chains if next(ch, "done") != "done"]


def _route_out(n, d):
    tm = ROUTE_STEP
    shapes = [jax.ShapeDtypeStruct((n, d // 2), I32), jax.ShapeDtypeStruct((2, n), I32),
              jax.ShapeDtypeStruct((n, 8), F32), jax.ShapeDtypeStruct((2, n), I32),
              jax.ShapeDtypeStruct((MOE_EXPERTS, LANES), F32)]
    specs = [pl.BlockSpec((tm, d // 2), lambda i: (i, 0)), pl.BlockSpec((2, tm), lambda i: (0, i)),
             pl.BlockSpec((tm, 8), lambda i: (i, 0)), pl.BlockSpec((2, tm), lambda i: (0, i)),
             pl.BlockSpec((MOE_EXPERTS, LANES), lambda i: (0, 0))]
    return shapes, specs, [pltpu.VMEM((MOE_EXPERTS, LANES), F32)]


def _even_out_router_kernel(nlat, x_ref, xc_ref, a_ref, ac_ref, b_ref, bc_ref, w_ref, mod_ref, gn_ref, wr_ref, u_ref,
                            xo_ref, xoc_ref, *route_refs):
    d = D_MODEL
    lat = pl.program_id(0) < nlat

    def tile(rs):
        a = jnp.where(lat, a_ref[rs, :], ac_ref[rs, :])
        b = jnp.where(lat, b_ref[rs, :], bc_ref[rs, :])
        y = _dot(a, w_ref[0:512, :]) + _dot(b, w_ref[512:1024, :])
        x = jnp.where(lat, x_ref[rs, :], xc_ref[rs, :]) + mod_ref[:, 2 * d:3 * d] * y

        @pl.when(lat)
        def _():
            xo_ref[rs, :] = x

        @pl.when(jnp.logical_not(lat))
        def _():
            xoc_ref[rs, :] = x

        yield
        yield from _route_tile(x, rs, mod_ref, gn_ref, wr_ref, u_ref, *route_refs)

    _run_tiles(route_refs[-1], [tile(slice(t * MOE_TILE, (t + 1) * MOE_TILE)) for t in range(ROUTE_STEP // MOE_TILE)])


def _even_out_router(x, xc, oa, oac, ob, obc, w, mod, ctx_row, gn, wr, u):
    b, n, d = x.shape
    tm = ROUTE_STEP
    flat = lambda t: t.reshape(-1, t.shape[-1])
    nlat = b * n // tm
    ntok = b * n + xc.shape[0] * xc.shape[1]
    lat = lambda wd: pl.BlockSpec((tm, wd), lambda i: (jnp.minimum(i, nlat - 1), 0))
    ctx = lambda wd: pl.BlockSpec((tm, wd), lambda i: (jnp.maximum(i - nlat, 0), 0))
    full = lambda t: pl.BlockSpec(t.shape, lambda i: (0,) * t.ndim)
    modrow = lambda i: (jnp.where(i < nlat, i // (n // tm), ctx_row), 0, 0)
    rshapes, rspecs, rscratch = _route_out(ntok, d)
    outs = pl.pallas_call(
        functools.partial(_even_out_router_kernel, nlat),
        out_shape=[jax.ShapeDtypeStruct((b * n, d), F32), jax.ShapeDtypeStruct((ntok - b * n, d), F32)] + rshapes,
        grid=(ntok // tm,),
        in_specs=[lat(d), ctx(d), lat(512), ctx(512), lat(512), ctx(512), full(w),
                  pl.BlockSpec((None, 1, 6 * d), modrow), full(gn), full(wr), full(u)],
        out_specs=[lat(d), ctx(d)] + rspecs,
        scratch_shapes=rscratch,
        compiler_params=_cparams(("arbitrary",)),
    )(flat(x), flat(xc), flat(oa), flat(oac), flat(ob), flat(obc), w, mod, gn, wr, u)
    return outs[0].reshape(x.shape), outs[1].reshape(xc.shape), outs[2:]


def _sc_permute_rows(src, dest, scatter):
    rows, d = dest.shape[0], src.shape[1]
    n = rows // 2
    info = plsc.get_sparse_core_info()
    workers = info.num_cores * info.num_subcores
    per_worker = rows // workers
    chunk = next(c for c in SC_CHUNKS if per_worker % c == 0)
    assert rows == per_worker * workers and n % per_worker == 0
    mesh = plsc.VectorSubcoreMesh(core_axis_name="c", subcore_axis_name="s")

    nchunk = per_worker // chunk

    def body(src_hbm, dest_hbm, out_hbm, idx_v, rows_v, sem):
        base = (lax.axis_index("s") * info.num_cores + lax.axis_index("c")) * per_worker
        pltpu.sync_copy(dest_hbm.at[pl.ds(base, per_worker)], idx_v)

        def stream(j, b):
            idx = idx_v.at[pl.ds(j * chunk, chunk)]
            if scatter:
                return pltpu.make_async_copy(rows_v.at[b], out_hbm.at[idx], sem.at[b])
            return pltpu.make_async_copy(src_hbm.at[idx], rows_v.at[b], sem.at[b])

        def load(j, b):
            a0 = base + j * chunk
            t0 = jnp.where(a0 >= n, a0 - n, a0)
            pltpu.sync_copy(src_hbm.at[pl.ds(t0, chunk)], rows_v.at[b])

        def step(j, b):
            if scatter:
                stream(j, b).start()

                @pl.when(j + 1 < nchunk)
                def _():
                    load(j + 1, 1 - b)

                stream(j, b).wait()
            else:
                @pl.when(j + 1 < nchunk)
                def _():
                    stream(j + 1, 1 - b).start()

                stream(j, b).wait()
                pltpu.sync_copy(rows_v.at[b], out_hbm.at[pl.ds(base + j * chunk, chunk)])

        if scatter:
            load(0, 0)
        else:
            stream(0, 0).start()

        @pl.loop(0, nchunk, step=2)
        def _(j):
            step(j, 0)
            if nchunk % 2:
                pl.when(j + 1 < nchunk)(lambda: step(j + 1, 1))
            else:
                step(j + 1, 1)

    return pl.kernel(
        body, out_type=jax.ShapeDtypeStruct((rows, d), src.dtype), mesh=mesh,
        scratch_types=[pltpu.VMEM((per_worker,), I32), pltpu.VMEM((2, chunk, d), src.dtype), pltpu.SemaphoreType.DMA((2,))],
    )(src, dest)


def _gmm_kernel(layer, vt_ref, ve_ref, vlo_ref, vhi_ref, vfirst_ref, vslot_ref, vnext_ref, nv_ref,
                xs_ref, wg_hbm, wu_hbm, wd_hbm, ys_ref, wgs_ref, wus_ref, wds_ref, wgb_ref, wub_ref, wdb_ref, sem):
    del vt_ref
    v = pl.program_id(0)

    def fetch(e, slot):
        return [pltpu.make_async_copy(w.at[layer, e], s.at[slot], sem.at[slot, i])
                for i, (w, s) in enumerate(((wg_hbm, wgs_ref), (wu_hbm, wus_ref), (wd_hbm, wds_ref)))]

    @pl.when(v < nv_ref[0])
    def _():
        @pl.when((v == 0) | (ve_ref[v] != ve_ref[jnp.maximum(v - 1, 0)]))
        def _():
            slot = vslot_ref[v]

            @pl.when(v == 0)
            def _():
                for c in fetch(ve_ref[0], 0):
                    c.start()

            for c in fetch(ve_ref[v], slot):
                c.wait()
            wgb_ref[...] = wgs_ref[slot].astype(BF16)
            wub_ref[...] = wus_ref[slot].astype(BF16)
            wdb_ref[...] = wds_ref[slot].astype(BF16)

            @pl.when(vnext_ref[v] >= 0)
            def _():
                for c in fetch(vnext_ref[v], 1 - slot):
                    c.start()

        x = _unpack_bf16_pairs(xs_ref[...]).astype(BF16)
        g = _dot(x, wgb_ref[...])
        u = _dot(x, wub_ref[...])
        y = _pack_bf16_pairs(_dot((g * jax.nn.sigmoid(g) * u).astype(BF16), wdb_ref[...]))
        row = lax.broadcasted_iota(I32, (y.shape[0], 1), 0)
        mine = (row >= vlo_ref[v]) & (row < vhi_ref[v])

        @pl.when(vfirst_ref[v] == 1)
        def _():
            ys_ref[...] = jnp.where(mine, y, 0)

        @pl.when(vfirst_ref[v] == 0)
        def _():
            ys_ref[...] = jnp.where(mine, y, ys_ref[...])


def _gmm(xs, visits, layer, wg, wu, wd):
    rows, dw = xs.shape
    d = 2 * dw
    tm = MOE_TILE
    hid = wg.shape[-1]
    nvis = rows // tm + MOE_EXPERTS - 1
    tile = lambda v, vt, *_: (vt[v], 0)
    anyspec = pl.BlockSpec(memory_space=pl.ANY)
    return pl.pallas_call(
        functools.partial(_gmm_kernel, layer),
        out_shape=jax.ShapeDtypeStruct((rows, dw), I32),
        grid_spec=pltpu.PrefetchScalarGridSpec(
            num_scalar_prefetch=8, grid=(nvis,),
            in_specs=[pl.BlockSpec((tm, dw), tile), anyspec, anyspec, anyspec],
            out_specs=pl.BlockSpec((tm, dw), tile),
            scratch_shapes=[pltpu.VMEM((2, d, hid), F32), pltpu.VMEM((2, d, hid), F32), pltpu.VMEM((2, hid, d), F32),
                            pltpu.VMEM((d, hid), BF16), pltpu.VMEM((d, hid), BF16), pltpu.VMEM((hid, d), BF16),
                            pltpu.SemaphoreType.DMA((2, 3))]),
        compiler_params=_cparams(("arbitrary",)),
    )(*visits, xs, wg, wu, wd)


def _combine_kernel(x_ref, wt_ref, mod_ref, fg_ref, y0_ref, y1_ref, o_ref):
    d = D_MODEL
    wt = wt_ref[...]
    y = wt[:, 0:1] * _unpack_bf16_pairs(y0_ref[...]) + wt[:, 1:2] * _unpack_bf16_pairs(y1_ref[...])
    o_ref[...] = _rms(x_ref[...] + mod_ref[:, 5 * d:6 * d] * y, fg_ref[...])


def _combine(x2, wt, mod, rows_per_mod, fg, yt, tm=COMBINE_TILE):
    n, d = x2.shape
    tm = min(tm, rows_per_mod)
    assert rows_per_mod % tm == 0
    ntiles = n // tm
    return pl.pallas_call(
        _combine_kernel,
        out_shape=jax.ShapeDtypeStruct((n, d), F32),
        grid=(ntiles,),
        in_specs=[pl.BlockSpec((tm, d), lambda i: (i, 0)), pl.BlockSpec((tm, 8), lambda i: (i, 0)),
                  pl.BlockSpec((None, 1, 6 * d), lambda i: (i // (rows_per_mod // tm), 0, 0)),
                  pl.BlockSpec(fg.shape, lambda i: (0, 0)),
                  pl.BlockSpec((tm, d // 2), lambda i: (i, 0)), pl.BlockSpec((tm, d // 2), lambda i: (ntiles + i, 0))],
        out_specs=pl.BlockSpec((tm, d), lambda i: (i, 0)),
        compiler_params=_cparams(("parallel",)),
    )(x2, wt, mod, fg, yt, yt)


def _pick(table, idx):
    hot = idx[..., None] == jnp.arange(table.shape[0], dtype=I32)
    return jnp.sum(jnp.where(hot, table, 0), axis=-1)


def _moe_plan(counts, e, r, rows):
    tm = MOE_TILE
    ends = jnp.cumsum(counts)
    starts = ends - counts
    dest = (_pick(starts, e) + r).reshape(-1)
    first_tile = starts // tm
    nvis = jnp.where(counts > 0, (ends - 1) // tm - first_tile + 1, 0)
    vend = jnp.cumsum(nvis)
    nv = vend[-1:]
    v = jnp.minimum(jnp.arange(rows // tm + MOE_EXPERTS - 1, dtype=I32), nv[0] - 1)
    ve = jnp.sum((vend[None, :] <= v[:, None]).astype(I32), axis=1)
    vt = _pick(first_tile, ve) + v - _pick(vend - nvis, ve)
    vlo = jnp.maximum(_pick(starts, ve) - vt * tm, 0)
    vhi = jnp.minimum(_pick(ends, ve) - vt * tm, tm)
    vfirst = jnp.concatenate([jnp.ones((1,), I32), (vt[1:] != vt[:-1]).astype(I32)])
    changed = jnp.concatenate([jnp.ones((1,), I32), (ve[1:] != ve[:-1]).astype(I32)])
    vslot = (jnp.cumsum(changed) - 1) % 2
    eid = jnp.arange(MOE_EXPERTS, dtype=I32)
    later = (eid[None, :] > eid[:, None]) & (counts[None, :] > 0)
    nxt = jnp.min(jnp.where(later, eid[None, :], MOE_EXPERTS), axis=1)
    vnext = _pick(jnp.where(nxt < MOE_EXPERTS, nxt, -1), ve)
    return dest, (vt, ve, vlo, vhi, vfirst, vslot.astype(I32), vnext.astype(I32), nv)


def _moe_experts(routed, layer, wg, wu, wd):
    h, e, wt, r, cnt = routed
    dest, visits = _moe_plan(cnt[:, 0].astype(I32), e, r, 2 * h.shape[0])
    xs = _sc_permute_rows(h, dest, scatter=True)
    ys = _gmm(xs, visits, layer, wg, wu, wd)
    return _sc_permute_rows(ys, dest, scatter=False), wt


def _rope_tables(rows, dim):
    row = jnp.repeat(jnp.arange(rows, dtype=F32), GRID_W)
    col = jnp.tile(jnp.arange(GRID_W, dtype=F32), rows)
    half = dim // 2
    inv = jnp.power(ROPE_BASE, -jnp.arange(0, half, 2, dtype=F32) / half)
    ar = row[:, None] * inv[None, :]
    ac = col[:, None] * inv[None, :]
    ang = jnp.concatenate([ar, ar, ac, ac], axis=-1)
    return jnp.cos(ang), jnp.sin(ang)


def _even_tables(n, with_rope):
    if with_rope:
        cm, sm = _rope_tables(n // GRID_W, MLA_ROPE)
        cwin, swin = _rope_tables(n // GRID_W, WIN_HEAD_DIM)
    else:
        cm, sm = jnp.ones((n, MLA_ROPE), F32), jnp.zeros((n, MLA_ROPE), F32)
        cwin, swin = jnp.ones((n, WIN_HEAD_DIM), F32), jnp.zeros((n, WIN_HEAD_DIM), F32)
    one, zero = jnp.ones((n, MLA_NOPE), F32), jnp.zeros((n, MLA_NOPE), F32)
    pad = jnp.zeros((n, LANES - MLA_NOPE - MLA_ROPE), F32)
    return (jnp.concatenate([one, cm, pad], axis=1), jnp.concatenate([zero, sm, pad], axis=1),
            jnp.concatenate([cwin, cwin], axis=1), jnp.concatenate([swin, swin], axis=1))


def _even_weights(w_in, qg, w_uq, kvg, w_ukv):
    d = w_in.shape[0]
    o = np.cumsum([0, MLA_Q_RANK, MLA_KV_RANK, MLA_ROPE, 512, 128, 128])
    cq, ckv, kr, qw, kw, vw = [w_in[:, o[i]:o[i + 1]] for i in range(6)]
    z = lambda c: jnp.zeros((d, c), F32)
    kr128 = jnp.concatenate([z(MLA_NOPE), kr, z(LANES - MLA_NOPE - MLA_ROPE)], axis=1)
    dup = lambda t: jnp.concatenate([t[:, 0:64], t[:, 0:64], t[:, 64:128], t[:, 64:128]], axis=1)
    win = jnp.concatenate([cq, ckv, kr128, qw, dup(kw), dup(vw)], axis=1).astype(BF16)
    uq = w_uq.reshape(MLA_Q_RANK, MLA_HEADS, MLA_NOPE + MLA_ROPE)
    uq = jnp.pad(uq, ((0, 0), (0, 0), (0, LANES - MLA_NOPE - MLA_ROPE))).reshape(MLA_Q_RANK, MLA_HEADS * LANES)
    ukv = w_ukv.reshape(MLA_KV_RANK, MLA_HEADS, MLA_NOPE + MLA_V)
    ukk = jnp.pad(ukv[:, :, :MLA_NOPE], ((0, 0), (0, 0), (0, LANES - MLA_NOPE))).reshape(MLA_KV_RANK, MLA_HEADS * LANES)
    ukvv = ukv[:, :, MLA_NOPE:].reshape(MLA_KV_RANK, MLA_HEADS * MLA_V)
    return (win, qg.reshape(1, -1), uq.astype(BF16), kvg.reshape(1, -1), ukk.astype(BF16), ukvv.T.astype(BF16))


def _odd_weights(w_in, w_g2, b_g, ln_g, ln_b, w_s, b_s):
    d = w_in.shape[0]
    o = np.cumsum([0, GLA_K, GLA_K, GLA_V, 2 * GLA_GATE_RANK, GLA_V, SG_WIDTH, SG_WIDTH])
    q, k, v, g, r, u, vg = [w_in[:, o[i]:o[i + 1]] for i in range(7)]
    g128 = jnp.concatenate([g, jnp.zeros((d, LANES - 2 * GLA_GATE_RANK), F32)], axis=1)
    win = jnp.concatenate([q, k, v, g128, r, u, vg], axis=1).astype(BF16)
    zr = jnp.zeros((GLA_GATE_RANK, GLA_K), F32)
    pad = jnp.zeros((LANES - 2 * GLA_GATE_RANK, GLA_K), F32)
    wg = jnp.concatenate([jnp.concatenate([w_g2[0], zr, pad], axis=0),
                          jnp.concatenate([zr, w_g2[1], pad], axis=0)], axis=1)
    bg = b_g.reshape(1, 2 * GLA_K)
    return (win, wg, bg, ln_g.reshape(1, -1), ln_b.reshape(1, -1), w_s.astype(BF16), b_s.T)


def kernel(x, c, ctx, c_ctx, ada_w, ada_b, norm_mix_g, norm_ffn_g, even_w_in, mla_q_norm_g, mla_w_uq, mla_kv_norm_g, mla_w_ukv, win_sink, even_w_out, odd_w_in, gla_w_g2, gla_b_g, gla_norm_g, sg_ln_g, sg_ln_b, sg_w_s, sg_b_s, odd_w_out, moe_w_rg, moe_w_re, moe_w_gate, moe_w_up, moe_w_down, final_norm_g):
    b, n, d = x.shape
    lc = ctx.shape[1]
    depth = ada_w.shape[0]
    assert depth == 2 and d == D_MODEL and b < 8
    assert n % 512 == 0 and lc % MOE_TILE == 0 and n % GRID_W == 0
    tm = 512 if n % 512 == 0 else 256
    tq = 256

    cond8 = jnp.concatenate([c, c_ctx[None, :], jnp.zeros((8 - b - 1, d), F32)], axis=0)
    mod_all = _adaln(cond8, ada_w, ada_b).reshape(depth, 8, 1, 6 * d)
    ctx_row = b
    u_tri = jnp.asarray(np.triu(np.ones((MOE_TILE, MOE_TILE), np.float32), 1), BF16)
    fg = final_norm_g.reshape(1, d)

    def router_w(layer):
        return jnp.concatenate([moe_w_rg[layer].T, jnp.zeros((8 - MOE_GROUPS, d), F32), moe_w_re[layer].T], axis=0)

    def moe_experts(routed, layer):
        return _moe_experts(routed, layer, moe_w_gate, moe_w_up, moe_w_down)

    mod = mod_all[0]
    gn = norm_mix_g[0].reshape(1, d)
    ew = _even_weights(even_w_in[0], mla_q_norm_g[0], mla_w_uq[0], mla_kv_norm_g[0], mla_w_ukv[0])
    qm_l, km_l, vm_l, qw_l, kw_l, vw_l = _even_in(x, mod, None, gn, ew, _even_tables(n, True), min(EVEN_IN_STEP, n))
    qm_c, km_c, vm_c, qw_c, kw_c, vw_c = _even_in(ctx, mod, ctx_row, gn, ew, _even_tables(lc, False), lc)
    w_out = even_w_out[0].astype(BF16)
    sink = win_sink[0]
    oa_l = _mla_attn(qm_l, [(km_l, vm_l), (km_c, vm_c)], tq)
    ob_l = _gqa(qw_l, kw_l, vw_l, kw_c, vw_c, sink, True)
    oa_c = _mla_attn(qm_c, [(km_c, vm_c)], lc)
    ob_c = _gqa(qw_c, None, None, kw_c, vw_c, sink, False)
    xl, xc, routed = _even_out_router(x, ctx, oa_l, oa_c, ob_l, ob_c, w_out, mod, ctx_row,
                                      norm_ffn_g[0].reshape(1, d), router_w(0), u_tri)
    pending = moe_experts(routed, 0)

    mod = mod_all[1]
    gn = norm_mix_g[1].reshape(1, d)
    ow = _odd_weights(odd_w_in[0], gla_w_g2[0], gla_b_g[0], sg_ln_g[0], sg_ln_b[0], sg_w_s[0], sg_b_s[0])
    xl, q_l, k_l, v_l, la_l, r_l, dl_l = _odd_in(xl, pending, 0, mod_all[0], mod, None, gn, ow)
    _, q_c, k_c, v_c, la_c, _, _ = _odd_in(xc, pending, b * n // MOE_TILE, mod_all[0], mod, ctx_row, gn, ow)
    cumq_np, cumkt_np, pm_np, nlev = _gla_tables()
    cumq, cumkt = jnp.asarray(cumq_np, BF16), jnp.asarray(cumkt_np, BF16)
    pm = jnp.asarray(pm_np, F32)
    s0 = jnp.zeros((b, 2, GLA_HEADS, GLA_DV, LANES), F32)
    _, _, s_ctx = _gla(q_c, k_c, v_c, la_c, s0, cumq, cumkt, pm, nlev)
    o_fwd, o_bwd, _ = _gla(q_l, k_l, v_l, la_l, s_ctx, cumq, cumkt, pm, nlev)
    xl, routed = _odd_out_router(xl, o_fwd, o_bwd, r_l, dl_l, gla_norm_g[0].reshape(1, -1),
                                 odd_w_out[0].astype(BF16), mod, norm_ffn_g[1].reshape(1, d), router_w(1), u_tri)
    yt, wt = moe_experts(routed, 1)
    return _combine(xl.reshape(b * n, d), wt, mod, n, fg, yt).reshape(b, n, d)
```
